```python
import math
import jax, jax.numpy as jnp
from jax import lax
import numpy as np

D_MODEL = 2048
BATCH = 4
SEQ = 2048
DEPTH = 2
DEC_BATCH = 128
DEC_SEQ = 4
PAST_LEN = 16384
PAGE_SIZE = 128

MIX_WIDTH = D_MODEL
M_WIDTH = MIX_WIDTH // 2
R_WIDTH = MIX_WIDTH // 4
S_WIDTH = MIX_WIDTH - M_WIDTH - R_WIDTH
M_HEADS = 8
M_HEAD_DIM = M_WIDTH // M_HEADS
M_CHUNK = 64
R_BLOCKS = 8
R_BLOCK_DIM = R_WIDTH // R_BLOCKS
RG_C = 8.0
S_GROUP = 16
S_GROUPS = S_WIDTH // S_GROUP
S_STATE = 64
CONV_W = 4
EPS = 1e-6

kernel_name = 'hymba_style_mlstm_rglru_s5_decode_step'


def _rmsnorm(x, w):
    xf = x.astype(jnp.float32)
    y = xf * lax.rsqrt(jnp.mean(xf * xf, axis=-1, keepdims=True) + EPS)
    return (y * w.astype(jnp.float32)).astype(x.dtype)


def _causal_conv(x, buf, w, b):
    L = x.shape[1]
    xp = jnp.concatenate([buf.astype(x.dtype), x], axis=1)
    out = xp[:, 0:L] * w[0]
    for j in range(1, CONV_W):
        out = out + xp[:, j:j + L] * w[j]
    return out + b, xp[:, xp.shape[1] - (CONV_W - 1):]


def _blockdiag(x, w):
    B, L, _ = x.shape
    nb, di, do = w.shape
    return jnp.einsum('blnd,nde->blne', x.reshape(B, L, nb, di), w).reshape(B, L, nb * do)


def _mlstm_chunkwise(q, k, v, log_i, log_f, C0, n0, m0):
    B, L, H, D = q.shape
    T = math.gcd(L, M_CHUNK)
    nc = L // T

    def chunks(a):
        return jnp.moveaxis(a.reshape((B, nc, T) + a.shape[2:]), 1, 0)

    causal = jnp.tril(jnp.ones((T, T), dtype=bool))

    def step(carry, inp):
        C, n, m = carry
        qc, kc, vc, li, lf = inp
        b = jnp.cumsum(lf, axis=1).transpose(0, 2, 1)
        lit = li.transpose(0, 2, 1)
        log_d = jnp.where(causal, b[..., :, None] - b[..., None, :] + lit[..., None, :], -jnp.inf)
        log_inter = b + m[..., None]
        m_t = jnp.maximum(log_inter, jnp.max(log_d, axis=-1))
        w_intra = jnp.exp(log_d - m_t[..., None])
        w_inter = jnp.exp(log_inter - m_t)
        s = jnp.einsum('bthd,bshd->bhts', qc, kc) * w_intra
        num = jnp.einsum('bhts,bshe->bhte', s, vc) + w_inter[..., None] * jnp.einsum('bhed,bthd->bhte', C, qc)
        den = jnp.sum(s, axis=-1) + w_inter * jnp.einsum('bhd,bthd->bht', n, qc)
        h = num / jnp.maximum(jnp.abs(den), jnp.exp(-m_t))[..., None]
        b_end = b[..., -1]
        log_src = b_end[..., None] - b + lit
        m_new = jnp.maximum(b_end + m, jnp.max(log_src, axis=-1))
        w_src = jnp.exp(log_src - m_new[..., None])
        decay = jnp.exp(b_end + m - m_new)
        C_new = decay[..., None, None] * C + jnp.einsum('bhs,bshe,bshd->bhed', w_src, vc, kc)
        n_new = decay[..., None] * n + jnp.einsum('bhs,bshd->bhd', w_src, kc)
        return (C_new, n_new, m_new), h.transpose(0, 2, 1, 3)

    (C, n, m), h = lax.scan(step, (C0, n0, m0),
                            (chunks(q), chunks(k), chunks(v), chunks(log_i), chunks(log_f)))
    h = jnp.moveaxis(h, 0, 1).reshape(B, L, H, D)
    return h, C, n, m


def _mlstm_branch(xm, C0, n0, m0, conv0, p):
    f32 = jnp.float32
    B, L, _ = xm.shape
    xc, conv_new = _causal_conv(xm, conv0, p['m_conv_w'], p['m_conv_b'])
    xc = jax.nn.silu(xc)
    q = _blockdiag(xc, p['m_wq'])
    k = _blockdiag(xc, p['m_wk'])
    v = _blockdiag(xm, p['m_wv'])
    o = jax.nn.sigmoid(_blockdiag(xm, p['m_wo']).astype(f32))
    gates = (jnp.concatenate([q, k, v], axis=-1) @ p['m_w_if'] + p['m_b_if']).astype(f32)
    log_i = gates[..., :M_HEADS]
    log_f = jax.nn.log_sigmoid(gates[..., M_HEADS:])

    def hd(a):
        return a.astype(f32).reshape(B, L, M_HEADS, M_HEAD_DIM)

    h, C, n, m = _mlstm_chunkwise(hd(q), hd(k) * (M_HEAD_DIM ** -0.5), hd(v), log_i, log_f,
                                  C0.astype(f32), n0.astype(f32), m0.astype(f32))
    mu = jnp.mean(h, axis=-1, keepdims=True)
    var = jnp.mean(jnp.square(h - mu), axis=-1, keepdims=True)
    hn = ((h - mu) * lax.rsqrt(var + EPS)).reshape(B, L, M_WIDTH) * p['m_norm_w'].astype(f32)
    out = o * hn + p['m_skip'].astype(f32) * xc.astype(f32)
    return out.astype(xm.dtype), (C, n, m, conv_new)


def _lin_combine(e1, e2):
    a1, b1 = e1
    a2, b2 = e2
    return a1 * a2, a2 * b1 + b2


def _rglru_branch(xr, h0, conv0, p):
    f32 = jnp.float32
    xc, conv_new = _causal_conv(xr, conv0, p['r_conv_w'], p['r_conv_b'])
    r = jax.nn.sigmoid((_blockdiag(xc, p['r_wa']) + p['r_ba']).astype(f32))
    i = jax.nn.sigmoid((_blockdiag(xc, p['r_wx']) + p['r_bx']).astype(f32))
    log_a = -RG_C * r * jax.nn.softplus(-p['r_lam'].astype(f32))
    a = jnp.exp(log_a)
    b = jnp.sqrt(-jnp.expm1(2.0 * log_a)) * (i * xc.astype(f32))
    b = b.at[:, 0].add(a[:, 0] * h0.astype(f32))
    _, h = lax.associative_scan(_lin_combine, (a, b), axis=1)
    return h.astype(xr.dtype), (h[:, -1], conv_new)


def _cplx_combine(e1, e2):
    a1r, a1i, b1r, b1i = e1
    a2r, a2i, b2r, b2i = e2
    return (a2r * a1r - a2i * a1i, a2r * a1i + a2i * a1r,
            a2r * b1r - a2i * b1i + b2r, a2r * b1i + a2i * b1r + b2i)


def _s5_branch(xs, s0_re, s0_im, p):
    f32 = jnp.float32
    B, L, _ = xs.shape
    u = xs.astype(f32).reshape(B, L, S_GROUPS, S_GROUP)
    dt = jnp.exp(p['s_log_step'].astype(f32))[:, None]
    lr = p['s_lam_re'].astype(f32)
    li = p['s_lam_im'].astype(f32)
    mag = jnp.exp(lr * dt)
    ang = li * dt
    ab_re = mag * jnp.cos(ang)
    ab_im = mag * jnp.sin(ang)
    den = lr * lr + li * li
    nr = ab_re - 1.0
    f_re = (nr * lr + ab_im * li) / den
    f_im = (ab_im * lr - nr * li) / den
    br = p['s_b_re'].astype(f32)
    bi = p['s_b_im'].astype(f32)
    bb_re = f_re[..., None] * br - f_im[..., None] * bi
    bb_im = f_re[..., None] * bi + f_im[..., None] * br
    bu_re = jnp.einsum('blgc,gpc->blgp', u, bb_re)
    bu_im = jnp.einsum('blgc,gpc->blgp', u, bb_im)
    s0r = s0_re.astype(f32)
    s0i = s0_im.astype(f32)
    bu_re = bu_re.at[:, 0].add(ab_re * s0r - ab_im * s0i)
    bu_im = bu_im.at[:, 0].add(ab_re * s0i + ab_im * s0r)
    a_re = jnp.broadcast_to(ab_re, bu_re.shape)
    a_im = jnp.broadcast_to(ab_im, bu_im.shape)
    _, _, s_re, s_im = lax.associative_scan(_cplx_combine, (a_re, a_im, bu_re, bu_im), axis=1)
    y = (jnp.einsum('gcp,blgp->blgc', p['s_c_re'].astype(f32), s_re)
         - jnp.einsum('gcp,blgp->blgc', p['s_c_im'].astype(f32), s_im)
         + p['s_d'].astype(f32).reshape(S_GROUPS, S_GROUP) * u).reshape(B, L, S_WIDTH)
    g = jax.nn.gelu(y)
    out = g * jax.nn.sigmoid(g @ p['s_w_glu'].astype(f32) + p['s_b_glu'].astype(f32))
    return out.astype(xs.dtype), (s_re[:, -1], s_im[:, -1])


def _layer(x, st, p):
    C0, n0, m0, mconv0, h0, rconv0, sre0, sim0 = st
    hn = _rmsnorm(x, p['norm_w'])
    proj = hn @ p['w_in']
    xm = proj[..., :M_WIDTH]
    xr = proj[..., M_WIDTH:M_WIDTH + R_WIDTH]
    xs = proj[..., M_WIDTH + R_WIDTH:MIX_WIDTH]
    z = proj[..., MIX_WIDTH:]
    om, (C, n, m, mconv) = _mlstm_branch(xm, C0, n0, m0, mconv0, p)
    orr, (h, rconv) = _rglru_branch(xr, h0, rconv0, p)
    os_, (sre, sim) = _s5_branch(xs, sre0, sim0, p)
    mixed = jnp.concatenate([om, orr, os_], axis=-1) * jax.nn.silu(z)
    y = x + mixed @ p['w_out']
    return y, (C, n, m, mconv, h, rconv, sre, sim)


def _zero_state(bsz, dtype):
    f32 = jnp.float32
    return (jnp.zeros((bsz, M_HEADS, M_HEAD_DIM, M_HEAD_DIM), f32),
            jnp.zeros((bsz, M_HEADS, M_HEAD_DIM), f32),
            jnp.zeros((bsz, M_HEADS), f32),
            jnp.zeros((bsz, CONV_W - 1, M_WIDTH), dtype),
            jnp.zeros((bsz, R_WIDTH), f32),
            jnp.zeros((bsz, CONV_W - 1, R_WIDTH), dtype),
            jnp.zeros((bsz, S_GROUPS, S_STATE), f32),
            jnp.zeros((bsz, S_GROUPS, S_STATE), f32))


def setup_inputs(seed: int = 0) -> dict:
    key = jax.random.key(seed)
    ks = iter(jax.random.split(key, 48))
    f32 = jnp.float32

    def nrm(shape, scale):
        return jax.random.normal(next(ks), shape, f32) * scale

    H, Dh, G, P = M_HEADS, M_HEAD_DIM, S_GROUPS, S_STATE
    x_prompt = nrm((BATCH, SEQ, D_MODEL), 1.0)
    x_sample = nrm((DEC_BATCH, DEC_SEQ, D_MODEL), 1.0)
    state_mlstm_C = nrm((DEPTH, DEC_BATCH, H, Dh, Dh), Dh ** -0.5)
    state_mlstm_n = nrm((DEPTH, DEC_BATCH, H, Dh), 1.0)
    state_mlstm_m = nrm((DEPTH, DEC_BATCH, H), 0.5)
    state_mlstm_conv = nrm((DEPTH, DEC_BATCH, CONV_W - 1, M_WIDTH), 1.0)
    state_rglru_h = nrm((DEPTH, DEC_BATCH, R_WIDTH), 0.5)
    state_rglru_conv = nrm((DEPTH, DEC_BATCH, CONV_W - 1, R_WIDTH), 1.0)
    state_s5_re = nrm((DEPTH, DEC_BATCH, G, P), 0.1)
    state_s5_im = nrm((DEPTH, DEC_BATCH, G, P), 0.1)

    norm_w = 1.0 + nrm((DEPTH, D_MODEL), 0.02)
    w_in = nrm((DEPTH, D_MODEL, 2 * MIX_WIDTH), D_MODEL ** -0.5)
    w_out = nrm((DEPTH, MIX_WIDTH, D_MODEL), MIX_WIDTH ** -0.5)
    m_conv_w = nrm((DEPTH, CONV_W, M_WIDTH), CONV_W ** -0.5)
    m_conv_b = nrm((DEPTH, M_WIDTH), 0.02)
    m_wq = nrm((DEPTH, H, Dh, Dh), Dh ** -0.5)
    m_wk = nrm((DEPTH, H, Dh, Dh), Dh ** -0.5)
    m_wv = nrm((DEPTH, H, Dh, Dh), Dh ** -0.5)
    m_wo = nrm((DEPTH, H, Dh, Dh), Dh ** -0.5)
    m_w_if = nrm((DEPTH, 3 * M_WIDTH, 2 * H), (3 * M_WIDTH) ** -0.5)
    m_b_if = jnp.concatenate([nrm((DEPTH, H), 0.1),
                              jnp.linspace(3.0, 6.0, H, dtype=f32)[None] + nrm((DEPTH, H), 0.1)], axis=-1)
    m_norm_w = 1.0 + nrm((DEPTH, M_WIDTH), 0.02)
    m_skip = 1.0 + nrm((DEPTH, M_WIDTH), 0.02)
    r_conv_w = nrm((DEPTH, CONV_W, R_WIDTH), CONV_W ** -0.5)
    r_conv_b = nrm((DEPTH, R_WIDTH), 0.02)
    r_wa = nrm((DEPTH, R_BLOCKS, R_BLOCK_DIM, R_BLOCK_DIM), R_BLOCK_DIM ** -0.5)
    r_ba = nrm((DEPTH, R_WIDTH), 0.02)
    r_wx = nrm((DEPTH, R_BLOCKS, R_BLOCK_DIM, R_BLOCK_DIM), R_BLOCK_DIM ** -0.5)
    r_bx = nrm((DEPTH, R_WIDTH), 0.02)
    a_pow = jax.random.uniform(next(ks), (DEPTH, R_WIDTH), f32, minval=0.9, maxval=0.999)
    root = a_pow ** (1.0 / RG_C)
    r_lam = jnp.log(root) - jnp.log1p(-root)
    s_lam_re = -0.5 + nrm((DEPTH, G, P), 0.01)
    s_lam_im = math.pi * jnp.arange(P, dtype=f32)[None, None] + nrm((DEPTH, G, P), 0.01)
    s_b_re = nrm((DEPTH, G, P, S_GROUP), (2 * S_GROUP) ** -0.5)
    s_b_im = nrm((DEPTH, G, P, S_GROUP), (2 * S_GROUP) ** -0.5)
    s_c_re = nrm((DEPTH, G, S_GROUP, P), (2 * P) ** -0.5)
    s_c_im = nrm((DEPTH, G, S_GROUP, P), (2 * P) ** -0.5)
    s_d = nrm((DEPTH, S_WIDTH), 1.0)
    s_log_step = jax.random.uniform(next(ks), (DEPTH, G), f32, minval=math.log(1e-3), maxval=math.log(1e-1))
    s_w_glu = nrm((DEPTH, S_WIDTH, S_WIDTH), S_WIDTH ** -0.5)
    s_b_glu = nrm((DEPTH, S_WIDTH), 0.02)
    final_norm_w = 1.0 + nrm((D_MODEL,), 0.02)
    return {'x_prompt': x_prompt, 'x_sample': x_sample,
            'state_mlstm_C': state_mlstm_C, 'state_mlstm_n': state_mlstm_n, 'state_mlstm_m': state_mlstm_m,
            'state_mlstm_conv': state_mlstm_conv, 'state_rglru_h': state_rglru_h,
            'state_rglru_conv': state_rglru_conv, 'state_s5_re': state_s5_re, 'state_s5_im': state_s5_im,
            'norm_w': norm_w, 'w_in': w_in, 'w_out': w_out,
            'm_conv_w': m_conv_w, 'm_conv_b': m_conv_b, 'm_wq': m_wq, 'm_wk': m_wk, 'm_wv': m_wv,
            'm_wo': m_wo, 'm_w_if': m_w_if, 'm_b_if': m_b_if, 'm_norm_w': m_norm_w, 'm_skip': m_skip,
            'r_conv_w': r_conv_w, 'r_conv_b': r_conv_b, 'r_wa': r_wa, 'r_ba': r_ba, 'r_wx': r_wx,
            'r_bx': r_bx, 'r_lam': r_lam,
            's_lam_re': s_lam_re, 's_lam_im': s_lam_im, 's_b_re': s_b_re, 's_b_im': s_b_im,
            's_c_re': s_c_re, 's_c_im': s_c_im, 's_d': s_d, 's_log_step': s_log_step,
            's_w_glu': s_w_glu, 's_b_glu': s_b_glu, 'final_norm_w': final_norm_w}


def reference(x_prompt, x_sample, state_mlstm_C, state_mlstm_n, state_mlstm_m, state_mlstm_conv,
              state_rglru_h, state_rglru_conv, state_s5_re, state_s5_im,
              norm_w, w_in, w_out, m_conv_w, m_conv_b, m_wq, m_wk, m_wv, m_wo, m_w_if, m_b_if,
              m_norm_w, m_skip, r_conv_w, r_conv_b, r_wa, r_ba, r_wx, r_bx, r_lam,
              s_lam_re, s_lam_im, s_b_re, s_b_im, s_c_re, s_c_im, s_d, s_log_step, s_w_glu, s_b_glu,
              final_norm_w):
    xp = x_prompt
    xs = x_sample
    new_pr = []
    new_sa = []
    for l in range(DEPTH):
        p = {'norm_w': norm_w[l], 'w_in': w_in[l], 'w_out': w_out[l],
             'm_conv_w': m_conv_w[l], 'm_conv_b': m_conv_b[l], 'm_wq': m_wq[l], 'm_wk': m_wk[l],
             'm_wv': m_wv[l], 'm_wo': m_wo[l], 'm_w_if': m_w_if[l], 'm_b_if': m_b_if[l],
             'm_norm_w': m_norm_w[l], 'm_skip': m_skip[l],
             'r_conv_w': r_conv_w[l], 'r_conv_b': r_conv_b[l], 'r_wa': r_wa[l], 'r_ba': r_ba[l],
             'r_wx': r_wx[l], 'r_bx': r_bx[l], 'r_lam': r_lam[l],
             's_lam_re': s_lam_re[l], 's_lam_im': s_lam_im[l], 's_b_re': s_b_re[l], 's_b_im': s_b_im[l],
             's_c_re': s_c_re[l], 's_c_im': s_c_im[l], 's_d': s_d[l], 's_log_step': s_log_step[l],
             's_w_glu': s_w_glu[l], 's_b_glu': s_b_glu[l]}
        st_sa = (state_mlstm_C[l], state_mlstm_n[l], state_mlstm_m[l], state_mlstm_conv[l],
                 state_rglru_h[l], state_rglru_conv[l], state_s5_re[l], state_s5_im[l])
        xp, st_p = _layer(xp, _zero_state(xp.shape[0], xp.dtype), p)
        xs, st_s = _layer(xs, st_sa, p)
        new_pr.append(st_p)
        new_sa.append(st_s)
    y_prompt = _rmsnorm(xp, final_norm_w)
    y_sample = _rmsnorm(xs, final_norm_w)
    pr_C, pr_n, pr_m, pr_mconv, pr_h, pr_rconv, pr_sre, pr_sim = [
        jnp.stack([st[i] for st in new_pr]) for i in range(8)]
    sa_C, sa_n, sa_m, sa_mconv, sa_h, sa_rconv, sa_sre, sa_sim = [
        jnp.stack([st[i] for st in new_sa]) for i in range(8)]
    return (y_prompt, y_sample,
            pr_C, pr_n, pr_m, pr_mconv, pr_h, pr_rconv, pr_sre, pr_sim,
            sa_C, sa_n, sa_m, sa_mconv, sa_h, sa_rconv, sa_sre, sa_sim)
```

```python
import functools
import math

import jax
import jax.numpy as jnp
from jax import lax
from jax.experimental import pallas as pl
from jax.experimental.pallas import tpu as pltpu

F32 = jnp.float32
BF16 = jnp.bfloat16

D_MODEL = 2048
DEPTH = 2
MIX_WIDTH = D_MODEL
M_WIDTH = MIX_WIDTH // 2
R_WIDTH = MIX_WIDTH // 4
S_WIDTH = MIX_WIDTH - M_WIDTH - R_WIDTH
M_HEADS = 8
M_HEAD_DIM = M_WIDTH // M_HEADS
R_BLOCKS = 8
R_BLOCK_DIM = R_WIDTH // R_BLOCKS
RG_C = 8.0
S_GROUP = 16
S_GROUPS = S_WIDTH // S_GROUP
S_STATE = 64
S_LANES = S_GROUPS * S_STATE
CONV_W = 4
EPS = 1e-6

SUBLANES = 8
LANES = 128
VMEM_LIMIT_BYTES = 56 * 1024 * 1024

PROMPT_CHUNK = 256
SAMPLE_ROWS = 8
INPROJ_TM = 512
INPROJ_TN = 1024
OUTPROJ_TM = 256
SCAN_LANE_BLOCK = 256


def _const_spec(shape):
    nd = len(shape)
    return pl.BlockSpec(shape, lambda *_: (0,) * nd)


def _s5_prep_kernel(lr_ref, li_ref, ls_ref, brt_ref, bit_ref, abre_ref, abim_ref, bbre_ref, bbim_ref):
    lr = lr_ref[0]
    li = li_ref[0]
    dt = jnp.exp(ls_ref[0])
    mag = jnp.exp(lr * dt)
    ang = li * dt
    ab_re = mag * jnp.cos(ang)
    ab_im = mag * jnp.sin(ang)
    den = lr * lr + li * li
    nr = ab_re - 1.0
    f_re = (nr * lr + ab_im * li) / den
    f_im = (ab_im * lr - nr * li) / den
    br = brt_ref[0]
    bi = bit_ref[0]
    abre_ref[0] = ab_re
    abim_ref[0] = ab_im
    bbre_ref[0] = f_re * br - f_im * bi
    bbim_ref[0] = f_re * bi + f_im * br


def _s5_prep(s_lam_re, s_lam_im, s_log_step, s_b_re, s_b_im):
    lr = s_lam_re.reshape(DEPTH, 1, S_LANES)
    li = s_lam_im.reshape(DEPTH, 1, S_LANES)
    ls = jnp.repeat(s_log_step, S_STATE, axis=-1).reshape(DEPTH, 1, S_LANES)
    brt = s_b_re.reshape(DEPTH, S_LANES, S_GROUP).transpose(0, 2, 1)
    bit = s_b_im.reshape(DEPTH, S_LANES, S_GROUP).transpose(0, 2, 1)
    vec = pl.BlockSpec((1, 1, S_LANES), lambda l: (l, 0, 0))
    mat = pl.BlockSpec((1, S_GROUP, S_LANES), lambda l: (l, 0, 0))
    return pl.pallas_call(
        _s5_prep_kernel,
        grid=(DEPTH,),
        in_specs=[vec, vec, vec, mat, mat],
        out_specs=[vec, vec, mat, mat],
        out_shape=[jax.ShapeDtypeStruct((DEPTH, 1, S_LANES), F32),
                   jax.ShapeDtypeStruct((DEPTH, 1, S_LANES), F32),
                   jax.ShapeDtypeStruct((DEPTH, S_GROUP, S_LANES), F32),
                   jax.ShapeDtypeStruct((DEPTH, S_GROUP, S_LANES), F32)],
        name="s5_prep",
    )(lr, li, ls, brt, bit)


def _inproj_kernel(x_ref, nw_ref, w_ref, o_ref, xn_ref):
    @pl.when(pl.program_id(1) == 0)
    def _():
        x = x_ref[...]
        ms = jnp.mean(x * x, axis=-1, keepdims=True)
        xn_ref[...] = (x * lax.rsqrt(ms + EPS) * nw_ref[...]).astype(BF16)

    o_ref[...] = jnp.dot(xn_ref[...], w_ref[...], preferred_element_type=F32)


def _inproj(x, norm_w, w_in_b):
    n = x.shape[0]
    tm = min(INPROJ_TM, n)
    return pl.pallas_call(
        _inproj_kernel,
        grid=(n // tm, (2 * MIX_WIDTH) // INPROJ_TN),
        in_specs=[pl.BlockSpec((tm, D_MODEL), lambda i, j: (i, 0)),
                  pl.BlockSpec((1, D_MODEL), lambda i, j: (0, 0)),
                  pl.BlockSpec((D_MODEL, INPROJ_TN), lambda i, j: (0, j))],
        out_specs=pl.BlockSpec((tm, INPROJ_TN), lambda i, j: (i, j)),
        out_shape=jax.ShapeDtypeStruct((n, 2 * MIX_WIDTH), F32),
        scratch_shapes=[pltpu.VMEM((tm, D_MODEL), BF16)],
        compiler_params=pltpu.CompilerParams(
            dimension_semantics=("parallel", "arbitrary"), vmem_limit_bytes=VMEM_LIMIT_BYTES),
        name="inproj",
    )(x, norm_w, w_in_b)


def _outproj_kernel(mixed_ref, x_ref, w_ref, fw_ref, o_ref, *, final):
    y = x_ref[...] + jnp.dot(mixed_ref[...], w_ref[...], preferred_element_type=F32)
    if final:
        ms = jnp.mean(y * y, axis=-1, keepdims=True)
        y = y * lax.rsqrt(ms + EPS) * fw_ref[...]
    o_ref[...] = y


def _outproj(mixed, x, w_out_b, final_w, final):
    n = x.shape[0]
    tm = min(OUTPROJ_TM, n)
    return pl.pallas_call(
        functools.partial(_outproj_kernel, final=final),
        grid=(n // tm,),
        in_specs=[pl.BlockSpec((tm, MIX_WIDTH), lambda i: (i, 0)),
                  pl.BlockSpec((tm, D_MODEL), lambda i: (i, 0)),
                  _const_spec((MIX_WIDTH, D_MODEL)),
                  _const_spec((1, D_MODEL))],
        out_specs=pl.BlockSpec((tm, D_MODEL), lambda i: (i, 0)),
        out_shape=jax.ShapeDtypeStruct((n, D_MODEL), F32),
        compiler_params=pltpu.CompilerParams(
            dimension_semantics=("parallel",), vmem_limit_bytes=VMEM_LIMIT_BYTES),
        name="outproj",
    )(mixed, x, w_out_b, final_w)


def _silu(x):
    return x * jax.nn.sigmoid(x)


def _log_sigmoid(x):
    return jnp.minimum(x, 0.0) - jnp.log1p(jnp.exp(-jnp.abs(x)))


def _softplus(x):
    return jnp.maximum(x, 0.0) + jnp.log1p(jnp.exp(-jnp.abs(x)))


def _cumsum_rows(x, rows):
    ridx = lax.broadcasted_iota(jnp.int32, x.shape, 0)
    s = 1
    while s < rows:
        x = x + jnp.where(ridx >= s, pltpu.roll(x, s, 0), 0.0)
        s *= 2
    return x


def _scan_real(a_ref, b_ref, rows, width):
    def body(j, carry):
        sl = pl.ds(pl.multiple_of(j * SCAN_LANE_BLOCK, SCAN_LANE_BLOCK), SCAN_LANE_BLOCK)
        a = a_ref[:, sl]
        b = b_ref[:, sl]
        ridx = lax.broadcasted_iota(jnp.int32, a.shape, 0)
        s = 1
        while s < rows:
            m = ridx >= s
            b = jnp.where(m, a * pltpu.roll(b, s, 0) + b, b)
            if 2 * s < rows:
                a = jnp.where(m, a * pltpu.roll(a, s, 0), a)
            s *= 2
        b_ref[:, sl] = b
        return carry

    lax.fori_loop(0, width // SCAN_LANE_BLOCK, body, 0)


def _scan_cplx_const(re_ref, im_ref, pr_ref, pi_ref, rows, width):
    def body(j, carry):
        sl = pl.ds(pl.multiple_of(j * SCAN_LANE_BLOCK, SCAN_LANE_BLOCK), SCAN_LANE_BLOCK)
        sr = re_ref[:, sl]
        si = im_ref[:, sl]
        pr = pr_ref[:, sl]
        pi = pi_ref[:, sl]
        ridx = lax.broadcasted_iota(jnp.int32, sr.shape, 0)
        s = 1
        while s < rows:
            m = ridx >= s
            sr_sh = pltpu.roll(sr, s, 0)
            si_sh = pltpu.roll(si, s, 0)
            nr = sr + (pr * sr_sh - pi * si_sh)
            ni = si + (pr * si_sh + pi * sr_sh)
            sr = jnp.where(m, nr, sr)
            si = jnp.where(m, ni, si)
            if 2 * s < rows:
                pr, pi = pr * pr - pi * pi, 2.0 * (pr * pi)
            s *= 2
        re_ref[:, sl] = sr
        im_ref[:, sl] = si
        return carry

    lax.fori_loop(0, width // SCAN_LANE_BLOCK, body, 0)


def _causal_conv(x_ref_val, ext_ref, tail_ref, w_ref, b_ref, rows, valid_rows):
    ext_ref[0:SUBLANES, :] = tail_ref[0]
    ext_ref[SUBLANES:SUBLANES + rows, :] = x_ref_val
    w = w_ref[...]
    out = w[3:4, :] * x_ref_val + b_ref[...]
    for j in range(1, CONV_W):
        out = out + w[3 - j:4 - j, :] * ext_ref[SUBLANES - j:SUBLANES - j + rows, :]
    tail_ref[0] = ext_ref[valid_rows:valid_rows + SUBLANES, :]
    return out


def _mixer_kernel(proj_ref, c0_ref, n0_ref, m0_ref, mtail0_ref, h0_ref, rtail0_ref, sre0_ref, sim0_ref,
                  mcw_ref, mcb_ref, wqk_ref, wvo_ref, wif_ref, bif_ref, mnw_ref, mskip_ref,
                  rcw_ref, rcb_ref, rwa_ref, rwx_ref, rba_ref, rbx_ref, rlam_ref,
                  abre_ref, abim_ref, wbure_ref, wbuim_ref, wcre_ref, wcim_ref, sd_ref, wglu_ref, bglu_ref,
                  mixed_ref, c_ref, n_ref, m_ref, mtail_ref, h_ref, rtail_ref, sre_ref, sim_ref,
                  ext_m, ext_r, q_scr, k_scr, v_scr, o_scr, ra_scr, rb_scr, ure_scr, uim_scr,
                  *, rows, valid_rows):
    T = rows

    @pl.when(pl.program_id(1) == 0)
    def _():
        c_ref[...] = c0_ref[...]
        n_ref[...] = n0_ref[...]
        m_ref[...] = m0_ref[...]
        mtail_ref[...] = mtail0_ref[...]
        h_ref[...] = h0_ref[...]
        rtail_ref[...] = rtail0_ref[...]
        sre_ref[...] = sre0_ref[...]
        sim_ref[...] = sim0_ref[...]

    row1 = lax.broadcasted_iota(jnp.int32, (T, 1), 0)
    first_row = row1 == 0

    xm = proj_ref[0, :, 0:M_WIDTH]
    xc = _silu(_causal_conv(xm, ext_m, mtail_ref, mcw_ref, mcb_ref, T, valid_rows))
    xc_b = xc.astype(BF16)
    xm_b = xm.astype(BF16)
    for h in range(M_HEADS):
        sl = slice(h * M_HEAD_DIM, (h + 1) * M_HEAD_DIM)
        qk = jnp.dot(xc_b[:, sl], wqk_ref[h], preferred_element_type=F32)
        vo = jnp.dot(xm_b[:, sl], wvo_ref[h], preferred_element_type=F32)
        q_scr[:, sl] = qk[:, :M_HEAD_DIM]
        k_scr[:, sl] = qk[:, M_HEAD_DIM:]
        v_scr[:, sl] = vo[:, :M_HEAD_DIM]
        o_scr[:, sl] = vo[:, M_HEAD_DIM:]

    gates = (jnp.dot(q_scr[...].astype(BF16), wif_ref[0:M_WIDTH, :], preferred_element_type=F32)
             + jnp.dot(k_scr[...].astype(BF16), wif_ref[M_WIDTH:2 * M_WIDTH, :], preferred_element_type=F32)
             + jnp.dot(v_scr[...].astype(BF16), wif_ref[2 * M_WIDTH:3 * M_WIDTH, :], preferred_element_type=F32)
             + bif_ref[...])
    log_f = _log_sigmoid(gates)
    if valid_rows < T:
        valid = row1 < valid_rows
        log_i = jnp.where(valid, gates, -jnp.inf)
        log_f = jnp.where(valid, log_f, 0.0)
    else:
        log_i = gates
    b_all = _cumsum_rows(log_f, T)

    rowi = lax.broadcasted_iota(jnp.int32, (T, T), 0)
    coli = lax.broadcasted_iota(jnp.int32, (T, T), 1)
    causal = coli <= rowi
    eye = coli == rowi
    k_scale = M_HEAD_DIM ** -0.5

    for h in range(M_HEADS):
        sl = slice(h * M_HEAD_DIM, (h + 1) * M_HEAD_DIM)
        bcol = b_all[:, M_HEADS + h:M_HEADS + h + 1]
        licol = log_i[:, h:h + 1]
        brow = jnp.sum(jnp.where(eye, bcol, 0.0), axis=0, keepdims=True)
        lirow = jnp.sum(jnp.where(eye, licol, 0.0), axis=0, keepdims=True)
        m_prev = m_ref[0, h:h + 1, 0:1]
        c_prev = c_ref[0, h]
        n_prev = n_ref[0, h:h + 1, :]

        q = q_scr[:, sl]
        ks = k_scr[:, sl] * k_scale
        v = v_scr[:, sl]
        q_b = q.astype(BF16)
        ks_b = ks.astype(BF16)

        log_d = jnp.where(causal, bcol - brow + lirow, -jnp.inf)
        log_inter = bcol + m_prev
        m_t = jnp.maximum(log_inter, jnp.max(log_d, axis=1, keepdims=True))
        w_intra = jnp.exp(log_d - m_t)
        w_inter = jnp.exp(log_inter - m_t)
        s = lax.dot_general(q_b, ks_b, (((1,), (1,)), ((), ())), preferred_element_type=F32) * w_intra
        inter = lax.dot_general(q_b, c_prev.astype(BF16), (((1,), (1,)), ((), ())),
                                preferred_element_type=F32)
        num = jnp.dot(s.astype(BF16), v.astype(BF16), preferred_element_type=F32) + w_inter * inter
        den = (jnp.sum(s, axis=1, keepdims=True)
               + w_inter * jnp.sum(q * n_prev, axis=1, keepdims=True))
        hh = num / jnp.maximum(jnp.abs(den), jnp.exp(-m_t))

        b_end = bcol[T - 1:T, :]
        m_new = jnp.maximum(b_end + m_prev, jnp.max(b_end - brow + lirow, axis=1, keepdims=True))
        w_src = jnp.exp(b_end - bcol + licol - m_new)
        decay = jnp.exp(b_end + m_prev - m_new)
        vw_t = (v * w_src).T.astype(BF16)
        c_ref[0, h] = decay * c_prev + jnp.dot(vw_t, ks_b, preferred_element_type=F32)
        n_ref[0, h:h + 1, :] = decay * n_prev + jnp.sum(ks * w_src, axis=0, keepdims=True)
        m_ref[0, h:h + 1, :] = jnp.broadcast_to(m_new, (1, LANES))

        mu = jnp.mean(hh, axis=1, keepdims=True)
        hc = hh - mu
        var = jnp.mean(hc * hc, axis=1, keepdims=True)
        hn = hc * lax.rsqrt(var + EPS) * mnw_ref[:, sl]
        om = jax.nn.sigmoid(o_scr[:, sl]) * hn + mskip_ref[:, sl] * xc[:, sl]
        z = proj_ref[0, :, MIX_WIDTH + h * M_HEAD_DIM:MIX_WIDTH + (h + 1) * M_HEAD_DIM]
        mixed_ref[0, :, sl] = (om * _silu(z)).astype(BF16)

    xr = proj_ref[0, :, M_WIDTH:M_WIDTH + R_WIDTH]
    xcr = _causal_conv(xr, ext_r, rtail_ref, rcw_ref, rcb_ref, T, valid_rows)
    xcr_b = xcr.astype(BF16)
    half = R_WIDTH // 2
    ra_pre = jnp.concatenate(
        [jnp.dot(xcr_b[:, :half], rwa_ref[0], preferred_element_type=F32),
         jnp.dot(xcr_b[:, half:], rwa_ref[1], preferred_element_type=F32)], axis=1) + rba_ref[...]
    rx_pre = jnp.concatenate(
        [jnp.dot(xcr_b[:, :half], rwx_ref[0], preferred_element_type=F32),
         jnp.dot(xcr_b[:, half:], rwx_ref[1], preferred_element_type=F32)], axis=1) + rbx_ref[...]
    log_a = (-RG_C) * jax.nn.sigmoid(ra_pre) * _softplus(-rlam_ref[...])
    a = jnp.exp(log_a)
    th = jnp.tanh(log_a)
    one_minus_a2 = (-2.0 * th) / (1.0 - th)
    bb = jnp.sqrt(one_minus_a2) * (jax.nn.sigmoid(rx_pre) * xcr)
    bb = bb + jnp.where(first_row, a * h_ref[0], 0.0)
    ra_scr[...] = a
    rb_scr[...] = bb
    _scan_real(ra_scr, rb_scr, T, R_WIDTH)
    h_ref[0] = rb_scr[valid_rows - 1:valid_rows, :]
    zr = proj_ref[0, :, MIX_WIDTH + M_WIDTH:MIX_WIDTH + M_WIDTH + R_WIDTH]
    mixed_ref[0, :, M_WIDTH:M_WIDTH + R_WIDTH] = (rb_scr[...] * _silu(zr)).astype(BF16)

    u = proj_ref[0, :, M_WIDTH + R_WIDTH:MIX_WIDTH]
    u_b = u.astype(BF16)
    pr = abre_ref[...]
    pi = abim_ref[...]
    s0r = sre_ref[0]
    s0i = sim_ref[0]
    ure_scr[...] = (jnp.dot(u_b, wbure_ref[...], preferred_element_type=F32)
                    + jnp.where(first_row, pr * s0r - pi * s0i, 0.0))
    uim_scr[...] = (jnp.dot(u_b, wbuim_ref[...], preferred_element_type=F32)
                    + jnp.where(first_row, pr * s0i + pi * s0r, 0.0))
    _scan_cplx_const(ure_scr, uim_scr, abre_ref, abim_ref, T, S_LANES)
    sre_ref[0] = ure_scr[valid_rows - 1:valid_rows, :]
    sim_ref[0] = uim_scr[valid_rows - 1:valid_rows, :]
    y = (jnp.dot(ure_scr[...].astype(BF16), wcre_ref[...], preferred_element_type=F32)
         - jnp.dot(uim_scr[...].astype(BF16), wcim_ref[...], preferred_element_type=F32)
         + sd_ref[...] * u)
    g = jax.nn.gelu(y)
    glu = g * jax.nn.sigmoid(jnp.dot(g.astype(BF16), wglu_ref[...], preferred_element_type=F32) + bglu_ref[...])
    zs = proj_ref[0, :, MIX_WIDTH + M_WIDTH + R_WIDTH:2 * MIX_WIDTH]
    mixed_ref[0, :, M_WIDTH + R_WIDTH:MIX_WIDTH] = (glu * _silu(zs)).astype(BF16)


def _mixer(proj, state, lw, rows, valid_rows):
    bsz, seq, _ = proj.shape
    nchunks = seq // rows

    def per_seq(shape):
        nd = len(shape)
        return pl.BlockSpec((1,) + shape, lambda b, c: (b,) + (0,) * nd)

    state_shapes = [(M_HEADS, M_HEAD_DIM, M_HEAD_DIM), (M_HEADS, M_HEAD_DIM), (M_HEADS, LANES),
                    (SUBLANES, M_WIDTH), (1, R_WIDTH), (SUBLANES, R_WIDTH), (1, S_LANES), (1, S_LANES)]
    state_specs = [per_seq(s) for s in state_shapes]
    weights = [lw[k] for k in ("mcw", "mcb", "wqk", "wvo", "wif", "bif", "mnw", "mskip",
                               "rcw", "rcb", "rwa", "rwx", "rba", "rbx", "rlam",
                               "abre", "abim", "wbure", "wbuim", "wcre", "wcim", "sd", "wglu", "bglu")]
    weight_specs = [_const_spec(w.shape) for w in weights]
    out_shape = ([jax.ShapeDtypeStruct((bsz, seq, MIX_WIDTH), BF16)]
                 + [jax.ShapeDtypeStruct((bsz,) + s, F32) for s in state_shapes])
    scratch = [pltpu.VMEM((rows + SUBLANES, M_WIDTH), F32), pltpu.VMEM((rows + SUBLANES, R_WIDTH), F32),
               pltpu.VMEM((rows, M_WIDTH), F32), pltpu.VMEM((rows, M_WIDTH), F32),
               pltpu.VMEM((rows, M_WIDTH), F32), pltpu.VMEM((rows, M_WIDTH), F32),
               pltpu.VMEM((rows, R_WIDTH), F32), pltpu.VMEM((rows, R_WIDTH), F32),
               pltpu.VMEM((rows, S_LANES), F32), pltpu.VMEM((rows, S_LANES), F32)]
    return pl.pallas_call(
        functools.partial(_mixer_kernel, rows=rows, valid_rows=valid_rows),
        grid=(bsz, nchunks),
        in_specs=[pl.BlockSpec((1, rows, 2 * MIX_WIDTH), lambda b, c: (b, c, 0))] + state_specs + weight_specs,
        out_specs=[pl.BlockSpec((1, rows, MIX_WIDTH), lambda b, c: (b, c, 0))] + state_specs,
        out_shape=out_shape,
        scratch_shapes=scratch,
        compiler_params=pltpu.CompilerParams(
            dimension_semantics=("parallel", "arbitrary"), vmem_limit_bytes=VMEM_LIMIT_BYTES),
        name="mixer",
    )(proj, *state, *weights)


def _block_diag(blocks):
    n, a, b = blocks.shape
    eye = jnp.eye(n, dtype=blocks.dtype)
    return jnp.einsum("nm,nab->namb", eye, blocks).reshape(n * a, n * b)


def _layer_weights(l, p, s5):
    abre, abim, bbre_t, bbim_t = s5
    eye_g = jnp.eye(S_GROUPS, dtype=F32)

    def bu_matrix(bb_t):
        return jnp.einsum("gh,chp->gchp", eye_g, bb_t.reshape(S_GROUP, S_GROUPS, S_STATE)).reshape(
            S_WIDTH, S_LANES).astype(BF16)

    def c_matrix(c):
        return jnp.einsum("hg,gcp->hpgc", eye_g, c).reshape(S_LANES, S_WIDTH).astype(BF16)

    wif = jnp.zeros((3 * M_WIDTH, LANES), F32).at[:, :2 * M_HEADS].set(p["m_w_if"][l]).astype(BF16)
    bif = jnp.zeros((1, LANES), F32).at[0, :2 * M_HEADS].set(p["m_b_if"][l])
    nb_half = R_BLOCKS // 2
    return {
        "norm_w": p["norm_w"][l].reshape(1, D_MODEL),
        "w_in": p["w_in"][l].astype(BF16),
        "w_out": p["w_out"][l].astype(BF16),
        "mcw": p["m_conv_w"][l], "mcb": p["m_conv_b"][l].reshape(1, M_WIDTH),
        "wqk": jnp.concatenate([p["m_wq"][l], p["m_wk"][l]], axis=-1).astype(BF16),
        "wvo": jnp.concatenate([p["m_wv"][l], p["m_wo"][l]], axis=-1).astype(BF16),
        "wif": wif, "bif": bif,
        "mnw": p["m_norm_w"][l].reshape(1, M_WIDTH), "mskip": p["m_skip"][l].reshape(1, M_WIDTH),
        "rcw": p["r_conv_w"][l], "rcb": p["r_conv_b"][l].reshape(1, R_WIDTH),
        "rwa": jnp.stack([_block_diag(p["r_wa"][l][:nb_half]), _block_diag(p["r_wa"][l][nb_half:])]).astype(BF16),
        "rwx": jnp.stack([_block_diag(p["r_wx"][l][:nb_half]), _block_diag(p["r_wx"][l][nb_half:])]).astype(BF16),
        "rba": p["r_ba"][l].reshape(1, R_WIDTH), "rbx": p["r_bx"][l].reshape(1, R_WIDTH),
        "rlam": p["r_lam"][l].reshape(1, R_WIDTH),
        "abre": abre[l], "abim": abim[l],
        "wbure": bu_matrix(bbre_t[l]), "wbuim": bu_matrix(bbim_t[l]),
        "wcre": c_matrix(p["s_c_re"][l]), "wcim": c_matrix(p["s_c_im"][l]),
        "sd": p["s_d"][l].reshape(1, S_WIDTH),
        "wglu": p["s_w_glu"][l].astype(BF16), "bglu": p["s_b_glu"][l].reshape(1, S_WIDTH),
    }


def _state_to_kernel(c, n, m, mconv, h, rconv, sre, sim):
    bsz = c.shape[0]
    pad = ((0, 0), (SUBLANES - (CONV_W - 1), 0), (0, 0))
    return (c, n, jnp.broadcast_to(m[:, :, None], (bsz, M_HEADS, LANES)),
            jnp.pad(mconv, pad), h.reshape(bsz, 1, R_WIDTH), jnp.pad(rconv, pad),
            sre.reshape(bsz, 1, S_LANES), sim.reshape(bsz, 1, S_LANES))


def _state_from_kernel(c, n, m, mtail, h, rtail, sre, sim):
    bsz = c.shape[0]
    keep = SUBLANES - (CONV_W - 1)
    return (c, n, m[:, :, 0], mtail[:, keep:], h.reshape(bsz, R_WIDTH), rtail[:, keep:],
            sre.reshape(bsz, S_GROUPS, S_STATE), sim.reshape(bsz, S_GROUPS, S_STATE))


def _zero_state(bsz):
    return (jnp.zeros((bsz, M_HEADS, M_HEAD_DIM, M_HEAD_DIM), F32), jnp.zeros((bsz, M_HEADS, M_HEAD_DIM), F32),
            jnp.zeros((bsz, M_HEADS), F32), jnp.zeros((bsz, CONV_W - 1, M_WIDTH), F32),
            jnp.zeros((bsz, R_WIDTH), F32), jnp.zeros((bsz, CONV_W - 1, R_WIDTH), F32),
            jnp.zeros((bsz, S_GROUPS, S_STATE), F32), jnp.zeros((bsz, S_GROUPS, S_STATE), F32))


def _layer(x, state, lw, rows, valid_rows, final, final_w):
    bsz, seq, _ = x.shape
    x2 = x.reshape(bsz * seq, D_MODEL)
    proj = _inproj(x2, lw["norm_w"], lw["w_in"]).reshape(bsz, seq, 2 * MIX_WIDTH)
    outs = _mixer(proj, _state_to_kernel(*state), lw, rows, valid_rows)
    y = _outproj(outs[0].reshape(bsz * seq, MIX_WIDTH), x2, lw["w_out"], final_w, final)
    return y.reshape(bsz, seq, D_MODEL), _state_from_kernel(*outs[1:])


def kernel(x_prompt, x_sample, state_mlstm_C, state_mlstm_n, state_mlstm_m, state_mlstm_conv, state_rglru_h, state_rglru_conv, state_s5_re, state_s5_im, norm_w, w_in, w_out, m_conv_w, m_conv_b, m_wq, m_wk, m_wv, m_wo, m_w_if, m_b_if, m_norm_w, m_skip, r_conv_w, r_conv_b, r_wa, r_ba, r_wx, r_bx, r_lam, s_lam_re, s_lam_im, s_b_re, s_b_im, s_c_re, s_c_im, s_d, s_log_step, s_w_glu, s_b_glu, final_norm_w):
    p = dict(norm_w=norm_w, w_in=w_in, w_out=w_out, m_conv_w=m_conv_w, m_conv_b=m_conv_b, m_wq=m_wq, m_wk=m_wk,
             m_wv=m_wv, m_wo=m_wo, m_w_if=m_w_if, m_b_if=m_b_if, m_norm_w=m_norm_w, m_skip=m_skip,
             r_conv_w=r_conv_w, r_conv_b=r_conv_b, r_wa=r_wa, r_ba=r_ba, r_wx=r_wx, r_bx=r_bx, r_lam=r_lam,
             s_c_re=s_c_re, s_c_im=s_c_im, s_d=s_d, s_w_glu=s_w_glu, s_b_glu=s_b_glu)
    s5 = _s5_prep(s_lam_re, s_lam_im, s_log_step, s_b_re, s_b_im)
    final_w = final_norm_w.reshape(1, D_MODEL)

    dec_batch, dec_seq, _ = x_sample.shape
    xp = x_prompt
    xs = jnp.pad(x_sample, ((0, 0), (0, SAMPLE_ROWS - dec_seq), (0, 0)))
    sample_state = (state_mlstm_C, state_mlstm_n, state_mlstm_m, state_mlstm_conv,
                    state_rglru_h, state_rglru_conv, state_s5_re, state_s5_im)
    new_pr, new_sa = [], []
    for l in range(DEPTH):
        lw = _layer_weights(l, p, s5)
        final = l == DEPTH - 1
        xp, st_p = _layer(xp, _zero_state(xp.shape[0]), lw, PROMPT_CHUNK, PROMPT_CHUNK, final, final_w)
        xs, st_s = _layer(xs, tuple(s[l] for s in sample_state), lw, SAMPLE_ROWS, dec_seq, final, final_w)
        new_pr.append(st_p)
        new_sa.append(st_s)
    pr = [jnp.stack([st[i] for st in new_pr]) for i in range(8)]
    sa = [jnp.stack([st[i] for st in new_sa]) for i in range(8)]
    return (xp, xs[:, :dec_seq], *pr, *sa)
```

```python
import functools

import jax
import jax.numpy as jnp
from jax import lax
from jax.experimental import pallas as pl
from jax.experimental.pallas import tpu as pltpu

F32 = jnp.float32
BF16 = jnp.bfloat16

D_MODEL = 2048
DEPTH = 2
MIX_WIDTH = D_MODEL
M_WIDTH = MIX_WIDTH // 2
R_WIDTH = MIX_WIDTH // 4
S_WIDTH = MIX_WIDTH - M_WIDTH - R_WIDTH
M_HEADS = 8
M_HEAD_DIM = M_WIDTH // M_HEADS
R_BLOCKS = 8
RG_C = 8.0
S_GROUP = 16
S_GROUPS = S_WIDTH // S_GROUP
S_STATE = 64
S_LANES = S_GROUPS * S_STATE
CONV_W = 4
EPS = 1e-6

SUBLANES = 8
LANES = 128
VMEM_LIMIT_BYTES = 56 * 1024 * 1024

PROMPT_CHUNK = 256
SAMPLE_GROUP = 8
SAMPLE_LEAD = SUBLANES - 4
INPROJ_TM = 512
INPROJ_TN = 1024
OUTPROJ_TM = 256
SCAN_LANE_BLOCK = 256

_NT = (((1,), (1,)), ((), ()))


def _const_spec(shape, single_buffer=False):
    nd = len(shape)
    if single_buffer:
        return pl.BlockSpec(shape, lambda *_: (0,) * nd, pipeline_mode=pl.Buffered(1))
    return pl.BlockSpec(shape, lambda *_: (0,) * nd)


def _s5_prep_kernel(lr_ref, li_ref, ls_ref, brt_ref, bit_ref, abre_ref, abim_ref, bbre_ref, bbim_ref):
    lr = lr_ref[0]
    li = li_ref[0]
    dt = jnp.exp(ls_ref[0])
    mag = jnp.exp(lr * dt)
    ang = li * dt
    ab_re = mag * jnp.cos(ang)
    ab_im = mag * jnp.sin(ang)
    den = lr * lr + li * li
    nr = ab_re - 1.0
    f_re = (nr * lr + ab_im * li) / den
    f_im = (ab_im * lr - nr * li) / den
    br = brt_ref[0]
    bi = bit_ref[0]
    abre_ref[0] = ab_re
    abim_ref[0] = ab_im
    bbre_ref[0] = f_re * br - f_im * bi
    bbim_ref[0] = f_re * bi + f_im * br


def _s5_prep(s_lam_re, s_lam_im, s_log_step, s_b_re, s_b_im):
    lr = s_lam_re.reshape(DEPTH, 1, S_LANES)
    li = s_lam_im.reshape(DEPTH, 1, S_LANES)
    ls = jnp.repeat(s_log_step, S_STATE, axis=-1).reshape(DEPTH, 1, S_LANES)
    brt = s_b_re.reshape(DEPTH, S_LANES, S_GROUP).transpose(0, 2, 1)
    bit = s_b_im.reshape(DEPTH, S_LANES, S_GROUP).transpose(0, 2, 1)
    vec = pl.BlockSpec((1, 1, S_LANES), lambda l: (l, 0, 0))
    mat = pl.BlockSpec((1, S_GROUP, S_LANES), lambda l: (l, 0, 0))
    return pl.pallas_call(
        _s5_prep_kernel,
        grid=(DEPTH,),
        in_specs=[vec, vec, vec, mat, mat],
        out_specs=[vec, vec, mat, mat],
        out_shape=[jax.ShapeDtypeStruct((DEPTH, 1, S_LANES), F32),
                   jax.ShapeDtypeStruct((DEPTH, 1, S_LANES), F32),
                   jax.ShapeDtypeStruct((DEPTH, S_GROUP, S_LANES), F32),
                   jax.ShapeDtypeStruct((DEPTH, S_GROUP, S_LANES), F32)],
        name="s5_prep",
    )(lr, li, ls, brt, bit)


def _inproj_kernel(x_ref, nw_ref, w_ref, o_ref, xn_ref):
    @pl.when(pl.program_id(1) == 0)
    def _():
        x = x_ref[...]
        ms = jnp.mean(x * x, axis=-1, keepdims=True)
        xn_ref[...] = (x * lax.rsqrt(ms + EPS) * nw_ref[...]).astype(BF16)

    o_ref[...] = jnp.dot(xn_ref[...], w_ref[...], preferred_element_type=F32)


def _inproj(x, norm_w, w_in_b):
    n = x.shape[0]
    tm = min(INPROJ_TM, n)
    return pl.pallas_call(
        _inproj_kernel,
        grid=(n // tm, (2 * MIX_WIDTH) // INPROJ_TN),
        in_specs=[pl.BlockSpec((tm, D_MODEL), lambda i, j: (i, 0)),
                  pl.BlockSpec((1, D_MODEL), lambda i, j: (0, 0)),
                  pl.BlockSpec((D_MODEL, INPROJ_TN), lambda i, j: (0, j))],
        out_specs=pl.BlockSpec((tm, INPROJ_TN), lambda i, j: (i, j)),
        out_shape=jax.ShapeDtypeStruct((n, 2 * MIX_WIDTH), F32),
        scratch_shapes=[pltpu.VMEM((tm, D_MODEL), BF16)],
        compiler_params=pltpu.CompilerParams(
            dimension_semantics=("parallel", "arbitrary"), vmem_limit_bytes=VMEM_LIMIT_BYTES),
        name="inproj",
    )(x, norm_w, w_in_b)


def _outproj_kernel(mixed_ref, x_ref, w_ref, fw_ref, o_ref, *, final):
    y = x_ref[...] + jnp.dot(mixed_ref[...], w_ref[...], preferred_element_type=F32)
    if final:
        ms = jnp.mean(y * y, axis=-1, keepdims=True)
        y = y * lax.rsqrt(ms + EPS) * fw_ref[...]
    o_ref[...] = y


def _outproj(mixed, x, w_out_b, final_w, final):
    n = x.shape[0]
    tm = min(OUTPROJ_TM, n)
    return pl.pallas_call(
        functools.partial(_outproj_kernel, final=final),
        grid=(n // tm,),
        in_specs=[pl.BlockSpec((tm, MIX_WIDTH), lambda i: (i, 0)),
                  pl.BlockSpec((tm, D_MODEL), lambda i: (i, 0)),
                  _const_spec((MIX_WIDTH, D_MODEL)),
                  _const_spec((1, D_MODEL))],
        out_specs=pl.BlockSpec((tm, D_MODEL), lambda i: (i, 0)),
        out_shape=jax.ShapeDtypeStruct((n, D_MODEL), F32),
        compiler_params=pltpu.CompilerParams(
            dimension_semantics=("parallel",), vmem_limit_bytes=VMEM_LIMIT_BYTES),
        name="outproj",
    )(mixed, x, w_out_b, final_w)


def _silu(x):
    return x * jax.nn.sigmoid(x)


def _log_sigmoid(x):
    return jnp.minimum(x, 0.0) - jnp.log1p(jnp.exp(-jnp.abs(x)))


def _softplus(x):
    return jnp.maximum(x, 0.0) + jnp.log1p(jnp.exp(-jnp.abs(x)))


def _seg_pos(shape, seg):
    return lax.broadcasted_iota(jnp.int32, shape, 0) & (seg - 1)


def _cumsum_rows(x, seg):
    pos = _seg_pos(x.shape, seg)
    s = 1
    while s < seg:
        x = x + jnp.where(pos >= s, pltpu.roll(x, s, 0), 0.0)
        s *= 2
    return x


def _scan_real(a_ref, b_ref, seg, width):
    def body(j, carry):
        sl = pl.ds(pl.multiple_of(j * SCAN_LANE_BLOCK, SCAN_LANE_BLOCK), SCAN_LANE_BLOCK)
        a = a_ref[:, sl]
        b = b_ref[:, sl]
        pos = _seg_pos(a.shape, seg)
        s = 1
        while s < seg:
            m = pos >= s
            b = jnp.where(m, a * pltpu.roll(b, s, 0) + b, b)
            if 2 * s < seg:
                a = jnp.where(m, a * pltpu.roll(a, s, 0), a)
            s *= 2
        b_ref[:, sl] = b
        return carry

    lax.fori_loop(0, width // SCAN_LANE_BLOCK, body, 0)


def _scan_cplx_const(re_ref, im_ref, pr_ref, pi_ref, seg, width):
    def body(j, carry):
        sl = pl.ds(pl.multiple_of(j * SCAN_LANE_BLOCK, SCAN_LANE_BLOCK), SCAN_LANE_BLOCK)
        sr = re_ref[:, sl]
        si = im_ref[:, sl]
        pr = pr_ref[:, sl]
        pi = pi_ref[:, sl]
        pos = _seg_pos(sr.shape, seg)
        s = 1
        while s < seg:
            m = pos >= s
            sr_sh = pltpu.roll(sr, s, 0)
            si_sh = pltpu.roll(si, s, 0)
            nr = sr + (pr * sr_sh - pi * si_sh)
            ni = si + (pr * si_sh + pi * sr_sh)
            sr = jnp.where(m, nr, sr)
            si = jnp.where(m, ni, si)
            if 2 * s < seg:
                pr, pi = pr * pr - pi * pi, 2.0 * (pr * pi)
            s *= 2
        re_ref[:, sl] = sr
        im_ref[:, sl] = si
        return carry

    lax.fori_loop(0, width // SCAN_LANE_BLOCK, body, 0)


def _row_from_col(col, eye):
    return jnp.sum(jnp.where(eye, col, 0.0), axis=0, keepdims=True)


def _mlstm_project(xc, xm, wqk_ref, wvo_ref, wif_ref, bif_ref, q_scr, k_scr, v_scr, o_scr):
    xc_b = xc.astype(BF16)
    xm_b = xm.astype(BF16)
    for h in range(M_HEADS):
        sl = slice(h * M_HEAD_DIM, (h + 1) * M_HEAD_DIM)
        qk = jnp.dot(xc_b[:, sl], wqk_ref[h], preferred_element_type=F32)
        vo = jnp.dot(xm_b[:, sl], wvo_ref[h], preferred_element_type=F32)
        q_scr[:, sl] = qk[:, :M_HEAD_DIM]
        k_scr[:, sl] = qk[:, M_HEAD_DIM:]
        v_scr[:, sl] = vo[:, :M_HEAD_DIM]
        o_scr[:, sl] = vo[:, M_HEAD_DIM:]
    return (jnp.dot(q_scr[...].astype(BF16), wif_ref[0:M_WIDTH, :], preferred_element_type=F32)
            + jnp.dot(k_scr[...].astype(BF16), wif_ref[M_WIDTH:2 * M_WIDTH, :], preferred_element_type=F32)
            + jnp.dot(v_scr[...].astype(BF16), wif_ref[2 * M_WIDTH:3 * M_WIDTH, :], preferred_element_type=F32)
            + bif_ref[...])


def _head_output(hh, o_pre, xc_h, z_h, mnw_h, mskip_h):
    mu = jnp.mean(hh, axis=1, keepdims=True)
    hc = hh - mu
    var = jnp.mean(hc * hc, axis=1, keepdims=True)
    hn = hc * lax.rsqrt(var + EPS) * mnw_h
    return (jax.nn.sigmoid(o_pre) * hn + mskip_h * xc_h) * _silu(z_h)


def _rglru_coeffs(xcr, rwa_ref, rwx_ref, rba_ref, rbx_ref, rlam_ref):
    xcr_b = xcr.astype(BF16)
    half = R_WIDTH // 2
    ra_pre = jnp.concatenate(
        [jnp.dot(xcr_b[:, :half], rwa_ref[0], preferred_element_type=F32),
         jnp.dot(xcr_b[:, half:], rwa_ref[1], preferred_element_type=F32)], axis=1) + rba_ref[...]
    rx_pre = jnp.concatenate(
        [jnp.dot(xcr_b[:, :half], rwx_ref[0], preferred_element_type=F32),
         jnp.dot(xcr_b[:, half:], rwx_ref[1], preferred_element_type=F32)], axis=1) + rbx_ref[...]
    log_a = (-RG_C) * jax.nn.sigmoid(ra_pre) * _softplus(-rlam_ref[...])
    a = jnp.exp(log_a)
    th = jnp.tanh(log_a)
    one_minus_a2 = (-2.0 * th) / (1.0 - th)
    return a, jnp.sqrt(one_minus_a2) * (jax.nn.sigmoid(rx_pre) * xcr)


def _s5_output(ure_scr, uim_scr, u, wcre_ref, wcim_ref, sd_ref, wglu_ref, bglu_ref):
    y = (jnp.dot(ure_scr[...].astype(BF16), wcre_ref[...], preferred_element_type=F32)
         - jnp.dot(uim_scr[...].astype(BF16), wcim_ref[...], preferred_element_type=F32)
         + sd_ref[...] * u)
    g = jax.nn.gelu(y)
    return g * jax.nn.sigmoid(jnp.dot(g.astype(BF16), wglu_ref[...], preferred_element_type=F32) + bglu_ref[...])


_WEIGHT_KEYS = ("mcw", "mcb", "wqk", "wvo", "wif", "bif", "mnw", "mskip",
                "rcw", "rcb", "rwa", "rwx", "rba", "rbx", "rlam",
                "abre", "abim", "wbure", "wbuim", "wcre", "wcim", "sd", "wglu", "bglu")


def _causal_conv(x, ext_ref, tail_ref, w_ref, b_ref, rows):
    ext_ref[0:SUBLANES, :] = tail_ref[0]
    ext_ref[SUBLANES:SUBLANES + rows, :] = x
    w = w_ref[...]
    out = w[CONV_W - 1:CONV_W, :] * x + b_ref[...]
    for j in range(1, CONV_W):
        out = out + w[CONV_W - 1 - j:CONV_W - j, :] * ext_ref[SUBLANES - j:SUBLANES - j + rows, :]
    tail_ref[0] = ext_ref[rows:rows + SUBLANES, :]
    return out


def _mixer_kernel(proj_ref, c0_ref, n0_ref, m0_ref, mtail0_ref, h0_ref, rtail0_ref, sre0_ref, sim0_ref,
                  mcw_ref, mcb_ref, wqk_ref, wvo_ref, wif_ref, bif_ref, mnw_ref, mskip_ref,
                  rcw_ref, rcb_ref, rwa_ref, rwx_ref, rba_ref, rbx_ref, rlam_ref,
                  abre_ref, abim_ref, wbure_ref, wbuim_ref, wcre_ref, wcim_ref, sd_ref, wglu_ref, bglu_ref,
                  mixed_ref, c_ref, n_ref, m_ref, mtail_ref, h_ref, rtail_ref, sre_ref, sim_ref,
                  ext_m, ext_r, q_scr, k_scr, v_scr, o_scr, ra_scr, rb_scr, ure_scr, uim_scr,
                  *, rows):
    T = rows

    @pl.when(pl.program_id(1) == 0)
    def _():
        c_ref[...] = c0_ref[...]
        n_ref[...] = n0_ref[...]
        m_ref[...] = m0_ref[...]
        mtail_ref[...] = mtail0_ref[...]
        h_ref[...] = h0_ref[...]
        rtail_ref[...] = rtail0_ref[...]
        sre_ref[...] = sre0_ref[...]
        sim_ref[...] = sim0_ref[...]

    first_row = lax.broadcasted_iota(jnp.int32, (T, 1), 0) == 0

    xm = proj_ref[0, :, 0:M_WIDTH]
    xc = _silu(_causal_conv(xm, ext_m, mtail_ref, mcw_ref, mcb_ref, T))
    gates = _mlstm_project(xc, xm, wqk_ref, wvo_ref, wif_ref, bif_ref, q_scr, k_scr, v_scr, o_scr)
    log_i = gates[:, :LANES]
    b_all = _cumsum_rows(_log_sigmoid(gates[:, LANES:]), T)

    rowi = lax.broadcasted_iota(jnp.int32, (T, T), 0)
    coli = lax.broadcasted_iota(jnp.int32, (T, T), 1)
    causal = coli <= rowi
    eye = coli == rowi
    k_scale = M_HEAD_DIM ** -0.5

    for h in range(M_HEADS):
        sl = slice(h * M_HEAD_DIM, (h + 1) * M_HEAD_DIM)
        bcol = b_all[:, h:h + 1]
        licol = log_i[:, h:h + 1]
        brow = _row_from_col(bcol, eye)
        lirow = _row_from_col(licol, eye)
        m_prev = m_ref[0, h:h + 1, 0:1]
        c_prev = c_ref[0, h]
        n_prev = n_ref[0, h:h + 1, :]

        q = q_scr[:, sl]
        ks = k_scr[:, sl] * k_scale
        v = v_scr[:, sl]
        q_b = q.astype(BF16)
        ks_b = ks.astype(BF16)

        log_d = jnp.where(causal, bcol - brow + lirow, -jnp.inf)
        log_inter = bcol + m_prev
        m_t = jnp.maximum(log_inter, jnp.max(log_d, axis=1, keepdims=True))
        w_intra = jnp.exp(log_d - m_t)
        w_inter = jnp.exp(log_inter - m_t)
        s = lax.dot_general(q_b, ks_b, _NT, preferred_element_type=F32) * w_intra
        inter = lax.dot_general(q_b, c_prev.astype(BF16), _NT, preferred_element_type=F32)
        num = jnp.dot(s.astype(BF16), v.astype(BF16), preferred_element_type=F32) + w_inter * inter
        den = (jnp.sum(s, axis=1, keepdims=True)
               + w_inter * jnp.sum(q * n_prev, axis=1, keepdims=True))
        hh = num / jnp.maximum(jnp.abs(den), jnp.exp(-m_t))

        b_end = bcol[T - 1:T, :]
        m_new = jnp.maximum(b_end + m_prev, jnp.max(b_end - brow + lirow, axis=1, keepdims=True))
        w_src = jnp.exp(b_end - bcol + licol - m_new)
        decay = jnp.exp(b_end + m_prev - m_new)
        vw_t = (v * w_src).T.astype(BF16)
        c_ref[0, h] = decay * c_prev + jnp.dot(vw_t, ks_b, preferred_element_type=F32)
        n_ref[0, h:h + 1, :] = decay * n_prev + jnp.sum(ks * w_src, axis=0, keepdims=True)
        m_ref[0, h:h + 1, :] = jnp.broadcast_to(m_new, (1, LANES))

        z = proj_ref[0, :, MIX_WIDTH + h * M_HEAD_DIM:MIX_WIDTH + (h + 1) * M_HEAD_DIM]
        mixed_ref[0, :, sl] = _head_output(hh, o_scr[:, sl], xc[:, sl], z,
                                           mnw_ref[:, sl], mskip_ref[:, sl]).astype(BF16)

    xr = proj_ref[0, :, M_WIDTH:M_WIDTH + R_WIDTH]
    xcr = _causal_conv(xr, ext_r, rtail_ref, rcw_ref, rcb_ref, T)
    a, bb = _rglru_coeffs(xcr, rwa_ref, rwx_ref, rba_ref, rbx_ref, rlam_ref)
    ra_scr[...] = a
    rb_scr[...] = bb + jnp.where(first_row, a * h_ref[0], 0.0)
    _scan_real(ra_scr, rb_scr, T, R_WIDTH)
    h_ref[0] = rb_scr[T - 1:T, :]
    zr = proj_ref[0, :, MIX_WIDTH + M_WIDTH:MIX_WIDTH + M_WIDTH + R_WIDTH]
    mixed_ref[0, :, M_WIDTH:M_WIDTH + R_WIDTH] = (rb_scr[...] * _silu(zr)).astype(BF16)

    u = proj_ref[0, :, M_WIDTH + R_WIDTH:MIX_WIDTH]
    u_b = u.astype(BF16)
    pr = abre_ref[...]
    pi = abim_ref[...]
    s0r = sre_ref[0]
    s0i = sim_ref[0]
    ure_scr[...] = (jnp.dot(u_b, wbure_ref[...], preferred_element_type=F32)
                    + jnp.where(first_row, pr * s0r - pi * s0i, 0.0))
    uim_scr[...] = (jnp.dot(u_b, wbuim_ref[...], preferred_element_type=F32)
                    + jnp.where(first_row, pr * s0i + pi * s0r, 0.0))
    _scan_cplx_const(ure_scr, uim_scr, abre_ref, abim_ref, T, S_LANES)
    sre_ref[0] = ure_scr[T - 1:T, :]
    sim_ref[0] = uim_scr[T - 1:T, :]
    glu = _s5_output(ure_scr, uim_scr, u, wcre_ref, wcim_ref, sd_ref, wglu_ref, bglu_ref)
    zs = proj_ref[0, :, MIX_WIDTH + M_WIDTH + R_WIDTH:2 * MIX_WIDTH]
    mixed_ref[0, :, M_WIDTH + R_WIDTH:MIX_WIDTH] = (glu * _silu(zs)).astype(BF16)


def _prompt_mixer(proj, lw):
    bsz, seq, _ = proj.shape
    rows = PROMPT_CHUNK

    def per_seq(shape):
        nd = len(shape)
        return pl.BlockSpec((1,) + shape, lambda b, c: (b,) + (0,) * nd)

    state_shapes = [(M_HEADS, M_HEAD_DIM, M_HEAD_DIM), (M_HEADS, M_HEAD_DIM), (M_HEADS, LANES),
                    (SUBLANES, M_WIDTH), (1, R_WIDTH), (SUBLANES, R_WIDTH), (1, S_LANES), (1, S_LANES)]
    state_specs = [per_seq(s) for s in state_shapes]
    zero_state = [jnp.zeros((bsz,) + s, F32) for s in state_shapes]
    weights = [lw[k] for k in _WEIGHT_KEYS]
    weight_specs = [_const_spec(w.shape) for w in weights]
    out_shape = ([jax.ShapeDtypeStruct((bsz, seq, MIX_WIDTH), BF16)]
                 + [jax.ShapeDtypeStruct((bsz,) + s, F32) for s in state_shapes])
    scratch = [pltpu.VMEM((rows + SUBLANES, M_WIDTH), F32), pltpu.VMEM((rows + SUBLANES, R_WIDTH), F32),
               pltpu.VMEM((rows, M_WIDTH), F32), pltpu.VMEM((rows, M_WIDTH), F32),
               pltpu.VMEM((rows, M_WIDTH), F32), pltpu.VMEM((rows, M_WIDTH), F32),
               pltpu.VMEM((rows, R_WIDTH), F32), pltpu.VMEM((rows, R_WIDTH), F32),
               pltpu.VMEM((rows, S_LANES), F32), pltpu.VMEM((rows, S_LANES), F32)]
    return pl.pallas_call(
        functools.partial(_mixer_kernel, rows=rows),
        grid=(bsz, seq // rows),
        in_specs=[pl.BlockSpec((1, rows, 2 * MIX_WIDTH), lambda b, c: (b, c, 0))] + state_specs + weight_specs,
        out_specs=[pl.BlockSpec((1, rows, MIX_WIDTH), lambda b, c: (b, c, 0))] + state_specs,
        out_shape=out_shape,
        scratch_shapes=scratch,
        compiler_params=pltpu.CompilerParams(
            dimension_semantics=("parallel", "arbitrary"), vmem_limit_bytes=VMEM_LIMIT_BYTES),
        name="prompt_mixer",
    )(proj, *zero_state, *weights)


def _seg_last(x, groups):
    x3 = x.reshape(groups, SUBLANES, x.shape[-1])
    return jnp.broadcast_to(x3[:, SUBLANES - 1:SUBLANES, :], x3.shape).reshape(x.shape)


def _seg_max(x, groups):
    x3 = x.reshape(groups, SUBLANES, x.shape[-1])
    return jnp.broadcast_to(jnp.max(x3, axis=1, keepdims=True), x3.shape).reshape(x.shape)


def _seg_sum(x, groups):
    x3 = x.reshape(groups, SUBLANES, x.shape[-1])
    return jnp.broadcast_to(jnp.sum(x3, axis=1, keepdims=True), x3.shape).reshape(x.shape)


def _conv_rolled(xf, w_ref, b_ref):
    w = w_ref[...]
    out = w[CONV_W - 1:CONV_W, :] * xf + b_ref[...]
    for j in range(1, CONV_W):
        out = out + w[CONV_W - 1 - j:CONV_W - j, :] * pltpu.roll(xf, j, 0)
    return out


def _sample_mixer_kernel(*refs, groups, aliased):
    (proj_ref, c0_ref, nrows_ref, mrows_ref, mtail0_ref, h0rows_ref, rtail0_ref, sre0rows_ref, sim0rows_ref,
     mcw_ref, mcb_ref, wqk_ref, wvo_ref, wif_ref, bif_ref, mnw_ref, mskip_ref,
     rcw_ref, rcb_ref, rwa_ref, rwx_ref, rba_ref, rbx_ref, rlam_ref,
     abre_ref, abim_ref, wbure_ref, wbuim_ref, wcre_ref, wcim_ref, sd_ref, wglu_ref, bglu_ref) = refs[:33]
    rest = refs[34:] if aliased else refs[33:]
    (mixed_ref, c_ref, n_ref, m_ref, xmf_ref, h_ref, xrf_ref, sre_ref, sim_ref,
     q_scr, k_scr, v_scr, o_scr, inter_scr, vwt_scr, ksb_scr, dec_scr, ra_scr) = rest
    G = groups
    R = G * SUBLANES

    valid = _seg_pos((R, 1), SUBLANES) >= SAMPLE_LEAD

    xm = jnp.where(valid, proj_ref[:, 0:M_WIDTH], mtail0_ref[...])
    xmf_ref[...] = xm
    xc = _silu(_conv_rolled(xm, mcw_ref, mcb_ref))
    gates = _mlstm_project(xc, xm, wqk_ref, wvo_ref, wif_ref, bif_ref, q_scr, k_scr, v_scr, o_scr)
    log_i = jnp.where(valid, gates[:, :LANES], -jnp.inf)
    log_f = jnp.where(valid, _log_sigmoid(gates[:, LANES:]), 0.0)
    b_all = _cumsum_rows(log_f, SUBLANES)
    b_end = _seg_last(b_all, G)
    m_rows = mrows_ref[...]
    log_src = b_end - b_all + log_i
    m_new = jnp.maximum(b_end + m_rows, _seg_max(log_src, G))
    w_src_all = jnp.exp(log_src - m_new)
    decay_all = jnp.exp(b_end + m_rows - m_new)
    dec_scr[...] = decay_all
    m_ref[...] = m_new

    def inter_body(b, carry):
        r0 = pl.multiple_of(b * SUBLANES, SUBLANES)
        for h in range(M_HEADS):
            sl = slice(h * M_HEAD_DIM, (h + 1) * M_HEAD_DIM)
            q_b = q_scr[pl.ds(r0, SUBLANES), sl].astype(BF16)
            inter_scr[pl.ds(r0, SUBLANES), sl] = lax.dot_general(
                q_b, c0_ref[0, b, h].astype(BF16), _NT, preferred_element_type=F32)
        return carry

    lax.fori_loop(0, G, inter_body, 0)

    rowi = lax.broadcasted_iota(jnp.int32, (R, R), 0)
    coli = lax.broadcasted_iota(jnp.int32, (R, R), 1)
    eye = coli == rowi
    same_causal = jnp.logical_and(coli <= rowi, (coli >> 3) == (rowi >> 3))
    k_scale = M_HEAD_DIM ** -0.5

    for h in range(M_HEADS):
        sl = slice(h * M_HEAD_DIM, (h + 1) * M_HEAD_DIM)
        bcol = b_all[:, h:h + 1]
        licol = log_i[:, h:h + 1]
        mcol = m_rows[:, h:h + 1]
        brow = _row_from_col(bcol, eye)
        lirow = _row_from_col(licol, eye)
        q = q_scr[:, sl]
        ks = k_scr[:, sl] * k_scale
        v = v_scr[:, sl]
        q_b = q.astype(BF16)
        ks_b = ks.astype(BF16)

        log_d = jnp.where(same_causal, bcol - brow + lirow, -jnp.inf)
        log_inter = bcol + mcol
        m_t = jnp.maximum(log_inter, jnp.max(log_d, axis=1, keepdims=True))
        w_intra = jnp.exp(log_d - m_t)
        w_inter = jnp.exp(log_inter - m_t)
        s = lax.dot_general(q_b, ks_b, _NT, preferred_element_type=F32) * w_intra
        num = (jnp.dot(s.astype(BF16), v.astype(BF16), preferred_element_type=F32)
               + w_inter * inter_scr[:, sl])
        den = (jnp.sum(s, axis=1, keepdims=True)
               + w_inter * jnp.sum(q * nrows_ref[:, sl], axis=1, keepdims=True))
        hh = num / jnp.maximum(jnp.abs(den), jnp.exp(-m_t))

        w_src = w_src_all[:, h:h + 1]
        n_ref[:, sl] = decay_all[:, h:h + 1] * nrows_ref[:, sl] + _seg_sum(ks * w_src, G)
        vwt_scr[h] = (v * w_src).T
        ksb_scr[h] = ks_b

        z = proj_ref[:, MIX_WIDTH + h * M_HEAD_DIM:MIX_WIDTH + (h + 1) * M_HEAD_DIM]
        out = _head_output(hh, o_scr[:, sl], xc[:, sl], z, mnw_ref[:, sl], mskip_ref[:, sl])
        mixed_ref[:, sl] = jnp.where(valid, out, 0.0).astype(BF16)

    lane_seg = lax.broadcasted_iota(jnp.int32, (M_HEAD_DIM, R), 1) >> 3

    def update_body(b, carry):
        own = lane_seg == b
        r0 = pl.multiple_of(b * SUBLANES, SUBLANES)
        for h in range(M_HEADS):
            lhs = jnp.where(own, vwt_scr[h], 0.0).astype(BF16)
            upd = jnp.dot(lhs, ksb_scr[h], preferred_element_type=F32)
            c_ref[0, b, h] = dec_scr[pl.ds(r0, 1), h:h + 1] * c0_ref[0, b, h] + upd
        return carry

    lax.fori_loop(0, G, update_body, 0)

    xr = jnp.where(valid, proj_ref[:, M_WIDTH:M_WIDTH + R_WIDTH], rtail0_ref[...])
    xrf_ref[...] = xr
    xcr = _conv_rolled(xr, rcw_ref, rcb_ref)
    a, bb = _rglru_coeffs(xcr, rwa_ref, rwx_ref, rba_ref, rbx_ref, rlam_ref)
    ra_scr[...] = a
    h_ref[...] = jnp.where(valid, bb, 0.0) + a * h0rows_ref[...]
    _scan_real(ra_scr, h_ref, SUBLANES, R_WIDTH)
    zr = proj_ref[:, MIX_WIDTH + M_WIDTH:MIX_WIDTH + M_WIDTH + R_WIDTH]
    mixed_ref[:, M_WIDTH:M_WIDTH + R_WIDTH] = jnp.where(valid, h_ref[...] * _silu(zr), 0.0).astype(BF16)

    u = proj_ref[:, M_WIDTH + R_WIDTH:MIX_WIDTH]
    u_b = u.astype(BF16)
    pr = abre_ref[...]
    pi = abim_ref[...]
    s0r = sre0rows_ref[...]
    s0i = sim0rows_ref[...]
    sre_ref[...] = (jnp.where(valid, jnp.dot(u_b, wbure_ref[...], preferred_element_type=F32), 0.0)
                    + (pr * s0r - pi * s0i))
    sim_ref[...] = (jnp.where(valid, jnp.dot(u_b, wbuim_ref[...], preferred_element_type=F32), 0.0)
                    + (pr * s0i + pi * s0r))
    _scan_cplx_const(sre_ref, sim_ref, abre_ref, abim_ref, SUBLANES, S_LANES)
    glu = _s5_output(sre_ref, sim_ref, u, wcre_ref, wcim_ref, sd_ref, wglu_ref, bglu_ref)
    zs = proj_ref[:, MIX_WIDTH + M_WIDTH + R_WIDTH:2 * MIX_WIDTH]
    mixed_ref[:, M_WIDTH + R_WIDTH:MIX_WIDTH] = jnp.where(valid, glu * _silu(zs), 0.0).astype(BF16)


def _sample_mixer(proj, l, c_all, c_out_prev, row_state, lw):
    nrows = proj.shape[0]
    bsz = nrows // SUBLANES
    G = SAMPLE_GROUP
    R = G * SUBLANES
    aliased = c_out_prev is not None

    def rows_spec(width):
        return pl.BlockSpec((R, width), lambda i: (i, 0))

    c_spec = pl.BlockSpec((1, G, M_HEADS, M_HEAD_DIM, M_HEAD_DIM), lambda i: (l, i, 0, 0, 0))
    weights = [lw[k] for k in _WEIGHT_KEYS]
    state_widths = (M_WIDTH, LANES, M_WIDTH, R_WIDTH, R_WIDTH, S_LANES, S_LANES)
    in_specs = ([rows_spec(2 * MIX_WIDTH), c_spec] + [rows_spec(w) for w in state_widths]
                + [_const_spec(w.shape, single_buffer=True) for w in weights])
    args = [proj, c_all, *row_state, *weights]
    aliases = {}
    if aliased:
        in_specs.append(pl.BlockSpec(memory_space=pl.ANY))
        args.append(c_out_prev)
        aliases = {len(args) - 1: 1}
    out_specs = [rows_spec(MIX_WIDTH), c_spec] + [rows_spec(w) for w in state_widths]
    out_shape = ([jax.ShapeDtypeStruct((nrows, MIX_WIDTH), BF16), jax.ShapeDtypeStruct(c_all.shape, F32)]
                 + [jax.ShapeDtypeStruct((nrows, w), F32) for w in state_widths])
    scratch = [pltpu.VMEM((R, M_WIDTH), F32), pltpu.VMEM((R, M_WIDTH), F32),
               pltpu.VMEM((R, M_WIDTH), F32), pltpu.VMEM((R, M_WIDTH), F32), pltpu.VMEM((R, M_WIDTH), F32),
               pltpu.VMEM((M_HEADS, M_HEAD_DIM, R), F32), pltpu.VMEM((M_HEADS, R, M_HEAD_DIM), BF16),
               pltpu.VMEM((R, LANES), F32), pltpu.VMEM((R, R_WIDTH), F32)]
    return pl.pallas_call(
        functools.partial(_sample_mixer_kernel, groups=G, aliased=aliased),
        grid=(bsz // G,),
        in_specs=in_specs,
        out_specs=out_specs,
        out_shape=out_shape,
        scratch_shapes=scratch,
        input_output_aliases=aliases,
        compiler_params=pltpu.CompilerParams(
            dimension_semantics=("parallel",), vmem_limit_bytes=VMEM_LIMIT_BYTES),
        name="sample_mixer",
    )(*args)


def _block_diag(blocks):
    n, a, b = blocks.shape
    eye = jnp.eye(n, dtype=blocks.dtype)
    return jnp.einsum("nm,nab->namb", eye, blocks).reshape(n * a, n * b)


def _layer_weights(l, p, s5):
    abre, abim, bbre_t, bbim_t = s5
    eye_g = jnp.eye(S_GROUPS, dtype=F32)

    def bu_matrix(bb_t):
        return jnp.einsum("gh,chp->gchp", eye_g, bb_t.reshape(S_GROUP, S_GROUPS, S_STATE)).reshape(
            S_WIDTH, S_LANES).astype(BF16)

    def c_matrix(c):
        return jnp.einsum("hg,gcp->hpgc", eye_g, c).reshape(S_LANES, S_WIDTH).astype(BF16)

    w_if = p["m_w_if"][l]
    b_if = p["m_b_if"][l]
    wif = (jnp.zeros((3 * M_WIDTH, 2 * LANES), F32)
           .at[:, :M_HEADS].set(w_if[:, :M_HEADS])
           .at[:, LANES:LANES + M_HEADS].set(w_if[:, M_HEADS:]).astype(BF16))
    bif = (jnp.zeros((1, 2 * LANES), F32)
           .at[0, :M_HEADS].set(b_if[:M_HEADS])
           .at[0, LANES:LANES + M_HEADS].set(b_if[M_HEADS:]))
    nb_half = R_BLOCKS // 2
    return {
        "norm_w": p["norm_w"][l].reshape(1, D_MODEL),
        "w_in": p["w_in"][l].astype(BF16),
        "w_out": p["w_out"][l].astype(BF16),
        "mcw": p["m_conv_w"][l], "mcb": p["m_conv_b"][l].reshape(1, M_WIDTH),
        "wqk": jnp.concatenate([p["m_wq"][l], p["m_wk"][l]], axis=-1).astype(BF16),
        "wvo": jnp.concatenate([p["m_wv"][l], p["m_wo"][l]], axis=-1).astype(BF16),
        "wif": wif, "bif": bif,
        "mnw": p["m_norm_w"][l].reshape(1, M_WIDTH), "mskip": p["m_skip"][l].reshape(1, M_WIDTH),
        "rcw": p["r_conv_w"][l], "rcb": p["r_conv_b"][l].reshape(1, R_WIDTH),
        "rwa": jnp.stack([_block_diag(p["r_wa"][l][:nb_half]), _block_diag(p["r_wa"][l][nb_half:])]).astype(BF16),
        "rwx": jnp.stack([_block_diag(p["r_wx"][l][:nb_half]), _block_diag(p["r_wx"][l][nb_half:])]).astype(BF16),
        "rba": p["r_ba"][l].reshape(1, R_WIDTH), "rbx": p["r_bx"][l].reshape(1, R_WIDTH),
        "rlam": p["r_lam"][l].reshape(1, R_WIDTH),
        "abre": abre[l], "abim": abim[l],
        "wbure": bu_matrix(bbre_t[l]), "wbuim": bu_matrix(bbim_t[l]),
        "wcre": c_matrix(p["s_c_re"][l]), "wcim": c_matrix(p["s_c_im"][l]),
        "sd": p["s_d"][l].reshape(1, S_WIDTH),
        "wglu": p["s_w_glu"][l].astype(BF16), "bglu": p["s_b_glu"][l].reshape(1, S_WIDTH),
    }


def _prompt_state_from_kernel(c, n, m, mtail, h, rtail, sre, sim):
    bsz = c.shape[0]
    keep = SUBLANES - (CONV_W - 1)
    return (c, n, m[:, :, 0], mtail[:, keep:], h.reshape(bsz, R_WIDTH), rtail[:, keep:],
            sre.reshape(bsz, S_GROUPS, S_STATE), sim.reshape(bsz, S_GROUPS, S_STATE))


def _sample_rows_state(n, m, mconv, h, rconv, sre, sim):
    bsz = n.shape[0]

    def on_first_token(x):
        return jnp.pad(x[:, None, :], ((0, 0), (SAMPLE_LEAD, SUBLANES - SAMPLE_LEAD - 1), (0, 0))).reshape(
            bsz * SUBLANES, x.shape[-1])

    def conv_rows(buf):
        return jnp.pad(buf, ((0, 0), (SAMPLE_LEAD - (CONV_W - 1), SUBLANES - SAMPLE_LEAD), (0, 0))).reshape(
            bsz * SUBLANES, buf.shape[-1])

    m_pad = jnp.pad(m, ((0, 0), (0, LANES - M_HEADS)))
    return (jnp.repeat(n.reshape(bsz, M_WIDTH), SUBLANES, axis=0), jnp.repeat(m_pad, SUBLANES, axis=0),
            conv_rows(mconv), on_first_token(h), conv_rows(rconv),
            on_first_token(sre.reshape(bsz, S_LANES)), on_first_token(sim.reshape(bsz, S_LANES)))


def _sample_state_from_rows(n, m, xmf, h, xrf, sre, sim):
    bsz = n.shape[0] // SUBLANES

    def seg(x):
        return x.reshape(bsz, SUBLANES, x.shape[-1])

    conv_from = SUBLANES - (CONV_W - 1)
    return (seg(n)[:, -1].reshape(bsz, M_HEADS, M_HEAD_DIM), seg(m)[:, -1, :M_HEADS], seg(xmf)[:, conv_from:],
            seg(h)[:, -1], seg(xrf)[:, conv_from:],
            seg(sre)[:, -1].reshape(bsz, S_GROUPS, S_STATE), seg(sim)[:, -1].reshape(bsz, S_GROUPS, S_STATE))


def kernel(x_prompt, x_sample, state_mlstm_C, state_mlstm_n, state_mlstm_m, state_mlstm_conv, state_rglru_h, state_rglru_conv, state_s5_re, state_s5_im, norm_w, w_in, w_out, m_conv_w, m_conv_b, m_wq, m_wk, m_wv, m_wo, m_w_if, m_b_if, m_norm_w, m_skip, r_conv_w, r_conv_b, r_wa, r_ba, r_wx, r_bx, r_lam, s_lam_re, s_lam_im, s_b_re, s_b_im, s_c_re, s_c_im, s_d, s_log_step, s_w_glu, s_b_glu, final_norm_w):
    p = dict(norm_w=norm_w, w_in=w_in, w_out=w_out, m_conv_w=m_conv_w, m_conv_b=m_conv_b, m_wq=m_wq, m_wk=m_wk,
             m_wv=m_wv, m_wo=m_wo, m_w_if=m_w_if, m_b_if=m_b_if, m_norm_w=m_norm_w, m_skip=m_skip,
             r_conv_w=r_conv_w, r_conv_b=r_conv_b, r_wa=r_wa, r_ba=r_ba, r_wx=r_wx, r_bx=r_bx, r_lam=r_lam,
             s_c_re=s_c_re, s_c_im=s_c_im, s_d=s_d, s_w_glu=s_w_glu, s_b_glu=s_b_glu)
    s5 = _s5_prep(s_lam_re, s_lam_im, s_log_step, s_b_re, s_b_im)
    final_w = final_norm_w.reshape(1, D_MODEL)

    bsz, seq, _ = x_prompt.shape
    dec_batch, dec_seq, _ = x_sample.shape
    xp = x_prompt.reshape(bsz * seq, D_MODEL)
    xs = jnp.pad(x_sample, ((0, 0), (SAMPLE_LEAD, 0), (0, 0))).reshape(dec_batch * SUBLANES, D_MODEL)
    new_pr, new_sa = [], []
    sa_c = None
    for l in range(DEPTH):
        lw = _layer_weights(l, p, s5)
        final = l == DEPTH - 1

        proj = _inproj(xp, lw["norm_w"], lw["w_in"]).reshape(bsz, seq, 2 * MIX_WIDTH)
        outs = _prompt_mixer(proj, lw)
        xp = _outproj(outs[0].reshape(bsz * seq, MIX_WIDTH), xp, lw["w_out"], final_w, final)
        new_pr.append(_prompt_state_from_kernel(*outs[1:]))

        proj = _inproj(xs, lw["norm_w"], lw["w_in"])
        rows_state = _sample_rows_state(state_mlstm_n[l], state_mlstm_m[l], state_mlstm_conv[l], state_rglru_h[l],
                                        state_rglru_conv[l], state_s5_re[l], state_s5_im[l])
        mixed, sa_c, *rows_out = _sample_mixer(proj, l, state_mlstm_C, sa_c, rows_state, lw)
        xs = _outproj(mixed, xs, lw["w_out"], final_w, final)
        new_sa.append(_sample_state_from_rows(*rows_out))
    pr = [jnp.stack([st[i] for st in new_pr]) for i in range(8)]
    sa = [jnp.stack([st[i] for st in new_sa]) for i in range(7)]
    y_prompt = xp.reshape(bsz, seq, D_MODEL)
    y_sample = xs.reshape(dec_batch, SUBLANES, D_MODEL)[:, SAMPLE_LEAD:]
    return (y_prompt, y_sample, *pr, sa_c, *sa)
```

```python
import functools

import jax
import jax.numpy as jnp
from jax import lax
from jax.experimental import pallas as pl
from jax.experimental.pallas import tpu as pltpu

F32 = jnp.float32
BF16 = jnp.bfloat16

D_MODEL = 2048
DEPTH = 2
MIX_WIDTH = D_MODEL
M_WIDTH = MIX_WIDTH // 2
R_WIDTH = MIX_WIDTH // 4
S_WIDTH = MIX_WIDTH - M_WIDTH - R_WIDTH
M_HEADS = 8
M_HEAD_DIM = M_WIDTH // M_HEADS
R_BLOCKS = 8
RG_C = 8.0
S_GROUP = 16
S_GROUPS = S_WIDTH // S_GROUP
S_STATE = 64
S_LANES = S_GROUPS * S_STATE
CONV_W = 4
EPS = 1e-6

SUBLANES = 8
SUBLANE_SHIFT = SUBLANES.bit_length() - 1
LANES = 128
VMEM_LIMIT_BYTES = 56 * 1024 * 1024

PROMPT_CHUNK = 256
SAMPLE_GROUP = 8
SAMPLE_LEAD = SUBLANES - 4
INPROJ_TM = 512
INPROJ_TN = 1024
OUTPROJ_TM = 256
SCAN_LANE_BLOCK = 256

_NT = (((1,), (1,)), ((), ()))


def _const_spec(shape, single_buffer=False):
    nd = len(shape)
    if single_buffer:
        return pl.BlockSpec(shape, lambda *_: (0,) * nd, pipeline_mode=pl.Buffered(1))
    return pl.BlockSpec(shape, lambda *_: (0,) * nd)


def _s5_prep_kernel(lr_ref, li_ref, ls_ref, brt_ref, bit_ref, abre_ref, abim_ref, bbre_ref, bbim_ref):
    lr = lr_ref[0]
    li = li_ref[0]
    dt = jnp.exp(ls_ref[0])
    mag = jnp.exp(lr * dt)
    ang = li * dt
    ab_re = mag * jnp.cos(ang)
    ab_im = mag * jnp.sin(ang)
    den = lr * lr + li * li
    nr = ab_re - 1.0
    f_re = (nr * lr + ab_im * li) / den
    f_im = (ab_im * lr - nr * li) / den
    br = brt_ref[0]
    bi = bit_ref[0]
    abre_ref[0] = ab_re
    abim_ref[0] = ab_im
    bbre_ref[0] = f_re * br - f_im * bi
    bbim_ref[0] = f_re * bi + f_im * br


def _s5_prep(s_lam_re, s_lam_im, s_log_step, s_b_re, s_b_im):
    lr = s_lam_re.reshape(DEPTH, 1, S_LANES)
    li = s_lam_im.reshape(DEPTH, 1, S_LANES)
    ls = jnp.repeat(s_log_step, S_STATE, axis=-1).reshape(DEPTH, 1, S_LANES)
    brt = s_b_re.reshape(DEPTH, S_LANES, S_GROUP).transpose(0, 2, 1)
    bit = s_b_im.reshape(DEPTH, S_LANES, S_GROUP).transpose(0, 2, 1)
    vec = pl.BlockSpec((1, 1, S_LANES), lambda l: (l, 0, 0))
    mat = pl.BlockSpec((1, S_GROUP, S_LANES), lambda l: (l, 0, 0))
    return pl.pallas_call(
        _s5_prep_kernel,
        grid=(DEPTH,),
        in_specs=[vec, vec, vec, mat, mat],
        out_specs=[vec, vec, mat, mat],
        out_shape=[jax.ShapeDtypeStruct((DEPTH, 1, S_LANES), F32),
                   jax.ShapeDtypeStruct((DEPTH, 1, S_LANES), F32),
                   jax.ShapeDtypeStruct((DEPTH, S_GROUP, S_LANES), F32),
                   jax.ShapeDtypeStruct((DEPTH, S_GROUP, S_LANES), F32)],
        name="s5_prep",
    )(lr, li, ls, brt, bit)


def _inproj_kernel(x_ref, nw_ref, w_ref, o_ref, xn_ref):
    @pl.when(pl.program_id(1) == 0)
    def _():
        x = x_ref[...]
        ms = jnp.mean(x * x, axis=-1, keepdims=True)
        xn_ref[...] = (x * lax.rsqrt(ms + EPS) * nw_ref[...]).astype(BF16)

    o_ref[...] = jnp.dot(xn_ref[...], w_ref[...], preferred_element_type=F32)


def _inproj(x, norm_w, w_in_b):
    n = x.shape[0]
    tm = min(INPROJ_TM, n)
    return pl.pallas_call(
        _inproj_kernel,
        grid=(n // tm, (2 * MIX_WIDTH) // INPROJ_TN),
        in_specs=[pl.BlockSpec((tm, D_MODEL), lambda i, j: (i, 0)),
                  pl.BlockSpec((1, D_MODEL), lambda i, j: (0, 0)),
                  pl.BlockSpec((D_MODEL, INPROJ_TN), lambda i, j: (0, j))],
        out_specs=pl.BlockSpec((tm, INPROJ_TN), lambda i, j: (i, j)),
        out_shape=jax.ShapeDtypeStruct((n, 2 * MIX_WIDTH), F32),
        scratch_shapes=[pltpu.VMEM((tm, D_MODEL), BF16)],
        compiler_params=pltpu.CompilerParams(
            dimension_semantics=("parallel", "arbitrary"), vmem_limit_bytes=VMEM_LIMIT_BYTES),
        name="inproj",
    )(x, norm_w, w_in_b)


def _outproj_kernel(mixed_ref, x_ref, w_ref, fw_ref, o_ref, *, final):
    y = x_ref[...] + jnp.dot(mixed_ref[...], w_ref[...], preferred_element_type=F32)
    if final:
        ms = jnp.mean(y * y, axis=-1, keepdims=True)
        y = y * lax.rsqrt(ms + EPS) * fw_ref[...]
    o_ref[...] = y


def _outproj(mixed, x, w_out_b, final_w, final):
    n = x.shape[0]
    tm = min(OUTPROJ_TM, n)
    return pl.pallas_call(
        functools.partial(_outproj_kernel, final=final),
        grid=(n // tm,),
        in_specs=[pl.BlockSpec((tm, MIX_WIDTH), lambda i: (i, 0)),
                  pl.BlockSpec((tm, D_MODEL), lambda i: (i, 0)),
                  _const_spec((MIX_WIDTH, D_MODEL)),
                  _const_spec((1, D_MODEL))],
        out_specs=pl.BlockSpec((tm, D_MODEL), lambda i: (i, 0)),
        out_shape=jax.ShapeDtypeStruct((n, D_MODEL), F32),
        compiler_params=pltpu.CompilerParams(
            dimension_semantics=("parallel",), vmem_limit_bytes=VMEM_LIMIT_BYTES),
        name="outproj",
    )(mixed, x, w_out_b, final_w)


def _silu(x):
    return x * jax.nn.sigmoid(x)


def _log_sigmoid(x):
    return jnp.minimum(x, 0.0) - jnp.log1p(jnp.exp(-jnp.abs(x)))


def _softplus(x):
    return jnp.maximum(x, 0.0) + jnp.log1p(jnp.exp(-jnp.abs(x)))


def _seg_pos(shape, seg):
    return lax.broadcasted_iota(jnp.int32, shape, 0) & (seg - 1)


def _cumsum_rows(x, seg):
    pos = _seg_pos(x.shape, seg)
    s = 1
    while s < seg:
        x = x + jnp.where(pos >= s, pltpu.roll(x, s, 0), 0.0)
        s *= 2
    return x


def _scan_real(a_ref, b_ref, seg, width):
    def body(j, carry):
        sl = pl.ds(pl.multiple_of(j * SCAN_LANE_BLOCK, SCAN_LANE_BLOCK), SCAN_LANE_BLOCK)
        a = a_ref[:, sl]
        b = b_ref[:, sl]
        pos = _seg_pos(a.shape, seg)
        s = 1
        while s < seg:
            m = pos >= s
            b = jnp.where(m, a * pltpu.roll(b, s, 0) + b, b)
            if 2 * s < seg:
                a = jnp.where(m, a * pltpu.roll(a, s, 0), a)
            s *= 2
        b_ref[:, sl] = b
        return carry

    lax.fori_loop(0, width // SCAN_LANE_BLOCK, body, 0)


def _scan_cplx_const(re_ref, im_ref, pr_ref, pi_ref, seg, width):
    def body(j, carry):
        sl = pl.ds(pl.multiple_of(j * SCAN_LANE_BLOCK, SCAN_LANE_BLOCK), SCAN_LANE_BLOCK)
        sr = re_ref[:, sl]
        si = im_ref[:, sl]
        pr = pr_ref[:, sl]
        pi = pi_ref[:, sl]
        pos = _seg_pos(sr.shape, seg)
        s = 1
        while s < seg:
            m = pos >= s
            sr_sh = pltpu.roll(sr, s, 0)
            si_sh = pltpu.roll(si, s, 0)
            nr = sr + (pr * sr_sh - pi * si_sh)
            ni = si + (pr * si_sh + pi * sr_sh)
            sr = jnp.where(m, nr, sr)
            si = jnp.where(m, ni, si)
            if 2 * s < seg:
                pr, pi = pr * pr - pi * pi, 2.0 * (pr * pi)
            s *= 2
        re_ref[:, sl] = sr
        im_ref[:, sl] = si
        return carry

    lax.fori_loop(0, width // SCAN_LANE_BLOCK, body, 0)


def _row_from_col(col, eye):
    return jnp.sum(jnp.where(eye, col, 0.0), axis=0, keepdims=True)


def _mlstm_project(xc, xm, wqk_ref, wvo_ref, wif_ref, bif_ref, q_scr, k_scr, v_scr, o_scr):
    xc_b = xc.astype(BF16)
    xm_b = xm.astype(BF16)
    for h in range(M_HEADS):
        sl = slice(h * M_HEAD_DIM, (h + 1) * M_HEAD_DIM)
        qk = jnp.dot(xc_b[:, sl], wqk_ref[h], preferred_element_type=F32)
        vo = jnp.dot(xm_b[:, sl], wvo_ref[h], preferred_element_type=F32)
        q_scr[:, sl] = qk[:, :M_HEAD_DIM]
        k_scr[:, sl] = qk[:, M_HEAD_DIM:]
        v_scr[:, sl] = vo[:, :M_HEAD_DIM]
        o_scr[:, sl] = vo[:, M_HEAD_DIM:]
    return (jnp.dot(q_scr[...].astype(BF16), wif_ref[0:M_WIDTH, :], preferred_element_type=F32)
            + jnp.dot(k_scr[...].astype(BF16), wif_ref[M_WIDTH:2 * M_WIDTH, :], preferred_element_type=F32)
            + jnp.dot(v_scr[...].astype(BF16), wif_ref[2 * M_WIDTH:3 * M_WIDTH, :], preferred_element_type=F32)
            + bif_ref[...])


def _head_output(hh, o_pre, xc_h, z_h, mnw_h, mskip_h):
    mu = jnp.mean(hh, axis=1, keepdims=True)
    hc = hh - mu
    var = jnp.mean(hc * hc, axis=1, keepdims=True)
    hn = hc * lax.rsqrt(var + EPS) * mnw_h
    return (jax.nn.sigmoid(o_pre) * hn + mskip_h * xc_h) * _silu(z_h)


def _rglru_coeffs(xcr, rwa_ref, rwx_ref, rba_ref, rbx_ref, rlam_ref):
    xcr_b = xcr.astype(BF16)
    half = R_WIDTH // 2
    ra_pre = jnp.concatenate(
        [jnp.dot(xcr_b[:, :half], rwa_ref[0], preferred_element_type=F32),
         jnp.dot(xcr_b[:, half:], rwa_ref[1], preferred_element_type=F32)], axis=1) + rba_ref[...]
    rx_pre = jnp.concatenate(
        [jnp.dot(xcr_b[:, :half], rwx_ref[0], preferred_element_type=F32),
         jnp.dot(xcr_b[:, half:], rwx_ref[1], preferred_element_type=F32)], axis=1) + rbx_ref[...]
    log_a = (-RG_C) * jax.nn.sigmoid(ra_pre) * _softplus(-rlam_ref[...])
    a = jnp.exp(log_a)
    th = jnp.tanh(log_a)
    one_minus_a2 = (-2.0 * th) / (1.0 - th)
    return a, jnp.sqrt(one_minus_a2) * (jax.nn.sigmoid(rx_pre) * xcr)


def _s5_project(u_b, wbu_ref, re_ref, im_ref):
    per_slice = (LANES // S_GROUP) * S_STATE
    for k in range(S_WIDTH // LANES):
        res = jnp.dot(u_b[:, k * LANES:(k + 1) * LANES], wbu_ref[k], preferred_element_type=F32)
        re_ref[:, k * per_slice:(k + 1) * per_slice] = res[:, :per_slice]
        im_ref[:, k * per_slice:(k + 1) * per_slice] = res[:, per_slice:]


def _s5_output(re_ref, im_ref, u, wcre_ref, wcim_ref, sd_ref, wglu_ref, bglu_ref):
    nblk = wcre_ref.shape[0]
    k_blk = S_LANES // nblk
    parts = []
    for m in range(nblk):
        ks = slice(m * k_blk, (m + 1) * k_blk)
        parts.append(jnp.dot(re_ref[:, ks].astype(BF16), wcre_ref[m], preferred_element_type=F32)
                     - jnp.dot(im_ref[:, ks].astype(BF16), wcim_ref[m], preferred_element_type=F32))
    y = jnp.concatenate(parts, axis=1) + sd_ref[...] * u
    g = jax.nn.gelu(y)
    return g * jax.nn.sigmoid(jnp.dot(g.astype(BF16), wglu_ref[...], preferred_element_type=F32) + bglu_ref[...])


_WEIGHT_KEYS = ("mcw", "mcb", "wqk", "wvo", "wif", "bif", "mnw", "mskip",
                "rcw", "rcb", "rwa", "rwx", "rba", "rbx", "rlam",
                "abre", "abim", "wbu", "wcre", "wcim", "sd", "wglu", "bglu")


def _tile(x, i):
    return x[i * SUBLANES:(i + 1) * SUBLANES]


def _conv_interleaved(x, tail_ref, w_ref, b_ref, rows):
    ntiles = rows // SUBLANES
    pos = lax.broadcasted_iota(jnp.int32, (SUBLANES, x.shape[1]), 0)
    prev = tail_ref[0]
    before = [pltpu.roll(jnp.where(pos == SUBLANES - 1, _tile(prev, CONV_W - 1 - d), _tile(x, ntiles - d)), 1, 0)
              for d in range(1, CONV_W)]
    w = w_ref[...]
    out = w[CONV_W - 1:CONV_W, :] * x + b_ref[...]
    for j in range(1, CONV_W):
        shifted = jnp.concatenate(before[:j][::-1] + [x[:rows - j * SUBLANES]], axis=0)
        out = out + w[CONV_W - 1 - j:CONV_W - j, :] * shifted
    tail_ref[0] = x[rows - (CONV_W - 1) * SUBLANES:]
    return out


def _sublane_cumsum(x):
    pos = lax.broadcasted_iota(jnp.int32, x.shape, 0)
    s = 1
    while s < SUBLANES:
        x = x + jnp.where(pos >= s, pltpu.roll(x, s, 0), 0.0)
        s *= 2
    return x


def _cumsum_interleaved(x, rows):
    tiles = [_tile(x, 0)]
    for i in range(1, rows // SUBLANES):
        tiles.append(tiles[-1] + _tile(x, i))
    total = tiles[-1]
    start = _sublane_cumsum(total) - total
    return jnp.concatenate([t + start for t in tiles], axis=0)


def _scan_real_interleaved(a_ref, b_ref, carry, rows):
    ntiles = rows // SUBLANES
    pos = lax.broadcasted_iota(jnp.int32, (SUBLANES, a_ref.shape[1]), 0)
    h = _tile(b_ref, 0)
    aprod = _tile(a_ref, 0)
    for i in range(1, ntiles):
        a = _tile(a_ref, i)
        h = a * h + _tile(b_ref, i)
        aprod = aprod * a
    g = h + jnp.where(pos == 0, aprod * carry, 0.0)
    s = 1
    while s < SUBLANES:
        m = pos >= s
        g = jnp.where(m, aprod * pltpu.roll(g, s, 0) + g, g)
        if 2 * s < SUBLANES:
            aprod = jnp.where(m, aprod * pltpu.roll(aprod, s, 0), aprod)
        s *= 2
    h = jnp.where(pos == 0, carry, pltpu.roll(g, 1, 0))
    for i in range(ntiles):
        h = _tile(a_ref, i) * h + _tile(b_ref, i)
        b_ref[i * SUBLANES:(i + 1) * SUBLANES, :] = h
    return g[SUBLANES - 1:SUBLANES]


def _scan_cplx_interleaved(re_ref, im_ref, pr_ref, pi_ref, cre_ref, cim_ref, rows):
    ntiles = rows // SUBLANES
    width = re_ref.shape[1]
    pos = lax.broadcasted_iota(jnp.int32, (SUBLANES, SCAN_LANE_BLOCK), 0)
    for blk in range(width // SCAN_LANE_BLOCK):
        sl = slice(blk * SCAN_LANE_BLOCK, (blk + 1) * SCAN_LANE_BLOCK)
        pr = jnp.broadcast_to(pr_ref[:, sl], pos.shape)
        pi = jnp.broadcast_to(pi_ref[:, sl], pos.shape)
        c_r = cre_ref[0, :, sl]
        c_i = cim_ref[0, :, sl]
        sr = re_ref[0:SUBLANES, sl]
        si = im_ref[0:SUBLANES, sl]
        for i in range(1, ntiles):
            rs = slice(i * SUBLANES, (i + 1) * SUBLANES)
            sr, si = pr * sr - pi * si + re_ref[rs, sl], pr * si + pi * sr + im_ref[rs, sl]
        qr, qi = pr, pi
        n = 1
        while n < ntiles:
            qr, qi = qr * qr - qi * qi, 2.0 * (qr * qi)
            n *= 2
        gr = sr + jnp.where(pos == 0, qr * c_r - qi * c_i, 0.0)
        gi = si + jnp.where(pos == 0, qr * c_i + qi * c_r, 0.0)
        s = 1
        while s < SUBLANES:
            m = pos >= s
            gr_sh = pltpu.roll(gr, s, 0)
            gi_sh = pltpu.roll(gi, s, 0)
            gr, gi = (jnp.where(m, gr + (qr * gr_sh - qi * gi_sh), gr),
                      jnp.where(m, gi + (qr * gi_sh + qi * gr_sh), gi))
            if 2 * s < SUBLANES:
                qr, qi = qr * qr - qi * qi, 2.0 * (qr * qi)
            s *= 2
        sr = jnp.where(pos == 0, c_r, pltpu.roll(gr, 1, 0))
        si = jnp.where(pos == 0, c_i, pltpu.roll(gi, 1, 0))
        for i in range(ntiles):
            rs = slice(i * SUBLANES, (i + 1) * SUBLANES)
            sr, si = pr * sr - pi * si + re_ref[rs, sl], pr * si + pi * sr + im_ref[rs, sl]
            re_ref[rs, sl] = sr
            im_ref[rs, sl] = si
        cre_ref[0, :, sl] = gr[SUBLANES - 1:SUBLANES]
        cim_ref[0, :, sl] = gi[SUBLANES - 1:SUBLANES]


def _mixer_kernel(proj_ref,
                  mcw_ref, mcb_ref, wqk_ref, wvo_ref, wif_ref, bif_ref, mnw_ref, mskip_ref,
                  rcw_ref, rcb_ref, rwa_ref, rwx_ref, rba_ref, rbx_ref, rlam_ref,
                  abre_ref, abim_ref, wbu_ref, wcre_ref, wcim_ref, sd_ref, wglu_ref, bglu_ref,
                  mixed_ref, c_ref, n_ref, m_ref, mtail_ref, h_ref, rtail_ref, sre_ref, sim_ref,
                  q_scr, k_scr, v_scr, o_scr, ra_scr, rb_scr, ure_scr, uim_scr,
                  *, rows):
    T = rows

    @pl.when(pl.program_id(1) == 0)
    def _():
        for ref in (c_ref, n_ref, m_ref, mtail_ref, h_ref, rtail_ref, sre_ref, sim_ref):
            ref[...] = jnp.zeros(ref.shape, F32)

    xm = proj_ref[0, :, 0:M_WIDTH]
    xc = _silu(_conv_interleaved(xm, mtail_ref, mcw_ref, mcb_ref, T))
    gates = _mlstm_project(xc, xm, wqk_ref, wvo_ref, wif_ref, bif_ref, q_scr, k_scr, v_scr, o_scr)
    log_i = gates[:, :LANES]
    b_all = _cumsum_interleaved(_log_sigmoid(gates[:, LANES:]), T)

    rowi = lax.broadcasted_iota(jnp.int32, (T, T), 0)
    coli = lax.broadcasted_iota(jnp.int32, (T, T), 1)
    sub_len = T // SUBLANES

    def time_of(r):
        return (r & (SUBLANES - 1)) * sub_len + (r >> SUBLANE_SHIFT)

    causal = time_of(coli) <= time_of(rowi)
    eye = coli == rowi
    k_scale = M_HEAD_DIM ** -0.5

    for h in range(M_HEADS):
        sl = slice(h * M_HEAD_DIM, (h + 1) * M_HEAD_DIM)
        bcol = b_all[:, h:h + 1]
        licol = log_i[:, h:h + 1]
        brow = _row_from_col(bcol, eye)
        lirow = _row_from_col(licol, eye)
        m_prev = m_ref[0, h:h + 1, 0:1]
        c_prev = c_ref[0, h]
        n_prev = n_ref[0, h:h + 1, :]

        q = q_scr[:, sl]
        ks = k_scr[:, sl] * k_scale
        v = v_scr[:, sl]
        q_b = q.astype(BF16)
        ks_b = ks.astype(BF16)

        log_d = jnp.where(causal, bcol - brow + lirow, -jnp.inf)
        log_inter = bcol + m_prev
        m_t = jnp.maximum(log_inter, jnp.max(log_d, axis=1, keepdims=True))
        w_intra = jnp.exp(log_d - m_t)
        w_inter = jnp.exp(log_inter - m_t)
        s = lax.dot_general(q_b, ks_b, _NT, preferred_element_type=F32) * w_intra
        inter = lax.dot_general(q_b, c_prev.astype(BF16), _NT, preferred_element_type=F32)
        num = jnp.dot(s.astype(BF16), v.astype(BF16), preferred_element_type=F32) + w_inter * inter
        den = (jnp.sum(s, axis=1, keepdims=True)
               + w_inter * jnp.sum(q * n_prev, axis=1, keepdims=True))
        hh = num / jnp.maximum(jnp.abs(den), jnp.exp(-m_t))

        b_end = bcol[T - 1:T, :]
        m_new = jnp.maximum(b_end + m_prev, jnp.max(b_end - brow + lirow, axis=1, keepdims=True))
        w_src = jnp.exp(b_end - bcol + licol - m_new)
        decay = jnp.exp(b_end + m_prev - m_new)
        vw_t = (v * w_src).T.astype(BF16)
        c_ref[0, h] = decay * c_prev + jnp.dot(vw_t, ks_b, preferred_element_type=F32)
        n_ref[0, h:h + 1, :] = decay * n_prev + jnp.sum(ks * w_src, axis=0, keepdims=True)
        m_ref[0, h:h + 1, :] = jnp.broadcast_to(m_new, (1, LANES))

        z = proj_ref[0, :, MIX_WIDTH + h * M_HEAD_DIM:MIX_WIDTH + (h + 1) * M_HEAD_DIM]
        mixed_ref[0, :, sl] = _head_output(hh, o_scr[:, sl], xc[:, sl], z,
                                           mnw_ref[:, sl], mskip_ref[:, sl]).astype(BF16)

    xr = proj_ref[0, :, M_WIDTH:M_WIDTH + R_WIDTH]
    xcr = _conv_interleaved(xr, rtail_ref, rcw_ref, rcb_ref, T)
    a, bb = _rglru_coeffs(xcr, rwa_ref, rwx_ref, rba_ref, rbx_ref, rlam_ref)
    ra_scr[...] = a
    rb_scr[...] = bb
    h_ref[0] = _scan_real_interleaved(ra_scr, rb_scr, h_ref[0], T)
    zr = proj_ref[0, :, MIX_WIDTH + M_WIDTH:MIX_WIDTH + M_WIDTH + R_WIDTH]
    mixed_ref[0, :, M_WIDTH:M_WIDTH + R_WIDTH] = (rb_scr[...] * _silu(zr)).astype(BF16)

    u = proj_ref[0, :, M_WIDTH + R_WIDTH:MIX_WIDTH]
    _s5_project(u.astype(BF16), wbu_ref, ure_scr, uim_scr)
    _scan_cplx_interleaved(ure_scr, uim_scr, abre_ref, abim_ref, sre_ref, sim_ref, T)
    glu = _s5_output(ure_scr, uim_scr, u, wcre_ref, wcim_ref, sd_ref, wglu_ref, bglu_ref)
    zs = proj_ref[0, :, MIX_WIDTH + M_WIDTH + R_WIDTH:2 * MIX_WIDTH]
    mixed_ref[0, :, M_WIDTH + R_WIDTH:MIX_WIDTH] = (glu * _silu(zs)).astype(BF16)


def _prompt_mixer(proj, lw):
    bsz, seq, _ = proj.shape
    rows = PROMPT_CHUNK

    def per_seq(shape):
        nd = len(shape)
        return pl.BlockSpec((1,) + shape, lambda b, c: (b,) + (0,) * nd)

    tail_rows = (CONV_W - 1) * SUBLANES
    state_shapes = [(M_HEADS, M_HEAD_DIM, M_HEAD_DIM), (M_HEADS, M_HEAD_DIM), (M_HEADS, LANES),
                    (tail_rows, M_WIDTH), (1, R_WIDTH), (tail_rows, R_WIDTH), (1, S_LANES), (1, S_LANES)]
    weights = [lw[k] for k in _WEIGHT_KEYS]
    out_shape = ([jax.ShapeDtypeStruct((bsz, seq, MIX_WIDTH), BF16)]
                 + [jax.ShapeDtypeStruct((bsz,) + s, F32) for s in state_shapes])
    scratch = [pltpu.VMEM((rows, M_WIDTH), F32), pltpu.VMEM((rows, M_WIDTH), F32),
               pltpu.VMEM((rows, M_WIDTH), F32), pltpu.VMEM((rows, M_WIDTH), F32),
               pltpu.VMEM((rows, R_WIDTH), F32), pltpu.VMEM((rows, R_WIDTH), F32),
               pltpu.VMEM((rows, S_LANES), F32), pltpu.VMEM((rows, S_LANES), F32)]
    return pl.pallas_call(
        functools.partial(_mixer_kernel, rows=rows),
        grid=(bsz, seq // rows),
        in_specs=([pl.BlockSpec((1, rows, 2 * MIX_WIDTH), lambda b, c: (b, c, 0))]
                  + [_const_spec(w.shape, single_buffer=True) for w in weights]),
        out_specs=[pl.BlockSpec((1, rows, MIX_WIDTH), lambda b, c: (b, c, 0))] + [per_seq(s) for s in state_shapes],
        out_shape=out_shape,
        scratch_shapes=scratch,
        compiler_params=pltpu.CompilerParams(
            dimension_semantics=("parallel", "arbitrary"), vmem_limit_bytes=VMEM_LIMIT_BYTES),
        name="prompt_mixer",
    )(proj, *weights)


def _seg_last(x, groups):
    x3 = x.reshape(groups, SUBLANES, x.shape[-1])
    return jnp.broadcast_to(x3[:, SUBLANES - 1:SUBLANES, :], x3.shape).reshape(x.shape)


def _seg_max(x, groups):
    x3 = x.reshape(groups, SUBLANES, x.shape[-1])
    return jnp.broadcast_to(jnp.max(x3, axis=1, keepdims=True), x3.shape).reshape(x.shape)


def _seg_sum(x, groups):
    x3 = x.reshape(groups, SUBLANES, x.shape[-1])
    return jnp.broadcast_to(jnp.sum(x3, axis=1, keepdims=True), x3.shape).reshape(x.shape)


def _conv_rolled(xf, w_ref, b_ref):
    w = w_ref[...]
    out = w[CONV_W - 1:CONV_W, :] * xf + b_ref[...]
    for j in range(1, CONV_W):
        out = out + w[CONV_W - 1 - j:CONV_W - j, :] * pltpu.roll(xf, j, 0)
    return out


def _sample_mixer_kernel(*refs, groups, aliased):
    (proj_ref, c0_ref, nrows_ref, mrows_ref, mtail0_ref, h0rows_ref, rtail0_ref, sre0rows_ref, sim0rows_ref,
     mcw_ref, mcb_ref, wqk_ref, wvo_ref, wif_ref, bif_ref, mnw_ref, mskip_ref,
     rcw_ref, rcb_ref, rwa_ref, rwx_ref, rba_ref, rbx_ref, rlam_ref,
     abre_ref, abim_ref, wbu_ref, wcre_ref, wcim_ref, sd_ref, wglu_ref, bglu_ref) = refs[:32]
    rest = refs[33:] if aliased else refs[32:]
    (mixed_ref, c_ref, n_ref, m_ref, xmf_ref, h_ref, xrf_ref, sre_ref, sim_ref,
     q_scr, k_scr, v_scr, o_scr, inter_scr, vwt_scr, ksb_scr, dec_scr, ra_scr) = rest
    G = groups
    R = G * SUBLANES

    valid = _seg_pos((R, 1), SUBLANES) >= SAMPLE_LEAD

    xm = jnp.where(valid, proj_ref[:, 0:M_WIDTH], mtail0_ref[...])
    xmf_ref[...] = xm
    xc = _silu(_conv_rolled(xm, mcw_ref, mcb_ref))
    gates = _mlstm_project(xc, xm, wqk_ref, wvo_ref, wif_ref, bif_ref, q_scr, k_scr, v_scr, o_scr)
    log_i = jnp.where(valid, gates[:, :LANES], -jnp.inf)
    log_f = jnp.where(valid, _log_sigmoid(gates[:, LANES:]), 0.0)
    b_all = _cumsum_rows(log_f, SUBLANES)
    b_end = _seg_last(b_all, G)
    m_rows = mrows_ref[...]
    log_src = b_end - b_all + log_i
    m_new = jnp.maximum(b_end + m_rows, _seg_max(log_src, G))
    w_src_all = jnp.exp(log_src - m_new)
    decay_all = jnp.exp(b_end + m_rows - m_new)
    dec_scr[...] = decay_all
    m_ref[...] = m_new

    def inter_body(b, carry):
        r0 = pl.multiple_of(b * SUBLANES, SUBLANES)
        for h in range(M_HEADS):
            sl = slice(h * M_HEAD_DIM, (h + 1) * M_HEAD_DIM)
            q_b = q_scr[pl.ds(r0, SUBLANES), sl].astype(BF16)
            inter_scr[pl.ds(r0, SUBLANES), sl] = lax.dot_general(
                q_b, c0_ref[0, b, h].astype(BF16), _NT, preferred_element_type=F32)
        return carry

    lax.fori_loop(0, G, inter_body, 0)

    rowi = lax.broadcasted_iota(jnp.int32, (R, R), 0)
    coli = lax.broadcasted_iota(jnp.int32, (R, R), 1)
    eye = coli == rowi
    same_causal = jnp.logical_and(coli <= rowi, (coli >> SUBLANE_SHIFT) == (rowi >> SUBLANE_SHIFT))
    k_scale = M_HEAD_DIM ** -0.5

    for h in range(M_HEADS):
        sl = slice(h * M_HEAD_DIM, (h + 1) * M_HEAD_DIM)
        bcol = b_all[:, h:h + 1]
        licol = log_i[:, h:h + 1]
        mcol = m_rows[:, h:h + 1]
        brow = _row_from_col(bcol, eye)
        lirow = _row_from_col(licol, eye)
        q = q_scr[:, sl]
        ks = k_scr[:, sl] * k_scale
        v = v_scr[:, sl]
        q_b = q.astype(BF16)
        ks_b = ks.astype(BF16)

        log_d = jnp.where(same_causal, bcol - brow + lirow, -jnp.inf)
        log_inter = bcol + mcol
        m_t = jnp.maximum(log_inter, jnp.max(log_d, axis=1, keepdims=True))
        w_intra = jnp.exp(log_d - m_t)
        w_inter = jnp.exp(log_inter - m_t)
        s = lax.dot_general(q_b, ks_b, _NT, preferred_element_type=F32) * w_intra
        num = (jnp.dot(s.astype(BF16), v.astype(BF16), preferred_element_type=F32)
               + w_inter * inter_scr[:, sl])
        den = (jnp.sum(s, axis=1, keepdims=True)
               + w_inter * jnp.sum(q * nrows_ref[:, sl], axis=1, keepdims=True))
        hh = num / jnp.maximum(jnp.abs(den), jnp.exp(-m_t))

        w_src = w_src_all[:, h:h + 1]
        n_ref[:, sl] = decay_all[:, h:h + 1] * nrows_ref[:, sl] + _seg_sum(ks * w_src, G)
        vwt_scr[h] = (v * w_src).T
        ksb_scr[h] = ks_b

        z = proj_ref[:, MIX_WIDTH + h * M_HEAD_DIM:MIX_WIDTH + (h + 1) * M_HEAD_DIM]
        out = _head_output(hh, o_scr[:, sl], xc[:, sl], z, mnw_ref[:, sl], mskip_ref[:, sl])
        mixed_ref[:, sl] = jnp.where(valid, out, 0.0).astype(BF16)

    lane_seg = lax.broadcasted_iota(jnp.int32, (M_HEAD_DIM, R), 1) >> SUBLANE_SHIFT

    def update_body(b, carry):
        own = lane_seg == b
        r0 = pl.multiple_of(b * SUBLANES, SUBLANES)
        for h in range(M_HEADS):
            lhs = jnp.where(own, vwt_scr[h], 0.0).astype(BF16)
            upd = jnp.dot(lhs, ksb_scr[h], preferred_element_type=F32)
            c_ref[0, b, h] = dec_scr[pl.ds(r0, 1), h:h + 1] * c0_ref[0, b, h] + upd
        return carry

    lax.fori_loop(0, G, update_body, 0)

    xr = jnp.where(valid, proj_ref[:, M_WIDTH:M_WIDTH + R_WIDTH], rtail0_ref[...])
    xrf_ref[...] = xr
    xcr = _conv_rolled(xr, rcw_ref, rcb_ref)
    a, bb = _rglru_coeffs(xcr, rwa_ref, rwx_ref, rba_ref, rbx_ref, rlam_ref)
    ra_scr[...] = a
    h_ref[...] = jnp.where(valid, bb, 0.0) + a * h0rows_ref[...]
    _scan_real(ra_scr, h_ref, SUBLANES, R_WIDTH)
    zr = proj_ref[:, MIX_WIDTH + M_WIDTH:MIX_WIDTH + M_WIDTH + R_WIDTH]
    mixed_ref[:, M_WIDTH:M_WIDTH + R_WIDTH] = jnp.where(valid, h_ref[...] * _silu(zr), 0.0).astype(BF16)

    u = proj_ref[:, M_WIDTH + R_WIDTH:MIX_WIDTH]
    u_b = u.astype(BF16)
    pr = abre_ref[...]
    pi = abim_ref[...]
    s0r = sre0rows_ref[...]
    s0i = sim0rows_ref[...]
    _s5_project(u_b, wbu_ref, sre_ref, sim_ref)
    sre_ref[...] = jnp.where(valid, sre_ref[...], 0.0) + (pr * s0r - pi * s0i)
    sim_ref[...] = jnp.where(valid, sim_ref[...], 0.0) + (pr * s0i + pi * s0r)
    _scan_cplx_const(sre_ref, sim_ref, abre_ref, abim_ref, SUBLANES, S_LANES)
    glu = _s5_output(sre_ref, sim_ref, u, wcre_ref, wcim_ref, sd_ref, wglu_ref, bglu_ref)
    zs = proj_ref[:, MIX_WIDTH + M_WIDTH + R_WIDTH:2 * MIX_WIDTH]
    mixed_ref[:, M_WIDTH + R_WIDTH:MIX_WIDTH] = jnp.where(valid, glu * _silu(zs), 0.0).astype(BF16)


def _sample_mixer(proj, l, c_all, c_out_prev, row_state, lw):
    nrows = proj.shape[0]
    bsz = nrows // SUBLANES
    G = SAMPLE_GROUP
    R = G * SUBLANES
    aliased = c_out_prev is not None

    def rows_spec(width):
        return pl.BlockSpec((R, width), lambda i: (i, 0))

    c_spec = pl.BlockSpec((1, G, M_HEADS, M_HEAD_DIM, M_HEAD_DIM), lambda i: (l, i, 0, 0, 0))
    weights = [lw[k] for k in _WEIGHT_KEYS]
    state_widths = (M_WIDTH, LANES, M_WIDTH, R_WIDTH, R_WIDTH, S_LANES, S_LANES)
    in_specs = ([rows_spec(2 * MIX_WIDTH), c_spec] + [rows_spec(w) for w in state_widths]
                + [_const_spec(w.shape, single_buffer=True) for w in weights])
    args = [proj, c_all, *row_state, *weights]
    aliases = {}
    if aliased:
        in_specs.append(pl.BlockSpec(memory_space=pl.ANY))
        args.append(c_out_prev)
        aliases = {len(args) - 1: 1}
    out_specs = [rows_spec(MIX_WIDTH), c_spec] + [rows_spec(w) for w in state_widths]
    out_shape = ([jax.ShapeDtypeStruct((nrows, MIX_WIDTH), BF16), jax.ShapeDtypeStruct(c_all.shape, F32)]
                 + [jax.ShapeDtypeStruct((nrows, w), F32) for w in state_widths])
    scratch = [pltpu.VMEM((R, M_WIDTH), F32), pltpu.VMEM((R, M_WIDTH), F32),
               pltpu.VMEM((R, M_WIDTH), F32), pltpu.VMEM((R, M_WIDTH), F32), pltpu.VMEM((R, M_WIDTH), F32),
               pltpu.VMEM((M_HEADS, M_HEAD_DIM, R), F32), pltpu.VMEM((M_HEADS, R, M_HEAD_DIM), BF16),
               pltpu.VMEM((R, LANES), F32), pltpu.VMEM((R, R_WIDTH), F32)]
    return pl.pallas_call(
        functools.partial(_sample_mixer_kernel, groups=G, aliased=aliased),
        grid=(bsz // G,),
        in_specs=in_specs,
        out_specs=out_specs,
        out_shape=out_shape,
        scratch_shapes=scratch,
        input_output_aliases=aliases,
        compiler_params=pltpu.CompilerParams(
            dimension_semantics=("parallel",), vmem_limit_bytes=VMEM_LIMIT_BYTES),
        name="sample_mixer",
    )(*args)


def _block_diag(blocks):
    n, a, b = blocks.shape
    eye = jnp.eye(n, dtype=blocks.dtype)
    return jnp.einsum("nm,nab->namb", eye, blocks).reshape(n * a, n * b)


def _layer_weights(l, p, s5):
    abre, abim, bbre_t, bbim_t = s5
    n_in = S_WIDTH // LANES
    g_in = LANES // S_GROUP
    n_out = 2
    g_out = S_GROUPS // n_out

    def bu_blocks(bb_t):
        bb4 = bb_t.reshape(S_GROUP, n_in, g_in, S_STATE)
        return jnp.einsum("ab,ckbp->kacbp", jnp.eye(g_in, dtype=F32), bb4).reshape(n_in, LANES, g_in * S_STATE)

    def c_blocks(c):
        c4 = c.reshape(n_out, g_out, S_GROUP, S_STATE)
        return jnp.einsum("ab,mbcp->mapbc", jnp.eye(g_out, dtype=F32), c4).reshape(
            n_out, g_out * S_STATE, g_out * S_GROUP).astype(BF16)

    w_if = p["m_w_if"][l]
    b_if = p["m_b_if"][l]
    wif = (jnp.zeros((3 * M_WIDTH, 2 * LANES), F32)
           .at[:, :M_HEADS].set(w_if[:, :M_HEADS])
           .at[:, LANES:LANES + M_HEADS].set(w_if[:, M_HEADS:]).astype(BF16))
    bif = (jnp.zeros((1, 2 * LANES), F32)
           .at[0, :M_HEADS].set(b_if[:M_HEADS])
           .at[0, LANES:LANES + M_HEADS].set(b_if[M_HEADS:]))
    nb_half = R_BLOCKS // 2
    return {
        "norm_w": p["norm_w"][l].reshape(1, D_MODEL),
        "w_in": p["w_in"][l].astype(BF16),
        "w_out": p["w_out"][l].astype(BF16),
        "mcw": p["m_conv_w"][l], "mcb": p["m_conv_b"][l].reshape(1, M_WIDTH),
        "wqk": jnp.concatenate([p["m_wq"][l], p["m_wk"][l]], axis=-1).astype(BF16),
        "wvo": jnp.concatenate([p["m_wv"][l], p["m_wo"][l]], axis=-1).astype(BF16),
        "wif": wif, "bif": bif,
        "mnw": p["m_norm_w"][l].reshape(1, M_WIDTH), "mskip": p["m_skip"][l].reshape(1, M_WIDTH),
        "rcw": p["r_conv_w"][l], "rcb": p["r_conv_b"][l].reshape(1, R_WIDTH),
        "rwa": jnp.stack([_block_diag(p["r_wa"][l][:nb_half]), _block_diag(p["r_wa"][l][nb_half:])]).astype(BF16),
        "rwx": jnp.stack([_block_diag(p["r_wx"][l][:nb_half]), _block_diag(p["r_wx"][l][nb_half:])]).astype(BF16),
        "rba": p["r_ba"][l].reshape(1, R_WIDTH), "rbx": p["r_bx"][l].reshape(1, R_WIDTH),
        "rlam": p["r_lam"][l].reshape(1, R_WIDTH),
        "abre": abre[l], "abim": abim[l],
        "wbu": jnp.concatenate([bu_blocks(bbre_t[l]), bu_blocks(bbim_t[l])], axis=-1).astype(BF16),
        "wcre": c_blocks(p["s_c_re"][l]), "wcim": c_blocks(p["s_c_im"][l]),
        "sd": p["s_d"][l].reshape(1, S_WIDTH),
        "wglu": p["s_w_glu"][l].astype(BF16), "bglu": p["s_b_glu"][l].reshape(1, S_WIDTH),
    }


def _prompt_state_from_kernel(c, n, m, mtail, h, rtail, sre, sim):
    bsz = c.shape[0]
    return (c, n, m[:, :, 0], mtail[:, SUBLANES - 1::SUBLANES], h.reshape(bsz, R_WIDTH),
            rtail[:, SUBLANES - 1::SUBLANES],
            sre.reshape(bsz, S_GROUPS, S_STATE), sim.reshape(bsz, S_GROUPS, S_STATE))


def _interleave_chunks(x, inverse=False):
    bsz, seq, d = x.shape
    sub_len = PROMPT_CHUNK // SUBLANES
    inner = (sub_len, SUBLANES) if inverse else (SUBLANES, sub_len)
    return x.reshape(bsz, seq // PROMPT_CHUNK, *inner, d).transpose(0, 1, 3, 2, 4).reshape(bsz, seq, d)


def _sample_rows_state(n, m, mconv, h, rconv, sre, sim):
    bsz = n.shape[0]

    def on_first_token(x):
        return jnp.pad(x[:, None, :], ((0, 0), (SAMPLE_LEAD, SUBLANES - SAMPLE_LEAD - 1), (0, 0))).reshape(
            bsz * SUBLANES, x.shape[-1])

    def conv_rows(buf):
        return jnp.pad(buf, ((0, 0), (SAMPLE_LEAD - (CONV_W - 1), SUBLANES - SAMPLE_LEAD), (0, 0))).reshape(
            bsz * SUBLANES, buf.shape[-1])

    m_pad = jnp.pad(m, ((0, 0), (0, LANES - M_HEADS)))
    return (jnp.repeat(n.reshape(bsz, M_WIDTH), SUBLANES, axis=0), jnp.repeat(m_pad, SUBLANES, axis=0),
            conv_rows(mconv), on_first_token(h), conv_rows(rconv),
            on_first_token(sre.reshape(bsz, S_LANES)), on_first_token(sim.reshape(bsz, S_LANES)))


def _sample_state_from_rows(n, m, xmf, h, xrf, sre, sim):
    bsz = n.shape[0] // SUBLANES

    def seg(x):
        return x.reshape(bsz, SUBLANES, x.shape[-1])

    conv_from = SUBLANES - (CONV_W - 1)
    return (seg(n)[:, -1].reshape(bsz, M_HEADS, M_HEAD_DIM), seg(m)[:, -1, :M_HEADS], seg(xmf)[:, conv_from:],
            seg(h)[:, -1], seg(xrf)[:, conv_from:],
            seg(sre)[:, -1].reshape(bsz, S_GROUPS, S_STATE), seg(sim)[:, -1].reshape(bsz, S_GROUPS, S_STATE))


def kernel(x_prompt, x_sample, state_mlstm_C, state_mlstm_n, state_mlstm_m, state_mlstm_conv, state_rglru_h, state_rglru_conv, state_s5_re, state_s5_im, norm_w, w_in, w_out, m_conv_w, m_conv_b, m_wq, m_wk, m_wv, m_wo, m_w_if, m_b_if, m_norm_w, m_skip, r_conv_w, r_conv_b, r_wa, r_ba, r_wx, r_bx, r_lam, s_lam_re, s_lam_im, s_b_re, s_b_im, s_c_re, s_c_im, s_d, s_log_step, s_w_glu, s_b_glu, final_norm_w):
    p = dict(norm_w=norm_w, w_in=w_in, w_out=w_out, m_conv_w=m_conv_w, m_conv_b=m_conv_b, m_wq=m_wq, m_wk=m_wk,
             m_wv=m_wv, m_wo=m_wo, m_w_if=m_w_if, m_b_if=m_b_if, m_norm_w=m_norm_w, m_skip=m_skip,
             r_conv_w=r_conv_w, r_conv_b=r_conv_b, r_wa=r_wa, r_ba=r_ba, r_wx=r_wx, r_bx=r_bx, r_lam=r_lam,
             s_c_re=s_c_re, s_c_im=s_c_im, s_d=s_d, s_w_glu=s_w_glu, s_b_glu=s_b_glu)
    s5 = _s5_prep(s_lam_re, s_lam_im, s_log_step, s_b_re, s_b_im)
    final_w = final_norm_w.reshape(1, D_MODEL)

    bsz, seq, _ = x_prompt.shape
    dec_batch, dec_seq, _ = x_sample.shape
    xp = _interleave_chunks(x_prompt).reshape(bsz * seq, D_MODEL)
    xs = jnp.pad(x_sample, ((0, 0), (SAMPLE_LEAD, 0), (0, 0))).reshape(dec_batch * SUBLANES, D_MODEL)
    new_pr, new_sa = [], []
    sa_c = None
    for l in range(DEPTH):
        lw = _layer_weights(l, p, s5)
        final = l == DEPTH - 1

        proj = _inproj(xp, lw["norm_w"], lw["w_in"]).reshape(bsz, seq, 2 * MIX_WIDTH)
        outs = _prompt_mixer(proj, lw)
        xp = _outproj(outs[0].reshape(bsz * seq, MIX_WIDTH), xp, lw["w_out"], final_w, final)
        new_pr.append(_prompt_state_from_kernel(*outs[1:]))

        proj = _inproj(xs, lw["norm_w"], lw["w_in"])
        rows_state = _sample_rows_state(state_mlstm_n[l], state_mlstm_m[l], state_mlstm_conv[l], state_rglru_h[l],
                                        state_rglru_conv[l], state_s5_re[l], state_s5_im[l])
        mixed, sa_c, *rows_out = _sample_mixer(proj, l, state_mlstm_C, sa_c, rows_state, lw)
        xs = _outproj(mixed, xs, lw["w_out"], final_w, final)
        new_sa.append(_sample_state_from_rows(*rows_out))
    pr = [jnp.stack([st[i] for st in new_pr]) for i in range(8)]
    sa = [jnp.stack([st[i] for st in new_sa]) for i in range(7)]
    y_prompt = _interleave_chunks(xp.reshape(bsz, seq, D_MODEL), inverse=True)
    y_sample = xs.reshape(dec_batch, SUBLANES, D_MODEL)[:, SAMPLE_LEAD:]
    return (y_prompt, y_sample, *pr, sa_c, *sa)
```

```python
import functools

import jax
import jax.numpy as jnp
from jax import lax
from jax.experimental import pallas as pl
from jax.experimental.pallas import tpu as pltpu

F32 = jnp.float32
BF16 = jnp.bfloat16

D_MODEL = 2048
DEPTH = 2
MIX_WIDTH = D_MODEL
M_WIDTH = MIX_WIDTH // 2
R_WIDTH = MIX_WIDTH // 4
S_WIDTH = MIX_WIDTH - M_WIDTH - R_WIDTH
M_HEADS = 8
M_HEAD_DIM = M_WIDTH // M_HEADS
R_BLOCKS = 8
RG_C = 8.0
S_GROUP = 16
S_GROUPS = S_WIDTH // S_GROUP
S_STATE = 64
S_LANES = S_GROUPS * S_STATE
CONV_W = 4
EPS = 1e-6

SUBLANES = 8
SUBLANE_SHIFT = SUBLANES.bit_length() - 1
LANES = 128
VMEM_LIMIT_BYTES = 56 * 1024 * 1024

PROMPT_CHUNK = 256
SAMPLE_GROUP = 8
SAMPLE_LEAD = SUBLANES - 4
INPROJ_TM = 512
INPROJ_TN = 1024
OUTPROJ_TM = 256
SCAN_LANE_BLOCK = 256

_NT = (((1,), (1,)), ((), ()))

_VEC_WIDTH = D_MODEL
_VEC_LAYOUT = {}
_row = 0
for _name, _n, _w in (("mcw", CONV_W, M_WIDTH), ("rcw", CONV_W, R_WIDTH), ("mcb", 1, M_WIDTH), ("mnw", 1, M_WIDTH),
                      ("mskip", 1, M_WIDTH), ("rcb", 1, R_WIDTH), ("rba", 1, R_WIDTH), ("rbx", 1, R_WIDTH),
                      ("rlam", 1, R_WIDTH), ("sd", 1, S_WIDTH), ("bglu", 1, S_WIDTH), ("bif", 1, 2 * LANES),
                      ("abre", 1, S_LANES), ("abim", 1, S_LANES), ("norm_w", 1, D_MODEL)):
    _VEC_LAYOUT[_name] = (_row, _n, _w)
    _row += _n
_VEC_ROWS = -(-_row // SUBLANES) * SUBLANES


def _vec(vec_ref, name):
    r0, n, w = _VEC_LAYOUT[name]
    return vec_ref[r0:r0 + n, 0:w]


def _layer_spec(arr, l):
    nd = arr.ndim - 1
    return pl.BlockSpec((None,) + arr.shape[1:], lambda *_: (l,) + (0,) * nd, pipeline_mode=pl.Buffered(1))


_ANY_SPEC = pl.BlockSpec(memory_space=pl.ANY)


def _s5_prep_kernel(lr_ref, li_ref, ls_ref, brt_ref, bit_ref, abre_ref, abim_ref, bbre_ref, bbim_ref):
    lr = lr_ref[0]
    li = li_ref[0]
    dt = jnp.exp(ls_ref[0])
    mag = jnp.exp(lr * dt)
    ang = li * dt
    ab_re = mag * jnp.cos(ang)
    ab_im = mag * jnp.sin(ang)
    den = lr * lr + li * li
    nr = ab_re - 1.0
    f_re = (nr * lr + ab_im * li) / den
    f_im = (ab_im * lr - nr * li) / den
    br = brt_ref[0]
    bi = bit_ref[0]
    abre_ref[0] = ab_re
    abim_ref[0] = ab_im
    bbre_ref[0] = f_re * br - f_im * bi
    bbim_ref[0] = f_re * bi + f_im * br


def _s5_prep(s_lam_re, s_lam_im, s_log_step, s_b_re, s_b_im):
    lr = s_lam_re.reshape(DEPTH, 1, S_LANES)
    li = s_lam_im.reshape(DEPTH, 1, S_LANES)
    ls = jnp.repeat(s_log_step, S_STATE, axis=-1).reshape(DEPTH, 1, S_LANES)
    brt = s_b_re.reshape(DEPTH, S_LANES, S_GROUP).transpose(0, 2, 1)
    bit = s_b_im.reshape(DEPTH, S_LANES, S_GROUP).transpose(0, 2, 1)
    vec = pl.BlockSpec((1, 1, S_LANES), lambda l: (l, 0, 0))
    mat = pl.BlockSpec((1, S_GROUP, S_LANES), lambda l: (l, 0, 0))
    return pl.pallas_call(
        _s5_prep_kernel,
        grid=(DEPTH,),
        in_specs=[vec, vec, vec, mat, mat],
        out_specs=[vec, vec, mat, mat],
        out_shape=[jax.ShapeDtypeStruct((DEPTH, 1, S_LANES), F32),
                   jax.ShapeDtypeStruct((DEPTH, 1, S_LANES), F32),
                   jax.ShapeDtypeStruct((DEPTH, S_GROUP, S_LANES), F32),
                   jax.ShapeDtypeStruct((DEPTH, S_GROUP, S_LANES), F32)],
        name="s5_prep",
    )(lr, li, ls, brt, bit)


def _inproj_kernel(x_ref, vec_ref, w_ref, o_ref):
    x = x_ref[...]
    ms = jnp.mean(x * x, axis=-1, keepdims=True)
    xn = (x * lax.rsqrt(ms + EPS) * _vec(vec_ref, "norm_w")).astype(BF16)
    for j in range((2 * MIX_WIDTH) // INPROJ_TN):
        cols = slice(j * INPROJ_TN, (j + 1) * INPROJ_TN)
        o_ref[:, cols] = jnp.dot(xn, w_ref[:, cols], preferred_element_type=F32)


def _inproj(x, vec, w_in_b, l):
    n = x.shape[0]
    return pl.pallas_call(
        _inproj_kernel,
        grid=(n // INPROJ_TM,),
        in_specs=[pl.BlockSpec((INPROJ_TM, D_MODEL), lambda i: (i, 0)), _layer_spec(vec, l), _layer_spec(w_in_b, l)],
        out_specs=pl.BlockSpec((INPROJ_TM, 2 * MIX_WIDTH), lambda i: (i, 0)),
        out_shape=jax.ShapeDtypeStruct((n, 2 * MIX_WIDTH), F32),
        compiler_params=pltpu.CompilerParams(
            dimension_semantics=("parallel",), vmem_limit_bytes=VMEM_LIMIT_BYTES),
        name="inproj",
    )(x, vec, w_in_b)


def _outproj_kernel(mixed_ref, x_ref, w_ref, fw_ref, o_ref, *, final):
    y = x_ref[...] + jnp.dot(mixed_ref[...], w_ref[...], preferred_element_type=F32)
    if final:
        ms = jnp.mean(y * y, axis=-1, keepdims=True)
        y = y * lax.rsqrt(ms + EPS) * fw_ref[...]
    o_ref[...] = y


def _outproj(mixed, x, w_out_b, l, final_w, final):
    n = x.shape[0]
    return pl.pallas_call(
        functools.partial(_outproj_kernel, final=final),
        grid=(n // OUTPROJ_TM,),
        in_specs=[pl.BlockSpec((OUTPROJ_TM, MIX_WIDTH), lambda i: (i, 0)),
                  pl.BlockSpec((OUTPROJ_TM, D_MODEL), lambda i: (i, 0)),
                  _layer_spec(w_out_b, l),
                  pl.BlockSpec((1, D_MODEL), lambda i: (0, 0))],
        out_specs=pl.BlockSpec((OUTPROJ_TM, D_MODEL), lambda i: (i, 0)),
        out_shape=jax.ShapeDtypeStruct((n, D_MODEL), F32),
        compiler_params=pltpu.CompilerParams(
            dimension_semantics=("parallel",), vmem_limit_bytes=VMEM_LIMIT_BYTES),
        name="outproj",
    )(mixed, x, w_out_b, final_w)


def _silu(x):
    return x * jax.nn.sigmoid(x)


def _log_sigmoid(x):
    return jnp.minimum(x, 0.0) - jnp.log1p(jnp.exp(-jnp.abs(x)))


def _softplus(x):
    return jnp.maximum(x, 0.0) + jnp.log1p(jnp.exp(-jnp.abs(x)))


def _sublane_pos(shape):
    return lax.broadcasted_iota(jnp.int32, shape, 0) & (SUBLANES - 1)


def _tile_cumsum(x):
    pos = _sublane_pos(x.shape)
    s = 1
    while s < SUBLANES:
        x = x + jnp.where(pos >= s, pltpu.roll(x, s, 0), 0.0)
        s *= 2
    return x


def _tile_scan_real(a, b):
    pos = _sublane_pos(a.shape)
    s = 1
    while s < SUBLANES:
        m = pos >= s
        b = jnp.where(m, a * pltpu.roll(b, s, 0) + b, b)
        if 2 * s < SUBLANES:
            a = jnp.where(m, a * pltpu.roll(a, s, 0), a)
        s *= 2
    return b


def _tile_scan_cplx(sr, si, pr, pi):
    pos = _sublane_pos(sr.shape)
    s = 1
    while s < SUBLANES:
        m = pos >= s
        sr_sh = pltpu.roll(sr, s, 0)
        si_sh = pltpu.roll(si, s, 0)
        sr, si = (jnp.where(m, sr + (pr * sr_sh - pi * si_sh), sr),
                  jnp.where(m, si + (pr * si_sh + pi * sr_sh), si))
        if 2 * s < SUBLANES:
            pr, pi = pr * pr - pi * pi, 2.0 * (pr * pi)
        s *= 2
    return sr, si


def _row_from_col(col, eye):
    return jnp.sum(jnp.where(eye, col, 0.0), axis=0, keepdims=True)


def _mlstm_project(xc, xm, vec_ref, wqk_ref, wvo_ref, wif_ref, q_scr, k_scr, v_scr, o_scr):
    xc_b = xc.astype(BF16)
    xm_b = xm.astype(BF16)
    for h in range(M_HEADS):
        sl = slice(h * M_HEAD_DIM, (h + 1) * M_HEAD_DIM)
        qk = jnp.dot(xc_b[:, sl], wqk_ref[h], preferred_element_type=F32)
        vo = jnp.dot(xm_b[:, sl], wvo_ref[h], preferred_element_type=F32)
        q_scr[:, sl] = qk[:, :M_HEAD_DIM]
        k_scr[:, sl] = qk[:, M_HEAD_DIM:]
        v_scr[:, sl] = vo[:, :M_HEAD_DIM]
        o_scr[:, sl] = vo[:, M_HEAD_DIM:]
    return (jnp.dot(q_scr[...].astype(BF16), wif_ref[0:M_WIDTH, :], preferred_element_type=F32)
            + jnp.dot(k_scr[...].astype(BF16), wif_ref[M_WIDTH:2 * M_WIDTH, :], preferred_element_type=F32)
            + jnp.dot(v_scr[...].astype(BF16), wif_ref[2 * M_WIDTH:3 * M_WIDTH, :], preferred_element_type=F32)
            + _vec(vec_ref, "bif"))


def _head_output(hh, o_pre, xc_h, z_h, mnw_h, mskip_h):
    mu = jnp.mean(hh, axis=1, keepdims=True)
    hc = hh - mu
    var = jnp.mean(hc * hc, axis=1, keepdims=True)
    hn = hc * lax.rsqrt(var + EPS) * mnw_h
    return (jax.nn.sigmoid(o_pre) * hn + mskip_h * xc_h) * _silu(z_h)


def _rglru_coeffs(xcr, vec_ref, rwa_ref, rwx_ref):
    xcr_b = xcr.astype(BF16)
    half = R_WIDTH // 2
    ra_pre = jnp.concatenate(
        [jnp.dot(xcr_b[:, :half], rwa_ref[0], preferred_element_type=F32),
         jnp.dot(xcr_b[:, half:], rwa_ref[1], preferred_element_type=F32)], axis=1) + _vec(vec_ref, "rba")
    rx_pre = jnp.concatenate(
        [jnp.dot(xcr_b[:, :half], rwx_ref[0], preferred_element_type=F32),
         jnp.dot(xcr_b[:, half:], rwx_ref[1], preferred_element_type=F32)], axis=1) + _vec(vec_ref, "rbx")
    log_a = (-RG_C) * jax.nn.sigmoid(ra_pre) * _softplus(-_vec(vec_ref, "rlam"))
    a = jnp.exp(log_a)
    th = jnp.tanh(log_a)
    one_minus_a2 = (-2.0 * th) / (1.0 - th)
    return a, jnp.sqrt(one_minus_a2) * (jax.nn.sigmoid(rx_pre) * xcr)


def _s5_project(u_b, wbu_ref, re_ref, im_ref):
    per_slice = (LANES // S_GROUP) * S_STATE
    for k in range(S_WIDTH // LANES):
        res = jnp.dot(u_b[:, k * LANES:(k + 1) * LANES], wbu_ref[k], preferred_element_type=F32)
        re_ref[:, k * per_slice:(k + 1) * per_slice] = res[:, :per_slice]
        im_ref[:, k * per_slice:(k + 1) * per_slice] = res[:, per_slice:]


def _s5_output(re_ref, im_ref, u, vec_ref, wcre_ref, wcim_ref, wglu_ref):
    nblk = wcre_ref.shape[0]
    k_blk = S_LANES // nblk
    parts = []
    for m in range(nblk):
        ks = slice(m * k_blk, (m + 1) * k_blk)
        parts.append(jnp.dot(re_ref[:, ks].astype(BF16), wcre_ref[m], preferred_element_type=F32)
                     - jnp.dot(im_ref[:, ks].astype(BF16), wcim_ref[m], preferred_element_type=F32))
    y = jnp.concatenate(parts, axis=1) + _vec(vec_ref, "sd") * u
    g = jax.nn.gelu(y)
    return g * jax.nn.sigmoid(jnp.dot(g.astype(BF16), wglu_ref[...], preferred_element_type=F32)
                              + _vec(vec_ref, "bglu"))


_MATRIX_KEYS = ("wqk", "wvo", "wif", "rwa", "rwx", "wbu", "wcre", "wcim", "wglu")


def _tile(x, i):
    return x[i * SUBLANES:(i + 1) * SUBLANES]


def _conv_interleaved(x, tail_ref, w, bias, rows):
    ntiles = rows // SUBLANES
    pos = lax.broadcasted_iota(jnp.int32, (SUBLANES, x.shape[1]), 0)
    prev = tail_ref[...]
    before = [pltpu.roll(jnp.where(pos == SUBLANES - 1, _tile(prev, CONV_W - 1 - d), _tile(x, ntiles - d)), 1, 0)
              for d in range(1, CONV_W)]
    out = w[CONV_W - 1:CONV_W, :] * x + bias
    for j in range(1, CONV_W):
        shifted = jnp.concatenate(before[:j][::-1] + [x[:rows - j * SUBLANES]], axis=0)
        out = out + w[CONV_W - 1 - j:CONV_W - j, :] * shifted
    tail_ref[...] = x[rows - (CONV_W - 1) * SUBLANES:]
    return out


def _cumsum_interleaved(x, rows):
    tiles = [_tile(x, 0)]
    for i in range(1, rows // SUBLANES):
        tiles.append(tiles[-1] + _tile(x, i))
    total = tiles[-1]
    start = _tile_cumsum(total) - total
    return jnp.concatenate([t + start for t in tiles], axis=0)


def _scan_real_interleaved(a_ref, b_ref, carry, rows):
    ntiles = rows // SUBLANES
    pos = lax.broadcasted_iota(jnp.int32, (SUBLANES, a_ref.shape[1]), 0)
    h = _tile(b_ref, 0)
    aprod = _tile(a_ref, 0)
    for i in range(1, ntiles):
        a = _tile(a_ref, i)
        h = a * h + _tile(b_ref, i)
        aprod = aprod * a
    g = _tile_scan_real(aprod, h + jnp.where(pos == 0, aprod * carry, 0.0))
    h = jnp.where(pos == 0, carry, pltpu.roll(g, 1, 0))
    for i in range(ntiles):
        h = _tile(a_ref, i) * h + _tile(b_ref, i)
        b_ref[i * SUBLANES:(i + 1) * SUBLANES, :] = h
    return g[SUBLANES - 1:SUBLANES]


def _scan_cplx_interleaved(re_ref, im_ref, p_re, p_im, cre_ref, cim_ref, rows):
    ntiles = rows // SUBLANES
    width = re_ref.shape[1]
    pos = lax.broadcasted_iota(jnp.int32, (SUBLANES, SCAN_LANE_BLOCK), 0)
    for blk in range(width // SCAN_LANE_BLOCK):
        sl = slice(blk * SCAN_LANE_BLOCK, (blk + 1) * SCAN_LANE_BLOCK)
        pr = jnp.broadcast_to(p_re[:, sl], pos.shape)
        pi = jnp.broadcast_to(p_im[:, sl], pos.shape)
        c_r = cre_ref[:, sl]
        c_i = cim_ref[:, sl]
        sr = re_ref[0:SUBLANES, sl]
        si = im_ref[0:SUBLANES, sl]
        for i in range(1, ntiles):
            rs = slice(i * SUBLANES, (i + 1) * SUBLANES)
            sr, si = pr * sr - pi * si + re_ref[rs, sl], pr * si + pi * sr + im_ref[rs, sl]
        qr, qi = pr, pi
        n = 1
        while n < ntiles:
            qr, qi = qr * qr - qi * qi, 2.0 * (qr * qi)
            n *= 2
        gr, gi = _tile_scan_cplx(sr + jnp.where(pos == 0, qr * c_r - qi * c_i, 0.0),
                                 si + jnp.where(pos == 0, qr * c_i + qi * c_r, 0.0), qr, qi)
        sr = jnp.where(pos == 0, c_r, pltpu.roll(gr, 1, 0))
        si = jnp.where(pos == 0, c_i, pltpu.roll(gi, 1, 0))
        for i in range(ntiles):
            rs = slice(i * SUBLANES, (i + 1) * SUBLANES)
            sr, si = pr * sr - pi * si + re_ref[rs, sl], pr * si + pi * sr + im_ref[rs, sl]
            re_ref[rs, sl] = sr
            im_ref[rs, sl] = si
        cre_ref[:, sl] = gr[SUBLANES - 1:SUBLANES]
        cim_ref[:, sl] = gi[SUBLANES - 1:SUBLANES]


def _prompt_mixer_kernel(*refs, rows, n_alias):
    (proj_ref, vec_ref, wqk_ref, wvo_ref, wif_ref, rwa_ref, rwx_ref, wbu_ref, wcre_ref, wcim_ref, wglu_ref) = refs[:11]
    (mixed_ref, c_ref, n_ref, m_ref, mtail_ref, h_ref, rtail_ref, sre_ref, sim_ref,
     q_scr, k_scr, v_scr, o_scr, ra_scr, rb_scr, ure_scr, uim_scr) = refs[11 + n_alias:]
    T = rows

    @pl.when(pl.program_id(1) == 0)
    def _():
        for ref in (c_ref, n_ref, m_ref, mtail_ref, h_ref, rtail_ref, sre_ref, sim_ref):
            ref[...] = jnp.zeros(ref.shape, F32)

    xm = proj_ref[:, 0:M_WIDTH]
    xc = _silu(_conv_interleaved(xm, mtail_ref, _vec(vec_ref, "mcw"), _vec(vec_ref, "mcb"), T))
    gates = _mlstm_project(xc, xm, vec_ref, wqk_ref, wvo_ref, wif_ref, q_scr, k_scr, v_scr, o_scr)
    log_i = gates[:, :LANES]
    b_all = _cumsum_interleaved(_log_sigmoid(gates[:, LANES:]), T)

    rowi = lax.broadcasted_iota(jnp.int32, (T, T), 0)
    coli = lax.broadcasted_iota(jnp.int32, (T, T), 1)
    sub_len = T // SUBLANES

    def time_of(r):
        return (r & (SUBLANES - 1)) * sub_len + (r >> SUBLANE_SHIFT)

    causal = time_of(coli) <= time_of(rowi)
    eye = coli == rowi
    k_scale = M_HEAD_DIM ** -0.5
    mnw = _vec(vec_ref, "mnw")
    mskip = _vec(vec_ref, "mskip")

    for h in range(M_HEADS):
        sl = slice(h * M_HEAD_DIM, (h + 1) * M_HEAD_DIM)
        bcol = b_all[:, h:h + 1]
        licol = log_i[:, h:h + 1]
        brow = _row_from_col(bcol, eye)
        lirow = _row_from_col(licol, eye)
        m_prev = m_ref[h:h + 1, 0:1]
        c_prev = c_ref[h]
        n_prev = n_ref[h:h + 1, :]

        q = q_scr[:, sl]
        ks = k_scr[:, sl] * k_scale
        v = v_scr[:, sl]
        q_b = q.astype(BF16)
        ks_b = ks.astype(BF16)

        log_d = jnp.where(causal, bcol - brow + lirow, -jnp.inf)
        log_inter = bcol + m_prev
        m_t = jnp.maximum(log_inter, jnp.max(log_d, axis=1, keepdims=True))
        w_intra = jnp.exp(log_d - m_t)
        w_inter = jnp.exp(log_inter - m_t)
        s = lax.dot_general(q_b, ks_b, _NT, preferred_element_type=F32) * w_intra
        inter = lax.dot_general(q_b, c_prev.astype(BF16), _NT, preferred_element_type=F32)
        num = jnp.dot(s.astype(BF16), v.astype(BF16), preferred_element_type=F32) + w_inter * inter
        den = (jnp.sum(s, axis=1, keepdims=True)
               + w_inter * jnp.sum(q * n_prev, axis=1, keepdims=True))
        hh = num / jnp.maximum(jnp.abs(den), jnp.exp(-m_t))

        b_end = bcol[T - 1:T, :]
        m_new = jnp.maximum(b_end + m_prev, jnp.max(b_end - brow + lirow, axis=1, keepdims=True))
        w_src = jnp.exp(b_end - bcol + licol - m_new)
        decay = jnp.exp(b_end + m_prev - m_new)
        vw_t = (v * w_src).T.astype(BF16)
        c_ref[h] = decay * c_prev + jnp.dot(vw_t, ks_b, preferred_element_type=F32)
        n_ref[h:h + 1, :] = decay * n_prev + jnp.sum(ks * w_src, axis=0, keepdims=True)
        m_ref[h:h + 1, :] = jnp.broadcast_to(m_new, (1, LANES))

        z = proj_ref[:, MIX_WIDTH + h * M_HEAD_DIM:MIX_WIDTH + (h + 1) * M_HEAD_DIM]
        mixed_ref[:, sl] = _head_output(hh, o_scr[:, sl], xc[:, sl], z, mnw[:, sl], mskip[:, sl]).astype(BF16)

    xr = proj_ref[:, M_WIDTH:M_WIDTH + R_WIDTH]
    xcr = _conv_interleaved(xr, rtail_ref, _vec(vec_ref, "rcw"), _vec(vec_ref, "rcb"), T)
    a, bb = _rglru_coeffs(xcr, vec_ref, rwa_ref, rwx_ref)
    ra_scr[...] = a
    rb_scr[...] = bb
    h_ref[...] = _scan_real_interleaved(ra_scr, rb_scr, h_ref[...], T)
    zr = proj_ref[:, MIX_WIDTH + M_WIDTH:MIX_WIDTH + M_WIDTH + R_WIDTH]
    mixed_ref[:, M_WIDTH:M_WIDTH + R_WIDTH] = (rb_scr[...] * _silu(zr)).astype(BF16)

    u = proj_ref[:, M_WIDTH + R_WIDTH:MIX_WIDTH]
    _s5_project(u.astype(BF16), wbu_ref, ure_scr, uim_scr)
    _scan_cplx_interleaved(ure_scr, uim_scr, _vec(vec_ref, "abre"), _vec(vec_ref, "abim"), sre_ref, sim_ref, T)
    glu = _s5_output(ure_scr, uim_scr, u, vec_ref, wcre_ref, wcim_ref, wglu_ref)
    zs = proj_ref[:, MIX_WIDTH + M_WIDTH + R_WIDTH:2 * MIX_WIDTH]
    mixed_ref[:, M_WIDTH + R_WIDTH:MIX_WIDTH] = (glu * _silu(zs)).astype(BF16)


_PROMPT_STATE_SHAPES = ((M_HEADS, M_HEAD_DIM, M_HEAD_DIM), (M_HEADS, M_HEAD_DIM), (M_HEADS, LANES),
                        ((CONV_W - 1) * SUBLANES, M_WIDTH), (1, R_WIDTH), ((CONV_W - 1) * SUBLANES, R_WIDTH),
                        (1, S_LANES), (1, S_LANES))


def _prompt_mixer(proj, l, bsz, seq, vec, mats, prev_states):
    rows = PROMPT_CHUNK
    nchunk = seq // rows
    aliases = {}
    alias_args = []
    if prev_states is not None:
        alias_args = list(prev_states)
        aliases = {2 + len(mats) + k: 1 + k for k in range(len(alias_args))}

    def state_spec(shape):
        nd = len(shape)
        return pl.BlockSpec((None, None) + shape, lambda b, c: (l, b) + (0,) * nd)

    rows_map = lambda b, c: (b * nchunk + c, 0)
    out_shape = ([jax.ShapeDtypeStruct((proj.shape[0], MIX_WIDTH), BF16)]
                 + [jax.ShapeDtypeStruct((DEPTH, bsz) + s, F32) for s in _PROMPT_STATE_SHAPES])
    scratch = [pltpu.VMEM((rows, M_WIDTH), F32), pltpu.VMEM((rows, M_WIDTH), F32),
               pltpu.VMEM((rows, M_WIDTH), F32), pltpu.VMEM((rows, M_WIDTH), F32),
               pltpu.VMEM((rows, R_WIDTH), F32), pltpu.VMEM((rows, R_WIDTH), F32),
               pltpu.VMEM((rows, S_LANES), F32), pltpu.VMEM((rows, S_LANES), F32)]
    return pl.pallas_call(
        functools.partial(_prompt_mixer_kernel, rows=rows, n_alias=len(alias_args)),
        grid=(bsz, nchunk),
        in_specs=([pl.BlockSpec((rows, 2 * MIX_WIDTH), rows_map), _layer_spec(vec, l)]
                  + [_layer_spec(w, l) for w in mats] + [_ANY_SPEC] * len(alias_args)),
        out_specs=[pl.BlockSpec((rows, MIX_WIDTH), rows_map)] + [state_spec(s) for s in _PROMPT_STATE_SHAPES],
        out_shape=out_shape,
        scratch_shapes=scratch,
        input_output_aliases=aliases,
        compiler_params=pltpu.CompilerParams(
            dimension_semantics=("parallel", "arbitrary"), vmem_limit_bytes=VMEM_LIMIT_BYTES),
        name="prompt_mixer",
    )(proj, vec, *mats, *alias_args)


def _seg_last(x, groups):
    x3 = x.reshape(groups, SUBLANES, x.shape[-1])
    return jnp.broadcast_to(x3[:, SUBLANES - 1:SUBLANES, :], x3.shape).reshape(x.shape)


def _seg_max(x, groups):
    x3 = x.reshape(groups, SUBLANES, x.shape[-1])
    return jnp.broadcast_to(jnp.max(x3, axis=1, keepdims=True), x3.shape).reshape(x.shape)


def _seg_sum(x, groups):
    x3 = x.reshape(groups, SUBLANES, x.shape[-1])
    return jnp.broadcast_to(jnp.sum(x3, axis=1, keepdims=True), x3.shape).reshape(x.shape)


def _conv_rolled(xf, w, bias):
    out = w[CONV_W - 1:CONV_W, :] * xf + bias
    for j in range(1, CONV_W):
        out = out + w[CONV_W - 1 - j:CONV_W - j, :] * pltpu.roll(xf, j, 0)
    return out


_SAMPLE_STATE_WIDTHS = (M_WIDTH, LANES, M_WIDTH, R_WIDTH, R_WIDTH, S_LANES, S_LANES)


def _sample_mixer_kernel(*refs, groups, n_alias):
    (proj_ref, c0_ref, nrows_ref, mrows_ref, mtail0_ref, h0rows_ref, rtail0_ref, sre0rows_ref, sim0rows_ref,
     vec_ref, wqk_ref, wvo_ref, wif_ref, rwa_ref, rwx_ref, wbu_ref, wcre_ref, wcim_ref, wglu_ref) = refs[:19]
    (mixed_ref, c_ref, n_ref, m_ref, xmf_ref, h_ref, xrf_ref, sre_ref, sim_ref,
     q_scr, k_scr, v_scr, o_scr, inter_scr, vwt_scr, ksb_scr, dec_scr) = refs[19 + n_alias:]
    G = groups
    R = G * SUBLANES

    valid = _sublane_pos((R, 1)) >= SAMPLE_LEAD

    xm = jnp.where(valid, proj_ref[:, 0:M_WIDTH], mtail0_ref[...])
    xmf_ref[...] = xm
    xc = _silu(_conv_rolled(xm, _vec(vec_ref, "mcw"), _vec(vec_ref, "mcb")))
    gates = _mlstm_project(xc, xm, vec_ref, wqk_ref, wvo_ref, wif_ref, q_scr, k_scr, v_scr, o_scr)
    log_i = jnp.where(valid, gates[:, :LANES], -jnp.inf)
    log_f = jnp.where(valid, _log_sigmoid(gates[:, LANES:]), 0.0)
    b_all = _tile_cumsum(log_f)
    b_end = _seg_last(b_all, G)
    m_rows = mrows_ref[...]
    log_src = b_end - b_all + log_i
    m_new = jnp.maximum(b_end + m_rows, _seg_max(log_src, G))
    w_src_all = jnp.exp(log_src - m_new)
    decay_all = jnp.exp(b_end + m_rows - m_new)
    dec_scr[...] = decay_all
    m_ref[...] = m_new

    def inter_body(b, carry):
        r0 = pl.multiple_of(b * SUBLANES, SUBLANES)
        for h in range(M_HEADS):
            sl = slice(h * M_HEAD_DIM, (h + 1) * M_HEAD_DIM)
            q_b = q_scr[pl.ds(r0, SUBLANES), sl].astype(BF16)
            inter_scr[pl.ds(r0, SUBLANES), sl] = lax.dot_general(
                q_b, c0_ref[b, h].astype(BF16), _NT, preferred_element_type=F32)
        return carry

    lax.fori_loop(0, G, inter_body, 0)

    rowi = lax.broadcasted_iota(jnp.int32, (R, R), 0)
    coli = lax.broadcasted_iota(jnp.int32, (R, R), 1)
    eye = coli == rowi
    same_causal = jnp.logical_and(coli <= rowi, (coli >> SUBLANE_SHIFT) == (rowi >> SUBLANE_SHIFT))
    k_scale = M_HEAD_DIM ** -0.5
    mnw = _vec(vec_ref, "mnw")
    mskip = _vec(vec_ref, "mskip")

    for h in range(M_HEADS):
        sl = slice(h * M_HEAD_DIM, (h + 1) * M_HEAD_DIM)
        bcol = b_all[:, h:h + 1]
        licol = log_i[:, h:h + 1]
        mcol = m_rows[:, h:h + 1]
        brow = _row_from_col(bcol, eye)
        lirow = _row_from_col(licol, eye)
        q = q_scr[:, sl]
        ks = k_scr[:, sl] * k_scale
        v = v_scr[:, sl]
        q_b = q.astype(BF16)
        ks_b = ks.astype(BF16)

        log_d = jnp.where(same_causal, bcol - brow + lirow, -jnp.inf)
        log_inter = bcol + mcol
        m_t = jnp.maximum(log_inter, jnp.max(log_d, axis=1, keepdims=True))
        w_intra = jnp.exp(log_d - m_t)
        w_inter = jnp.exp(log_inter - m_t)
        s = lax.dot_general(q_b, ks_b, _NT, preferred_element_type=F32) * w_intra
        num = (jnp.dot(s.astype(BF16), v.astype(BF16), preferred_element_type=F32)
               + w_inter * inter_scr[:, sl])
        den = (jnp.sum(s, axis=1, keepdims=True)
               + w_inter * jnp.sum(q * nrows_ref[:, sl], axis=1, keepdims=True))
        hh = num / jnp.maximum(jnp.abs(den), jnp.exp(-m_t))

        w_src = w_src_all[:, h:h + 1]
        n_ref[:, sl] = decay_all[:, h:h + 1] * nrows_ref[:, sl] + _seg_sum(ks * w_src, G)
        vwt_scr[h] = (v * w_src).T
        ksb_scr[h] = ks_b

        z = proj_ref[:, MIX_WIDTH + h * M_HEAD_DIM:MIX_WIDTH + (h + 1) * M_HEAD_DIM]
        out = _head_output(hh, o_scr[:, sl], xc[:, sl], z, mnw[:, sl], mskip[:, sl])
        mixed_ref[:, sl] = jnp.where(valid, out, 0.0).astype(BF16)

    lane_seg = lax.broadcasted_iota(jnp.int32, (M_HEAD_DIM, R), 1) >> SUBLANE_SHIFT

    def update_body(b, carry):
        own = lane_seg == b
        r0 = pl.multiple_of(b * SUBLANES, SUBLANES)
        for h in range(M_HEADS):
            lhs = jnp.where(own, vwt_scr[h], 0.0).astype(BF16)
            upd = jnp.dot(lhs, ksb_scr[h], preferred_element_type=F32)
            c_ref[b, h] = dec_scr[pl.ds(r0, 1), h:h + 1] * c0_ref[b, h] + upd
        return carry

    lax.fori_loop(0, G, update_body, 0)

    xr = jnp.where(valid, proj_ref[:, M_WIDTH:M_WIDTH + R_WIDTH], rtail0_ref[...])
    xrf_ref[...] = xr
    xcr = _conv_rolled(xr, _vec(vec_ref, "rcw"), _vec(vec_ref, "rcb"))
    a, bb = _rglru_coeffs(xcr, vec_ref, rwa_ref, rwx_ref)
    hs = _tile_scan_real(a, jnp.where(valid, bb, 0.0) + a * h0rows_ref[...])
    h_ref[...] = hs
    zr = proj_ref[:, MIX_WIDTH + M_WIDTH:MIX_WIDTH + M_WIDTH + R_WIDTH]
    mixed_ref[:, M_WIDTH:M_WIDTH + R_WIDTH] = jnp.where(valid, hs * _silu(zr), 0.0).astype(BF16)

    u = proj_ref[:, M_WIDTH + R_WIDTH:MIX_WIDTH]
    _s5_project(u.astype(BF16), wbu_ref, sre_ref, sim_ref)
    p_re = _vec(vec_ref, "abre")
    p_im = _vec(vec_ref, "abim")
    for blk in range(S_LANES // SCAN_LANE_BLOCK):
        sl = slice(blk * SCAN_LANE_BLOCK, (blk + 1) * SCAN_LANE_BLOCK)
        pr = jnp.broadcast_to(p_re[:, sl], (R, SCAN_LANE_BLOCK))
        pi = jnp.broadcast_to(p_im[:, sl], (R, SCAN_LANE_BLOCK))
        s0r = sre0rows_ref[:, sl]
        s0i = sim0rows_ref[:, sl]
        sr, si = _tile_scan_cplx(jnp.where(valid, sre_ref[:, sl], 0.0) + (pr * s0r - pi * s0i),
                                 jnp.where(valid, sim_ref[:, sl], 0.0) + (pr * s0i + pi * s0r), pr, pi)
        sre_ref[:, sl] = sr
        sim_ref[:, sl] = si
    glu = _s5_output(sre_ref, sim_ref, u, vec_ref, wcre_ref, wcim_ref, wglu_ref)
    zs = proj_ref[:, MIX_WIDTH + M_WIDTH + R_WIDTH:2 * MIX_WIDTH]
    mixed_ref[:, M_WIDTH + R_WIDTH:MIX_WIDTH] = jnp.where(valid, glu * _silu(zs), 0.0).astype(BF16)


def _sample_mixer(proj, l, row0, nrows, mixed_prev, c_all, row_state, vec, mats, prev_out):
    G = SAMPLE_GROUP
    R = G * SUBLANES
    blk0 = row0 // R
    stream_map = lambda i: (blk0 + i, 0)
    c_spec = pl.BlockSpec((None, G, M_HEADS, M_HEAD_DIM, M_HEAD_DIM), lambda i: (l, i, 0, 0, 0))

    def rows_spec(width):
        return pl.BlockSpec((None, R, width), lambda i: (l, i, 0))

    alias_args = [mixed_prev] + (list(prev_out) if prev_out is not None else [])
    first_alias = 2 + len(row_state) + 1 + len(mats)
    aliases = {first_alias + k: k for k in range(len(alias_args))}
    out_shape = ([jax.ShapeDtypeStruct(mixed_prev.shape, BF16), jax.ShapeDtypeStruct(c_all.shape, F32)]
                 + [jax.ShapeDtypeStruct((DEPTH, nrows, w), F32) for w in _SAMPLE_STATE_WIDTHS])
    scratch = [pltpu.VMEM((R, M_WIDTH), F32), pltpu.VMEM((R, M_WIDTH), F32),
               pltpu.VMEM((R, M_WIDTH), F32), pltpu.VMEM((R, M_WIDTH), F32), pltpu.VMEM((R, M_WIDTH), F32),
               pltpu.VMEM((M_HEADS, M_HEAD_DIM, R), F32), pltpu.VMEM((M_HEADS, R, M_HEAD_DIM), BF16),
               pltpu.VMEM((R, LANES), F32)]
    return pl.pallas_call(
        functools.partial(_sample_mixer_kernel, groups=G, n_alias=len(alias_args)),
        grid=(nrows // R,),
        in_specs=([pl.BlockSpec((R, 2 * MIX_WIDTH), stream_map), c_spec]
                  + [rows_spec(w) for w in _SAMPLE_STATE_WIDTHS] + [_layer_spec(vec, l)]
                  + [_layer_spec(w, l) for w in mats] + [_ANY_SPEC] * len(alias_args)),
        out_specs=([pl.BlockSpec((R, MIX_WIDTH), stream_map), c_spec]
                   + [rows_spec(w) for w in _SAMPLE_STATE_WIDTHS]),
        out_shape=out_shape,
        scratch_shapes=scratch,
        input_output_aliases=aliases,
        compiler_params=pltpu.CompilerParams(
            dimension_semantics=("parallel",), vmem_limit_bytes=VMEM_LIMIT_BYTES),
        name="sample_mixer",
    )(proj, c_all, *row_state, vec, *mats, *alias_args)


def _prepare_weights(p, s5):
    abre, abim, bbre_t, bbim_t = s5

    def row(v):
        v = v if v.ndim == 3 else v[:, None, :]
        return jnp.pad(v, ((0, 0), (0, 0), (0, _VEC_WIDTH - v.shape[-1])))

    def split_gates(g):
        pad = [(0, 0)] * (g.ndim - 1) + [(0, LANES - M_HEADS)]
        return jnp.concatenate([jnp.pad(g[..., :M_HEADS], pad), jnp.pad(g[..., M_HEADS:], pad)], axis=-1)

    table = {"mcw": p["m_conv_w"], "rcw": p["r_conv_w"], "mcb": p["m_conv_b"], "mnw": p["m_norm_w"],
             "mskip": p["m_skip"], "rcb": p["r_conv_b"], "rba": p["r_ba"], "rbx": p["r_bx"], "rlam": p["r_lam"],
             "sd": p["s_d"], "bglu": p["s_b_glu"], "bif": split_gates(p["m_b_if"]), "abre": abre, "abim": abim,
             "norm_w": p["norm_w"]}
    used = sum(n for _, n, _ in _VEC_LAYOUT.values())
    vec = jnp.concatenate([row(table[k]) for k in _VEC_LAYOUT]
                          + [jnp.zeros((DEPTH, _VEC_ROWS - used, _VEC_WIDTH), F32)], axis=1)

    def block_diag_halves(w):
        nb = R_BLOCKS // 2
        w5 = w.reshape(DEPTH, 2, nb, w.shape[-2], w.shape[-1])
        return jnp.einsum("ab,dhaij->dhaibj", jnp.eye(nb, dtype=F32), w5).reshape(
            DEPTH, 2, nb * w.shape[-2], nb * w.shape[-1]).astype(BF16)

    n_in = S_WIDTH // LANES
    g_in = LANES // S_GROUP
    n_out = 2
    g_out = S_GROUPS // n_out

    def bu_blocks(bb_t):
        bb5 = bb_t.reshape(DEPTH, S_GROUP, n_in, g_in, S_STATE)
        return jnp.einsum("ab,dckbp->dkacbp", jnp.eye(g_in, dtype=F32), bb5).reshape(
            DEPTH, n_in, LANES, g_in * S_STATE)

    def c_blocks(c):
        c5 = c.reshape(DEPTH, n_out, g_out, S_GROUP, S_STATE)
        return jnp.einsum("ab,dmbcp->dmapbc", jnp.eye(g_out, dtype=F32), c5).reshape(
            DEPTH, n_out, g_out * S_STATE, g_out * S_GROUP).astype(BF16)

    mats = {
        "wqk": jnp.concatenate([p["m_wq"], p["m_wk"]], axis=-1).astype(BF16),
        "wvo": jnp.concatenate([p["m_wv"], p["m_wo"]], axis=-1).astype(BF16),
        "wif": split_gates(p["m_w_if"]).astype(BF16),
        "rwa": block_diag_halves(p["r_wa"]), "rwx": block_diag_halves(p["r_wx"]),
        "wbu": jnp.concatenate([bu_blocks(bbre_t), bu_blocks(bbim_t)], axis=-1).astype(BF16),
        "wcre": c_blocks(p["s_c_re"]), "wcim": c_blocks(p["s_c_im"]),
        "wglu": p["s_w_glu"].astype(BF16),
    }
    return vec, p["w_in"].astype(BF16), p["w_out"].astype(BF16), [mats[k] for k in _MATRIX_KEYS]


def _sample_rows_state(n, m, mconv, h, rconv, sre, sim):
    bsz = n.shape[1]

    def flat(x):
        return x.reshape(DEPTH, bsz * SUBLANES, x.shape[-1])

    def on_first_token(x):
        return flat(jnp.pad(x[:, :, None, :], ((0, 0), (0, 0), (SAMPLE_LEAD, SUBLANES - SAMPLE_LEAD - 1), (0, 0))))

    def conv_rows(buf):
        return flat(jnp.pad(buf, ((0, 0), (0, 0), (SAMPLE_LEAD - (CONV_W - 1), SUBLANES - SAMPLE_LEAD), (0, 0))))

    def on_all_rows(x):
        return flat(jnp.broadcast_to(x[:, :, None, :], (DEPTH, bsz, SUBLANES, x.shape[-1])))

    m_pad = jnp.pad(m, ((0, 0), (0, 0), (0, LANES - M_HEADS)))
    return (on_all_rows(n.reshape(DEPTH, bsz, M_WIDTH)), on_all_rows(m_pad),
            conv_rows(mconv), on_first_token(h), conv_rows(rconv),
            on_first_token(sre.reshape(DEPTH, bsz, S_LANES)), on_first_token(sim.reshape(DEPTH, bsz, S_LANES)))


def _sample_state_from_rows(n, m, xmf, h, xrf, sre, sim):
    bsz = n.shape[1] // SUBLANES

    def seg(x):
        return x.reshape(DEPTH, bsz, SUBLANES, x.shape[-1])

    conv_from = SUBLANES - (CONV_W - 1)
    return (seg(n)[:, :, -1].reshape(DEPTH, bsz, M_HEADS, M_HEAD_DIM), seg(m)[:, :, -1, :M_HEADS],
            seg(xmf)[:, :, conv_from:], seg(h)[:, :, -1], seg(xrf)[:, :, conv_from:],
            seg(sre)[:, :, -1].reshape(DEPTH, bsz, S_GROUPS, S_STATE),
            seg(sim)[:, :, -1].reshape(DEPTH, bsz, S_GROUPS, S_STATE))


def _prompt_state_from_kernel(c, n, m, mtail, h, rtail, sre, sim):
    bsz = c.shape[1]
    return (c, n, m[..., 0], mtail[:, :, SUBLANES - 1::SUBLANES], h.reshape(DEPTH, bsz, R_WIDTH),
            rtail[:, :, SUBLANES - 1::SUBLANES],
            sre.reshape(DEPTH, bsz, S_GROUPS, S_STATE), sim.reshape(DEPTH, bsz, S_GROUPS, S_STATE))


def _interleave_chunks(x, inverse=False):
    bsz, seq, d = x.shape
    sub_len = PROMPT_CHUNK // SUBLANES
    inner = (sub_len, SUBLANES) if inverse else (SUBLANES, sub_len)
    return x.reshape(bsz, seq // PROMPT_CHUNK, *inner, d).transpose(0, 1, 3, 2, 4).reshape(bsz, seq, d)


def kernel(x_prompt, x_sample, state_mlstm_C, state_mlstm_n, state_mlstm_m, state_mlstm_conv, state_rglru_h, state_rglru_conv, state_s5_re, state_s5_im, norm_w, w_in, w_out, m_conv_w, m_conv_b, m_wq, m_wk, m_wv, m_wo, m_w_if, m_b_if, m_norm_w, m_skip, r_conv_w, r_conv_b, r_wa, r_ba, r_wx, r_bx, r_lam, s_lam_re, s_lam_im, s_b_re, s_b_im, s_c_re, s_c_im, s_d, s_log_step, s_w_glu, s_b_glu, final_norm_w):
    p = dict(norm_w=norm_w, w_in=w_in, w_out=w_out, m_conv_w=m_conv_w, m_conv_b=m_conv_b, m_wq=m_wq, m_wk=m_wk,
             m_wv=m_wv, m_wo=m_wo, m_w_if=m_w_if, m_b_if=m_b_if, m_norm_w=m_norm_w, m_skip=m_skip,
             r_conv_w=r_conv_w, r_conv_b=r_conv_b, r_wa=r_wa, r_ba=r_ba, r_wx=r_wx, r_bx=r_bx, r_lam=r_lam,
             s_c_re=s_c_re, s_c_im=s_c_im, s_d=s_d, s_w_glu=s_w_glu, s_b_glu=s_b_glu)
    s5 = _s5_prep(s_lam_re, s_lam_im, s_log_step, s_b_re, s_b_im)
    vec, w_in_b, w_out_b, mats = _prepare_weights(p, s5)
    final_w = final_norm_w.reshape(1, D_MODEL)

    bsz, seq, _ = x_prompt.shape
    dec_batch, dec_seq, _ = x_sample.shape
    n_prompt = bsz * seq
    n_sample = dec_batch * SUBLANES
    stream = jnp.concatenate(
        [_interleave_chunks(x_prompt).reshape(n_prompt, D_MODEL),
         jnp.pad(x_sample, ((0, 0), (SAMPLE_LEAD, 0), (0, 0))).reshape(n_sample, D_MODEL)], axis=0)
    rows_state = _sample_rows_state(state_mlstm_n, state_mlstm_m, state_mlstm_conv, state_rglru_h,
                                    state_rglru_conv, state_s5_re, state_s5_im)
    pr_states = None
    sa_states = None
    for l in range(DEPTH):
        proj = _inproj(stream, vec, w_in_b, l)
        mixed, *pr_states = _prompt_mixer(proj, l, bsz, seq, vec, mats, pr_states)
        mixed, *sa_states = _sample_mixer(proj, l, n_prompt, n_sample, mixed, state_mlstm_C, rows_state, vec, mats,
                                          sa_states)
        stream = _outproj(mixed, stream, w_out_b, l, final_w, l == DEPTH - 1)
    y_prompt = _interleave_chunks(stream[:n_prompt].reshape(bsz, seq, D_MODEL), inverse=True)
    y_sample = stream[n_prompt:].reshape(dec_batch, SUBLANES, D_MODEL)[:, SAMPLE_LEAD:]
    return (y_prompt, y_sample, *_prompt_state_from_kernel(*pr_states),
            sa_states[0], *_sample_state_from_rows(*sa_states[1:]))
```

```python
import functools

import jax
import jax.numpy as jnp
from jax import lax
from jax.experimental import pallas as pl
from jax.experimental.pallas import tpu as pltpu

F32 = jnp.float32
BF16 = jnp.bfloat16

D_MODEL = 2048
DEPTH = 2
MIX_WIDTH = D_MODEL
M_WIDTH = MIX_WIDTH // 2
R_WIDTH = MIX_WIDTH // 4
S_WIDTH = MIX_WIDTH - M_WIDTH - R_WIDTH
M_HEADS = 8
M_HEAD_DIM = M_WIDTH // M_HEADS
R_BLOCKS = 8
RG_C = 8.0
S_GROUP = 16
S_GROUPS = S_WIDTH // S_GROUP
S_STATE = 64
S_LANES = S_GROUPS * S_STATE
CONV_W = 4
EPS = 1e-6

SUBLANES = 8
SUBLANE_SHIFT = SUBLANES.bit_length() - 1
LANES = 128
VMEM_LIMIT_BYTES = 56 * 1024 * 1024

PROMPT_CHUNK = 256
SAMPLE_ROWS = 128
SAMPLE_C_SEQS = 8
SAMPLE_LEAD = SUBLANES - 4
INPROJ_TM = 512
INPROJ_TN = 1024
OUTPROJ_TM = 512
SCAN_LANE_BLOCK = 256

_NT = (((1,), (1,)), ((), ()))

_VEC_WIDTH = D_MODEL
_VEC_LAYOUT = {}
_row = 0
for _name, _n, _w in (("mcw", CONV_W, M_WIDTH), ("rcw", CONV_W, R_WIDTH), ("mcb", 1, M_WIDTH), ("mnw", 1, M_WIDTH),
                      ("mskip", 1, M_WIDTH), ("rcb", 1, R_WIDTH), ("rba", 1, R_WIDTH), ("rbx", 1, R_WIDTH),
                      ("rlam", 1, R_WIDTH), ("sd", 1, S_WIDTH), ("bglu", 1, S_WIDTH), ("bif", 1, 2 * LANES),
                      ("abre", 1, S_LANES), ("abim", 1, S_LANES), ("norm_w", 1, D_MODEL)):
    _VEC_LAYOUT[_name] = (_row, _n, _w)
    _row += _n
_VEC_ROWS = -(-_row // SUBLANES) * SUBLANES


def _vec(vec_ref, name):
    r0, n, w = _VEC_LAYOUT[name]
    return vec_ref[r0:r0 + n, 0:w]


def _layer_spec(arr, l):
    nd = arr.ndim - 1
    return pl.BlockSpec((None,) + arr.shape[1:], lambda *_: (l,) + (0,) * nd, pipeline_mode=pl.Buffered(1))


_ANY_SPEC = pl.BlockSpec(memory_space=pl.ANY)


def _s5_prep_kernel(lr_ref, li_ref, ls_ref, brt_ref, bit_ref, abre_ref, abim_ref, bbre_ref, bbim_ref):
    lr = lr_ref[0]
    li = li_ref[0]
    dt = jnp.exp(ls_ref[0])
    mag = jnp.exp(lr * dt)
    ang = li * dt
    ab_re = mag * jnp.cos(ang)
    ab_im = mag * jnp.sin(ang)
    den = lr * lr + li * li
    nr = ab_re - 1.0
    f_re = (nr * lr + ab_im * li) / den
    f_im = (ab_im * lr - nr * li) / den
    br = brt_ref[0]
    bi = bit_ref[0]
    abre_ref[0] = ab_re
    abim_ref[0] = ab_im
    bbre_ref[0] = f_re * br - f_im * bi
    bbim_ref[0] = f_re * bi + f_im * br


def _s5_prep(s_lam_re, s_lam_im, s_log_step, s_b_re, s_b_im):
    lr = s_lam_re.reshape(DEPTH, 1, S_LANES)
    li = s_lam_im.reshape(DEPTH, 1, S_LANES)
    ls = jnp.repeat(s_log_step, S_STATE, axis=-1).reshape(DEPTH, 1, S_LANES)
    brt = s_b_re.reshape(DEPTH, S_LANES, S_GROUP).transpose(0, 2, 1)
    bit = s_b_im.reshape(DEPTH, S_LANES, S_GROUP).transpose(0, 2, 1)
    vec = pl.BlockSpec((1, 1, S_LANES), lambda l: (l, 0, 0))
    mat = pl.BlockSpec((1, S_GROUP, S_LANES), lambda l: (l, 0, 0))
    return pl.pallas_call(
        _s5_prep_kernel,
        grid=(DEPTH,),
        in_specs=[vec, vec, vec, mat, mat],
        out_specs=[vec, vec, mat, mat],
        out_shape=[jax.ShapeDtypeStruct((DEPTH, 1, S_LANES), F32),
                   jax.ShapeDtypeStruct((DEPTH, 1, S_LANES), F32),
                   jax.ShapeDtypeStruct((DEPTH, S_GROUP, S_LANES), F32),
                   jax.ShapeDtypeStruct((DEPTH, S_GROUP, S_LANES), F32)],
        name="s5_prep",
    )(lr, li, ls, brt, bit)


def _inproj_kernel(x_ref, vec_ref, w_ref, o_ref):
    x = x_ref[...]
    ms = jnp.mean(x * x, axis=-1, keepdims=True)
    xn = (x * lax.rsqrt(ms + EPS) * _vec(vec_ref, "norm_w")).astype(BF16)
    for j in range((2 * MIX_WIDTH) // INPROJ_TN):
        cols = slice(j * INPROJ_TN, (j + 1) * INPROJ_TN)
        o_ref[:, cols] = jnp.dot(xn, w_ref[:, cols], preferred_element_type=F32)


def _inproj(x, vec, w_in_b, l):
    n = x.shape[0]
    return pl.pallas_call(
        _inproj_kernel,
        grid=(n // INPROJ_TM,),
        in_specs=[pl.BlockSpec((INPROJ_TM, D_MODEL), lambda i: (i, 0)), _layer_spec(vec, l), _layer_spec(w_in_b, l)],
        out_specs=pl.BlockSpec((INPROJ_TM, 2 * MIX_WIDTH), lambda i: (i, 0)),
        out_shape=jax.ShapeDtypeStruct((n, 2 * MIX_WIDTH), F32),
        compiler_params=pltpu.CompilerParams(
            dimension_semantics=("parallel",), vmem_limit_bytes=VMEM_LIMIT_BYTES),
        name="inproj",
    )(x, vec, w_in_b)


def _outproj_kernel(mixed_ref, x_ref, w_ref, fw_ref, o_ref, *, final):
    y = x_ref[...] + jnp.dot(mixed_ref[...], w_ref[...], preferred_element_type=F32)
    if final:
        ms = jnp.mean(y * y, axis=-1, keepdims=True)
        y = y * lax.rsqrt(ms + EPS) * fw_ref[...]
    o_ref[...] = y


def _outproj(mixed, x, w_out_b, l, final_w, final):
    n = x.shape[0]
    return pl.pallas_call(
        functools.partial(_outproj_kernel, final=final),
        grid=(n // OUTPROJ_TM,),
        in_specs=[pl.BlockSpec((OUTPROJ_TM, MIX_WIDTH), lambda i: (i, 0)),
                  pl.BlockSpec((OUTPROJ_TM, D_MODEL), lambda i: (i, 0)),
                  _layer_spec(w_out_b, l),
                  pl.BlockSpec((1, D_MODEL), lambda i: (0, 0))],
        out_specs=pl.BlockSpec((OUTPROJ_TM, D_MODEL), lambda i: (i, 0)),
        out_shape=jax.ShapeDtypeStruct((n, D_MODEL), F32),
        compiler_params=pltpu.CompilerParams(
            dimension_semantics=("parallel",), vmem_limit_bytes=VMEM_LIMIT_BYTES),
        name="outproj",
    )(mixed, x, w_out_b, final_w)


def _silu(x):
    return x * jax.nn.sigmoid(x)


def _log_sigmoid(x):
    return jnp.minimum(x, 0.0) - jnp.log1p(jnp.exp(-jnp.abs(x)))


def _softplus(x):
    return jnp.maximum(x, 0.0) + jnp.log1p(jnp.exp(-jnp.abs(x)))


def _sublane_pos(shape):
    return lax.broadcasted_iota(jnp.int32, shape, 0) & (SUBLANES - 1)


def _tile_cumsum(x):
    pos = _sublane_pos(x.shape)
    s = 1
    while s < SUBLANES:
        x = x + jnp.where(pos >= s, pltpu.roll(x, s, 0), 0.0)
        s *= 2
    return x


def _tile_scan_real(a, b):
    pos = _sublane_pos(a.shape)
    s = 1
    while s < SUBLANES:
        m = pos >= s
        b = jnp.where(m, a * pltpu.roll(b, s, 0) + b, b)
        if 2 * s < SUBLANES:
            a = jnp.where(m, a * pltpu.roll(a, s, 0), a)
        s *= 2
    return b


def _tile_scan_cplx(sr, si, pr, pi):
    pos = _sublane_pos(sr.shape)
    s = 1
    while s < SUBLANES:
        m = pos >= s
        sr_sh = pltpu.roll(sr, s, 0)
        si_sh = pltpu.roll(si, s, 0)
        sr, si = (jnp.where(m, sr + (pr * sr_sh - pi * si_sh), sr),
                  jnp.where(m, si + (pr * si_sh + pi * sr_sh), si))
        if 2 * s < SUBLANES:
            pr, pi = pr * pr - pi * pi, 2.0 * (pr * pi)
        s *= 2
    return sr, si


def _row_from_col(col, eye):
    return jnp.sum(jnp.where(eye, col, 0.0), axis=0, keepdims=True)


def _mlstm_project(xc, xm, vec_ref, wqk_ref, wvo_ref, wif_ref, q_scr, k_scr, v_scr, o_scr):
    xc_b = xc.astype(BF16)
    xm_b = xm.astype(BF16)
    for h in range(M_HEADS):
        sl = slice(h * M_HEAD_DIM, (h + 1) * M_HEAD_DIM)
        qk = jnp.dot(xc_b[:, sl], wqk_ref[h], preferred_element_type=F32)
        vo = jnp.dot(xm_b[:, sl], wvo_ref[h], preferred_element_type=F32)
        q_scr[:, sl] = qk[:, :M_HEAD_DIM]
        k_scr[:, sl] = qk[:, M_HEAD_DIM:]
        v_scr[:, sl] = vo[:, :M_HEAD_DIM]
        o_scr[:, sl] = vo[:, M_HEAD_DIM:]
    return (jnp.dot(q_scr[...].astype(BF16), wif_ref[0:M_WIDTH, :], preferred_element_type=F32)
            + jnp.dot(k_scr[...].astype(BF16), wif_ref[M_WIDTH:2 * M_WIDTH, :], preferred_element_type=F32)
            + jnp.dot(v_scr[...].astype(BF16), wif_ref[2 * M_WIDTH:3 * M_WIDTH, :], preferred_element_type=F32)
            + _vec(vec_ref, "bif"))


def _head_output(hh, o_pre, xc_h, z_h, mnw_h, mskip_h):
    mu = jnp.mean(hh, axis=1, keepdims=True)
    hc = hh - mu
    var = jnp.mean(hc * hc, axis=1, keepdims=True)
    hn = hc * lax.rsqrt(var + EPS) * mnw_h
    return (jax.nn.sigmoid(o_pre) * hn + mskip_h * xc_h) * _silu(z_h)


def _rglru_coeffs(xcr, vec_ref, rwa_ref, rwx_ref):
    xcr_b = xcr.astype(BF16)
    half = R_WIDTH // 2
    ra_pre = jnp.concatenate(
        [jnp.dot(xcr_b[:, :half], rwa_ref[0], preferred_element_type=F32),
         jnp.dot(xcr_b[:, half:], rwa_ref[1], preferred_element_type=F32)], axis=1) + _vec(vec_ref, "rba")
    rx_pre = jnp.concatenate(
        [jnp.dot(xcr_b[:, :half], rwx_ref[0], preferred_element_type=F32),
         jnp.dot(xcr_b[:, half:], rwx_ref[1], preferred_element_type=F32)], axis=1) + _vec(vec_ref, "rbx")
    log_a = (-RG_C) * jax.nn.sigmoid(ra_pre) * _softplus(-_vec(vec_ref, "rlam"))
    a = jnp.exp(log_a)
    th = jnp.tanh(log_a)
    one_minus_a2 = (-2.0 * th) / (1.0 - th)
    return a, jnp.sqrt(one_minus_a2) * (jax.nn.sigmoid(rx_pre) * xcr)


def _s5_project(u_b, wbu_ref, re_ref, im_ref):
    per_slice = (LANES // S_GROUP) * S_STATE
    for k in range(S_WIDTH // LANES):
        res = jnp.dot(u_b[:, k * LANES:(k + 1) * LANES], wbu_ref[k], preferred_element_type=F32)
        re_ref[:, k * per_slice:(k + 1) * per_slice] = res[:, :per_slice]
        im_ref[:, k * per_slice:(k + 1) * per_slice] = res[:, per_slice:]


def _s5_output(re_ref, im_ref, u, vec_ref, wcre_ref, wcim_ref, wglu_ref):
    nblk = wcre_ref.shape[0]
    k_blk = S_LANES // nblk
    parts = []
    for m in range(nblk):
        ks = slice(m * k_blk, (m + 1) * k_blk)
        parts.append(jnp.dot(re_ref[:, ks].astype(BF16), wcre_ref[m], preferred_element_type=F32)
                     - jnp.dot(im_ref[:, ks].astype(BF16), wcim_ref[m], preferred_element_type=F32))
    y = jnp.concatenate(parts, axis=1) + _vec(vec_ref, "sd") * u
    g = jax.nn.gelu(y)
    return g * jax.nn.sigmoid(jnp.dot(g.astype(BF16), wglu_ref[...], preferred_element_type=F32)
                              + _vec(vec_ref, "bglu"))


_MATRIX_KEYS = ("wqk", "wvo", "wif", "rwa", "rwx", "wbu", "wcre", "wcim", "wglu")


def _tile(x, i):
    return x[i * SUBLANES:(i + 1) * SUBLANES]


def _conv_interleaved(x, tail_ref, w, bias, rows):
    ntiles = rows // SUBLANES
    pos = lax.broadcasted_iota(jnp.int32, (SUBLANES, x.shape[1]), 0)
    prev = tail_ref[...]
    before = [pltpu.roll(jnp.where(pos == SUBLANES - 1, _tile(prev, CONV_W - 1 - d), _tile(x, ntiles - d)), 1, 0)
              for d in range(1, CONV_W)]
    out = w[CONV_W - 1:CONV_W, :] * x + bias
    for j in range(1, CONV_W):
        shifted = jnp.concatenate(before[:j][::-1] + [x[:rows - j * SUBLANES]], axis=0)
        out = out + w[CONV_W - 1 - j:CONV_W - j, :] * shifted
    tail_ref[...] = x[rows - (CONV_W - 1) * SUBLANES:]
    return out


def _cumsum_interleaved(x, rows):
    tiles = [_tile(x, 0)]
    for i in range(1, rows // SUBLANES):
        tiles.append(tiles[-1] + _tile(x, i))
    total = tiles[-1]
    start = _tile_cumsum(total) - total
    return jnp.concatenate([t + start for t in tiles], axis=0)


def _scan_real_interleaved(a_ref, b_ref, carry, rows):
    ntiles = rows // SUBLANES
    pos = lax.broadcasted_iota(jnp.int32, (SUBLANES, a_ref.shape[1]), 0)
    h = _tile(b_ref, 0)
    aprod = _tile(a_ref, 0)
    for i in range(1, ntiles):
        a = _tile(a_ref, i)
        h = a * h + _tile(b_ref, i)
        aprod = aprod * a
    g = _tile_scan_real(aprod, h + jnp.where(pos == 0, aprod * carry, 0.0))
    h = jnp.where(pos == 0, carry, pltpu.roll(g, 1, 0))
    for i in range(ntiles):
        h = _tile(a_ref, i) * h + _tile(b_ref, i)
        b_ref[i * SUBLANES:(i + 1) * SUBLANES, :] = h
    return g[SUBLANES - 1:SUBLANES]


def _scan_cplx_interleaved(re_ref, im_ref, p_re, p_im, cre_ref, cim_ref, rows):
    ntiles = rows // SUBLANES
    width = re_ref.shape[1]
    pos = lax.broadcasted_iota(jnp.int32, (SUBLANES, SCAN_LANE_BLOCK), 0)
    for blk in range(width // SCAN_LANE_BLOCK):
        sl = slice(blk * SCAN_LANE_BLOCK, (blk + 1) * SCAN_LANE_BLOCK)
        pr = jnp.broadcast_to(p_re[:, sl], pos.shape)
        pi = jnp.broadcast_to(p_im[:, sl], pos.shape)
        c_r = cre_ref[:, sl]
        c_i = cim_ref[:, sl]
        sr = re_ref[0:SUBLANES, sl]
        si = im_ref[0:SUBLANES, sl]
        for i in range(1, ntiles):
            rs = slice(i * SUBLANES, (i + 1) * SUBLANES)
            sr, si = pr * sr - pi * si + re_ref[rs, sl], pr * si + pi * sr + im_ref[rs, sl]
        qr, qi = pr, pi
        n = 1
        while n < ntiles:
            qr, qi = qr * qr - qi * qi, 2.0 * (qr * qi)
            n *= 2
        gr, gi = _tile_scan_cplx(sr + jnp.where(pos == 0, qr * c_r - qi * c_i, 0.0),
                                 si + jnp.where(pos == 0, qr * c_i + qi * c_r, 0.0), qr, qi)
        sr = jnp.where(pos == 0, c_r, pltpu.roll(gr, 1, 0))
        si = jnp.where(pos == 0, c_i, pltpu.roll(gi, 1, 0))
        for i in range(ntiles):
            rs = slice(i * SUBLANES, (i + 1) * SUBLANES)
            sr, si = pr * sr - pi * si + re_ref[rs, sl], pr * si + pi * sr + im_ref[rs, sl]
            re_ref[rs, sl] = sr
            im_ref[rs, sl] = si
        cre_ref[:, sl] = gr[SUBLANES - 1:SUBLANES]
        cim_ref[:, sl] = gi[SUBLANES - 1:SUBLANES]


def _prompt_mixer_kernel(*refs, rows, n_alias):
    (proj_ref, vec_ref, wqk_ref, wvo_ref, wif_ref, rwa_ref, rwx_ref, wbu_ref, wcre_ref, wcim_ref, wglu_ref) = refs[:11]
    (mixed_ref, c_ref, n_ref, m_ref, mtail_ref, h_ref, rtail_ref, sre_ref, sim_ref,
     q_scr, k_scr, v_scr, o_scr, ra_scr, rb_scr, ure_scr, uim_scr) = refs[11 + n_alias:]
    T = rows

    @pl.when(pl.program_id(1) == 0)
    def _():
        for ref in (c_ref, n_ref, m_ref, mtail_ref, h_ref, rtail_ref, sre_ref, sim_ref):
            ref[...] = jnp.zeros(ref.shape, F32)

    xm = proj_ref[:, 0:M_WIDTH]
    xc = _silu(_conv_interleaved(xm, mtail_ref, _vec(vec_ref, "mcw"), _vec(vec_ref, "mcb"), T))
    gates = _mlstm_project(xc, xm, vec_ref, wqk_ref, wvo_ref, wif_ref, q_scr, k_scr, v_scr, o_scr)
    log_i = gates[:, :LANES]
    b_all = _cumsum_interleaved(_log_sigmoid(gates[:, LANES:]), T)

    rowi = lax.broadcasted_iota(jnp.int32, (T, T), 0)
    coli = lax.broadcasted_iota(jnp.int32, (T, T), 1)
    sub_len = T // SUBLANES

    def time_of(r):
        return (r & (SUBLANES - 1)) * sub_len + (r >> SUBLANE_SHIFT)

    causal = time_of(coli) <= time_of(rowi)
    eye = coli == rowi
    k_scale = M_HEAD_DIM ** -0.5
    mnw = _vec(vec_ref, "mnw")
    mskip = _vec(vec_ref, "mskip")

    for h in range(M_HEADS):
        sl = slice(h * M_HEAD_DIM, (h + 1) * M_HEAD_DIM)
        bcol = b_all[:, h:h + 1]
        licol = log_i[:, h:h + 1]
        brow = _row_from_col(bcol, eye)
        lirow = _row_from_col(licol, eye)
        m_prev = m_ref[h:h + 1, 0:1]
        c_prev = c_ref[h]
        n_prev = n_ref[h:h + 1, :]

        q = q_scr[:, sl]
        ks = k_scr[:, sl] * k_scale
        v = v_scr[:, sl]
        q_b = q.astype(BF16)
        ks_b = ks.astype(BF16)

        log_d = jnp.where(causal, bcol - brow + lirow, -jnp.inf)
        log_inter = bcol + m_prev
        m_t = jnp.maximum(log_inter, jnp.max(log_d, axis=1, keepdims=True))
        w_intra = jnp.exp(log_d - m_t)
        w_inter = jnp.exp(log_inter - m_t)
        s = lax.dot_general(q_b, ks_b, _NT, preferred_element_type=F32) * w_intra
        inter = lax.dot_general(q_b, c_prev.astype(BF16), _NT, preferred_element_type=F32)
        num = jnp.dot(s.astype(BF16), v.astype(BF16), preferred_element_type=F32) + w_inter * inter
        den = (jnp.sum(s, axis=1, keepdims=True)
               + w_inter * jnp.sum(q * n_prev, axis=1, keepdims=True))
        hh = num / jnp.maximum(jnp.abs(den), jnp.exp(-m_t))

        b_end = bcol[T - 1:T, :]
        m_new = jnp.maximum(b_end + m_prev, jnp.max(b_end - brow + lirow, axis=1, keepdims=True))
        w_src = jnp.exp(b_end - bcol + licol - m_new)
        decay = jnp.exp(b_end + m_prev - m_new)
        vw_t = (v * w_src).T.astype(BF16)
        c_ref[h] = decay * c_prev + jnp.dot(vw_t, ks_b, preferred_element_type=F32)
        n_ref[h:h + 1, :] = decay * n_prev + jnp.sum(ks * w_src, axis=0, keepdims=True)
        m_ref[h:h + 1, :] = jnp.broadcast_to(m_new, (1, LANES))

        z = proj_ref[:, MIX_WIDTH + h * M_HEAD_DIM:MIX_WIDTH + (h + 1) * M_HEAD_DIM]
        mixed_ref[:, sl] = _head_output(hh, o_scr[:, sl], xc[:, sl], z, mnw[:, sl], mskip[:, sl]).astype(BF16)

    xr = proj_ref[:, M_WIDTH:M_WIDTH + R_WIDTH]
    xcr = _conv_interleaved(xr, rtail_ref, _vec(vec_ref, "rcw"), _vec(vec_ref, "rcb"), T)
    a, bb = _rglru_coeffs(xcr, vec_ref, rwa_ref, rwx_ref)
    ra_scr[...] = a
    rb_scr[...] = bb
    h_ref[...] = _scan_real_interleaved(ra_scr, rb_scr, h_ref[...], T)
    zr = proj_ref[:, MIX_WIDTH + M_WIDTH:MIX_WIDTH + M_WIDTH + R_WIDTH]
    mixed_ref[:, M_WIDTH:M_WIDTH + R_WIDTH] = (rb_scr[...] * _silu(zr)).astype(BF16)

    u = proj_ref[:, M_WIDTH + R_WIDTH:MIX_WIDTH]
    _s5_project(u.astype(BF16), wbu_ref, ure_scr, uim_scr)
    _scan_cplx_interleaved(ure_scr, uim_scr, _vec(vec_ref, "abre"), _vec(vec_ref, "abim"), sre_ref, sim_ref, T)
    glu = _s5_output(ure_scr, uim_scr, u, vec_ref, wcre_ref, wcim_ref, wglu_ref)
    zs = proj_ref[:, MIX_WIDTH + M_WIDTH + R_WIDTH:2 * MIX_WIDTH]
    mixed_ref[:, M_WIDTH + R_WIDTH:MIX_WIDTH] = (glu * _silu(zs)).astype(BF16)


_PROMPT_STATE_SHAPES = ((M_HEADS, M_HEAD_DIM, M_HEAD_DIM), (M_HEADS, M_HEAD_DIM), (M_HEADS, LANES),
                        ((CONV_W - 1) * SUBLANES, M_WIDTH), (1, R_WIDTH), ((CONV_W - 1) * SUBLANES, R_WIDTH),
                        (1, S_LANES), (1, S_LANES))


def _prompt_mixer(proj, l, bsz, seq, vec, mats, prev_states):
    rows = PROMPT_CHUNK
    nchunk = seq // rows
    aliases = {}
    alias_args = []
    if prev_states is not None:
        alias_args = list(prev_states)
        aliases = {2 + len(mats) + k: 1 + k for k in range(len(alias_args))}

    def state_spec(shape):
        nd = len(shape)
        return pl.BlockSpec((None, None) + shape, lambda b, c: (l, b) + (0,) * nd)

    rows_map = lambda b, c: (b * nchunk + c, 0)
    out_shape = ([jax.ShapeDtypeStruct((proj.shape[0], MIX_WIDTH), BF16)]
                 + [jax.ShapeDtypeStruct((DEPTH, bsz) + s, F32) for s in _PROMPT_STATE_SHAPES])
    scratch = [pltpu.VMEM((rows, M_WIDTH), F32), pltpu.VMEM((rows, M_WIDTH), F32),
               pltpu.VMEM((rows, M_WIDTH), F32), pltpu.VMEM((rows, M_WIDTH), F32),
               pltpu.VMEM((rows, R_WIDTH), F32), pltpu.VMEM((rows, R_WIDTH), F32),
               pltpu.VMEM((rows, S_LANES), F32), pltpu.VMEM((rows, S_LANES), F32)]
    return pl.pallas_call(
        functools.partial(_prompt_mixer_kernel, rows=rows, n_alias=len(alias_args)),
        grid=(bsz, nchunk),
        in_specs=([pl.BlockSpec((rows, 2 * MIX_WIDTH), rows_map), _layer_spec(vec, l)]
                  + [_layer_spec(w, l) for w in mats] + [_ANY_SPEC] * len(alias_args)),
        out_specs=[pl.BlockSpec((rows, MIX_WIDTH), rows_map)] + [state_spec(s) for s in _PROMPT_STATE_SHAPES],
        out_shape=out_shape,
        scratch_shapes=scratch,
        input_output_aliases=aliases,
        compiler_params=pltpu.CompilerParams(
            dimension_semantics=("parallel", "arbitrary"), vmem_limit_bytes=VMEM_LIMIT_BYTES),
        name="prompt_mixer",
    )(proj, vec, *mats, *alias_args)


def _seg_last(x, groups):
    x3 = x.reshape(groups, SUBLANES, x.shape[-1])
    return jnp.broadcast_to(x3[:, SUBLANES - 1:SUBLANES, :], x3.shape).reshape(x.shape)


def _seg_max(x, groups):
    x3 = x.reshape(groups, SUBLANES, x.shape[-1])
    return jnp.broadcast_to(jnp.max(x3, axis=1, keepdims=True), x3.shape).reshape(x.shape)


def _seg_sum(x, groups):
    x3 = x.reshape(groups, SUBLANES, x.shape[-1])
    return jnp.broadcast_to(jnp.sum(x3, axis=1, keepdims=True), x3.shape).reshape(x.shape)


def _conv_rolled(xf, w, bias):
    out = w[CONV_W - 1:CONV_W, :] * xf + bias
    for j in range(1, CONV_W):
        out = out + w[CONV_W - 1 - j:CONV_W - j, :] * pltpu.roll(xf, j, 0)
    return out


_SAMPLE_STATE_WIDTHS = (M_WIDTH, LANES, M_WIDTH, R_WIDTH, R_WIDTH, S_LANES, S_LANES)


def _sample_mixer_kernel(*refs, rows, c_seqs, n_alias):
    (proj_ref, c0_ref, nrows_ref, mrows_ref, mtail0_ref, h0rows_ref, rtail0_ref, sre0rows_ref, sim0rows_ref,
     vec_ref, wqk_ref, wvo_ref, wif_ref, rwa_ref, rwx_ref, wbu_ref, wcre_ref, wcim_ref, wglu_ref) = refs[:19]
    (mixed_ref, c_ref, n_ref, m_ref, xmf_ref, h_ref, xrf_ref, sre_ref, sim_ref,
     q_scr, k_scr, v_scr, o_scr, xc_scr, intert_scr, vwt_scr, qt_scr, ksb_scr, dec_scr, b_scr, li_scr) = refs[19 + n_alias:]
    R = rows
    G = R // SUBLANES
    sub = pl.program_id(1)
    k_scale = M_HEAD_DIM ** -0.5
    valid = _sublane_pos((R, 1)) >= SAMPLE_LEAD

    @pl.when(sub == 0)
    def _():
        xm = jnp.where(valid, proj_ref[:, 0:M_WIDTH], mtail0_ref[...])
        xmf_ref[...] = xm
        xc = _silu(_conv_rolled(xm, _vec(vec_ref, "mcw"), _vec(vec_ref, "mcb")))
        xc_scr[...] = xc
        gates = _mlstm_project(xc, xm, vec_ref, wqk_ref, wvo_ref, wif_ref, q_scr, k_scr, v_scr, o_scr)
        log_i = jnp.where(valid, gates[:, :LANES], -jnp.inf)
        log_f = jnp.where(valid, _log_sigmoid(gates[:, LANES:]), 0.0)
        b_all = _tile_cumsum(log_f)
        b_end = _seg_last(b_all, G)
        m_rows = mrows_ref[...]
        log_src = b_end - b_all + log_i
        m_new = jnp.maximum(b_end + m_rows, _seg_max(log_src, G))
        w_src_all = jnp.exp(log_src - m_new)
        decay_all = jnp.exp(b_end + m_rows - m_new)
        dec_scr[...] = decay_all
        b_scr[...] = b_all
        li_scr[...] = log_i
        m_ref[...] = m_new
        for h in range(M_HEADS):
            sl = slice(h * M_HEAD_DIM, (h + 1) * M_HEAD_DIM)
            ks = k_scr[:, sl] * k_scale
            w_src = w_src_all[:, h:h + 1]
            n_ref[:, sl] = decay_all[:, h:h + 1] * nrows_ref[:, sl] + _seg_sum(ks * w_src, G)
            vwt_scr[h] = (v_scr[:, sl] * w_src).T
            qt_scr[h] = q_scr[:, sl].T.astype(BF16)
            ksb_scr[h] = ks.astype(BF16)
            intert_scr[h] = jnp.zeros((M_HEAD_DIM, R), F32)

        xr = jnp.where(valid, proj_ref[:, M_WIDTH:M_WIDTH + R_WIDTH], rtail0_ref[...])
        xrf_ref[...] = xr
        xcr = _conv_rolled(xr, _vec(vec_ref, "rcw"), _vec(vec_ref, "rcb"))
        a, bb = _rglru_coeffs(xcr, vec_ref, rwa_ref, rwx_ref)
        hs = _tile_scan_real(a, jnp.where(valid, bb, 0.0) + a * h0rows_ref[...])
        h_ref[...] = hs
        zr = proj_ref[:, MIX_WIDTH + M_WIDTH:MIX_WIDTH + M_WIDTH + R_WIDTH]
        mixed_ref[:, M_WIDTH:M_WIDTH + R_WIDTH] = jnp.where(valid, hs * _silu(zr), 0.0).astype(BF16)

        u = proj_ref[:, M_WIDTH + R_WIDTH:MIX_WIDTH]
        _s5_project(u.astype(BF16), wbu_ref, sre_ref, sim_ref)
        p_re = _vec(vec_ref, "abre")
        p_im = _vec(vec_ref, "abim")
        for blk in range(S_LANES // SCAN_LANE_BLOCK):
            sl = slice(blk * SCAN_LANE_BLOCK, (blk + 1) * SCAN_LANE_BLOCK)
            pr = jnp.broadcast_to(p_re[:, sl], (R, SCAN_LANE_BLOCK))
            pi = jnp.broadcast_to(p_im[:, sl], (R, SCAN_LANE_BLOCK))
            s0r = sre0rows_ref[:, sl]
            s0i = sim0rows_ref[:, sl]
            sr, si = _tile_scan_cplx(jnp.where(valid, sre_ref[:, sl], 0.0) + (pr * s0r - pi * s0i),
                                     jnp.where(valid, sim_ref[:, sl], 0.0) + (pr * s0i + pi * s0r), pr, pi)
            sre_ref[:, sl] = sr
            sim_ref[:, sl] = si
        glu = _s5_output(sre_ref, sim_ref, u, vec_ref, wcre_ref, wcim_ref, wglu_ref)
        zs = proj_ref[:, MIX_WIDTH + M_WIDTH + R_WIDTH:2 * MIX_WIDTH]
        mixed_ref[:, M_WIDTH + R_WIDTH:MIX_WIDTH] = jnp.where(valid, glu * _silu(zs), 0.0).astype(BF16)

    lane_seq = lax.broadcasted_iota(jnp.int32, (M_HEAD_DIM, R), 1) >> SUBLANE_SHIFT
    seq0 = sub * c_seqs
    for h in range(M_HEADS):
        c_old = c0_ref[:, h].reshape(c_seqs * M_HEAD_DIM, M_HEAD_DIM)
        readout = jnp.dot(c_old.astype(BF16), qt_scr[h], preferred_element_type=F32)
        acc = intert_scr[h]
        vwt = vwt_scr[h]
        lhs = []
        for s in range(c_seqs):
            own = lane_seq == seq0 + s
            acc = jnp.where(own, readout[s * M_HEAD_DIM:(s + 1) * M_HEAD_DIM], acc)
            lhs.append(jnp.where(own, vwt, 0.0))
        intert_scr[h] = acc
        upd = jnp.dot(jnp.concatenate(lhs, axis=0).astype(BF16), ksb_scr[h], preferred_element_type=F32)
        for s in range(c_seqs):
            r0 = pl.multiple_of((seq0 + s) * SUBLANES, SUBLANES)
            c_ref[s, h] = (dec_scr[pl.ds(r0, 1), h:h + 1] * c0_ref[s, h]
                           + upd[s * M_HEAD_DIM:(s + 1) * M_HEAD_DIM])

    @pl.when(sub == pl.num_programs(1) - 1)
    def _():
        _sample_heads(proj_ref, mrows_ref, nrows_ref, vec_ref, mixed_ref,
                      q_scr, k_scr, v_scr, o_scr, xc_scr, intert_scr, b_scr, li_scr, valid, R)


def _sample_heads(proj_ref, mrows_ref, nrows_ref, vec_ref, mixed_ref,
                  q_scr, k_scr, v_scr, o_scr, xc_scr, intert_scr, b_scr, li_scr, valid, R):
    rowi = lax.broadcasted_iota(jnp.int32, (R, R), 0)
    coli = lax.broadcasted_iota(jnp.int32, (R, R), 1)
    eye = coli == rowi
    same_causal = jnp.logical_and(coli <= rowi, (coli >> SUBLANE_SHIFT) == (rowi >> SUBLANE_SHIFT))
    k_scale = M_HEAD_DIM ** -0.5
    mnw = _vec(vec_ref, "mnw")
    mskip = _vec(vec_ref, "mskip")
    b_all = b_scr[...]
    log_i = li_scr[...]
    m_rows = mrows_ref[...]

    for h in range(M_HEADS):
        sl = slice(h * M_HEAD_DIM, (h + 1) * M_HEAD_DIM)
        bcol = b_all[:, h:h + 1]
        licol = log_i[:, h:h + 1]
        mcol = m_rows[:, h:h + 1]
        brow = _row_from_col(bcol, eye)
        lirow = _row_from_col(licol, eye)
        q = q_scr[:, sl]
        ks = k_scr[:, sl] * k_scale
        v = v_scr[:, sl]
        q_b = q.astype(BF16)
        ks_b = ks.astype(BF16)

        log_d = jnp.where(same_causal, bcol - brow + lirow, -jnp.inf)
        log_inter = bcol + mcol
        m_t = jnp.maximum(log_inter, jnp.max(log_d, axis=1, keepdims=True))
        w_intra = jnp.exp(log_d - m_t)
        w_inter = jnp.exp(log_inter - m_t)
        s = lax.dot_general(q_b, ks_b, _NT, preferred_element_type=F32) * w_intra
        num = (jnp.dot(s.astype(BF16), v.astype(BF16), preferred_element_type=F32)
               + w_inter * intert_scr[h].T)
        den = (jnp.sum(s, axis=1, keepdims=True)
               + w_inter * jnp.sum(q * nrows_ref[:, sl], axis=1, keepdims=True))
        hh = num / jnp.maximum(jnp.abs(den), jnp.exp(-m_t))

        z = proj_ref[:, MIX_WIDTH + h * M_HEAD_DIM:MIX_WIDTH + (h + 1) * M_HEAD_DIM]
        out = _head_output(hh, o_scr[:, sl], xc_scr[:, sl], z, mnw[:, sl], mskip[:, sl])
        mixed_ref[:, sl] = jnp.where(valid, out, 0.0).astype(BF16)


def _sample_mixer(proj, l, row0, nrows, mixed_prev, c_all, row_state, vec, mats, prev_out):
    R = SAMPLE_ROWS
    nsub = (R // SUBLANES) // SAMPLE_C_SEQS
    blk0 = row0 // R
    stream_map = lambda i, j: (blk0 + i, 0)
    c_spec = pl.BlockSpec((None, SAMPLE_C_SEQS, M_HEADS, M_HEAD_DIM, M_HEAD_DIM),
                          lambda i, j: (l, i * nsub + j, 0, 0, 0))

    def rows_spec(width):
        return pl.BlockSpec((None, R, width), lambda i, j: (l, i, 0))

    alias_args = [mixed_prev] + (list(prev_out) if prev_out is not None else [])
    first_alias = 2 + len(row_state) + 1 + len(mats)
    aliases = {first_alias + k: k for k in range(len(alias_args))}
    out_shape = ([jax.ShapeDtypeStruct(mixed_prev.shape, BF16), jax.ShapeDtypeStruct(c_all.shape, F32)]
                 + [jax.ShapeDtypeStruct((DEPTH, nrows, w), F32) for w in _SAMPLE_STATE_WIDTHS])
    scratch = ([pltpu.VMEM((R, M_WIDTH), F32)] * 5
               + [pltpu.VMEM((M_HEADS, M_HEAD_DIM, R), F32), pltpu.VMEM((M_HEADS, M_HEAD_DIM, R), F32),
                  pltpu.VMEM((M_HEADS, M_HEAD_DIM, R), BF16), pltpu.VMEM((M_HEADS, R, M_HEAD_DIM), BF16)]
               + [pltpu.VMEM((R, LANES), F32)] * 3)
    return pl.pallas_call(
        functools.partial(_sample_mixer_kernel, rows=R, c_seqs=SAMPLE_C_SEQS, n_alias=len(alias_args)),
        grid=(nrows // R, nsub),
        in_specs=([pl.BlockSpec((R, 2 * MIX_WIDTH), stream_map), c_spec]
                  + [rows_spec(w) for w in _SAMPLE_STATE_WIDTHS] + [_layer_spec(vec, l)]
                  + [_layer_spec(w, l) for w in mats] + [_ANY_SPEC] * len(alias_args)),
        out_specs=([pl.BlockSpec((R, MIX_WIDTH), stream_map), c_spec]
                   + [rows_spec(w) for w in _SAMPLE_STATE_WIDTHS]),
        out_shape=out_shape,
        scratch_shapes=scratch,
        input_output_aliases=aliases,
        compiler_params=pltpu.CompilerParams(
            dimension_semantics=("parallel", "arbitrary"), vmem_limit_bytes=VMEM_LIMIT_BYTES),
        name="sample_mixer",
    )(proj, c_all, *row_state, vec, *mats, *alias_args)


def _prepare_weights(p, s5):
    abre, abim, bbre_t, bbim_t = s5

    def row(v):
        v = v if v.ndim == 3 else v[:, None, :]
        return jnp.pad(v, ((0, 0), (0, 0), (0, _VEC_WIDTH - v.shape[-1])))

    def split_gates(g):
        pad = [(0, 0)] * (g.ndim - 1) + [(0, LANES - M_HEADS)]
        return jnp.concatenate([jnp.pad(g[..., :M_HEADS], pad), jnp.pad(g[..., M_HEADS:], pad)], axis=-1)

    table = {"mcw": p["m_conv_w"], "rcw": p["r_conv_w"], "mcb": p["m_conv_b"], "mnw": p["m_norm_w"],
             "mskip": p["m_skip"], "rcb": p["r_conv_b"], "rba": p["r_ba"], "rbx": p["r_bx"], "rlam": p["r_lam"],
             "sd": p["s_d"], "bglu": p["s_b_glu"], "bif": split_gates(p["m_b_if"]), "abre": abre, "abim": abim,
             "norm_w": p["norm_w"]}
    used = sum(n for _, n, _ in _VEC_LAYOUT.values())
    vec = jnp.concatenate([row(table[k]) for k in _VEC_LAYOUT]
                          + [jnp.zeros((DEPTH, _VEC_ROWS - used, _VEC_WIDTH), F32)], axis=1)

    def block_diag_halves(w):
        nb = R_BLOCKS // 2
        w5 = w.reshape(DEPTH, 2, nb, w.shape[-2], w.shape[-1])
        return jnp.einsum("ab,dhaij->dhaibj", jnp.eye(nb, dtype=F32), w5).reshape(
            DEPTH, 2, nb * w.shape[-2], nb * w.shape[-1]).astype(BF16)

    n_in = S_WIDTH // LANES
    g_in = LANES // S_GROUP
    n_out = 2
    g_out = S_GROUPS // n_out

    def bu_blocks(bb_t):
        bb5 = bb_t.reshape(DEPTH, S_GROUP, n_in, g_in, S_STATE)
        return jnp.einsum("ab,dckbp->dkacbp", jnp.eye(g_in, dtype=F32), bb5).reshape(
            DEPTH, n_in, LANES, g_in * S_STATE)

    def c_blocks(c):
        c5 = c.reshape(DEPTH, n_out, g_out, S_GROUP, S_STATE)
        return jnp.einsum("ab,dmbcp->dmapbc", jnp.eye(g_out, dtype=F32), c5).reshape(
            DEPTH, n_out, g_out * S_STATE, g_out * S_GROUP).astype(BF16)

    mats = {
        "wqk": jnp.concatenate([p["m_wq"], p["m_wk"]], axis=-1).astype(BF16),
        "wvo": jnp.concatenate([p["m_wv"], p["m_wo"]], axis=-1).astype(BF16),
        "wif": split_gates(p["m_w_if"]).astype(BF16),
        "rwa": block_diag_halves(p["r_wa"]), "rwx": block_diag_halves(p["r_wx"]),
        "wbu": jnp.concatenate([bu_blocks(bbre_t), bu_blocks(bbim_t)], axis=-1).astype(BF16),
        "wcre": c_blocks(p["s_c_re"]), "wcim": c_blocks(p["s_c_im"]),
        "wglu": p["s_w_glu"].astype(BF16),
    }
    return vec, p["w_in"].astype(BF16), p["w_out"].astype(BF16), [mats[k] for k in _MATRIX_KEYS]


def _sample_rows_state(n, m, mconv, h, rconv, sre, sim):
    bsz = n.shape[1]

    def flat(x):
        return x.reshape(DEPTH, bsz * SUBLANES, x.shape[-1])

    def on_first_token(x):
        return flat(jnp.pad(x[:, :, None, :], ((0, 0), (0, 0), (SAMPLE_LEAD, SUBLANES - SAMPLE_LEAD - 1), (0, 0))))

    def conv_rows(buf):
        return flat(jnp.pad(buf, ((0, 0), (0, 0), (SAMPLE_LEAD - (CONV_W - 1), SUBLANES - SAMPLE_LEAD), (0, 0))))

    def on_all_rows(x):
        return flat(jnp.broadcast_to(x[:, :, None, :], (DEPTH, bsz, SUBLANES, x.shape[-1])))

    m_pad = jnp.pad(m, ((0, 0), (0, 0), (0, LANES - M_HEADS)))
    return (on_all_rows(n.reshape(DEPTH, bsz, M_WIDTH)), on_all_rows(m_pad),
            conv_rows(mconv), on_first_token(h), conv_rows(rconv),
            on_first_token(sre.reshape(DEPTH, bsz, S_LANES)), on_first_token(sim.reshape(DEPTH, bsz, S_LANES)))


def _sample_state_from_rows(n, m, xmf, h, xrf, sre, sim):
    bsz = n.shape[1] // SUBLANES

    def seg(x):
        return x.reshape(DEPTH, bsz, SUBLANES, x.shape[-1])

    conv_from = SUBLANES - (CONV_W - 1)
    return (seg(n)[:, :, -1].reshape(DEPTH, bsz, M_HEADS, M_HEAD_DIM), seg(m)[:, :, -1, :M_HEADS],
            seg(xmf)[:, :, conv_from:], seg(h)[:, :, -1], seg(xrf)[:, :, conv_from:],
            seg(sre)[:, :, -1].reshape(DEPTH, bsz, S_GROUPS, S_STATE),
            seg(sim)[:, :, -1].reshape(DEPTH, bsz, S_GROUPS, S_STATE))


def _prompt_state_from_kernel(c, n, m, mtail, h, rtail, sre, sim):
    bsz = c.shape[1]
    return (c, n, m[..., 0], mtail[:, :, SUBLANES - 1::SUBLANES], h.reshape(DEPTH, bsz, R_WIDTH),
            rtail[:, :, SUBLANES - 1::SUBLANES],
            sre.reshape(DEPTH, bsz, S_GROUPS, S_STATE), sim.reshape(DEPTH, bsz, S_GROUPS, S_STATE))


def _interleave_chunks(x, inverse=False):
    bsz, seq, d = x.shape
    sub_len = PROMPT_CHUNK // SUBLANES
    inner = (sub_len, SUBLANES) if inverse else (SUBLANES, sub_len)
    return x.reshape(bsz, seq // PROMPT_CHUNK, *inner, d).transpose(0, 1, 3, 2, 4).reshape(bsz, seq, d)


def kernel(x_prompt, x_sample, state_mlstm_C, state_mlstm_n, state_mlstm_m, state_mlstm_conv, state_rglru_h, state_rglru_conv, state_s5_re, state_s5_im, norm_w, w_in, w_out, m_conv_w, m_conv_b, m_wq, m_wk, m_wv, m_wo, m_w_if, m_b_if, m_norm_w, m_skip, r_conv_w, r_conv_b, r_wa, r_ba, r_wx, r_bx, r_lam, s_lam_re, s_lam_im, s_b_re, s_b_im, s_c_re, s_c_im, s_d, s_log_step, s_w_glu, s_b_glu, final_norm_w):
    p = dict(norm_w=norm_w, w_in=w_in, w_out=w_out, m_conv_w=m_conv_w, m_conv_b=m_conv_b, m_wq=m_wq, m_wk=m_wk,
             m_wv=m_wv, m_wo=m_wo, m_w_if=m_w_if, m_b_if=m_b_if, m_norm_w=m_norm_w, m_skip=m_skip,
             r_conv_w=r_conv_w, r_conv_b=r_conv_b, r_wa=r_wa, r_ba=r_ba, r_wx=r_wx, r_bx=r_bx, r_lam=r_lam,
             s_c_re=s_c_re, s_c_im=s_c_im, s_d=s_d, s_w_glu=s_w_glu, s_b_glu=s_b_glu)
    s5 = _s5_prep(s_lam_re, s_lam_im, s_log_step, s_b_re, s_b_im)
    vec, w_in_b, w_out_b, mats = _prepare_weights(p, s5)
    final_w = final_norm_w.reshape(1, D_MODEL)

    bsz, seq, _ = x_prompt.shape
    dec_batch, dec_seq, _ = x_sample.shape
    n_prompt = bsz * seq
    n_sample = dec_batch * SUBLANES
    stream = jnp.concatenate(
        [_interleave_chunks(x_prompt).reshape(n_prompt, D_MODEL),
         jnp.pad(x_sample, ((0, 0), (SAMPLE_LEAD, 0), (0, 0))).reshape(n_sample, D_MODEL)], axis=0)
    rows_state = _sample_rows_state(state_mlstm_n, state_mlstm_m, state_mlstm_conv, state_rglru_h,
                                    state_rglru_conv, state_s5_re, state_s5_im)
    pr_states = None
    sa_states = None
    for l in range(DEPTH):
        proj = _inproj(stream, vec, w_in_b, l)
        mixed, *pr_states = _prompt_mixer(proj, l, bsz, seq, vec, mats, pr_states)
        mixed, *sa_states = _sample_mixer(proj, l, n_prompt, n_sample, mixed, state_mlstm_C, rows_state, vec, mats,
                                          sa_states)
        stream = _outproj(mixed, stream, w_out_b, l, final_w, l == DEPTH - 1)
    y_prompt = _interleave_chunks(stream[:n_prompt].reshape(bsz, seq, D_MODEL), inverse=True)
    y_sample = stream[n_prompt:].reshape(dec_batch, SUBLANES, D_MODEL)[:, SAMPLE_LEAD:]
    return (y_prompt, y_sample, *_prompt_state_from_kernel(*pr_states),
            sa_states[0], *_sample_state_from_rows(*sa_states[1:]))
```

```python
import functools

import jax
import jax.numpy as jnp
from jax import lax
from jax.experimental import pallas as pl
from jax.experimental.pallas import tpu as pltpu

F32 = jnp.float32
BF16 = jnp.bfloat16

D_MODEL = 2048
DEPTH = 2
MIX_WIDTH = D_MODEL
M_WIDTH = MIX_WIDTH // 2
R_WIDTH = MIX_WIDTH // 4
S_WIDTH = MIX_WIDTH - M_WIDTH - R_WIDTH
M_HEADS = 8
M_HEAD_DIM = M_WIDTH // M_HEADS
R_BLOCKS = 8
RG_C = 8.0
S_GROUP = 16
S_GROUPS = S_WIDTH // S_GROUP
S_STATE = 64
S_LANES = S_GROUPS * S_STATE
CONV_W = 4
EPS = 1e-6

SUBLANES = 8
SUBLANE_SHIFT = SUBLANES.bit_length() - 1
LANES = 128
VMEM_LIMIT_BYTES = 56 * 1024 * 1024

PROMPT_CHUNK = 256
SAMPLE_ROWS = 128
SAMPLE_C_SEQS = 8
SAMPLE_LEAD = SUBLANES - 4
INPROJ_TM = 512
INPROJ_TN = 1024
OUTPROJ_TM = 512
SCAN_LANE_BLOCK = 256

_NT = (((1,), (1,)), ((), ()))

_VEC_WIDTH = D_MODEL
_VEC_LAYOUT = {}
_row = 0
for _name, _n, _w in (("mcw", CONV_W, M_WIDTH), ("rcw", CONV_W, R_WIDTH), ("mcb", 1, M_WIDTH), ("mnw", 1, M_WIDTH),
                      ("mskip", 1, M_WIDTH), ("rcb", 1, R_WIDTH), ("rba", 1, R_WIDTH), ("rbx", 1, R_WIDTH),
                      ("rlam", 1, R_WIDTH), ("sd", 1, S_WIDTH), ("bglu", 1, S_WIDTH), ("bif", 1, 2 * LANES),
                      ("abre", 1, S_LANES), ("abim", 1, S_LANES), ("norm_w", 1, D_MODEL)):
    _VEC_LAYOUT[_name] = (_row, _n, _w)
    _row += _n
_VEC_ROWS = -(-_row // SUBLANES) * SUBLANES


def _vec(vec_ref, name):
    r0, n, w = _VEC_LAYOUT[name]
    return vec_ref[r0:r0 + n, 0:w]


def _layer_spec(arr, l):
    nd = arr.ndim - 1
    return pl.BlockSpec((None,) + arr.shape[1:], lambda *_: (l,) + (0,) * nd, pipeline_mode=pl.Buffered(1))


_ANY_SPEC = pl.BlockSpec(memory_space=pl.ANY)


def _s5_prep_kernel(lr_ref, li_ref, ls_ref, brt_ref, bit_ref, abre_ref, abim_ref, bbre_ref, bbim_ref):
    lr = lr_ref[0]
    li = li_ref[0]
    dt = jnp.exp(ls_ref[0])
    mag = jnp.exp(lr * dt)
    ang = li * dt
    ab_re = mag * jnp.cos(ang)
    ab_im = mag * jnp.sin(ang)
    den = lr * lr + li * li
    nr = ab_re - 1.0
    f_re = (nr * lr + ab_im * li) / den
    f_im = (ab_im * lr - nr * li) / den
    br = brt_ref[0]
    bi = bit_ref[0]
    abre_ref[0] = ab_re
    abim_ref[0] = ab_im
    bbre_ref[0] = f_re * br - f_im * bi
    bbim_ref[0] = f_re * bi + f_im * br


def _s5_prep(s_lam_re, s_lam_im, s_log_step, s_b_re, s_b_im):
    lr = s_lam_re.reshape(DEPTH, 1, S_LANES)
    li = s_lam_im.reshape(DEPTH, 1, S_LANES)
    ls = jnp.repeat(s_log_step, S_STATE, axis=-1).reshape(DEPTH, 1, S_LANES)
    brt = s_b_re.reshape(DEPTH, S_LANES, S_GROUP).transpose(0, 2, 1)
    bit = s_b_im.reshape(DEPTH, S_LANES, S_GROUP).transpose(0, 2, 1)
    vec = pl.BlockSpec((1, 1, S_LANES), lambda l: (l, 0, 0))
    mat = pl.BlockSpec((1, S_GROUP, S_LANES), lambda l: (l, 0, 0))
    return pl.pallas_call(
        _s5_prep_kernel,
        grid=(DEPTH,),
        in_specs=[vec, vec, vec, mat, mat],
        out_specs=[vec, vec, mat, mat],
        out_shape=[jax.ShapeDtypeStruct((DEPTH, 1, S_LANES), F32),
                   jax.ShapeDtypeStruct((DEPTH, 1, S_LANES), F32),
                   jax.ShapeDtypeStruct((DEPTH, S_GROUP, S_LANES), F32),
                   jax.ShapeDtypeStruct((DEPTH, S_GROUP, S_LANES), F32)],
        name="s5_prep",
    )(lr, li, ls, brt, bit)


def _stream_specs(parts, tm):
    if len(parts) == 1:
        return [pl.BlockSpec((tm, D_MODEL), lambda i: (i, 0))]
    nb0 = parts[0].shape[0] // tm
    return [pl.BlockSpec((tm, D_MODEL), lambda i: (jnp.minimum(i, nb0 - 1), 0)),
            pl.BlockSpec((tm, D_MODEL), lambda i: (jnp.maximum(i - nb0, 0), 0))]


def _read_stream(refs, nb0):
    if len(refs) == 1:
        return refs[0][...]
    return jnp.where(pl.program_id(0) < nb0, refs[0][...], refs[1][...])


def _inproj_kernel(*refs, n_x, nb0):
    x_refs, (vec_ref, w_ref, o_ref) = refs[:n_x], refs[n_x:]
    x = _read_stream(x_refs, nb0)
    ms = jnp.mean(x * x, axis=-1, keepdims=True)
    xn = (x * lax.rsqrt(ms + EPS) * _vec(vec_ref, "norm_w")).astype(BF16)
    for j in range((2 * MIX_WIDTH) // INPROJ_TN):
        cols = slice(j * INPROJ_TN, (j + 1) * INPROJ_TN)
        o_ref[:, cols] = jnp.dot(xn, w_ref[:, cols], preferred_element_type=F32)


def _inproj(x_parts, vec, w_in_b, l):
    n = sum(x.shape[0] for x in x_parts)
    return pl.pallas_call(
        functools.partial(_inproj_kernel, n_x=len(x_parts), nb0=x_parts[0].shape[0] // INPROJ_TM),
        grid=(n // INPROJ_TM,),
        in_specs=_stream_specs(x_parts, INPROJ_TM) + [_layer_spec(vec, l), _layer_spec(w_in_b, l)],
        out_specs=pl.BlockSpec((INPROJ_TM, 2 * MIX_WIDTH), lambda i: (i, 0)),
        out_shape=jax.ShapeDtypeStruct((n, 2 * MIX_WIDTH), F32),
        compiler_params=pltpu.CompilerParams(
            dimension_semantics=("arbitrary",), vmem_limit_bytes=VMEM_LIMIT_BYTES),
        name="inproj",
    )(*x_parts, vec, w_in_b)


def _outproj_kernel(*refs, n_x, nb0, n_out, nb0_out, final):
    mixed_ref, x_refs = refs[0], refs[1:1 + n_x]
    w_ref, fw_ref = refs[1 + n_x:3 + n_x]
    o_refs = refs[3 + n_x:]
    y = _read_stream(x_refs, nb0) + jnp.dot(mixed_ref[...], w_ref[...], preferred_element_type=F32)
    if final:
        ms = jnp.mean(y * y, axis=-1, keepdims=True)
        y = y * lax.rsqrt(ms + EPS) * fw_ref[...]
    if n_out == 1:
        o_refs[0][...] = y
    else:
        @pl.when(pl.program_id(0) < nb0_out)
        def _():
            o_refs[0][...] = y

        @pl.when(pl.program_id(0) >= nb0_out)
        def _():
            o_refs[1][...] = y


def _outproj(mixed, x_parts, w_out_b, l, final_w, final, split_rows=None):
    n = mixed.shape[0]
    out_rows = [n] if split_rows is None else [split_rows, n - split_rows]
    out_parts = [jax.ShapeDtypeStruct((r, D_MODEL), F32) for r in out_rows]
    return pl.pallas_call(
        functools.partial(_outproj_kernel, n_x=len(x_parts), nb0=x_parts[0].shape[0] // OUTPROJ_TM,
                          n_out=len(out_parts), nb0_out=out_rows[0] // OUTPROJ_TM, final=final),
        grid=(n // OUTPROJ_TM,),
        in_specs=([pl.BlockSpec((OUTPROJ_TM, MIX_WIDTH), lambda i: (i, 0))] + _stream_specs(x_parts, OUTPROJ_TM)
                  + [_layer_spec(w_out_b, l), pl.BlockSpec((1, D_MODEL), lambda i: (0, 0))]),
        out_specs=_stream_specs(out_parts, OUTPROJ_TM),
        out_shape=out_parts,
        compiler_params=pltpu.CompilerParams(
            dimension_semantics=("arbitrary",), vmem_limit_bytes=VMEM_LIMIT_BYTES),
        name="outproj",
    )(mixed, *x_parts, w_out_b, final_w)


def _silu(x):
    return x * jax.nn.sigmoid(x)


def _log_sigmoid(x):
    return jnp.minimum(x, 0.0) - jnp.log1p(jnp.exp(-jnp.abs(x)))


def _softplus(x):
    return jnp.maximum(x, 0.0) + jnp.log1p(jnp.exp(-jnp.abs(x)))


def _sublane_pos(shape):
    return lax.broadcasted_iota(jnp.int32, shape, 0) & (SUBLANES - 1)


def _tile_cumsum(x):
    pos = _sublane_pos(x.shape)
    s = 1
    while s < SUBLANES:
        x = x + jnp.where(pos >= s, pltpu.roll(x, s, 0), 0.0)
        s *= 2
    return x


def _tile_scan_real(a, b):
    pos = _sublane_pos(a.shape)
    s = 1
    while s < SUBLANES:
        m = pos >= s
        b = jnp.where(m, a * pltpu.roll(b, s, 0) + b, b)
        if 2 * s < SUBLANES:
            a = jnp.where(m, a * pltpu.roll(a, s, 0), a)
        s *= 2
    return b


def _tile_scan_cplx(sr, si, pr, pi):
    pos = _sublane_pos(sr.shape)
    s = 1
    while s < SUBLANES:
        m = pos >= s
        sr_sh = pltpu.roll(sr, s, 0)
        si_sh = pltpu.roll(si, s, 0)
        sr, si = (jnp.where(m, sr + (pr * sr_sh - pi * si_sh), sr),
                  jnp.where(m, si + (pr * si_sh + pi * sr_sh), si))
        if 2 * s < SUBLANES:
            pr, pi = pr * pr - pi * pi, 2.0 * (pr * pi)
        s *= 2
    return sr, si


def _row_from_col(col, eye):
    return jnp.sum(jnp.where(eye, col, 0.0), axis=0, keepdims=True)


def _mlstm_project(xc, xm, vec_ref, wqk_ref, wvo_ref, wif_ref, q_scr, k_scr, v_scr, o_scr):
    xc_b = xc.astype(BF16)
    xm_b = xm.astype(BF16)
    for h in range(M_HEADS):
        sl = slice(h * M_HEAD_DIM, (h + 1) * M_HEAD_DIM)
        qk = jnp.dot(xc_b[:, sl], wqk_ref[h], preferred_element_type=F32)
        vo = jnp.dot(xm_b[:, sl], wvo_ref[h], preferred_element_type=F32)
        q_scr[:, sl] = qk[:, :M_HEAD_DIM]
        k_scr[:, sl] = qk[:, M_HEAD_DIM:]
        v_scr[:, sl] = vo[:, :M_HEAD_DIM]
        o_scr[:, sl] = vo[:, M_HEAD_DIM:]
    return (jnp.dot(q_scr[...].astype(BF16), wif_ref[0:M_WIDTH, :], preferred_element_type=F32)
            + jnp.dot(k_scr[...].astype(BF16), wif_ref[M_WIDTH:2 * M_WIDTH, :], preferred_element_type=F32)
            + jnp.dot(v_scr[...].astype(BF16), wif_ref[2 * M_WIDTH:3 * M_WIDTH, :], preferred_element_type=F32)
            + _vec(vec_ref, "bif"))


def _head_output(hh, o_pre, xc_h, z_h, mnw_h, mskip_h):
    mu = jnp.mean(hh, axis=1, keepdims=True)
    hc = hh - mu
    var = jnp.mean(hc * hc, axis=1, keepdims=True)
    hn = hc * lax.rsqrt(var + EPS) * mnw_h
    return (jax.nn.sigmoid(o_pre) * hn + mskip_h * xc_h) * _silu(z_h)


def _rglru_coeffs(xcr, vec_ref, rwa_ref, rwx_ref):
    xcr_b = xcr.astype(BF16)
    half = R_WIDTH // 2
    ra_pre = jnp.concatenate(
        [jnp.dot(xcr_b[:, :half], rwa_ref[0], preferred_element_type=F32),
         jnp.dot(xcr_b[:, half:], rwa_ref[1], preferred_element_type=F32)], axis=1) + _vec(vec_ref, "rba")
    rx_pre = jnp.concatenate(
        [jnp.dot(xcr_b[:, :half], rwx_ref[0], preferred_element_type=F32),
         jnp.dot(xcr_b[:, half:], rwx_ref[1], preferred_element_type=F32)], axis=1) + _vec(vec_ref, "rbx")
    log_a = (-RG_C) * jax.nn.sigmoid(ra_pre) * _softplus(-_vec(vec_ref, "rlam"))
    a = jnp.exp(log_a)
    th = jnp.tanh(log_a)
    one_minus_a2 = (-2.0 * th) / (1.0 - th)
    return a, jnp.sqrt(one_minus_a2) * (jax.nn.sigmoid(rx_pre) * xcr)


def _s5_project(u_b, wbu_ref, re_ref, im_ref):
    per_slice = (LANES // S_GROUP) * S_STATE
    for k in range(S_WIDTH // LANES):
        res = jnp.dot(u_b[:, k * LANES:(k + 1) * LANES], wbu_ref[k], preferred_element_type=F32)
        re_ref[:, k * per_slice:(k + 1) * per_slice] = res[:, :per_slice]
        im_ref[:, k * per_slice:(k + 1) * per_slice] = res[:, per_slice:]


def _s5_output(re_ref, im_ref, u, vec_ref, wcre_ref, wcim_ref, wglu_ref):
    nblk = wcre_ref.shape[0]
    k_blk = S_LANES // nblk
    parts = []
    for m in range(nblk):
        ks = slice(m * k_blk, (m + 1) * k_blk)
        parts.append(jnp.dot(re_ref[:, ks].astype(BF16), wcre_ref[m], preferred_element_type=F32)
                     - jnp.dot(im_ref[:, ks].astype(BF16), wcim_ref[m], preferred_element_type=F32))
    y = jnp.concatenate(parts, axis=1) + _vec(vec_ref, "sd") * u
    g = jax.nn.gelu(y)
    return g * jax.nn.sigmoid(jnp.dot(g.astype(BF16), wglu_ref[...], preferred_element_type=F32)
                              + _vec(vec_ref, "bglu"))


_MATRIX_KEYS = ("wqk", "wvo", "wif", "rwa", "rwx", "wbu", "wcre", "wcim", "wglu")


def _tile(x, i):
    return x[i * SUBLANES:(i + 1) * SUBLANES]


def _conv_interleaved(x, tail_ref, w, bias, rows):
    ntiles = rows // SUBLANES
    pos = lax.broadcasted_iota(jnp.int32, (SUBLANES, x.shape[1]), 0)
    prev = tail_ref[...]
    before = [pltpu.roll(jnp.where(pos == SUBLANES - 1, _tile(prev, CONV_W - 1 - d), _tile(x, ntiles - d)), 1, 0)
              for d in range(1, CONV_W)]
    out = w[CONV_W - 1:CONV_W, :] * x + bias
    for j in range(1, CONV_W):
        shifted = jnp.concatenate(before[:j][::-1] + [x[:rows - j * SUBLANES]], axis=0)
        out = out + w[CONV_W - 1 - j:CONV_W - j, :] * shifted
    tail_ref[...] = x[rows - (CONV_W - 1) * SUBLANES:]
    return out


def _cumsum_interleaved(x, rows):
    tiles = [_tile(x, 0)]
    for i in range(1, rows // SUBLANES):
        tiles.append(tiles[-1] + _tile(x, i))
    total = tiles[-1]
    start = _tile_cumsum(total) - total
    return jnp.concatenate([t + start for t in tiles], axis=0)


def _scan_real_interleaved(a_ref, b_ref, carry, rows):
    ntiles = rows // SUBLANES
    pos = lax.broadcasted_iota(jnp.int32, (SUBLANES, a_ref.shape[1]), 0)
    h = _tile(b_ref, 0)
    aprod = _tile(a_ref, 0)
    for i in range(1, ntiles):
        a = _tile(a_ref, i)
        h = a * h + _tile(b_ref, i)
        aprod = aprod * a
    g = _tile_scan_real(aprod, h + jnp.where(pos == 0, aprod * carry, 0.0))
    h = jnp.where(pos == 0, carry, pltpu.roll(g, 1, 0))
    for i in range(ntiles):
        h = _tile(a_ref, i) * h + _tile(b_ref, i)
        b_ref[i * SUBLANES:(i + 1) * SUBLANES, :] = h
    return g[SUBLANES - 1:SUBLANES]


def _scan_cplx_interleaved(re_ref, im_ref, p_re, p_im, cre_ref, cim_ref, rows):
    ntiles = rows // SUBLANES
    width = re_ref.shape[1]
    pos = lax.broadcasted_iota(jnp.int32, (SUBLANES, SCAN_LANE_BLOCK), 0)
    for blk in range(width // SCAN_LANE_BLOCK):
        sl = slice(blk * SCAN_LANE_BLOCK, (blk + 1) * SCAN_LANE_BLOCK)
        pr = jnp.broadcast_to(p_re[:, sl], pos.shape)
        pi = jnp.broadcast_to(p_im[:, sl], pos.shape)
        c_r = cre_ref[:, sl]
        c_i = cim_ref[:, sl]
        sr = re_ref[0:SUBLANES, sl]
        si = im_ref[0:SUBLANES, sl]
        for i in range(1, ntiles):
            rs = slice(i * SUBLANES, (i + 1) * SUBLANES)
            sr, si = pr * sr - pi * si + re_ref[rs, sl], pr * si + pi * sr + im_ref[rs, sl]
        qr, qi = pr, pi
        n = 1
        while n < ntiles:
            qr, qi = qr * qr - qi * qi, 2.0 * (qr * qi)
            n *= 2
        gr, gi = _tile_scan_cplx(sr + jnp.where(pos == 0, qr * c_r - qi * c_i, 0.0),
                                 si + jnp.where(pos == 0, qr * c_i + qi * c_r, 0.0), qr, qi)
        sr = jnp.where(pos == 0, c_r, pltpu.roll(gr, 1, 0))
        si = jnp.where(pos == 0, c_i, pltpu.roll(gi, 1, 0))
        for i in range(ntiles):
            rs = slice(i * SUBLANES, (i + 1) * SUBLANES)
            sr, si = pr * sr - pi * si + re_ref[rs, sl], pr * si + pi * sr + im_ref[rs, sl]
            re_ref[rs, sl] = sr
            im_ref[rs, sl] = si
        cre_ref[:, sl] = gr[SUBLANES - 1:SUBLANES]
        cim_ref[:, sl] = gi[SUBLANES - 1:SUBLANES]


def _prompt_mixer_kernel(*refs, rows, n_alias):
    (proj_ref, vec_ref, wqk_ref, wvo_ref, wif_ref, rwa_ref, rwx_ref, wbu_ref, wcre_ref, wcim_ref, wglu_ref) = refs[:11]
    (mixed_ref, c_ref, n_ref, m_ref, mtail_ref, h_ref, rtail_ref, sre_ref, sim_ref,
     q_scr, k_scr, v_scr, o_scr, ra_scr, rb_scr, ure_scr, uim_scr) = refs[11 + n_alias:]
    T = rows

    @pl.when(pl.program_id(1) == 0)
    def _():
        for ref in (c_ref, n_ref, m_ref, mtail_ref, h_ref, rtail_ref, sre_ref, sim_ref):
            ref[...] = jnp.zeros(ref.shape, F32)

    xm = proj_ref[:, 0:M_WIDTH]
    xc = _silu(_conv_interleaved(xm, mtail_ref, _vec(vec_ref, "mcw"), _vec(vec_ref, "mcb"), T))
    gates = _mlstm_project(xc, xm, vec_ref, wqk_ref, wvo_ref, wif_ref, q_scr, k_scr, v_scr, o_scr)
    log_i = gates[:, :LANES]
    b_all = _cumsum_interleaved(_log_sigmoid(gates[:, LANES:]), T)

    rowi = lax.broadcasted_iota(jnp.int32, (T, T), 0)
    coli = lax.broadcasted_iota(jnp.int32, (T, T), 1)
    sub_len = T // SUBLANES

    def time_of(r):
        return (r & (SUBLANES - 1)) * sub_len + (r >> SUBLANE_SHIFT)

    causal = time_of(coli) <= time_of(rowi)
    eye = coli == rowi
    k_scale = M_HEAD_DIM ** -0.5
    mnw = _vec(vec_ref, "mnw")
    mskip = _vec(vec_ref, "mskip")

    for h in range(M_HEADS):
        sl = slice(h * M_HEAD_DIM, (h + 1) * M_HEAD_DIM)
        bcol = b_all[:, h:h + 1]
        licol = log_i[:, h:h + 1]
        brow = _row_from_col(bcol, eye)
        lirow = _row_from_col(licol, eye)
        m_prev = m_ref[h:h + 1, 0:1]
        c_prev = c_ref[h]
        n_prev = n_ref[h:h + 1, :]

        q = q_scr[:, sl]
        ks = k_scr[:, sl] * k_scale
        v = v_scr[:, sl]
        q_b = q.astype(BF16)
        ks_b = ks.astype(BF16)

        log_d = jnp.where(causal, bcol - brow + lirow, -jnp.inf)
        log_inter = bcol + m_prev
        m_t = jnp.maximum(log_inter, jnp.max(log_d, axis=1, keepdims=True))
        w_intra = jnp.exp(log_d - m_t)
        w_inter = jnp.exp(log_inter - m_t)
        s = lax.dot_general(q_b, ks_b, _NT, preferred_element_type=F32) * w_intra
        inter = lax.dot_general(q_b, c_prev.astype(BF16), _NT, preferred_element_type=F32)
        num = jnp.dot(s.astype(BF16), v.astype(BF16), preferred_element_type=F32) + w_inter * inter
        den = (jnp.sum(s, axis=1, keepdims=True)
               + w_inter * jnp.sum(q * n_prev, axis=1, keepdims=True))
        hh = num / jnp.maximum(jnp.abs(den), jnp.exp(-m_t))

        b_end = bcol[T - 1:T, :]
        m_new = jnp.maximum(b_end + m_prev, jnp.max(b_end - brow + lirow, axis=1, keepdims=True))
        w_src = jnp.exp(b_end - bcol + licol - m_new)
        decay = jnp.exp(b_end + m_prev - m_new)
        vw_t = (v * w_src).T.astype(BF16)
        c_ref[h] = decay * c_prev + jnp.dot(vw_t, ks_b, preferred_element_type=F32)
        n_ref[h:h + 1, :] = decay * n_prev + jnp.sum(ks * w_src, axis=0, keepdims=True)
        m_ref[h:h + 1, :] = jnp.broadcast_to(m_new, (1, LANES))

        z = proj_ref[:, MIX_WIDTH + h * M_HEAD_DIM:MIX_WIDTH + (h + 1) * M_HEAD_DIM]
        mixed_ref[:, sl] = _head_output(hh, o_scr[:, sl], xc[:, sl], z, mnw[:, sl], mskip[:, sl]).astype(BF16)

    xr = proj_ref[:, M_WIDTH:M_WIDTH + R_WIDTH]
    xcr = _conv_interleaved(xr, rtail_ref, _vec(vec_ref, "rcw"), _vec(vec_ref, "rcb"), T)
    a, bb = _rglru_coeffs(xcr, vec_ref, rwa_ref, rwx_ref)
    ra_scr[...] = a
    rb_scr[...] = bb
    h_ref[...] = _scan_real_interleaved(ra_scr, rb_scr, h_ref[...], T)
    zr = proj_ref[:, MIX_WIDTH + M_WIDTH:MIX_WIDTH + M_WIDTH + R_WIDTH]
    mixed_ref[:, M_WIDTH:M_WIDTH + R_WIDTH] = (rb_scr[...] * _silu(zr)).astype(BF16)

    u = proj_ref[:, M_WIDTH + R_WIDTH:MIX_WIDTH]
    _s5_project(u.astype(BF16), wbu_ref, ure_scr, uim_scr)
    _scan_cplx_interleaved(ure_scr, uim_scr, _vec(vec_ref, "abre"), _vec(vec_ref, "abim"), sre_ref, sim_ref, T)
    glu = _s5_output(ure_scr, uim_scr, u, vec_ref, wcre_ref, wcim_ref, wglu_ref)
    zs = proj_ref[:, MIX_WIDTH + M_WIDTH + R_WIDTH:2 * MIX_WIDTH]
    mixed_ref[:, M_WIDTH + R_WIDTH:MIX_WIDTH] = (glu * _silu(zs)).astype(BF16)


_PROMPT_STATE_SHAPES = ((M_HEADS, M_HEAD_DIM, M_HEAD_DIM), (M_HEADS, M_HEAD_DIM), (M_HEADS, LANES),
                        ((CONV_W - 1) * SUBLANES, M_WIDTH), (1, R_WIDTH), ((CONV_W - 1) * SUBLANES, R_WIDTH),
                        (1, S_LANES), (1, S_LANES))


def _prompt_mixer(proj, l, bsz, seq, vec, mats, prev_states):
    rows = PROMPT_CHUNK
    nchunk = seq // rows
    aliases = {}
    alias_args = []
    if prev_states is not None:
        alias_args = list(prev_states)
        aliases = {2 + len(mats) + k: 1 + k for k in range(len(alias_args))}

    def state_spec(shape):
        nd = len(shape)
        return pl.BlockSpec((None, None) + shape, lambda b, c: (l, b) + (0,) * nd)

    rows_map = lambda b, c: (b * nchunk + c, 0)
    out_shape = ([jax.ShapeDtypeStruct((proj.shape[0], MIX_WIDTH), BF16)]
                 + [jax.ShapeDtypeStruct((DEPTH, bsz) + s, F32) for s in _PROMPT_STATE_SHAPES])
    scratch = [pltpu.VMEM((rows, M_WIDTH), F32), pltpu.VMEM((rows, M_WIDTH), F32),
               pltpu.VMEM((rows, M_WIDTH), F32), pltpu.VMEM((rows, M_WIDTH), F32),
               pltpu.VMEM((rows, R_WIDTH), F32), pltpu.VMEM((rows, R_WIDTH), F32),
               pltpu.VMEM((rows, S_LANES), F32), pltpu.VMEM((rows, S_LANES), F32)]
    return pl.pallas_call(
        functools.partial(_prompt_mixer_kernel, rows=rows, n_alias=len(alias_args)),
        grid=(bsz, nchunk),
        in_specs=([pl.BlockSpec((rows, 2 * MIX_WIDTH), rows_map), _layer_spec(vec, l)]
                  + [_layer_spec(w, l) for w in mats] + [_ANY_SPEC] * len(alias_args)),
        out_specs=[pl.BlockSpec((rows, MIX_WIDTH), rows_map)] + [state_spec(s) for s in _PROMPT_STATE_SHAPES],
        out_shape=out_shape,
        scratch_shapes=scratch,
        input_output_aliases=aliases,
        compiler_params=pltpu.CompilerParams(
            dimension_semantics=("parallel", "arbitrary"), vmem_limit_bytes=VMEM_LIMIT_BYTES),
        name="prompt_mixer",
    )(proj, vec, *mats, *alias_args)


def _seg_last(x, groups):
    x3 = x.reshape(groups, SUBLANES, x.shape[-1])
    return jnp.broadcast_to(x3[:, SUBLANES - 1:SUBLANES, :], x3.shape).reshape(x.shape)


def _seg_max(x, groups):
    x3 = x.reshape(groups, SUBLANES, x.shape[-1])
    return jnp.broadcast_to(jnp.max(x3, axis=1, keepdims=True), x3.shape).reshape(x.shape)


def _seg_sum(x, groups):
    x3 = x.reshape(groups, SUBLANES, x.shape[-1])
    return jnp.broadcast_to(jnp.sum(x3, axis=1, keepdims=True), x3.shape).reshape(x.shape)


def _seg_rows(state_ref, lanes=slice(None)):
    x = state_ref[:, :, lanes]
    return jnp.broadcast_to(x, (x.shape[0], SUBLANES, x.shape[2])).reshape(x.shape[0] * SUBLANES, x.shape[2])


def _seg_state(x):
    x3 = x.reshape(x.shape[0] // SUBLANES, SUBLANES, x.shape[-1])
    return x3[:, SUBLANES - 1:SUBLANES, :]


def _conv_rolled(xf, w, bias):
    out = w[CONV_W - 1:CONV_W, :] * xf + bias
    for j in range(1, CONV_W):
        out = out + w[CONV_W - 1 - j:CONV_W - j, :] * pltpu.roll(xf, j, 0)
    return out


_SAMPLE_STATE_WIDTHS = (M_WIDTH, LANES, M_WIDTH, R_WIDTH, R_WIDTH, S_LANES, S_LANES)
_SAMPLE_STATE_PER_ROW = (False, False, True, False, True, False, False)


def _sample_mixer_kernel(*refs, rows, c_seqs, n_alias):
    (proj_ref, c0_ref, n0_ref, m0_ref, mtail0_ref, h0_ref, rtail0_ref, sre0_ref, sim0_ref,
     vec_ref, wqk_ref, wvo_ref, wif_ref, rwa_ref, rwx_ref, wbu_ref, wcre_ref, wcim_ref, wglu_ref) = refs[:19]
    (mixed_ref, c_ref, n_ref, m_ref, xmf_ref, h_ref, xrf_ref, sre_ref, sim_ref,
     q_scr, k_scr, v_scr, o_scr, xc_scr, intert_scr, vwt_scr, qt_scr, ksb_scr, dec_scr, b_scr, li_scr,
     ure_scr, uim_scr) = refs[19 + n_alias:]
    R = rows
    G = R // SUBLANES
    sub = pl.program_id(1)
    k_scale = M_HEAD_DIM ** -0.5
    pos = _sublane_pos((R, 1))
    valid = pos >= SAMPLE_LEAD
    first_token = pos == SAMPLE_LEAD

    @pl.when(sub == 0)
    def _():
        xm = jnp.where(valid, proj_ref[:, 0:M_WIDTH], mtail0_ref[...])
        xmf_ref[...] = xm
        xc = _silu(_conv_rolled(xm, _vec(vec_ref, "mcw"), _vec(vec_ref, "mcb")))
        xc_scr[...] = xc
        gates = _mlstm_project(xc, xm, vec_ref, wqk_ref, wvo_ref, wif_ref, q_scr, k_scr, v_scr, o_scr)
        log_i = jnp.where(valid, gates[:, :LANES], -jnp.inf)
        log_f = jnp.where(valid, _log_sigmoid(gates[:, LANES:]), 0.0)
        b_all = _tile_cumsum(log_f)
        b_end = _seg_last(b_all, G)
        m_rows = _seg_rows(m0_ref)
        n_rows = _seg_rows(n0_ref)
        log_src = b_end - b_all + log_i
        m_new = jnp.maximum(b_end + m_rows, _seg_max(log_src, G))
        w_src_all = jnp.exp(log_src - m_new)
        decay_all = jnp.exp(b_end + m_rows - m_new)
        dec_scr[...] = decay_all
        b_scr[...] = b_all
        li_scr[...] = log_i
        m_ref[...] = _seg_state(m_new)
        for h in range(M_HEADS):
            sl = slice(h * M_HEAD_DIM, (h + 1) * M_HEAD_DIM)
            ks = k_scr[:, sl] * k_scale
            w_src = w_src_all[:, h:h + 1]
            n_ref[:, :, sl] = _seg_state(decay_all[:, h:h + 1] * n_rows[:, sl] + _seg_sum(ks * w_src, G))
            vwt_scr[h] = (v_scr[:, sl] * w_src).T
            qt_scr[h] = q_scr[:, sl].T.astype(BF16)
            ksb_scr[h] = ks.astype(BF16)
            intert_scr[h] = jnp.zeros((M_HEAD_DIM, R), F32)

        xr = jnp.where(valid, proj_ref[:, M_WIDTH:M_WIDTH + R_WIDTH], rtail0_ref[...])
        xrf_ref[...] = xr
        xcr = _conv_rolled(xr, _vec(vec_ref, "rcw"), _vec(vec_ref, "rcb"))
        a, bb = _rglru_coeffs(xcr, vec_ref, rwa_ref, rwx_ref)
        hs = _tile_scan_real(a, jnp.where(valid, bb, 0.0) + jnp.where(first_token, a * _seg_rows(h0_ref), 0.0))
        h_ref[...] = _seg_state(hs)
        zr = proj_ref[:, MIX_WIDTH + M_WIDTH:MIX_WIDTH + M_WIDTH + R_WIDTH]
        mixed_ref[:, M_WIDTH:M_WIDTH + R_WIDTH] = jnp.where(valid, hs * _silu(zr), 0.0).astype(BF16)

        u = proj_ref[:, M_WIDTH + R_WIDTH:MIX_WIDTH]
        _s5_project(u.astype(BF16), wbu_ref, ure_scr, uim_scr)
        p_re = _vec(vec_ref, "abre")
        p_im = _vec(vec_ref, "abim")
        for blk in range(S_LANES // SCAN_LANE_BLOCK):
            sl = slice(blk * SCAN_LANE_BLOCK, (blk + 1) * SCAN_LANE_BLOCK)
            pr = jnp.broadcast_to(p_re[:, sl], (R, SCAN_LANE_BLOCK))
            pi = jnp.broadcast_to(p_im[:, sl], (R, SCAN_LANE_BLOCK))
            s0r = _seg_rows(sre0_ref, sl)
            s0i = _seg_rows(sim0_ref, sl)
            sr, si = _tile_scan_cplx(
                jnp.where(valid, ure_scr[:, sl], 0.0) + jnp.where(first_token, pr * s0r - pi * s0i, 0.0),
                jnp.where(valid, uim_scr[:, sl], 0.0) + jnp.where(first_token, pr * s0i + pi * s0r, 0.0), pr, pi)
            ure_scr[:, sl] = sr
            uim_scr[:, sl] = si
            sre_ref[:, :, sl] = _seg_state(sr)
            sim_ref[:, :, sl] = _seg_state(si)
        glu = _s5_output(ure_scr, uim_scr, u, vec_ref, wcre_ref, wcim_ref, wglu_ref)
        zs = proj_ref[:, MIX_WIDTH + M_WIDTH + R_WIDTH:2 * MIX_WIDTH]
        mixed_ref[:, M_WIDTH + R_WIDTH:MIX_WIDTH] = jnp.where(valid, glu * _silu(zs), 0.0).astype(BF16)

    lane_seq = lax.broadcasted_iota(jnp.int32, (M_HEAD_DIM, R), 1) >> SUBLANE_SHIFT
    seq0 = sub * c_seqs
    for h in range(M_HEADS):
        c_old = c0_ref[:, h].reshape(c_seqs * M_HEAD_DIM, M_HEAD_DIM)
        readout = jnp.dot(c_old.astype(BF16), qt_scr[h], preferred_element_type=F32)
        acc = intert_scr[h]
        vwt = vwt_scr[h]
        lhs = []
        for s in range(c_seqs):
            own = lane_seq == seq0 + s
            acc = jnp.where(own, readout[s * M_HEAD_DIM:(s + 1) * M_HEAD_DIM], acc)
            lhs.append(jnp.where(own, vwt, 0.0))
        intert_scr[h] = acc
        upd = jnp.dot(jnp.concatenate(lhs, axis=0).astype(BF16), ksb_scr[h], preferred_element_type=F32)
        for s in range(c_seqs):
            r0 = pl.multiple_of((seq0 + s) * SUBLANES, SUBLANES)
            c_ref[s, h] = (dec_scr[pl.ds(r0, 1), h:h + 1] * c0_ref[s, h]
                           + upd[s * M_HEAD_DIM:(s + 1) * M_HEAD_DIM])

    @pl.when(sub == pl.num_programs(1) - 1)
    def _():
        _sample_heads(proj_ref, m0_ref, n0_ref, vec_ref, mixed_ref,
                      q_scr, k_scr, v_scr, o_scr, xc_scr, intert_scr, b_scr, li_scr, valid, R)


def _sample_heads(proj_ref, m0_ref, n0_ref, vec_ref, mixed_ref,
                  q_scr, k_scr, v_scr, o_scr, xc_scr, intert_scr, b_scr, li_scr, valid, R):
    rowi = lax.broadcasted_iota(jnp.int32, (R, R), 0)
    coli = lax.broadcasted_iota(jnp.int32, (R, R), 1)
    eye = coli == rowi
    same_causal = jnp.logical_and(coli <= rowi, (coli >> SUBLANE_SHIFT) == (rowi >> SUBLANE_SHIFT))
    k_scale = M_HEAD_DIM ** -0.5
    mnw = _vec(vec_ref, "mnw")
    mskip = _vec(vec_ref, "mskip")
    b_all = b_scr[...]
    log_i = li_scr[...]
    m_rows = _seg_rows(m0_ref)
    n_rows = _seg_rows(n0_ref)

    for h in range(M_HEADS):
        sl = slice(h * M_HEAD_DIM, (h + 1) * M_HEAD_DIM)
        bcol = b_all[:, h:h + 1]
        licol = log_i[:, h:h + 1]
        mcol = m_rows[:, h:h + 1]
        brow = _row_from_col(bcol, eye)
        lirow = _row_from_col(licol, eye)
        q = q_scr[:, sl]
        ks = k_scr[:, sl] * k_scale
        v = v_scr[:, sl]
        q_b = q.astype(BF16)
        ks_b = ks.astype(BF16)

        log_d = jnp.where(same_causal, bcol - brow + lirow, -jnp.inf)
        log_inter = bcol + mcol
        m_t = jnp.maximum(log_inter, jnp.max(log_d, axis=1, keepdims=True))
        w_intra = jnp.exp(log_d - m_t)
        w_inter = jnp.exp(log_inter - m_t)
        s = lax.dot_general(q_b, ks_b, _NT, preferred_element_type=F32) * w_intra
        num = (jnp.dot(s.astype(BF16), v.astype(BF16), preferred_element_type=F32)
               + w_inter * intert_scr[h].T)
        den = (jnp.sum(s, axis=1, keepdims=True)
               + w_inter * jnp.sum(q * n_rows[:, sl], axis=1, keepdims=True))
        hh = num / jnp.maximum(jnp.abs(den), jnp.exp(-m_t))

        z = proj_ref[:, MIX_WIDTH + h * M_HEAD_DIM:MIX_WIDTH + (h + 1) * M_HEAD_DIM]
        out = _head_output(hh, o_scr[:, sl], xc_scr[:, sl], z, mnw[:, sl], mskip[:, sl])
        mixed_ref[:, sl] = jnp.where(valid, out, 0.0).astype(BF16)


def _sample_mixer(proj, l, row0, nrows, mixed_prev, c_all, row_state, vec, mats, prev_out):
    R = SAMPLE_ROWS
    nsub = (R // SUBLANES) // SAMPLE_C_SEQS
    blk0 = row0 // R
    stream_map = lambda i, j: (blk0 + i, 0)
    c_spec = pl.BlockSpec((None, SAMPLE_C_SEQS, M_HEADS, M_HEAD_DIM, M_HEAD_DIM),
                          lambda i, j: (l, i * nsub + j, 0, 0, 0))

    def state_spec(width, per_row):
        if per_row:
            return pl.BlockSpec((None, R, width), lambda i, j: (l, i, 0))
        return pl.BlockSpec((None, R // SUBLANES, 1, width), lambda i, j: (l, i, 0, 0))

    def state_shape(width, per_row):
        return (DEPTH, nrows, width) if per_row else (DEPTH, nrows // SUBLANES, 1, width)

    state_specs = [state_spec(w, r) for w, r in zip(_SAMPLE_STATE_WIDTHS, _SAMPLE_STATE_PER_ROW)]
    alias_args = [mixed_prev] + (list(prev_out) if prev_out is not None else [])
    first_alias = 2 + len(row_state) + 1 + len(mats)
    aliases = {first_alias + k: k for k in range(len(alias_args))}
    out_shape = ([jax.ShapeDtypeStruct(mixed_prev.shape, BF16), jax.ShapeDtypeStruct(c_all.shape, F32)]
                 + [jax.ShapeDtypeStruct(state_shape(w, r), F32)
                    for w, r in zip(_SAMPLE_STATE_WIDTHS, _SAMPLE_STATE_PER_ROW)])
    scratch = ([pltpu.VMEM((R, M_WIDTH), F32)] * 5
               + [pltpu.VMEM((M_HEADS, M_HEAD_DIM, R), F32), pltpu.VMEM((M_HEADS, M_HEAD_DIM, R), F32),
                  pltpu.VMEM((M_HEADS, M_HEAD_DIM, R), BF16), pltpu.VMEM((M_HEADS, R, M_HEAD_DIM), BF16)]
               + [pltpu.VMEM((R, LANES), F32)] * 3
               + [pltpu.VMEM((R, S_LANES), F32)] * 2)
    return pl.pallas_call(
        functools.partial(_sample_mixer_kernel, rows=R, c_seqs=SAMPLE_C_SEQS, n_alias=len(alias_args)),
        grid=(nrows // R, nsub),
        in_specs=([pl.BlockSpec((R, 2 * MIX_WIDTH), stream_map), c_spec] + state_specs + [_layer_spec(vec, l)]
                  + [_layer_spec(w, l) for w in mats] + [_ANY_SPEC] * len(alias_args)),
        out_specs=[pl.BlockSpec((R, MIX_WIDTH), stream_map), c_spec] + state_specs,
        out_shape=out_shape,
        scratch_shapes=scratch,
        input_output_aliases=aliases,
        compiler_params=pltpu.CompilerParams(
            dimension_semantics=("parallel", "arbitrary"), vmem_limit_bytes=VMEM_LIMIT_BYTES),
        name="sample_mixer",
    )(proj, c_all, *row_state, vec, *mats, *alias_args)


def _prepare_weights(p, s5):
    abre, abim, bbre_t, bbim_t = s5

    def row(v):
        v = v if v.ndim == 3 else v[:, None, :]
        return jnp.pad(v, ((0, 0), (0, 0), (0, _VEC_WIDTH - v.shape[-1])))

    def split_gates(g):
        pad = [(0, 0)] * (g.ndim - 1) + [(0, LANES - M_HEADS)]
        return jnp.concatenate([jnp.pad(g[..., :M_HEADS], pad), jnp.pad(g[..., M_HEADS:], pad)], axis=-1)

    table = {"mcw": p["m_conv_w"], "rcw": p["r_conv_w"], "mcb": p["m_conv_b"], "mnw": p["m_norm_w"],
             "mskip": p["m_skip"], "rcb": p["r_conv_b"], "rba": p["r_ba"], "rbx": p["r_bx"], "rlam": p["r_lam"],
             "sd": p["s_d"], "bglu": p["s_b_glu"], "bif": split_gates(p["m_b_if"]), "abre": abre, "abim": abim,
             "norm_w": p["norm_w"]}
    used = sum(n for _, n, _ in _VEC_LAYOUT.values())
    vec = jnp.concatenate([row(table[k]) for k in _VEC_LAYOUT]
                          + [jnp.zeros((DEPTH, _VEC_ROWS - used, _VEC_WIDTH), F32)], axis=1)

    def block_diag_halves(w):
        nb = R_BLOCKS // 2
        w5 = w.reshape(DEPTH, 2, nb, w.shape[-2], w.shape[-1])
        return jnp.einsum("ab,dhaij->dhaibj", jnp.eye(nb, dtype=F32), w5).reshape(
            DEPTH, 2, nb * w.shape[-2], nb * w.shape[-1]).astype(BF16)

    n_in = S_WIDTH // LANES
    g_in = LANES // S_GROUP
    n_out = 2
    g_out = S_GROUPS // n_out

    def bu_blocks(bb_t):
        bb5 = bb_t.reshape(DEPTH, S_GROUP, n_in, g_in, S_STATE)
        return jnp.einsum("ab,dckbp->dkacbp", jnp.eye(g_in, dtype=F32), bb5).reshape(
            DEPTH, n_in, LANES, g_in * S_STATE)

    def c_blocks(c):
        c5 = c.reshape(DEPTH, n_out, g_out, S_GROUP, S_STATE)
        return jnp.einsum("ab,dmbcp->dmapbc", jnp.eye(g_out, dtype=F32), c5).reshape(
            DEPTH, n_out, g_out * S_STATE, g_out * S_GROUP).astype(BF16)

    mats = {
        "wqk": jnp.concatenate([p["m_wq"], p["m_wk"]], axis=-1).astype(BF16),
        "wvo": jnp.concatenate([p["m_wv"], p["m_wo"]], axis=-1).astype(BF16),
        "wif": split_gates(p["m_w_if"]).astype(BF16),
        "rwa": block_diag_halves(p["r_wa"]), "rwx": block_diag_halves(p["r_wx"]),
        "wbu": jnp.concatenate([bu_blocks(bbre_t), bu_blocks(bbim_t)], axis=-1).astype(BF16),
        "wcre": c_blocks(p["s_c_re"]), "wcim": c_blocks(p["s_c_im"]),
        "wglu": p["s_w_glu"].astype(BF16),
    }
    return vec, p["w_in"].astype(BF16), p["w_out"].astype(BF16), [mats[k] for k in _MATRIX_KEYS]


def _sample_rows_state(n, m, mconv, h, rconv, sre, sim):
    bsz = n.shape[1]

    def per_seq(x):
        return x.reshape(DEPTH, bsz, 1, x.shape[-1])

    def conv_rows(buf):
        padded = jnp.pad(buf, ((0, 0), (0, 0), (SAMPLE_LEAD - (CONV_W - 1), SUBLANES - SAMPLE_LEAD), (0, 0)))
        return padded.reshape(DEPTH, bsz * SUBLANES, buf.shape[-1])

    m_pad = jnp.pad(m, ((0, 0), (0, 0), (0, LANES - M_HEADS)))
    return (per_seq(n.reshape(DEPTH, bsz, M_WIDTH)), per_seq(m_pad), conv_rows(mconv), per_seq(h), conv_rows(rconv),
            per_seq(sre.reshape(DEPTH, bsz, S_LANES)), per_seq(sim.reshape(DEPTH, bsz, S_LANES)))


def _sample_state_from_rows(n, m, xmf, h, xrf, sre, sim):
    bsz = n.shape[1]
    conv_from = SUBLANES - (CONV_W - 1)

    def conv_state(x):
        return x.reshape(DEPTH, bsz, SUBLANES, x.shape[-1])[:, :, conv_from:]

    return (n.reshape(DEPTH, bsz, M_HEADS, M_HEAD_DIM), m.reshape(DEPTH, bsz, LANES)[..., :M_HEADS],
            conv_state(xmf), h.reshape(DEPTH, bsz, R_WIDTH), conv_state(xrf),
            sre.reshape(DEPTH, bsz, S_GROUPS, S_STATE), sim.reshape(DEPTH, bsz, S_GROUPS, S_STATE))


def _prompt_state_from_kernel(c, n, m, mtail, h, rtail, sre, sim):
    bsz = c.shape[1]
    return (c, n, m[..., 0], mtail[:, :, SUBLANES - 1::SUBLANES], h.reshape(DEPTH, bsz, R_WIDTH),
            rtail[:, :, SUBLANES - 1::SUBLANES],
            sre.reshape(DEPTH, bsz, S_GROUPS, S_STATE), sim.reshape(DEPTH, bsz, S_GROUPS, S_STATE))


def _interleave_chunks(x, inverse=False):
    bsz, seq, d = x.shape
    sub_len = PROMPT_CHUNK // SUBLANES
    inner = (sub_len, SUBLANES) if inverse else (SUBLANES, sub_len)
    return x.reshape(bsz, seq // PROMPT_CHUNK, *inner, d).transpose(0, 1, 3, 2, 4).reshape(bsz, seq, d)


def kernel(x_prompt, x_sample, state_mlstm_C, state_mlstm_n, state_mlstm_m, state_mlstm_conv, state_rglru_h, state_rglru_conv, state_s5_re, state_s5_im, norm_w, w_in, w_out, m_conv_w, m_conv_b, m_wq, m_wk, m_wv, m_wo, m_w_if, m_b_if, m_norm_w, m_skip, r_conv_w, r_conv_b, r_wa, r_ba, r_wx, r_bx, r_lam, s_lam_re, s_lam_im, s_b_re, s_b_im, s_c_re, s_c_im, s_d, s_log_step, s_w_glu, s_b_glu, final_norm_w):
    p = dict(norm_w=norm_w, w_in=w_in, w_out=w_out, m_conv_w=m_conv_w, m_conv_b=m_conv_b, m_wq=m_wq, m_wk=m_wk,
             m_wv=m_wv, m_wo=m_wo, m_w_if=m_w_if, m_b_if=m_b_if, m_norm_w=m_norm_w, m_skip=m_skip,
             r_conv_w=r_conv_w, r_conv_b=r_conv_b, r_wa=r_wa, r_ba=r_ba, r_wx=r_wx, r_bx=r_bx, r_lam=r_lam,
             s_c_re=s_c_re, s_c_im=s_c_im, s_d=s_d, s_w_glu=s_w_glu, s_b_glu=s_b_glu)
    s5 = _s5_prep(s_lam_re, s_lam_im, s_log_step, s_b_re, s_b_im)
    vec, w_in_b, w_out_b, mats = _prepare_weights(p, s5)
    final_w = final_norm_w.reshape(1, D_MODEL)

    bsz, seq, _ = x_prompt.shape
    dec_batch, dec_seq, _ = x_sample.shape
    n_prompt = bsz * seq
    n_sample = dec_batch * SUBLANES
    stream = [_interleave_chunks(x_prompt).reshape(n_prompt, D_MODEL),
              jnp.pad(x_sample, ((0, 0), (SAMPLE_LEAD, 0), (0, 0))).reshape(n_sample, D_MODEL)]
    rows_state = _sample_rows_state(state_mlstm_n, state_mlstm_m, state_mlstm_conv, state_rglru_h,
                                    state_rglru_conv, state_s5_re, state_s5_im)
    pr_states = None
    sa_states = None
    for l in range(DEPTH):
        last = l == DEPTH - 1
        proj = _inproj(stream, vec, w_in_b, l)
        mixed, *pr_states = _prompt_mixer(proj, l, bsz, seq, vec, mats, pr_states)
        mixed, *sa_states = _sample_mixer(proj, l, n_prompt, n_sample, mixed, state_mlstm_C, rows_state, vec, mats,
                                          sa_states)
        stream = _outproj(mixed, stream, w_out_b, l, final_w, last, split_rows=n_prompt if last else None)
    y_prompt = _interleave_chunks(stream[0].reshape(bsz, seq, D_MODEL), inverse=True)
    y_sample = stream[1].reshape(dec_batch, SUBLANES, D_MODEL)[:, SAMPLE_LEAD:]
    return (y_prompt, y_sample, *_prompt_state_from_kernel(*pr_states),
            sa_states[0], *_sample_state_from_rows(*sa_states[1:]))
```

```python
import functools

import jax
import jax.numpy as jnp
from jax import lax
from jax.experimental import pallas as pl
from jax.experimental.pallas import tpu as pltpu

F32 = jnp.float32
BF16 = jnp.bfloat16

D_MODEL = 2048
DEPTH = 2
MIX_WIDTH = D_MODEL
M_WIDTH = MIX_WIDTH // 2
R_WIDTH = MIX_WIDTH // 4
S_WIDTH = MIX_WIDTH - M_WIDTH - R_WIDTH
M_HEADS = 8
M_HEAD_DIM = M_WIDTH // M_HEADS
R_BLOCKS = 8
RG_C = 8.0
S_GROUP = 16
S_GROUPS = S_WIDTH // S_GROUP
S_STATE = 64
S_LANES = S_GROUPS * S_STATE
CONV_W = 4
EPS = 1e-6

SUBLANES = 8
SUBLANE_SHIFT = SUBLANES.bit_length() - 1
LANES = 128
VMEM_LIMIT_BYTES = 56 * 1024 * 1024

PROMPT_CHUNK = 256
SAMPLE_ROWS = 128
SAMPLE_C_SEQS = 8
SAMPLE_LEAD = SUBLANES - 4
INPROJ_TM = 512
INPROJ_TN = 1024
OUTPROJ_TM = 512
SCAN_LANE_BLOCK = 256

_NT = (((1,), (1,)), ((), ()))

_VEC_WIDTH = D_MODEL
_VEC_LAYOUT = {}
_row = 0
for _name, _n, _w in (("mcw", CONV_W, M_WIDTH), ("rcw", CONV_W, R_WIDTH), ("mcb", 1, M_WIDTH), ("mnw", 1, M_WIDTH),
                      ("mskip", 1, M_WIDTH), ("rcb", 1, R_WIDTH), ("rba", 1, R_WIDTH), ("rbx", 1, R_WIDTH),
                      ("rlam", 1, R_WIDTH), ("sd", 1, S_WIDTH), ("bglu", 1, S_WIDTH), ("bif", 1, 2 * LANES),
                      ("abre", 1, S_LANES), ("abim", 1, S_LANES), ("norm_w", 1, D_MODEL)):
    _VEC_LAYOUT[_name] = (_row, _n, _w)
    _row += _n
_VEC_ROWS = -(-_row // SUBLANES) * SUBLANES


def _vec(vec_ref, name):
    r0, n, w = _VEC_LAYOUT[name]
    return vec_ref[r0:r0 + n, 0:w]


def _layer_spec(arr, l):
    nd = arr.ndim - 1
    return pl.BlockSpec((None,) + arr.shape[1:], lambda *_: (l,) + (0,) * nd, pipeline_mode=pl.Buffered(1))


_ANY_SPEC = pl.BlockSpec(memory_space=pl.ANY)


def _s5_prep_kernel(lr_ref, li_ref, ls_ref, brt_ref, bit_ref, abre_ref, abim_ref, bbre_ref, bbim_ref):
    lr = lr_ref[0]
    li = li_ref[0]
    dt = jnp.exp(ls_ref[0])
    mag = jnp.exp(lr * dt)
    ang = li * dt
    ab_re = mag * jnp.cos(ang)
    ab_im = mag * jnp.sin(ang)
    den = lr * lr + li * li
    nr = ab_re - 1.0
    f_re = (nr * lr + ab_im * li) / den
    f_im = (ab_im * lr - nr * li) / den
    br = brt_ref[0]
    bi = bit_ref[0]
    abre_ref[0] = ab_re
    abim_ref[0] = ab_im
    bbre_ref[0] = f_re * br - f_im * bi
    bbim_ref[0] = f_re * bi + f_im * br


def _s5_prep(s_lam_re, s_lam_im, s_log_step, s_b_re, s_b_im):
    lr = s_lam_re.reshape(DEPTH, 1, S_LANES)
    li = s_lam_im.reshape(DEPTH, 1, S_LANES)
    ls = jnp.repeat(s_log_step, S_STATE, axis=-1).reshape(DEPTH, 1, S_LANES)
    brt = s_b_re.reshape(DEPTH, S_LANES, S_GROUP).transpose(0, 2, 1)
    bit = s_b_im.reshape(DEPTH, S_LANES, S_GROUP).transpose(0, 2, 1)
    vec = pl.BlockSpec((1, 1, S_LANES), lambda l: (l, 0, 0))
    mat = pl.BlockSpec((1, S_GROUP, S_LANES), lambda l: (l, 0, 0))
    return pl.pallas_call(
        _s5_prep_kernel,
        grid=(DEPTH,),
        in_specs=[vec, vec, vec, mat, mat],
        out_specs=[vec, vec, mat, mat],
        out_shape=[jax.ShapeDtypeStruct((DEPTH, 1, S_LANES), F32),
                   jax.ShapeDtypeStruct((DEPTH, 1, S_LANES), F32),
                   jax.ShapeDtypeStruct((DEPTH, S_GROUP, S_LANES), F32),
                   jax.ShapeDtypeStruct((DEPTH, S_GROUP, S_LANES), F32)],
        name="s5_prep",
    )(lr, li, ls, brt, bit)


def _stream_specs(parts, tm):
    if len(parts) == 1:
        return [pl.BlockSpec((tm, D_MODEL), lambda i: (i, 0))]
    nb0 = parts[0].shape[0] // tm
    return [pl.BlockSpec((tm, D_MODEL), lambda i: (jnp.minimum(i, nb0 - 1), 0)),
            pl.BlockSpec((tm, D_MODEL), lambda i: (jnp.maximum(i - nb0, 0), 0))]


def _read_stream(refs, nb0):
    if len(refs) == 1:
        return refs[0][...]
    return jnp.where(pl.program_id(0) < nb0, refs[0][...], refs[1][...])


def _inproj_kernel(*refs, n_x, nb0):
    x_refs, (vec_ref, w_ref, o_ref) = refs[:n_x], refs[n_x:]
    x = _read_stream(x_refs, nb0)
    ms = jnp.mean(x * x, axis=-1, keepdims=True)
    xn = (x * lax.rsqrt(ms + EPS) * _vec(vec_ref, "norm_w")).astype(BF16)
    for j in range((2 * MIX_WIDTH) // INPROJ_TN):
        cols = slice(j * INPROJ_TN, (j + 1) * INPROJ_TN)
        o_ref[:, cols] = jnp.dot(xn, w_ref[:, cols], preferred_element_type=F32)


def _inproj(x_parts, vec, w_in_b, l):
    n = sum(x.shape[0] for x in x_parts)
    return pl.pallas_call(
        functools.partial(_inproj_kernel, n_x=len(x_parts), nb0=x_parts[0].shape[0] // INPROJ_TM),
        grid=(n // INPROJ_TM,),
        in_specs=_stream_specs(x_parts, INPROJ_TM) + [_layer_spec(vec, l), _layer_spec(w_in_b, l)],
        out_specs=pl.BlockSpec((INPROJ_TM, 2 * MIX_WIDTH), lambda i: (i, 0)),
        out_shape=jax.ShapeDtypeStruct((n, 2 * MIX_WIDTH), F32),
        compiler_params=pltpu.CompilerParams(
            dimension_semantics=("arbitrary",), vmem_limit_bytes=VMEM_LIMIT_BYTES),
        name="inproj",
    )(*x_parts, vec, w_in_b)


def _outproj_kernel(*refs, n_x, nb0, n_out, nb0_out, final):
    mixed_ref, x_refs = refs[0], refs[1:1 + n_x]
    w_ref, fw_ref = refs[1 + n_x:3 + n_x]
    o_refs = refs[3 + n_x:]
    y = _read_stream(x_refs, nb0) + jnp.dot(mixed_ref[...], w_ref[...], preferred_element_type=F32)
    if final:
        ms = jnp.mean(y * y, axis=-1, keepdims=True)
        y = y * lax.rsqrt(ms + EPS) * fw_ref[...]
    if n_out == 1:
        o_refs[0][...] = y
    else:
        @pl.when(pl.program_id(0) < nb0_out)
        def _():
            o_refs[0][...] = y

        @pl.when(pl.program_id(0) >= nb0_out)
        def _():
            o_refs[1][...] = y


def _outproj(mixed, x_parts, w_out_b, l, final_w, final, split_rows=None):
    n = mixed.shape[0]
    out_rows = [n] if split_rows is None else [split_rows, n - split_rows]
    out_parts = [jax.ShapeDtypeStruct((r, D_MODEL), F32) for r in out_rows]
    return pl.pallas_call(
        functools.partial(_outproj_kernel, n_x=len(x_parts), nb0=x_parts[0].shape[0] // OUTPROJ_TM,
                          n_out=len(out_parts), nb0_out=out_rows[0] // OUTPROJ_TM, final=final),
        grid=(n // OUTPROJ_TM,),
        in_specs=([pl.BlockSpec((OUTPROJ_TM, MIX_WIDTH), lambda i: (i, 0))] + _stream_specs(x_parts, OUTPROJ_TM)
                  + [_layer_spec(w_out_b, l), pl.BlockSpec((1, D_MODEL), lambda i: (0, 0))]),
        out_specs=_stream_specs(out_parts, OUTPROJ_TM),
        out_shape=out_parts,
        compiler_params=pltpu.CompilerParams(
            dimension_semantics=("arbitrary",), vmem_limit_bytes=VMEM_LIMIT_BYTES),
        name="outproj",
    )(mixed, *x_parts, w_out_b, final_w)


def _silu(x):
    return x * jax.nn.sigmoid(x)


def _log_sigmoid(x):
    return jnp.minimum(x, 0.0) - jnp.log1p(jnp.exp(-jnp.abs(x)))


def _softplus(x):
    return jnp.maximum(x, 0.0) + jnp.log1p(jnp.exp(-jnp.abs(x)))


def _sublane_pos(shape):
    return lax.broadcasted_iota(jnp.int32, shape, 0) & (SUBLANES - 1)


def _tile_cumsum(x):
    pos = _sublane_pos(x.shape)
    s = 1
    while s < SUBLANES:
        x = x + jnp.where(pos >= s, pltpu.roll(x, s, 0), 0.0)
        s *= 2
    return x


def _tile_scan_real(a, b):
    pos = _sublane_pos(a.shape)
    s = 1
    while s < SUBLANES:
        m = pos >= s
        b = jnp.where(m, a * pltpu.roll(b, s, 0) + b, b)
        if 2 * s < SUBLANES:
            a = jnp.where(m, a * pltpu.roll(a, s, 0), a)
        s *= 2
    return b


def _tile_scan_cplx(sr, si, pr, pi):
    pos = _sublane_pos(sr.shape)
    s = 1
    while s < SUBLANES:
        m = pos >= s
        sr_sh = pltpu.roll(sr, s, 0)
        si_sh = pltpu.roll(si, s, 0)
        sr, si = (jnp.where(m, sr + (pr * sr_sh - pi * si_sh), sr),
                  jnp.where(m, si + (pr * si_sh + pi * sr_sh), si))
        if 2 * s < SUBLANES:
            pr, pi = pr * pr - pi * pi, 2.0 * (pr * pi)
        s *= 2
    return sr, si


def _row_from_col(col, eye):
    return jnp.sum(jnp.where(eye, col, 0.0), axis=0, keepdims=True)


def _mlstm_project(xc, xm, vec_ref, wqk_ref, wvo_ref, wif_ref, q_scr, k_scr, v_scr, o_scr):
    xc_b = xc.astype(BF16)
    xm_b = xm.astype(BF16)
    for h in range(M_HEADS):
        sl = slice(h * M_HEAD_DIM, (h + 1) * M_HEAD_DIM)
        qk = jnp.dot(xc_b[:, sl], wqk_ref[h], preferred_element_type=F32)
        vo = jnp.dot(xm_b[:, sl], wvo_ref[h], preferred_element_type=F32)
        q_scr[:, sl] = qk[:, :M_HEAD_DIM]
        k_scr[:, sl] = qk[:, M_HEAD_DIM:]
        v_scr[:, sl] = vo[:, :M_HEAD_DIM]
        o_scr[:, sl] = vo[:, M_HEAD_DIM:]
    return (jnp.dot(q_scr[...].astype(BF16), wif_ref[0:M_WIDTH, :], preferred_element_type=F32)
            + jnp.dot(k_scr[...].astype(BF16), wif_ref[M_WIDTH:2 * M_WIDTH, :], preferred_element_type=F32)
            + jnp.dot(v_scr[...].astype(BF16), wif_ref[2 * M_WIDTH:3 * M_WIDTH, :], preferred_element_type=F32)
            + _vec(vec_ref, "bif"))


def _head_output(hh, o_pre, xc_h, z_h, mnw_h, mskip_h):
    mu = jnp.mean(hh, axis=1, keepdims=True)
    hc = hh - mu
    var = jnp.mean(hc * hc, axis=1, keepdims=True)
    hn = hc * lax.rsqrt(var + EPS) * mnw_h
    return (jax.nn.sigmoid(o_pre) * hn + mskip_h * xc_h) * _silu(z_h)


def _rglru_coeffs(xcr, vec_ref, rwa_ref, rwx_ref):
    xcr_b = xcr.astype(BF16)
    half = R_WIDTH // 2
    ra_pre = jnp.concatenate(
        [jnp.dot(xcr_b[:, :half], rwa_ref[0], preferred_element_type=F32),
         jnp.dot(xcr_b[:, half:], rwa_ref[1], preferred_element_type=F32)], axis=1) + _vec(vec_ref, "rba")
    rx_pre = jnp.concatenate(
        [jnp.dot(xcr_b[:, :half], rwx_ref[0], preferred_element_type=F32),
         jnp.dot(xcr_b[:, half:], rwx_ref[1], preferred_element_type=F32)], axis=1) + _vec(vec_ref, "rbx")
    log_a = (-RG_C) * jax.nn.sigmoid(ra_pre) * _softplus(-_vec(vec_ref, "rlam"))
    a = jnp.exp(log_a)
    th = jnp.tanh(log_a)
    one_minus_a2 = (-2.0 * th) / (1.0 - th)
    return a, jnp.sqrt(one_minus_a2) * (jax.nn.sigmoid(rx_pre) * xcr)


def _s5_project(u_b, wbu_ref, re_ref, im_ref):
    per_slice = (LANES // S_GROUP) * S_STATE
    for k in range(S_WIDTH // LANES):
        res = jnp.dot(u_b[:, k * LANES:(k + 1) * LANES], wbu_ref[k], preferred_element_type=F32)
        re_ref[:, k * per_slice:(k + 1) * per_slice] = res[:, :per_slice]
        im_ref[:, k * per_slice:(k + 1) * per_slice] = res[:, per_slice:]


def _s5_output(re_ref, im_ref, u, vec_ref, wcre_ref, wcim_ref, wglu_ref):
    nblk = wcre_ref.shape[0]
    k_blk = S_LANES // nblk
    parts = []
    for m in range(nblk):
        ks = slice(m * k_blk, (m + 1) * k_blk)
        parts.append(jnp.dot(re_ref[:, ks].astype(BF16), wcre_ref[m], preferred_element_type=F32)
                     - jnp.dot(im_ref[:, ks].astype(BF16), wcim_ref[m], preferred_element_type=F32))
    y = jnp.concatenate(parts, axis=1) + _vec(vec_ref, "sd") * u
    g = jax.nn.gelu(y)
    return g * jax.nn.sigmoid(jnp.dot(g.astype(BF16), wglu_ref[...], preferred_element_type=F32)
                              + _vec(vec_ref, "bglu"))


_MATRIX_KEYS = ("wqk", "wvo", "wif", "rwa", "rwx", "wbu", "wcre", "wcim", "wglu")


def _tile(x, i):
    return x[i * SUBLANES:(i + 1) * SUBLANES]


def _conv_interleaved(x, tail_ref, w, bias, rows):
    ntiles = rows // SUBLANES
    pos = lax.broadcasted_iota(jnp.int32, (SUBLANES, x.shape[1]), 0)
    prev = tail_ref[...]
    before = [pltpu.roll(jnp.where(pos == SUBLANES - 1, _tile(prev, CONV_W - 1 - d), _tile(x, ntiles - d)), 1, 0)
              for d in range(1, CONV_W)]
    out = w[CONV_W - 1:CONV_W, :] * x + bias
    for j in range(1, CONV_W):
        shifted = jnp.concatenate(before[:j][::-1] + [x[:rows - j * SUBLANES]], axis=0)
        out = out + w[CONV_W - 1 - j:CONV_W - j, :] * shifted
    tail_ref[...] = x[rows - (CONV_W - 1) * SUBLANES:]
    return out


def _cumsum_interleaved(x, rows):
    tiles = [_tile(x, 0)]
    for i in range(1, rows // SUBLANES):
        tiles.append(tiles[-1] + _tile(x, i))
    total = tiles[-1]
    start = _tile_cumsum(total) - total
    return jnp.concatenate([t + start for t in tiles], axis=0)


def _cummax_interleaved(x, rows):
    tiles = [_tile(x, 0)]
    for i in range(1, rows // SUBLANES):
        tiles.append(jnp.maximum(tiles[-1], _tile(x, i)))
    best = tiles[-1]
    pos = lax.broadcasted_iota(jnp.int32, best.shape, 0)
    s = 1
    while s < SUBLANES:
        best = jnp.maximum(best, jnp.where(pos >= s, pltpu.roll(best, s, 0), -jnp.inf))
        s *= 2
    start = jnp.where(pos == 0, -jnp.inf, pltpu.roll(best, 1, 0))
    return jnp.concatenate([jnp.maximum(t, start) for t in tiles], axis=0)


def _scan_real_interleaved(a_ref, b_ref, carry, rows):
    ntiles = rows // SUBLANES
    pos = lax.broadcasted_iota(jnp.int32, (SUBLANES, a_ref.shape[1]), 0)
    h = _tile(b_ref, 0)
    aprod = _tile(a_ref, 0)
    for i in range(1, ntiles):
        a = _tile(a_ref, i)
        h = a * h + _tile(b_ref, i)
        aprod = aprod * a
    g = _tile_scan_real(aprod, h + jnp.where(pos == 0, aprod * carry, 0.0))
    h = jnp.where(pos == 0, carry, pltpu.roll(g, 1, 0))
    for i in range(ntiles):
        h = _tile(a_ref, i) * h + _tile(b_ref, i)
        b_ref[i * SUBLANES:(i + 1) * SUBLANES, :] = h
    return g[SUBLANES - 1:SUBLANES]


def _scan_cplx_interleaved(re_ref, im_ref, p_re, p_im, cre_ref, cim_ref, rows):
    ntiles = rows // SUBLANES
    width = re_ref.shape[1]
    pos = lax.broadcasted_iota(jnp.int32, (SUBLANES, SCAN_LANE_BLOCK), 0)
    for blk in range(width // SCAN_LANE_BLOCK):
        sl = slice(blk * SCAN_LANE_BLOCK, (blk + 1) * SCAN_LANE_BLOCK)
        pr = jnp.broadcast_to(p_re[:, sl], pos.shape)
        pi = jnp.broadcast_to(p_im[:, sl], pos.shape)
        c_r = cre_ref[:, sl]
        c_i = cim_ref[:, sl]
        sr = re_ref[0:SUBLANES, sl]
        si = im_ref[0:SUBLANES, sl]
        for i in range(1, ntiles):
            rs = slice(i * SUBLANES, (i + 1) * SUBLANES)
            sr, si = pr * sr - pi * si + re_ref[rs, sl], pr * si + pi * sr + im_ref[rs, sl]
        qr, qi = pr, pi
        n = 1
        while n < ntiles:
            qr, qi = qr * qr - qi * qi, 2.0 * (qr * qi)
            n *= 2
        gr, gi = _tile_scan_cplx(sr + jnp.where(pos == 0, qr * c_r - qi * c_i, 0.0),
                                 si + jnp.where(pos == 0, qr * c_i + qi * c_r, 0.0), qr, qi)
        sr = jnp.where(pos == 0, c_r, pltpu.roll(gr, 1, 0))
        si = jnp.where(pos == 0, c_i, pltpu.roll(gi, 1, 0))
        for i in range(ntiles):
            rs = slice(i * SUBLANES, (i + 1) * SUBLANES)
            sr, si = pr * sr - pi * si + re_ref[rs, sl], pr * si + pi * sr + im_ref[rs, sl]
            re_ref[rs, sl] = sr
            im_ref[rs, sl] = si
        cre_ref[:, sl] = gr[SUBLANES - 1:SUBLANES]
        cim_ref[:, sl] = gi[SUBLANES - 1:SUBLANES]


def _prompt_mixer_kernel(*refs, rows, n_alias):
    (proj_ref, vec_ref, wqk_ref, wvo_ref, wif_ref, rwa_ref, rwx_ref, wbu_ref, wcre_ref, wcim_ref, wglu_ref) = refs[:11]
    (mixed_ref, c_ref, n_ref, m_ref, mtail_ref, h_ref, rtail_ref, sre_ref, sim_ref,
     q_scr, k_scr, v_scr, o_scr, ra_scr, rb_scr, ure_scr, uim_scr) = refs[11 + n_alias:]
    T = rows

    @pl.when(pl.program_id(1) == 0)
    def _():
        for ref in (c_ref, n_ref, m_ref, mtail_ref, h_ref, rtail_ref, sre_ref, sim_ref):
            ref[...] = jnp.zeros(ref.shape, F32)

    xm = proj_ref[:, 0:M_WIDTH]
    xc = _silu(_conv_interleaved(xm, mtail_ref, _vec(vec_ref, "mcw"), _vec(vec_ref, "mcb"), T))
    gates = _mlstm_project(xc, xm, vec_ref, wqk_ref, wvo_ref, wif_ref, q_scr, k_scr, v_scr, o_scr)
    log_i = gates[:, :LANES]
    b_all = _cumsum_interleaved(_log_sigmoid(gates[:, LANES:]), T)
    a_max = _cummax_interleaved(log_i - b_all, T)
    m_prev = m_ref[...]
    log_inter = b_all + m_prev
    m_t_all = jnp.maximum(log_inter, b_all + a_max)
    w_inter_all = jnp.exp(log_inter - m_t_all)
    floor_all = jnp.exp(-m_t_all)
    b_end = b_all[T - 1:T, :]
    m_new = jnp.maximum(b_end + m_prev, b_end + a_max[T - 1:T, :])
    w_src_all = jnp.exp(b_end - b_all + log_i - m_new)
    decay_all = jnp.exp(b_end + m_prev - m_new)
    m_ref[...] = m_new
    b_t = b_all.T
    li_t = log_i.T

    rowi = lax.broadcasted_iota(jnp.int32, (T, T), 0)
    coli = lax.broadcasted_iota(jnp.int32, (T, T), 1)
    sub_len = T // SUBLANES

    def time_of(r):
        return (r & (SUBLANES - 1)) * sub_len + (r >> SUBLANE_SHIFT)

    causal = time_of(coli) <= time_of(rowi)
    k_scale = M_HEAD_DIM ** -0.5
    mnw = _vec(vec_ref, "mnw")
    mskip = _vec(vec_ref, "mskip")

    for h in range(M_HEADS):
        sl = slice(h * M_HEAD_DIM, (h + 1) * M_HEAD_DIM)
        c_prev = c_ref[h]
        n_prev = n_ref[h:h + 1, :]
        w_inter = w_inter_all[:, h:h + 1]
        w_src = w_src_all[:, h:h + 1]
        decay = decay_all[:, h:h + 1]

        q = q_scr[:, sl]
        ks = k_scr[:, sl] * k_scale
        v = v_scr[:, sl]
        q_b = q.astype(BF16)
        ks_b = ks.astype(BF16)

        log_d = jnp.where(causal, b_all[:, h:h + 1] - b_t[h:h + 1, :] + li_t[h:h + 1, :], -jnp.inf)
        w_intra = jnp.exp(log_d - m_t_all[:, h:h + 1])
        s = lax.dot_general(q_b, ks_b, _NT, preferred_element_type=F32) * w_intra
        inter = lax.dot_general(q_b, c_prev.astype(BF16), _NT, preferred_element_type=F32)
        num = jnp.dot(s.astype(BF16), v.astype(BF16), preferred_element_type=F32) + w_inter * inter
        den = (jnp.sum(s, axis=1, keepdims=True)
               + w_inter * jnp.sum(q * n_prev, axis=1, keepdims=True))
        hh = num / jnp.maximum(jnp.abs(den), floor_all[:, h:h + 1])

        vw_t = (v * w_src).T.astype(BF16)
        c_ref[h] = decay * c_prev + jnp.dot(vw_t, ks_b, preferred_element_type=F32)
        n_ref[h:h + 1, :] = decay * n_prev + jnp.sum(ks * w_src, axis=0, keepdims=True)

        z = proj_ref[:, MIX_WIDTH + h * M_HEAD_DIM:MIX_WIDTH + (h + 1) * M_HEAD_DIM]
        mixed_ref[:, sl] = _head_output(hh, o_scr[:, sl], xc[:, sl], z, mnw[:, sl], mskip[:, sl]).astype(BF16)

    xr = proj_ref[:, M_WIDTH:M_WIDTH + R_WIDTH]
    xcr = _conv_interleaved(xr, rtail_ref, _vec(vec_ref, "rcw"), _vec(vec_ref, "rcb"), T)
    a, bb = _rglru_coeffs(xcr, vec_ref, rwa_ref, rwx_ref)
    ra_scr[...] = a
    rb_scr[...] = bb
    h_ref[...] = _scan_real_interleaved(ra_scr, rb_scr, h_ref[...], T)
    zr = proj_ref[:, MIX_WIDTH + M_WIDTH:MIX_WIDTH + M_WIDTH + R_WIDTH]
    mixed_ref[:, M_WIDTH:M_WIDTH + R_WIDTH] = (rb_scr[...] * _silu(zr)).astype(BF16)

    u = proj_ref[:, M_WIDTH + R_WIDTH:MIX_WIDTH]
    _s5_project(u.astype(BF16), wbu_ref, ure_scr, uim_scr)
    _scan_cplx_interleaved(ure_scr, uim_scr, _vec(vec_ref, "abre"), _vec(vec_ref, "abim"), sre_ref, sim_ref, T)
    glu = _s5_output(ure_scr, uim_scr, u, vec_ref, wcre_ref, wcim_ref, wglu_ref)
    zs = proj_ref[:, MIX_WIDTH + M_WIDTH + R_WIDTH:2 * MIX_WIDTH]
    mixed_ref[:, M_WIDTH + R_WIDTH:MIX_WIDTH] = (glu * _silu(zs)).astype(BF16)


_PROMPT_STATE_SHAPES = ((M_HEADS, M_HEAD_DIM, M_HEAD_DIM), (M_HEADS, M_HEAD_DIM), (1, LANES),
                        ((CONV_W - 1) * SUBLANES, M_WIDTH), (1, R_WIDTH), ((CONV_W - 1) * SUBLANES, R_WIDTH),
                        (1, S_LANES), (1, S_LANES))


def _prompt_mixer(proj, l, bsz, seq, vec, mats, prev_states):
    rows = PROMPT_CHUNK
    nchunk = seq // rows
    aliases = {}
    alias_args = []
    if prev_states is not None:
        alias_args = list(prev_states)
        aliases = {2 + len(mats) + k: 1 + k for k in range(len(alias_args))}

    def state_spec(shape):
        nd = len(shape)
        return pl.BlockSpec((None, None) + shape, lambda b, c: (l, b) + (0,) * nd)

    rows_map = lambda b, c: (b * nchunk + c, 0)
    out_shape = ([jax.ShapeDtypeStruct((proj.shape[0], MIX_WIDTH), BF16)]
                 + [jax.ShapeDtypeStruct((DEPTH, bsz) + s, F32) for s in _PROMPT_STATE_SHAPES])
    scratch = [pltpu.VMEM((rows, M_WIDTH), F32), pltpu.VMEM((rows, M_WIDTH), F32),
               pltpu.VMEM((rows, M_WIDTH), F32), pltpu.VMEM((rows, M_WIDTH), F32),
               pltpu.VMEM((rows, R_WIDTH), F32), pltpu.VMEM((rows, R_WIDTH), F32),
               pltpu.VMEM((rows, S_LANES), F32), pltpu.VMEM((rows, S_LANES), F32)]
    return pl.pallas_call(
        functools.partial(_prompt_mixer_kernel, rows=rows, n_alias=len(alias_args)),
        grid=(bsz, nchunk),
        in_specs=([pl.BlockSpec((rows, 2 * MIX_WIDTH), rows_map), _layer_spec(vec, l)]
                  + [_layer_spec(w, l) for w in mats] + [_ANY_SPEC] * len(alias_args)),
        out_specs=[pl.BlockSpec((rows, MIX_WIDTH), rows_map)] + [state_spec(s) for s in _PROMPT_STATE_SHAPES],
        out_shape=out_shape,
        scratch_shapes=scratch,
        input_output_aliases=aliases,
        compiler_params=pltpu.CompilerParams(
            dimension_semantics=("parallel", "arbitrary"), vmem_limit_bytes=VMEM_LIMIT_BYTES),
        name="prompt_mixer",
    )(proj, vec, *mats, *alias_args)


def _seg_last(x, groups):
    x3 = x.reshape(groups, SUBLANES, x.shape[-1])
    return jnp.broadcast_to(x3[:, SUBLANES - 1:SUBLANES, :], x3.shape).reshape(x.shape)


def _seg_max(x, groups):
    x3 = x.reshape(groups, SUBLANES, x.shape[-1])
    return jnp.broadcast_to(jnp.max(x3, axis=1, keepdims=True), x3.shape).reshape(x.shape)


def _seg_sum(x, groups):
    x3 = x.reshape(groups, SUBLANES, x.shape[-1])
    return jnp.broadcast_to(jnp.sum(x3, axis=1, keepdims=True), x3.shape).reshape(x.shape)


def _seg_rows(state_ref, lanes=slice(None)):
    x = state_ref[:, :, lanes]
    return jnp.broadcast_to(x, (x.shape[0], SUBLANES, x.shape[2])).reshape(x.shape[0] * SUBLANES, x.shape[2])


def _seg_state(x):
    x3 = x.reshape(x.shape[0] // SUBLANES, SUBLANES, x.shape[-1])
    return x3[:, SUBLANES - 1:SUBLANES, :]


def _conv_rolled(xf, w, bias):
    out = w[CONV_W - 1:CONV_W, :] * xf + bias
    for j in range(1, CONV_W):
        out = out + w[CONV_W - 1 - j:CONV_W - j, :] * pltpu.roll(xf, j, 0)
    return out


_SAMPLE_STATE_WIDTHS = (M_WIDTH, LANES, M_WIDTH, R_WIDTH, R_WIDTH, S_LANES, S_LANES)
_SAMPLE_STATE_PER_ROW = (False, False, True, False, True, False, False)


def _sample_mixer_kernel(*refs, rows, c_seqs, n_alias):
    (proj_ref, c0_ref, n0_ref, m0_ref, mtail0_ref, h0_ref, rtail0_ref, sre0_ref, sim0_ref,
     vec_ref, wqk_ref, wvo_ref, wif_ref, rwa_ref, rwx_ref, wbu_ref, wcre_ref, wcim_ref, wglu_ref) = refs[:19]
    (mixed_ref, c_ref, n_ref, m_ref, xmf_ref, h_ref, xrf_ref, sre_ref, sim_ref,
     q_scr, k_scr, v_scr, o_scr, xc_scr, intert_scr, vwt_scr, qt_scr, ksb_scr, dec_scr, b_scr, li_scr,
     ure_scr, uim_scr) = refs[19 + n_alias:]
    R = rows
    G = R // SUBLANES
    sub = pl.program_id(1)
    k_scale = M_HEAD_DIM ** -0.5
    pos = _sublane_pos((R, 1))
    valid = pos >= SAMPLE_LEAD
    first_token = pos == SAMPLE_LEAD

    @pl.when(sub == 0)
    def _():
        xm = jnp.where(valid, proj_ref[:, 0:M_WIDTH], mtail0_ref[...])
        xmf_ref[...] = xm
        xc = _silu(_conv_rolled(xm, _vec(vec_ref, "mcw"), _vec(vec_ref, "mcb")))
        xc_scr[...] = xc
        gates = _mlstm_project(xc, xm, vec_ref, wqk_ref, wvo_ref, wif_ref, q_scr, k_scr, v_scr, o_scr)
        log_i = jnp.where(valid, gates[:, :LANES], -jnp.inf)
        log_f = jnp.where(valid, _log_sigmoid(gates[:, LANES:]), 0.0)
        b_all = _tile_cumsum(log_f)
        b_end = _seg_last(b_all, G)
        m_rows = _seg_rows(m0_ref)
        n_rows = _seg_rows(n0_ref)
        log_src = b_end - b_all + log_i
        m_new = jnp.maximum(b_end + m_rows, _seg_max(log_src, G))
        w_src_all = jnp.exp(log_src - m_new)
        decay_all = jnp.exp(b_end + m_rows - m_new)
        dec_scr[...] = decay_all
        b_scr[...] = b_all
        li_scr[...] = log_i
        m_ref[...] = _seg_state(m_new)
        for h in range(M_HEADS):
            sl = slice(h * M_HEAD_DIM, (h + 1) * M_HEAD_DIM)
            ks = k_scr[:, sl] * k_scale
            w_src = w_src_all[:, h:h + 1]
            n_ref[:, :, sl] = _seg_state(decay_all[:, h:h + 1] * n_rows[:, sl] + _seg_sum(ks * w_src, G))
            vwt_scr[h] = (v_scr[:, sl] * w_src).T
            qt_scr[h] = q_scr[:, sl].T.astype(BF16)
            ksb_scr[h] = ks.astype(BF16)
            intert_scr[h] = jnp.zeros((M_HEAD_DIM, R), F32)

        xr = jnp.where(valid, proj_ref[:, M_WIDTH:M_WIDTH + R_WIDTH], rtail0_ref[...])
        xrf_ref[...] = xr
        xcr = _conv_rolled(xr, _vec(vec_ref, "rcw"), _vec(vec_ref, "rcb"))
        a, bb = _rglru_coeffs(xcr, vec_ref, rwa_ref, rwx_ref)
        hs = _tile_scan_real(a, jnp.where(valid, bb, 0.0) + jnp.where(first_token, a * _seg_rows(h0_ref), 0.0))
        h_ref[...] = _seg_state(hs)
        zr = proj_ref[:, MIX_WIDTH + M_WIDTH:MIX_WIDTH + M_WIDTH + R_WIDTH]
        mixed_ref[:, M_WIDTH:M_WIDTH + R_WIDTH] = jnp.where(valid, hs * _silu(zr), 0.0).astype(BF16)

        u = proj_ref[:, M_WIDTH + R_WIDTH:MIX_WIDTH]
        _s5_project(u.astype(BF16), wbu_ref, ure_scr, uim_scr)
        p_re = _vec(vec_ref, "abre")
        p_im = _vec(vec_ref, "abim")
        for blk in range(S_LANES // SCAN_LANE_BLOCK):
            sl = slice(blk * SCAN_LANE_BLOCK, (blk + 1) * SCAN_LANE_BLOCK)
            pr = jnp.broadcast_to(p_re[:, sl], (R, SCAN_LANE_BLOCK))
            pi = jnp.broadcast_to(p_im[:, sl], (R, SCAN_LANE_BLOCK))
            s0r = _seg_rows(sre0_ref, sl)
            s0i = _seg_rows(sim0_ref, sl)
            sr, si = _tile_scan_cplx(
                jnp.where(valid, ure_scr[:, sl], 0.0) + jnp.where(first_token, pr * s0r - pi * s0i, 0.0),
                jnp.where(valid, uim_scr[:, sl], 0.0) + jnp.where(first_token, pr * s0i + pi * s0r, 0.0), pr, pi)
            ure_scr[:, sl] = sr
            uim_scr[:, sl] = si
            sre_ref[:, :, sl] = _seg_state(sr)
            sim_ref[:, :, sl] = _seg_state(si)
        glu = _s5_output(ure_scr, uim_scr, u, vec_ref, wcre_ref, wcim_ref, wglu_ref)
        zs = proj_ref[:, MIX_WIDTH + M_WIDTH + R_WIDTH:2 * MIX_WIDTH]
        mixed_ref[:, M_WIDTH + R_WIDTH:MIX_WIDTH] = jnp.where(valid, glu * _silu(zs), 0.0).astype(BF16)

    lane_seq = lax.broadcasted_iota(jnp.int32, (M_HEAD_DIM, R), 1) >> SUBLANE_SHIFT
    seq0 = sub * c_seqs
    for h in range(M_HEADS):
        c_old = c0_ref[:, h].reshape(c_seqs * M_HEAD_DIM, M_HEAD_DIM)
        readout = jnp.dot(c_old.astype(BF16), qt_scr[h], preferred_element_type=F32)
        acc = intert_scr[h]
        vwt = vwt_scr[h]
        lhs = []
        for s in range(c_seqs):
            own = lane_seq == seq0 + s
            acc = jnp.where(own, readout[s * M_HEAD_DIM:(s + 1) * M_HEAD_DIM], acc)
            lhs.append(jnp.where(own, vwt, 0.0))
        intert_scr[h] = acc
        upd = jnp.dot(jnp.concatenate(lhs, axis=0).astype(BF16), ksb_scr[h], preferred_element_type=F32)
        for s in range(c_seqs):
            r0 = pl.multiple_of((seq0 + s) * SUBLANES, SUBLANES)
            c_ref[s, h] = (dec_scr[pl.ds(r0, 1), h:h + 1] * c0_ref[s, h]
                           + upd[s * M_HEAD_DIM:(s + 1) * M_HEAD_DIM])

    @pl.when(sub == pl.num_programs(1) - 1)
    def _():
        _sample_heads(proj_ref, m0_ref, n0_ref, vec_ref, mixed_ref,
                      q_scr, k_scr, v_scr, o_scr, xc_scr, intert_scr, b_scr, li_scr, valid, R)


def _sample_heads(proj_ref, m0_ref, n0_ref, vec_ref, mixed_ref,
                  q_scr, k_scr, v_scr, o_scr, xc_scr, intert_scr, b_scr, li_scr, valid, R):
    rowi = lax.broadcasted_iota(jnp.int32, (R, R), 0)
    coli = lax.broadcasted_iota(jnp.int32, (R, R), 1)
    eye = coli == rowi
    same_causal = jnp.logical_and(coli <= rowi, (coli >> SUBLANE_SHIFT) == (rowi >> SUBLANE_SHIFT))
    k_scale = M_HEAD_DIM ** -0.5
    mnw = _vec(vec_ref, "mnw")
    mskip = _vec(vec_ref, "mskip")
    b_all = b_scr[...]
    log_i = li_scr[...]
    m_rows = _seg_rows(m0_ref)
    n_rows = _seg_rows(n0_ref)

    for h in range(M_HEADS):
        sl = slice(h * M_HEAD_DIM, (h + 1) * M_HEAD_DIM)
        bcol = b_all[:, h:h + 1]
        licol = log_i[:, h:h + 1]
        mcol = m_rows[:, h:h + 1]
        brow = _row_from_col(bcol, eye)
        lirow = _row_from_col(licol, eye)
        q = q_scr[:, sl]
        ks = k_scr[:, sl] * k_scale
        v = v_scr[:, sl]
        q_b = q.astype(BF16)
        ks_b = ks.astype(BF16)

        log_d = jnp.where(same_causal, bcol - brow + lirow, -jnp.inf)
        log_inter = bcol + mcol
        m_t = jnp.maximum(log_inter, jnp.max(log_d, axis=1, keepdims=True))
        w_intra = jnp.exp(log_d - m_t)
        w_inter = jnp.exp(log_inter - m_t)
        s = lax.dot_general(q_b, ks_b, _NT, preferred_element_type=F32) * w_intra
        num = (jnp.dot(s.astype(BF16), v.astype(BF16), preferred_element_type=F32)
               + w_inter * intert_scr[h].T)
        den = (jnp.sum(s, axis=1, keepdims=True)
               + w_inter * jnp.sum(q * n_rows[:, sl], axis=1, keepdims=True))
        hh = num / jnp.maximum(jnp.abs(den), jnp.exp(-m_t))

        z = proj_ref[:, MIX_WIDTH + h * M_HEAD_DIM:MIX_WIDTH + (h + 1) * M_HEAD_DIM]
        out = _head_output(hh, o_scr[:, sl], xc_scr[:, sl], z, mnw[:, sl], mskip[:, sl])
        mixed_ref[:, sl] = jnp.where(valid, out, 0.0).astype(BF16)


def _sample_mixer(proj, l, row0, nrows, mixed_prev, c_all, row_state, vec, mats, prev_out):
    R = SAMPLE_ROWS
    nsub = (R // SUBLANES) // SAMPLE_C_SEQS
    blk0 = row0 // R
    stream_map = lambda i, j: (blk0 + i, 0)
    c_spec = pl.BlockSpec((None, SAMPLE_C_SEQS, M_HEADS, M_HEAD_DIM, M_HEAD_DIM),
                          lambda i, j: (l, i * nsub + j, 0, 0, 0))

    def state_spec(width, per_row):
        if per_row:
            return pl.BlockSpec((None, R, width), lambda i, j: (l, i, 0))
        return pl.BlockSpec((None, R // SUBLANES, 1, width), lambda i, j: (l, i, 0, 0))

    def state_shape(width, per_row):
        return (DEPTH, nrows, width) if per_row else (DEPTH, nrows // SUBLANES, 1, width)

    state_specs = [state_spec(w, r) for w, r in zip(_SAMPLE_STATE_WIDTHS, _SAMPLE_STATE_PER_ROW)]
    alias_args = [mixed_prev] + (list(prev_out) if prev_out is not None else [])
    first_alias = 2 + len(row_state) + 1 + len(mats)
    aliases = {first_alias + k: k for k in range(len(alias_args))}
    out_shape = ([jax.ShapeDtypeStruct(mixed_prev.shape, BF16), jax.ShapeDtypeStruct(c_all.shape, F32)]
                 + [jax.ShapeDtypeStruct(state_shape(w, r), F32)
                    for w, r in zip(_SAMPLE_STATE_WIDTHS, _SAMPLE_STATE_PER_ROW)])
    scratch = ([pltpu.VMEM((R, M_WIDTH), F32)] * 5
               + [pltpu.VMEM((M_HEADS, M_HEAD_DIM, R), F32), pltpu.VMEM((M_HEADS, M_HEAD_DIM, R), F32),
                  pltpu.VMEM((M_HEADS, M_HEAD_DIM, R), BF16), pltpu.VMEM((M_HEADS, R, M_HEAD_DIM), BF16)]
               + [pltpu.VMEM((R, LANES), F32)] * 3
               + [pltpu.VMEM((R, S_LANES), F32)] * 2)
    return pl.pallas_call(
        functools.partial(_sample_mixer_kernel, rows=R, c_seqs=SAMPLE_C_SEQS, n_alias=len(alias_args)),
        grid=(nrows // R, nsub),
        in_specs=([pl.BlockSpec((R, 2 * MIX_WIDTH), stream_map), c_spec] + state_specs + [_layer_spec(vec, l)]
                  + [_layer_spec(w, l) for w in mats] + [_ANY_SPEC] * len(alias_args)),
        out_specs=[pl.BlockSpec((R, MIX_WIDTH), stream_map), c_spec] + state_specs,
        out_shape=out_shape,
        scratch_shapes=scratch,
        input_output_aliases=aliases,
        compiler_params=pltpu.CompilerParams(
            dimension_semantics=("parallel", "arbitrary"), vmem_limit_bytes=VMEM_LIMIT_BYTES),
        name="sample_mixer",
    )(proj, c_all, *row_state, vec, *mats, *alias_args)


def _prepare_weights(p, s5):
    abre, abim, bbre_t, bbim_t = s5

    def row(v):
        v = v if v.ndim == 3 else v[:, None, :]
        return jnp.pad(v, ((0, 0), (0, 0), (0, _VEC_WIDTH - v.shape[-1])))

    def split_gates(g):
        pad = [(0, 0)] * (g.ndim - 1) + [(0, LANES - M_HEADS)]
        return jnp.concatenate([jnp.pad(g[..., :M_HEADS], pad), jnp.pad(g[..., M_HEADS:], pad)], axis=-1)

    table = {"mcw": p["m_conv_w"], "rcw": p["r_conv_w"], "mcb": p["m_conv_b"], "mnw": p["m_norm_w"],
             "mskip": p["m_skip"], "rcb": p["r_conv_b"], "rba": p["r_ba"], "rbx": p["r_bx"], "rlam": p["r_lam"],
             "sd": p["s_d"], "bglu": p["s_b_glu"], "bif": split_gates(p["m_b_if"]), "abre": abre, "abim": abim,
             "norm_w": p["norm_w"]}
    used = sum(n for _, n, _ in _VEC_LAYOUT.values())
    vec = jnp.concatenate([row(table[k]) for k in _VEC_LAYOUT]
                          + [jnp.zeros((DEPTH, _VEC_ROWS - used, _VEC_WIDTH), F32)], axis=1)

    def block_diag_halves(w):
        nb = R_BLOCKS // 2
        w5 = w.reshape(DEPTH, 2, nb, w.shape[-2], w.shape[-1])
        return jnp.einsum("ab,dhaij->dhaibj", jnp.eye(nb, dtype=F32), w5).reshape(
            DEPTH, 2, nb * w.shape[-2], nb * w.shape[-1]).astype(BF16)

    n_in = S_WIDTH // LANES
    g_in = LANES // S_GROUP
    n_out = 2
    g_out = S_GROUPS // n_out

    def bu_blocks(bb_t):
        bb5 = bb_t.reshape(DEPTH, S_GROUP, n_in, g_in, S_STATE)
        return jnp.einsum("ab,dckbp->dkacbp", jnp.eye(g_in, dtype=F32), bb5).reshape(
            DEPTH, n_in, LANES, g_in * S_STATE)

    def c_blocks(c):
        c5 = c.reshape(DEPTH, n_out, g_out, S_GROUP, S_STATE)
        return jnp.einsum("ab,dmbcp->dmapbc", jnp.eye(g_out, dtype=F32), c5).reshape(
            DEPTH, n_out, g_out * S_STATE, g_out * S_GROUP).astype(BF16)

    mats = {
        "wqk": jnp.concatenate([p["m_wq"], p["m_wk"]], axis=-1).astype(BF16),
        "wvo": jnp.concatenate([p["m_wv"], p["m_wo"]], axis=-1).astype(BF16),
        "wif": split_gates(p["m_w_if"]).astype(BF16),
        "rwa": block_diag_halves(p["r_wa"]), "rwx": block_diag_halves(p["r_wx"]),
        "wbu": jnp.concatenate([bu_blocks(bbre_t), bu_blocks(bbim_t)], axis=-1).astype(BF16),
        "wcre": c_blocks(p["s_c_re"]), "wcim": c_blocks(p["s_c_im"]),
        "wglu": p["s_w_glu"].astype(BF16),
    }
    return vec, p["w_in"].astype(BF16), p["w_out"].astype(BF16), [mats[k] for k in _MATRIX_KEYS]


def _sample_rows_state(n, m, mconv, h, rconv, sre, sim):
    bsz = n.shape[1]

    def per_seq(x):
        return x.reshape(DEPTH, bsz, 1, x.shape[-1])

    def conv_rows(buf):
        padded = jnp.pad(buf, ((0, 0), (0, 0), (SAMPLE_LEAD - (CONV_W - 1), SUBLANES - SAMPLE_LEAD), (0, 0)))
        return padded.reshape(DEPTH, bsz * SUBLANES, buf.shape[-1])

    m_pad = jnp.pad(m, ((0, 0), (0, 0), (0, LANES - M_HEADS)))
    return (per_seq(n.reshape(DEPTH, bsz, M_WIDTH)), per_seq(m_pad), conv_rows(mconv), per_seq(h), conv_rows(rconv),
            per_seq(sre.reshape(DEPTH, bsz, S_LANES)), per_seq(sim.reshape(DEPTH, bsz, S_LANES)))


def _sample_state_from_rows(n, m, xmf, h, xrf, sre, sim):
    bsz = n.shape[1]
    conv_from = SUBLANES - (CONV_W - 1)

    def conv_state(x):
        return x.reshape(DEPTH, bsz, SUBLANES, x.shape[-1])[:, :, conv_from:]

    return (n.reshape(DEPTH, bsz, M_HEADS, M_HEAD_DIM), m.reshape(DEPTH, bsz, LANES)[..., :M_HEADS],
            conv_state(xmf), h.reshape(DEPTH, bsz, R_WIDTH), conv_state(xrf),
            sre.reshape(DEPTH, bsz, S_GROUPS, S_STATE), sim.reshape(DEPTH, bsz, S_GROUPS, S_STATE))


def _prompt_state_from_kernel(c, n, m, mtail, h, rtail, sre, sim):
    bsz = c.shape[1]
    return (c, n, m[:, :, 0, :M_HEADS], mtail[:, :, SUBLANES - 1::SUBLANES], h.reshape(DEPTH, bsz, R_WIDTH),
            rtail[:, :, SUBLANES - 1::SUBLANES],
            sre.reshape(DEPTH, bsz, S_GROUPS, S_STATE), sim.reshape(DEPTH, bsz, S_GROUPS, S_STATE))


def _interleave_chunks(x, inverse=False):
    bsz, seq, d = x.shape
    sub_len = PROMPT_CHUNK // SUBLANES
    inner = (sub_len, SUBLANES) if inverse else (SUBLANES, sub_len)
    return x.reshape(bsz, seq // PROMPT_CHUNK, *inner, d).transpose(0, 1, 3, 2, 4).reshape(bsz, seq, d)


def kernel(x_prompt, x_sample, state_mlstm_C, state_mlstm_n, state_mlstm_m, state_mlstm_conv, state_rglru_h, state_rglru_conv, state_s5_re, state_s5_im, norm_w, w_in, w_out, m_conv_w, m_conv_b, m_wq, m_wk, m_wv, m_wo, m_w_if, m_b_if, m_norm_w, m_skip, r_conv_w, r_conv_b, r_wa, r_ba, r_wx, r_bx, r_lam, s_lam_re, s_lam_im, s_b_re, s_b_im, s_c_re, s_c_im, s_d, s_log_step, s_w_glu, s_b_glu, final_norm_w):
    p = dict(norm_w=norm_w, w_in=w_in, w_out=w_out, m_conv_w=m_conv_w, m_conv_b=m_conv_b, m_wq=m_wq, m_wk=m_wk,
             m_wv=m_wv, m_wo=m_wo, m_w_if=m_w_if, m_b_if=m_b_if, m_norm_w=m_norm_w, m_skip=m_skip,
             r_conv_w=r_conv_w, r_conv_b=r_conv_b, r_wa=r_wa, r_ba=r_ba, r_wx=r_wx, r_bx=r_bx, r_lam=r_lam,
             s_c_re=s_c_re, s_c_im=s_c_im, s_d=s_d, s_w_glu=s_w_glu, s_b_glu=s_b_glu)
    s5 = _s5_prep(s_lam_re, s_lam_im, s_log_step, s_b_re, s_b_im)
    vec, w_in_b, w_out_b, mats = _prepare_weights(p, s5)
    final_w = final_norm_w.reshape(1, D_MODEL)

    bsz, seq, _ = x_prompt.shape
    dec_batch, dec_seq, _ = x_sample.shape
    n_prompt = bsz * seq
    n_sample = dec_batch * SUBLANES
    stream = [_interleave_chunks(x_prompt).reshape(n_prompt, D_MODEL),
              jnp.pad(x_sample, ((0, 0), (SAMPLE_LEAD, 0), (0, 0))).reshape(n_sample, D_MODEL)]
    rows_state = _sample_rows_state(state_mlstm_n, state_mlstm_m, state_mlstm_conv, state_rglru_h,
                                    state_rglru_conv, state_s5_re, state_s5_im)
    pr_states = None
    sa_states = None
    for l in range(DEPTH):
        last = l == DEPTH - 1
        proj = _inproj(stream, vec, w_in_b, l)
        mixed, *pr_states = _prompt_mixer(proj, l, bsz, seq, vec, mats, pr_states)
        mixed, *sa_states = _sample_mixer(proj, l, n_prompt, n_sample, mixed, state_mlstm_C, rows_state, vec, mats,
                                          sa_states)
        stream = _outproj(mixed, stream, w_out_b, l, final_w, last, split_rows=n_prompt if last else None)
    y_prompt = _interleave_chunks(stream[0].reshape(bsz, seq, D_MODEL), inverse=True)
    y_sample = stream[1].reshape(dec_batch, SUBLANES, D_MODEL)[:, SAMPLE_LEAD:]
    return (y_prompt, y_sample, *_prompt_state_from_kernel(*pr_states),
            sa_states[0], *_sample_state_from_rows(*sa_states[1:]))
```

```python
import functools

import jax
import jax.numpy as jnp
from jax import lax
from jax.experimental import pallas as pl
from jax.experimental.pallas import tpu as pltpu

F32 = jnp.float32
BF16 = jnp.bfloat16

D_MODEL = 2048
DEPTH = 2
MIX_WIDTH = D_MODEL
M_WIDTH = MIX_WIDTH // 2
R_WIDTH = MIX_WIDTH // 4
S_WIDTH = MIX_WIDTH - M_WIDTH - R_WIDTH
M_HEADS = 8
M_HEAD_DIM = M_WIDTH // M_HEADS
R_BLOCKS = 8
RG_C = 8.0
S_GROUP = 16
S_GROUPS = S_WIDTH // S_GROUP
S_STATE = 64
S_LANES = S_GROUPS * S_STATE
CONV_W = 4
EPS = 1e-6

SUBLANES = 8
SUBLANE_SHIFT = SUBLANES.bit_length() - 1
LANES = 128
VMEM_LIMIT_BYTES = 56 * 1024 * 1024

PROMPT_CHUNK = 256
SAMPLE_ROWS = 128
SAMPLE_C_SEQS = 8
SAMPLE_LEAD = SUBLANES - 4
INPROJ_TM = 512
INPROJ_TN = 1024
OUTPROJ_TM = 512
SCAN_LANE_BLOCK = 256

_NT = (((1,), (1,)), ((), ()))

_VEC_WIDTH = D_MODEL
_VEC_LAYOUT = {}
_row = 0
for _name, _n, _w in (("mcw", CONV_W, M_WIDTH), ("rcw", CONV_W, R_WIDTH), ("mcb", 1, M_WIDTH), ("mnw", 1, M_WIDTH),
                      ("mskip", 1, M_WIDTH), ("rcb", 1, R_WIDTH), ("rba", 1, R_WIDTH), ("rbx", 1, R_WIDTH),
                      ("rlam", 1, R_WIDTH), ("sd", 1, S_WIDTH), ("bglu", 1, S_WIDTH), ("bif", 1, 2 * LANES),
                      ("abre", 1, S_LANES), ("abim", 1, S_LANES), ("norm_w", 1, D_MODEL)):
    _VEC_LAYOUT[_name] = (_row, _n, _w)
    _row += _n
_VEC_ROWS = -(-_row // SUBLANES) * SUBLANES


def _vec(vec_ref, name):
    r0, n, w = _VEC_LAYOUT[name]
    return vec_ref[r0:r0 + n, 0:w]


def _layer_spec(arr, l):
    nd = arr.ndim - 1
    return pl.BlockSpec((None,) + arr.shape[1:], lambda *_: (l,) + (0,) * nd, pipeline_mode=pl.Buffered(1))


_ANY_SPEC = pl.BlockSpec(memory_space=pl.ANY)


def _s5_prep_kernel(lr_ref, li_ref, ls_ref, brt_ref, bit_ref, abre_ref, abim_ref, bbre_ref, bbim_ref):
    lr = lr_ref[0]
    li = li_ref[0]
    dt = jnp.exp(ls_ref[0])
    mag = jnp.exp(lr * dt)
    ang = li * dt
    ab_re = mag * jnp.cos(ang)
    ab_im = mag * jnp.sin(ang)
    den = lr * lr + li * li
    nr = ab_re - 1.0
    f_re = (nr * lr + ab_im * li) / den
    f_im = (ab_im * lr - nr * li) / den
    br = brt_ref[0]
    bi = bit_ref[0]
    abre_ref[0] = ab_re
    abim_ref[0] = ab_im
    bbre_ref[0] = f_re * br - f_im * bi
    bbim_ref[0] = f_re * bi + f_im * br


def _s5_prep(s_lam_re, s_lam_im, s_log_step, s_b_re, s_b_im):
    lr = s_lam_re.reshape(DEPTH, 1, S_LANES)
    li = s_lam_im.reshape(DEPTH, 1, S_LANES)
    ls = jnp.repeat(s_log_step, S_STATE, axis=-1).reshape(DEPTH, 1, S_LANES)
    brt = s_b_re.reshape(DEPTH, S_LANES, S_GROUP).transpose(0, 2, 1)
    bit = s_b_im.reshape(DEPTH, S_LANES, S_GROUP).transpose(0, 2, 1)
    vec = pl.BlockSpec((1, 1, S_LANES), lambda l: (l, 0, 0))
    mat = pl.BlockSpec((1, S_GROUP, S_LANES), lambda l: (l, 0, 0))
    return pl.pallas_call(
        _s5_prep_kernel,
        grid=(DEPTH,),
        in_specs=[vec, vec, vec, mat, mat],
        out_specs=[vec, vec, mat, mat],
        out_shape=[jax.ShapeDtypeStruct((DEPTH, 1, S_LANES), F32),
                   jax.ShapeDtypeStruct((DEPTH, 1, S_LANES), F32),
                   jax.ShapeDtypeStruct((DEPTH, S_GROUP, S_LANES), F32),
                   jax.ShapeDtypeStruct((DEPTH, S_GROUP, S_LANES), F32)],
        name="s5_prep",
    )(lr, li, ls, brt, bit)


def _stream_specs(parts, tm):
    if len(parts) == 1:
        return [pl.BlockSpec((tm, D_MODEL), lambda i: (i, 0))]
    nb0 = parts[0].shape[0] // tm
    return [pl.BlockSpec((tm, D_MODEL), lambda i: (jnp.minimum(i, nb0 - 1), 0)),
            pl.BlockSpec((tm, D_MODEL), lambda i: (jnp.maximum(i - nb0, 0), 0))]


def _read_stream(refs, nb0):
    if len(refs) == 1:
        return refs[0][...]
    return jnp.where(pl.program_id(0) < nb0, refs[0][...], refs[1][...])


def _rmsnorm_bf16(x, vec_ref):
    ms = jnp.mean(x * x, axis=-1, keepdims=True)
    return (x * lax.rsqrt(ms + EPS) * _vec(vec_ref, "norm_w")).astype(BF16)


def _inproj_kernel(x_ref, vec_ref, w_ref, o_ref):
    xn = _rmsnorm_bf16(x_ref[...], vec_ref)
    for j in range((2 * MIX_WIDTH) // INPROJ_TN):
        cols = slice(j * INPROJ_TN, (j + 1) * INPROJ_TN)
        o_ref[:, cols] = jnp.dot(xn, w_ref[:, cols], preferred_element_type=F32)


def _inproj(x, row0, nrows, vec, w_in_b, l):
    blk0 = row0 // INPROJ_TM
    return pl.pallas_call(
        _inproj_kernel,
        grid=(nrows // INPROJ_TM,),
        in_specs=[pl.BlockSpec((INPROJ_TM, D_MODEL), lambda i: (blk0 + i, 0)),
                  _layer_spec(vec, l), _layer_spec(w_in_b, l)],
        out_specs=pl.BlockSpec((INPROJ_TM, 2 * MIX_WIDTH), lambda i: (i, 0)),
        out_shape=jax.ShapeDtypeStruct((nrows, 2 * MIX_WIDTH), F32),
        compiler_params=pltpu.CompilerParams(
            dimension_semantics=("parallel",), vmem_limit_bytes=VMEM_LIMIT_BYTES),
        name="inproj",
    )(x, vec, w_in_b)


def _outproj_kernel(*refs, n_x, nb0, n_out, nb0_out, final):
    mixed_ref, x_refs = refs[0], refs[1:1 + n_x]
    w_ref, fw_ref = refs[1 + n_x:3 + n_x]
    o_refs = refs[3 + n_x:]
    y = _read_stream(x_refs, nb0) + jnp.dot(mixed_ref[...], w_ref[...], preferred_element_type=F32)
    if final:
        ms = jnp.mean(y * y, axis=-1, keepdims=True)
        y = y * lax.rsqrt(ms + EPS) * fw_ref[...]
    if n_out == 1:
        o_refs[0][...] = y
    else:
        @pl.when(pl.program_id(0) < nb0_out)
        def _():
            o_refs[0][...] = y

        @pl.when(pl.program_id(0) >= nb0_out)
        def _():
            o_refs[1][...] = y


def _outproj(mixed, x_parts, w_out_b, l, final_w, final, split_rows=None):
    n = mixed.shape[0]
    out_rows = [n] if split_rows is None else [split_rows, n - split_rows]
    out_parts = [jax.ShapeDtypeStruct((r, D_MODEL), F32) for r in out_rows]
    return pl.pallas_call(
        functools.partial(_outproj_kernel, n_x=len(x_parts), nb0=x_parts[0].shape[0] // OUTPROJ_TM,
                          n_out=len(out_parts), nb0_out=out_rows[0] // OUTPROJ_TM, final=final),
        grid=(n // OUTPROJ_TM,),
        in_specs=([pl.BlockSpec((OUTPROJ_TM, MIX_WIDTH), lambda i: (i, 0))] + _stream_specs(x_parts, OUTPROJ_TM)
                  + [_layer_spec(w_out_b, l), pl.BlockSpec((1, D_MODEL), lambda i: (0, 0))]),
        out_specs=_stream_specs(out_parts, OUTPROJ_TM),
        out_shape=out_parts,
        compiler_params=pltpu.CompilerParams(
            dimension_semantics=("arbitrary",), vmem_limit_bytes=VMEM_LIMIT_BYTES),
        name="outproj",
    )(mixed, *x_parts, w_out_b, final_w)


def _silu(x):
    return x * jax.nn.sigmoid(x)


def _log_sigmoid(x):
    return jnp.minimum(x, 0.0) - jnp.log1p(jnp.exp(-jnp.abs(x)))


def _softplus(x):
    return jnp.maximum(x, 0.0) + jnp.log1p(jnp.exp(-jnp.abs(x)))


def _sublane_pos(shape):
    return lax.broadcasted_iota(jnp.int32, shape, 0) & (SUBLANES - 1)


def _tile_cumsum(x):
    pos = _sublane_pos(x.shape)
    s = 1
    while s < SUBLANES:
        x = x + jnp.where(pos >= s, pltpu.roll(x, s, 0), 0.0)
        s *= 2
    return x


def _tile_scan_real(a, b):
    pos = _sublane_pos(a.shape)
    s = 1
    while s < SUBLANES:
        m = pos >= s
        b = jnp.where(m, a * pltpu.roll(b, s, 0) + b, b)
        if 2 * s < SUBLANES:
            a = jnp.where(m, a * pltpu.roll(a, s, 0), a)
        s *= 2
    return b


def _tile_scan_cplx(sr, si, pr, pi):
    pos = _sublane_pos(sr.shape)
    s = 1
    while s < SUBLANES:
        m = pos >= s
        sr_sh = pltpu.roll(sr, s, 0)
        si_sh = pltpu.roll(si, s, 0)
        sr, si = (jnp.where(m, sr + (pr * sr_sh - pi * si_sh), sr),
                  jnp.where(m, si + (pr * si_sh + pi * sr_sh), si))
        if 2 * s < SUBLANES:
            pr, pi = pr * pr - pi * pi, 2.0 * (pr * pi)
        s *= 2
    return sr, si


def _row_from_col(col, eye):
    return jnp.sum(jnp.where(eye, col, 0.0), axis=0, keepdims=True)


def _mlstm_project(xc, xm, vec_ref, wqk_ref, wvo_ref, wif_ref, q_scr, k_scr, v_scr, o_scr):
    xc_b = xc.astype(BF16)
    xm_b = xm.astype(BF16)
    for h in range(M_HEADS):
        sl = slice(h * M_HEAD_DIM, (h + 1) * M_HEAD_DIM)
        qk = jnp.dot(xc_b[:, sl], wqk_ref[h], preferred_element_type=F32)
        vo = jnp.dot(xm_b[:, sl], wvo_ref[h], preferred_element_type=F32)
        q_scr[:, sl] = qk[:, :M_HEAD_DIM]
        k_scr[:, sl] = qk[:, M_HEAD_DIM:]
        v_scr[:, sl] = vo[:, :M_HEAD_DIM]
        o_scr[:, sl] = vo[:, M_HEAD_DIM:]
    return (jnp.dot(q_scr[...].astype(BF16), wif_ref[0:M_WIDTH, :], preferred_element_type=F32)
            + jnp.dot(k_scr[...].astype(BF16), wif_ref[M_WIDTH:2 * M_WIDTH, :], preferred_element_type=F32)
            + jnp.dot(v_scr[...].astype(BF16), wif_ref[2 * M_WIDTH:3 * M_WIDTH, :], preferred_element_type=F32)
            + _vec(vec_ref, "bif"))


def _head_output(hh, o_pre, xc_h, z_h, mnw_h, mskip_h):
    mu = jnp.mean(hh, axis=1, keepdims=True)
    hc = hh - mu
    var = jnp.mean(hc * hc, axis=1, keepdims=True)
    hn = hc * lax.rsqrt(var + EPS) * mnw_h
    return (jax.nn.sigmoid(o_pre) * hn + mskip_h * xc_h) * _silu(z_h)


def _rglru_coeffs(xcr, vec_ref, rwa_ref, rwx_ref):
    xcr_b = xcr.astype(BF16)
    half = R_WIDTH // 2
    ra_pre = jnp.concatenate(
        [jnp.dot(xcr_b[:, :half], rwa_ref[0], preferred_element_type=F32),
         jnp.dot(xcr_b[:, half:], rwa_ref[1], preferred_element_type=F32)], axis=1) + _vec(vec_ref, "rba")
    rx_pre = jnp.concatenate(
        [jnp.dot(xcr_b[:, :half], rwx_ref[0], preferred_element_type=F32),
         jnp.dot(xcr_b[:, half:], rwx_ref[1], preferred_element_type=F32)], axis=1) + _vec(vec_ref, "rbx")
    log_a = (-RG_C) * jax.nn.sigmoid(ra_pre) * _softplus(-_vec(vec_ref, "rlam"))
    a = jnp.exp(log_a)
    th = jnp.tanh(log_a)
    one_minus_a2 = (-2.0 * th) / (1.0 - th)
    return a, jnp.sqrt(one_minus_a2) * (jax.nn.sigmoid(rx_pre) * xcr)


def _s5_project(u_b, wbu_ref, re_ref, im_ref):
    per_slice = (LANES // S_GROUP) * S_STATE
    for k in range(S_WIDTH // LANES):
        res = jnp.dot(u_b[:, k * LANES:(k + 1) * LANES], wbu_ref[k], preferred_element_type=F32)
        re_ref[:, k * per_slice:(k + 1) * per_slice] = res[:, :per_slice]
        im_ref[:, k * per_slice:(k + 1) * per_slice] = res[:, per_slice:]


def _s5_output(re_ref, im_ref, u, vec_ref, wcre_ref, wcim_ref, wglu_ref):
    nblk = wcre_ref.shape[0]
    k_blk = S_LANES // nblk
    parts = []
    for m in range(nblk):
        ks = slice(m * k_blk, (m + 1) * k_blk)
        parts.append(jnp.dot(re_ref[:, ks].astype(BF16), wcre_ref[m], preferred_element_type=F32)
                     - jnp.dot(im_ref[:, ks].astype(BF16), wcim_ref[m], preferred_element_type=F32))
    y = jnp.concatenate(parts, axis=1) + _vec(vec_ref, "sd") * u
    g = jax.nn.gelu(y)
    return g * jax.nn.sigmoid(jnp.dot(g.astype(BF16), wglu_ref[...], preferred_element_type=F32)
                              + _vec(vec_ref, "bglu"))


_MATRIX_KEYS = ("wqk", "wvo", "wif", "rwa", "rwx", "wbu", "wcre", "wcim", "wglu")


def _tile(x, i):
    return x[i * SUBLANES:(i + 1) * SUBLANES]


def _conv_interleaved(x, tail_ref, w, bias, rows):
    ntiles = rows // SUBLANES
    pos = lax.broadcasted_iota(jnp.int32, (SUBLANES, x.shape[1]), 0)
    prev = tail_ref[...]
    before = [pltpu.roll(jnp.where(pos == SUBLANES - 1, _tile(prev, CONV_W - 1 - d), _tile(x, ntiles - d)), 1, 0)
              for d in range(1, CONV_W)]
    out = w[CONV_W - 1:CONV_W, :] * x + bias
    for j in range(1, CONV_W):
        shifted = jnp.concatenate(before[:j][::-1] + [x[:rows - j * SUBLANES]], axis=0)
        out = out + w[CONV_W - 1 - j:CONV_W - j, :] * shifted
    tail_ref[...] = x[rows - (CONV_W - 1) * SUBLANES:]
    return out


def _cumsum_interleaved(x, rows):
    tiles = [_tile(x, 0)]
    for i in range(1, rows // SUBLANES):
        tiles.append(tiles[-1] + _tile(x, i))
    total = tiles[-1]
    start = _tile_cumsum(total) - total
    return jnp.concatenate([t + start for t in tiles], axis=0)


def _cummax_interleaved(x, rows):
    tiles = [_tile(x, 0)]
    for i in range(1, rows // SUBLANES):
        tiles.append(jnp.maximum(tiles[-1], _tile(x, i)))
    best = tiles[-1]
    pos = lax.broadcasted_iota(jnp.int32, best.shape, 0)
    s = 1
    while s < SUBLANES:
        best = jnp.maximum(best, jnp.where(pos >= s, pltpu.roll(best, s, 0), -jnp.inf))
        s *= 2
    start = jnp.where(pos == 0, -jnp.inf, pltpu.roll(best, 1, 0))
    return jnp.concatenate([jnp.maximum(t, start) for t in tiles], axis=0)


def _scan_real_interleaved(a_ref, b_ref, carry, rows):
    ntiles = rows // SUBLANES
    pos = lax.broadcasted_iota(jnp.int32, (SUBLANES, a_ref.shape[1]), 0)
    h = _tile(b_ref, 0)
    aprod = _tile(a_ref, 0)
    for i in range(1, ntiles):
        a = _tile(a_ref, i)
        h = a * h + _tile(b_ref, i)
        aprod = aprod * a
    g = _tile_scan_real(aprod, h + jnp.where(pos == 0, aprod * carry, 0.0))
    h = jnp.where(pos == 0, carry, pltpu.roll(g, 1, 0))
    for i in range(ntiles):
        h = _tile(a_ref, i) * h + _tile(b_ref, i)
        b_ref[i * SUBLANES:(i + 1) * SUBLANES, :] = h
    return g[SUBLANES - 1:SUBLANES]


def _scan_cplx_interleaved(re_ref, im_ref, p_re, p_im, cre_ref, cim_ref, rows):
    ntiles = rows // SUBLANES
    width = re_ref.shape[1]
    pos = lax.broadcasted_iota(jnp.int32, (SUBLANES, SCAN_LANE_BLOCK), 0)
    for blk in range(width // SCAN_LANE_BLOCK):
        sl = slice(blk * SCAN_LANE_BLOCK, (blk + 1) * SCAN_LANE_BLOCK)
        pr = jnp.broadcast_to(p_re[:, sl], pos.shape)
        pi = jnp.broadcast_to(p_im[:, sl], pos.shape)
        c_r = cre_ref[:, sl]
        c_i = cim_ref[:, sl]
        sr = re_ref[0:SUBLANES, sl]
        si = im_ref[0:SUBLANES, sl]
        for i in range(1, ntiles):
            rs = slice(i * SUBLANES, (i + 1) * SUBLANES)
            sr, si = pr * sr - pi * si + re_ref[rs, sl], pr * si + pi * sr + im_ref[rs, sl]
        qr, qi = pr, pi
        n = 1
        while n < ntiles:
            qr, qi = qr * qr - qi * qi, 2.0 * (qr * qi)
            n *= 2
        gr, gi = _tile_scan_cplx(sr + jnp.where(pos == 0, qr * c_r - qi * c_i, 0.0),
                                 si + jnp.where(pos == 0, qr * c_i + qi * c_r, 0.0), qr, qi)
        sr = jnp.where(pos == 0, c_r, pltpu.roll(gr, 1, 0))
        si = jnp.where(pos == 0, c_i, pltpu.roll(gi, 1, 0))
        for i in range(ntiles):
            rs = slice(i * SUBLANES, (i + 1) * SUBLANES)
            sr, si = pr * sr - pi * si + re_ref[rs, sl], pr * si + pi * sr + im_ref[rs, sl]
            re_ref[rs, sl] = sr
            im_ref[rs, sl] = si
        cre_ref[:, sl] = gr[SUBLANES - 1:SUBLANES]
        cim_ref[:, sl] = gi[SUBLANES - 1:SUBLANES]


def _prompt_mixer_kernel(*refs, rows, nchunk, n_alias):
    (xnext_ref, xfirst_ref, vec_ref, win_ref,
     wqk_ref, wvo_ref, wif_ref, rwa_ref, rwx_ref, wbu_ref, wcre_ref, wcim_ref, wglu_ref) = refs[:13]
    (mixed_ref, c_ref, n_ref, m_ref, mtail_ref, h_ref, rtail_ref, sre_ref, sim_ref,
     q_scr, k_scr, v_scr, o_scr, ra_scr, rb_scr, ure_scr, uim_scr, proj_scr, xn_scr) = refs[13 + n_alias:]
    T = rows
    step = pl.program_id(0)
    slot = lax.rem(step, 2)
    proj_ref = proj_scr.at[slot]
    next_proj_ref = proj_scr.at[1 - slot]
    slab = (2 * MIX_WIDTH) // M_HEADS

    @pl.when(step == 0)
    def _():
        xn = _rmsnorm_bf16(xfirst_ref[...], vec_ref)
        for j in range(M_HEADS):
            cols = slice(j * slab, (j + 1) * slab)
            proj_scr[0, :, cols] = jnp.dot(xn, win_ref[:, cols], preferred_element_type=F32)

    @pl.when(lax.rem(step, nchunk) == 0)
    def _():
        for ref in (c_ref, n_ref, m_ref, mtail_ref, h_ref, rtail_ref, sre_ref, sim_ref):
            ref[...] = jnp.zeros(ref.shape, F32)

    xn_scr[...] = _rmsnorm_bf16(xnext_ref[...], vec_ref)

    xm = proj_ref[:, 0:M_WIDTH]
    xc = _silu(_conv_interleaved(xm, mtail_ref, _vec(vec_ref, "mcw"), _vec(vec_ref, "mcb"), T))
    gates = _mlstm_project(xc, xm, vec_ref, wqk_ref, wvo_ref, wif_ref, q_scr, k_scr, v_scr, o_scr)
    log_i = gates[:, :LANES]
    b_all = _cumsum_interleaved(_log_sigmoid(gates[:, LANES:]), T)
    a_max = _cummax_interleaved(log_i - b_all, T)
    m_prev = m_ref[...]
    log_inter = b_all + m_prev
    m_t_all = jnp.maximum(log_inter, b_all + a_max)
    w_inter_all = jnp.exp(log_inter - m_t_all)
    floor_all = jnp.exp(-m_t_all)
    b_end = b_all[T - 1:T, :]
    m_new = jnp.maximum(b_end + m_prev, b_end + a_max[T - 1:T, :])
    w_src_all = jnp.exp(b_end - b_all + log_i - m_new)
    decay_all = jnp.exp(b_end + m_prev - m_new)
    m_ref[...] = m_new
    b_t = b_all.T
    li_t = log_i.T

    rowi = lax.broadcasted_iota(jnp.int32, (T, T), 0)
    coli = lax.broadcasted_iota(jnp.int32, (T, T), 1)
    sub_len = T // SUBLANES

    def time_of(r):
        return (r & (SUBLANES - 1)) * sub_len + (r >> SUBLANE_SHIFT)

    causal = time_of(coli) <= time_of(rowi)
    k_scale = M_HEAD_DIM ** -0.5
    mnw = _vec(vec_ref, "mnw")
    mskip = _vec(vec_ref, "mskip")

    for h in range(M_HEADS):
        sl = slice(h * M_HEAD_DIM, (h + 1) * M_HEAD_DIM)
        c_prev = c_ref[h]
        n_prev = n_ref[h:h + 1, :]
        w_inter = w_inter_all[:, h:h + 1]
        w_src = w_src_all[:, h:h + 1]
        decay = decay_all[:, h:h + 1]

        q = q_scr[:, sl]
        ks = k_scr[:, sl] * k_scale
        v = v_scr[:, sl]
        q_b = q.astype(BF16)
        ks_b = ks.astype(BF16)

        log_d = jnp.where(causal, b_all[:, h:h + 1] - b_t[h:h + 1, :] + li_t[h:h + 1, :], -jnp.inf)
        w_intra = jnp.exp(log_d - m_t_all[:, h:h + 1])
        s = lax.dot_general(q_b, ks_b, _NT, preferred_element_type=F32) * w_intra
        inter = lax.dot_general(q_b, c_prev.astype(BF16), _NT, preferred_element_type=F32)
        num = jnp.dot(s.astype(BF16), v.astype(BF16), preferred_element_type=F32) + w_inter * inter
        den = (jnp.sum(s, axis=1, keepdims=True)
               + w_inter * jnp.sum(q * n_prev, axis=1, keepdims=True))
        hh = num / jnp.maximum(jnp.abs(den), floor_all[:, h:h + 1])

        vw_t = (v * w_src).T.astype(BF16)
        c_ref[h] = decay * c_prev + jnp.dot(vw_t, ks_b, preferred_element_type=F32)
        n_ref[h:h + 1, :] = decay * n_prev + jnp.sum(ks * w_src, axis=0, keepdims=True)

        z = proj_ref[:, MIX_WIDTH + h * M_HEAD_DIM:MIX_WIDTH + (h + 1) * M_HEAD_DIM]
        mixed_ref[:, sl] = _head_output(hh, o_scr[:, sl], xc[:, sl], z, mnw[:, sl], mskip[:, sl]).astype(BF16)

        cols = slice(h * slab, (h + 1) * slab)
        next_proj_ref[:, cols] = jnp.dot(xn_scr[...], win_ref[:, cols], preferred_element_type=F32)

    xr = proj_ref[:, M_WIDTH:M_WIDTH + R_WIDTH]
    xcr = _conv_interleaved(xr, rtail_ref, _vec(vec_ref, "rcw"), _vec(vec_ref, "rcb"), T)
    a, bb = _rglru_coeffs(xcr, vec_ref, rwa_ref, rwx_ref)
    ra_scr[...] = a
    rb_scr[...] = bb
    h_ref[...] = _scan_real_interleaved(ra_scr, rb_scr, h_ref[...], T)
    zr = proj_ref[:, MIX_WIDTH + M_WIDTH:MIX_WIDTH + M_WIDTH + R_WIDTH]
    mixed_ref[:, M_WIDTH:M_WIDTH + R_WIDTH] = (rb_scr[...] * _silu(zr)).astype(BF16)

    u = proj_ref[:, M_WIDTH + R_WIDTH:MIX_WIDTH]
    _s5_project(u.astype(BF16), wbu_ref, ure_scr, uim_scr)
    _scan_cplx_interleaved(ure_scr, uim_scr, _vec(vec_ref, "abre"), _vec(vec_ref, "abim"), sre_ref, sim_ref, T)
    glu = _s5_output(ure_scr, uim_scr, u, vec_ref, wcre_ref, wcim_ref, wglu_ref)
    zs = proj_ref[:, MIX_WIDTH + M_WIDTH + R_WIDTH:2 * MIX_WIDTH]
    mixed_ref[:, M_WIDTH + R_WIDTH:MIX_WIDTH] = (glu * _silu(zs)).astype(BF16)


_PROMPT_STATE_SHAPES = ((M_HEADS, M_HEAD_DIM, M_HEAD_DIM), (M_HEADS, M_HEAD_DIM), (1, LANES),
                        ((CONV_W - 1) * SUBLANES, M_WIDTH), (1, R_WIDTH), ((CONV_W - 1) * SUBLANES, R_WIDTH),
                        (1, S_LANES), (1, S_LANES))


def _prompt_mixer(x, total_rows, l, bsz, seq, vec, w_in_b, mats, prev_states):
    rows = PROMPT_CHUNK
    nchunk = seq // rows
    nstep = bsz * nchunk
    aliases = {}
    alias_args = []
    n_in = 4 + len(mats)
    if prev_states is not None:
        alias_args = list(prev_states)
        aliases = {n_in + k: 1 + k for k in range(len(alias_args))}

    def state_spec(shape):
        nd = len(shape)
        return pl.BlockSpec((None, None) + shape, lambda t: (l, t // nchunk) + (0,) * nd)

    out_shape = ([jax.ShapeDtypeStruct((total_rows, MIX_WIDTH), BF16)]
                 + [jax.ShapeDtypeStruct((DEPTH, bsz) + s, F32) for s in _PROMPT_STATE_SHAPES])
    scratch = ([pltpu.VMEM((rows, w), F32)
                for w in (M_WIDTH, M_WIDTH, M_WIDTH, M_WIDTH, R_WIDTH, R_WIDTH, S_LANES, S_LANES)]
               + [pltpu.VMEM((2, rows, 2 * MIX_WIDTH), F32), pltpu.VMEM((rows, D_MODEL), BF16)])
    return pl.pallas_call(
        functools.partial(_prompt_mixer_kernel, rows=rows, nchunk=nchunk, n_alias=len(alias_args)),
        grid=(nstep,),
        in_specs=([pl.BlockSpec((rows, D_MODEL), lambda t: (jnp.minimum(t + 1, nstep - 1), 0)),
                   pl.BlockSpec((rows, D_MODEL), lambda t: (0, 0), pipeline_mode=pl.Buffered(1)),
                   _layer_spec(vec, l), _layer_spec(w_in_b, l)]
                  + [_layer_spec(w, l) for w in mats] + [_ANY_SPEC] * len(alias_args)),
        out_specs=([pl.BlockSpec((rows, MIX_WIDTH), lambda t: (t, 0))]
                   + [state_spec(s) for s in _PROMPT_STATE_SHAPES]),
        out_shape=out_shape,
        scratch_shapes=scratch,
        input_output_aliases=aliases,
        compiler_params=pltpu.CompilerParams(
            dimension_semantics=("arbitrary",), vmem_limit_bytes=VMEM_LIMIT_BYTES),
        name="prompt_mixer",
    )(x, x, vec, w_in_b, *mats, *alias_args)


def _seg_last(x, groups):
    x3 = x.reshape(groups, SUBLANES, x.shape[-1])
    return jnp.broadcast_to(x3[:, SUBLANES - 1:SUBLANES, :], x3.shape).reshape(x.shape)


def _seg_max(x, groups):
    x3 = x.reshape(groups, SUBLANES, x.shape[-1])
    return jnp.broadcast_to(jnp.max(x3, axis=1, keepdims=True), x3.shape).reshape(x.shape)


def _seg_sum(x, groups):
    x3 = x.reshape(groups, SUBLANES, x.shape[-1])
    return jnp.broadcast_to(jnp.sum(x3, axis=1, keepdims=True), x3.shape).reshape(x.shape)


def _seg_rows(state_ref, lanes=slice(None)):
    x = state_ref[:, :, lanes]
    return jnp.broadcast_to(x, (x.shape[0], SUBLANES, x.shape[2])).reshape(x.shape[0] * SUBLANES, x.shape[2])


def _seg_state(x):
    x3 = x.reshape(x.shape[0] // SUBLANES, SUBLANES, x.shape[-1])
    return x3[:, SUBLANES - 1:SUBLANES, :]


def _conv_rolled(xf, w, bias):
    out = w[CONV_W - 1:CONV_W, :] * xf + bias
    for j in range(1, CONV_W):
        out = out + w[CONV_W - 1 - j:CONV_W - j, :] * pltpu.roll(xf, j, 0)
    return out


_SAMPLE_STATE_WIDTHS = (M_WIDTH, LANES, M_WIDTH, R_WIDTH, R_WIDTH, S_LANES, S_LANES)
_SAMPLE_STATE_PER_ROW = (False, False, True, False, True, False, False)


def _sample_mixer_kernel(*refs, rows, c_seqs, n_alias):
    (proj_ref, c0_ref, n0_ref, m0_ref, mtail0_ref, h0_ref, rtail0_ref, sre0_ref, sim0_ref,
     vec_ref, wqk_ref, wvo_ref, wif_ref, rwa_ref, rwx_ref, wbu_ref, wcre_ref, wcim_ref, wglu_ref) = refs[:19]
    (mixed_ref, c_ref, n_ref, m_ref, xmf_ref, h_ref, xrf_ref, sre_ref, sim_ref,
     q_scr, k_scr, v_scr, o_scr, xc_scr, intert_scr, vwt_scr, qt_scr, ksb_scr, dec_scr, b_scr, li_scr,
     ure_scr, uim_scr) = refs[19 + n_alias:]
    R = rows
    G = R // SUBLANES
    sub = pl.program_id(1)
    k_scale = M_HEAD_DIM ** -0.5
    pos = _sublane_pos((R, 1))
    valid = pos >= SAMPLE_LEAD
    first_token = pos == SAMPLE_LEAD

    @pl.when(sub == 0)
    def _():
        xm = jnp.where(valid, proj_ref[:, 0:M_WIDTH], mtail0_ref[...])
        xmf_ref[...] = xm
        xc = _silu(_conv_rolled(xm, _vec(vec_ref, "mcw"), _vec(vec_ref, "mcb")))
        xc_scr[...] = xc
        gates = _mlstm_project(xc, xm, vec_ref, wqk_ref, wvo_ref, wif_ref, q_scr, k_scr, v_scr, o_scr)
        log_i = jnp.where(valid, gates[:, :LANES], -jnp.inf)
        log_f = jnp.where(valid, _log_sigmoid(gates[:, LANES:]), 0.0)
        b_all = _tile_cumsum(log_f)
        b_end = _seg_last(b_all, G)
        m_rows = _seg_rows(m0_ref)
        n_rows = _seg_rows(n0_ref)
        log_src = b_end - b_all + log_i
        m_new = jnp.maximum(b_end + m_rows, _seg_max(log_src, G))
        w_src_all = jnp.exp(log_src - m_new)
        decay_all = jnp.exp(b_end + m_rows - m_new)
        dec_scr[...] = decay_all
        b_scr[...] = b_all
        li_scr[...] = log_i
        m_ref[...] = _seg_state(m_new)
        for h in range(M_HEADS):
            sl = slice(h * M_HEAD_DIM, (h + 1) * M_HEAD_DIM)
            ks = k_scr[:, sl] * k_scale
            w_src = w_src_all[:, h:h + 1]
            n_ref[:, :, sl] = _seg_state(decay_all[:, h:h + 1] * n_rows[:, sl] + _seg_sum(ks * w_src, G))
            vwt_scr[h] = (v_scr[:, sl] * w_src).T
            qt_scr[h] = q_scr[:, sl].T.astype(BF16)
            ksb_scr[h] = ks.astype(BF16)
            intert_scr[h] = jnp.zeros((M_HEAD_DIM, R), F32)

        xr = jnp.where(valid, proj_ref[:, M_WIDTH:M_WIDTH + R_WIDTH], rtail0_ref[...])
        xrf_ref[...] = xr
        xcr = _conv_rolled(xr, _vec(vec_ref, "rcw"), _vec(vec_ref, "rcb"))
        a, bb = _rglru_coeffs(xcr, vec_ref, rwa_ref, rwx_ref)
        hs = _tile_scan_real(a, jnp.where(valid, bb, 0.0) + jnp.where(first_token, a * _seg_rows(h0_ref), 0.0))
        h_ref[...] = _seg_state(hs)
        zr = proj_ref[:, MIX_WIDTH + M_WIDTH:MIX_WIDTH + M_WIDTH + R_WIDTH]
        mixed_ref[:, M_WIDTH:M_WIDTH + R_WIDTH] = jnp.where(valid, hs * _silu(zr), 0.0).astype(BF16)

        u = proj_ref[:, M_WIDTH + R_WIDTH:MIX_WIDTH]
        _s5_project(u.astype(BF16), wbu_ref, ure_scr, uim_scr)
        p_re = _vec(vec_ref, "abre")
        p_im = _vec(vec_ref, "abim")
        for blk in range(S_LANES // SCAN_LANE_BLOCK):
            sl = slice(blk * SCAN_LANE_BLOCK, (blk + 1) * SCAN_LANE_BLOCK)
            pr = jnp.broadcast_to(p_re[:, sl], (R, SCAN_LANE_BLOCK))
            pi = jnp.broadcast_to(p_im[:, sl], (R, SCAN_LANE_BLOCK))
            s0r = _seg_rows(sre0_ref, sl)
            s0i = _seg_rows(sim0_ref, sl)
            sr, si = _tile_scan_cplx(
                jnp.where(valid, ure_scr[:, sl], 0.0) + jnp.where(first_token, pr * s0r - pi * s0i, 0.0),
                jnp.where(valid, uim_scr[:, sl], 0.0) + jnp.where(first_token, pr * s0i + pi * s0r, 0.0), pr, pi)
            ure_scr[:, sl] = sr
            uim_scr[:, sl] = si
            sre_ref[:, :, sl] = _seg_state(sr)
            sim_ref[:, :, sl] = _seg_state(si)
        glu = _s5_output(ure_scr, uim_scr, u, vec_ref, wcre_ref, wcim_ref, wglu_ref)
        zs = proj_ref[:, MIX_WIDTH + M_WIDTH + R_WIDTH:2 * MIX_WIDTH]
        mixed_ref[:, M_WIDTH + R_WIDTH:MIX_WIDTH] = jnp.where(valid, glu * _silu(zs), 0.0).astype(BF16)

    lane_seq = lax.broadcasted_iota(jnp.int32, (M_HEAD_DIM, R), 1) >> SUBLANE_SHIFT
    seq0 = sub * c_seqs
    for h in range(M_HEADS):
        c_old = c0_ref[:, h].reshape(c_seqs * M_HEAD_DIM, M_HEAD_DIM)
        readout = jnp.dot(c_old.astype(BF16), qt_scr[h], preferred_element_type=F32)
        acc = intert_scr[h]
        vwt = vwt_scr[h]
        lhs = []
        for s in range(c_seqs):
            own = lane_seq == seq0 + s
            acc = jnp.where(own, readout[s * M_HEAD_DIM:(s + 1) * M_HEAD_DIM], acc)
            lhs.append(jnp.where(own, vwt, 0.0))
        intert_scr[h] = acc
        upd = jnp.dot(jnp.concatenate(lhs, axis=0).astype(BF16), ksb_scr[h], preferred_element_type=F32)
        for s in range(c_seqs):
            r0 = pl.multiple_of((seq0 + s) * SUBLANES, SUBLANES)
            c_ref[s, h] = (dec_scr[pl.ds(r0, 1), h:h + 1] * c0_ref[s, h]
                           + upd[s * M_HEAD_DIM:(s + 1) * M_HEAD_DIM])

    @pl.when(sub == pl.num_programs(1) - 1)
    def _():
        _sample_heads(proj_ref, m0_ref, n0_ref, vec_ref, mixed_ref,
                      q_scr, k_scr, v_scr, o_scr, xc_scr, intert_scr, b_scr, li_scr, valid, R)


def _sample_heads(proj_ref, m0_ref, n0_ref, vec_ref, mixed_ref,
                  q_scr, k_scr, v_scr, o_scr, xc_scr, intert_scr, b_scr, li_scr, valid, R):
    rowi = lax.broadcasted_iota(jnp.int32, (R, R), 0)
    coli = lax.broadcasted_iota(jnp.int32, (R, R), 1)
    eye = coli == rowi
    same_causal = jnp.logical_and(coli <= rowi, (coli >> SUBLANE_SHIFT) == (rowi >> SUBLANE_SHIFT))
    k_scale = M_HEAD_DIM ** -0.5
    mnw = _vec(vec_ref, "mnw")
    mskip = _vec(vec_ref, "mskip")
    b_all = b_scr[...]
    log_i = li_scr[...]
    m_rows = _seg_rows(m0_ref)
    n_rows = _seg_rows(n0_ref)

    for h in range(M_HEADS):
        sl = slice(h * M_HEAD_DIM, (h + 1) * M_HEAD_DIM)
        bcol = b_all[:, h:h + 1]
        licol = log_i[:, h:h + 1]
        mcol = m_rows[:, h:h + 1]
        brow = _row_from_col(bcol, eye)
        lirow = _row_from_col(licol, eye)
        q = q_scr[:, sl]
        ks = k_scr[:, sl] * k_scale
        v = v_scr[:, sl]
        q_b = q.astype(BF16)
        ks_b = ks.astype(BF16)

        log_d = jnp.where(same_causal, bcol - brow + lirow, -jnp.inf)
        log_inter = bcol + mcol
        m_t = jnp.maximum(log_inter, jnp.max(log_d, axis=1, keepdims=True))
        w_intra = jnp.exp(log_d - m_t)
        w_inter = jnp.exp(log_inter - m_t)
        s = lax.dot_general(q_b, ks_b, _NT, preferred_element_type=F32) * w_intra
        num = (jnp.dot(s.astype(BF16), v.astype(BF16), preferred_element_type=F32)
               + w_inter * intert_scr[h].T)
        den = (jnp.sum(s, axis=1, keepdims=True)
               + w_inter * jnp.sum(q * n_rows[:, sl], axis=1, keepdims=True))
        hh = num / jnp.maximum(jnp.abs(den), jnp.exp(-m_t))

        z = proj_ref[:, MIX_WIDTH + h * M_HEAD_DIM:MIX_WIDTH + (h + 1) * M_HEAD_DIM]
        out = _head_output(hh, o_scr[:, sl], xc_scr[:, sl], z, mnw[:, sl], mskip[:, sl])
        mixed_ref[:, sl] = jnp.where(valid, out, 0.0).astype(BF16)


def _sample_mixer(proj, l, row0, mixed_prev, c_all, row_state, vec, mats, prev_out):
    R = SAMPLE_ROWS
    nrows = proj.shape[0]
    nsub = (R // SUBLANES) // SAMPLE_C_SEQS
    blk0 = row0 // R
    c_spec = pl.BlockSpec((None, SAMPLE_C_SEQS, M_HEADS, M_HEAD_DIM, M_HEAD_DIM),
                          lambda i, j: (l, i * nsub + j, 0, 0, 0))

    def state_spec(width, per_row):
        if per_row:
            return pl.BlockSpec((None, R, width), lambda i, j: (l, i, 0))
        return pl.BlockSpec((None, R // SUBLANES, 1, width), lambda i, j: (l, i, 0, 0))

    def state_shape(width, per_row):
        return (DEPTH, nrows, width) if per_row else (DEPTH, nrows // SUBLANES, 1, width)

    state_specs = [state_spec(w, r) for w, r in zip(_SAMPLE_STATE_WIDTHS, _SAMPLE_STATE_PER_ROW)]
    alias_args = [mixed_prev] + (list(prev_out) if prev_out is not None else [])
    first_alias = 2 + len(row_state) + 1 + len(mats)
    aliases = {first_alias + k: k for k in range(len(alias_args))}
    out_shape = ([jax.ShapeDtypeStruct(mixed_prev.shape, BF16), jax.ShapeDtypeStruct(c_all.shape, F32)]
                 + [jax.ShapeDtypeStruct(state_shape(w, r), F32)
                    for w, r in zip(_SAMPLE_STATE_WIDTHS, _SAMPLE_STATE_PER_ROW)])
    scratch = ([pltpu.VMEM((R, M_WIDTH), F32)] * 5
               + [pltpu.VMEM((M_HEADS, M_HEAD_DIM, R), F32), pltpu.VMEM((M_HEADS, M_HEAD_DIM, R), F32),
                  pltpu.VMEM((M_HEADS, M_HEAD_DIM, R), BF16), pltpu.VMEM((M_HEADS, R, M_HEAD_DIM), BF16)]
               + [pltpu.VMEM((R, LANES), F32)] * 3
               + [pltpu.VMEM((R, S_LANES), F32)] * 2)
    return pl.pallas_call(
        functools.partial(_sample_mixer_kernel, rows=R, c_seqs=SAMPLE_C_SEQS, n_alias=len(alias_args)),
        grid=(nrows // R, nsub),
        in_specs=([pl.BlockSpec((R, 2 * MIX_WIDTH), lambda i, j: (i, 0)), c_spec] + state_specs
                  + [_layer_spec(vec, l)] + [_layer_spec(w, l) for w in mats] + [_ANY_SPEC] * len(alias_args)),
        out_specs=[pl.BlockSpec((R, MIX_WIDTH), lambda i, j: (blk0 + i, 0)), c_spec] + state_specs,
        out_shape=out_shape,
        scratch_shapes=scratch,
        input_output_aliases=aliases,
        compiler_params=pltpu.CompilerParams(
            dimension_semantics=("parallel", "arbitrary"), vmem_limit_bytes=VMEM_LIMIT_BYTES),
        name="sample_mixer",
    )(proj, c_all, *row_state, vec, *mats, *alias_args)


def _prepare_weights(p, s5):
    abre, abim, bbre_t, bbim_t = s5

    def row(v):
        v = v if v.ndim == 3 else v[:, None, :]
        return jnp.pad(v, ((0, 0), (0, 0), (0, _VEC_WIDTH - v.shape[-1])))

    def split_gates(g):
        pad = [(0, 0)] * (g.ndim - 1) + [(0, LANES - M_HEADS)]
        return jnp.concatenate([jnp.pad(g[..., :M_HEADS], pad), jnp.pad(g[..., M_HEADS:], pad)], axis=-1)

    table = {"mcw": p["m_conv_w"], "rcw": p["r_conv_w"], "mcb": p["m_conv_b"], "mnw": p["m_norm_w"],
             "mskip": p["m_skip"], "rcb": p["r_conv_b"], "rba": p["r_ba"], "rbx": p["r_bx"], "rlam": p["r_lam"],
             "sd": p["s_d"], "bglu": p["s_b_glu"], "bif": split_gates(p["m_b_if"]), "abre": abre, "abim": abim,
             "norm_w": p["norm_w"]}
    used = sum(n for _, n, _ in _VEC_LAYOUT.values())
    vec = jnp.concatenate([row(table[k]) for k in _VEC_LAYOUT]
                          + [jnp.zeros((DEPTH, _VEC_ROWS - used, _VEC_WIDTH), F32)], axis=1)

    def block_diag_halves(w):
        nb = R_BLOCKS // 2
        w5 = w.reshape(DEPTH, 2, nb, w.shape[-2], w.shape[-1])
        return jnp.einsum("ab,dhaij->dhaibj", jnp.eye(nb, dtype=F32), w5).reshape(
            DEPTH, 2, nb * w.shape[-2], nb * w.shape[-1]).astype(BF16)

    n_in = S_WIDTH // LANES
    g_in = LANES // S_GROUP
    n_out = 2
    g_out = S_GROUPS // n_out

    def bu_blocks(bb_t):
        bb5 = bb_t.reshape(DEPTH, S_GROUP, n_in, g_in, S_STATE)
        return jnp.einsum("ab,dckbp->dkacbp", jnp.eye(g_in, dtype=F32), bb5).reshape(
            DEPTH, n_in, LANES, g_in * S_STATE)

    def c_blocks(c):
        c5 = c.reshape(DEPTH, n_out, g_out, S_GROUP, S_STATE)
        return jnp.einsum("ab,dmbcp->dmapbc", jnp.eye(g_out, dtype=F32), c5).reshape(
            DEPTH, n_out, g_out * S_STATE, g_out * S_GROUP).astype(BF16)

    mats = {
        "wqk": jnp.concatenate([p["m_wq"], p["m_wk"]], axis=-1).astype(BF16),
        "wvo": jnp.concatenate([p["m_wv"], p["m_wo"]], axis=-1).astype(BF16),
        "wif": split_gates(p["m_w_if"]).astype(BF16),
        "rwa": block_diag_halves(p["r_wa"]), "rwx": block_diag_halves(p["r_wx"]),
        "wbu": jnp.concatenate([bu_blocks(bbre_t), bu_blocks(bbim_t)], axis=-1).astype(BF16),
        "wcre": c_blocks(p["s_c_re"]), "wcim": c_blocks(p["s_c_im"]),
        "wglu": p["s_w_glu"].astype(BF16),
    }
    return vec, p["w_in"].astype(BF16), p["w_out"].astype(BF16), [mats[k] for k in _MATRIX_KEYS]


def _sample_rows_state(n, m, mconv, h, rconv, sre, sim):
    bsz = n.shape[1]

    def per_seq(x):
        return x.reshape(DEPTH, bsz, 1, x.shape[-1])

    def conv_rows(buf):
        padded = jnp.pad(buf, ((0, 0), (0, 0), (SAMPLE_LEAD - (CONV_W - 1), SUBLANES - SAMPLE_LEAD), (0, 0)))
        return padded.reshape(DEPTH, bsz * SUBLANES, buf.shape[-1])

    m_pad = jnp.pad(m, ((0, 0), (0, 0), (0, LANES - M_HEADS)))
    return (per_seq(n.reshape(DEPTH, bsz, M_WIDTH)), per_seq(m_pad), conv_rows(mconv), per_seq(h), conv_rows(rconv),
            per_seq(sre.reshape(DEPTH, bsz, S_LANES)), per_seq(sim.reshape(DEPTH, bsz, S_LANES)))


def _sample_state_from_rows(n, m, xmf, h, xrf, sre, sim):
    bsz = n.shape[1]
    conv_from = SUBLANES - (CONV_W - 1)

    def conv_state(x):
        return x.reshape(DEPTH, bsz, SUBLANES, x.shape[-1])[:, :, conv_from:]

    return (n.reshape(DEPTH, bsz, M_HEADS, M_HEAD_DIM), m.reshape(DEPTH, bsz, LANES)[..., :M_HEADS],
            conv_state(xmf), h.reshape(DEPTH, bsz, R_WIDTH), conv_state(xrf),
            sre.reshape(DEPTH, bsz, S_GROUPS, S_STATE), sim.reshape(DEPTH, bsz, S_GROUPS, S_STATE))


def _prompt_state_from_kernel(c, n, m, mtail, h, rtail, sre, sim):
    bsz = c.shape[1]
    return (c, n, m[:, :, 0, :M_HEADS], mtail[:, :, SUBLANES - 1::SUBLANES], h.reshape(DEPTH, bsz, R_WIDTH),
            rtail[:, :, SUBLANES - 1::SUBLANES],
            sre.reshape(DEPTH, bsz, S_GROUPS, S_STATE), sim.reshape(DEPTH, bsz, S_GROUPS, S_STATE))


def _interleave_chunks(x, inverse=False):
    bsz, seq, d = x.shape
    sub_len = PROMPT_CHUNK // SUBLANES
    inner = (sub_len, SUBLANES) if inverse else (SUBLANES, sub_len)
    return x.reshape(bsz, seq // PROMPT_CHUNK, *inner, d).transpose(0, 1, 3, 2, 4).reshape(bsz, seq, d)


def kernel(x_prompt, x_sample, state_mlstm_C, state_mlstm_n, state_mlstm_m, state_mlstm_conv, state_rglru_h, state_rglru_conv, state_s5_re, state_s5_im, norm_w, w_in, w_out, m_conv_w, m_conv_b, m_wq, m_wk, m_wv, m_wo, m_w_if, m_b_if, m_norm_w, m_skip, r_conv_w, r_conv_b, r_wa, r_ba, r_wx, r_bx, r_lam, s_lam_re, s_lam_im, s_b_re, s_b_im, s_c_re, s_c_im, s_d, s_log_step, s_w_glu, s_b_glu, final_norm_w):
    p = dict(norm_w=norm_w, w_in=w_in, w_out=w_out, m_conv_w=m_conv_w, m_conv_b=m_conv_b, m_wq=m_wq, m_wk=m_wk,
             m_wv=m_wv, m_wo=m_wo, m_w_if=m_w_if, m_b_if=m_b_if, m_norm_w=m_norm_w, m_skip=m_skip,
             r_conv_w=r_conv_w, r_conv_b=r_conv_b, r_wa=r_wa, r_ba=r_ba, r_wx=r_wx, r_bx=r_bx, r_lam=r_lam,
             s_c_re=s_c_re, s_c_im=s_c_im, s_d=s_d, s_w_glu=s_w_glu, s_b_glu=s_b_glu)
    s5 = _s5_prep(s_lam_re, s_lam_im, s_log_step, s_b_re, s_b_im)
    vec, w_in_b, w_out_b, mats = _prepare_weights(p, s5)
    final_w = final_norm_w.reshape(1, D_MODEL)

    bsz, seq, _ = x_prompt.shape
    dec_batch, dec_seq, _ = x_sample.shape
    n_prompt = bsz * seq
    n_sample = dec_batch * SUBLANES
    stream = [_interleave_chunks(x_prompt).reshape(n_prompt, D_MODEL),
              jnp.pad(x_sample, ((0, 0), (SAMPLE_LEAD, 0), (0, 0))).reshape(n_sample, D_MODEL)]
    rows_state = _sample_rows_state(state_mlstm_n, state_mlstm_m, state_mlstm_conv, state_rglru_h,
                                    state_rglru_conv, state_s5_re, state_s5_im)
    pr_states = None
    sa_states = None
    for l in range(DEPTH):
        last = l == DEPTH - 1
        mixed, *pr_states = _prompt_mixer(stream[0], n_prompt + n_sample, l, bsz, seq, vec, w_in_b, mats, pr_states)
        proj_s = _inproj(stream[-1], stream[-1].shape[0] - n_sample, n_sample, vec, w_in_b, l)
        mixed, *sa_states = _sample_mixer(proj_s, l, n_prompt, mixed, state_mlstm_C, rows_state, vec, mats, sa_states)
        stream = _outproj(mixed, stream, w_out_b, l, final_w, last, split_rows=n_prompt if last else None)
    y_prompt = _interleave_chunks(stream[0].reshape(bsz, seq, D_MODEL), inverse=True)
    y_sample = stream[1].reshape(dec_batch, SUBLANES, D_MODEL)[:, SAMPLE_LEAD:]
    return (y_prompt, y_sample, *_prompt_state_from_kernel(*pr_states),
            sa_states[0], *_sample_state_from_rows(*sa_states[1:]))
```

```python
import functools

import jax
import jax.numpy as jnp
from jax import lax
from jax.experimental import pallas as pl
from jax.experimental.pallas import tpu as pltpu

F32 = jnp.float32
BF16 = jnp.bfloat16

D_MODEL = 2048
DEPTH = 2
MIX_WIDTH = D_MODEL
M_WIDTH = MIX_WIDTH // 2
R_WIDTH = MIX_WIDTH // 4
S_WIDTH = MIX_WIDTH - M_WIDTH - R_WIDTH
M_HEADS = 8
M_HEAD_DIM = M_WIDTH // M_HEADS
R_BLOCKS = 8
RG_C = 8.0
S_GROUP = 16
S_GROUPS = S_WIDTH // S_GROUP
S_STATE = 64
S_LANES = S_GROUPS * S_STATE
CONV_W = 4
EPS = 1e-6

SUBLANES = 8
SUBLANE_SHIFT = SUBLANES.bit_length() - 1
LANES = 128
VMEM_LIMIT_BYTES = 56 * 1024 * 1024

PROMPT_CHUNK = 256
SAMPLE_ROWS = 128
SAMPLE_C_SEQS = 8
SAMPLE_LEAD = SUBLANES - 4
INPROJ_TM = 512
INPROJ_TN = 1024
OUTPROJ_TM = 512
SCAN_LANE_BLOCK = 256

_NT = (((1,), (1,)), ((), ()))

_VEC_WIDTH = D_MODEL
_VEC_GROUPS = ((("mcw", CONV_W, M_WIDTH), ("rcw", CONV_W, R_WIDTH)),
               (("mcb", 1, M_WIDTH), ("mnw", 1, M_WIDTH)),
               (("mskip", 1, M_WIDTH), ("rcb", 1, R_WIDTH), ("rba", 1, R_WIDTH)),
               (("rbx", 1, R_WIDTH), ("rlam", 1, R_WIDTH), ("sd", 1, S_WIDTH), ("bglu", 1, S_WIDTH)),
               (("bif", 1, 2 * LANES),),
               (("abre", 1, S_LANES),), (("abim", 1, S_LANES),), (("norm_w", 1, D_MODEL),))
_VEC_LAYOUT = {}
_row = 0
for _group in _VEC_GROUPS:
    _lane = 0
    for _name, _n, _w in _group:
        _VEC_LAYOUT[_name] = (_row, _n, _lane, _w)
        _lane += _w
    _row += _group[0][1]
_VEC_USED_ROWS = _row
_VEC_ROWS = -(-_row // SUBLANES) * SUBLANES


def _vec(vec_ref, name):
    r0, n, l0, w = _VEC_LAYOUT[name]
    return vec_ref[r0:r0 + n, l0:l0 + w]


def _layer_spec(arr, l):
    nd = arr.ndim - 1
    return pl.BlockSpec((None,) + arr.shape[1:], lambda *_: (l,) + (0,) * nd, pipeline_mode=pl.Buffered(1))


_ANY_SPEC = pl.BlockSpec(memory_space=pl.ANY)


def _s5_prep_kernel(lr_ref, li_ref, ls_ref, brt_ref, bit_ref, abre_ref, abim_ref, bbre_ref, bbim_ref):
    lr = lr_ref[0]
    li = li_ref[0]
    dt = jnp.exp(ls_ref[0])
    mag = jnp.exp(lr * dt)
    ang = li * dt
    ab_re = mag * jnp.cos(ang)
    ab_im = mag * jnp.sin(ang)
    den = lr * lr + li * li
    nr = ab_re - 1.0
    f_re = (nr * lr + ab_im * li) / den
    f_im = (ab_im * lr - nr * li) / den
    br = brt_ref[0]
    bi = bit_ref[0]
    abre_ref[0] = ab_re
    abim_ref[0] = ab_im
    bbre_ref[0] = f_re * br - f_im * bi
    bbim_ref[0] = f_re * bi + f_im * br


def _s5_prep(s_lam_re, s_lam_im, s_log_step, s_b_re, s_b_im):
    lr = s_lam_re.reshape(DEPTH, 1, S_LANES)
    li = s_lam_im.reshape(DEPTH, 1, S_LANES)
    ls = jnp.repeat(s_log_step, S_STATE, axis=-1).reshape(DEPTH, 1, S_LANES)
    brt = s_b_re.reshape(DEPTH, S_LANES, S_GROUP).transpose(0, 2, 1)
    bit = s_b_im.reshape(DEPTH, S_LANES, S_GROUP).transpose(0, 2, 1)
    vec = pl.BlockSpec((1, 1, S_LANES), lambda l: (l, 0, 0))
    mat = pl.BlockSpec((1, S_GROUP, S_LANES), lambda l: (l, 0, 0))
    return pl.pallas_call(
        _s5_prep_kernel,
        grid=(DEPTH,),
        in_specs=[vec, vec, vec, mat, mat],
        out_specs=[vec, vec, mat, mat],
        out_shape=[jax.ShapeDtypeStruct((DEPTH, 1, S_LANES), F32),
                   jax.ShapeDtypeStruct((DEPTH, 1, S_LANES), F32),
                   jax.ShapeDtypeStruct((DEPTH, S_GROUP, S_LANES), F32),
                   jax.ShapeDtypeStruct((DEPTH, S_GROUP, S_LANES), F32)],
        name="s5_prep",
    )(lr, li, ls, brt, bit)


def _stream_specs(parts, tm):
    if len(parts) == 1:
        return [pl.BlockSpec((tm, D_MODEL), lambda i: (i, 0))]
    nb0 = parts[0].shape[0] // tm
    return [pl.BlockSpec((tm, D_MODEL), lambda i: (jnp.minimum(i, nb0 - 1), 0)),
            pl.BlockSpec((tm, D_MODEL), lambda i: (jnp.maximum(i - nb0, 0), 0))]


def _read_stream(refs, nb0):
    if len(refs) == 1:
        return refs[0][...]
    return jnp.where(pl.program_id(0) < nb0, refs[0][...], refs[1][...])


def _rmsnorm_bf16(x, vec_ref):
    ms = jnp.mean(x * x, axis=-1, keepdims=True)
    return (x * lax.rsqrt(ms + EPS) * _vec(vec_ref, "norm_w")).astype(BF16)


def _inproj_kernel(x_ref, vec_ref, w_ref, o_ref):
    xn = _rmsnorm_bf16(x_ref[...], vec_ref)
    for j in range((2 * MIX_WIDTH) // INPROJ_TN):
        cols = slice(j * INPROJ_TN, (j + 1) * INPROJ_TN)
        o_ref[:, cols] = jnp.dot(xn, w_ref[:, cols], preferred_element_type=F32)


def _inproj(x, row0, nrows, vec, w_in_b, l):
    blk0 = row0 // INPROJ_TM
    return pl.pallas_call(
        _inproj_kernel,
        grid=(nrows // INPROJ_TM,),
        in_specs=[pl.BlockSpec((INPROJ_TM, D_MODEL), lambda i: (blk0 + i, 0)),
                  _layer_spec(vec, l), _layer_spec(w_in_b, l)],
        out_specs=pl.BlockSpec((INPROJ_TM, 2 * MIX_WIDTH), lambda i: (i, 0)),
        out_shape=jax.ShapeDtypeStruct((nrows, 2 * MIX_WIDTH), F32),
        compiler_params=pltpu.CompilerParams(
            dimension_semantics=("parallel",), vmem_limit_bytes=VMEM_LIMIT_BYTES),
        name="inproj",
    )(x, vec, w_in_b)


def _outproj_kernel(*refs, n_x, nb0, n_out, nb0_out, final):
    mixed_ref, x_refs = refs[0], refs[1:1 + n_x]
    w_ref, fw_ref = refs[1 + n_x:3 + n_x]
    o_refs = refs[3 + n_x:]
    y = _read_stream(x_refs, nb0) + jnp.dot(mixed_ref[...], w_ref[...], preferred_element_type=F32)
    if final:
        ms = jnp.mean(y * y, axis=-1, keepdims=True)
        y = y * lax.rsqrt(ms + EPS) * fw_ref[...]
    if n_out == 1:
        o_refs[0][...] = y
    else:
        @pl.when(pl.program_id(0) < nb0_out)
        def _():
            o_refs[0][...] = y

        @pl.when(pl.program_id(0) >= nb0_out)
        def _():
            o_refs[1][...] = y


def _outproj(mixed, x_parts, w_out_b, l, final_w, final, split_rows=None):
    n = mixed.shape[0]
    out_rows = [n] if split_rows is None else [split_rows, n - split_rows]
    out_parts = [jax.ShapeDtypeStruct((r, D_MODEL), F32) for r in out_rows]
    return pl.pallas_call(
        functools.partial(_outproj_kernel, n_x=len(x_parts), nb0=x_parts[0].shape[0] // OUTPROJ_TM,
                          n_out=len(out_parts), nb0_out=out_rows[0] // OUTPROJ_TM, final=final),
        grid=(n // OUTPROJ_TM,),
        in_specs=([pl.BlockSpec((OUTPROJ_TM, MIX_WIDTH), lambda i: (i, 0))] + _stream_specs(x_parts, OUTPROJ_TM)
                  + [_layer_spec(w_out_b, l), pl.BlockSpec((1, D_MODEL), lambda i: (0, 0))]),
        out_specs=_stream_specs(out_parts, OUTPROJ_TM),
        out_shape=out_parts,
        compiler_params=pltpu.CompilerParams(
            dimension_semantics=("arbitrary",), vmem_limit_bytes=VMEM_LIMIT_BYTES),
        name="outproj",
    )(mixed, *x_parts, w_out_b, final_w)


def _silu(x):
    return x * jax.nn.sigmoid(x)


def _log_sigmoid(x):
    return jnp.minimum(x, 0.0) - jnp.log1p(jnp.exp(-jnp.abs(x)))


def _softplus(x):
    return jnp.maximum(x, 0.0) + jnp.log1p(jnp.exp(-jnp.abs(x)))


def _sublane_pos(shape):
    return lax.broadcasted_iota(jnp.int32, shape, 0) & (SUBLANES - 1)


def _tile_cumsum(x):
    pos = _sublane_pos(x.shape)
    s = 1
    while s < SUBLANES:
        x = x + jnp.where(pos >= s, pltpu.roll(x, s, 0), 0.0)
        s *= 2
    return x


def _tile_cummax(x):
    pos = _sublane_pos(x.shape)
    s = 1
    while s < SUBLANES:
        x = jnp.maximum(x, jnp.where(pos >= s, pltpu.roll(x, s, 0), -jnp.inf))
        s *= 2
    return x


def _tile_scan_real(a, b):
    pos = _sublane_pos(a.shape)
    s = 1
    while s < SUBLANES:
        m = pos >= s
        b = jnp.where(m, a * pltpu.roll(b, s, 0) + b, b)
        if 2 * s < SUBLANES:
            a = jnp.where(m, a * pltpu.roll(a, s, 0), a)
        s *= 2
    return b


def _tile_scan_cplx(sr, si, pr, pi):
    pos = _sublane_pos(sr.shape)
    s = 1
    while s < SUBLANES:
        m = pos >= s
        sr_sh = pltpu.roll(sr, s, 0)
        si_sh = pltpu.roll(si, s, 0)
        sr, si = (jnp.where(m, sr + (pr * sr_sh - pi * si_sh), sr),
                  jnp.where(m, si + (pr * si_sh + pi * sr_sh), si))
        if 2 * s < SUBLANES:
            pr, pi = pr * pr - pi * pi, 2.0 * (pr * pi)
        s *= 2
    return sr, si


def _mlstm_project(xc, xm, vec_ref, wqk_ref, wvo_ref, wif_ref, q_scr, k_scr, v_scr, o_scr):
    xc_b = xc.astype(BF16)
    xm_b = xm.astype(BF16)
    for h in range(M_HEADS):
        sl = slice(h * M_HEAD_DIM, (h + 1) * M_HEAD_DIM)
        qk = jnp.dot(xc_b[:, sl], wqk_ref[h], preferred_element_type=F32)
        vo = jnp.dot(xm_b[:, sl], wvo_ref[h], preferred_element_type=F32)
        q_scr[:, sl] = qk[:, :M_HEAD_DIM]
        k_scr[:, sl] = qk[:, M_HEAD_DIM:]
        v_scr[:, sl] = vo[:, :M_HEAD_DIM]
        o_scr[:, sl] = vo[:, M_HEAD_DIM:]
    return (jnp.dot(q_scr[...].astype(BF16), wif_ref[0:M_WIDTH, :], preferred_element_type=F32)
            + jnp.dot(k_scr[...].astype(BF16), wif_ref[M_WIDTH:2 * M_WIDTH, :], preferred_element_type=F32)
            + jnp.dot(v_scr[...].astype(BF16), wif_ref[2 * M_WIDTH:3 * M_WIDTH, :], preferred_element_type=F32)
            + _vec(vec_ref, "bif"))


def _head_output(hh, o_pre, xc_h, z_h, mnw_h, mskip_h):
    mu = jnp.mean(hh, axis=1, keepdims=True)
    hc = hh - mu
    var = jnp.mean(hc * hc, axis=1, keepdims=True)
    hn = hc * lax.rsqrt(var + EPS) * mnw_h
    return (jax.nn.sigmoid(o_pre) * hn + mskip_h * xc_h) * _silu(z_h)


def _rglru_coeffs(xcr, vec_ref, rwa_ref, rwx_ref):
    xcr_b = xcr.astype(BF16)
    half = R_WIDTH // 2
    ra_pre = jnp.concatenate(
        [jnp.dot(xcr_b[:, :half], rwa_ref[0], preferred_element_type=F32),
         jnp.dot(xcr_b[:, half:], rwa_ref[1], preferred_element_type=F32)], axis=1) + _vec(vec_ref, "rba")
    rx_pre = jnp.concatenate(
        [jnp.dot(xcr_b[:, :half], rwx_ref[0], preferred_element_type=F32),
         jnp.dot(xcr_b[:, half:], rwx_ref[1], preferred_element_type=F32)], axis=1) + _vec(vec_ref, "rbx")
    log_a = (-RG_C) * jax.nn.sigmoid(ra_pre) * _softplus(-_vec(vec_ref, "rlam"))
    a = jnp.exp(log_a)
    th = jnp.tanh(log_a)
    one_minus_a2 = (-2.0 * th) / (1.0 - th)
    return a, jnp.sqrt(one_minus_a2) * (jax.nn.sigmoid(rx_pre) * xcr)


def _s5_project(u_b, wbu_ref, re_ref, im_ref):
    per_slice = (LANES // S_GROUP) * S_STATE
    for k in range(S_WIDTH // LANES):
        res = jnp.dot(u_b[:, k * LANES:(k + 1) * LANES], wbu_ref[k], preferred_element_type=F32)
        re_ref[:, k * per_slice:(k + 1) * per_slice] = res[:, :per_slice]
        im_ref[:, k * per_slice:(k + 1) * per_slice] = res[:, per_slice:]


def _s5_output(re_ref, im_ref, u, vec_ref, wcre_ref, wcim_ref, wglu_ref):
    nblk = wcre_ref.shape[0]
    k_blk = S_LANES // nblk
    parts = []
    for m in range(nblk):
        ks = slice(m * k_blk, (m + 1) * k_blk)
        parts.append(lax.dot_general(re_ref[:, ks].astype(BF16), wcre_ref[m], _NT, preferred_element_type=F32)
                     - lax.dot_general(im_ref[:, ks].astype(BF16), wcim_ref[m], _NT, preferred_element_type=F32))
    y = jnp.concatenate(parts, axis=1) + _vec(vec_ref, "sd") * u
    g = jax.nn.gelu(y)
    return g * jax.nn.sigmoid(jnp.dot(g.astype(BF16), wglu_ref[...], preferred_element_type=F32)
                              + _vec(vec_ref, "bglu"))


_MATRIX_KEYS = ("wqk", "wvo", "wif", "rwa", "rwx", "wbu", "wcre", "wcim", "wglu")


def _tile(x, i):
    return x[i * SUBLANES:(i + 1) * SUBLANES]


def _conv_interleaved(x, tail_ref, w, bias, rows):
    ntiles = rows // SUBLANES
    pos = lax.broadcasted_iota(jnp.int32, (SUBLANES, x.shape[1]), 0)
    prev = tail_ref[...]
    before = [pltpu.roll(jnp.where(pos == SUBLANES - 1, _tile(prev, CONV_W - 1 - d), _tile(x, ntiles - d)), 1, 0)
              for d in range(1, CONV_W)]
    out = w[CONV_W - 1:CONV_W, :] * x + bias
    for j in range(1, CONV_W):
        shifted = jnp.concatenate(before[:j][::-1] + [x[:rows - j * SUBLANES]], axis=0)
        out = out + w[CONV_W - 1 - j:CONV_W - j, :] * shifted
    tail_ref[...] = x[rows - (CONV_W - 1) * SUBLANES:]
    return out


def _cumsum_interleaved(x, rows):
    tiles = [_tile(x, 0)]
    for i in range(1, rows // SUBLANES):
        tiles.append(tiles[-1] + _tile(x, i))
    total = tiles[-1]
    start = _tile_cumsum(total) - total
    return jnp.concatenate([t + start for t in tiles], axis=0)


def _cummax_interleaved(x, rows):
    tiles = [_tile(x, 0)]
    for i in range(1, rows // SUBLANES):
        tiles.append(jnp.maximum(tiles[-1], _tile(x, i)))
    best = tiles[-1]
    pos = lax.broadcasted_iota(jnp.int32, best.shape, 0)
    s = 1
    while s < SUBLANES:
        best = jnp.maximum(best, jnp.where(pos >= s, pltpu.roll(best, s, 0), -jnp.inf))
        s *= 2
    start = jnp.where(pos == 0, -jnp.inf, pltpu.roll(best, 1, 0))
    return jnp.concatenate([jnp.maximum(t, start) for t in tiles], axis=0)


def _scan_real_interleaved(a_ref, b_ref, carry, rows):
    ntiles = rows // SUBLANES
    pos = lax.broadcasted_iota(jnp.int32, (SUBLANES, a_ref.shape[1]), 0)
    h = _tile(b_ref, 0)
    aprod = _tile(a_ref, 0)
    for i in range(1, ntiles):
        a = _tile(a_ref, i)
        h = a * h + _tile(b_ref, i)
        aprod = aprod * a
    g = _tile_scan_real(aprod, h + jnp.where(pos == 0, aprod * carry, 0.0))
    h = jnp.where(pos == 0, carry, pltpu.roll(g, 1, 0))
    for i in range(ntiles):
        h = _tile(a_ref, i) * h + _tile(b_ref, i)
        b_ref[i * SUBLANES:(i + 1) * SUBLANES, :] = h
    return g[SUBLANES - 1:SUBLANES]


def _scan_cplx_interleaved(re_ref, im_ref, p_re, p_im, cre_ref, cim_ref, rows):
    ntiles = rows // SUBLANES
    width = re_ref.shape[1]
    pos = lax.broadcasted_iota(jnp.int32, (SUBLANES, SCAN_LANE_BLOCK), 0)
    for blk in range(width // SCAN_LANE_BLOCK):
        sl = slice(blk * SCAN_LANE_BLOCK, (blk + 1) * SCAN_LANE_BLOCK)
        pr = jnp.broadcast_to(p_re[:, sl], pos.shape)
        pi = jnp.broadcast_to(p_im[:, sl], pos.shape)
        c_r = cre_ref[:, sl]
        c_i = cim_ref[:, sl]
        sr = re_ref[0:SUBLANES, sl]
        si = im_ref[0:SUBLANES, sl]
        for i in range(1, ntiles):
            rs = slice(i * SUBLANES, (i + 1) * SUBLANES)
            sr, si = pr * sr - pi * si + re_ref[rs, sl], pr * si + pi * sr + im_ref[rs, sl]
        qr, qi = pr, pi
        n = 1
        while n < ntiles:
            qr, qi = qr * qr - qi * qi, 2.0 * (qr * qi)
            n *= 2
        gr, gi = _tile_scan_cplx(sr + jnp.where(pos == 0, qr * c_r - qi * c_i, 0.0),
                                 si + jnp.where(pos == 0, qr * c_i + qi * c_r, 0.0), qr, qi)
        sr = jnp.where(pos == 0, c_r, pltpu.roll(gr, 1, 0))
        si = jnp.where(pos == 0, c_i, pltpu.roll(gi, 1, 0))
        for i in range(ntiles):
            rs = slice(i * SUBLANES, (i + 1) * SUBLANES)
            sr, si = pr * sr - pi * si + re_ref[rs, sl], pr * si + pi * sr + im_ref[rs, sl]
            re_ref[rs, sl] = sr
            im_ref[rs, sl] = si
        cre_ref[:, sl] = gr[SUBLANES - 1:SUBLANES]
        cim_ref[:, sl] = gi[SUBLANES - 1:SUBLANES]


def _prompt_mixer_kernel(*refs, rows, nchunk, n_alias):
    (xnext_ref, xfirst_ref, vec_ref, win_ref,
     wqk_ref, wvo_ref, wif_ref, rwa_ref, rwx_ref, wbu_ref, wcre_ref, wcim_ref, wglu_ref) = refs[:13]
    (mixed_ref, c_ref, n_ref, m_ref, mtail_ref, h_ref, rtail_ref, sre_ref, sim_ref,
     q_scr, k_scr, v_scr, o_scr, ra_scr, rb_scr, ure_scr, uim_scr, proj_scr, xn_scr) = refs[13 + n_alias:]
    T = rows
    step = pl.program_id(0)
    slot = lax.rem(step, 2)
    proj_ref = proj_scr.at[slot]
    next_proj_ref = proj_scr.at[1 - slot]
    slab = (2 * MIX_WIDTH) // M_HEADS

    @pl.when(step == 0)
    def _():
        xn = _rmsnorm_bf16(xfirst_ref[...], vec_ref)
        for j in range(M_HEADS):
            cols = slice(j * slab, (j + 1) * slab)
            proj_scr[0, :, cols] = jnp.dot(xn, win_ref[:, cols], preferred_element_type=F32)

    @pl.when(lax.rem(step, nchunk) == 0)
    def _():
        for ref in (c_ref, n_ref, m_ref, mtail_ref, h_ref, rtail_ref, sre_ref, sim_ref):
            ref[...] = jnp.zeros(ref.shape, F32)

    xn_scr[...] = _rmsnorm_bf16(xnext_ref[...], vec_ref)

    xm = proj_ref[:, 0:M_WIDTH]
    xc = _silu(_conv_interleaved(xm, mtail_ref, _vec(vec_ref, "mcw"), _vec(vec_ref, "mcb"), T))
    gates = _mlstm_project(xc, xm, vec_ref, wqk_ref, wvo_ref, wif_ref, q_scr, k_scr, v_scr, o_scr)
    log_i = gates[:, :LANES]
    b_all = _cumsum_interleaved(_log_sigmoid(gates[:, LANES:]), T)
    a_max = _cummax_interleaved(log_i - b_all, T)
    m_prev = m_ref[...]
    log_inter = b_all + m_prev
    m_t_all = jnp.maximum(log_inter, b_all + a_max)
    w_inter_all = jnp.exp(log_inter - m_t_all)
    floor_all = jnp.exp(-m_t_all)
    b_end = b_all[T - 1:T, :]
    m_new = jnp.maximum(b_end + m_prev, b_end + a_max[T - 1:T, :])
    w_src_all = jnp.exp(b_end - b_all + log_i - m_new)
    decay_all = jnp.exp(b_end + m_prev - m_new)
    m_ref[...] = m_new
    b_t = b_all.T
    li_t = log_i.T

    rowi = lax.broadcasted_iota(jnp.int32, (T, T), 0)
    coli = lax.broadcasted_iota(jnp.int32, (T, T), 1)
    sub_len = T // SUBLANES

    def time_of(r):
        return (r & (SUBLANES - 1)) * sub_len + (r >> SUBLANE_SHIFT)

    causal = time_of(coli) <= time_of(rowi)
    k_scale = M_HEAD_DIM ** -0.5
    mnw = _vec(vec_ref, "mnw")
    mskip = _vec(vec_ref, "mskip")

    for h in range(M_HEADS):
        sl = slice(h * M_HEAD_DIM, (h + 1) * M_HEAD_DIM)
        c_prev = c_ref[h]
        n_prev = n_ref[h:h + 1, :]
        w_inter = w_inter_all[:, h:h + 1]
        w_src = w_src_all[:, h:h + 1]
        decay = decay_all[:, h:h + 1]

        q = q_scr[:, sl]
        ks = k_scr[:, sl] * k_scale
        v = v_scr[:, sl]
        q_b = q.astype(BF16)
        ks_b = ks.astype(BF16)

        log_d = jnp.where(causal, b_all[:, h:h + 1] - b_t[h:h + 1, :] + li_t[h:h + 1, :], -jnp.inf)
        w_intra = jnp.exp(log_d - m_t_all[:, h:h + 1])
        s = lax.dot_general(q_b, ks_b, _NT, preferred_element_type=F32) * w_intra
        inter = lax.dot_general(q_b, c_prev.astype(BF16), _NT, preferred_element_type=F32)
        num = jnp.dot(s.astype(BF16), v.astype(BF16), preferred_element_type=F32) + w_inter * inter
        den = (jnp.sum(s, axis=1, keepdims=True)
               + w_inter * jnp.sum(q * n_prev, axis=1, keepdims=True))
        hh = num / jnp.maximum(jnp.abs(den), floor_all[:, h:h + 1])

        vw_t = (v * w_src).T.astype(BF16)
        c_ref[h] = decay * c_prev + jnp.dot(vw_t, ks_b, preferred_element_type=F32)
        n_ref[h:h + 1, :] = decay * n_prev + jnp.sum(ks * w_src, axis=0, keepdims=True)

        z = proj_ref[:, MIX_WIDTH + h * M_HEAD_DIM:MIX_WIDTH + (h + 1) * M_HEAD_DIM]
        mixed_ref[:, sl] = _head_output(hh, o_scr[:, sl], xc[:, sl], z, mnw[:, sl], mskip[:, sl]).astype(BF16)

        cols = slice(h * slab, (h + 1) * slab)
        next_proj_ref[:, cols] = jnp.dot(xn_scr[...], win_ref[:, cols], preferred_element_type=F32)

    xr = proj_ref[:, M_WIDTH:M_WIDTH + R_WIDTH]
    xcr = _conv_interleaved(xr, rtail_ref, _vec(vec_ref, "rcw"), _vec(vec_ref, "rcb"), T)
    a, bb = _rglru_coeffs(xcr, vec_ref, rwa_ref, rwx_ref)
    ra_scr[...] = a
    rb_scr[...] = bb
    h_ref[...] = _scan_real_interleaved(ra_scr, rb_scr, h_ref[...], T)
    zr = proj_ref[:, MIX_WIDTH + M_WIDTH:MIX_WIDTH + M_WIDTH + R_WIDTH]
    mixed_ref[:, M_WIDTH:M_WIDTH + R_WIDTH] = (rb_scr[...] * _silu(zr)).astype(BF16)

    u = proj_ref[:, M_WIDTH + R_WIDTH:MIX_WIDTH]
    _s5_project(u.astype(BF16), wbu_ref, ure_scr, uim_scr)
    _scan_cplx_interleaved(ure_scr, uim_scr, _vec(vec_ref, "abre"), _vec(vec_ref, "abim"), sre_ref, sim_ref, T)
    glu = _s5_output(ure_scr, uim_scr, u, vec_ref, wcre_ref, wcim_ref, wglu_ref)
    zs = proj_ref[:, MIX_WIDTH + M_WIDTH + R_WIDTH:2 * MIX_WIDTH]
    mixed_ref[:, M_WIDTH + R_WIDTH:MIX_WIDTH] = (glu * _silu(zs)).astype(BF16)


_PROMPT_STATE_SHAPES = ((M_HEADS, M_HEAD_DIM, M_HEAD_DIM), (M_HEADS, M_HEAD_DIM), (1, LANES),
                        ((CONV_W - 1) * SUBLANES, M_WIDTH), (1, R_WIDTH), ((CONV_W - 1) * SUBLANES, R_WIDTH),
                        (1, S_LANES), (1, S_LANES))


def _prompt_mixer(x, total_rows, l, bsz, seq, vec, w_in_b, mats, prev_states):
    rows = PROMPT_CHUNK
    nchunk = seq // rows
    nstep = bsz * nchunk
    aliases = {}
    alias_args = []
    n_in = 4 + len(mats)
    if prev_states is not None:
        alias_args = list(prev_states)
        aliases = {n_in + k: 1 + k for k in range(len(alias_args))}

    def state_spec(shape):
        nd = len(shape)
        return pl.BlockSpec((None, None) + shape, lambda t: (l, t // nchunk) + (0,) * nd)

    out_shape = ([jax.ShapeDtypeStruct((total_rows, MIX_WIDTH), BF16)]
                 + [jax.ShapeDtypeStruct((DEPTH, bsz) + s, F32) for s in _PROMPT_STATE_SHAPES])
    scratch = ([pltpu.VMEM((rows, w), F32)
                for w in (M_WIDTH, M_WIDTH, M_WIDTH, M_WIDTH, R_WIDTH, R_WIDTH, S_LANES, S_LANES)]
               + [pltpu.VMEM((2, rows, 2 * MIX_WIDTH), F32), pltpu.VMEM((rows, D_MODEL), BF16)])
    return pl.pallas_call(
        functools.partial(_prompt_mixer_kernel, rows=rows, nchunk=nchunk, n_alias=len(alias_args)),
        grid=(nstep,),
        in_specs=([pl.BlockSpec((rows, D_MODEL), lambda t: (jnp.minimum(t + 1, nstep - 1), 0)),
                   pl.BlockSpec((rows, D_MODEL), lambda t: (0, 0), pipeline_mode=pl.Buffered(1)),
                   _layer_spec(vec, l), _layer_spec(w_in_b, l)]
                  + [_layer_spec(w, l) for w in mats] + [_ANY_SPEC] * len(alias_args)),
        out_specs=([pl.BlockSpec((rows, MIX_WIDTH), lambda t: (t, 0))]
                   + [state_spec(s) for s in _PROMPT_STATE_SHAPES]),
        out_shape=out_shape,
        scratch_shapes=scratch,
        input_output_aliases=aliases,
        compiler_params=pltpu.CompilerParams(
            dimension_semantics=("arbitrary",), vmem_limit_bytes=VMEM_LIMIT_BYTES),
        name="prompt_mixer",
    )(x, x, vec, w_in_b, *mats, *alias_args)


def _seg_last(x, groups):
    x3 = x.reshape(groups, SUBLANES, x.shape[-1])
    return jnp.broadcast_to(x3[:, SUBLANES - 1:SUBLANES, :], x3.shape).reshape(x.shape)


def _seg_max(x, groups):
    x3 = x.reshape(groups, SUBLANES, x.shape[-1])
    return jnp.broadcast_to(jnp.max(x3, axis=1, keepdims=True), x3.shape).reshape(x.shape)


def _seg_sum(x, groups):
    x3 = x.reshape(groups, SUBLANES, x.shape[-1])
    return jnp.broadcast_to(jnp.sum(x3, axis=1, keepdims=True), x3.shape).reshape(x.shape)


def _seg_rows(state_ref, lanes=slice(None)):
    x = state_ref[:, :, lanes]
    return jnp.broadcast_to(x, (x.shape[0], SUBLANES, x.shape[2])).reshape(x.shape[0] * SUBLANES, x.shape[2])


def _seg_state(x):
    x3 = x.reshape(x.shape[0] // SUBLANES, SUBLANES, x.shape[-1])
    return x3[:, SUBLANES - 1:SUBLANES, :]


def _conv_rolled(xf, w, bias):
    out = w[CONV_W - 1:CONV_W, :] * xf + bias
    for j in range(1, CONV_W):
        out = out + w[CONV_W - 1 - j:CONV_W - j, :] * pltpu.roll(xf, j, 0)
    return out


_SAMPLE_STATE_WIDTHS = (M_WIDTH, LANES, M_WIDTH, R_WIDTH, R_WIDTH, S_LANES, S_LANES)
_SAMPLE_STATE_PER_ROW = (False, False, True, False, True, False, False)


def _sample_mixer_kernel(*refs, rows, c_seqs, n_alias):
    (proj_ref, c0_ref, n0_ref, m0_ref, mtail0_ref, h0_ref, rtail0_ref, sre0_ref, sim0_ref,
     vec_ref, wqk_ref, wvo_ref, wif_ref, rwa_ref, rwx_ref, wbu_ref, wcre_ref, wcim_ref, wglu_ref) = refs[:19]
    (mixed_ref, c_ref, n_ref, m_ref, xmf_ref, h_ref, xrf_ref, sre_ref, sim_ref,
     q_scr, k_scr, v_scr, o_scr, xc_scr, intert_scr, vwt_scr, qt_scr, ksb_scr, dec_scr, b_scr, li_scr,
     ure_scr, uim_scr) = refs[19 + n_alias:]
    R = rows
    G = R // SUBLANES
    sub = pl.program_id(1)
    k_scale = M_HEAD_DIM ** -0.5
    pos = _sublane_pos((R, 1))
    valid = pos >= SAMPLE_LEAD
    first_token = pos == SAMPLE_LEAD

    @pl.when(sub == 0)
    def _():
        xm = jnp.where(valid, proj_ref[:, 0:M_WIDTH], mtail0_ref[...])
        xmf_ref[...] = xm
        xc = _silu(_conv_rolled(xm, _vec(vec_ref, "mcw"), _vec(vec_ref, "mcb")))
        xc_scr[...] = xc
        gates = _mlstm_project(xc, xm, vec_ref, wqk_ref, wvo_ref, wif_ref, q_scr, k_scr, v_scr, o_scr)
        log_i = jnp.where(valid, gates[:, :LANES], -jnp.inf)
        log_f = jnp.where(valid, _log_sigmoid(gates[:, LANES:]), 0.0)
        b_all = _tile_cumsum(log_f)
        b_end = _seg_last(b_all, G)
        m_rows = _seg_rows(m0_ref)
        n_rows = _seg_rows(n0_ref)
        log_src = b_end - b_all + log_i
        m_new = jnp.maximum(b_end + m_rows, _seg_max(log_src, G))
        w_src_all = jnp.exp(log_src - m_new)
        decay_all = jnp.exp(b_end + m_rows - m_new)
        dec_scr[...] = decay_all
        b_scr[...] = b_all
        li_scr[...] = log_i
        m_ref[...] = _seg_state(m_new)
        for h in range(M_HEADS):
            sl = slice(h * M_HEAD_DIM, (h + 1) * M_HEAD_DIM)
            ks = k_scr[:, sl] * k_scale
            w_src = w_src_all[:, h:h + 1]
            n_ref[:, :, sl] = _seg_state(decay_all[:, h:h + 1] * n_rows[:, sl] + _seg_sum(ks * w_src, G))
            vwt_scr[h] = (v_scr[:, sl] * w_src).T
            qt_scr[h] = q_scr[:, sl].T.astype(BF16)
            ksb_scr[h] = ks.astype(BF16)
            intert_scr[h] = jnp.zeros((M_HEAD_DIM, R), F32)

        xr = jnp.where(valid, proj_ref[:, M_WIDTH:M_WIDTH + R_WIDTH], rtail0_ref[...])
        xrf_ref[...] = xr
        xcr = _conv_rolled(xr, _vec(vec_ref, "rcw"), _vec(vec_ref, "rcb"))
        a, bb = _rglru_coeffs(xcr, vec_ref, rwa_ref, rwx_ref)
        hs = _tile_scan_real(a, jnp.where(valid, bb, 0.0) + jnp.where(first_token, a * _seg_rows(h0_ref), 0.0))
        h_ref[...] = _seg_state(hs)
        zr = proj_ref[:, MIX_WIDTH + M_WIDTH:MIX_WIDTH + M_WIDTH + R_WIDTH]
        mixed_ref[:, M_WIDTH:M_WIDTH + R_WIDTH] = jnp.where(valid, hs * _silu(zr), 0.0).astype(BF16)

        u = proj_ref[:, M_WIDTH + R_WIDTH:MIX_WIDTH]
        _s5_project(u.astype(BF16), wbu_ref, ure_scr, uim_scr)
        p_re = _vec(vec_ref, "abre")
        p_im = _vec(vec_ref, "abim")
        for blk in range(S_LANES // SCAN_LANE_BLOCK):
            sl = slice(blk * SCAN_LANE_BLOCK, (blk + 1) * SCAN_LANE_BLOCK)
            pr = jnp.broadcast_to(p_re[:, sl], (R, SCAN_LANE_BLOCK))
            pi = jnp.broadcast_to(p_im[:, sl], (R, SCAN_LANE_BLOCK))
            s0r = _seg_rows(sre0_ref, sl)
            s0i = _seg_rows(sim0_ref, sl)
            sr, si = _tile_scan_cplx(
                jnp.where(valid, ure_scr[:, sl], 0.0) + jnp.where(first_token, pr * s0r - pi * s0i, 0.0),
                jnp.where(valid, uim_scr[:, sl], 0.0) + jnp.where(first_token, pr * s0i + pi * s0r, 0.0), pr, pi)
            ure_scr[:, sl] = sr
            uim_scr[:, sl] = si
            sre_ref[:, :, sl] = _seg_state(sr)
            sim_ref[:, :, sl] = _seg_state(si)
        glu = _s5_output(ure_scr, uim_scr, u, vec_ref, wcre_ref, wcim_ref, wglu_ref)
        zs = proj_ref[:, MIX_WIDTH + M_WIDTH + R_WIDTH:2 * MIX_WIDTH]
        mixed_ref[:, M_WIDTH + R_WIDTH:MIX_WIDTH] = jnp.where(valid, glu * _silu(zs), 0.0).astype(BF16)

    lane_seq = lax.broadcasted_iota(jnp.int32, (M_HEAD_DIM, R), 1) >> SUBLANE_SHIFT
    seq0 = sub * c_seqs
    for h in range(M_HEADS):
        c_old = c0_ref[:, h].reshape(c_seqs * M_HEAD_DIM, M_HEAD_DIM)
        readout = jnp.dot(c_old.astype(BF16), qt_scr[h], preferred_element_type=F32)
        acc = intert_scr[h]
        vwt = vwt_scr[h]
        lhs = []
        for s in range(c_seqs):
            own = lane_seq == seq0 + s
            acc = jnp.where(own, readout[s * M_HEAD_DIM:(s + 1) * M_HEAD_DIM], acc)
            lhs.append(jnp.where(own, vwt, 0.0))
        intert_scr[h] = acc
        upd = jnp.dot(jnp.concatenate(lhs, axis=0).astype(BF16), ksb_scr[h], preferred_element_type=F32)
        for s in range(c_seqs):
            r0 = pl.multiple_of((seq0 + s) * SUBLANES, SUBLANES)
            c_ref[s, h] = (dec_scr[pl.ds(r0, 1), h:h + 1] * c0_ref[s, h]
                           + upd[s * M_HEAD_DIM:(s + 1) * M_HEAD_DIM])

    @pl.when(sub == pl.num_programs(1) - 1)
    def _():
        _sample_heads(proj_ref, m0_ref, n0_ref, vec_ref, mixed_ref,
                      q_scr, k_scr, v_scr, o_scr, xc_scr, intert_scr, b_scr, li_scr, valid, R)


def _sample_heads(proj_ref, m0_ref, n0_ref, vec_ref, mixed_ref,
                  q_scr, k_scr, v_scr, o_scr, xc_scr, intert_scr, b_scr, li_scr, valid, R):
    rowi = lax.broadcasted_iota(jnp.int32, (R, R), 0)
    coli = lax.broadcasted_iota(jnp.int32, (R, R), 1)
    same_causal = jnp.logical_and(coli <= rowi, (coli >> SUBLANE_SHIFT) == (rowi >> SUBLANE_SHIFT))
    k_scale = M_HEAD_DIM ** -0.5
    mnw = _vec(vec_ref, "mnw")
    mskip = _vec(vec_ref, "mskip")
    n_rows = _seg_rows(n0_ref)
    b_all = b_scr[...]
    log_i = li_scr[...]
    log_inter = b_all + _seg_rows(m0_ref)
    m_t_all = jnp.maximum(log_inter, b_all + _tile_cummax(log_i - b_all))
    w_inter_all = jnp.exp(log_inter - m_t_all)
    floor_all = jnp.exp(-m_t_all)
    b_t = b_all.T
    li_t = log_i.T

    for h in range(M_HEADS):
        sl = slice(h * M_HEAD_DIM, (h + 1) * M_HEAD_DIM)
        q = q_scr[:, sl]
        ks = k_scr[:, sl] * k_scale
        v = v_scr[:, sl]
        q_b = q.astype(BF16)
        ks_b = ks.astype(BF16)
        w_inter = w_inter_all[:, h:h + 1]

        log_d = jnp.where(same_causal, b_all[:, h:h + 1] - b_t[h:h + 1, :] + li_t[h:h + 1, :], -jnp.inf)
        w_intra = jnp.exp(log_d - m_t_all[:, h:h + 1])
        s = lax.dot_general(q_b, ks_b, _NT, preferred_element_type=F32) * w_intra
        num = (jnp.dot(s.astype(BF16), v.astype(BF16), preferred_element_type=F32)
               + w_inter * intert_scr[h].T)
        den = (jnp.sum(s, axis=1, keepdims=True)
               + w_inter * jnp.sum(q * n_rows[:, sl], axis=1, keepdims=True))
        hh = num / jnp.maximum(jnp.abs(den), floor_all[:, h:h + 1])

        z = proj_ref[:, MIX_WIDTH + h * M_HEAD_DIM:MIX_WIDTH + (h + 1) * M_HEAD_DIM]
        out = _head_output(hh, o_scr[:, sl], xc_scr[:, sl], z, mnw[:, sl], mskip[:, sl])
        mixed_ref[:, sl] = jnp.where(valid, out, 0.0).astype(BF16)


def _sample_mixer(proj, l, row0, mixed_prev, c_all, row_state, vec, mats, prev_out):
    R = SAMPLE_ROWS
    nrows = proj.shape[0]
    nsub = (R // SUBLANES) // SAMPLE_C_SEQS
    blk0 = row0 // R
    c_spec = pl.BlockSpec((None, SAMPLE_C_SEQS, M_HEADS, M_HEAD_DIM, M_HEAD_DIM),
                          lambda i, j: (l, i * nsub + j, 0, 0, 0))

    def state_spec(width, per_row):
        if per_row:
            return pl.BlockSpec((None, R, width), lambda i, j: (l, i, 0))
        return pl.BlockSpec((None, R // SUBLANES, 1, width), lambda i, j: (l, i, 0, 0))

    def state_shape(width, per_row):
        return (DEPTH, nrows, width) if per_row else (DEPTH, nrows // SUBLANES, 1, width)

    state_specs = [state_spec(w, r) for w, r in zip(_SAMPLE_STATE_WIDTHS, _SAMPLE_STATE_PER_ROW)]
    alias_args = [mixed_prev] + (list(prev_out) if prev_out is not None else [])
    first_alias = 2 + len(row_state) + 1 + len(mats)
    aliases = {first_alias + k: k for k in range(len(alias_args))}
    out_shape = ([jax.ShapeDtypeStruct(mixed_prev.shape, BF16), jax.ShapeDtypeStruct(c_all.shape, F32)]
                 + [jax.ShapeDtypeStruct(state_shape(w, r), F32)
                    for w, r in zip(_SAMPLE_STATE_WIDTHS, _SAMPLE_STATE_PER_ROW)])
    scratch = ([pltpu.VMEM((R, M_WIDTH), F32)] * 5
               + [pltpu.VMEM((M_HEADS, M_HEAD_DIM, R), F32), pltpu.VMEM((M_HEADS, M_HEAD_DIM, R), F32),
                  pltpu.VMEM((M_HEADS, M_HEAD_DIM, R), BF16), pltpu.VMEM((M_HEADS, R, M_HEAD_DIM), BF16)]
               + [pltpu.VMEM((R, LANES), F32)] * 3
               + [pltpu.VMEM((R, S_LANES), F32)] * 2)
    return pl.pallas_call(
        functools.partial(_sample_mixer_kernel, rows=R, c_seqs=SAMPLE_C_SEQS, n_alias=len(alias_args)),
        grid=(nrows // R, nsub),
        in_specs=([pl.BlockSpec((R, 2 * MIX_WIDTH), lambda i, j: (i, 0)), c_spec] + state_specs
                  + [_layer_spec(vec, l)] + [_layer_spec(w, l) for w in mats] + [_ANY_SPEC] * len(alias_args)),
        out_specs=[pl.BlockSpec((R, MIX_WIDTH), lambda i, j: (blk0 + i, 0)), c_spec] + state_specs,
        out_shape=out_shape,
        scratch_shapes=scratch,
        input_output_aliases=aliases,
        compiler_params=pltpu.CompilerParams(
            dimension_semantics=("parallel", "arbitrary"), vmem_limit_bytes=VMEM_LIMIT_BYTES),
        name="sample_mixer",
    )(proj, c_all, *row_state, vec, *mats, *alias_args)


def _prepare_weights(p, s5):
    abre, abim, bbre_t, bbim_t = s5

    def split_gates(g):
        pad = [(0, 0)] * (g.ndim - 1) + [(0, LANES - M_HEADS)]
        return jnp.concatenate([jnp.pad(g[..., :M_HEADS], pad), jnp.pad(g[..., M_HEADS:], pad)], axis=-1)

    table = {"mcw": p["m_conv_w"], "rcw": p["r_conv_w"], "mcb": p["m_conv_b"], "mnw": p["m_norm_w"],
             "mskip": p["m_skip"], "rcb": p["r_conv_b"], "rba": p["r_ba"], "rbx": p["r_bx"], "rlam": p["r_lam"],
             "sd": p["s_d"], "bglu": p["s_b_glu"], "bif": split_gates(p["m_b_if"]), "abre": abre, "abim": abim,
             "norm_w": p["norm_w"]}

    def group_rows(group):
        parts = [table[name].reshape(DEPTH, n, w) for name, n, w in group]
        fill = _VEC_WIDTH - sum(w for _, _, w in group)
        if fill:
            parts.append(jnp.zeros((DEPTH, group[0][1], fill), F32))
        return parts[0] if len(parts) == 1 else jnp.concatenate(parts, axis=-1)

    vec = jnp.concatenate([group_rows(g) for g in _VEC_GROUPS]
                          + [jnp.zeros((DEPTH, _VEC_ROWS - _VEC_USED_ROWS, _VEC_WIDTH), F32)], axis=1)

    def block_diag_halves(w):
        nb = R_BLOCKS // 2
        w5 = w.reshape(DEPTH, 2, nb, w.shape[-2], w.shape[-1])
        eye = jnp.eye(nb, dtype=F32)[:, None, :, None]
        return (w5[:, :, :, :, None, :] * eye).reshape(
            DEPTH, 2, nb * w.shape[-2], nb * w.shape[-1]).astype(BF16)

    n_in = S_WIDTH // LANES
    g_in = LANES // S_GROUP
    n_out = 2
    g_out = S_GROUPS // n_out

    def bu_blocks(bb_t):
        bb5 = bb_t.reshape(DEPTH, S_GROUP, n_in, g_in, S_STATE)
        return jnp.einsum("ab,dckbp->dkacbp", jnp.eye(g_in, dtype=F32), bb5).reshape(
            DEPTH, n_in, LANES, g_in * S_STATE)

    def c_blocks(c):
        c5 = c.reshape(DEPTH, n_out, g_out, S_GROUP, S_STATE)
        eye = jnp.eye(g_out, dtype=F32)[:, None, :, None]
        return (c5[:, :, :, :, None, :] * eye).reshape(
            DEPTH, n_out, g_out * S_GROUP, g_out * S_STATE).astype(BF16)

    mats = {
        "wqk": jnp.concatenate([p["m_wq"], p["m_wk"]], axis=-1).astype(BF16),
        "wvo": jnp.concatenate([p["m_wv"], p["m_wo"]], axis=-1).astype(BF16),
        "wif": split_gates(p["m_w_if"]).astype(BF16),
        "rwa": block_diag_halves(p["r_wa"]), "rwx": block_diag_halves(p["r_wx"]),
        "wbu": jnp.concatenate([bu_blocks(bbre_t), bu_blocks(bbim_t)], axis=-1).astype(BF16),
        "wcre": c_blocks(p["s_c_re"]), "wcim": c_blocks(p["s_c_im"]),
        "wglu": p["s_w_glu"].astype(BF16),
    }
    return vec, p["w_in"].astype(BF16), p["w_out"].astype(BF16), [mats[k] for k in _MATRIX_KEYS]


def _sample_rows_state(n, m, mconv, h, rconv, sre, sim):
    bsz = n.shape[1]

    def per_seq(x):
        return x.reshape(DEPTH, bsz, 1, x.shape[-1])

    def conv_rows(buf):
        padded = jnp.pad(buf, ((0, 0), (0, 0), (SAMPLE_LEAD - (CONV_W - 1), SUBLANES - SAMPLE_LEAD), (0, 0)))
        return padded.reshape(DEPTH, bsz * SUBLANES, buf.shape[-1])

    m_pad = jnp.pad(m, ((0, 0), (0, 0), (0, LANES - M_HEADS)))
    return (per_seq(n.reshape(DEPTH, bsz, M_WIDTH)), per_seq(m_pad), conv_rows(mconv), per_seq(h), conv_rows(rconv),
            per_seq(sre.reshape(DEPTH, bsz, S_LANES)), per_seq(sim.reshape(DEPTH, bsz, S_LANES)))


def _sample_state_from_rows(n, m, xmf, h, xrf, sre, sim):
    bsz = n.shape[1]
    conv_from = SUBLANES - (CONV_W - 1)

    def conv_state(x):
        return x.reshape(DEPTH, bsz, SUBLANES, x.shape[-1])[:, :, conv_from:]

    return (n.reshape(DEPTH, bsz, M_HEADS, M_HEAD_DIM), m.reshape(DEPTH, bsz, LANES)[..., :M_HEADS],
            conv_state(xmf), h.reshape(DEPTH, bsz, R_WIDTH), conv_state(xrf),
            sre.reshape(DEPTH, bsz, S_GROUPS, S_STATE), sim.reshape(DEPTH, bsz, S_GROUPS, S_STATE))


def _prompt_state_from_kernel(c, n, m, mtail, h, rtail, sre, sim):
    bsz = c.shape[1]
    return (c, n, m[:, :, 0, :M_HEADS], mtail[:, :, SUBLANES - 1::SUBLANES], h.reshape(DEPTH, bsz, R_WIDTH),
            rtail[:, :, SUBLANES - 1::SUBLANES],
            sre.reshape(DEPTH, bsz, S_GROUPS, S_STATE), sim.reshape(DEPTH, bsz, S_GROUPS, S_STATE))


def _interleave_chunks(x, inverse=False):
    bsz, seq, d = x.shape
    sub_len = PROMPT_CHUNK // SUBLANES
    inner = (sub_len, SUBLANES) if inverse else (SUBLANES, sub_len)
    return x.reshape(bsz, seq // PROMPT_CHUNK, *inner, d).transpose(0, 1, 3, 2, 4).reshape(bsz, seq, d)


def kernel(x_prompt, x_sample, state_mlstm_C, state_mlstm_n, state_mlstm_m, state_mlstm_conv, state_rglru_h, state_rglru_conv, state_s5_re, state_s5_im, norm_w, w_in, w_out, m_conv_w, m_conv_b, m_wq, m_wk, m_wv, m_wo, m_w_if, m_b_if, m_norm_w, m_skip, r_conv_w, r_conv_b, r_wa, r_ba, r_wx, r_bx, r_lam, s_lam_re, s_lam_im, s_b_re, s_b_im, s_c_re, s_c_im, s_d, s_log_step, s_w_glu, s_b_glu, final_norm_w):
    p = dict(norm_w=norm_w, w_in=w_in, w_out=w_out, m_conv_w=m_conv_w, m_conv_b=m_conv_b, m_wq=m_wq, m_wk=m_wk,
             m_wv=m_wv, m_wo=m_wo, m_w_if=m_w_if, m_b_if=m_b_if, m_norm_w=m_norm_w, m_skip=m_skip,
             r_conv_w=r_conv_w, r_conv_b=r_conv_b, r_wa=r_wa, r_ba=r_ba, r_wx=r_wx, r_bx=r_bx, r_lam=r_lam,
             s_c_re=s_c_re, s_c_im=s_c_im, s_d=s_d, s_w_glu=s_w_glu, s_b_glu=s_b_glu)
    s5 = _s5_prep(s_lam_re, s_lam_im, s_log_step, s_b_re, s_b_im)
    vec, w_in_b, w_out_b, mats = _prepare_weights(p, s5)
    final_w = final_norm_w.reshape(1, D_MODEL)

    bsz, seq, _ = x_prompt.shape
    dec_batch, dec_seq, _ = x_sample.shape
    n_prompt = bsz * seq
    n_sample = dec_batch * SUBLANES
    stream = [_interleave_chunks(x_prompt).reshape(n_prompt, D_MODEL),
              jnp.pad(x_sample, ((0, 0), (SAMPLE_LEAD, 0), (0, 0))).reshape(n_sample, D_MODEL)]
    rows_state = _sample_rows_state(state_mlstm_n, state_mlstm_m, state_mlstm_conv, state_rglru_h,
                                    state_rglru_conv, state_s5_re, state_s5_im)
    pr_states = None
    sa_states = None
    for l in range(DEPTH):
        last = l == DEPTH - 1
        mixed, *pr_states = _prompt_mixer(stream[0], n_prompt + n_sample, l, bsz, seq, vec, w_in_b, mats, pr_states)
        proj_s = _inproj(stream[-1], stream[-1].shape[0] - n_sample, n_sample, vec, w_in_b, l)
        mixed, *sa_states = _sample_mixer(proj_s, l, n_prompt, mixed, state_mlstm_C, rows_state, vec, mats, sa_states)
        stream = _outproj(mixed, stream, w_out_b, l, final_w, last, split_rows=n_prompt if last else None)
    y_prompt = _interleave_chunks(stream[0].reshape(bsz, seq, D_MODEL), inverse=True)
    y_sample = stream[1].reshape(dec_batch, SUBLANES, D_MODEL)[:, SAMPLE_LEAD:]
    return (y_prompt, y_sample, *_prompt_state_from_kernel(*pr_states),
            sa_states[0], *_sample_state_from_rows(*sa_states[1:]))
```

```python
import functools

import jax
import jax.numpy as jnp
from jax import lax
from jax.experimental import pallas as pl
from jax.experimental.pallas import tpu as pltpu

F32 = jnp.float32
BF16 = jnp.bfloat16

D_MODEL = 2048
DEPTH = 2
MIX_WIDTH = D_MODEL
M_WIDTH = MIX_WIDTH // 2
R_WIDTH = MIX_WIDTH // 4
S_WIDTH = MIX_WIDTH - M_WIDTH - R_WIDTH
M_HEADS = 8
M_HEAD_DIM = M_WIDTH // M_HEADS
R_BLOCKS = 8
RG_C = 8.0
S_GROUP = 16
S_GROUPS = S_WIDTH // S_GROUP
S_STATE = 64
S_LANES = S_GROUPS * S_STATE
CONV_W = 4
EPS = 1e-6

SUBLANES = 8
SUBLANE_SHIFT = SUBLANES.bit_length() - 1
LANES = 128
VMEM_LIMIT_BYTES = 56 * 1024 * 1024

PROMPT_CHUNK = 256
SAMPLE_ROWS = 128
SAMPLE_C_SEQS = 8
SAMPLE_LEAD = SUBLANES - 4
INPROJ_TM = 512
INPROJ_TN = 1024
OUTPROJ_TM = 512
SCAN_LANE_BLOCK = 256

_NT = (((1,), (1,)), ((), ()))

_VEC_WIDTH = D_MODEL
_VEC_GROUPS = ((("mcw", CONV_W, M_WIDTH), ("rcw", CONV_W, R_WIDTH)),
               (("mcb", 1, M_WIDTH), ("mnw", 1, M_WIDTH)),
               (("mskip", 1, M_WIDTH), ("rcb", 1, R_WIDTH), ("rba", 1, R_WIDTH)),
               (("rbx", 1, R_WIDTH), ("rlam", 1, R_WIDTH), ("sd", 1, S_WIDTH), ("bglu", 1, S_WIDTH)),
               (("bif", 1, 2 * LANES),),
               (("abre", 1, S_LANES),), (("abim", 1, S_LANES),), (("norm_w", 1, D_MODEL),))
_VEC_LAYOUT = {}
_row = 0
for _group in _VEC_GROUPS:
    _lane = 0
    for _name, _n, _w in _group:
        _VEC_LAYOUT[_name] = (_row, _n, _lane, _w)
        _lane += _w
    _row += _group[0][1]
_VEC_USED_ROWS = _row
_VEC_ROWS = -(-_row // SUBLANES) * SUBLANES


def _vec(vec_ref, name):
    r0, n, l0, w = _VEC_LAYOUT[name]
    return vec_ref[r0:r0 + n, l0:l0 + w]


def _layer_spec(arr, l):
    nd = arr.ndim - 1
    return pl.BlockSpec((None,) + arr.shape[1:], lambda *_: (l,) + (0,) * nd, pipeline_mode=pl.Buffered(1))


_ANY_SPEC = pl.BlockSpec(memory_space=pl.ANY)


def _s5_prep_kernel(lr_ref, li_ref, ls_ref, brt_ref, bit_ref, abre_ref, abim_ref, bbre_ref, bbim_ref):
    lr = lr_ref[0]
    li = li_ref[0]
    dt = jnp.exp(ls_ref[0])
    mag = jnp.exp(lr * dt)
    ang = li * dt
    ab_re = mag * jnp.cos(ang)
    ab_im = mag * jnp.sin(ang)
    den = lr * lr + li * li
    nr = ab_re - 1.0
    f_re = (nr * lr + ab_im * li) / den
    f_im = (ab_im * lr - nr * li) / den
    br = brt_ref[0]
    bi = bit_ref[0]
    abre_ref[0] = ab_re
    abim_ref[0] = ab_im
    bbre_ref[0] = f_re * br - f_im * bi
    bbim_ref[0] = f_re * bi + f_im * br


def _s5_prep(s_lam_re, s_lam_im, s_log_step, s_b_re, s_b_im):
    lr = s_lam_re.reshape(DEPTH, 1, S_LANES)
    li = s_lam_im.reshape(DEPTH, 1, S_LANES)
    ls = jnp.repeat(s_log_step, S_STATE, axis=-1).reshape(DEPTH, 1, S_LANES)
    brt = s_b_re.reshape(DEPTH, S_LANES, S_GROUP).transpose(0, 2, 1)
    bit = s_b_im.reshape(DEPTH, S_LANES, S_GROUP).transpose(0, 2, 1)
    vec = pl.BlockSpec((1, 1, S_LANES), lambda l: (l, 0, 0))
    mat = pl.BlockSpec((1, S_GROUP, S_LANES), lambda l: (l, 0, 0))
    return pl.pallas_call(
        _s5_prep_kernel,
        grid=(DEPTH,),
        in_specs=[vec, vec, vec, mat, mat],
        out_specs=[vec, vec, mat, mat],
        out_shape=[jax.ShapeDtypeStruct((DEPTH, 1, S_LANES), F32),
                   jax.ShapeDtypeStruct((DEPTH, 1, S_LANES), F32),
                   jax.ShapeDtypeStruct((DEPTH, S_GROUP, S_LANES), F32),
                   jax.ShapeDtypeStruct((DEPTH, S_GROUP, S_LANES), F32)],
        name="s5_prep",
    )(lr, li, ls, brt, bit)


def _stream_specs(parts, tm):
    if len(parts) == 1:
        return [pl.BlockSpec((tm, D_MODEL), lambda i: (i, 0))]
    nb0 = parts[0].shape[0] // tm
    return [pl.BlockSpec((tm, D_MODEL), lambda i: (jnp.minimum(i, nb0 - 1), 0)),
            pl.BlockSpec((tm, D_MODEL), lambda i: (jnp.maximum(i - nb0, 0), 0))]


def _read_stream(refs, nb0):
    if len(refs) == 1:
        return refs[0][...]
    return jnp.where(pl.program_id(0) < nb0, refs[0][...], refs[1][...])


def _rmsnorm_bf16(x, vec_ref):
    ms = jnp.mean(x * x, axis=-1, keepdims=True)
    return (x * lax.rsqrt(ms + EPS) * _vec(vec_ref, "norm_w")).astype(BF16)


def _inproj_kernel(x_ref, vec_ref, w_ref, o_ref):
    xn = _rmsnorm_bf16(x_ref[...], vec_ref)
    for j in range((2 * MIX_WIDTH) // INPROJ_TN):
        cols = slice(j * INPROJ_TN, (j + 1) * INPROJ_TN)
        o_ref[:, cols] = jnp.dot(xn, w_ref[:, cols], preferred_element_type=F32)


def _inproj(x, row0, nrows, vec, w_in_b, l):
    blk0 = row0 // INPROJ_TM
    return pl.pallas_call(
        _inproj_kernel,
        grid=(nrows // INPROJ_TM,),
        in_specs=[pl.BlockSpec((INPROJ_TM, D_MODEL), lambda i: (blk0 + i, 0)),
                  _layer_spec(vec, l), _layer_spec(w_in_b, l)],
        out_specs=pl.BlockSpec((INPROJ_TM, 2 * MIX_WIDTH), lambda i: (i, 0)),
        out_shape=jax.ShapeDtypeStruct((nrows, 2 * MIX_WIDTH), F32),
        compiler_params=pltpu.CompilerParams(
            dimension_semantics=("parallel",), vmem_limit_bytes=VMEM_LIMIT_BYTES),
        name="inproj",
    )(x, vec, w_in_b)


def _outproj_kernel(*refs, n_x, nb0, n_out, nb0_out, final):
    mixed_ref, x_refs = refs[0], refs[1:1 + n_x]
    w_ref, fw_ref = refs[1 + n_x:3 + n_x]
    o_refs = refs[3 + n_x:]
    y = _read_stream(x_refs, nb0) + jnp.dot(mixed_ref[...], w_ref[...], preferred_element_type=F32)
    if final:
        ms = jnp.mean(y * y, axis=-1, keepdims=True)
        y = y * lax.rsqrt(ms + EPS) * fw_ref[...]
    if n_out == 1:
        o_refs[0][...] = y
    else:
        @pl.when(pl.program_id(0) < nb0_out)
        def _():
            o_refs[0][...] = y

        @pl.when(pl.program_id(0) >= nb0_out)
        def _():
            o_refs[1][...] = y.reshape(y.shape[0] // SUBLANES, SUBLANES, y.shape[1])[:, SAMPLE_LEAD:, :]


def _outproj(mixed, x_parts, w_out_b, l, final_w, final, split_rows=None):
    n = mixed.shape[0]
    if split_rows is None:
        out_parts = [jax.ShapeDtypeStruct((n, D_MODEL), F32)]
        out_specs = [pl.BlockSpec((OUTPROJ_TM, D_MODEL), lambda i: (i, 0))]
        nb0_out = n // OUTPROJ_TM
    else:
        nb0_out = split_rows // OUTPROJ_TM
        tokens = SUBLANES - SAMPLE_LEAD
        out_parts = [jax.ShapeDtypeStruct((split_rows, D_MODEL), F32),
                     jax.ShapeDtypeStruct(((n - split_rows) // SUBLANES, tokens, D_MODEL), F32)]
        out_specs = [pl.BlockSpec((OUTPROJ_TM, D_MODEL), lambda i: (jnp.minimum(i, nb0_out - 1), 0)),
                     pl.BlockSpec((OUTPROJ_TM // SUBLANES, tokens, D_MODEL),
                                  lambda i: (jnp.maximum(i - nb0_out, 0), 0, 0))]
    return pl.pallas_call(
        functools.partial(_outproj_kernel, n_x=len(x_parts), nb0=x_parts[0].shape[0] // OUTPROJ_TM,
                          n_out=len(out_parts), nb0_out=nb0_out, final=final),
        grid=(n // OUTPROJ_TM,),
        in_specs=([pl.BlockSpec((OUTPROJ_TM, MIX_WIDTH), lambda i: (i, 0))] + _stream_specs(x_parts, OUTPROJ_TM)
                  + [_layer_spec(w_out_b, l), pl.BlockSpec((1, D_MODEL), lambda i: (0, 0))]),
        out_specs=out_specs,
        out_shape=out_parts,
        compiler_params=pltpu.CompilerParams(
            dimension_semantics=("arbitrary",), vmem_limit_bytes=VMEM_LIMIT_BYTES),
        name="outproj",
    )(mixed, *x_parts, w_out_b, final_w)


def _silu(x):
    return x * jax.nn.sigmoid(x)


def _log_sigmoid(x):
    return jnp.minimum(x, 0.0) - jnp.log1p(jnp.exp(-jnp.abs(x)))


def _softplus(x):
    return jnp.maximum(x, 0.0) + jnp.log1p(jnp.exp(-jnp.abs(x)))


def _sublane_pos(shape):
    return lax.broadcasted_iota(jnp.int32, shape, 0) & (SUBLANES - 1)


def _tile_cumsum(x):
    pos = _sublane_pos(x.shape)
    s = 1
    while s < SUBLANES:
        x = x + jnp.where(pos >= s, pltpu.roll(x, s, 0), 0.0)
        s *= 2
    return x


def _tile_cummax(x):
    pos = _sublane_pos(x.shape)
    s = 1
    while s < SUBLANES:
        x = jnp.maximum(x, jnp.where(pos >= s, pltpu.roll(x, s, 0), -jnp.inf))
        s *= 2
    return x


def _tile_scan_real(a, b):
    pos = _sublane_pos(a.shape)
    s = 1
    while s < SUBLANES:
        m = pos >= s
        b = jnp.where(m, a * pltpu.roll(b, s, 0) + b, b)
        if 2 * s < SUBLANES:
            a = jnp.where(m, a * pltpu.roll(a, s, 0), a)
        s *= 2
    return b


def _tile_scan_cplx(sr, si, pr, pi):
    pos = _sublane_pos(sr.shape)
    s = 1
    while s < SUBLANES:
        m = pos >= s
        sr_sh = pltpu.roll(sr, s, 0)
        si_sh = pltpu.roll(si, s, 0)
        sr, si = (jnp.where(m, sr + (pr * sr_sh - pi * si_sh), sr),
                  jnp.where(m, si + (pr * si_sh + pi * sr_sh), si))
        if 2 * s < SUBLANES:
            pr, pi = pr * pr - pi * pi, 2.0 * (pr * pi)
        s *= 2
    return sr, si


def _mlstm_project(xc, xm, vec_ref, wqk_ref, wvo_ref, wif_ref, q_scr, k_scr, v_scr, o_scr):
    xc_b = xc.astype(BF16)
    xm_b = xm.astype(BF16)
    for h in range(M_HEADS):
        sl = slice(h * M_HEAD_DIM, (h + 1) * M_HEAD_DIM)
        qk = jnp.dot(xc_b[:, sl], wqk_ref[h], preferred_element_type=F32)
        vo = jnp.dot(xm_b[:, sl], wvo_ref[h], preferred_element_type=F32)
        q_scr[:, sl] = qk[:, :M_HEAD_DIM]
        k_scr[:, sl] = qk[:, M_HEAD_DIM:]
        v_scr[:, sl] = vo[:, :M_HEAD_DIM]
        o_scr[:, sl] = vo[:, M_HEAD_DIM:]
    return (jnp.dot(q_scr[...].astype(BF16), wif_ref[0:M_WIDTH, :], preferred_element_type=F32)
            + jnp.dot(k_scr[...].astype(BF16), wif_ref[M_WIDTH:2 * M_WIDTH, :], preferred_element_type=F32)
            + jnp.dot(v_scr[...].astype(BF16), wif_ref[2 * M_WIDTH:3 * M_WIDTH, :], preferred_element_type=F32)
            + _vec(vec_ref, "bif"))


def _head_output(hh, o_pre, xc_h, z_h, mnw_h, mskip_h):
    mu = jnp.mean(hh, axis=1, keepdims=True)
    hc = hh - mu
    var = jnp.mean(hc * hc, axis=1, keepdims=True)
    hn = hc * lax.rsqrt(var + EPS) * mnw_h
    return (jax.nn.sigmoid(o_pre) * hn + mskip_h * xc_h) * _silu(z_h)


def _rglru_coeffs(xcr, vec_ref, rwa_ref, rwx_ref):
    xcr_b = xcr.astype(BF16)
    half = R_WIDTH // 2
    ra_pre = jnp.concatenate(
        [jnp.dot(xcr_b[:, :half], rwa_ref[0], preferred_element_type=F32),
         jnp.dot(xcr_b[:, half:], rwa_ref[1], preferred_element_type=F32)], axis=1) + _vec(vec_ref, "rba")
    rx_pre = jnp.concatenate(
        [jnp.dot(xcr_b[:, :half], rwx_ref[0], preferred_element_type=F32),
         jnp.dot(xcr_b[:, half:], rwx_ref[1], preferred_element_type=F32)], axis=1) + _vec(vec_ref, "rbx")
    log_a = (-RG_C) * jax.nn.sigmoid(ra_pre) * _softplus(-_vec(vec_ref, "rlam"))
    a = jnp.exp(log_a)
    th = jnp.tanh(log_a)
    one_minus_a2 = (-2.0 * th) / (1.0 - th)
    return a, jnp.sqrt(one_minus_a2) * (jax.nn.sigmoid(rx_pre) * xcr)


def _s5_project(u_b, wbu_ref, re_ref, im_ref):
    per_slice = (LANES // S_GROUP) * S_STATE
    for k in range(S_WIDTH // LANES):
        res = jnp.dot(u_b[:, k * LANES:(k + 1) * LANES], wbu_ref[k], preferred_element_type=F32)
        re_ref[:, k * per_slice:(k + 1) * per_slice] = res[:, :per_slice]
        im_ref[:, k * per_slice:(k + 1) * per_slice] = res[:, per_slice:]


def _s5_output(re_ref, im_ref, u, vec_ref, wcre_ref, wcim_ref, wglu_ref):
    nblk = wcre_ref.shape[0]
    k_blk = S_LANES // nblk
    parts = []
    for m in range(nblk):
        ks = slice(m * k_blk, (m + 1) * k_blk)
        parts.append(lax.dot_general(re_ref[:, ks].astype(BF16), wcre_ref[m], _NT, preferred_element_type=F32)
                     - lax.dot_general(im_ref[:, ks].astype(BF16), wcim_ref[m], _NT, preferred_element_type=F32))
    y = jnp.concatenate(parts, axis=1) + _vec(vec_ref, "sd") * u
    g = jax.nn.gelu(y)
    return g * jax.nn.sigmoid(jnp.dot(g.astype(BF16), wglu_ref[...], preferred_element_type=F32)
                              + _vec(vec_ref, "bglu"))


_MATRIX_KEYS = ("wqk", "wvo", "wif", "rwa", "rwx", "wbu", "wcre", "wcim", "wglu")


def _tile(x, i):
    return x[i * SUBLANES:(i + 1) * SUBLANES]


def _conv_interleaved(x, tail_ref, w, bias, rows):
    ntiles = rows // SUBLANES
    pos = lax.broadcasted_iota(jnp.int32, (SUBLANES, x.shape[1]), 0)
    prev = tail_ref[...]
    before = [pltpu.roll(jnp.where(pos == SUBLANES - 1, _tile(prev, CONV_W - 1 - d), _tile(x, ntiles - d)), 1, 0)
              for d in range(1, CONV_W)]
    out = w[CONV_W - 1:CONV_W, :] * x + bias
    for j in range(1, CONV_W):
        shifted = jnp.concatenate(before[:j][::-1] + [x[:rows - j * SUBLANES]], axis=0)
        out = out + w[CONV_W - 1 - j:CONV_W - j, :] * shifted
    tail_ref[...] = x[rows - (CONV_W - 1) * SUBLANES:]
    return out


def _cumsum_interleaved(x, rows):
    tiles = [_tile(x, 0)]
    for i in range(1, rows // SUBLANES):
        tiles.append(tiles[-1] + _tile(x, i))
    total = tiles[-1]
    start = _tile_cumsum(total) - total
    return jnp.concatenate([t + start for t in tiles], axis=0)


def _cummax_interleaved(x, rows):
    tiles = [_tile(x, 0)]
    for i in range(1, rows // SUBLANES):
        tiles.append(jnp.maximum(tiles[-1], _tile(x, i)))
    best = tiles[-1]
    pos = lax.broadcasted_iota(jnp.int32, best.shape, 0)
    s = 1
    while s < SUBLANES:
        best = jnp.maximum(best, jnp.where(pos >= s, pltpu.roll(best, s, 0), -jnp.inf))
        s *= 2
    start = jnp.where(pos == 0, -jnp.inf, pltpu.roll(best, 1, 0))
    return jnp.concatenate([jnp.maximum(t, start) for t in tiles], axis=0)


def _scan_real_interleaved(a_ref, b_ref, carry, rows):
    ntiles = rows // SUBLANES
    pos = lax.broadcasted_iota(jnp.int32, (SUBLANES, a_ref.shape[1]), 0)
    h = _tile(b_ref, 0)
    aprod = _tile(a_ref, 0)
    for i in range(1, ntiles):
        a = _tile(a_ref, i)
        h = a * h + _tile(b_ref, i)
        aprod = aprod * a
    g = _tile_scan_real(aprod, h + jnp.where(pos == 0, aprod * carry, 0.0))
    h = jnp.where(pos == 0, carry, pltpu.roll(g, 1, 0))
    for i in range(ntiles):
        h = _tile(a_ref, i) * h + _tile(b_ref, i)
        b_ref[i * SUBLANES:(i + 1) * SUBLANES, :] = h
    return g[SUBLANES - 1:SUBLANES]


def _scan_cplx_interleaved(re_ref, im_ref, p_re, p_im, cre_ref, cim_ref, rows):
    ntiles = rows // SUBLANES
    width = re_ref.shape[1]
    pos = lax.broadcasted_iota(jnp.int32, (SUBLANES, SCAN_LANE_BLOCK), 0)
    for blk in range(width // SCAN_LANE_BLOCK):
        sl = slice(blk * SCAN_LANE_BLOCK, (blk + 1) * SCAN_LANE_BLOCK)
        pr = jnp.broadcast_to(p_re[:, sl], pos.shape)
        pi = jnp.broadcast_to(p_im[:, sl], pos.shape)
        c_r = cre_ref[:, sl]
        c_i = cim_ref[:, sl]
        sr = re_ref[0:SUBLANES, sl]
        si = im_ref[0:SUBLANES, sl]
        for i in range(1, ntiles):
            rs = slice(i * SUBLANES, (i + 1) * SUBLANES)
            sr, si = pr * sr - pi * si + re_ref[rs, sl], pr * si + pi * sr + im_ref[rs, sl]
        qr, qi = pr, pi
        n = 1
        while n < ntiles:
            qr, qi = qr * qr - qi * qi, 2.0 * (qr * qi)
            n *= 2
        gr, gi = _tile_scan_cplx(sr + jnp.where(pos == 0, qr * c_r - qi * c_i, 0.0),
                                 si + jnp.where(pos == 0, qr * c_i + qi * c_r, 0.0), qr, qi)
        sr = jnp.where(pos == 0, c_r, pltpu.roll(gr, 1, 0))
        si = jnp.where(pos == 0, c_i, pltpu.roll(gi, 1, 0))
        for i in range(ntiles):
            rs = slice(i * SUBLANES, (i + 1) * SUBLANES)
            sr, si = pr * sr - pi * si + re_ref[rs, sl], pr * si + pi * sr + im_ref[rs, sl]
            re_ref[rs, sl] = sr
            im_ref[rs, sl] = si
        cre_ref[:, sl] = gr[SUBLANES - 1:SUBLANES]
        cim_ref[:, sl] = gi[SUBLANES - 1:SUBLANES]


def _prompt_mixer_kernel(*refs, rows, nchunk, n_alias):
    (xnext_ref, xfirst_ref, vec_ref, win_ref,
     wqk_ref, wvo_ref, wif_ref, rwa_ref, rwx_ref, wbu_ref, wcre_ref, wcim_ref, wglu_ref) = refs[:13]
    (mixed_ref, c_ref, n_ref, m_ref, mtail_ref, h_ref, rtail_ref, sre_ref, sim_ref,
     q_scr, k_scr, v_scr, o_scr, ra_scr, rb_scr, ure_scr, uim_scr, proj_scr, xn_scr) = refs[13 + n_alias:]
    T = rows
    step = pl.program_id(0)
    slot = lax.rem(step, 2)
    proj_ref = proj_scr.at[slot]
    next_proj_ref = proj_scr.at[1 - slot]
    slab = (2 * MIX_WIDTH) // M_HEADS

    @pl.when(step == 0)
    def _():
        xn = _rmsnorm_bf16(xfirst_ref[...], vec_ref)
        for j in range(M_HEADS):
            cols = slice(j * slab, (j + 1) * slab)
            proj_scr[0, :, cols] = jnp.dot(xn, win_ref[:, cols], preferred_element_type=F32)

    @pl.when(lax.rem(step, nchunk) == 0)
    def _():
        for ref in (c_ref, n_ref, m_ref, mtail_ref, h_ref, rtail_ref, sre_ref, sim_ref):
            ref[...] = jnp.zeros(ref.shape, F32)

    xn_scr[...] = _rmsnorm_bf16(xnext_ref[...], vec_ref)

    xm = proj_ref[:, 0:M_WIDTH]
    xc = _silu(_conv_interleaved(xm, mtail_ref, _vec(vec_ref, "mcw"), _vec(vec_ref, "mcb"), T))
    gates = _mlstm_project(xc, xm, vec_ref, wqk_ref, wvo_ref, wif_ref, q_scr, k_scr, v_scr, o_scr)
    log_i = gates[:, :LANES]
    b_all = _cumsum_interleaved(_log_sigmoid(gates[:, LANES:]), T)
    a_max = _cummax_interleaved(log_i - b_all, T)
    m_prev = m_ref[...]
    log_inter = b_all + m_prev
    m_t_all = jnp.maximum(log_inter, b_all + a_max)
    w_inter_all = jnp.exp(log_inter - m_t_all)
    floor_all = jnp.exp(-m_t_all)
    b_end = b_all[T - 1:T, :]
    m_new = jnp.maximum(b_end + m_prev, b_end + a_max[T - 1:T, :])
    w_src_all = jnp.exp(b_end - b_all + log_i - m_new)
    decay_all = jnp.exp(b_end + m_prev - m_new)
    m_ref[...] = m_new
    b_t = b_all.T
    li_t = log_i.T

    rowi = lax.broadcasted_iota(jnp.int32, (T, T), 0)
    coli = lax.broadcasted_iota(jnp.int32, (T, T), 1)
    sub_len = T // SUBLANES

    def time_of(r):
        return (r & (SUBLANES - 1)) * sub_len + (r >> SUBLANE_SHIFT)

    causal = time_of(coli) <= time_of(rowi)
    k_scale = M_HEAD_DIM ** -0.5
    mnw = _vec(vec_ref, "mnw")
    mskip = _vec(vec_ref, "mskip")

    for h in range(M_HEADS):
        sl = slice(h * M_HEAD_DIM, (h + 1) * M_HEAD_DIM)
        c_prev = c_ref[h]
        n_prev = n_ref[h:h + 1, :]
        w_inter = w_inter_all[:, h:h + 1]
        w_src = w_src_all[:, h:h + 1]
        decay = decay_all[:, h:h + 1]

        q = q_scr[:, sl]
        ks = k_scr[:, sl] * k_scale
        v = v_scr[:, sl]
        q_b = q.astype(BF16)
        ks_b = ks.astype(BF16)

        log_d = jnp.where(causal, b_all[:, h:h + 1] - b_t[h:h + 1, :] + li_t[h:h + 1, :], -jnp.inf)
        w_intra = jnp.exp(log_d - m_t_all[:, h:h + 1])
        s = lax.dot_general(q_b, ks_b, _NT, preferred_element_type=F32) * w_intra
        inter = lax.dot_general(q_b, c_prev.astype(BF16), _NT, preferred_element_type=F32)
        num = jnp.dot(s.astype(BF16), v.astype(BF16), preferred_element_type=F32) + w_inter * inter
        den = (jnp.sum(s, axis=1, keepdims=True)
               + w_inter * jnp.sum(q * n_prev, axis=1, keepdims=True))
        hh = num / jnp.maximum(jnp.abs(den), floor_all[:, h:h + 1])

        vw_t = (v * w_src).T.astype(BF16)
        c_ref[h] = decay * c_prev + jnp.dot(vw_t, ks_b, preferred_element_type=F32)
        n_ref[h:h + 1, :] = decay * n_prev + jnp.sum(ks * w_src, axis=0, keepdims=True)

        z = proj_ref[:, MIX_WIDTH + h * M_HEAD_DIM:MIX_WIDTH + (h + 1) * M_HEAD_DIM]
        mixed_ref[:, sl] = _head_output(hh, o_scr[:, sl], xc[:, sl], z, mnw[:, sl], mskip[:, sl]).astype(BF16)

        cols = slice(h * slab, (h + 1) * slab)
        next_proj_ref[:, cols] = jnp.dot(xn_scr[...], win_ref[:, cols], preferred_element_type=F32)

    xr = proj_ref[:, M_WIDTH:M_WIDTH + R_WIDTH]
    xcr = _conv_interleaved(xr, rtail_ref, _vec(vec_ref, "rcw"), _vec(vec_ref, "rcb"), T)
    a, bb = _rglru_coeffs(xcr, vec_ref, rwa_ref, rwx_ref)
    ra_scr[...] = a
    rb_scr[...] = bb
    h_ref[...] = _scan_real_interleaved(ra_scr, rb_scr, h_ref[...], T)
    zr = proj_ref[:, MIX_WIDTH + M_WIDTH:MIX_WIDTH + M_WIDTH + R_WIDTH]
    mixed_ref[:, M_WIDTH:M_WIDTH + R_WIDTH] = (rb_scr[...] * _silu(zr)).astype(BF16)

    u = proj_ref[:, M_WIDTH + R_WIDTH:MIX_WIDTH]
    _s5_project(u.astype(BF16), wbu_ref, ure_scr, uim_scr)
    _scan_cplx_interleaved(ure_scr, uim_scr, _vec(vec_ref, "abre"), _vec(vec_ref, "abim"), sre_ref, sim_ref, T)
    glu = _s5_output(ure_scr, uim_scr, u, vec_ref, wcre_ref, wcim_ref, wglu_ref)
    zs = proj_ref[:, MIX_WIDTH + M_WIDTH + R_WIDTH:2 * MIX_WIDTH]
    mixed_ref[:, M_WIDTH + R_WIDTH:MIX_WIDTH] = (glu * _silu(zs)).astype(BF16)


_PROMPT_STATE_SHAPES = ((M_HEADS, M_HEAD_DIM, M_HEAD_DIM), (M_HEADS, M_HEAD_DIM), (1, LANES),
                        ((CONV_W - 1) * SUBLANES, M_WIDTH), (1, R_WIDTH), ((CONV_W - 1) * SUBLANES, R_WIDTH),
                        (1, S_LANES), (1, S_LANES))


def _prompt_mixer(x, total_rows, l, bsz, seq, vec, w_in_b, mats, prev_states):
    rows = PROMPT_CHUNK
    nchunk = seq // rows
    nstep = bsz * nchunk
    aliases = {}
    alias_args = []
    n_in = 4 + len(mats)
    if prev_states is not None:
        alias_args = list(prev_states)
        aliases = {n_in + k: 1 + k for k in range(len(alias_args))}

    def state_spec(shape):
        nd = len(shape)
        return pl.BlockSpec((None, None) + shape, lambda t: (l, t // nchunk) + (0,) * nd)

    out_shape = ([jax.ShapeDtypeStruct((total_rows, MIX_WIDTH), BF16)]
                 + [jax.ShapeDtypeStruct((DEPTH, bsz) + s, F32) for s in _PROMPT_STATE_SHAPES])
    scratch = ([pltpu.VMEM((rows, w), F32)
                for w in (M_WIDTH, M_WIDTH, M_WIDTH, M_WIDTH, R_WIDTH, R_WIDTH, S_LANES, S_LANES)]
               + [pltpu.VMEM((2, rows, 2 * MIX_WIDTH), F32), pltpu.VMEM((rows, D_MODEL), BF16)])
    return pl.pallas_call(
        functools.partial(_prompt_mixer_kernel, rows=rows, nchunk=nchunk, n_alias=len(alias_args)),
        grid=(nstep,),
        in_specs=([pl.BlockSpec((rows, D_MODEL), lambda t: (jnp.minimum(t + 1, nstep - 1), 0)),
                   pl.BlockSpec((rows, D_MODEL), lambda t: (0, 0), pipeline_mode=pl.Buffered(1)),
                   _layer_spec(vec, l), _layer_spec(w_in_b, l)]
                  + [_layer_spec(w, l) for w in mats] + [_ANY_SPEC] * len(alias_args)),
        out_specs=([pl.BlockSpec((rows, MIX_WIDTH), lambda t: (t, 0))]
                   + [state_spec(s) for s in _PROMPT_STATE_SHAPES]),
        out_shape=out_shape,
        scratch_shapes=scratch,
        input_output_aliases=aliases,
        compiler_params=pltpu.CompilerParams(
            dimension_semantics=("arbitrary",), vmem_limit_bytes=VMEM_LIMIT_BYTES),
        name="prompt_mixer",
    )(x, x, vec, w_in_b, *mats, *alias_args)


def _seg_last(x, groups):
    x3 = x.reshape(groups, SUBLANES, x.shape[-1])
    return jnp.broadcast_to(x3[:, SUBLANES - 1:SUBLANES, :], x3.shape).reshape(x.shape)


def _seg_max(x, groups):
    x3 = x.reshape(groups, SUBLANES, x.shape[-1])
    return jnp.broadcast_to(jnp.max(x3, axis=1, keepdims=True), x3.shape).reshape(x.shape)


def _seg_sum(x, groups):
    x3 = x.reshape(groups, SUBLANES, x.shape[-1])
    return jnp.broadcast_to(jnp.sum(x3, axis=1, keepdims=True), x3.shape).reshape(x.shape)


def _seg_rows(state_ref, lanes=slice(None)):
    x = state_ref[:, :, lanes]
    return jnp.broadcast_to(x, (x.shape[0], SUBLANES, x.shape[2])).reshape(x.shape[0] * SUBLANES, x.shape[2])


def _seg_state(x):
    x3 = x.reshape(x.shape[0] // SUBLANES, SUBLANES, x.shape[-1])
    return x3[:, SUBLANES - 1:SUBLANES, :]


def _with_history(x, buf_ref, new_buf_ref):
    g = x.shape[0] // SUBLANES
    x3 = x.reshape(g, SUBLANES, x.shape[1])
    lead = SAMPLE_LEAD - (CONV_W - 1)
    full = jnp.concatenate([jnp.zeros((g, lead, x.shape[1]), F32), buf_ref[...], x3[:, SAMPLE_LEAD:, :]], axis=1)
    new_buf_ref[...] = full[:, SUBLANES - (CONV_W - 1):, :]
    return full.reshape(x.shape)


def _conv_rolled(xf, w, bias):
    out = w[CONV_W - 1:CONV_W, :] * xf + bias
    for j in range(1, CONV_W):
        out = out + w[CONV_W - 1 - j:CONV_W - j, :] * pltpu.roll(xf, j, 0)
    return out


_SAMPLE_STATE_WIDTHS = (M_WIDTH, LANES, M_WIDTH, R_WIDTH, R_WIDTH, S_LANES, S_LANES)
_SAMPLE_STATE_ROWS = (1, 1, CONV_W - 1, 1, CONV_W - 1, 1, 1)


def _sample_mixer_kernel(*refs, rows, c_seqs, n_alias):
    (proj_ref, c0_ref, n0_ref, m0_ref, mtail0_ref, h0_ref, rtail0_ref, sre0_ref, sim0_ref,
     vec_ref, wqk_ref, wvo_ref, wif_ref, rwa_ref, rwx_ref, wbu_ref, wcre_ref, wcim_ref, wglu_ref) = refs[:19]
    (mixed_ref, c_ref, n_ref, m_ref, mconv_ref, h_ref, rconv_ref, sre_ref, sim_ref,
     q_scr, k_scr, v_scr, o_scr, xc_scr, intert_scr, vwt_scr, qt_scr, ksb_scr, dec_scr, b_scr, li_scr,
     ure_scr, uim_scr) = refs[19 + n_alias:]
    R = rows
    G = R // SUBLANES
    sub = pl.program_id(1)
    k_scale = M_HEAD_DIM ** -0.5
    pos = _sublane_pos((R, 1))
    valid = pos >= SAMPLE_LEAD
    first_token = pos == SAMPLE_LEAD

    @pl.when(sub == 0)
    def _():
        xm = _with_history(proj_ref[:, 0:M_WIDTH], mtail0_ref, mconv_ref)
        xc = _silu(_conv_rolled(xm, _vec(vec_ref, "mcw"), _vec(vec_ref, "mcb")))
        xc_scr[...] = xc
        gates = _mlstm_project(xc, xm, vec_ref, wqk_ref, wvo_ref, wif_ref, q_scr, k_scr, v_scr, o_scr)
        log_i = jnp.where(valid, gates[:, :LANES], -jnp.inf)
        log_f = jnp.where(valid, _log_sigmoid(gates[:, LANES:]), 0.0)
        b_all = _tile_cumsum(log_f)
        b_end = _seg_last(b_all, G)
        m_rows = _seg_rows(m0_ref)
        n_rows = _seg_rows(n0_ref)
        log_src = b_end - b_all + log_i
        m_new = jnp.maximum(b_end + m_rows, _seg_max(log_src, G))
        w_src_all = jnp.exp(log_src - m_new)
        decay_all = jnp.exp(b_end + m_rows - m_new)
        dec_scr[...] = decay_all
        b_scr[...] = b_all
        li_scr[...] = log_i
        m_ref[...] = _seg_state(m_new)
        for h in range(M_HEADS):
            sl = slice(h * M_HEAD_DIM, (h + 1) * M_HEAD_DIM)
            ks = k_scr[:, sl] * k_scale
            w_src = w_src_all[:, h:h + 1]
            n_ref[:, :, sl] = _seg_state(decay_all[:, h:h + 1] * n_rows[:, sl] + _seg_sum(ks * w_src, G))
            vwt_scr[h] = (v_scr[:, sl] * w_src).T
            qt_scr[h] = q_scr[:, sl].T.astype(BF16)
            ksb_scr[h] = ks.astype(BF16)
            intert_scr[h] = jnp.zeros((M_HEAD_DIM, R), F32)

        xr = _with_history(proj_ref[:, M_WIDTH:M_WIDTH + R_WIDTH], rtail0_ref, rconv_ref)
        xcr = _conv_rolled(xr, _vec(vec_ref, "rcw"), _vec(vec_ref, "rcb"))
        a, bb = _rglru_coeffs(xcr, vec_ref, rwa_ref, rwx_ref)
        hs = _tile_scan_real(a, jnp.where(valid, bb, 0.0) + jnp.where(first_token, a * _seg_rows(h0_ref), 0.0))
        h_ref[...] = _seg_state(hs)
        zr = proj_ref[:, MIX_WIDTH + M_WIDTH:MIX_WIDTH + M_WIDTH + R_WIDTH]
        mixed_ref[:, M_WIDTH:M_WIDTH + R_WIDTH] = jnp.where(valid, hs * _silu(zr), 0.0).astype(BF16)

        u = proj_ref[:, M_WIDTH + R_WIDTH:MIX_WIDTH]
        _s5_project(u.astype(BF16), wbu_ref, ure_scr, uim_scr)
        p_re = _vec(vec_ref, "abre")
        p_im = _vec(vec_ref, "abim")
        for blk in range(S_LANES // SCAN_LANE_BLOCK):
            sl = slice(blk * SCAN_LANE_BLOCK, (blk + 1) * SCAN_LANE_BLOCK)
            pr = jnp.broadcast_to(p_re[:, sl], (R, SCAN_LANE_BLOCK))
            pi = jnp.broadcast_to(p_im[:, sl], (R, SCAN_LANE_BLOCK))
            s0r = _seg_rows(sre0_ref, sl)
            s0i = _seg_rows(sim0_ref, sl)
            sr, si = _tile_scan_cplx(
                jnp.where(valid, ure_scr[:, sl], 0.0) + jnp.where(first_token, pr * s0r - pi * s0i, 0.0),
                jnp.where(valid, uim_scr[:, sl], 0.0) + jnp.where(first_token, pr * s0i + pi * s0r, 0.0), pr, pi)
            ure_scr[:, sl] = sr
            uim_scr[:, sl] = si
            sre_ref[:, :, sl] = _seg_state(sr)
            sim_ref[:, :, sl] = _seg_state(si)
        glu = _s5_output(ure_scr, uim_scr, u, vec_ref, wcre_ref, wcim_ref, wglu_ref)
        zs = proj_ref[:, MIX_WIDTH + M_WIDTH + R_WIDTH:2 * MIX_WIDTH]
        mixed_ref[:, M_WIDTH + R_WIDTH:MIX_WIDTH] = jnp.where(valid, glu * _silu(zs), 0.0).astype(BF16)

    lane_seq = lax.broadcasted_iota(jnp.int32, (M_HEAD_DIM, R), 1) >> SUBLANE_SHIFT
    seq0 = sub * c_seqs
    for h in range(M_HEADS):
        c_old = c0_ref[:, h].reshape(c_seqs * M_HEAD_DIM, M_HEAD_DIM)
        readout = jnp.dot(c_old.astype(BF16), qt_scr[h], preferred_element_type=F32)
        acc = intert_scr[h]
        vwt = vwt_scr[h]
        lhs = []
        for s in range(c_seqs):
            own = lane_seq == seq0 + s
            acc = jnp.where(own, readout[s * M_HEAD_DIM:(s + 1) * M_HEAD_DIM], acc)
            lhs.append(jnp.where(own, vwt, 0.0))
        intert_scr[h] = acc
        upd = jnp.dot(jnp.concatenate(lhs, axis=0).astype(BF16), ksb_scr[h], preferred_element_type=F32)
        for s in range(c_seqs):
            r0 = pl.multiple_of((seq0 + s) * SUBLANES, SUBLANES)
            c_ref[s, h] = (dec_scr[pl.ds(r0, 1), h:h + 1] * c0_ref[s, h]
                           + upd[s * M_HEAD_DIM:(s + 1) * M_HEAD_DIM])

    @pl.when(sub == pl.num_programs(1) - 1)
    def _():
        _sample_heads(proj_ref, m0_ref, n0_ref, vec_ref, mixed_ref,
                      q_scr, k_scr, v_scr, o_scr, xc_scr, intert_scr, b_scr, li_scr, valid, R)


def _sample_heads(proj_ref, m0_ref, n0_ref, vec_ref, mixed_ref,
                  q_scr, k_scr, v_scr, o_scr, xc_scr, intert_scr, b_scr, li_scr, valid, R):
    rowi = lax.broadcasted_iota(jnp.int32, (R, R), 0)
    coli = lax.broadcasted_iota(jnp.int32, (R, R), 1)
    same_causal = jnp.logical_and(coli <= rowi, (coli >> SUBLANE_SHIFT) == (rowi >> SUBLANE_SHIFT))
    k_scale = M_HEAD_DIM ** -0.5
    mnw = _vec(vec_ref, "mnw")
    mskip = _vec(vec_ref, "mskip")
    n_rows = _seg_rows(n0_ref)
    b_all = b_scr[...]
    log_i = li_scr[...]
    log_inter = b_all + _seg_rows(m0_ref)
    m_t_all = jnp.maximum(log_inter, b_all + _tile_cummax(log_i - b_all))
    w_inter_all = jnp.exp(log_inter - m_t_all)
    floor_all = jnp.exp(-m_t_all)
    b_t = b_all.T
    li_t = log_i.T

    for h in range(M_HEADS):
        sl = slice(h * M_HEAD_DIM, (h + 1) * M_HEAD_DIM)
        q = q_scr[:, sl]
        ks = k_scr[:, sl] * k_scale
        v = v_scr[:, sl]
        q_b = q.astype(BF16)
        ks_b = ks.astype(BF16)
        w_inter = w_inter_all[:, h:h + 1]

        log_d = jnp.where(same_causal, b_all[:, h:h + 1] - b_t[h:h + 1, :] + li_t[h:h + 1, :], -jnp.inf)
        w_intra = jnp.exp(log_d - m_t_all[:, h:h + 1])
        s = lax.dot_general(q_b, ks_b, _NT, preferred_element_type=F32) * w_intra
        num = (jnp.dot(s.astype(BF16), v.astype(BF16), preferred_element_type=F32)
               + w_inter * intert_scr[h].T)
        den = (jnp.sum(s, axis=1, keepdims=True)
               + w_inter * jnp.sum(q * n_rows[:, sl], axis=1, keepdims=True))
        hh = num / jnp.maximum(jnp.abs(den), floor_all[:, h:h + 1])

        z = proj_ref[:, MIX_WIDTH + h * M_HEAD_DIM:MIX_WIDTH + (h + 1) * M_HEAD_DIM]
        out = _head_output(hh, o_scr[:, sl], xc_scr[:, sl], z, mnw[:, sl], mskip[:, sl])
        mixed_ref[:, sl] = jnp.where(valid, out, 0.0).astype(BF16)


def _sample_mixer(proj, l, row0, mixed_prev, c_all, row_state, vec, mats, prev_out):
    R = SAMPLE_ROWS
    nrows = proj.shape[0]
    nsub = (R // SUBLANES) // SAMPLE_C_SEQS
    blk0 = row0 // R
    c_spec = pl.BlockSpec((None, SAMPLE_C_SEQS, M_HEADS, M_HEAD_DIM, M_HEAD_DIM),
                          lambda i, j: (l, i * nsub + j, 0, 0, 0))

    def state_spec(width, rows_per_seq):
        return pl.BlockSpec((None, R // SUBLANES, rows_per_seq, width), lambda i, j: (l, i, 0, 0))

    def state_shape(width, rows_per_seq):
        return (DEPTH, nrows // SUBLANES, rows_per_seq, width)

    state_specs = [state_spec(w, r) for w, r in zip(_SAMPLE_STATE_WIDTHS, _SAMPLE_STATE_ROWS)]
    alias_args = [mixed_prev] + (list(prev_out) if prev_out is not None else [])
    first_alias = 2 + len(row_state) + 1 + len(mats)
    aliases = {first_alias + k: k for k in range(len(alias_args))}
    out_shape = ([jax.ShapeDtypeStruct(mixed_prev.shape, BF16), jax.ShapeDtypeStruct(c_all.shape, F32)]
                 + [jax.ShapeDtypeStruct(state_shape(w, r), F32)
                    for w, r in zip(_SAMPLE_STATE_WIDTHS, _SAMPLE_STATE_ROWS)])
    scratch = ([pltpu.VMEM((R, M_WIDTH), F32)] * 5
               + [pltpu.VMEM((M_HEADS, M_HEAD_DIM, R), F32), pltpu.VMEM((M_HEADS, M_HEAD_DIM, R), F32),
                  pltpu.VMEM((M_HEADS, M_HEAD_DIM, R), BF16), pltpu.VMEM((M_HEADS, R, M_HEAD_DIM), BF16)]
               + [pltpu.VMEM((R, LANES), F32)] * 3
               + [pltpu.VMEM((R, S_LANES), F32)] * 2)
    return pl.pallas_call(
        functools.partial(_sample_mixer_kernel, rows=R, c_seqs=SAMPLE_C_SEQS, n_alias=len(alias_args)),
        grid=(nrows // R, nsub),
        in_specs=([pl.BlockSpec((R, 2 * MIX_WIDTH), lambda i, j: (i, 0)), c_spec] + state_specs
                  + [_layer_spec(vec, l)] + [_layer_spec(w, l) for w in mats] + [_ANY_SPEC] * len(alias_args)),
        out_specs=[pl.BlockSpec((R, MIX_WIDTH), lambda i, j: (blk0 + i, 0)), c_spec] + state_specs,
        out_shape=out_shape,
        scratch_shapes=scratch,
        input_output_aliases=aliases,
        compiler_params=pltpu.CompilerParams(
            dimension_semantics=("parallel", "arbitrary"), vmem_limit_bytes=VMEM_LIMIT_BYTES),
        name="sample_mixer",
    )(proj, c_all, *row_state, vec, *mats, *alias_args)


def _prepare_weights(p, s5):
    abre, abim, bbre_t, bbim_t = s5

    def split_gates(g):
        pad = [(0, 0)] * (g.ndim - 1) + [(0, LANES - M_HEADS)]
        return jnp.concatenate([jnp.pad(g[..., :M_HEADS], pad), jnp.pad(g[..., M_HEADS:], pad)], axis=-1)

    table = {"mcw": p["m_conv_w"], "rcw": p["r_conv_w"], "mcb": p["m_conv_b"], "mnw": p["m_norm_w"],
             "mskip": p["m_skip"], "rcb": p["r_conv_b"], "rba": p["r_ba"], "rbx": p["r_bx"], "rlam": p["r_lam"],
             "sd": p["s_d"], "bglu": p["s_b_glu"], "bif": split_gates(p["m_b_if"]), "abre": abre, "abim": abim,
             "norm_w": p["norm_w"]}

    def group_rows(group):
        parts = [table[name].reshape(DEPTH, n, w) for name, n, w in group]
        fill = _VEC_WIDTH - sum(w for _, _, w in group)
        if fill:
            parts.append(jnp.zeros((DEPTH, group[0][1], fill), F32))
        return parts[0] if len(parts) == 1 else jnp.concatenate(parts, axis=-1)

    vec = jnp.concatenate([group_rows(g) for g in _VEC_GROUPS]
                          + [jnp.zeros((DEPTH, _VEC_ROWS - _VEC_USED_ROWS, _VEC_WIDTH), F32)], axis=1)

    def block_diag_halves(w):
        nb = R_BLOCKS // 2
        w5 = w.reshape(DEPTH, 2, nb, w.shape[-2], w.shape[-1])
        eye = jnp.eye(nb, dtype=F32)[:, None, :, None]
        return (w5[:, :, :, :, None, :] * eye).reshape(
            DEPTH, 2, nb * w.shape[-2], nb * w.shape[-1]).astype(BF16)

    n_in = S_WIDTH // LANES
    g_in = LANES // S_GROUP
    n_out = 2
    g_out = S_GROUPS // n_out

    def bu_blocks(bb_t):
        bb5 = bb_t.reshape(DEPTH, S_GROUP, n_in, g_in, S_STATE)
        return jnp.einsum("ab,dckbp->dkacbp", jnp.eye(g_in, dtype=F32), bb5).reshape(
            DEPTH, n_in, LANES, g_in * S_STATE)

    def c_blocks(c):
        c5 = c.reshape(DEPTH, n_out, g_out, S_GROUP, S_STATE)
        eye = jnp.eye(g_out, dtype=F32)[:, None, :, None]
        return (c5[:, :, :, :, None, :] * eye).reshape(
            DEPTH, n_out, g_out * S_GROUP, g_out * S_STATE).astype(BF16)

    mats = {
        "wqk": jnp.concatenate([p["m_wq"], p["m_wk"]], axis=-1).astype(BF16),
        "wvo": jnp.concatenate([p["m_wv"], p["m_wo"]], axis=-1).astype(BF16),
        "wif": split_gates(p["m_w_if"]).astype(BF16),
        "rwa": block_diag_halves(p["r_wa"]), "rwx": block_diag_halves(p["r_wx"]),
        "wbu": jnp.concatenate([bu_blocks(bbre_t), bu_blocks(bbim_t)], axis=-1).astype(BF16),
        "wcre": c_blocks(p["s_c_re"]), "wcim": c_blocks(p["s_c_im"]),
        "wglu": p["s_w_glu"].astype(BF16),
    }
    return vec, p["w_in"].astype(BF16), p["w_out"].astype(BF16), [mats[k] for k in _MATRIX_KEYS]


def _sample_rows_state(n, m, mconv, h, rconv, sre, sim):
    bsz = n.shape[1]

    def per_seq(x):
        return x.reshape(DEPTH, bsz, 1, x.shape[-1])

    m_pad = jnp.pad(m, ((0, 0), (0, 0), (0, LANES - M_HEADS)))
    return (per_seq(n.reshape(DEPTH, bsz, M_WIDTH)), per_seq(m_pad), mconv, per_seq(h), rconv,
            per_seq(sre.reshape(DEPTH, bsz, S_LANES)), per_seq(sim.reshape(DEPTH, bsz, S_LANES)))


def _sample_state_from_rows(n, m, mconv, h, rconv, sre, sim):
    bsz = n.shape[1]
    return (n.reshape(DEPTH, bsz, M_HEADS, M_HEAD_DIM), m.reshape(DEPTH, bsz, LANES)[..., :M_HEADS],
            mconv, h.reshape(DEPTH, bsz, R_WIDTH), rconv,
            sre.reshape(DEPTH, bsz, S_GROUPS, S_STATE), sim.reshape(DEPTH, bsz, S_GROUPS, S_STATE))


def _prompt_state_from_kernel(c, n, m, mtail, h, rtail, sre, sim):
    bsz = c.shape[1]
    return (c, n, m[:, :, 0, :M_HEADS], mtail[:, :, SUBLANES - 1::SUBLANES], h.reshape(DEPTH, bsz, R_WIDTH),
            rtail[:, :, SUBLANES - 1::SUBLANES],
            sre.reshape(DEPTH, bsz, S_GROUPS, S_STATE), sim.reshape(DEPTH, bsz, S_GROUPS, S_STATE))


def _interleave_chunks(x, inverse=False):
    bsz, seq, d = x.shape
    sub_len = PROMPT_CHUNK // SUBLANES
    inner = (sub_len, SUBLANES) if inverse else (SUBLANES, sub_len)
    return x.reshape(bsz, seq // PROMPT_CHUNK, *inner, d).transpose(0, 1, 3, 2, 4).reshape(bsz, seq, d)


def kernel(x_prompt, x_sample, state_mlstm_C, state_mlstm_n, state_mlstm_m, state_mlstm_conv, state_rglru_h, state_rglru_conv, state_s5_re, state_s5_im, norm_w, w_in, w_out, m_conv_w, m_conv_b, m_wq, m_wk, m_wv, m_wo, m_w_if, m_b_if, m_norm_w, m_skip, r_conv_w, r_conv_b, r_wa, r_ba, r_wx, r_bx, r_lam, s_lam_re, s_lam_im, s_b_re, s_b_im, s_c_re, s_c_im, s_d, s_log_step, s_w_glu, s_b_glu, final_norm_w):
    p = dict(norm_w=norm_w, w_in=w_in, w_out=w_out, m_conv_w=m_conv_w, m_conv_b=m_conv_b, m_wq=m_wq, m_wk=m_wk,
             m_wv=m_wv, m_wo=m_wo, m_w_if=m_w_if, m_b_if=m_b_if, m_norm_w=m_norm_w, m_skip=m_skip,
             r_conv_w=r_conv_w, r_conv_b=r_conv_b, r_wa=r_wa, r_ba=r_ba, r_wx=r_wx, r_bx=r_bx, r_lam=r_lam,
             s_c_re=s_c_re, s_c_im=s_c_im, s_d=s_d, s_w_glu=s_w_glu, s_b_glu=s_b_glu)
    s5 = _s5_prep(s_lam_re, s_lam_im, s_log_step, s_b_re, s_b_im)
    vec, w_in_b, w_out_b, mats = _prepare_weights(p, s5)
    final_w = final_norm_w.reshape(1, D_MODEL)

    bsz, seq, _ = x_prompt.shape
    dec_batch, dec_seq, _ = x_sample.shape
    n_prompt = bsz * seq
    n_sample = dec_batch * SUBLANES
    stream = [_interleave_chunks(x_prompt).reshape(n_prompt, D_MODEL),
              jnp.pad(x_sample, ((0, 0), (SAMPLE_LEAD, 0), (0, 0))).reshape(n_sample, D_MODEL)]
    rows_state = _sample_rows_state(state_mlstm_n, state_mlstm_m, state_mlstm_conv, state_rglru_h,
                                    state_rglru_conv, state_s5_re, state_s5_im)
    pr_states = None
    sa_states = None
    for l in range(DEPTH):
        last = l == DEPTH - 1
        mixed, *pr_states = _prompt_mixer(stream[0], n_prompt + n_sample, l, bsz, seq, vec, w_in_b, mats, pr_states)
        proj_s = _inproj(stream[-1], stream[-1].shape[0] - n_sample, n_sample, vec, w_in_b, l)
        mixed, *sa_states = _sample_mixer(proj_s, l, n_prompt, mixed, state_mlstm_C, rows_state, vec, mats, sa_states)
        stream = _outproj(mixed, stream, w_out_b, l, final_w, last, split_rows=n_prompt if last else None)
    y_prompt = _interleave_chunks(stream[0].reshape(bsz, seq, D_MODEL), inverse=True)
    y_sample = stream[1]
    return (y_prompt, y_sample, *_prompt_state_from_kernel(*pr_states),
            sa_states[0], *_sample_state_from_rows(*sa_states[1:]))
```

```python
import functools

import jax
import jax.numpy as jnp
from jax import lax
from jax.experimental import pallas as pl
from jax.experimental.pallas import tpu as pltpu

F32 = jnp.float32
BF16 = jnp.bfloat16

D_MODEL = 2048
DEPTH = 2
MIX_WIDTH = D_MODEL
M_WIDTH = MIX_WIDTH // 2
R_WIDTH = MIX_WIDTH // 4
S_WIDTH = MIX_WIDTH - M_WIDTH - R_WIDTH
M_HEADS = 8
M_HEAD_DIM = M_WIDTH // M_HEADS
R_BLOCKS = 8
RG_C = 8.0
S_GROUP = 16
S_GROUPS = S_WIDTH // S_GROUP
S_STATE = 64
S_LANES = S_GROUPS * S_STATE
CONV_W = 4
EPS = 1e-6

SUBLANES = 8
SUBLANE_SHIFT = SUBLANES.bit_length() - 1
LANES = 128
VMEM_LIMIT_BYTES = 56 * 1024 * 1024

PROMPT_CHUNK = 256
SAMPLE_ROWS = 128
SAMPLE_C_SEQS = 8
SAMPLE_LEAD = SUBLANES - 4
INPROJ_TM = 512
INPROJ_TN = 1024
OUTPROJ_TM = 512
SCAN_LANE_BLOCK = 256

_NT = (((1,), (1,)), ((), ()))

_VEC_WIDTH = D_MODEL
_VEC_GROUPS = ((("mcw", CONV_W, M_WIDTH), ("rcw", CONV_W, R_WIDTH)),
               (("mcb", 1, M_WIDTH), ("mnw", 1, M_WIDTH)),
               (("mskip", 1, M_WIDTH), ("rcb", 1, R_WIDTH), ("rba", 1, R_WIDTH)),
               (("rbx", 1, R_WIDTH), ("rlam", 1, R_WIDTH), ("sd", 1, S_WIDTH), ("bglu", 1, S_WIDTH)),
               (("bif", 1, 2 * LANES),),
               (("abre", 1, S_LANES),), (("abim", 1, S_LANES),), (("norm_w", 1, D_MODEL),))
_VEC_LAYOUT = {}
_row = 0
for _group in _VEC_GROUPS:
    _lane = 0
    for _name, _n, _w in _group:
        _VEC_LAYOUT[_name] = (_row, _n, _lane, _w)
        _lane += _w
    _row += _group[0][1]
_VEC_USED_ROWS = _row
_VEC_ROWS = -(-_row // SUBLANES) * SUBLANES


def _vec(vec_ref, name):
    r0, n, l0, w = _VEC_LAYOUT[name]
    return vec_ref[r0:r0 + n, l0:l0 + w]


def _layer_spec(arr, l):
    nd = arr.ndim - 1
    return pl.BlockSpec((None,) + arr.shape[1:], lambda *_: (l,) + (0,) * nd, pipeline_mode=pl.Buffered(1))


_ANY_SPEC = pl.BlockSpec(memory_space=pl.ANY)


def _s5_prep_kernel(lr_ref, li_ref, ls_ref, brt_ref, bit_ref, abre_ref, abim_ref, bbre_ref, bbim_ref):
    lr = lr_ref[0]
    li = li_ref[0]
    dt = jnp.exp(ls_ref[0])
    mag = jnp.exp(lr * dt)
    ang = li * dt
    ab_re = mag * jnp.cos(ang)
    ab_im = mag * jnp.sin(ang)
    den = lr * lr + li * li
    nr = ab_re - 1.0
    f_re = (nr * lr + ab_im * li) / den
    f_im = (ab_im * lr - nr * li) / den
    br = brt_ref[0]
    bi = bit_ref[0]
    abre_ref[0] = ab_re
    abim_ref[0] = ab_im
    bbre_ref[0] = f_re * br - f_im * bi
    bbim_ref[0] = f_re * bi + f_im * br


def _s5_prep(s_lam_re, s_lam_im, s_log_step, s_b_re, s_b_im):
    lr = s_lam_re.reshape(DEPTH, 1, S_LANES)
    li = s_lam_im.reshape(DEPTH, 1, S_LANES)
    ls = jnp.repeat(s_log_step, S_STATE, axis=-1).reshape(DEPTH, 1, S_LANES)
    brt = s_b_re.reshape(DEPTH, S_LANES, S_GROUP).transpose(0, 2, 1)
    bit = s_b_im.reshape(DEPTH, S_LANES, S_GROUP).transpose(0, 2, 1)
    vec = pl.BlockSpec((1, 1, S_LANES), lambda l: (l, 0, 0))
    mat = pl.BlockSpec((1, S_GROUP, S_LANES), lambda l: (l, 0, 0))
    return pl.pallas_call(
        _s5_prep_kernel,
        grid=(DEPTH,),
        in_specs=[vec, vec, vec, mat, mat],
        out_specs=[vec, vec, mat, mat],
        out_shape=[jax.ShapeDtypeStruct((DEPTH, 1, S_LANES), F32),
                   jax.ShapeDtypeStruct((DEPTH, 1, S_LANES), F32),
                   jax.ShapeDtypeStruct((DEPTH, S_GROUP, S_LANES), F32),
                   jax.ShapeDtypeStruct((DEPTH, S_GROUP, S_LANES), F32)],
        name="s5_prep",
    )(lr, li, ls, brt, bit)


def _stream_specs(parts, tm):
    if len(parts) == 1:
        return [pl.BlockSpec((tm, D_MODEL), lambda i: (i, 0))]
    nb0 = parts[0].shape[0] // tm
    return [pl.BlockSpec((tm, D_MODEL), lambda i: (jnp.minimum(i, nb0 - 1), 0)),
            pl.BlockSpec((tm, D_MODEL), lambda i: (jnp.maximum(i - nb0, 0), 0))]


def _read_stream(refs, nb0):
    if len(refs) == 1:
        return refs[0][...]
    return jnp.where(pl.program_id(0) < nb0, refs[0][...], refs[1][...])


def _rmsnorm_bf16(x, vec_ref):
    ms = jnp.mean(x * x, axis=-1, keepdims=True)
    return (x * lax.rsqrt(ms + EPS) * _vec(vec_ref, "norm_w")).astype(BF16)


def _inproj_kernel(x_ref, vec_ref, w_ref, o_ref, wb_ref, xn_scr):
    @pl.when(pl.program_id(0) == 0)
    def _():
        xn_scr[...] = _rmsnorm_bf16(x_ref[...], vec_ref)

    wb = w_ref[...].astype(BF16)
    wb_ref[...] = wb
    o_ref[...] = jnp.dot(xn_scr[...], wb, preferred_element_type=F32)


def _inproj(x, row0, nrows, vec, w_in, l):
    blk0 = row0 // nrows
    return pl.pallas_call(
        _inproj_kernel,
        grid=((2 * MIX_WIDTH) // INPROJ_TN,),
        in_specs=[pl.BlockSpec((nrows, D_MODEL), lambda j: (blk0, 0), pipeline_mode=pl.Buffered(1)),
                  _layer_spec(vec, l),
                  pl.BlockSpec((None, D_MODEL, INPROJ_TN), lambda j: (l, 0, j))],
        out_specs=[pl.BlockSpec((nrows, INPROJ_TN), lambda j: (0, j)),
                   pl.BlockSpec((D_MODEL, INPROJ_TN), lambda j: (0, j))],
        out_shape=[jax.ShapeDtypeStruct((nrows, 2 * MIX_WIDTH), F32),
                   jax.ShapeDtypeStruct((D_MODEL, 2 * MIX_WIDTH), BF16)],
        scratch_shapes=[pltpu.VMEM((nrows, D_MODEL), BF16)],
        compiler_params=pltpu.CompilerParams(
            dimension_semantics=("arbitrary",), vmem_limit_bytes=VMEM_LIMIT_BYTES),
        name="inproj",
    )(x, vec, w_in)


def _outproj_kernel(*refs, n_x, nb0, n_out, nb0_out, final):
    mixed_ref, x_refs = refs[0], refs[1:1 + n_x]
    w_ref, fw_ref = refs[1 + n_x:3 + n_x]
    o_refs, wb_scr = refs[3 + n_x:-1], refs[-1]

    @pl.when(pl.program_id(0) == 0)
    def _():
        wb_scr[...] = w_ref[...].astype(BF16)

    y = _read_stream(x_refs, nb0) + jnp.dot(mixed_ref[...], wb_scr[...], preferred_element_type=F32)
    if final:
        ms = jnp.mean(y * y, axis=-1, keepdims=True)
        y = y * lax.rsqrt(ms + EPS) * fw_ref[...]
    if n_out == 1:
        o_refs[0][...] = y
    else:
        @pl.when(pl.program_id(0) < nb0_out)
        def _():
            o_refs[0][...] = y

        @pl.when(pl.program_id(0) >= nb0_out)
        def _():
            o_refs[1][...] = y.reshape(y.shape[0] // SUBLANES, SUBLANES, y.shape[1])[:, SAMPLE_LEAD:, :]


def _outproj(mixed, x_parts, w_out, l, final_w, final, split_rows=None):
    n = mixed.shape[0]
    if split_rows is None:
        out_parts = [jax.ShapeDtypeStruct((n, D_MODEL), F32)]
        out_specs = [pl.BlockSpec((OUTPROJ_TM, D_MODEL), lambda i: (i, 0))]
        nb0_out = n // OUTPROJ_TM
    else:
        nb0_out = split_rows // OUTPROJ_TM
        tokens = SUBLANES - SAMPLE_LEAD
        out_parts = [jax.ShapeDtypeStruct((split_rows, D_MODEL), F32),
                     jax.ShapeDtypeStruct(((n - split_rows) // SUBLANES, tokens, D_MODEL), F32)]
        out_specs = [pl.BlockSpec((OUTPROJ_TM, D_MODEL), lambda i: (jnp.minimum(i, nb0_out - 1), 0)),
                     pl.BlockSpec((OUTPROJ_TM // SUBLANES, tokens, D_MODEL),
                                  lambda i: (jnp.maximum(i - nb0_out, 0), 0, 0))]
    return pl.pallas_call(
        functools.partial(_outproj_kernel, n_x=len(x_parts), nb0=x_parts[0].shape[0] // OUTPROJ_TM,
                          n_out=len(out_parts), nb0_out=nb0_out, final=final),
        grid=(n // OUTPROJ_TM,),
        in_specs=([pl.BlockSpec((OUTPROJ_TM, MIX_WIDTH), lambda i: (i, 0))] + _stream_specs(x_parts, OUTPROJ_TM)
                  + [_layer_spec(w_out, l), pl.BlockSpec((1, D_MODEL), lambda i: (0, 0))]),
        out_specs=out_specs,
        out_shape=out_parts,
        scratch_shapes=[pltpu.VMEM((MIX_WIDTH, D_MODEL), BF16)],
        compiler_params=pltpu.CompilerParams(
            dimension_semantics=("arbitrary",), vmem_limit_bytes=VMEM_LIMIT_BYTES),
        name="outproj",
    )(mixed, *x_parts, w_out, final_w)


def _silu(x):
    return x * jax.nn.sigmoid(x)


def _log_sigmoid(x):
    return jnp.minimum(x, 0.0) - jnp.log1p(jnp.exp(-jnp.abs(x)))


def _softplus(x):
    return jnp.maximum(x, 0.0) + jnp.log1p(jnp.exp(-jnp.abs(x)))


def _sublane_pos(shape):
    return lax.broadcasted_iota(jnp.int32, shape, 0) & (SUBLANES - 1)


def _tile_cumsum(x):
    pos = _sublane_pos(x.shape)
    s = 1
    while s < SUBLANES:
        x = x + jnp.where(pos >= s, pltpu.roll(x, s, 0), 0.0)
        s *= 2
    return x


def _tile_cummax(x):
    pos = _sublane_pos(x.shape)
    s = 1
    while s < SUBLANES:
        x = jnp.maximum(x, jnp.where(pos >= s, pltpu.roll(x, s, 0), -jnp.inf))
        s *= 2
    return x


def _tile_scan_real(a, b):
    pos = _sublane_pos(a.shape)
    s = 1
    while s < SUBLANES:
        m = pos >= s
        b = jnp.where(m, a * pltpu.roll(b, s, 0) + b, b)
        if 2 * s < SUBLANES:
            a = jnp.where(m, a * pltpu.roll(a, s, 0), a)
        s *= 2
    return b


def _tile_scan_cplx(sr, si, pr, pi):
    pos = _sublane_pos(sr.shape)
    s = 1
    while s < SUBLANES:
        m = pos >= s
        sr_sh = pltpu.roll(sr, s, 0)
        si_sh = pltpu.roll(si, s, 0)
        sr, si = (jnp.where(m, sr + (pr * sr_sh - pi * si_sh), sr),
                  jnp.where(m, si + (pr * si_sh + pi * sr_sh), si))
        if 2 * s < SUBLANES:
            pr, pi = pr * pr - pi * pi, 2.0 * (pr * pi)
        s *= 2
    return sr, si


def _mlstm_project(xc, xm, vec_ref, wqk_ref, wvo_ref, wif_ref, q_scr, k_scr, v_scr, o_scr):
    xc_b = xc.astype(BF16)
    xm_b = xm.astype(BF16)
    for h in range(M_HEADS):
        sl = slice(h * M_HEAD_DIM, (h + 1) * M_HEAD_DIM)
        qk = jnp.dot(xc_b[:, sl], wqk_ref[h], preferred_element_type=F32)
        vo = jnp.dot(xm_b[:, sl], wvo_ref[h], preferred_element_type=F32)
        q_scr[:, sl] = qk[:, :M_HEAD_DIM]
        k_scr[:, sl] = qk[:, M_HEAD_DIM:]
        v_scr[:, sl] = vo[:, :M_HEAD_DIM]
        o_scr[:, sl] = vo[:, M_HEAD_DIM:]
    return (jnp.dot(q_scr[...].astype(BF16), wif_ref[0:M_WIDTH, :], preferred_element_type=F32)
            + jnp.dot(k_scr[...].astype(BF16), wif_ref[M_WIDTH:2 * M_WIDTH, :], preferred_element_type=F32)
            + jnp.dot(v_scr[...].astype(BF16), wif_ref[2 * M_WIDTH:3 * M_WIDTH, :], preferred_element_type=F32)
            + _vec(vec_ref, "bif"))


def _head_output(hh, o_pre, xc_h, z_h, mnw_h, mskip_h):
    mu = jnp.mean(hh, axis=1, keepdims=True)
    hc = hh - mu
    var = jnp.mean(hc * hc, axis=1, keepdims=True)
    hn = hc * lax.rsqrt(var + EPS) * mnw_h
    return (jax.nn.sigmoid(o_pre) * hn + mskip_h * xc_h) * _silu(z_h)


def _rglru_coeffs(xcr, vec_ref, rwa_ref, rwx_ref):
    xcr_b = xcr.astype(BF16)
    half = R_WIDTH // 2
    ra_pre = jnp.concatenate(
        [jnp.dot(xcr_b[:, :half], rwa_ref[0], preferred_element_type=F32),
         jnp.dot(xcr_b[:, half:], rwa_ref[1], preferred_element_type=F32)], axis=1) + _vec(vec_ref, "rba")
    rx_pre = jnp.concatenate(
        [jnp.dot(xcr_b[:, :half], rwx_ref[0], preferred_element_type=F32),
         jnp.dot(xcr_b[:, half:], rwx_ref[1], preferred_element_type=F32)], axis=1) + _vec(vec_ref, "rbx")
    log_a = (-RG_C) * jax.nn.sigmoid(ra_pre) * _softplus(-_vec(vec_ref, "rlam"))
    a = jnp.exp(log_a)
    th = jnp.tanh(log_a)
    one_minus_a2 = (-2.0 * th) / (1.0 - th)
    return a, jnp.sqrt(one_minus_a2) * (jax.nn.sigmoid(rx_pre) * xcr)


def _s5_project(u_b, wbu_ref, re_ref, im_ref):
    per_slice = (LANES // S_GROUP) * S_STATE
    for k in range(S_WIDTH // LANES):
        res = jnp.dot(u_b[:, k * LANES:(k + 1) * LANES], wbu_ref[k], preferred_element_type=F32)
        re_ref[:, k * per_slice:(k + 1) * per_slice] = res[:, :per_slice]
        im_ref[:, k * per_slice:(k + 1) * per_slice] = res[:, per_slice:]


def _s5_output(re_ref, im_ref, u, vec_ref, wcre_ref, wcim_ref, wglu_ref):
    nblk = wcre_ref.shape[0]
    k_blk = S_LANES // nblk
    parts = []
    for m in range(nblk):
        ks = slice(m * k_blk, (m + 1) * k_blk)
        parts.append(lax.dot_general(re_ref[:, ks].astype(BF16), wcre_ref[m], _NT, preferred_element_type=F32)
                     - lax.dot_general(im_ref[:, ks].astype(BF16), wcim_ref[m], _NT, preferred_element_type=F32))
    y = jnp.concatenate(parts, axis=1) + _vec(vec_ref, "sd") * u
    g = jax.nn.gelu(y)
    return g * jax.nn.sigmoid(jnp.dot(g.astype(BF16), wglu_ref[...], preferred_element_type=F32)
                              + _vec(vec_ref, "bglu"))


_MATRIX_KEYS = ("wqk", "wvo", "wif", "rwa", "rwx", "wbu", "wcre", "wcim", "wglu")


def _tile(x, i):
    return x[i * SUBLANES:(i + 1) * SUBLANES]


def _conv_interleaved(x, tail_ref, w, bias, rows):
    ntiles = rows // SUBLANES
    pos = lax.broadcasted_iota(jnp.int32, (SUBLANES, x.shape[1]), 0)
    prev = tail_ref[...]
    before = [pltpu.roll(jnp.where(pos == SUBLANES - 1, _tile(prev, CONV_W - 1 - d), _tile(x, ntiles - d)), 1, 0)
              for d in range(1, CONV_W)]
    out = w[CONV_W - 1:CONV_W, :] * x + bias
    for j in range(1, CONV_W):
        shifted = jnp.concatenate(before[:j][::-1] + [x[:rows - j * SUBLANES]], axis=0)
        out = out + w[CONV_W - 1 - j:CONV_W - j, :] * shifted
    tail_ref[...] = x[rows - (CONV_W - 1) * SUBLANES:]
    return out


def _cumsum_interleaved(x, rows):
    tiles = [_tile(x, 0)]
    for i in range(1, rows // SUBLANES):
        tiles.append(tiles[-1] + _tile(x, i))
    total = tiles[-1]
    start = _tile_cumsum(total) - total
    return jnp.concatenate([t + start for t in tiles], axis=0)


def _cummax_interleaved(x, rows):
    tiles = [_tile(x, 0)]
    for i in range(1, rows // SUBLANES):
        tiles.append(jnp.maximum(tiles[-1], _tile(x, i)))
    best = tiles[-1]
    pos = lax.broadcasted_iota(jnp.int32, best.shape, 0)
    s = 1
    while s < SUBLANES:
        best = jnp.maximum(best, jnp.where(pos >= s, pltpu.roll(best, s, 0), -jnp.inf))
        s *= 2
    start = jnp.where(pos == 0, -jnp.inf, pltpu.roll(best, 1, 0))
    return jnp.concatenate([jnp.maximum(t, start) for t in tiles], axis=0)


def _scan_real_interleaved(a_ref, b_ref, carry, rows):
    ntiles = rows // SUBLANES
    pos = lax.broadcasted_iota(jnp.int32, (SUBLANES, a_ref.shape[1]), 0)
    h = _tile(b_ref, 0)
    aprod = _tile(a_ref, 0)
    for i in range(1, ntiles):
        a = _tile(a_ref, i)
        h = a * h + _tile(b_ref, i)
        aprod = aprod * a
    g = _tile_scan_real(aprod, h + jnp.where(pos == 0, aprod * carry, 0.0))
    h = jnp.where(pos == 0, carry, pltpu.roll(g, 1, 0))
    for i in range(ntiles):
        h = _tile(a_ref, i) * h + _tile(b_ref, i)
        b_ref[i * SUBLANES:(i + 1) * SUBLANES, :] = h
    return g[SUBLANES - 1:SUBLANES]


def _scan_cplx_interleaved(re_ref, im_ref, p_re, p_im, cre_ref, cim_ref, rows):
    ntiles = rows // SUBLANES
    width = re_ref.shape[1]
    pos = lax.broadcasted_iota(jnp.int32, (SUBLANES, SCAN_LANE_BLOCK), 0)
    for blk in range(width // SCAN_LANE_BLOCK):
        sl = slice(blk * SCAN_LANE_BLOCK, (blk + 1) * SCAN_LANE_BLOCK)
        pr = jnp.broadcast_to(p_re[:, sl], pos.shape)
        pi = jnp.broadcast_to(p_im[:, sl], pos.shape)
        c_r = cre_ref[:, sl]
        c_i = cim_ref[:, sl]
        sr = re_ref[0:SUBLANES, sl]
        si = im_ref[0:SUBLANES, sl]
        for i in range(1, ntiles):
            rs = slice(i * SUBLANES, (i + 1) * SUBLANES)
            sr, si = pr * sr - pi * si + re_ref[rs, sl], pr * si + pi * sr + im_ref[rs, sl]
        qr, qi = pr, pi
        n = 1
        while n < ntiles:
            qr, qi = qr * qr - qi * qi, 2.0 * (qr * qi)
            n *= 2
        gr, gi = _tile_scan_cplx(sr + jnp.where(pos == 0, qr * c_r - qi * c_i, 0.0),
                                 si + jnp.where(pos == 0, qr * c_i + qi * c_r, 0.0), qr, qi)
        sr = jnp.where(pos == 0, c_r, pltpu.roll(gr, 1, 0))
        si = jnp.where(pos == 0, c_i, pltpu.roll(gi, 1, 0))
        for i in range(ntiles):
            rs = slice(i * SUBLANES, (i + 1) * SUBLANES)
            sr, si = pr * sr - pi * si + re_ref[rs, sl], pr * si + pi * sr + im_ref[rs, sl]
            re_ref[rs, sl] = sr
            im_ref[rs, sl] = si
        cre_ref[:, sl] = gr[SUBLANES - 1:SUBLANES]
        cim_ref[:, sl] = gi[SUBLANES - 1:SUBLANES]


def _prompt_mixer_kernel(*refs, rows, nchunk, n_alias):
    (xnext_ref, xfirst_ref, vec_ref, win_ref,
     wqk_ref, wvo_ref, wif_ref, rwa_ref, rwx_ref, wbu_ref, wcre_ref, wcim_ref, wglu_ref) = refs[:13]
    (mixed_ref, c_ref, n_ref, m_ref, mtail_ref, h_ref, rtail_ref, sre_ref, sim_ref,
     q_scr, k_scr, v_scr, o_scr, ra_scr, rb_scr, ure_scr, uim_scr, proj_scr, xn_scr) = refs[13 + n_alias:]
    T = rows
    step = pl.program_id(0)
    slot = lax.rem(step, 2)
    proj_ref = proj_scr.at[slot]
    next_proj_ref = proj_scr.at[1 - slot]
    slab = (2 * MIX_WIDTH) // M_HEADS

    @pl.when(step == 0)
    def _():
        xn = _rmsnorm_bf16(xfirst_ref[...], vec_ref)
        for j in range(M_HEADS):
            cols = slice(j * slab, (j + 1) * slab)
            proj_scr[0, :, cols] = jnp.dot(xn, win_ref[:, cols], preferred_element_type=F32)

    @pl.when(lax.rem(step, nchunk) == 0)
    def _():
        for ref in (c_ref, n_ref, m_ref, mtail_ref, h_ref, rtail_ref, sre_ref, sim_ref):
            ref[...] = jnp.zeros(ref.shape, F32)

    xn_scr[...] = _rmsnorm_bf16(xnext_ref[...], vec_ref)

    xm = proj_ref[:, 0:M_WIDTH]
    xc = _silu(_conv_interleaved(xm, mtail_ref, _vec(vec_ref, "mcw"), _vec(vec_ref, "mcb"), T))
    gates = _mlstm_project(xc, xm, vec_ref, wqk_ref, wvo_ref, wif_ref, q_scr, k_scr, v_scr, o_scr)
    log_i = gates[:, :LANES]
    b_all = _cumsum_interleaved(_log_sigmoid(gates[:, LANES:]), T)
    a_max = _cummax_interleaved(log_i - b_all, T)
    m_prev = m_ref[...]
    log_inter = b_all + m_prev
    m_t_all = jnp.maximum(log_inter, b_all + a_max)
    w_inter_all = jnp.exp(log_inter - m_t_all)
    floor_all = jnp.exp(-m_t_all)
    b_end = b_all[T - 1:T, :]
    m_new = jnp.maximum(b_end + m_prev, b_end + a_max[T - 1:T, :])
    w_src_all = jnp.exp(b_end - b_all + log_i - m_new)
    decay_all = jnp.exp(b_end + m_prev - m_new)
    m_ref[...] = m_new
    b_t = b_all.T
    li_t = log_i.T

    rowi = lax.broadcasted_iota(jnp.int32, (T, T), 0)
    coli = lax.broadcasted_iota(jnp.int32, (T, T), 1)
    sub_len = T // SUBLANES

    def time_of(r):
        return (r & (SUBLANES - 1)) * sub_len + (r >> SUBLANE_SHIFT)

    causal = time_of(coli) <= time_of(rowi)
    k_scale = M_HEAD_DIM ** -0.5
    mnw = _vec(vec_ref, "mnw")
    mskip = _vec(vec_ref, "mskip")

    for h in range(M_HEADS):
        sl = slice(h * M_HEAD_DIM, (h + 1) * M_HEAD_DIM)
        c_prev = c_ref[h]
        n_prev = n_ref[h:h + 1, :]
        w_inter = w_inter_all[:, h:h + 1]
        w_src = w_src_all[:, h:h + 1]
        decay = decay_all[:, h:h + 1]

        q = q_scr[:, sl]
        ks = k_scr[:, sl] * k_scale
        v = v_scr[:, sl]
        q_b = q.astype(BF16)
        ks_b = ks.astype(BF16)

        log_d = jnp.where(causal, b_all[:, h:h + 1] - b_t[h:h + 1, :] + li_t[h:h + 1, :], -jnp.inf)
        w_intra = jnp.exp(log_d - m_t_all[:, h:h + 1])
        s = lax.dot_general(q_b, ks_b, _NT, preferred_element_type=F32) * w_intra
        inter = lax.dot_general(q_b, c_prev.astype(BF16), _NT, preferred_element_type=F32)
        num = jnp.dot(s.astype(BF16), v.astype(BF16), preferred_element_type=F32) + w_inter * inter
        den = (jnp.sum(s, axis=1, keepdims=True)
               + w_inter * jnp.sum(q * n_prev, axis=1, keepdims=True))
        hh = num / jnp.maximum(jnp.abs(den), floor_all[:, h:h + 1])

        vw_t = (v * w_src).T.astype(BF16)
        c_ref[h] = decay * c_prev + jnp.dot(vw_t, ks_b, preferred_element_type=F32)
        n_ref[h:h + 1, :] = decay * n_prev + jnp.sum(ks * w_src, axis=0, keepdims=True)

        z = proj_ref[:, MIX_WIDTH + h * M_HEAD_DIM:MIX_WIDTH + (h + 1) * M_HEAD_DIM]
        mixed_ref[:, sl] = _head_output(hh, o_scr[:, sl], xc[:, sl], z, mnw[:, sl], mskip[:, sl]).astype(BF16)

        cols = slice(h * slab, (h + 1) * slab)
        next_proj_ref[:, cols] = jnp.dot(xn_scr[...], win_ref[:, cols], preferred_element_type=F32)

    xr = proj_ref[:, M_WIDTH:M_WIDTH + R_WIDTH]
    xcr = _conv_interleaved(xr, rtail_ref, _vec(vec_ref, "rcw"), _vec(vec_ref, "rcb"), T)
    a, bb = _rglru_coeffs(xcr, vec_ref, rwa_ref, rwx_ref)
    ra_scr[...] = a
    rb_scr[...] = bb
    h_ref[...] = _scan_real_interleaved(ra_scr, rb_scr, h_ref[...], T)
    zr = proj_ref[:, MIX_WIDTH + M_WIDTH:MIX_WIDTH + M_WIDTH + R_WIDTH]
    mixed_ref[:, M_WIDTH:M_WIDTH + R_WIDTH] = (rb_scr[...] * _silu(zr)).astype(BF16)

    u = proj_ref[:, M_WIDTH + R_WIDTH:MIX_WIDTH]
    _s5_project(u.astype(BF16), wbu_ref, ure_scr, uim_scr)
    _scan_cplx_interleaved(ure_scr, uim_scr, _vec(vec_ref, "abre"), _vec(vec_ref, "abim"), sre_ref, sim_ref, T)
    glu = _s5_output(ure_scr, uim_scr, u, vec_ref, wcre_ref, wcim_ref, wglu_ref)
    zs = proj_ref[:, MIX_WIDTH + M_WIDTH + R_WIDTH:2 * MIX_WIDTH]
    mixed_ref[:, M_WIDTH + R_WIDTH:MIX_WIDTH] = (glu * _silu(zs)).astype(BF16)


_PROMPT_STATE_SHAPES = ((M_HEADS, M_HEAD_DIM, M_HEAD_DIM), (M_HEADS, M_HEAD_DIM), (1, LANES),
                        ((CONV_W - 1) * SUBLANES, M_WIDTH), (1, R_WIDTH), ((CONV_W - 1) * SUBLANES, R_WIDTH),
                        (1, S_LANES), (1, S_LANES))


def _prompt_mixer(x, total_rows, l, bsz, seq, vec, w_in_b, mats, prev_states):
    rows = PROMPT_CHUNK
    nchunk = seq // rows
    nstep = bsz * nchunk
    aliases = {}
    alias_args = []
    n_in = 4 + len(mats)
    if prev_states is not None:
        alias_args = list(prev_states)
        aliases = {n_in + k: 1 + k for k in range(len(alias_args))}

    def state_spec(shape):
        nd = len(shape)
        return pl.BlockSpec((None, None) + shape, lambda t: (l, t // nchunk) + (0,) * nd)

    out_shape = ([jax.ShapeDtypeStruct((total_rows, MIX_WIDTH), BF16)]
                 + [jax.ShapeDtypeStruct((DEPTH, bsz) + s, F32) for s in _PROMPT_STATE_SHAPES])
    scratch = ([pltpu.VMEM((rows, w), F32)
                for w in (M_WIDTH, M_WIDTH, M_WIDTH, M_WIDTH, R_WIDTH, R_WIDTH, S_LANES, S_LANES)]
               + [pltpu.VMEM((2, rows, 2 * MIX_WIDTH), F32), pltpu.VMEM((rows, D_MODEL), BF16)])
    return pl.pallas_call(
        functools.partial(_prompt_mixer_kernel, rows=rows, nchunk=nchunk, n_alias=len(alias_args)),
        grid=(nstep,),
        in_specs=([pl.BlockSpec((rows, D_MODEL), lambda t: (jnp.minimum(t + 1, nstep - 1), 0)),
                   pl.BlockSpec((rows, D_MODEL), lambda t: (0, 0), pipeline_mode=pl.Buffered(1)),
                   _layer_spec(vec, l),
                   pl.BlockSpec(w_in_b.shape, lambda t: (0, 0), pipeline_mode=pl.Buffered(1))]
                  + [_layer_spec(w, l) for w in mats] + [_ANY_SPEC] * len(alias_args)),
        out_specs=([pl.BlockSpec((rows, MIX_WIDTH), lambda t: (t, 0))]
                   + [state_spec(s) for s in _PROMPT_STATE_SHAPES]),
        out_shape=out_shape,
        scratch_shapes=scratch,
        input_output_aliases=aliases,
        compiler_params=pltpu.CompilerParams(
            dimension_semantics=("arbitrary",), vmem_limit_bytes=VMEM_LIMIT_BYTES),
        name="prompt_mixer",
    )(x, x, vec, w_in_b, *mats, *alias_args)


def _seg_last(x, groups):
    x3 = x.reshape(groups, SUBLANES, x.shape[-1])
    return jnp.broadcast_to(x3[:, SUBLANES - 1:SUBLANES, :], x3.shape).reshape(x.shape)


def _seg_max(x, groups):
    x3 = x.reshape(groups, SUBLANES, x.shape[-1])
    return jnp.broadcast_to(jnp.max(x3, axis=1, keepdims=True), x3.shape).reshape(x.shape)


def _seg_sum(x, groups):
    x3 = x.reshape(groups, SUBLANES, x.shape[-1])
    return jnp.broadcast_to(jnp.sum(x3, axis=1, keepdims=True), x3.shape).reshape(x.shape)


def _seg_rows(state_ref, lanes=slice(None)):
    x = state_ref[:, :, lanes]
    return jnp.broadcast_to(x, (x.shape[0], SUBLANES, x.shape[2])).reshape(x.shape[0] * SUBLANES, x.shape[2])


def _seg_state(x):
    x3 = x.reshape(x.shape[0] // SUBLANES, SUBLANES, x.shape[-1])
    return x3[:, SUBLANES - 1:SUBLANES, :]


def _with_history(x, buf_ref, new_buf_ref):
    g = x.shape[0] // SUBLANES
    x3 = x.reshape(g, SUBLANES, x.shape[1])
    lead = SAMPLE_LEAD - (CONV_W - 1)
    full = jnp.concatenate([jnp.zeros((g, lead, x.shape[1]), F32), buf_ref[...], x3[:, SAMPLE_LEAD:, :]], axis=1)
    new_buf_ref[...] = full[:, SUBLANES - (CONV_W - 1):, :]
    return full.reshape(x.shape)


def _conv_rolled(xf, w, bias):
    out = w[CONV_W - 1:CONV_W, :] * xf + bias
    for j in range(1, CONV_W):
        out = out + w[CONV_W - 1 - j:CONV_W - j, :] * pltpu.roll(xf, j, 0)
    return out


_SAMPLE_STATE_WIDTHS = (M_WIDTH, LANES, M_WIDTH, R_WIDTH, R_WIDTH, S_LANES, S_LANES)
_SAMPLE_STATE_ROWS = (1, 1, CONV_W - 1, 1, CONV_W - 1, 1, 1)


def _sample_mixer_kernel(*refs, rows, c_seqs, n_alias):
    (proj_ref, c0_ref, n0_ref, m0_ref, mtail0_ref, h0_ref, rtail0_ref, sre0_ref, sim0_ref,
     vec_ref, wqk_ref, wvo_ref, wif_ref, rwa_ref, rwx_ref, wbu_ref, wcre_ref, wcim_ref, wglu_ref) = refs[:19]
    (mixed_ref, c_ref, n_ref, m_ref, mconv_ref, h_ref, rconv_ref, sre_ref, sim_ref,
     q_scr, k_scr, v_scr, o_scr, xc_scr, intert_scr, vwt_scr, qt_scr, ksb_scr, dec_scr, b_scr, li_scr,
     ure_scr, uim_scr) = refs[19 + n_alias:]
    R = rows
    G = R // SUBLANES
    sub = pl.program_id(1)
    k_scale = M_HEAD_DIM ** -0.5
    pos = _sublane_pos((R, 1))
    valid = pos >= SAMPLE_LEAD
    first_token = pos == SAMPLE_LEAD

    @pl.when(sub == 0)
    def _():
        xm = _with_history(proj_ref[:, 0:M_WIDTH], mtail0_ref, mconv_ref)
        xc = _silu(_conv_rolled(xm, _vec(vec_ref, "mcw"), _vec(vec_ref, "mcb")))
        xc_scr[...] = xc
        gates = _mlstm_project(xc, xm, vec_ref, wqk_ref, wvo_ref, wif_ref, q_scr, k_scr, v_scr, o_scr)
        log_i = jnp.where(valid, gates[:, :LANES], -jnp.inf)
        log_f = jnp.where(valid, _log_sigmoid(gates[:, LANES:]), 0.0)
        b_all = _tile_cumsum(log_f)
        b_end = _seg_last(b_all, G)
        m_rows = _seg_rows(m0_ref)
        n_rows = _seg_rows(n0_ref)
        log_src = b_end - b_all + log_i
        m_new = jnp.maximum(b_end + m_rows, _seg_max(log_src, G))
        w_src_all = jnp.exp(log_src - m_new)
        decay_all = jnp.exp(b_end + m_rows - m_new)
        dec_scr[...] = decay_all
        b_scr[...] = b_all
        li_scr[...] = log_i
        m_ref[...] = _seg_state(m_new)
        for h in range(M_HEADS):
            sl = slice(h * M_HEAD_DIM, (h + 1) * M_HEAD_DIM)
            ks = k_scr[:, sl] * k_scale
            w_src = w_src_all[:, h:h + 1]
            n_ref[:, :, sl] = _seg_state(decay_all[:, h:h + 1] * n_rows[:, sl] + _seg_sum(ks * w_src, G))
            vwt_scr[h] = (v_scr[:, sl] * w_src).T
            qt_scr[h] = q_scr[:, sl].T.astype(BF16)
            ksb_scr[h] = ks.astype(BF16)
            intert_scr[h] = jnp.zeros((M_HEAD_DIM, R), F32)

        xr = _with_history(proj_ref[:, M_WIDTH:M_WIDTH + R_WIDTH], rtail0_ref, rconv_ref)
        xcr = _conv_rolled(xr, _vec(vec_ref, "rcw"), _vec(vec_ref, "rcb"))
        a, bb = _rglru_coeffs(xcr, vec_ref, rwa_ref, rwx_ref)
        hs = _tile_scan_real(a, jnp.where(valid, bb, 0.0) + jnp.where(first_token, a * _seg_rows(h0_ref), 0.0))
        h_ref[...] = _seg_state(hs)
        zr = proj_ref[:, MIX_WIDTH + M_WIDTH:MIX_WIDTH + M_WIDTH + R_WIDTH]
        mixed_ref[:, M_WIDTH:M_WIDTH + R_WIDTH] = jnp.where(valid, hs * _silu(zr), 0.0).astype(BF16)

        u = proj_ref[:, M_WIDTH + R_WIDTH:MIX_WIDTH]
        _s5_project(u.astype(BF16), wbu_ref, ure_scr, uim_scr)
        p_re = _vec(vec_ref, "abre")
        p_im = _vec(vec_ref, "abim")
        for blk in range(S_LANES // SCAN_LANE_BLOCK):
            sl = slice(blk * SCAN_LANE_BLOCK, (blk + 1) * SCAN_LANE_BLOCK)
            pr = jnp.broadcast_to(p_re[:, sl], (R, SCAN_LANE_BLOCK))
            pi = jnp.broadcast_to(p_im[:, sl], (R, SCAN_LANE_BLOCK))
            s0r = _seg_rows(sre0_ref, sl)
            s0i = _seg_rows(sim0_ref, sl)
            sr, si = _tile_scan_cplx(
                jnp.where(valid, ure_scr[:, sl], 0.0) + jnp.where(first_token, pr * s0r - pi * s0i, 0.0),
                jnp.where(valid, uim_scr[:, sl], 0.0) + jnp.where(first_token, pr * s0i + pi * s0r, 0.0), pr, pi)
            ure_scr[:, sl] = sr
            uim_scr[:, sl] = si
            sre_ref[:, :, sl] = _seg_state(sr)
            sim_ref[:, :, sl] = _seg_state(si)
        glu = _s5_output(ure_scr, uim_scr, u, vec_ref, wcre_ref, wcim_ref, wglu_ref)
        zs = proj_ref[:, MIX_WIDTH + M_WIDTH + R_WIDTH:2 * MIX_WIDTH]
        mixed_ref[:, M_WIDTH + R_WIDTH:MIX_WIDTH] = jnp.where(valid, glu * _silu(zs), 0.0).astype(BF16)

    lane_seq = lax.broadcasted_iota(jnp.int32, (M_HEAD_DIM, R), 1) >> SUBLANE_SHIFT
    seq0 = sub * c_seqs
    for h in range(M_HEADS):
        c_old = c0_ref[:, h].reshape(c_seqs * M_HEAD_DIM, M_HEAD_DIM)
        readout = jnp.dot(c_old.astype(BF16), qt_scr[h], preferred_element_type=F32)
        acc = intert_scr[h]
        vwt = vwt_scr[h]
        lhs = []
        for s in range(c_seqs):
            own = lane_seq == seq0 + s
            acc = jnp.where(own, readout[s * M_HEAD_DIM:(s + 1) * M_HEAD_DIM], acc)
            lhs.append(jnp.where(own, vwt, 0.0))
        intert_scr[h] = acc
        upd = jnp.dot(jnp.concatenate(lhs, axis=0).astype(BF16), ksb_scr[h], preferred_element_type=F32)
        for s in range(c_seqs):
            r0 = pl.multiple_of((seq0 + s) * SUBLANES, SUBLANES)
            c_ref[s, h] = (dec_scr[pl.ds(r0, 1), h:h + 1] * c0_ref[s, h]
                           + upd[s * M_HEAD_DIM:(s + 1) * M_HEAD_DIM])

    @pl.when(sub == pl.num_programs(1) - 1)
    def _():
        _sample_heads(proj_ref, m0_ref, n0_ref, vec_ref, mixed_ref,
                      q_scr, k_scr, v_scr, o_scr, xc_scr, intert_scr, b_scr, li_scr, valid, R)


def _sample_heads(proj_ref, m0_ref, n0_ref, vec_ref, mixed_ref,
                  q_scr, k_scr, v_scr, o_scr, xc_scr, intert_scr, b_scr, li_scr, valid, R):
    rowi = lax.broadcasted_iota(jnp.int32, (R, R), 0)
    coli = lax.broadcasted_iota(jnp.int32, (R, R), 1)
    same_causal = jnp.logical_and(coli <= rowi, (coli >> SUBLANE_SHIFT) == (rowi >> SUBLANE_SHIFT))
    k_scale = M_HEAD_DIM ** -0.5
    mnw = _vec(vec_ref, "mnw")
    mskip = _vec(vec_ref, "mskip")
    n_rows = _seg_rows(n0_ref)
    b_all = b_scr[...]
    log_i = li_scr[...]
    log_inter = b_all + _seg_rows(m0_ref)
    m_t_all = jnp.maximum(log_inter, b_all + _tile_cummax(log_i - b_all))
    w_inter_all = jnp.exp(log_inter - m_t_all)
    floor_all = jnp.exp(-m_t_all)
    b_t = b_all.T
    li_t = log_i.T

    for h in range(M_HEADS):
        sl = slice(h * M_HEAD_DIM, (h + 1) * M_HEAD_DIM)
        q = q_scr[:, sl]
        ks = k_scr[:, sl] * k_scale
        v = v_scr[:, sl]
        q_b = q.astype(BF16)
        ks_b = ks.astype(BF16)
        w_inter = w_inter_all[:, h:h + 1]

        log_d = jnp.where(same_causal, b_all[:, h:h + 1] - b_t[h:h + 1, :] + li_t[h:h + 1, :], -jnp.inf)
        w_intra = jnp.exp(log_d - m_t_all[:, h:h + 1])
        s = lax.dot_general(q_b, ks_b, _NT, preferred_element_type=F32) * w_intra
        num = (jnp.dot(s.astype(BF16), v.astype(BF16), preferred_element_type=F32)
               + w_inter * intert_scr[h].T)
        den = (jnp.sum(s, axis=1, keepdims=True)
               + w_inter * jnp.sum(q * n_rows[:, sl], axis=1, keepdims=True))
        hh = num / jnp.maximum(jnp.abs(den), floor_all[:, h:h + 1])

        z = proj_ref[:, MIX_WIDTH + h * M_HEAD_DIM:MIX_WIDTH + (h + 1) * M_HEAD_DIM]
        out = _head_output(hh, o_scr[:, sl], xc_scr[:, sl], z, mnw[:, sl], mskip[:, sl])
        mixed_ref[:, sl] = jnp.where(valid, out, 0.0).astype(BF16)


def _sample_mixer(proj, l, row0, mixed_prev, c_all, row_state, vec, mats, prev_out):
    R = SAMPLE_ROWS
    nrows = proj.shape[0]
    nsub = (R // SUBLANES) // SAMPLE_C_SEQS
    blk0 = row0 // R
    c_spec = pl.BlockSpec((None, SAMPLE_C_SEQS, M_HEADS, M_HEAD_DIM, M_HEAD_DIM),
                          lambda i, j: (l, i * nsub + j, 0, 0, 0))

    def state_spec(width, rows_per_seq):
        return pl.BlockSpec((None, R // SUBLANES, rows_per_seq, width), lambda i, j: (l, i, 0, 0))

    def state_shape(width, rows_per_seq):
        return (DEPTH, nrows // SUBLANES, rows_per_seq, width)

    state_specs = [state_spec(w, r) for w, r in zip(_SAMPLE_STATE_WIDTHS, _SAMPLE_STATE_ROWS)]
    alias_args = [mixed_prev] + (list(prev_out) if prev_out is not None else [])
    first_alias = 2 + len(row_state) + 1 + len(mats)
    aliases = {first_alias + k: k for k in range(len(alias_args))}
    out_shape = ([jax.ShapeDtypeStruct(mixed_prev.shape, BF16), jax.ShapeDtypeStruct(c_all.shape, F32)]
                 + [jax.ShapeDtypeStruct(state_shape(w, r), F32)
                    for w, r in zip(_SAMPLE_STATE_WIDTHS, _SAMPLE_STATE_ROWS)])
    scratch = ([pltpu.VMEM((R, M_WIDTH), F32)] * 5
               + [pltpu.VMEM((M_HEADS, M_HEAD_DIM, R), F32), pltpu.VMEM((M_HEADS, M_HEAD_DIM, R), F32),
                  pltpu.VMEM((M_HEADS, M_HEAD_DIM, R), BF16), pltpu.VMEM((M_HEADS, R, M_HEAD_DIM), BF16)]
               + [pltpu.VMEM((R, LANES), F32)] * 3
               + [pltpu.VMEM((R, S_LANES), F32)] * 2)
    return pl.pallas_call(
        functools.partial(_sample_mixer_kernel, rows=R, c_seqs=SAMPLE_C_SEQS, n_alias=len(alias_args)),
        grid=(nrows // R, nsub),
        in_specs=([pl.BlockSpec((R, 2 * MIX_WIDTH), lambda i, j: (i, 0)), c_spec] + state_specs
                  + [_layer_spec(vec, l)] + [_layer_spec(w, l) for w in mats] + [_ANY_SPEC] * len(alias_args)),
        out_specs=[pl.BlockSpec((R, MIX_WIDTH), lambda i, j: (blk0 + i, 0)), c_spec] + state_specs,
        out_shape=out_shape,
        scratch_shapes=scratch,
        input_output_aliases=aliases,
        compiler_params=pltpu.CompilerParams(
            dimension_semantics=("parallel", "arbitrary"), vmem_limit_bytes=VMEM_LIMIT_BYTES),
        name="sample_mixer",
    )(proj, c_all, *row_state, vec, *mats, *alias_args)


def _prepare_weights(p, s5):
    abre, abim, bbre_t, bbim_t = s5

    def split_gates(g):
        pad = [(0, 0)] * (g.ndim - 1) + [(0, LANES - M_HEADS)]
        return jnp.concatenate([jnp.pad(g[..., :M_HEADS], pad), jnp.pad(g[..., M_HEADS:], pad)], axis=-1)

    table = {"mcw": p["m_conv_w"], "rcw": p["r_conv_w"], "mcb": p["m_conv_b"], "mnw": p["m_norm_w"],
             "mskip": p["m_skip"], "rcb": p["r_conv_b"], "rba": p["r_ba"], "rbx": p["r_bx"], "rlam": p["r_lam"],
             "sd": p["s_d"], "bglu": p["s_b_glu"], "bif": split_gates(p["m_b_if"]), "abre": abre, "abim": abim,
             "norm_w": p["norm_w"]}

    def group_rows(group):
        parts = [table[name].reshape(DEPTH, n, w) for name, n, w in group]
        fill = _VEC_WIDTH - sum(w for _, _, w in group)
        if fill:
            parts.append(jnp.zeros((DEPTH, group[0][1], fill), F32))
        return parts[0] if len(parts) == 1 else jnp.concatenate(parts, axis=-1)

    vec = jnp.concatenate([group_rows(g) for g in _VEC_GROUPS]
                          + [jnp.zeros((DEPTH, _VEC_ROWS - _VEC_USED_ROWS, _VEC_WIDTH), F32)], axis=1)

    def block_diag_halves(w):
        nb = R_BLOCKS // 2
        w5 = w.reshape(DEPTH, 2, nb, w.shape[-2], w.shape[-1])
        eye = jnp.eye(nb, dtype=F32)[:, None, :, None]
        return (w5[:, :, :, :, None, :] * eye).reshape(
            DEPTH, 2, nb * w.shape[-2], nb * w.shape[-1]).astype(BF16)

    n_in = S_WIDTH // LANES
    g_in = LANES // S_GROUP
    n_out = 2
    g_out = S_GROUPS // n_out

    def bu_blocks(bb_t):
        bb5 = bb_t.reshape(DEPTH, S_GROUP, n_in, g_in, S_STATE)
        return jnp.einsum("ab,dckbp->dkacbp", jnp.eye(g_in, dtype=F32), bb5).reshape(
            DEPTH, n_in, LANES, g_in * S_STATE)

    def c_blocks(c):
        c5 = c.reshape(DEPTH, n_out, g_out, S_GROUP, S_STATE)
        eye = jnp.eye(g_out, dtype=F32)[:, None, :, None]
        return (c5[:, :, :, :, None, :] * eye).reshape(
            DEPTH, n_out, g_out * S_GROUP, g_out * S_STATE).astype(BF16)

    mats = {
        "wqk": jnp.concatenate([p["m_wq"], p["m_wk"]], axis=-1).astype(BF16),
        "wvo": jnp.concatenate([p["m_wv"], p["m_wo"]], axis=-1).astype(BF16),
        "wif": split_gates(p["m_w_if"]).astype(BF16),
        "rwa": block_diag_halves(p["r_wa"]), "rwx": block_diag_halves(p["r_wx"]),
        "wbu": jnp.concatenate([bu_blocks(bbre_t), bu_blocks(bbim_t)], axis=-1).astype(BF16),
        "wcre": c_blocks(p["s_c_re"]), "wcim": c_blocks(p["s_c_im"]),
        "wglu": p["s_w_glu"].astype(BF16),
    }
    return vec, [mats[k] for k in _MATRIX_KEYS]


def _sample_rows_state(n, m, mconv, h, rconv, sre, sim):
    bsz = n.shape[1]

    def per_seq(x):
        return x.reshape(DEPTH, bsz, 1, x.shape[-1])

    m_pad = jnp.pad(m, ((0, 0), (0, 0), (0, LANES - M_HEADS)))
    return (per_seq(n.reshape(DEPTH, bsz, M_WIDTH)), per_seq(m_pad), mconv, per_seq(h), rconv,
            per_seq(sre.reshape(DEPTH, bsz, S_LANES)), per_seq(sim.reshape(DEPTH, bsz, S_LANES)))


def _sample_state_from_rows(n, m, mconv, h, rconv, sre, sim):
    bsz = n.shape[1]
    return (n.reshape(DEPTH, bsz, M_HEADS, M_HEAD_DIM), m.reshape(DEPTH, bsz, LANES)[..., :M_HEADS],
            mconv, h.reshape(DEPTH, bsz, R_WIDTH), rconv,
            sre.reshape(DEPTH, bsz, S_GROUPS, S_STATE), sim.reshape(DEPTH, bsz, S_GROUPS, S_STATE))


def _prompt_state_from_kernel(c, n, m, mtail, h, rtail, sre, sim):
    bsz = c.shape[1]
    return (c, n, m[:, :, 0, :M_HEADS], mtail[:, :, SUBLANES - 1::SUBLANES], h.reshape(DEPTH, bsz, R_WIDTH),
            rtail[:, :, SUBLANES - 1::SUBLANES],
            sre.reshape(DEPTH, bsz, S_GROUPS, S_STATE), sim.reshape(DEPTH, bsz, S_GROUPS, S_STATE))


def _interleave_chunks(x, inverse=False):
    bsz, seq, d = x.shape
    sub_len = PROMPT_CHUNK // SUBLANES
    inner = (sub_len, SUBLANES) if inverse else (SUBLANES, sub_len)
    return x.reshape(bsz, seq // PROMPT_CHUNK, *inner, d).transpose(0, 1, 3, 2, 4).reshape(bsz, seq, d)


def kernel(x_prompt, x_sample, state_mlstm_C, state_mlstm_n, state_mlstm_m, state_mlstm_conv, state_rglru_h, state_rglru_conv, state_s5_re, state_s5_im, norm_w, w_in, w_out, m_conv_w, m_conv_b, m_wq, m_wk, m_wv, m_wo, m_w_if, m_b_if, m_norm_w, m_skip, r_conv_w, r_conv_b, r_wa, r_ba, r_wx, r_bx, r_lam, s_lam_re, s_lam_im, s_b_re, s_b_im, s_c_re, s_c_im, s_d, s_log_step, s_w_glu, s_b_glu, final_norm_w):
    p = dict(norm_w=norm_w, w_in=w_in, w_out=w_out, m_conv_w=m_conv_w, m_conv_b=m_conv_b, m_wq=m_wq, m_wk=m_wk,
             m_wv=m_wv, m_wo=m_wo, m_w_if=m_w_if, m_b_if=m_b_if, m_norm_w=m_norm_w, m_skip=m_skip,
             r_conv_w=r_conv_w, r_conv_b=r_conv_b, r_wa=r_wa, r_ba=r_ba, r_wx=r_wx, r_bx=r_bx, r_lam=r_lam,
             s_c_re=s_c_re, s_c_im=s_c_im, s_d=s_d, s_w_glu=s_w_glu, s_b_glu=s_b_glu)
    s5 = _s5_prep(s_lam_re, s_lam_im, s_log_step, s_b_re, s_b_im)
    vec, mats = _prepare_weights(p, s5)
    final_w = final_norm_w.reshape(1, D_MODEL)

    bsz, seq, _ = x_prompt.shape
    dec_batch, dec_seq, _ = x_sample.shape
    n_prompt = bsz * seq
    n_sample = dec_batch * SUBLANES
    stream = [_interleave_chunks(x_prompt).reshape(n_prompt, D_MODEL),
              jnp.pad(x_sample, ((0, 0), (SAMPLE_LEAD, 0), (0, 0))).reshape(n_sample, D_MODEL)]
    rows_state = _sample_rows_state(state_mlstm_n, state_mlstm_m, state_mlstm_conv, state_rglru_h,
                                    state_rglru_conv, state_s5_re, state_s5_im)
    pr_states = None
    sa_states = None
    for l in range(DEPTH):
        last = l == DEPTH - 1
        proj_s, w_in_b = _inproj(stream[-1], stream[-1].shape[0] - n_sample, n_sample, vec, w_in, l)
        mixed, *pr_states = _prompt_mixer(stream[0], n_prompt + n_sample, l, bsz, seq, vec, w_in_b, mats, pr_states)
        mixed, *sa_states = _sample_mixer(proj_s, l, n_prompt, mixed, state_mlstm_C, rows_state, vec, mats, sa_states)
        stream = _outproj(mixed, stream, w_out, l, final_w, last, split_rows=n_prompt if last else None)
    y_prompt = _interleave_chunks(stream[0].reshape(bsz, seq, D_MODEL), inverse=True)
    y_sample = stream[1]
    return (y_prompt, y_sample, *_prompt_state_from_kernel(*pr_states),
            sa_states[0], *_sample_state_from_rows(*sa_states[1:]))
```

```python
import functools

import jax
import jax.numpy as jnp
from jax import lax
from jax.experimental import pallas as pl
from jax.experimental.pallas import tpu as pltpu

F32 = jnp.float32
BF16 = jnp.bfloat16

D_MODEL = 2048
DEPTH = 2
MIX_WIDTH = D_MODEL
M_WIDTH = MIX_WIDTH // 2
R_WIDTH = MIX_WIDTH // 4
S_WIDTH = MIX_WIDTH - M_WIDTH - R_WIDTH
M_HEADS = 8
M_HEAD_DIM = M_WIDTH // M_HEADS
R_BLOCKS = 8
RG_C = 8.0
S_GROUP = 16
S_GROUPS = S_WIDTH // S_GROUP
S_STATE = 64
S_LANES = S_GROUPS * S_STATE
CONV_W = 4
EPS = 1e-6

SUBLANES = 8
SUBLANE_SHIFT = SUBLANES.bit_length() - 1
LANES = 128
VMEM_LIMIT_BYTES = 56 * 1024 * 1024

PROMPT_CHUNK = 256
SAMPLE_ROWS = 128
SAMPLE_C_SEQS = 8
SAMPLE_LEAD = SUBLANES - 4
INPROJ_TM = 512
INPROJ_TN = 1024
OUTPROJ_TM = 512
SCAN_LANE_BLOCK = 256

_NT = (((1,), (1,)), ((), ()))

_VEC_WIDTH = D_MODEL
_VEC_GROUPS = ((("mcw", CONV_W, M_WIDTH), ("rcw", CONV_W, R_WIDTH)),
               (("mcb", 1, M_WIDTH), ("mnw", 1, M_WIDTH)),
               (("mskip", 1, M_WIDTH), ("rcb", 1, R_WIDTH), ("rba", 1, R_WIDTH)),
               (("rbx", 1, R_WIDTH), ("rlam", 1, R_WIDTH), ("sd", 1, S_WIDTH), ("bglu", 1, S_WIDTH)),
               (("bif", 1, 2 * LANES),),
               (("abre", 1, S_LANES),), (("abim", 1, S_LANES),), (("norm_w", 1, D_MODEL),))
_VEC_LAYOUT = {}
_row = 0
for _group in _VEC_GROUPS:
    _lane = 0
    for _name, _n, _w in _group:
        _VEC_LAYOUT[_name] = (_row, _n, _lane, _w)
        _lane += _w
    _row += _group[0][1]
_VEC_USED_ROWS = _row
_VEC_ROWS = -(-_row // SUBLANES) * SUBLANES


def _vec(vec_ref, name):
    r0, n, l0, w = _VEC_LAYOUT[name]
    return vec_ref[r0:r0 + n, l0:l0 + w]


def _layer_spec(arr, l):
    nd = arr.ndim - 1
    return pl.BlockSpec((None,) + arr.shape[1:], lambda *_: (l,) + (0,) * nd, pipeline_mode=pl.Buffered(1))


_ANY_SPEC = pl.BlockSpec(memory_space=pl.ANY)


S5_IN_SLICES = S_WIDTH // LANES
S5_IN_GROUPS = LANES // S_GROUP
S5_OUT_HALVES = 2
S5_OUT_GROUPS = S_GROUPS // S5_OUT_HALVES


def _s5_prep_kernel(lr_ref, li_ref, ls_ref, brt_ref, bit_ref, cre_ref, cim_ref,
                    abre_ref, abim_ref, wbu_ref, wcre_ref, wcim_ref):
    lr = lr_ref[0]
    li = li_ref[0]
    dt = jnp.exp(ls_ref[0])
    mag = jnp.exp(lr * dt)
    ang = li * dt
    ab_re = mag * jnp.cos(ang)
    ab_im = mag * jnp.sin(ang)
    den = lr * lr + li * li
    nr = ab_re - 1.0
    f_re = (nr * lr + ab_im * li) / den
    f_im = (ab_im * lr - nr * li) / den
    br = brt_ref[0]
    bi = bit_ref[0]
    abre_ref[0] = ab_re
    abim_ref[0] = ab_im
    bb_re = f_re * br - f_im * bi
    bb_im = f_re * bi + f_im * br

    per_slice = S5_IN_GROUPS * S_STATE
    lane_group = lax.broadcasted_iota(jnp.int32, (S_GROUP, per_slice), 1) >> (S_STATE.bit_length() - 1)
    for k in range(S5_IN_SLICES):
        re_k = bb_re[:, k * per_slice:(k + 1) * per_slice]
        im_k = bb_im[:, k * per_slice:(k + 1) * per_slice]
        for a in range(S5_IN_GROUPS):
            rows = slice(a * S_GROUP, (a + 1) * S_GROUP)
            wbu_ref[0, k, rows, 0:per_slice] = jnp.where(lane_group == a, re_k, 0.0).astype(BF16)
            wbu_ref[0, k, rows, per_slice:2 * per_slice] = jnp.where(lane_group == a, im_k, 0.0).astype(BF16)

    row_group = lax.broadcasted_iota(jnp.int32, (S5_OUT_GROUPS * S_GROUP, S_STATE), 0) >> (S_GROUP.bit_length() - 1)
    for src_ref, dst_ref in ((cre_ref, wcre_ref), (cim_ref, wcim_ref)):
        for m in range(S5_OUT_HALVES):
            c = src_ref[0, m]
            for a in range(S5_OUT_GROUPS):
                dst_ref[0, m, :, a * S_STATE:(a + 1) * S_STATE] = jnp.where(row_group == a, c, 0.0).astype(BF16)


def _s5_prep(s_lam_re, s_lam_im, s_log_step, s_b_re, s_b_im, s_c_re, s_c_im):
    lr = s_lam_re.reshape(DEPTH, 1, S_LANES)
    li = s_lam_im.reshape(DEPTH, 1, S_LANES)
    ls = jnp.repeat(s_log_step, S_STATE, axis=-1).reshape(DEPTH, 1, S_LANES)
    brt = s_b_re.reshape(DEPTH, S_LANES, S_GROUP).transpose(0, 2, 1)
    bit = s_b_im.reshape(DEPTH, S_LANES, S_GROUP).transpose(0, 2, 1)
    c_shape = (DEPTH, S5_OUT_HALVES, S5_OUT_GROUPS * S_GROUP, S_STATE)
    wbu_shape = (DEPTH, S5_IN_SLICES, LANES, 2 * S5_IN_GROUPS * S_STATE)
    wc_shape = (DEPTH, S5_OUT_HALVES, S5_OUT_GROUPS * S_GROUP, S5_OUT_GROUPS * S_STATE)

    def layer_block(shape):
        nd = len(shape) - 1
        return pl.BlockSpec((1,) + shape[1:], lambda l: (l,) + (0,) * nd)

    vec = layer_block((DEPTH, 1, S_LANES))
    mat = layer_block((DEPTH, S_GROUP, S_LANES))
    return pl.pallas_call(
        _s5_prep_kernel,
        grid=(DEPTH,),
        in_specs=[vec, vec, vec, mat, mat, layer_block(c_shape), layer_block(c_shape)],
        out_specs=[vec, vec, layer_block(wbu_shape), layer_block(wc_shape), layer_block(wc_shape)],
        out_shape=[jax.ShapeDtypeStruct((DEPTH, 1, S_LANES), F32),
                   jax.ShapeDtypeStruct((DEPTH, 1, S_LANES), F32),
                   jax.ShapeDtypeStruct(wbu_shape, BF16),
                   jax.ShapeDtypeStruct(wc_shape, BF16),
                   jax.ShapeDtypeStruct(wc_shape, BF16)],
        name="s5_prep",
    )(lr, li, ls, brt, bit, s_c_re.reshape(c_shape), s_c_im.reshape(c_shape))


def _stream_specs(parts, tm):
    if len(parts) == 1:
        return [pl.BlockSpec((tm, D_MODEL), lambda i: (i, 0))]
    nb0 = parts[0].shape[0] // tm
    return [pl.BlockSpec((tm, D_MODEL), lambda i: (jnp.minimum(i, nb0 - 1), 0)),
            pl.BlockSpec((tm, D_MODEL), lambda i: (jnp.maximum(i - nb0, 0), 0))]


def _read_stream(refs, nb0):
    if len(refs) == 1:
        return refs[0][...]
    return jnp.where(pl.program_id(0) < nb0, refs[0][...], refs[1][...])


def _rmsnorm_bf16(x, vec_ref):
    ms = jnp.mean(x * x, axis=-1, keepdims=True)
    return (x * lax.rsqrt(ms + EPS) * _vec(vec_ref, "norm_w")).astype(BF16)


def _inproj_kernel(x_ref, vec_ref, w_ref, o_ref, wb_ref, xn_scr):
    @pl.when(pl.program_id(0) == 0)
    def _():
        xn_scr[...] = _rmsnorm_bf16(x_ref[...], vec_ref)

    wb = w_ref[...].astype(BF16)
    wb_ref[...] = wb
    o_ref[...] = jnp.dot(xn_scr[...], wb, preferred_element_type=F32)


def _inproj(x, row0, nrows, vec, w_in, l):
    blk0 = row0 // nrows
    return pl.pallas_call(
        _inproj_kernel,
        grid=((2 * MIX_WIDTH) // INPROJ_TN,),
        in_specs=[pl.BlockSpec((nrows, D_MODEL), lambda j: (blk0, 0), pipeline_mode=pl.Buffered(1)),
                  _layer_spec(vec, l),
                  pl.BlockSpec((None, D_MODEL, INPROJ_TN), lambda j: (l, 0, j))],
        out_specs=[pl.BlockSpec((nrows, INPROJ_TN), lambda j: (0, j)),
                   pl.BlockSpec((D_MODEL, INPROJ_TN), lambda j: (0, j))],
        out_shape=[jax.ShapeDtypeStruct((nrows, 2 * MIX_WIDTH), F32),
                   jax.ShapeDtypeStruct((D_MODEL, 2 * MIX_WIDTH), BF16)],
        scratch_shapes=[pltpu.VMEM((nrows, D_MODEL), BF16)],
        compiler_params=pltpu.CompilerParams(
            dimension_semantics=("arbitrary",), vmem_limit_bytes=VMEM_LIMIT_BYTES),
        name="inproj",
    )(x, vec, w_in)


def _outproj_kernel(*refs, n_x, nb0, n_out, nb0_out, final):
    mixed_ref, x_refs = refs[0], refs[1:1 + n_x]
    w_ref, fw_ref = refs[1 + n_x:3 + n_x]
    o_refs, wb_scr = refs[3 + n_x:-1], refs[-1]

    @pl.when(pl.program_id(0) == 0)
    def _():
        wb_scr[...] = w_ref[...].astype(BF16)

    y = _read_stream(x_refs, nb0) + jnp.dot(mixed_ref[...], wb_scr[...], preferred_element_type=F32)
    if final:
        ms = jnp.mean(y * y, axis=-1, keepdims=True)
        y = y * lax.rsqrt(ms + EPS) * fw_ref[...]
    if n_out == 1:
        o_refs[0][...] = y
    else:
        @pl.when(pl.program_id(0) < nb0_out)
        def _():
            o_refs[0][...] = y

        @pl.when(pl.program_id(0) >= nb0_out)
        def _():
            o_refs[1][...] = y.reshape(y.shape[0] // SUBLANES, SUBLANES, y.shape[1])[:, SAMPLE_LEAD:, :]


def _outproj(mixed, x_parts, w_out, l, final_w, final, split_rows=None):
    n = mixed.shape[0]
    if split_rows is None:
        out_parts = [jax.ShapeDtypeStruct((n, D_MODEL), F32)]
        out_specs = [pl.BlockSpec((OUTPROJ_TM, D_MODEL), lambda i: (i, 0))]
        nb0_out = n // OUTPROJ_TM
    else:
        nb0_out = split_rows // OUTPROJ_TM
        tokens = SUBLANES - SAMPLE_LEAD
        out_parts = [jax.ShapeDtypeStruct((split_rows, D_MODEL), F32),
                     jax.ShapeDtypeStruct(((n - split_rows) // SUBLANES, tokens, D_MODEL), F32)]
        out_specs = [pl.BlockSpec((OUTPROJ_TM, D_MODEL), lambda i: (jnp.minimum(i, nb0_out - 1), 0)),
                     pl.BlockSpec((OUTPROJ_TM // SUBLANES, tokens, D_MODEL),
                                  lambda i: (jnp.maximum(i - nb0_out, 0), 0, 0))]
    return pl.pallas_call(
        functools.partial(_outproj_kernel, n_x=len(x_parts), nb0=x_parts[0].shape[0] // OUTPROJ_TM,
                          n_out=len(out_parts), nb0_out=nb0_out, final=final),
        grid=(n // OUTPROJ_TM,),
        in_specs=([pl.BlockSpec((OUTPROJ_TM, MIX_WIDTH), lambda i: (i, 0))] + _stream_specs(x_parts, OUTPROJ_TM)
                  + [_layer_spec(w_out, l), pl.BlockSpec((1, D_MODEL), lambda i: (0, 0))]),
        out_specs=out_specs,
        out_shape=out_parts,
        scratch_shapes=[pltpu.VMEM((MIX_WIDTH, D_MODEL), BF16)],
        compiler_params=pltpu.CompilerParams(
            dimension_semantics=("arbitrary",), vmem_limit_bytes=VMEM_LIMIT_BYTES),
        name="outproj",
    )(mixed, *x_parts, w_out, final_w)


def _silu(x):
    return x * jax.nn.sigmoid(x)


def _log_sigmoid(x):
    return jnp.minimum(x, 0.0) - jnp.log1p(jnp.exp(-jnp.abs(x)))


def _softplus(x):
    return jnp.maximum(x, 0.0) + jnp.log1p(jnp.exp(-jnp.abs(x)))


def _sublane_pos(shape):
    return lax.broadcasted_iota(jnp.int32, shape, 0) & (SUBLANES - 1)


def _tile_cumsum(x):
    pos = _sublane_pos(x.shape)
    s = 1
    while s < SUBLANES:
        x = x + jnp.where(pos >= s, pltpu.roll(x, s, 0), 0.0)
        s *= 2
    return x


def _tile_cummax(x):
    pos = _sublane_pos(x.shape)
    s = 1
    while s < SUBLANES:
        x = jnp.maximum(x, jnp.where(pos >= s, pltpu.roll(x, s, 0), -jnp.inf))
        s *= 2
    return x


def _tile_scan_real(a, b):
    pos = _sublane_pos(a.shape)
    s = 1
    while s < SUBLANES:
        m = pos >= s
        b = jnp.where(m, a * pltpu.roll(b, s, 0) + b, b)
        if 2 * s < SUBLANES:
            a = jnp.where(m, a * pltpu.roll(a, s, 0), a)
        s *= 2
    return b


def _tile_scan_cplx(sr, si, pr, pi):
    pos = _sublane_pos(sr.shape)
    s = 1
    while s < SUBLANES:
        m = pos >= s
        sr_sh = pltpu.roll(sr, s, 0)
        si_sh = pltpu.roll(si, s, 0)
        sr, si = (jnp.where(m, sr + (pr * sr_sh - pi * si_sh), sr),
                  jnp.where(m, si + (pr * si_sh + pi * sr_sh), si))
        if 2 * s < SUBLANES:
            pr, pi = pr * pr - pi * pi, 2.0 * (pr * pi)
        s *= 2
    return sr, si


def _mlstm_project(xc, xm, vec_ref, wqk_ref, wvo_ref, wif_ref, q_scr, k_scr, v_scr, o_scr):
    xc_b = xc.astype(BF16)
    xm_b = xm.astype(BF16)
    for h in range(M_HEADS):
        sl = slice(h * M_HEAD_DIM, (h + 1) * M_HEAD_DIM)
        qk = jnp.dot(xc_b[:, sl], wqk_ref[h], preferred_element_type=F32)
        vo = jnp.dot(xm_b[:, sl], wvo_ref[h], preferred_element_type=F32)
        q_scr[:, sl] = qk[:, :M_HEAD_DIM]
        k_scr[:, sl] = qk[:, M_HEAD_DIM:]
        v_scr[:, sl] = vo[:, :M_HEAD_DIM]
        o_scr[:, sl] = vo[:, M_HEAD_DIM:]
    return (jnp.dot(q_scr[...].astype(BF16), wif_ref[0:M_WIDTH, :], preferred_element_type=F32)
            + jnp.dot(k_scr[...].astype(BF16), wif_ref[M_WIDTH:2 * M_WIDTH, :], preferred_element_type=F32)
            + jnp.dot(v_scr[...].astype(BF16), wif_ref[2 * M_WIDTH:3 * M_WIDTH, :], preferred_element_type=F32)
            + _vec(vec_ref, "bif"))


def _head_output(hh, o_pre, xc_h, z_h, mnw_h, mskip_h):
    mu = jnp.mean(hh, axis=1, keepdims=True)
    hc = hh - mu
    var = jnp.mean(hc * hc, axis=1, keepdims=True)
    hn = hc * lax.rsqrt(var + EPS) * mnw_h
    return (jax.nn.sigmoid(o_pre) * hn + mskip_h * xc_h) * _silu(z_h)


def _rglru_coeffs(xcr, vec_ref, rwa_ref, rwx_ref):
    xcr_b = xcr.astype(BF16)
    half = R_WIDTH // 2
    ra_pre = jnp.concatenate(
        [jnp.dot(xcr_b[:, :half], rwa_ref[0], preferred_element_type=F32),
         jnp.dot(xcr_b[:, half:], rwa_ref[1], preferred_element_type=F32)], axis=1) + _vec(vec_ref, "rba")
    rx_pre = jnp.concatenate(
        [jnp.dot(xcr_b[:, :half], rwx_ref[0], preferred_element_type=F32),
         jnp.dot(xcr_b[:, half:], rwx_ref[1], preferred_element_type=F32)], axis=1) + _vec(vec_ref, "rbx")
    log_a = (-RG_C) * jax.nn.sigmoid(ra_pre) * _softplus(-_vec(vec_ref, "rlam"))
    a = jnp.exp(log_a)
    th = jnp.tanh(log_a)
    one_minus_a2 = (-2.0 * th) / (1.0 - th)
    return a, jnp.sqrt(one_minus_a2) * (jax.nn.sigmoid(rx_pre) * xcr)


def _s5_project(u_b, wbu_ref, re_ref, im_ref):
    per_slice = (LANES // S_GROUP) * S_STATE
    for k in range(S_WIDTH // LANES):
        res = jnp.dot(u_b[:, k * LANES:(k + 1) * LANES], wbu_ref[k], preferred_element_type=F32)
        re_ref[:, k * per_slice:(k + 1) * per_slice] = res[:, :per_slice]
        im_ref[:, k * per_slice:(k + 1) * per_slice] = res[:, per_slice:]


def _s5_output(re_ref, im_ref, u, vec_ref, wcre_ref, wcim_ref, wglu_ref):
    nblk = wcre_ref.shape[0]
    k_blk = S_LANES // nblk
    parts = []
    for m in range(nblk):
        ks = slice(m * k_blk, (m + 1) * k_blk)
        parts.append(lax.dot_general(re_ref[:, ks].astype(BF16), wcre_ref[m], _NT, preferred_element_type=F32)
                     - lax.dot_general(im_ref[:, ks].astype(BF16), wcim_ref[m], _NT, preferred_element_type=F32))
    y = jnp.concatenate(parts, axis=1) + _vec(vec_ref, "sd") * u
    g = jax.nn.gelu(y)
    return g * jax.nn.sigmoid(jnp.dot(g.astype(BF16), wglu_ref[...], preferred_element_type=F32)
                              + _vec(vec_ref, "bglu"))


_MATRIX_KEYS = ("wqk", "wvo", "wif", "rwa", "rwx", "wbu", "wcre", "wcim", "wglu")


def _tile(x, i):
    return x[i * SUBLANES:(i + 1) * SUBLANES]


def _conv_interleaved(x, tail_ref, w, bias, rows):
    ntiles = rows // SUBLANES
    pos = lax.broadcasted_iota(jnp.int32, (SUBLANES, x.shape[1]), 0)
    prev = tail_ref[...]
    before = [pltpu.roll(jnp.where(pos == SUBLANES - 1, _tile(prev, CONV_W - 1 - d), _tile(x, ntiles - d)), 1, 0)
              for d in range(1, CONV_W)]
    out = w[CONV_W - 1:CONV_W, :] * x + bias
    for j in range(1, CONV_W):
        shifted = jnp.concatenate(before[:j][::-1] + [x[:rows - j * SUBLANES]], axis=0)
        out = out + w[CONV_W - 1 - j:CONV_W - j, :] * shifted
    tail_ref[...] = x[rows - (CONV_W - 1) * SUBLANES:]
    return out


def _cumsum_interleaved(x, rows):
    tiles = [_tile(x, 0)]
    for i in range(1, rows // SUBLANES):
        tiles.append(tiles[-1] + _tile(x, i))
    total = tiles[-1]
    start = _tile_cumsum(total) - total
    return jnp.concatenate([t + start for t in tiles], axis=0)


def _cummax_interleaved(x, rows):
    tiles = [_tile(x, 0)]
    for i in range(1, rows // SUBLANES):
        tiles.append(jnp.maximum(tiles[-1], _tile(x, i)))
    best = tiles[-1]
    pos = lax.broadcasted_iota(jnp.int32, best.shape, 0)
    s = 1
    while s < SUBLANES:
        best = jnp.maximum(best, jnp.where(pos >= s, pltpu.roll(best, s, 0), -jnp.inf))
        s *= 2
    start = jnp.where(pos == 0, -jnp.inf, pltpu.roll(best, 1, 0))
    return jnp.concatenate([jnp.maximum(t, start) for t in tiles], axis=0)


def _scan_real_interleaved(a_ref, b_ref, carry, rows):
    ntiles = rows // SUBLANES
    pos = lax.broadcasted_iota(jnp.int32, (SUBLANES, a_ref.shape[1]), 0)
    h = _tile(b_ref, 0)
    aprod = _tile(a_ref, 0)
    for i in range(1, ntiles):
        a = _tile(a_ref, i)
        h = a * h + _tile(b_ref, i)
        aprod = aprod * a
    g = _tile_scan_real(aprod, h + jnp.where(pos == 0, aprod * carry, 0.0))
    h = jnp.where(pos == 0, carry, pltpu.roll(g, 1, 0))
    for i in range(ntiles):
        h = _tile(a_ref, i) * h + _tile(b_ref, i)
        b_ref[i * SUBLANES:(i + 1) * SUBLANES, :] = h
    return g[SUBLANES - 1:SUBLANES]


def _scan_cplx_interleaved(re_ref, im_ref, p_re, p_im, cre_ref, cim_ref, rows):
    ntiles = rows // SUBLANES
    width = re_ref.shape[1]
    pos = lax.broadcasted_iota(jnp.int32, (SUBLANES, SCAN_LANE_BLOCK), 0)
    for blk in range(width // SCAN_LANE_BLOCK):
        sl = slice(blk * SCAN_LANE_BLOCK, (blk + 1) * SCAN_LANE_BLOCK)
        pr = jnp.broadcast_to(p_re[:, sl], pos.shape)
        pi = jnp.broadcast_to(p_im[:, sl], pos.shape)
        c_r = cre_ref[:, sl]
        c_i = cim_ref[:, sl]
        sr = re_ref[0:SUBLANES, sl]
        si = im_ref[0:SUBLANES, sl]
        for i in range(1, ntiles):
            rs = slice(i * SUBLANES, (i + 1) * SUBLANES)
            sr, si = pr * sr - pi * si + re_ref[rs, sl], pr * si + pi * sr + im_ref[rs, sl]
        qr, qi = pr, pi
        n = 1
        while n < ntiles:
            qr, qi = qr * qr - qi * qi, 2.0 * (qr * qi)
            n *= 2
        gr, gi = _tile_scan_cplx(sr + jnp.where(pos == 0, qr * c_r - qi * c_i, 0.0),
                                 si + jnp.where(pos == 0, qr * c_i + qi * c_r, 0.0), qr, qi)
        sr = jnp.where(pos == 0, c_r, pltpu.roll(gr, 1, 0))
        si = jnp.where(pos == 0, c_i, pltpu.roll(gi, 1, 0))
        for i in range(ntiles):
            rs = slice(i * SUBLANES, (i + 1) * SUBLANES)
            sr, si = pr * sr - pi * si + re_ref[rs, sl], pr * si + pi * sr + im_ref[rs, sl]
            re_ref[rs, sl] = sr
            im_ref[rs, sl] = si
        cre_ref[:, sl] = gr[SUBLANES - 1:SUBLANES]
        cim_ref[:, sl] = gi[SUBLANES - 1:SUBLANES]


def _prompt_mixer_kernel(*refs, rows, nchunk, n_alias):
    (xnext_ref, xfirst_ref, vec_ref, win_ref,
     wqk_ref, wvo_ref, wif_ref, rwa_ref, rwx_ref, wbu_ref, wcre_ref, wcim_ref, wglu_ref) = refs[:13]
    (mixed_ref, c_ref, n_ref, m_ref, mtail_ref, h_ref, rtail_ref, sre_ref, sim_ref,
     q_scr, k_scr, v_scr, o_scr, ra_scr, rb_scr, ure_scr, uim_scr, proj_scr, xn_scr) = refs[13 + n_alias:]
    T = rows
    step = pl.program_id(0)
    slot = lax.rem(step, 2)
    proj_ref = proj_scr.at[slot]
    next_proj_ref = proj_scr.at[1 - slot]
    slab = (2 * MIX_WIDTH) // M_HEADS

    @pl.when(step == 0)
    def _():
        xn = _rmsnorm_bf16(xfirst_ref[...], vec_ref)
        for j in range(M_HEADS):
            cols = slice(j * slab, (j + 1) * slab)
            proj_scr[0, :, cols] = jnp.dot(xn, win_ref[:, cols], preferred_element_type=F32)

    @pl.when(lax.rem(step, nchunk) == 0)
    def _():
        for ref in (c_ref, n_ref, m_ref, mtail_ref, h_ref, rtail_ref, sre_ref, sim_ref):
            ref[...] = jnp.zeros(ref.shape, F32)

    xn_scr[...] = _rmsnorm_bf16(xnext_ref[...], vec_ref)

    xm = proj_ref[:, 0:M_WIDTH]
    xc = _silu(_conv_interleaved(xm, mtail_ref, _vec(vec_ref, "mcw"), _vec(vec_ref, "mcb"), T))
    gates = _mlstm_project(xc, xm, vec_ref, wqk_ref, wvo_ref, wif_ref, q_scr, k_scr, v_scr, o_scr)
    log_i = gates[:, :LANES]
    b_all = _cumsum_interleaved(_log_sigmoid(gates[:, LANES:]), T)
    a_max = _cummax_interleaved(log_i - b_all, T)
    m_prev = m_ref[...]
    log_inter = b_all + m_prev
    m_t_all = jnp.maximum(log_inter, b_all + a_max)
    w_inter_all = jnp.exp(log_inter - m_t_all)
    floor_all = jnp.exp(-m_t_all)
    b_end = b_all[T - 1:T, :]
    m_new = jnp.maximum(b_end + m_prev, b_end + a_max[T - 1:T, :])
    w_src_all = jnp.exp(b_end - b_all + log_i - m_new)
    decay_all = jnp.exp(b_end + m_prev - m_new)
    m_ref[...] = m_new
    b_t = b_all.T
    li_t = log_i.T

    rowi = lax.broadcasted_iota(jnp.int32, (T, T), 0)
    coli = lax.broadcasted_iota(jnp.int32, (T, T), 1)
    sub_len = T // SUBLANES

    def time_of(r):
        return (r & (SUBLANES - 1)) * sub_len + (r >> SUBLANE_SHIFT)

    causal = time_of(coli) <= time_of(rowi)
    k_scale = M_HEAD_DIM ** -0.5
    mnw = _vec(vec_ref, "mnw")
    mskip = _vec(vec_ref, "mskip")

    for h in range(M_HEADS):
        sl = slice(h * M_HEAD_DIM, (h + 1) * M_HEAD_DIM)
        c_prev = c_ref[h]
        n_prev = n_ref[h:h + 1, :]
        w_inter = w_inter_all[:, h:h + 1]
        w_src = w_src_all[:, h:h + 1]
        decay = decay_all[:, h:h + 1]

        q = q_scr[:, sl]
        ks = k_scr[:, sl] * k_scale
        v = v_scr[:, sl]
        q_b = q.astype(BF16)
        ks_b = ks.astype(BF16)

        log_d = jnp.where(causal, b_all[:, h:h + 1] - b_t[h:h + 1, :] + li_t[h:h + 1, :], -jnp.inf)
        w_intra = jnp.exp(log_d - m_t_all[:, h:h + 1])
        s = lax.dot_general(q_b, ks_b, _NT, preferred_element_type=F32) * w_intra
        inter = lax.dot_general(q_b, c_prev.astype(BF16), _NT, preferred_element_type=F32)
        num = jnp.dot(s.astype(BF16), v.astype(BF16), preferred_element_type=F32) + w_inter * inter
        den = (jnp.sum(s, axis=1, keepdims=True)
               + w_inter * jnp.sum(q * n_prev, axis=1, keepdims=True))
        hh = num / jnp.maximum(jnp.abs(den), floor_all[:, h:h + 1])

        vw_t = (v * w_src).T.astype(BF16)
        c_ref[h] = decay * c_prev + jnp.dot(vw_t, ks_b, preferred_element_type=F32)
        n_ref[h:h + 1, :] = decay * n_prev + jnp.sum(ks * w_src, axis=0, keepdims=True)

        z = proj_ref[:, MIX_WIDTH + h * M_HEAD_DIM:MIX_WIDTH + (h + 1) * M_HEAD_DIM]
        mixed_ref[:, sl] = _head_output(hh, o_scr[:, sl], xc[:, sl], z, mnw[:, sl], mskip[:, sl]).astype(BF16)

        cols = slice(h * slab, (h + 1) * slab)
        next_proj_ref[:, cols] = jnp.dot(xn_scr[...], win_ref[:, cols], preferred_element_type=F32)

    xr = proj_ref[:, M_WIDTH:M_WIDTH + R_WIDTH]
    xcr = _conv_interleaved(xr, rtail_ref, _vec(vec_ref, "rcw"), _vec(vec_ref, "rcb"), T)
    a, bb = _rglru_coeffs(xcr, vec_ref, rwa_ref, rwx_ref)
    ra_scr[...] = a
    rb_scr[...] = bb
    h_ref[...] = _scan_real_interleaved(ra_scr, rb_scr, h_ref[...], T)
    zr = proj_ref[:, MIX_WIDTH + M_WIDTH:MIX_WIDTH + M_WIDTH + R_WIDTH]
    mixed_ref[:, M_WIDTH:M_WIDTH + R_WIDTH] = (rb_scr[...] * _silu(zr)).astype(BF16)

    u = proj_ref[:, M_WIDTH + R_WIDTH:MIX_WIDTH]
    _s5_project(u.astype(BF16), wbu_ref, ure_scr, uim_scr)
    _scan_cplx_interleaved(ure_scr, uim_scr, _vec(vec_ref, "abre"), _vec(vec_ref, "abim"), sre_ref, sim_ref, T)
    glu = _s5_output(ure_scr, uim_scr, u, vec_ref, wcre_ref, wcim_ref, wglu_ref)
    zs = proj_ref[:, MIX_WIDTH + M_WIDTH + R_WIDTH:2 * MIX_WIDTH]
    mixed_ref[:, M_WIDTH + R_WIDTH:MIX_WIDTH] = (glu * _silu(zs)).astype(BF16)


_PROMPT_STATE_SHAPES = ((M_HEADS, M_HEAD_DIM, M_HEAD_DIM), (M_HEADS, M_HEAD_DIM), (1, LANES),
                        ((CONV_W - 1) * SUBLANES, M_WIDTH), (1, R_WIDTH), ((CONV_W - 1) * SUBLANES, R_WIDTH),
                        (1, S_LANES), (1, S_LANES))


def _prompt_mixer(x, total_rows, l, bsz, seq, vec, w_in_b, mats, prev_states):
    rows = PROMPT_CHUNK
    nchunk = seq // rows
    nstep = bsz * nchunk
    aliases = {}
    alias_args = []
    n_in = 4 + len(mats)
    if prev_states is not None:
        alias_args = list(prev_states)
        aliases = {n_in + k: 1 + k for k in range(len(alias_args))}

    def state_spec(shape):
        nd = len(shape)
        return pl.BlockSpec((None, None) + shape, lambda t: (l, t // nchunk) + (0,) * nd)

    out_shape = ([jax.ShapeDtypeStruct((total_rows, MIX_WIDTH), BF16)]
                 + [jax.ShapeDtypeStruct((DEPTH, bsz) + s, F32) for s in _PROMPT_STATE_SHAPES])
    scratch = ([pltpu.VMEM((rows, w), F32)
                for w in (M_WIDTH, M_WIDTH, M_WIDTH, M_WIDTH, R_WIDTH, R_WIDTH, S_LANES, S_LANES)]
               + [pltpu.VMEM((2, rows, 2 * MIX_WIDTH), F32), pltpu.VMEM((rows, D_MODEL), BF16)])
    return pl.pallas_call(
        functools.partial(_prompt_mixer_kernel, rows=rows, nchunk=nchunk, n_alias=len(alias_args)),
        grid=(nstep,),
        in_specs=([pl.BlockSpec((rows, D_MODEL), lambda t: (jnp.minimum(t + 1, nstep - 1), 0)),
                   pl.BlockSpec((rows, D_MODEL), lambda t: (0, 0), pipeline_mode=pl.Buffered(1)),
                   _layer_spec(vec, l),
                   pl.BlockSpec(w_in_b.shape, lambda t: (0, 0), pipeline_mode=pl.Buffered(1))]
                  + [_layer_spec(w, l) for w in mats] + [_ANY_SPEC] * len(alias_args)),
        out_specs=([pl.BlockSpec((rows, MIX_WIDTH), lambda t: (t, 0))]
                   + [state_spec(s) for s in _PROMPT_STATE_SHAPES]),
        out_shape=out_shape,
        scratch_shapes=scratch,
        input_output_aliases=aliases,
        compiler_params=pltpu.CompilerParams(
            dimension_semantics=("arbitrary",), vmem_limit_bytes=VMEM_LIMIT_BYTES),
        name="prompt_mixer",
    )(x, x, vec, w_in_b, *mats, *alias_args)


def _seg_last(x, groups):
    x3 = x.reshape(groups, SUBLANES, x.shape[-1])
    return jnp.broadcast_to(x3[:, SUBLANES - 1:SUBLANES, :], x3.shape).reshape(x.shape)


def _seg_max(x, groups):
    x3 = x.reshape(groups, SUBLANES, x.shape[-1])
    return jnp.broadcast_to(jnp.max(x3, axis=1, keepdims=True), x3.shape).reshape(x.shape)


def _seg_sum(x, groups):
    x3 = x.reshape(groups, SUBLANES, x.shape[-1])
    return jnp.broadcast_to(jnp.sum(x3, axis=1, keepdims=True), x3.shape).reshape(x.shape)


def _seg_rows(state_ref, lanes=slice(None)):
    x = state_ref[:, :, lanes]
    return jnp.broadcast_to(x, (x.shape[0], SUBLANES, x.shape[2])).reshape(x.shape[0] * SUBLANES, x.shape[2])


def _seg_state(x):
    x3 = x.reshape(x.shape[0] // SUBLANES, SUBLANES, x.shape[-1])
    return x3[:, SUBLANES - 1:SUBLANES, :]


def _with_history(x, buf_ref, new_buf_ref):
    g = x.shape[0] // SUBLANES
    x3 = x.reshape(g, SUBLANES, x.shape[1])
    lead = SAMPLE_LEAD - (CONV_W - 1)
    full = jnp.concatenate([jnp.zeros((g, lead, x.shape[1]), F32), buf_ref[...], x3[:, SAMPLE_LEAD:, :]], axis=1)
    new_buf_ref[...] = full[:, SUBLANES - (CONV_W - 1):, :]
    return full.reshape(x.shape)


def _conv_rolled(xf, w, bias):
    out = w[CONV_W - 1:CONV_W, :] * xf + bias
    for j in range(1, CONV_W):
        out = out + w[CONV_W - 1 - j:CONV_W - j, :] * pltpu.roll(xf, j, 0)
    return out


_SAMPLE_STATE_WIDTHS = (M_WIDTH, LANES, M_WIDTH, R_WIDTH, R_WIDTH, S_LANES, S_LANES)
_SAMPLE_STATE_ROWS = (1, 1, CONV_W - 1, 1, CONV_W - 1, 1, 1)


def _sample_mixer_kernel(*refs, rows, c_seqs, n_alias):
    (proj_ref, c0_ref, n0_ref, m0_ref, mtail0_ref, h0_ref, rtail0_ref, sre0_ref, sim0_ref,
     vec_ref, wqk_ref, wvo_ref, wif_ref, rwa_ref, rwx_ref, wbu_ref, wcre_ref, wcim_ref, wglu_ref) = refs[:19]
    (mixed_ref, c_ref, n_ref, m_ref, mconv_ref, h_ref, rconv_ref, sre_ref, sim_ref,
     q_scr, k_scr, v_scr, o_scr, xc_scr, intert_scr, vwt_scr, qt_scr, ksb_scr, dec_scr, b_scr, li_scr,
     ure_scr, uim_scr) = refs[19 + n_alias:]
    R = rows
    G = R // SUBLANES
    sub = pl.program_id(1)
    k_scale = M_HEAD_DIM ** -0.5
    pos = _sublane_pos((R, 1))
    valid = pos >= SAMPLE_LEAD
    first_token = pos == SAMPLE_LEAD

    @pl.when(sub == 0)
    def _():
        xm = _with_history(proj_ref[:, 0:M_WIDTH], mtail0_ref, mconv_ref)
        xc = _silu(_conv_rolled(xm, _vec(vec_ref, "mcw"), _vec(vec_ref, "mcb")))
        xc_scr[...] = xc
        gates = _mlstm_project(xc, xm, vec_ref, wqk_ref, wvo_ref, wif_ref, q_scr, k_scr, v_scr, o_scr)
        log_i = jnp.where(valid, gates[:, :LANES], -jnp.inf)
        log_f = jnp.where(valid, _log_sigmoid(gates[:, LANES:]), 0.0)
        b_all = _tile_cumsum(log_f)
        b_end = _seg_last(b_all, G)
        m_rows = _seg_rows(m0_ref)
        n_rows = _seg_rows(n0_ref)
        log_src = b_end - b_all + log_i
        m_new = jnp.maximum(b_end + m_rows, _seg_max(log_src, G))
        w_src_all = jnp.exp(log_src - m_new)
        decay_all = jnp.exp(b_end + m_rows - m_new)
        dec_scr[...] = decay_all
        b_scr[...] = b_all
        li_scr[...] = log_i
        m_ref[...] = _seg_state(m_new)
        for h in range(M_HEADS):
            sl = slice(h * M_HEAD_DIM, (h + 1) * M_HEAD_DIM)
            ks = k_scr[:, sl] * k_scale
            w_src = w_src_all[:, h:h + 1]
            n_ref[:, :, sl] = _seg_state(decay_all[:, h:h + 1] * n_rows[:, sl] + _seg_sum(ks * w_src, G))
            vwt_scr[h] = (v_scr[:, sl] * w_src).T
            qt_scr[h] = q_scr[:, sl].T.astype(BF16)
            ksb_scr[h] = ks.astype(BF16)
            intert_scr[h] = jnp.zeros((M_HEAD_DIM, R), F32)

        xr = _with_history(proj_ref[:, M_WIDTH:M_WIDTH + R_WIDTH], rtail0_ref, rconv_ref)
        xcr = _conv_rolled(xr, _vec(vec_ref, "rcw"), _vec(vec_ref, "rcb"))
        a, bb = _rglru_coeffs(xcr, vec_ref, rwa_ref, rwx_ref)
        hs = _tile_scan_real(a, jnp.where(valid, bb, 0.0) + jnp.where(first_token, a * _seg_rows(h0_ref), 0.0))
        h_ref[...] = _seg_state(hs)
        zr = proj_ref[:, MIX_WIDTH + M_WIDTH:MIX_WIDTH + M_WIDTH + R_WIDTH]
        mixed_ref[:, M_WIDTH:M_WIDTH + R_WIDTH] = jnp.where(valid, hs * _silu(zr), 0.0).astype(BF16)

        u = proj_ref[:, M_WIDTH + R_WIDTH:MIX_WIDTH]
        _s5_project(u.astype(BF16), wbu_ref, ure_scr, uim_scr)
        p_re = _vec(vec_ref, "abre")
        p_im = _vec(vec_ref, "abim")
        for blk in range(S_LANES // SCAN_LANE_BLOCK):
            sl = slice(blk * SCAN_LANE_BLOCK, (blk + 1) * SCAN_LANE_BLOCK)
            pr = jnp.broadcast_to(p_re[:, sl], (R, SCAN_LANE_BLOCK))
            pi = jnp.broadcast_to(p_im[:, sl], (R, SCAN_LANE_BLOCK))
            s0r = _seg_rows(sre0_ref, sl)
            s0i = _seg_rows(sim0_ref, sl)
            sr, si = _tile_scan_cplx(
                jnp.where(valid, ure_scr[:, sl], 0.0) + jnp.where(first_token, pr * s0r - pi * s0i, 0.0),
                jnp.where(valid, uim_scr[:, sl], 0.0) + jnp.where(first_token, pr * s0i + pi * s0r, 0.0), pr, pi)
            ure_scr[:, sl] = sr
            uim_scr[:, sl] = si
            sre_ref[:, :, sl] = _seg_state(sr)
            sim_ref[:, :, sl] = _seg_state(si)
        glu = _s5_output(ure_scr, uim_scr, u, vec_ref, wcre_ref, wcim_ref, wglu_ref)
        zs = proj_ref[:, MIX_WIDTH + M_WIDTH + R_WIDTH:2 * MIX_WIDTH]
        mixed_ref[:, M_WIDTH + R_WIDTH:MIX_WIDTH] = jnp.where(valid, glu * _silu(zs), 0.0).astype(BF16)

    lane_seq = lax.broadcasted_iota(jnp.int32, (M_HEAD_DIM, R), 1) >> SUBLANE_SHIFT
    seq0 = sub * c_seqs
    for h in range(M_HEADS):
        c_old = c0_ref[:, h].reshape(c_seqs * M_HEAD_DIM, M_HEAD_DIM)
        readout = jnp.dot(c_old.astype(BF16), qt_scr[h], preferred_element_type=F32)
        acc = intert_scr[h]
        vwt = vwt_scr[h]
        lhs = []
        for s in range(c_seqs):
            own = lane_seq == seq0 + s
            acc = jnp.where(own, readout[s * M_HEAD_DIM:(s + 1) * M_HEAD_DIM], acc)
            lhs.append(jnp.where(own, vwt, 0.0))
        intert_scr[h] = acc
        upd = jnp.dot(jnp.concatenate(lhs, axis=0).astype(BF16), ksb_scr[h], preferred_element_type=F32)
        for s in range(c_seqs):
            r0 = pl.multiple_of((seq0 + s) * SUBLANES, SUBLANES)
            c_ref[s, h] = (dec_scr[pl.ds(r0, 1), h:h + 1] * c0_ref[s, h]
                           + upd[s * M_HEAD_DIM:(s + 1) * M_HEAD_DIM])

    @pl.when(sub == pl.num_programs(1) - 1)
    def _():
        _sample_heads(proj_ref, m0_ref, n0_ref, vec_ref, mixed_ref,
                      q_scr, k_scr, v_scr, o_scr, xc_scr, intert_scr, b_scr, li_scr, valid, R)


def _sample_heads(proj_ref, m0_ref, n0_ref, vec_ref, mixed_ref,
                  q_scr, k_scr, v_scr, o_scr, xc_scr, intert_scr, b_scr, li_scr, valid, R):
    rowi = lax.broadcasted_iota(jnp.int32, (R, R), 0)
    coli = lax.broadcasted_iota(jnp.int32, (R, R), 1)
    same_causal = jnp.logical_and(coli <= rowi, (coli >> SUBLANE_SHIFT) == (rowi >> SUBLANE_SHIFT))
    k_scale = M_HEAD_DIM ** -0.5
    mnw = _vec(vec_ref, "mnw")
    mskip = _vec(vec_ref, "mskip")
    n_rows = _seg_rows(n0_ref)
    b_all = b_scr[...]
    log_i = li_scr[...]
    log_inter = b_all + _seg_rows(m0_ref)
    m_t_all = jnp.maximum(log_inter, b_all + _tile_cummax(log_i - b_all))
    w_inter_all = jnp.exp(log_inter - m_t_all)
    floor_all = jnp.exp(-m_t_all)
    b_t = b_all.T
    li_t = log_i.T

    for h in range(M_HEADS):
        sl = slice(h * M_HEAD_DIM, (h + 1) * M_HEAD_DIM)
        q = q_scr[:, sl]
        ks = k_scr[:, sl] * k_scale
        v = v_scr[:, sl]
        q_b = q.astype(BF16)
        ks_b = ks.astype(BF16)
        w_inter = w_inter_all[:, h:h + 1]

        log_d = jnp.where(same_causal, b_all[:, h:h + 1] - b_t[h:h + 1, :] + li_t[h:h + 1, :], -jnp.inf)
        w_intra = jnp.exp(log_d - m_t_all[:, h:h + 1])
        s = lax.dot_general(q_b, ks_b, _NT, preferred_element_type=F32) * w_intra
        num = (jnp.dot(s.astype(BF16), v.astype(BF16), preferred_element_type=F32)
               + w_inter * intert_scr[h].T)
        den = (jnp.sum(s, axis=1, keepdims=True)
               + w_inter * jnp.sum(q * n_rows[:, sl], axis=1, keepdims=True))
        hh = num / jnp.maximum(jnp.abs(den), floor_all[:, h:h + 1])

        z = proj_ref[:, MIX_WIDTH + h * M_HEAD_DIM:MIX_WIDTH + (h + 1) * M_HEAD_DIM]
        out = _head_output(hh, o_scr[:, sl], xc_scr[:, sl], z, mnw[:, sl], mskip[:, sl])
        mixed_ref[:, sl] = jnp.where(valid, out, 0.0).astype(BF16)


def _sample_mixer(proj, l, row0, mixed_prev, c_all, row_state, vec, mats, prev_out):
    R = SAMPLE_ROWS
    nrows = proj.shape[0]
    nsub = (R // SUBLANES) // SAMPLE_C_SEQS
    blk0 = row0 // R
    c_spec = pl.BlockSpec((None, SAMPLE_C_SEQS, M_HEADS, M_HEAD_DIM, M_HEAD_DIM),
                          lambda i, j: (l, i * nsub + j, 0, 0, 0))

    def state_spec(width, rows_per_seq):
        return pl.BlockSpec((None, R // SUBLANES, rows_per_seq, width), lambda i, j: (l, i, 0, 0))

    def state_shape(width, rows_per_seq):
        return (DEPTH, nrows // SUBLANES, rows_per_seq, width)

    state_specs = [state_spec(w, r) for w, r in zip(_SAMPLE_STATE_WIDTHS, _SAMPLE_STATE_ROWS)]
    alias_args = [mixed_prev] + (list(prev_out) if prev_out is not None else [])
    first_alias = 2 + len(row_state) + 1 + len(mats)
    aliases = {first_alias + k: k for k in range(len(alias_args))}
    out_shape = ([jax.ShapeDtypeStruct(mixed_prev.shape, BF16), jax.ShapeDtypeStruct(c_all.shape, F32)]
                 + [jax.ShapeDtypeStruct(state_shape(w, r), F32)
                    for w, r in zip(_SAMPLE_STATE_WIDTHS, _SAMPLE_STATE_ROWS)])
    scratch = ([pltpu.VMEM((R, M_WIDTH), F32)] * 5
               + [pltpu.VMEM((M_HEADS, M_HEAD_DIM, R), F32), pltpu.VMEM((M_HEADS, M_HEAD_DIM, R), F32),
                  pltpu.VMEM((M_HEADS, M_HEAD_DIM, R), BF16), pltpu.VMEM((M_HEADS, R, M_HEAD_DIM), BF16)]
               + [pltpu.VMEM((R, LANES), F32)] * 3
               + [pltpu.VMEM((R, S_LANES), F32)] * 2)
    return pl.pallas_call(
        functools.partial(_sample_mixer_kernel, rows=R, c_seqs=SAMPLE_C_SEQS, n_alias=len(alias_args)),
        grid=(nrows // R, nsub),
        in_specs=([pl.BlockSpec((R, 2 * MIX_WIDTH), lambda i, j: (i, 0)), c_spec] + state_specs
                  + [_layer_spec(vec, l)] + [_layer_spec(w, l) for w in mats] + [_ANY_SPEC] * len(alias_args)),
        out_specs=[pl.BlockSpec((R, MIX_WIDTH), lambda i, j: (blk0 + i, 0)), c_spec] + state_specs,
        out_shape=out_shape,
        scratch_shapes=scratch,
        input_output_aliases=aliases,
        compiler_params=pltpu.CompilerParams(
            dimension_semantics=("parallel", "arbitrary"), vmem_limit_bytes=VMEM_LIMIT_BYTES),
        name="sample_mixer",
    )(proj, c_all, *row_state, vec, *mats, *alias_args)


def _prepare_weights(p, s5):
    abre, abim, wbu, wcre, wcim = s5

    def split_gates(g):
        pad = [(0, 0)] * (g.ndim - 1) + [(0, LANES - M_HEADS)]
        return jnp.concatenate([jnp.pad(g[..., :M_HEADS], pad), jnp.pad(g[..., M_HEADS:], pad)], axis=-1)

    table = {"mcw": p["m_conv_w"], "rcw": p["r_conv_w"], "mcb": p["m_conv_b"], "mnw": p["m_norm_w"],
             "mskip": p["m_skip"], "rcb": p["r_conv_b"], "rba": p["r_ba"], "rbx": p["r_bx"], "rlam": p["r_lam"],
             "sd": p["s_d"], "bglu": p["s_b_glu"], "bif": split_gates(p["m_b_if"]), "abre": abre, "abim": abim,
             "norm_w": p["norm_w"]}

    def group_rows(group):
        parts = [table[name].reshape(DEPTH, n, w) for name, n, w in group]
        fill = _VEC_WIDTH - sum(w for _, _, w in group)
        if fill:
            parts.append(jnp.zeros((DEPTH, group[0][1], fill), F32))
        return parts[0] if len(parts) == 1 else jnp.concatenate(parts, axis=-1)

    vec = jnp.concatenate([group_rows(g) for g in _VEC_GROUPS]
                          + [jnp.zeros((DEPTH, _VEC_ROWS - _VEC_USED_ROWS, _VEC_WIDTH), F32)], axis=1)

    def block_diag_halves(w):
        nb = R_BLOCKS // 2
        w5 = w.reshape(DEPTH, 2, nb, w.shape[-2], w.shape[-1])
        eye = jnp.eye(nb, dtype=F32)[:, None, :, None]
        return (w5[:, :, :, :, None, :] * eye).reshape(
            DEPTH, 2, nb * w.shape[-2], nb * w.shape[-1]).astype(BF16)

    mats = {
        "wqk": jnp.concatenate([p["m_wq"], p["m_wk"]], axis=-1).astype(BF16),
        "wvo": jnp.concatenate([p["m_wv"], p["m_wo"]], axis=-1).astype(BF16),
        "wif": split_gates(p["m_w_if"]).astype(BF16),
        "rwa": block_diag_halves(p["r_wa"]), "rwx": block_diag_halves(p["r_wx"]),
        "wbu": wbu, "wcre": wcre, "wcim": wcim,
        "wglu": p["s_w_glu"].astype(BF16),
    }
    return vec, [mats[k] for k in _MATRIX_KEYS]


def _sample_rows_state(n, m, mconv, h, rconv, sre, sim):
    bsz = n.shape[1]

    def per_seq(x):
        return x.reshape(DEPTH, bsz, 1, x.shape[-1])

    m_pad = jnp.pad(m, ((0, 0), (0, 0), (0, LANES - M_HEADS)))
    return (per_seq(n.reshape(DEPTH, bsz, M_WIDTH)), per_seq(m_pad), mconv, per_seq(h), rconv,
            per_seq(sre.reshape(DEPTH, bsz, S_LANES)), per_seq(sim.reshape(DEPTH, bsz, S_LANES)))


def _sample_state_from_rows(n, m, mconv, h, rconv, sre, sim):
    bsz = n.shape[1]
    return (n.reshape(DEPTH, bsz, M_HEADS, M_HEAD_DIM), m.reshape(DEPTH, bsz, LANES)[..., :M_HEADS],
            mconv, h.reshape(DEPTH, bsz, R_WIDTH), rconv,
            sre.reshape(DEPTH, bsz, S_GROUPS, S_STATE), sim.reshape(DEPTH, bsz, S_GROUPS, S_STATE))


def _prompt_state_from_kernel(c, n, m, mtail, h, rtail, sre, sim):
    bsz = c.shape[1]
    return (c, n, m[:, :, 0, :M_HEADS], mtail[:, :, SUBLANES - 1::SUBLANES], h.reshape(DEPTH, bsz, R_WIDTH),
            rtail[:, :, SUBLANES - 1::SUBLANES],
            sre.reshape(DEPTH, bsz, S_GROUPS, S_STATE), sim.reshape(DEPTH, bsz, S_GROUPS, S_STATE))


def _interleave_chunks(x, inverse=False):
    bsz, seq, d = x.shape
    sub_len = PROMPT_CHUNK // SUBLANES
    inner = (sub_len, SUBLANES) if inverse else (SUBLANES, sub_len)
    return x.reshape(bsz, seq // PROMPT_CHUNK, *inner, d).transpose(0, 1, 3, 2, 4).reshape(bsz, seq, d)


def kernel(x_prompt, x_sample, state_mlstm_C, state_mlstm_n, state_mlstm_m, state_mlstm_conv, state_rglru_h, state_rglru_conv, state_s5_re, state_s5_im, norm_w, w_in, w_out, m_conv_w, m_conv_b, m_wq, m_wk, m_wv, m_wo, m_w_if, m_b_if, m_norm_w, m_skip, r_conv_w, r_conv_b, r_wa, r_ba, r_wx, r_bx, r_lam, s_lam_re, s_lam_im, s_b_re, s_b_im, s_c_re, s_c_im, s_d, s_log_step, s_w_glu, s_b_glu, final_norm_w):
    p = dict(norm_w=norm_w, w_in=w_in, w_out=w_out, m_conv_w=m_conv_w, m_conv_b=m_conv_b, m_wq=m_wq, m_wk=m_wk,
             m_wv=m_wv, m_wo=m_wo, m_w_if=m_w_if, m_b_if=m_b_if, m_norm_w=m_norm_w, m_skip=m_skip,
             r_conv_w=r_conv_w, r_conv_b=r_conv_b, r_wa=r_wa, r_ba=r_ba, r_wx=r_wx, r_bx=r_bx, r_lam=r_lam,
             s_d=s_d, s_w_glu=s_w_glu, s_b_glu=s_b_glu)
    s5 = _s5_prep(s_lam_re, s_lam_im, s_log_step, s_b_re, s_b_im, s_c_re, s_c_im)
    vec, mats = _prepare_weights(p, s5)
    final_w = final_norm_w.reshape(1, D_MODEL)

    bsz, seq, _ = x_prompt.shape
    dec_batch, dec_seq, _ = x_sample.shape
    n_prompt = bsz * seq
    n_sample = dec_batch * SUBLANES
    stream = [_interleave_chunks(x_prompt).reshape(n_prompt, D_MODEL),
              jnp.pad(x_sample, ((0, 0), (SAMPLE_LEAD, 0), (0, 0))).reshape(n_sample, D_MODEL)]
    rows_state = _sample_rows_state(state_mlstm_n, state_mlstm_m, state_mlstm_conv, state_rglru_h,
                                    state_rglru_conv, state_s5_re, state_s5_im)
    pr_states = None
    sa_states = None
    for l in range(DEPTH):
        last = l == DEPTH - 1
        proj_s, w_in_b = _inproj(stream[-1], stream[-1].shape[0] - n_sample, n_sample, vec, w_in, l)
        mixed, *pr_states = _prompt_mixer(stream[0], n_prompt + n_sample, l, bsz, seq, vec, w_in_b, mats, pr_states)
        mixed, *sa_states = _sample_mixer(proj_s, l, n_prompt, mixed, state_mlstm_C, rows_state, vec, mats, sa_states)
        stream = _outproj(mixed, stream, w_out, l, final_w, last, split_rows=n_prompt if last else None)
    y_prompt = _interleave_chunks(stream[0].reshape(bsz, seq, D_MODEL), inverse=True)
    y_sample = stream[1]
    return (y_prompt, y_sample, *_prompt_state_from_kernel(*pr_states),
            sa_states[0], *_sample_state_from_rows(*sa_states[1:]))
```

```python
import functools

import jax
import jax.numpy as jnp
from jax import lax
from jax.experimental import pallas as pl
from jax.experimental.pallas import tpu as pltpu

F32 = jnp.float32
BF16 = jnp.bfloat16

D_MODEL = 2048
DEPTH = 2
MIX_WIDTH = D_MODEL
M_WIDTH = MIX_WIDTH // 2
R_WIDTH = MIX_WIDTH // 4
S_WIDTH = MIX_WIDTH - M_WIDTH - R_WIDTH
M_HEADS = 8
M_HEAD_DIM = M_WIDTH // M_HEADS
R_BLOCKS = 8
RG_C = 8.0
S_GROUP = 16
S_GROUPS = S_WIDTH // S_GROUP
S_STATE = 64
S_LANES = S_GROUPS * S_STATE
CONV_W = 4
EPS = 1e-6

SUBLANES = 8
SUBLANE_SHIFT = SUBLANES.bit_length() - 1
LANES = 128
VMEM_LIMIT_BYTES = 56 * 1024 * 1024

PROMPT_CHUNK = 256
SAMPLE_ROWS = 128
SAMPLE_C_SEQS = 8
SAMPLE_LEAD = SUBLANES - 4
INPROJ_TM = 512
INPROJ_TN = 1024
OUTPROJ_TM = 512
SCAN_LANE_BLOCK = 256

_NT = (((1,), (1,)), ((), ()))

_VEC_WIDTH = D_MODEL
_VEC_GROUPS = ((("mcw", CONV_W, M_WIDTH), ("rcw", CONV_W, R_WIDTH)),
               (("mcb", 1, M_WIDTH), ("mnw", 1, M_WIDTH)),
               (("mskip", 1, M_WIDTH), ("rcb", 1, R_WIDTH), ("rba", 1, R_WIDTH)),
               (("rbx", 1, R_WIDTH), ("rlam", 1, R_WIDTH), ("sd", 1, S_WIDTH), ("bglu", 1, S_WIDTH)),
               (("bif", 1, 2 * LANES),),
               (("abre", 1, S_LANES),), (("abim", 1, S_LANES),), (("norm_w", 1, D_MODEL),))
_VEC_LAYOUT = {}
_row = 0
for _group in _VEC_GROUPS:
    _lane = 0
    for _name, _n, _w in _group:
        _VEC_LAYOUT[_name] = (_row, _n, _lane, _w)
        _lane += _w
    _row += _group[0][1]
_VEC_USED_ROWS = _row
_VEC_ROWS = -(-_row // SUBLANES) * SUBLANES


def _vec(vec_ref, name):
    r0, n, l0, w = _VEC_LAYOUT[name]
    return vec_ref[r0:r0 + n, l0:l0 + w]


def _layer_spec(arr, l):
    nd = arr.ndim - 1
    return pl.BlockSpec((None,) + arr.shape[1:], lambda *_: (l,) + (0,) * nd, pipeline_mode=pl.Buffered(1))


_ANY_SPEC = pl.BlockSpec(memory_space=pl.ANY)


S5_IN_SLICES = S_WIDTH // LANES
S5_IN_GROUPS = LANES // S_GROUP
S5_OUT_HALVES = 2
S5_OUT_GROUPS = S_GROUPS // S5_OUT_HALVES


def _pack_mixer_weights(wq_ref, wk_ref, wv_ref, wo_ref, wif_in_ref, ra_ref, rx_ref, glu_ref,
                        wqk_ref, wvo_ref, wif_ref, rwa_ref, rwx_ref, wglu_ref):
    for h in range(M_HEADS):
        wqk_ref[0, h, :, 0:M_HEAD_DIM] = wq_ref[0, h].astype(BF16)
        wqk_ref[0, h, :, M_HEAD_DIM:] = wk_ref[0, h].astype(BF16)
        wvo_ref[0, h, :, 0:M_HEAD_DIM] = wv_ref[0, h].astype(BF16)
        wvo_ref[0, h, :, M_HEAD_DIM:] = wo_ref[0, h].astype(BF16)
    w_if = wif_in_ref[0]
    wif_ref[0] = jnp.zeros(wif_ref.shape[1:], BF16)
    wif_ref[0, :, 0:M_HEADS] = w_if[:, 0:M_HEADS].astype(BF16)
    wif_ref[0, :, LANES:LANES + M_HEADS] = w_if[:, M_HEADS:2 * M_HEADS].astype(BF16)
    per_half = R_BLOCKS // 2
    blk = R_WIDTH // R_BLOCKS
    for src_ref, dst_ref in ((ra_ref, rwa_ref), (rx_ref, rwx_ref)):
        dst_ref[0] = jnp.zeros(dst_ref.shape[1:], BF16)
        for n in range(R_BLOCKS):
            half, a = divmod(n, per_half)
            dst_ref[0, half, a * blk:(a + 1) * blk, a * blk:(a + 1) * blk] = src_ref[0, n].astype(BF16)
    wglu_ref[0] = glu_ref[0].astype(BF16)


def _prep_kernel(lr_ref, li_ref, ls_ref, brt_ref, bit_ref, cre_ref, cim_ref,
                 wq_ref, wk_ref, wv_ref, wo_ref, wif_in_ref, ra_ref, rx_ref, glu_ref,
                 abre_ref, abim_ref, wbu_ref, wcre_ref, wcim_ref,
                 wqk_ref, wvo_ref, wif_ref, rwa_ref, rwx_ref, wglu_ref):
    _pack_mixer_weights(wq_ref, wk_ref, wv_ref, wo_ref, wif_in_ref, ra_ref, rx_ref, glu_ref,
                        wqk_ref, wvo_ref, wif_ref, rwa_ref, rwx_ref, wglu_ref)
    lr = lr_ref[0]
    li = li_ref[0]
    dt = jnp.exp(ls_ref[0])
    mag = jnp.exp(lr * dt)
    ang = li * dt
    ab_re = mag * jnp.cos(ang)
    ab_im = mag * jnp.sin(ang)
    den = lr * lr + li * li
    nr = ab_re - 1.0
    f_re = (nr * lr + ab_im * li) / den
    f_im = (ab_im * lr - nr * li) / den
    br = brt_ref[0]
    bi = bit_ref[0]
    abre_ref[0] = ab_re
    abim_ref[0] = ab_im
    bb_re = f_re * br - f_im * bi
    bb_im = f_re * bi + f_im * br

    per_slice = S5_IN_GROUPS * S_STATE
    lane_group = lax.broadcasted_iota(jnp.int32, (S_GROUP, per_slice), 1) >> (S_STATE.bit_length() - 1)
    for k in range(S5_IN_SLICES):
        re_k = bb_re[:, k * per_slice:(k + 1) * per_slice]
        im_k = bb_im[:, k * per_slice:(k + 1) * per_slice]
        for a in range(S5_IN_GROUPS):
            rows = slice(a * S_GROUP, (a + 1) * S_GROUP)
            wbu_ref[0, k, rows, 0:per_slice] = jnp.where(lane_group == a, re_k, 0.0).astype(BF16)
            wbu_ref[0, k, rows, per_slice:2 * per_slice] = jnp.where(lane_group == a, im_k, 0.0).astype(BF16)

    row_group = lax.broadcasted_iota(jnp.int32, (S5_OUT_GROUPS * S_GROUP, S_STATE), 0) >> (S_GROUP.bit_length() - 1)
    for src_ref, dst_ref in ((cre_ref, wcre_ref), (cim_ref, wcim_ref)):
        for m in range(S5_OUT_HALVES):
            c = src_ref[0, m]
            for a in range(S5_OUT_GROUPS):
                dst_ref[0, m, :, a * S_STATE:(a + 1) * S_STATE] = jnp.where(row_group == a, c, 0.0).astype(BF16)


def _prep(p):
    s_lam_re, s_lam_im, s_log_step = p["s_lam_re"], p["s_lam_im"], p["s_log_step"]
    s_b_re, s_b_im, s_c_re, s_c_im = p["s_b_re"], p["s_b_im"], p["s_c_re"], p["s_c_im"]
    lr = s_lam_re.reshape(DEPTH, 1, S_LANES)
    li = s_lam_im.reshape(DEPTH, 1, S_LANES)
    ls = jnp.repeat(s_log_step, S_STATE, axis=-1).reshape(DEPTH, 1, S_LANES)
    brt = s_b_re.reshape(DEPTH, S_LANES, S_GROUP).transpose(0, 2, 1)
    bit = s_b_im.reshape(DEPTH, S_LANES, S_GROUP).transpose(0, 2, 1)
    c_shape = (DEPTH, S5_OUT_HALVES, S5_OUT_GROUPS * S_GROUP, S_STATE)
    wbu_shape = (DEPTH, S5_IN_SLICES, LANES, 2 * S5_IN_GROUPS * S_STATE)
    wc_shape = (DEPTH, S5_OUT_HALVES, S5_OUT_GROUPS * S_GROUP, S5_OUT_GROUPS * S_STATE)

    def layer_block(shape):
        nd = len(shape) - 1
        return pl.BlockSpec((1,) + shape[1:], lambda l: (l,) + (0,) * nd)

    vec = layer_block((DEPTH, 1, S_LANES))
    mat = layer_block((DEPTH, S_GROUP, S_LANES))
    raw = [p["m_wq"], p["m_wk"], p["m_wv"], p["m_wo"], p["m_w_if"], p["r_wa"], p["r_wx"], p["s_w_glu"]]
    half = R_WIDTH // 2
    packed_shapes = [(DEPTH, M_HEADS, M_HEAD_DIM, 2 * M_HEAD_DIM)] * 2 + [
        (DEPTH, 3 * M_WIDTH, 2 * LANES), (DEPTH, 2, half, half), (DEPTH, 2, half, half), (DEPTH, S_WIDTH, S_WIDTH)]
    out_shapes = [(DEPTH, 1, S_LANES)] * 2 + [wbu_shape, wc_shape, wc_shape] + packed_shapes
    out_dtypes = [F32] * 2 + [BF16] * (len(out_shapes) - 2)
    abre, abim, wbu, wcre, wcim, wqk, wvo, wif, rwa, rwx, wglu = pl.pallas_call(
        _prep_kernel,
        grid=(DEPTH,),
        in_specs=([vec, vec, vec, mat, mat, layer_block(c_shape), layer_block(c_shape)]
                  + [layer_block(w.shape) for w in raw]),
        out_specs=[layer_block(s) for s in out_shapes],
        out_shape=[jax.ShapeDtypeStruct(s, d) for s, d in zip(out_shapes, out_dtypes)],
        compiler_params=pltpu.CompilerParams(vmem_limit_bytes=VMEM_LIMIT_BYTES),
        name="param_prep",
    )(lr, li, ls, brt, bit, s_c_re.reshape(c_shape), s_c_im.reshape(c_shape), *raw)
    mats = dict(wqk=wqk, wvo=wvo, wif=wif, rwa=rwa, rwx=rwx, wbu=wbu, wcre=wcre, wcim=wcim, wglu=wglu)
    return abre, abim, [mats[k] for k in _MATRIX_KEYS]


def _stream_specs(parts, tm):
    if len(parts) == 1:
        return [pl.BlockSpec((tm, D_MODEL), lambda i: (i, 0))]
    nb0 = parts[0].shape[0] // tm
    return [pl.BlockSpec((tm, D_MODEL), lambda i: (jnp.minimum(i, nb0 - 1), 0)),
            pl.BlockSpec((tm, D_MODEL), lambda i: (jnp.maximum(i - nb0, 0), 0))]


def _read_stream(refs, nb0):
    if len(refs) == 1:
        return refs[0][...]
    return jnp.where(pl.program_id(0) < nb0, refs[0][...], refs[1][...])


def _rmsnorm_bf16(x, vec_ref):
    ms = jnp.mean(x * x, axis=-1, keepdims=True)
    return (x * lax.rsqrt(ms + EPS) * _vec(vec_ref, "norm_w")).astype(BF16)


def _inproj_kernel(x_ref, vec_ref, w_ref, o_ref, wb_ref, xn_scr):
    @pl.when(pl.program_id(0) == 0)
    def _():
        xn_scr[...] = _rmsnorm_bf16(x_ref[...], vec_ref)

    wb = w_ref[...].astype(BF16)
    wb_ref[...] = wb
    o_ref[...] = jnp.dot(xn_scr[...], wb, preferred_element_type=F32)


def _inproj(x, row0, nrows, vec, w_in, l):
    blk0 = row0 // nrows
    return pl.pallas_call(
        _inproj_kernel,
        grid=((2 * MIX_WIDTH) // INPROJ_TN,),
        in_specs=[pl.BlockSpec((nrows, D_MODEL), lambda j: (blk0, 0), pipeline_mode=pl.Buffered(1)),
                  _layer_spec(vec, l),
                  pl.BlockSpec((None, D_MODEL, INPROJ_TN), lambda j: (l, 0, j))],
        out_specs=[pl.BlockSpec((nrows, INPROJ_TN), lambda j: (0, j)),
                   pl.BlockSpec((D_MODEL, INPROJ_TN), lambda j: (0, j))],
        out_shape=[jax.ShapeDtypeStruct((nrows, 2 * MIX_WIDTH), F32),
                   jax.ShapeDtypeStruct((D_MODEL, 2 * MIX_WIDTH), BF16)],
        scratch_shapes=[pltpu.VMEM((nrows, D_MODEL), BF16)],
        compiler_params=pltpu.CompilerParams(
            dimension_semantics=("arbitrary",), vmem_limit_bytes=VMEM_LIMIT_BYTES),
        name="inproj",
    )(x, vec, w_in)


def _outproj_kernel(*refs, n_x, nb0, n_out, nb0_out, final):
    mixed_ref, x_refs = refs[0], refs[1:1 + n_x]
    w_ref, fw_ref = refs[1 + n_x:3 + n_x]
    o_refs, wb_scr = refs[3 + n_x:-1], refs[-1]

    @pl.when(pl.program_id(0) == 0)
    def _():
        wb_scr[...] = w_ref[...].astype(BF16)

    y = _read_stream(x_refs, nb0) + jnp.dot(mixed_ref[...], wb_scr[...], preferred_element_type=F32)
    if final:
        ms = jnp.mean(y * y, axis=-1, keepdims=True)
        y = y * lax.rsqrt(ms + EPS) * fw_ref[...]
    if n_out == 1:
        o_refs[0][...] = y
    else:
        @pl.when(pl.program_id(0) < nb0_out)
        def _():
            o_refs[0][...] = y

        @pl.when(pl.program_id(0) >= nb0_out)
        def _():
            o_refs[1][...] = y.reshape(y.shape[0] // SUBLANES, SUBLANES, y.shape[1])[:, SAMPLE_LEAD:, :]


def _outproj(mixed, x_parts, w_out, l, final_w, final, split_rows=None):
    n = mixed.shape[0]
    if split_rows is None:
        out_parts = [jax.ShapeDtypeStruct((n, D_MODEL), F32)]
        out_specs = [pl.BlockSpec((OUTPROJ_TM, D_MODEL), lambda i: (i, 0))]
        nb0_out = n // OUTPROJ_TM
    else:
        nb0_out = split_rows // OUTPROJ_TM
        tokens = SUBLANES - SAMPLE_LEAD
        out_parts = [jax.ShapeDtypeStruct((split_rows, D_MODEL), F32),
                     jax.ShapeDtypeStruct(((n - split_rows) // SUBLANES, tokens, D_MODEL), F32)]
        out_specs = [pl.BlockSpec((OUTPROJ_TM, D_MODEL), lambda i: (jnp.minimum(i, nb0_out - 1), 0)),
                     pl.BlockSpec((OUTPROJ_TM // SUBLANES, tokens, D_MODEL),
                                  lambda i: (jnp.maximum(i - nb0_out, 0), 0, 0))]
    return pl.pallas_call(
        functools.partial(_outproj_kernel, n_x=len(x_parts), nb0=x_parts[0].shape[0] // OUTPROJ_TM,
                          n_out=len(out_parts), nb0_out=nb0_out, final=final),
        grid=(n // OUTPROJ_TM,),
        in_specs=([pl.BlockSpec((OUTPROJ_TM, MIX_WIDTH), lambda i: (i, 0))] + _stream_specs(x_parts, OUTPROJ_TM)
                  + [_layer_spec(w_out, l), pl.BlockSpec((1, D_MODEL), lambda i: (0, 0))]),
        out_specs=out_specs,
        out_shape=out_parts,
        scratch_shapes=[pltpu.VMEM((MIX_WIDTH, D_MODEL), BF16)],
        compiler_params=pltpu.CompilerParams(
            dimension_semantics=("arbitrary",), vmem_limit_bytes=VMEM_LIMIT_BYTES),
        name="outproj",
    )(mixed, *x_parts, w_out, final_w)


def _silu(x):
    return x * jax.nn.sigmoid(x)


def _log_sigmoid(x):
    return jnp.minimum(x, 0.0) - jnp.log1p(jnp.exp(-jnp.abs(x)))


def _softplus(x):
    return jnp.maximum(x, 0.0) + jnp.log1p(jnp.exp(-jnp.abs(x)))


def _sublane_pos(shape):
    return lax.broadcasted_iota(jnp.int32, shape, 0) & (SUBLANES - 1)


def _tile_cumsum(x):
    pos = _sublane_pos(x.shape)
    s = 1
    while s < SUBLANES:
        x = x + jnp.where(pos >= s, pltpu.roll(x, s, 0), 0.0)
        s *= 2
    return x


def _tile_cummax(x):
    pos = _sublane_pos(x.shape)
    s = 1
    while s < SUBLANES:
        x = jnp.maximum(x, jnp.where(pos >= s, pltpu.roll(x, s, 0), -jnp.inf))
        s *= 2
    return x


def _tile_scan_real(a, b):
    pos = _sublane_pos(a.shape)
    s = 1
    while s < SUBLANES:
        m = pos >= s
        b = jnp.where(m, a * pltpu.roll(b, s, 0) + b, b)
        if 2 * s < SUBLANES:
            a = jnp.where(m, a * pltpu.roll(a, s, 0), a)
        s *= 2
    return b


def _tile_scan_cplx(sr, si, pr, pi):
    pos = _sublane_pos(sr.shape)
    s = 1
    while s < SUBLANES:
        m = pos >= s
        sr_sh = pltpu.roll(sr, s, 0)
        si_sh = pltpu.roll(si, s, 0)
        sr, si = (jnp.where(m, sr + (pr * sr_sh - pi * si_sh), sr),
                  jnp.where(m, si + (pr * si_sh + pi * sr_sh), si))
        if 2 * s < SUBLANES:
            pr, pi = pr * pr - pi * pi, 2.0 * (pr * pi)
        s *= 2
    return sr, si


def _mlstm_project(xc, xm, vec_ref, wqk_ref, wvo_ref, wif_ref, q_scr, k_scr, v_scr, o_scr):
    xc_b = xc.astype(BF16)
    xm_b = xm.astype(BF16)
    for h in range(M_HEADS):
        sl = slice(h * M_HEAD_DIM, (h + 1) * M_HEAD_DIM)
        qk = jnp.dot(xc_b[:, sl], wqk_ref[h], preferred_element_type=F32)
        vo = jnp.dot(xm_b[:, sl], wvo_ref[h], preferred_element_type=F32)
        q_scr[:, sl] = qk[:, :M_HEAD_DIM]
        k_scr[:, sl] = qk[:, M_HEAD_DIM:]
        v_scr[:, sl] = vo[:, :M_HEAD_DIM]
        o_scr[:, sl] = vo[:, M_HEAD_DIM:]
    return (jnp.dot(q_scr[...].astype(BF16), wif_ref[0:M_WIDTH, :], preferred_element_type=F32)
            + jnp.dot(k_scr[...].astype(BF16), wif_ref[M_WIDTH:2 * M_WIDTH, :], preferred_element_type=F32)
            + jnp.dot(v_scr[...].astype(BF16), wif_ref[2 * M_WIDTH:3 * M_WIDTH, :], preferred_element_type=F32)
            + _vec(vec_ref, "bif"))


def _head_output(hh, o_pre, xc_h, z_h, mnw_h, mskip_h):
    mu = jnp.mean(hh, axis=1, keepdims=True)
    hc = hh - mu
    var = jnp.mean(hc * hc, axis=1, keepdims=True)
    hn = hc * lax.rsqrt(var + EPS) * mnw_h
    return (jax.nn.sigmoid(o_pre) * hn + mskip_h * xc_h) * _silu(z_h)


def _rglru_coeffs(xcr, vec_ref, rwa_ref, rwx_ref):
    xcr_b = xcr.astype(BF16)
    half = R_WIDTH // 2
    ra_pre = jnp.concatenate(
        [jnp.dot(xcr_b[:, :half], rwa_ref[0], preferred_element_type=F32),
         jnp.dot(xcr_b[:, half:], rwa_ref[1], preferred_element_type=F32)], axis=1) + _vec(vec_ref, "rba")
    rx_pre = jnp.concatenate(
        [jnp.dot(xcr_b[:, :half], rwx_ref[0], preferred_element_type=F32),
         jnp.dot(xcr_b[:, half:], rwx_ref[1], preferred_element_type=F32)], axis=1) + _vec(vec_ref, "rbx")
    log_a = (-RG_C) * jax.nn.sigmoid(ra_pre) * _softplus(-_vec(vec_ref, "rlam"))
    a = jnp.exp(log_a)
    th = jnp.tanh(log_a)
    one_minus_a2 = (-2.0 * th) / (1.0 - th)
    return a, jnp.sqrt(one_minus_a2) * (jax.nn.sigmoid(rx_pre) * xcr)


def _s5_project(u_b, wbu_ref, re_ref, im_ref):
    per_slice = (LANES // S_GROUP) * S_STATE
    for k in range(S_WIDTH // LANES):
        res = jnp.dot(u_b[:, k * LANES:(k + 1) * LANES], wbu_ref[k], preferred_element_type=F32)
        re_ref[:, k * per_slice:(k + 1) * per_slice] = res[:, :per_slice]
        im_ref[:, k * per_slice:(k + 1) * per_slice] = res[:, per_slice:]


def _s5_output(re_ref, im_ref, u, vec_ref, wcre_ref, wcim_ref, wglu_ref):
    nblk = wcre_ref.shape[0]
    k_blk = S_LANES // nblk
    parts = []
    for m in range(nblk):
        ks = slice(m * k_blk, (m + 1) * k_blk)
        parts.append(lax.dot_general(re_ref[:, ks].astype(BF16), wcre_ref[m], _NT, preferred_element_type=F32)
                     - lax.dot_general(im_ref[:, ks].astype(BF16), wcim_ref[m], _NT, preferred_element_type=F32))
    y = jnp.concatenate(parts, axis=1) + _vec(vec_ref, "sd") * u
    g = jax.nn.gelu(y)
    return g * jax.nn.sigmoid(jnp.dot(g.astype(BF16), wglu_ref[...], preferred_element_type=F32)
                              + _vec(vec_ref, "bglu"))


_MATRIX_KEYS = ("wqk", "wvo", "wif", "rwa", "rwx", "wbu", "wcre", "wcim", "wglu")


def _tile(x, i):
    return x[i * SUBLANES:(i + 1) * SUBLANES]


def _conv_interleaved(x, tail_ref, w, bias, rows):
    ntiles = rows // SUBLANES
    pos = lax.broadcasted_iota(jnp.int32, (SUBLANES, x.shape[1]), 0)
    prev = tail_ref[...]
    before = [pltpu.roll(jnp.where(pos == SUBLANES - 1, _tile(prev, CONV_W - 1 - d), _tile(x, ntiles - d)), 1, 0)
              for d in range(1, CONV_W)]
    out = w[CONV_W - 1:CONV_W, :] * x + bias
    for j in range(1, CONV_W):
        shifted = jnp.concatenate(before[:j][::-1] + [x[:rows - j * SUBLANES]], axis=0)
        out = out + w[CONV_W - 1 - j:CONV_W - j, :] * shifted
    tail_ref[...] = x[rows - (CONV_W - 1) * SUBLANES:]
    return out


def _cumsum_interleaved(x, rows):
    tiles = [_tile(x, 0)]
    for i in range(1, rows // SUBLANES):
        tiles.append(tiles[-1] + _tile(x, i))
    total = tiles[-1]
    start = _tile_cumsum(total) - total
    return jnp.concatenate([t + start for t in tiles], axis=0)


def _cummax_interleaved(x, rows):
    tiles = [_tile(x, 0)]
    for i in range(1, rows // SUBLANES):
        tiles.append(jnp.maximum(tiles[-1], _tile(x, i)))
    best = tiles[-1]
    pos = lax.broadcasted_iota(jnp.int32, best.shape, 0)
    s = 1
    while s < SUBLANES:
        best = jnp.maximum(best, jnp.where(pos >= s, pltpu.roll(best, s, 0), -jnp.inf))
        s *= 2
    start = jnp.where(pos == 0, -jnp.inf, pltpu.roll(best, 1, 0))
    return jnp.concatenate([jnp.maximum(t, start) for t in tiles], axis=0)


def _scan_real_interleaved(a_ref, b_ref, carry, rows):
    ntiles = rows // SUBLANES
    pos = lax.broadcasted_iota(jnp.int32, (SUBLANES, a_ref.shape[1]), 0)
    h = _tile(b_ref, 0)
    aprod = _tile(a_ref, 0)
    for i in range(1, ntiles):
        a = _tile(a_ref, i)
        h = a * h + _tile(b_ref, i)
        aprod = aprod * a
    g = _tile_scan_real(aprod, h + jnp.where(pos == 0, aprod * carry, 0.0))
    h = jnp.where(pos == 0, carry, pltpu.roll(g, 1, 0))
    for i in range(ntiles):
        h = _tile(a_ref, i) * h + _tile(b_ref, i)
        b_ref[i * SUBLANES:(i + 1) * SUBLANES, :] = h
    return g[SUBLANES - 1:SUBLANES]


def _scan_cplx_interleaved(re_ref, im_ref, p_re, p_im, cre_ref, cim_ref, rows):
    ntiles = rows // SUBLANES
    width = re_ref.shape[1]
    pos = lax.broadcasted_iota(jnp.int32, (SUBLANES, SCAN_LANE_BLOCK), 0)
    for blk in range(width // SCAN_LANE_BLOCK):
        sl = slice(blk * SCAN_LANE_BLOCK, (blk + 1) * SCAN_LANE_BLOCK)
        pr = jnp.broadcast_to(p_re[:, sl], pos.shape)
        pi = jnp.broadcast_to(p_im[:, sl], pos.shape)
        c_r = cre_ref[:, sl]
        c_i = cim_ref[:, sl]
        sr = re_ref[0:SUBLANES, sl]
        si = im_ref[0:SUBLANES, sl]
        for i in range(1, ntiles):
            rs = slice(i * SUBLANES, (i + 1) * SUBLANES)
            sr, si = pr * sr - pi * si + re_ref[rs, sl], pr * si + pi * sr + im_ref[rs, sl]
        qr, qi = pr, pi
        n = 1
        while n < ntiles:
            qr, qi = qr * qr - qi * qi, 2.0 * (qr * qi)
            n *= 2
        gr, gi = _tile_scan_cplx(sr + jnp.where(pos == 0, qr * c_r - qi * c_i, 0.0),
                                 si + jnp.where(pos == 0, qr * c_i + qi * c_r, 0.0), qr, qi)
        sr = jnp.where(pos == 0, c_r, pltpu.roll(gr, 1, 0))
        si = jnp.where(pos == 0, c_i, pltpu.roll(gi, 1, 0))
        for i in range(ntiles):
            rs = slice(i * SUBLANES, (i + 1) * SUBLANES)
            sr, si = pr * sr - pi * si + re_ref[rs, sl], pr * si + pi * sr + im_ref[rs, sl]
            re_ref[rs, sl] = sr
            im_ref[rs, sl] = si
        cre_ref[:, sl] = gr[SUBLANES - 1:SUBLANES]
        cim_ref[:, sl] = gi[SUBLANES - 1:SUBLANES]


def _prompt_mixer_kernel(*refs, rows, nchunk, n_alias):
    (xnext_ref, xfirst_ref, vec_ref, win_ref,
     wqk_ref, wvo_ref, wif_ref, rwa_ref, rwx_ref, wbu_ref, wcre_ref, wcim_ref, wglu_ref) = refs[:13]
    (mixed_ref, c_ref, n_ref, m_ref, mtail_ref, h_ref, rtail_ref, sre_ref, sim_ref,
     q_scr, k_scr, v_scr, o_scr, ra_scr, rb_scr, ure_scr, uim_scr, proj_scr, xn_scr) = refs[13 + n_alias:]
    T = rows
    step = pl.program_id(0)
    slot = lax.rem(step, 2)
    proj_ref = proj_scr.at[slot]
    next_proj_ref = proj_scr.at[1 - slot]
    slab = (2 * MIX_WIDTH) // M_HEADS

    @pl.when(step == 0)
    def _():
        xn = _rmsnorm_bf16(xfirst_ref[...], vec_ref)
        for j in range(M_HEADS):
            cols = slice(j * slab, (j + 1) * slab)
            proj_scr[0, :, cols] = jnp.dot(xn, win_ref[:, cols], preferred_element_type=F32)

    @pl.when(lax.rem(step, nchunk) == 0)
    def _():
        for ref in (c_ref, n_ref, m_ref, mtail_ref, h_ref, rtail_ref, sre_ref, sim_ref):
            ref[...] = jnp.zeros(ref.shape, F32)

    xn_scr[...] = _rmsnorm_bf16(xnext_ref[...], vec_ref)

    xm = proj_ref[:, 0:M_WIDTH]
    xc = _silu(_conv_interleaved(xm, mtail_ref, _vec(vec_ref, "mcw"), _vec(vec_ref, "mcb"), T))
    gates = _mlstm_project(xc, xm, vec_ref, wqk_ref, wvo_ref, wif_ref, q_scr, k_scr, v_scr, o_scr)
    log_i = gates[:, :LANES]
    b_all = _cumsum_interleaved(_log_sigmoid(gates[:, LANES:]), T)
    a_max = _cummax_interleaved(log_i - b_all, T)
    m_prev = m_ref[...]
    log_inter = b_all + m_prev
    m_t_all = jnp.maximum(log_inter, b_all + a_max)
    w_inter_all = jnp.exp(log_inter - m_t_all)
    floor_all = jnp.exp(-m_t_all)
    b_end = b_all[T - 1:T, :]
    m_new = jnp.maximum(b_end + m_prev, b_end + a_max[T - 1:T, :])
    w_src_all = jnp.exp(b_end - b_all + log_i - m_new)
    decay_all = jnp.exp(b_end + m_prev - m_new)
    m_ref[...] = m_new
    b_t = b_all.T
    li_t = log_i.T

    rowi = lax.broadcasted_iota(jnp.int32, (T, T), 0)
    coli = lax.broadcasted_iota(jnp.int32, (T, T), 1)
    sub_len = T // SUBLANES

    def time_of(r):
        return (r & (SUBLANES - 1)) * sub_len + (r >> SUBLANE_SHIFT)

    causal = time_of(coli) <= time_of(rowi)
    k_scale = M_HEAD_DIM ** -0.5
    mnw = _vec(vec_ref, "mnw")
    mskip = _vec(vec_ref, "mskip")

    for h in range(M_HEADS):
        sl = slice(h * M_HEAD_DIM, (h + 1) * M_HEAD_DIM)
        c_prev = c_ref[h]
        n_prev = n_ref[h:h + 1, :]
        w_inter = w_inter_all[:, h:h + 1]
        w_src = w_src_all[:, h:h + 1]
        decay = decay_all[:, h:h + 1]

        q = q_scr[:, sl]
        ks = k_scr[:, sl] * k_scale
        v = v_scr[:, sl]
        q_b = q.astype(BF16)
        ks_b = ks.astype(BF16)

        log_d = jnp.where(causal, b_all[:, h:h + 1] - b_t[h:h + 1, :] + li_t[h:h + 1, :], -jnp.inf)
        w_intra = jnp.exp(log_d - m_t_all[:, h:h + 1])
        s = lax.dot_general(q_b, ks_b, _NT, preferred_element_type=F32) * w_intra
        inter = lax.dot_general(q_b, c_prev.astype(BF16), _NT, preferred_element_type=F32)
        num = jnp.dot(s.astype(BF16), v.astype(BF16), preferred_element_type=F32) + w_inter * inter
        den = (jnp.sum(s, axis=1, keepdims=True)
               + w_inter * jnp.sum(q * n_prev, axis=1, keepdims=True))
        hh = num / jnp.maximum(jnp.abs(den), floor_all[:, h:h + 1])

        vw_t = (v * w_src).T.astype(BF16)
        c_ref[h] = decay * c_prev + jnp.dot(vw_t, ks_b, preferred_element_type=F32)
        n_ref[h:h + 1, :] = decay * n_prev + jnp.sum(ks * w_src, axis=0, keepdims=True)

        z = proj_ref[:, MIX_WIDTH + h * M_HEAD_DIM:MIX_WIDTH + (h + 1) * M_HEAD_DIM]
        mixed_ref[:, sl] = _head_output(hh, o_scr[:, sl], xc[:, sl], z, mnw[:, sl], mskip[:, sl]).astype(BF16)

        cols = slice(h * slab, (h + 1) * slab)
        next_proj_ref[:, cols] = jnp.dot(xn_scr[...], win_ref[:, cols], preferred_element_type=F32)

    xr = proj_ref[:, M_WIDTH:M_WIDTH + R_WIDTH]
    xcr = _conv_interleaved(xr, rtail_ref, _vec(vec_ref, "rcw"), _vec(vec_ref, "rcb"), T)
    a, bb = _rglru_coeffs(xcr, vec_ref, rwa_ref, rwx_ref)
    ra_scr[...] = a
    rb_scr[...] = bb
    h_ref[...] = _scan_real_interleaved(ra_scr, rb_scr, h_ref[...], T)
    zr = proj_ref[:, MIX_WIDTH + M_WIDTH:MIX_WIDTH + M_WIDTH + R_WIDTH]
    mixed_ref[:, M_WIDTH:M_WIDTH + R_WIDTH] = (rb_scr[...] * _silu(zr)).astype(BF16)

    u = proj_ref[:, M_WIDTH + R_WIDTH:MIX_WIDTH]
    _s5_project(u.astype(BF16), wbu_ref, ure_scr, uim_scr)
    _scan_cplx_interleaved(ure_scr, uim_scr, _vec(vec_ref, "abre"), _vec(vec_ref, "abim"), sre_ref, sim_ref, T)
    glu = _s5_output(ure_scr, uim_scr, u, vec_ref, wcre_ref, wcim_ref, wglu_ref)
    zs = proj_ref[:, MIX_WIDTH + M_WIDTH + R_WIDTH:2 * MIX_WIDTH]
    mixed_ref[:, M_WIDTH + R_WIDTH:MIX_WIDTH] = (glu * _silu(zs)).astype(BF16)


_PROMPT_STATE_SHAPES = ((M_HEADS, M_HEAD_DIM, M_HEAD_DIM), (M_HEADS, M_HEAD_DIM), (1, LANES),
                        ((CONV_W - 1) * SUBLANES, M_WIDTH), (1, R_WIDTH), ((CONV_W - 1) * SUBLANES, R_WIDTH),
                        (1, S_LANES), (1, S_LANES))


def _prompt_mixer(x, total_rows, l, bsz, seq, vec, w_in_b, mats, prev_states):
    rows = PROMPT_CHUNK
    nchunk = seq // rows
    nstep = bsz * nchunk
    aliases = {}
    alias_args = []
    n_in = 4 + len(mats)
    if prev_states is not None:
        alias_args = list(prev_states)
        aliases = {n_in + k: 1 + k for k in range(len(alias_args))}

    def state_spec(shape):
        nd = len(shape)
        return pl.BlockSpec((None, None) + shape, lambda t: (l, t // nchunk) + (0,) * nd)

    out_shape = ([jax.ShapeDtypeStruct((total_rows, MIX_WIDTH), BF16)]
                 + [jax.ShapeDtypeStruct((DEPTH, bsz) + s, F32) for s in _PROMPT_STATE_SHAPES])
    scratch = ([pltpu.VMEM((rows, w), F32)
                for w in (M_WIDTH, M_WIDTH, M_WIDTH, M_WIDTH, R_WIDTH, R_WIDTH, S_LANES, S_LANES)]
               + [pltpu.VMEM((2, rows, 2 * MIX_WIDTH), F32), pltpu.VMEM((rows, D_MODEL), BF16)])
    return pl.pallas_call(
        functools.partial(_prompt_mixer_kernel, rows=rows, nchunk=nchunk, n_alias=len(alias_args)),
        grid=(nstep,),
        in_specs=([pl.BlockSpec((rows, D_MODEL), lambda t: (jnp.minimum(t + 1, nstep - 1), 0)),
                   pl.BlockSpec((rows, D_MODEL), lambda t: (0, 0), pipeline_mode=pl.Buffered(1)),
                   _layer_spec(vec, l),
                   pl.BlockSpec(w_in_b.shape, lambda t: (0, 0), pipeline_mode=pl.Buffered(1))]
                  + [_layer_spec(w, l) for w in mats] + [_ANY_SPEC] * len(alias_args)),
        out_specs=([pl.BlockSpec((rows, MIX_WIDTH), lambda t: (t, 0))]
                   + [state_spec(s) for s in _PROMPT_STATE_SHAPES]),
        out_shape=out_shape,
        scratch_shapes=scratch,
        input_output_aliases=aliases,
        compiler_params=pltpu.CompilerParams(
            dimension_semantics=("arbitrary",), vmem_limit_bytes=VMEM_LIMIT_BYTES),
        name="prompt_mixer",
    )(x, x, vec, w_in_b, *mats, *alias_args)


def _seg_last(x, groups):
    x3 = x.reshape(groups, SUBLANES, x.shape[-1])
    return jnp.broadcast_to(x3[:, SUBLANES - 1:SUBLANES, :], x3.shape).reshape(x.shape)


def _seg_max(x, groups):
    x3 = x.reshape(groups, SUBLANES, x.shape[-1])
    return jnp.broadcast_to(jnp.max(x3, axis=1, keepdims=True), x3.shape).reshape(x.shape)


def _seg_sum(x, groups):
    x3 = x.reshape(groups, SUBLANES, x.shape[-1])
    return jnp.broadcast_to(jnp.sum(x3, axis=1, keepdims=True), x3.shape).reshape(x.shape)


def _seg_rows(state_ref, lanes=slice(None)):
    x = state_ref[:, :, lanes]
    return jnp.broadcast_to(x, (x.shape[0], SUBLANES, x.shape[2])).reshape(x.shape[0] * SUBLANES, x.shape[2])


def _seg_state(x):
    x3 = x.reshape(x.shape[0] // SUBLANES, SUBLANES, x.shape[-1])
    return x3[:, SUBLANES - 1:SUBLANES, :]


def _with_history(x, buf_ref, new_buf_ref):
    g = x.shape[0] // SUBLANES
    x3 = x.reshape(g, SUBLANES, x.shape[1])
    lead = SAMPLE_LEAD - (CONV_W - 1)
    full = jnp.concatenate([jnp.zeros((g, lead, x.shape[1]), F32), buf_ref[...], x3[:, SAMPLE_LEAD:, :]], axis=1)
    new_buf_ref[...] = full[:, SUBLANES - (CONV_W - 1):, :]
    return full.reshape(x.shape)


def _conv_rolled(xf, w, bias):
    out = w[CONV_W - 1:CONV_W, :] * xf + bias
    for j in range(1, CONV_W):
        out = out + w[CONV_W - 1 - j:CONV_W - j, :] * pltpu.roll(xf, j, 0)
    return out


_SAMPLE_STATE_WIDTHS = (M_WIDTH, LANES, M_WIDTH, R_WIDTH, R_WIDTH, S_LANES, S_LANES)
_SAMPLE_STATE_ROWS = (1, 1, CONV_W - 1, 1, CONV_W - 1, 1, 1)


def _sample_mixer_kernel(*refs, rows, c_seqs, n_alias):
    (proj_ref, c0_ref, n0_ref, m0_ref, mtail0_ref, h0_ref, rtail0_ref, sre0_ref, sim0_ref,
     vec_ref, wqk_ref, wvo_ref, wif_ref, rwa_ref, rwx_ref, wbu_ref, wcre_ref, wcim_ref, wglu_ref) = refs[:19]
    (mixed_ref, c_ref, n_ref, m_ref, mconv_ref, h_ref, rconv_ref, sre_ref, sim_ref,
     q_scr, k_scr, v_scr, o_scr, xc_scr, intert_scr, vwt_scr, qt_scr, ksb_scr, dec_scr, b_scr, li_scr,
     ure_scr, uim_scr) = refs[19 + n_alias:]
    R = rows
    G = R // SUBLANES
    sub = pl.program_id(1)
    k_scale = M_HEAD_DIM ** -0.5
    pos = _sublane_pos((R, 1))
    valid = pos >= SAMPLE_LEAD
    first_token = pos == SAMPLE_LEAD

    @pl.when(sub == 0)
    def _():
        xm = _with_history(proj_ref[:, 0:M_WIDTH], mtail0_ref, mconv_ref)
        xc = _silu(_conv_rolled(xm, _vec(vec_ref, "mcw"), _vec(vec_ref, "mcb")))
        xc_scr[...] = xc
        gates = _mlstm_project(xc, xm, vec_ref, wqk_ref, wvo_ref, wif_ref, q_scr, k_scr, v_scr, o_scr)
        log_i = jnp.where(valid, gates[:, :LANES], -jnp.inf)
        log_f = jnp.where(valid, _log_sigmoid(gates[:, LANES:]), 0.0)
        b_all = _tile_cumsum(log_f)
        b_end = _seg_last(b_all, G)
        m_rows = _seg_rows(m0_ref)
        n_rows = _seg_rows(n0_ref)
        log_src = b_end - b_all + log_i
        m_new = jnp.maximum(b_end + m_rows, _seg_max(log_src, G))
        w_src_all = jnp.exp(log_src - m_new)
        decay_all = jnp.exp(b_end + m_rows - m_new)
        dec_scr[...] = decay_all
        b_scr[...] = b_all
        li_scr[...] = log_i
        m_ref[...] = _seg_state(m_new)
        for h in range(M_HEADS):
            sl = slice(h * M_HEAD_DIM, (h + 1) * M_HEAD_DIM)
            ks = k_scr[:, sl] * k_scale
            w_src = w_src_all[:, h:h + 1]
            n_ref[:, :, sl] = _seg_state(decay_all[:, h:h + 1] * n_rows[:, sl] + _seg_sum(ks * w_src, G))
            vwt_scr[h] = (v_scr[:, sl] * w_src).T
            qt_scr[h] = q_scr[:, sl].T.astype(BF16)
            ksb_scr[h] = ks.astype(BF16)
            intert_scr[h] = jnp.zeros((M_HEAD_DIM, R), F32)

        xr = _with_history(proj_ref[:, M_WIDTH:M_WIDTH + R_WIDTH], rtail0_ref, rconv_ref)
        xcr = _conv_rolled(xr, _vec(vec_ref, "rcw"), _vec(vec_ref, "rcb"))
        a, bb = _rglru_coeffs(xcr, vec_ref, rwa_ref, rwx_ref)
        hs = _tile_scan_real(a, jnp.where(valid, bb, 0.0) + jnp.where(first_token, a * _seg_rows(h0_ref), 0.0))
        h_ref[...] = _seg_state(hs)
        zr = proj_ref[:, MIX_WIDTH + M_WIDTH:MIX_WIDTH + M_WIDTH + R_WIDTH]
        mixed_ref[:, M_WIDTH:M_WIDTH + R_WIDTH] = jnp.where(valid, hs * _silu(zr), 0.0).astype(BF16)

        u = proj_ref[:, M_WIDTH + R_WIDTH:MIX_WIDTH]
        _s5_project(u.astype(BF16), wbu_ref, ure_scr, uim_scr)
        p_re = _vec(vec_ref, "abre")
        p_im = _vec(vec_ref, "abim")
        for blk in range(S_LANES // SCAN_LANE_BLOCK):
            sl = slice(blk * SCAN_LANE_BLOCK, (blk + 1) * SCAN_LANE_BLOCK)
            pr = jnp.broadcast_to(p_re[:, sl], (R, SCAN_LANE_BLOCK))
            pi = jnp.broadcast_to(p_im[:, sl], (R, SCAN_LANE_BLOCK))
            s0r = _seg_rows(sre0_ref, sl)
            s0i = _seg_rows(sim0_ref, sl)
            sr, si = _tile_scan_cplx(
                jnp.where(valid, ure_scr[:, sl], 0.0) + jnp.where(first_token, pr * s0r - pi * s0i, 0.0),
                jnp.where(valid, uim_scr[:, sl], 0.0) + jnp.where(first_token, pr * s0i + pi * s0r, 0.0), pr, pi)
            ure_scr[:, sl] = sr
            uim_scr[:, sl] = si
            sre_ref[:, :, sl] = _seg_state(sr)
            sim_ref[:, :, sl] = _seg_state(si)
        glu = _s5_output(ure_scr, uim_scr, u, vec_ref, wcre_ref, wcim_ref, wglu_ref)
        zs = proj_ref[:, MIX_WIDTH + M_WIDTH + R_WIDTH:2 * MIX_WIDTH]
        mixed_ref[:, M_WIDTH + R_WIDTH:MIX_WIDTH] = jnp.where(valid, glu * _silu(zs), 0.0).astype(BF16)

    lane_seq = lax.broadcasted_iota(jnp.int32, (M_HEAD_DIM, R), 1) >> SUBLANE_SHIFT
    seq0 = sub * c_seqs
    for h in range(M_HEADS):
        c_old = c0_ref[:, h].reshape(c_seqs * M_HEAD_DIM, M_HEAD_DIM)
        readout = jnp.dot(c_old.astype(BF16), qt_scr[h], preferred_element_type=F32)
        acc = intert_scr[h]
        vwt = vwt_scr[h]
        lhs = []
        for s in range(c_seqs):
            own = lane_seq == seq0 + s
            acc = jnp.where(own, readout[s * M_HEAD_DIM:(s + 1) * M_HEAD_DIM], acc)
            lhs.append(jnp.where(own, vwt, 0.0))
        intert_scr[h] = acc
        upd = jnp.dot(jnp.concatenate(lhs, axis=0).astype(BF16), ksb_scr[h], preferred_element_type=F32)
        for s in range(c_seqs):
            r0 = pl.multiple_of((seq0 + s) * SUBLANES, SUBLANES)
            c_ref[s, h] = (dec_scr[pl.ds(r0, 1), h:h + 1] * c0_ref[s, h]
                           + upd[s * M_HEAD_DIM:(s + 1) * M_HEAD_DIM])

    @pl.when(sub == pl.num_programs(1) - 1)
    def _():
        _sample_heads(proj_ref, m0_ref, n0_ref, vec_ref, mixed_ref,
                      q_scr, k_scr, v_scr, o_scr, xc_scr, intert_scr, b_scr, li_scr, valid, R)


def _sample_heads(proj_ref, m0_ref, n0_ref, vec_ref, mixed_ref,
                  q_scr, k_scr, v_scr, o_scr, xc_scr, intert_scr, b_scr, li_scr, valid, R):
    rowi = lax.broadcasted_iota(jnp.int32, (R, R), 0)
    coli = lax.broadcasted_iota(jnp.int32, (R, R), 1)
    same_causal = jnp.logical_and(coli <= rowi, (coli >> SUBLANE_SHIFT) == (rowi >> SUBLANE_SHIFT))
    k_scale = M_HEAD_DIM ** -0.5
    mnw = _vec(vec_ref, "mnw")
    mskip = _vec(vec_ref, "mskip")
    n_rows = _seg_rows(n0_ref)
    b_all = b_scr[...]
    log_i = li_scr[...]
    log_inter = b_all + _seg_rows(m0_ref)
    m_t_all = jnp.maximum(log_inter, b_all + _tile_cummax(log_i - b_all))
    w_inter_all = jnp.exp(log_inter - m_t_all)
    floor_all = jnp.exp(-m_t_all)
    b_t = b_all.T
    li_t = log_i.T

    for h in range(M_HEADS):
        sl = slice(h * M_HEAD_DIM, (h + 1) * M_HEAD_DIM)
        q = q_scr[:, sl]
        ks = k_scr[:, sl] * k_scale
        v = v_scr[:, sl]
        q_b = q.astype(BF16)
        ks_b = ks.astype(BF16)
        w_inter = w_inter_all[:, h:h + 1]

        log_d = jnp.where(same_causal, b_all[:, h:h + 1] - b_t[h:h + 1, :] + li_t[h:h + 1, :], -jnp.inf)
        w_intra = jnp.exp(log_d - m_t_all[:, h:h + 1])
        s = lax.dot_general(q_b, ks_b, _NT, preferred_element_type=F32) * w_intra
        num = (jnp.dot(s.astype(BF16), v.astype(BF16), preferred_element_type=F32)
               + w_inter * intert_scr[h].T)
        den = (jnp.sum(s, axis=1, keepdims=True)
               + w_inter * jnp.sum(q * n_rows[:, sl], axis=1, keepdims=True))
        hh = num / jnp.maximum(jnp.abs(den), floor_all[:, h:h + 1])

        z = proj_ref[:, MIX_WIDTH + h * M_HEAD_DIM:MIX_WIDTH + (h + 1) * M_HEAD_DIM]
        out = _head_output(hh, o_scr[:, sl], xc_scr[:, sl], z, mnw[:, sl], mskip[:, sl])
        mixed_ref[:, sl] = jnp.where(valid, out, 0.0).astype(BF16)


def _sample_mixer(proj, l, row0, mixed_prev, c_all, row_state, vec, mats, prev_out):
    R = SAMPLE_ROWS
    nrows = proj.shape[0]
    nsub = (R // SUBLANES) // SAMPLE_C_SEQS
    blk0 = row0 // R
    c_spec = pl.BlockSpec((None, SAMPLE_C_SEQS, M_HEADS, M_HEAD_DIM, M_HEAD_DIM),
                          lambda i, j: (l, i * nsub + j, 0, 0, 0))

    def state_spec(width, rows_per_seq):
        return pl.BlockSpec((None, R // SUBLANES, rows_per_seq, width), lambda i, j: (l, i, 0, 0))

    def state_shape(width, rows_per_seq):
        return (DEPTH, nrows // SUBLANES, rows_per_seq, width)

    state_specs = [state_spec(w, r) for w, r in zip(_SAMPLE_STATE_WIDTHS, _SAMPLE_STATE_ROWS)]
    alias_args = [mixed_prev] + (list(prev_out) if prev_out is not None else [])
    first_alias = 2 + len(row_state) + 1 + len(mats)
    aliases = {first_alias + k: k for k in range(len(alias_args))}
    out_shape = ([jax.ShapeDtypeStruct(mixed_prev.shape, BF16), jax.ShapeDtypeStruct(c_all.shape, F32)]
                 + [jax.ShapeDtypeStruct(state_shape(w, r), F32)
                    for w, r in zip(_SAMPLE_STATE_WIDTHS, _SAMPLE_STATE_ROWS)])
    scratch = ([pltpu.VMEM((R, M_WIDTH), F32)] * 5
               + [pltpu.VMEM((M_HEADS, M_HEAD_DIM, R), F32), pltpu.VMEM((M_HEADS, M_HEAD_DIM, R), F32),
                  pltpu.VMEM((M_HEADS, M_HEAD_DIM, R), BF16), pltpu.VMEM((M_HEADS, R, M_HEAD_DIM), BF16)]
               + [pltpu.VMEM((R, LANES), F32)] * 3
               + [pltpu.VMEM((R, S_LANES), F32)] * 2)
    return pl.pallas_call(
        functools.partial(_sample_mixer_kernel, rows=R, c_seqs=SAMPLE_C_SEQS, n_alias=len(alias_args)),
        grid=(nrows // R, nsub),
        in_specs=([pl.BlockSpec((R, 2 * MIX_WIDTH), lambda i, j: (i, 0)), c_spec] + state_specs
                  + [_layer_spec(vec, l)] + [_layer_spec(w, l) for w in mats] + [_ANY_SPEC] * len(alias_args)),
        out_specs=[pl.BlockSpec((R, MIX_WIDTH), lambda i, j: (blk0 + i, 0)), c_spec] + state_specs,
        out_shape=out_shape,
        scratch_shapes=scratch,
        input_output_aliases=aliases,
        compiler_params=pltpu.CompilerParams(
            dimension_semantics=("parallel", "arbitrary"), vmem_limit_bytes=VMEM_LIMIT_BYTES),
        name="sample_mixer",
    )(proj, c_all, *row_state, vec, *mats, *alias_args)


def _vector_table(p, abre, abim):
    def split_gates(g):
        pad = [(0, 0)] * (g.ndim - 1) + [(0, LANES - M_HEADS)]
        return jnp.concatenate([jnp.pad(g[..., :M_HEADS], pad), jnp.pad(g[..., M_HEADS:], pad)], axis=-1)

    table = {"mcw": p["m_conv_w"], "rcw": p["r_conv_w"], "mcb": p["m_conv_b"], "mnw": p["m_norm_w"],
             "mskip": p["m_skip"], "rcb": p["r_conv_b"], "rba": p["r_ba"], "rbx": p["r_bx"], "rlam": p["r_lam"],
             "sd": p["s_d"], "bglu": p["s_b_glu"], "bif": split_gates(p["m_b_if"]), "abre": abre, "abim": abim,
             "norm_w": p["norm_w"]}

    def group_rows(group):
        parts = [table[name].reshape(DEPTH, n, w) for name, n, w in group]
        fill = _VEC_WIDTH - sum(w for _, _, w in group)
        if fill:
            parts.append(jnp.zeros((DEPTH, group[0][1], fill), F32))
        return parts[0] if len(parts) == 1 else jnp.concatenate(parts, axis=-1)

    return jnp.concatenate([group_rows(g) for g in _VEC_GROUPS]
                           + [jnp.zeros((DEPTH, _VEC_ROWS - _VEC_USED_ROWS, _VEC_WIDTH), F32)], axis=1)


def _sample_rows_state(n, m, mconv, h, rconv, sre, sim):
    bsz = n.shape[1]

    def per_seq(x):
        return x.reshape(DEPTH, bsz, 1, x.shape[-1])

    m_pad = jnp.pad(m, ((0, 0), (0, 0), (0, LANES - M_HEADS)))
    return (per_seq(n.reshape(DEPTH, bsz, M_WIDTH)), per_seq(m_pad), mconv, per_seq(h), rconv,
            per_seq(sre.reshape(DEPTH, bsz, S_LANES)), per_seq(sim.reshape(DEPTH, bsz, S_LANES)))


def _sample_state_from_rows(n, m, mconv, h, rconv, sre, sim):
    bsz = n.shape[1]
    return (n.reshape(DEPTH, bsz, M_HEADS, M_HEAD_DIM), m.reshape(DEPTH, bsz, LANES)[..., :M_HEADS],
            mconv, h.reshape(DEPTH, bsz, R_WIDTH), rconv,
            sre.reshape(DEPTH, bsz, S_GROUPS, S_STATE), sim.reshape(DEPTH, bsz, S_GROUPS, S_STATE))


def _prompt_state_from_kernel(c, n, m, mtail, h, rtail, sre, sim):
    bsz = c.shape[1]
    return (c, n, m[:, :, 0, :M_HEADS], mtail[:, :, SUBLANES - 1::SUBLANES], h.reshape(DEPTH, bsz, R_WIDTH),
            rtail[:, :, SUBLANES - 1::SUBLANES],
            sre.reshape(DEPTH, bsz, S_GROUPS, S_STATE), sim.reshape(DEPTH, bsz, S_GROUPS, S_STATE))


def _interleave_chunks(x, inverse=False):
    bsz, seq, d = x.shape
    sub_len = PROMPT_CHUNK // SUBLANES
    inner = (sub_len, SUBLANES) if inverse else (SUBLANES, sub_len)
    return x.reshape(bsz, seq // PROMPT_CHUNK, *inner, d).transpose(0, 1, 3, 2, 4).reshape(bsz, seq, d)


def kernel(x_prompt, x_sample, state_mlstm_C, state_mlstm_n, state_mlstm_m, state_mlstm_conv, state_rglru_h, state_rglru_conv, state_s5_re, state_s5_im, norm_w, w_in, w_out, m_conv_w, m_conv_b, m_wq, m_wk, m_wv, m_wo, m_w_if, m_b_if, m_norm_w, m_skip, r_conv_w, r_conv_b, r_wa, r_ba, r_wx, r_bx, r_lam, s_lam_re, s_lam_im, s_b_re, s_b_im, s_c_re, s_c_im, s_d, s_log_step, s_w_glu, s_b_glu, final_norm_w):
    p = dict(norm_w=norm_w, m_conv_w=m_conv_w, m_conv_b=m_conv_b, m_wq=m_wq, m_wk=m_wk,
             m_wv=m_wv, m_wo=m_wo, m_w_if=m_w_if, m_b_if=m_b_if, m_norm_w=m_norm_w, m_skip=m_skip,
             r_conv_w=r_conv_w, r_conv_b=r_conv_b, r_wa=r_wa, r_ba=r_ba, r_wx=r_wx, r_bx=r_bx, r_lam=r_lam,
             s_lam_re=s_lam_re, s_lam_im=s_lam_im, s_log_step=s_log_step, s_b_re=s_b_re, s_b_im=s_b_im,
             s_c_re=s_c_re, s_c_im=s_c_im, s_d=s_d, s_w_glu=s_w_glu, s_b_glu=s_b_glu)
    abre, abim, mats = _prep(p)
    vec = _vector_table(p, abre, abim)
    final_w = final_norm_w.reshape(1, D_MODEL)

    bsz, seq, _ = x_prompt.shape
    dec_batch, dec_seq, _ = x_sample.shape
    n_prompt = bsz * seq
    n_sample = dec_batch * SUBLANES
    stream = [_interleave_chunks(x_prompt).reshape(n_prompt, D_MODEL),
              jnp.pad(x_sample, ((0, 0), (SAMPLE_LEAD, 0), (0, 0))).reshape(n_sample, D_MODEL)]
    rows_state = _sample_rows_state(state_mlstm_n, state_mlstm_m, state_mlstm_conv, state_rglru_h,
                                    state_rglru_conv, state_s5_re, state_s5_im)
    pr_states = None
    sa_states = None
    for l in range(DEPTH):
        last = l == DEPTH - 1
        proj_s, w_in_b = _inproj(stream[-1], stream[-1].shape[0] - n_sample, n_sample, vec, w_in, l)
        mixed, *pr_states = _prompt_mixer(stream[0], n_prompt + n_sample, l, bsz, seq, vec, w_in_b, mats, pr_states)
        mixed, *sa_states = _sample_mixer(proj_s, l, n_prompt, mixed, state_mlstm_C, rows_state, vec, mats, sa_states)
        stream = _outproj(mixed, stream, w_out, l, final_w, last, split_rows=n_prompt if last else None)
    y_prompt = _interleave_chunks(stream[0].reshape(bsz, seq, D_MODEL), inverse=True)
    y_sample = stream[1]
    return (y_prompt, y_sample, *_prompt_state_from_kernel(*pr_states),
            sa_states[0], *_sample_state_from_rows(*sa_states[1:]))
```

```python
import functools

import jax
import jax.numpy as jnp
from jax import lax
from jax.experimental import pallas as pl
from jax.experimental.pallas import tpu as pltpu

F32 = jnp.float32
BF16 = jnp.bfloat16

D_MODEL = 2048
DEPTH = 2
MIX_WIDTH = D_MODEL
M_WIDTH = MIX_WIDTH // 2
R_WIDTH = MIX_WIDTH // 4
S_WIDTH = MIX_WIDTH - M_WIDTH - R_WIDTH
M_HEADS = 8
M_HEAD_DIM = M_WIDTH // M_HEADS
R_BLOCKS = 8
RG_C = 8.0
S_GROUP = 16
S_GROUPS = S_WIDTH // S_GROUP
S_STATE = 64
S_LANES = S_GROUPS * S_STATE
CONV_W = 4
EPS = 1e-6

SUBLANES = 8
SUBLANE_SHIFT = SUBLANES.bit_length() - 1
LANES = 128
VMEM_LIMIT_BYTES = 56 * 1024 * 1024

PROMPT_CHUNK = 256
SAMPLE_ROWS = 128
SAMPLE_C_SEQS = 8
SAMPLE_LEAD = SUBLANES - 4
INPROJ_TM = 512
INPROJ_TN = 1024
OUTPROJ_TM = 512
SCAN_LANE_BLOCK = 256

_NT = (((1,), (1,)), ((), ()))

_VEC_WIDTH = D_MODEL
_VEC_GROUPS = ((("mcw", CONV_W, M_WIDTH), ("rcw", CONV_W, R_WIDTH)),
               (("mcb", 1, M_WIDTH), ("mnw", 1, M_WIDTH)),
               (("mskip", 1, M_WIDTH), ("rcb", 1, R_WIDTH), ("rba", 1, R_WIDTH)),
               (("rbx", 1, R_WIDTH), ("rlam", 1, R_WIDTH), ("sd", 1, S_WIDTH), ("bglu", 1, S_WIDTH)),
               (("bif", 1, 2 * LANES),),
               (("abre", 1, S_LANES),), (("abim", 1, S_LANES),), (("norm_w", 1, D_MODEL),))
_VEC_LAYOUT = {}
_row = 0
for _group in _VEC_GROUPS:
    _lane = 0
    for _name, _n, _w in _group:
        _VEC_LAYOUT[_name] = (_row, _n, _lane, _w)
        _lane += _w
    _row += _group[0][1]
_VEC_USED_ROWS = _row
_VEC_ROWS = -(-_row // SUBLANES) * SUBLANES


def _vec(vec_ref, name):
    r0, n, l0, w = _VEC_LAYOUT[name]
    return vec_ref[r0:r0 + n, l0:l0 + w]


def _layer_spec(arr, l):
    nd = arr.ndim - 1
    return pl.BlockSpec((None,) + arr.shape[1:], lambda *_: (l,) + (0,) * nd, pipeline_mode=pl.Buffered(1))


_ANY_SPEC = pl.BlockSpec(memory_space=pl.ANY)


S5_IN_SLICES = S_WIDTH // LANES
S5_IN_GROUPS = LANES // S_GROUP
S5_OUT_HALVES = 2
S5_OUT_GROUPS = S_GROUPS // S5_OUT_HALVES


def _s5_prep_kernel(lr_ref, li_ref, ls_ref, brt_ref, bit_ref, cre_ref, cim_ref,
                    abre_ref, abim_ref, wbu_ref, wcre_ref, wcim_ref):
    lr = lr_ref[0]
    li = li_ref[0]
    dt = jnp.exp(ls_ref[0])
    mag = jnp.exp(lr * dt)
    ang = li * dt
    ab_re = mag * jnp.cos(ang)
    ab_im = mag * jnp.sin(ang)
    den = lr * lr + li * li
    nr = ab_re - 1.0
    f_re = (nr * lr + ab_im * li) / den
    f_im = (ab_im * lr - nr * li) / den
    br = brt_ref[0]
    bi = bit_ref[0]
    abre_ref[0] = ab_re
    abim_ref[0] = ab_im
    bb_re = f_re * br - f_im * bi
    bb_im = f_re * bi + f_im * br

    per_slice = S5_IN_GROUPS * S_STATE
    lane_group = lax.broadcasted_iota(jnp.int32, (S_GROUP, per_slice), 1) >> (S_STATE.bit_length() - 1)
    for k in range(S5_IN_SLICES):
        re_k = bb_re[:, k * per_slice:(k + 1) * per_slice]
        im_k = bb_im[:, k * per_slice:(k + 1) * per_slice]
        for a in range(S5_IN_GROUPS):
            rows = slice(a * S_GROUP, (a + 1) * S_GROUP)
            wbu_ref[0, k, rows, 0:per_slice] = jnp.where(lane_group == a, re_k, 0.0).astype(BF16)
            wbu_ref[0, k, rows, per_slice:2 * per_slice] = jnp.where(lane_group == a, im_k, 0.0).astype(BF16)

    row_group = lax.broadcasted_iota(jnp.int32, (S5_OUT_GROUPS * S_GROUP, S_STATE), 0) >> (S_GROUP.bit_length() - 1)
    for src_ref, dst_ref in ((cre_ref, wcre_ref), (cim_ref, wcim_ref)):
        for m in range(S5_OUT_HALVES):
            c = src_ref[0, m]
            for a in range(S5_OUT_GROUPS):
                dst_ref[0, m, :, a * S_STATE:(a + 1) * S_STATE] = jnp.where(row_group == a, c, 0.0).astype(BF16)


def _s5_prep(s_lam_re, s_lam_im, s_log_step, s_b_re, s_b_im, s_c_re, s_c_im):
    lr = s_lam_re.reshape(DEPTH, 1, S_LANES)
    li = s_lam_im.reshape(DEPTH, 1, S_LANES)
    ls = jnp.repeat(s_log_step, S_STATE, axis=-1).reshape(DEPTH, 1, S_LANES)
    brt = s_b_re.reshape(DEPTH, S_LANES, S_GROUP).transpose(0, 2, 1)
    bit = s_b_im.reshape(DEPTH, S_LANES, S_GROUP).transpose(0, 2, 1)
    c_shape = (DEPTH, S5_OUT_HALVES, S5_OUT_GROUPS * S_GROUP, S_STATE)
    wbu_shape = (DEPTH, S5_IN_SLICES, LANES, 2 * S5_IN_GROUPS * S_STATE)
    wc_shape = (DEPTH, S5_OUT_HALVES, S5_OUT_GROUPS * S_GROUP, S5_OUT_GROUPS * S_STATE)

    def layer_block(shape):
        nd = len(shape) - 1
        return pl.BlockSpec((1,) + shape[1:], lambda l: (l,) + (0,) * nd)

    vec = layer_block((DEPTH, 1, S_LANES))
    mat = layer_block((DEPTH, S_GROUP, S_LANES))
    return pl.pallas_call(
        _s5_prep_kernel,
        grid=(DEPTH,),
        in_specs=[vec, vec, vec, mat, mat, layer_block(c_shape), layer_block(c_shape)],
        out_specs=[vec, vec, layer_block(wbu_shape), layer_block(wc_shape), layer_block(wc_shape)],
        out_shape=[jax.ShapeDtypeStruct((DEPTH, 1, S_LANES), F32),
                   jax.ShapeDtypeStruct((DEPTH, 1, S_LANES), F32),
                   jax.ShapeDtypeStruct(wbu_shape, BF16),
                   jax.ShapeDtypeStruct(wc_shape, BF16),
                   jax.ShapeDtypeStruct(wc_shape, BF16)],
        name="s5_prep",
    )(lr, li, ls, brt, bit, s_c_re.reshape(c_shape), s_c_im.reshape(c_shape))


def _stream_specs(parts, tm):
    if len(parts) == 1:
        return [pl.BlockSpec((tm, D_MODEL), lambda i: (i, 0))]
    nb0 = parts[0].shape[0] // tm
    return [pl.BlockSpec((tm, D_MODEL), lambda i: (jnp.minimum(i, nb0 - 1), 0)),
            pl.BlockSpec((tm, D_MODEL), lambda i: (jnp.maximum(i - nb0, 0), 0))]


def _read_stream(refs, nb0):
    if len(refs) == 1:
        return refs[0][...]
    return jnp.where(pl.program_id(0) < nb0, refs[0][...], refs[1][...])


def _rmsnorm_bf16(x, vec_ref):
    ms = jnp.mean(x * x, axis=-1, keepdims=True)
    return (x * lax.rsqrt(ms + EPS) * _vec(vec_ref, "norm_w")).astype(BF16)


def _inproj_kernel(x_ref, vec_ref, w_ref, o_ref, wb_ref, xn_scr):
    @pl.when(pl.program_id(0) == 0)
    def _():
        xn_scr[...] = _rmsnorm_bf16(x_ref[...], vec_ref)

    wb = w_ref[...].astype(BF16)
    wb_ref[...] = wb
    o_ref[...] = jnp.dot(xn_scr[...], wb, preferred_element_type=F32)


def _inproj(x, row0, nrows, vec, w_in, l):
    blk0 = row0 // nrows
    return pl.pallas_call(
        _inproj_kernel,
        grid=((2 * MIX_WIDTH) // INPROJ_TN,),
        in_specs=[pl.BlockSpec((nrows, D_MODEL), lambda j: (blk0, 0), pipeline_mode=pl.Buffered(1)),
                  _layer_spec(vec, l),
                  pl.BlockSpec((None, D_MODEL, INPROJ_TN), lambda j: (l, 0, j))],
        out_specs=[pl.BlockSpec((nrows, INPROJ_TN), lambda j: (0, j)),
                   pl.BlockSpec((D_MODEL, INPROJ_TN), lambda j: (0, j))],
        out_shape=[jax.ShapeDtypeStruct((nrows, 2 * MIX_WIDTH), F32),
                   jax.ShapeDtypeStruct((D_MODEL, 2 * MIX_WIDTH), BF16)],
        scratch_shapes=[pltpu.VMEM((nrows, D_MODEL), BF16)],
        compiler_params=pltpu.CompilerParams(
            dimension_semantics=("arbitrary",), vmem_limit_bytes=VMEM_LIMIT_BYTES),
        name="inproj",
    )(x, vec, w_in)


def _outproj_kernel(*refs, n_x, nb0, n_out, nb0_out, final):
    mixed_ref, x_refs = refs[0], refs[1:1 + n_x]
    w_ref, fw_ref = refs[1 + n_x:3 + n_x]
    o_refs, wb_scr = refs[3 + n_x:-1], refs[-1]

    @pl.when(pl.program_id(0) == 0)
    def _():
        wb_scr[...] = w_ref[...].astype(BF16)

    y = _read_stream(x_refs, nb0) + jnp.dot(mixed_ref[...], wb_scr[...], preferred_element_type=F32)
    if final:
        ms = jnp.mean(y * y, axis=-1, keepdims=True)
        y = y * lax.rsqrt(ms + EPS) * fw_ref[...]
    if n_out == 1:
        o_refs[0][...] = y
    else:
        @pl.when(pl.program_id(0) < nb0_out)
        def _():
            o_refs[0][...] = y

        @pl.when(pl.program_id(0) >= nb0_out)
        def _():
            o_refs[1][...] = y.reshape(y.shape[0] // SUBLANES, SUBLANES, y.shape[1])[:, SAMPLE_LEAD:, :]


def _outproj(mixed, x_parts, w_out, l, final_w, final, split_rows=None):
    n = mixed.shape[0]
    if split_rows is None:
        out_parts = [jax.ShapeDtypeStruct((n, D_MODEL), F32)]
        out_specs = [pl.BlockSpec((OUTPROJ_TM, D_MODEL), lambda i: (i, 0))]
        nb0_out = n // OUTPROJ_TM
    else:
        nb0_out = split_rows // OUTPROJ_TM
        tokens = SUBLANES - SAMPLE_LEAD
        out_parts = [jax.ShapeDtypeStruct((split_rows, D_MODEL), F32),
                     jax.ShapeDtypeStruct(((n - split_rows) // SUBLANES, tokens, D_MODEL), F32)]
        out_specs = [pl.BlockSpec((OUTPROJ_TM, D_MODEL), lambda i: (jnp.minimum(i, nb0_out - 1), 0)),
                     pl.BlockSpec((OUTPROJ_TM // SUBLANES, tokens, D_MODEL),
                                  lambda i: (jnp.maximum(i - nb0_out, 0), 0, 0))]
    return pl.pallas_call(
        functools.partial(_outproj_kernel, n_x=len(x_parts), nb0=x_parts[0].shape[0] // OUTPROJ_TM,
                          n_out=len(out_parts), nb0_out=nb0_out, final=final),
        grid=(n // OUTPROJ_TM,),
        in_specs=([pl.BlockSpec((OUTPROJ_TM, MIX_WIDTH), lambda i: (i, 0))] + _stream_specs(x_parts, OUTPROJ_TM)
                  + [_layer_spec(w_out, l), pl.BlockSpec((1, D_MODEL), lambda i: (0, 0))]),
        out_specs=out_specs,
        out_shape=out_parts,
        scratch_shapes=[pltpu.VMEM((MIX_WIDTH, D_MODEL), BF16)],
        compiler_params=pltpu.CompilerParams(
            dimension_semantics=("arbitrary",), vmem_limit_bytes=VMEM_LIMIT_BYTES),
        name="outproj",
    )(mixed, *x_parts, w_out, final_w)


def _silu(x):
    return x * jax.nn.sigmoid(x)


def _log_sigmoid(x):
    return jnp.minimum(x, 0.0) - jnp.log1p(jnp.exp(-jnp.abs(x)))


def _softplus(x):
    return jnp.maximum(x, 0.0) + jnp.log1p(jnp.exp(-jnp.abs(x)))


def _sublane_pos(shape):
    return lax.broadcasted_iota(jnp.int32, shape, 0) & (SUBLANES - 1)


def _tile_cumsum(x):
    pos = _sublane_pos(x.shape)
    s = 1
    while s < SUBLANES:
        x = x + jnp.where(pos >= s, pltpu.roll(x, s, 0), 0.0)
        s *= 2
    return x


def _tile_cummax(x):
    pos = _sublane_pos(x.shape)
    s = 1
    while s < SUBLANES:
        x = jnp.maximum(x, jnp.where(pos >= s, pltpu.roll(x, s, 0), -jnp.inf))
        s *= 2
    return x


def _tile_scan_real(a, b):
    pos = _sublane_pos(a.shape)
    s = 1
    while s < SUBLANES:
        m = pos >= s
        b = jnp.where(m, a * pltpu.roll(b, s, 0) + b, b)
        if 2 * s < SUBLANES:
            a = jnp.where(m, a * pltpu.roll(a, s, 0), a)
        s *= 2
    return b


def _tile_scan_cplx(sr, si, pr, pi):
    pos = _sublane_pos(sr.shape)
    s = 1
    while s < SUBLANES:
        m = pos >= s
        sr_sh = pltpu.roll(sr, s, 0)
        si_sh = pltpu.roll(si, s, 0)
        sr, si = (jnp.where(m, sr + (pr * sr_sh - pi * si_sh), sr),
                  jnp.where(m, si + (pr * si_sh + pi * sr_sh), si))
        if 2 * s < SUBLANES:
            pr, pi = pr * pr - pi * pi, 2.0 * (pr * pi)
        s *= 2
    return sr, si


def _mlstm_project(xc, xm, vec_ref, wqk_ref, wvo_ref, wif_ref, q_scr, k_scr, v_scr, o_scr):
    xc_b = xc.astype(BF16)
    xm_b = xm.astype(BF16)
    for h in range(M_HEADS):
        sl = slice(h * M_HEAD_DIM, (h + 1) * M_HEAD_DIM)
        qk = jnp.dot(xc_b[:, sl], wqk_ref[h], preferred_element_type=F32)
        vo = jnp.dot(xm_b[:, sl], wvo_ref[h], preferred_element_type=F32)
        q_scr[:, sl] = qk[:, :M_HEAD_DIM]
        k_scr[:, sl] = qk[:, M_HEAD_DIM:]
        v_scr[:, sl] = vo[:, :M_HEAD_DIM]
        o_scr[:, sl] = vo[:, M_HEAD_DIM:]
    return (jnp.dot(q_scr[...].astype(BF16), wif_ref[0:M_WIDTH, :], preferred_element_type=F32)
            + jnp.dot(k_scr[...].astype(BF16), wif_ref[M_WIDTH:2 * M_WIDTH, :], preferred_element_type=F32)
            + jnp.dot(v_scr[...].astype(BF16), wif_ref[2 * M_WIDTH:3 * M_WIDTH, :], preferred_element_type=F32)
            + _vec(vec_ref, "bif"))


def _head_output(hh, o_pre, xc_h, z_h, mnw_h, mskip_h):
    mu = jnp.mean(hh, axis=1, keepdims=True)
    hc = hh - mu
    var = jnp.mean(hc * hc, axis=1, keepdims=True)
    hn = hc * lax.rsqrt(var + EPS) * mnw_h
    return (jax.nn.sigmoid(o_pre) * hn + mskip_h * xc_h) * _silu(z_h)


def _rglru_coeffs(xcr, vec_ref, rwa_ref, rwx_ref):
    xcr_b = xcr.astype(BF16)
    half = R_WIDTH // 2
    ra_pre = jnp.concatenate(
        [jnp.dot(xcr_b[:, :half], rwa_ref[0], preferred_element_type=F32),
         jnp.dot(xcr_b[:, half:], rwa_ref[1], preferred_element_type=F32)], axis=1) + _vec(vec_ref, "rba")
    rx_pre = jnp.concatenate(
        [jnp.dot(xcr_b[:, :half], rwx_ref[0], preferred_element_type=F32),
         jnp.dot(xcr_b[:, half:], rwx_ref[1], preferred_element_type=F32)], axis=1) + _vec(vec_ref, "rbx")
    log_a = (-RG_C) * jax.nn.sigmoid(ra_pre) * _softplus(-_vec(vec_ref, "rlam"))
    a = jnp.exp(log_a)
    th = jnp.tanh(log_a)
    one_minus_a2 = (-2.0 * th) / (1.0 - th)
    return a, jnp.sqrt(one_minus_a2) * (jax.nn.sigmoid(rx_pre) * xcr)


def _s5_project(u_b, wbu_ref, re_ref, im_ref):
    per_slice = (LANES // S_GROUP) * S_STATE
    for k in range(S_WIDTH // LANES):
        res = jnp.dot(u_b[:, k * LANES:(k + 1) * LANES], wbu_ref[k], preferred_element_type=F32)
        re_ref[:, k * per_slice:(k + 1) * per_slice] = res[:, :per_slice]
        im_ref[:, k * per_slice:(k + 1) * per_slice] = res[:, per_slice:]


def _s5_output(re_ref, im_ref, u, vec_ref, wcre_ref, wcim_ref, wglu_ref):
    nblk = wcre_ref.shape[0]
    k_blk = S_LANES // nblk
    parts = []
    for m in range(nblk):
        ks = slice(m * k_blk, (m + 1) * k_blk)
        parts.append(lax.dot_general(re_ref[:, ks].astype(BF16), wcre_ref[m], _NT, preferred_element_type=F32)
                     - lax.dot_general(im_ref[:, ks].astype(BF16), wcim_ref[m], _NT, preferred_element_type=F32))
    y = jnp.concatenate(parts, axis=1) + _vec(vec_ref, "sd") * u
    g = jax.nn.gelu(y)
    return g * jax.nn.sigmoid(jnp.dot(g.astype(BF16), wglu_ref[...], preferred_element_type=F32)
                              + _vec(vec_ref, "bglu"))


_MATRIX_KEYS = ("wqk", "wvo", "wif", "rwa", "rwx", "wbu", "wcre", "wcim", "wglu")


def _tile(x, i):
    return x[i * SUBLANES:(i + 1) * SUBLANES]


def _conv_interleaved(x, tail_ref, w, bias, rows):
    ntiles = rows // SUBLANES
    pos = lax.broadcasted_iota(jnp.int32, (SUBLANES, x.shape[1]), 0)
    prev = tail_ref[...]
    before = [pltpu.roll(jnp.where(pos == SUBLANES - 1, _tile(prev, CONV_W - 1 - d), _tile(x, ntiles - d)), 1, 0)
              for d in range(1, CONV_W)]
    out = w[CONV_W - 1:CONV_W, :] * x + bias
    for j in range(1, CONV_W):
        shifted = jnp.concatenate(before[:j][::-1] + [x[:rows - j * SUBLANES]], axis=0)
        out = out + w[CONV_W - 1 - j:CONV_W - j, :] * shifted
    tail_ref[...] = x[rows - (CONV_W - 1) * SUBLANES:]
    return out


def _cumsum_interleaved(x, rows):
    tiles = [_tile(x, 0)]
    for i in range(1, rows // SUBLANES):
        tiles.append(tiles[-1] + _tile(x, i))
    total = tiles[-1]
    start = _tile_cumsum(total) - total
    return jnp.concatenate([t + start for t in tiles], axis=0)


def _cummax_interleaved(x, rows):
    tiles = [_tile(x, 0)]
    for i in range(1, rows // SUBLANES):
        tiles.append(jnp.maximum(tiles[-1], _tile(x, i)))
    best = tiles[-1]
    pos = lax.broadcasted_iota(jnp.int32, best.shape, 0)
    s = 1
    while s < SUBLANES:
        best = jnp.maximum(best, jnp.where(pos >= s, pltpu.roll(best, s, 0), -jnp.inf))
        s *= 2
    start = jnp.where(pos == 0, -jnp.inf, pltpu.roll(best, 1, 0))
    return jnp.concatenate([jnp.maximum(t, start) for t in tiles], axis=0)


def _scan_real_interleaved(a_ref, b_ref, carry, rows):
    ntiles = rows // SUBLANES
    pos = lax.broadcasted_iota(jnp.int32, (SUBLANES, a_ref.shape[1]), 0)
    h = _tile(b_ref, 0)
    aprod = _tile(a_ref, 0)
    for i in range(1, ntiles):
        a = _tile(a_ref, i)
        h = a * h + _tile(b_ref, i)
        aprod = aprod * a
    g = _tile_scan_real(aprod, h + jnp.where(pos == 0, aprod * carry, 0.0))
    h = jnp.where(pos == 0, carry, pltpu.roll(g, 1, 0))
    for i in range(ntiles):
        h = _tile(a_ref, i) * h + _tile(b_ref, i)
        b_ref[i * SUBLANES:(i + 1) * SUBLANES, :] = h
    return g[SUBLANES - 1:SUBLANES]


def _scan_cplx_interleaved(re_ref, im_ref, p_re, p_im, cre_ref, cim_ref, rows):
    ntiles = rows // SUBLANES
    width = re_ref.shape[1]
    pos = lax.broadcasted_iota(jnp.int32, (SUBLANES, SCAN_LANE_BLOCK), 0)
    for blk in range(width // SCAN_LANE_BLOCK):
        sl = slice(blk * SCAN_LANE_BLOCK, (blk + 1) * SCAN_LANE_BLOCK)
        pr = jnp.broadcast_to(p_re[:, sl], pos.shape)
        pi = jnp.broadcast_to(p_im[:, sl], pos.shape)
        c_r = cre_ref[:, sl]
        c_i = cim_ref[:, sl]
        sr = re_ref[0:SUBLANES, sl]
        si = im_ref[0:SUBLANES, sl]
        for i in range(1, ntiles):
            rs = slice(i * SUBLANES, (i + 1) * SUBLANES)
            sr, si = pr * sr - pi * si + re_ref[rs, sl], pr * si + pi * sr + im_ref[rs, sl]
        qr, qi = pr, pi
        n = 1
        while n < ntiles:
            qr, qi = qr * qr - qi * qi, 2.0 * (qr * qi)
            n *= 2
        gr, gi = _tile_scan_cplx(sr + jnp.where(pos == 0, qr * c_r - qi * c_i, 0.0),
                                 si + jnp.where(pos == 0, qr * c_i + qi * c_r, 0.0), qr, qi)
        sr = jnp.where(pos == 0, c_r, pltpu.roll(gr, 1, 0))
        si = jnp.where(pos == 0, c_i, pltpu.roll(gi, 1, 0))
        for i in range(ntiles):
            rs = slice(i * SUBLANES, (i + 1) * SUBLANES)
            sr, si = pr * sr - pi * si + re_ref[rs, sl], pr * si + pi * sr + im_ref[rs, sl]
            re_ref[rs, sl] = sr
            im_ref[rs, sl] = si
        cre_ref[:, sl] = gr[SUBLANES - 1:SUBLANES]
        cim_ref[:, sl] = gi[SUBLANES - 1:SUBLANES]


def _prompt_mixer_kernel(*refs, rows, nchunk, n_alias):
    (xnext_ref, xfirst_ref, vec_ref, win_ref,
     wqk_ref, wvo_ref, wif_ref, rwa_ref, rwx_ref, wbu_ref, wcre_ref, wcim_ref, wglu_ref) = refs[:13]
    (mixed_ref, c_ref, n_ref, m_ref, mtail_ref, h_ref, rtail_ref, sre_ref, sim_ref,
     q_scr, k_scr, v_scr, o_scr, ra_scr, rb_scr, ure_scr, uim_scr, proj_scr, xn_scr) = refs[13 + n_alias:]
    T = rows
    step = pl.program_id(0)
    slot = lax.rem(step, 2)
    proj_ref = proj_scr.at[slot]
    next_proj_ref = proj_scr.at[1 - slot]
    slab = (2 * MIX_WIDTH) // M_HEADS

    @pl.when(step == 0)
    def _():
        xn = _rmsnorm_bf16(xfirst_ref[...], vec_ref)
        for j in range(M_HEADS):
            cols = slice(j * slab, (j + 1) * slab)
            proj_scr[0, :, cols] = jnp.dot(xn, win_ref[:, cols], preferred_element_type=F32)

    @pl.when(lax.rem(step, nchunk) == 0)
    def _():
        for ref in (c_ref, n_ref, m_ref, mtail_ref, h_ref, rtail_ref, sre_ref, sim_ref):
            ref[...] = jnp.zeros(ref.shape, F32)

    xn_scr[...] = _rmsnorm_bf16(xnext_ref[...], vec_ref)

    xm = proj_ref[:, 0:M_WIDTH]
    xc = _silu(_conv_interleaved(xm, mtail_ref, _vec(vec_ref, "mcw"), _vec(vec_ref, "mcb"), T))
    gates = _mlstm_project(xc, xm, vec_ref, wqk_ref, wvo_ref, wif_ref, q_scr, k_scr, v_scr, o_scr)
    log_i = gates[:, :LANES]
    b_all = _cumsum_interleaved(_log_sigmoid(gates[:, LANES:]), T)
    a_max = _cummax_interleaved(log_i - b_all, T)
    m_prev = m_ref[...]
    log_inter = b_all + m_prev
    m_t_all = jnp.maximum(log_inter, b_all + a_max)
    w_inter_all = jnp.exp(log_inter - m_t_all)
    floor_all = jnp.exp(-m_t_all)
    b_end = b_all[T - 1:T, :]
    m_new = jnp.maximum(b_end + m_prev, b_end + a_max[T - 1:T, :])
    w_src_all = jnp.exp(b_end - b_all + log_i - m_new)
    decay_all = jnp.exp(b_end + m_prev - m_new)
    m_ref[...] = m_new
    b_t = b_all.T
    li_t = log_i.T

    rowi = lax.broadcasted_iota(jnp.int32, (T, T), 0)
    coli = lax.broadcasted_iota(jnp.int32, (T, T), 1)
    sub_len = T // SUBLANES

    def time_of(r):
        return (r & (SUBLANES - 1)) * sub_len + (r >> SUBLANE_SHIFT)

    causal = time_of(coli) <= time_of(rowi)
    k_scale = M_HEAD_DIM ** -0.5
    mnw = _vec(vec_ref, "mnw")
    mskip = _vec(vec_ref, "mskip")

    for h in range(M_HEADS):
        sl = slice(h * M_HEAD_DIM, (h + 1) * M_HEAD_DIM)
        c_prev = c_ref[h]
        n_prev = n_ref[h:h + 1, :]
        w_inter = w_inter_all[:, h:h + 1]
        w_src = w_src_all[:, h:h + 1]
        decay = decay_all[:, h:h + 1]

        q = q_scr[:, sl]
        ks = k_scr[:, sl] * k_scale
        v = v_scr[:, sl]
        q_b = q.astype(BF16)
        ks_b = ks.astype(BF16)

        log_d = jnp.where(causal, b_all[:, h:h + 1] - b_t[h:h + 1, :] + li_t[h:h + 1, :], -jnp.inf)
        w_intra = jnp.exp(log_d - m_t_all[:, h:h + 1])
        s = lax.dot_general(q_b, ks_b, _NT, preferred_element_type=F32) * w_intra
        inter = lax.dot_general(q_b, c_prev.astype(BF16), _NT, preferred_element_type=F32)
        num = jnp.dot(s.astype(BF16), v.astype(BF16), preferred_element_type=F32) + w_inter * inter
        den = (jnp.sum(s, axis=1, keepdims=True)
               + w_inter * jnp.sum(q * n_prev, axis=1, keepdims=True))
        hh = num / jnp.maximum(jnp.abs(den), floor_all[:, h:h + 1])

        vw_t = (v * w_src).T.astype(BF16)
        c_ref[h] = decay * c_prev + jnp.dot(vw_t, ks_b, preferred_element_type=F32)
        n_ref[h:h + 1, :] = decay * n_prev + jnp.sum(ks * w_src, axis=0, keepdims=True)

        z = proj_ref[:, MIX_WIDTH + h * M_HEAD_DIM:MIX_WIDTH + (h + 1) * M_HEAD_DIM]
        mixed_ref[:, sl] = _head_output(hh, o_scr[:, sl], xc[:, sl], z, mnw[:, sl], mskip[:, sl]).astype(BF16)

        cols = slice(h * slab, (h + 1) * slab)
        next_proj_ref[:, cols] = jnp.dot(xn_scr[...], win_ref[:, cols], preferred_element_type=F32)

    xr = proj_ref[:, M_WIDTH:M_WIDTH + R_WIDTH]
    xcr = _conv_interleaved(xr, rtail_ref, _vec(vec_ref, "rcw"), _vec(vec_ref, "rcb"), T)
    a, bb = _rglru_coeffs(xcr, vec_ref, rwa_ref, rwx_ref)
    ra_scr[...] = a
    rb_scr[...] = bb
    h_ref[...] = _scan_real_interleaved(ra_scr, rb_scr, h_ref[...], T)
    zr = proj_ref[:, MIX_WIDTH + M_WIDTH:MIX_WIDTH + M_WIDTH + R_WIDTH]
    mixed_ref[:, M_WIDTH:M_WIDTH + R_WIDTH] = (rb_scr[...] * _silu(zr)).astype(BF16)

    u = proj_ref[:, M_WIDTH + R_WIDTH:MIX_WIDTH]
    _s5_project(u.astype(BF16), wbu_ref, ure_scr, uim_scr)
    _scan_cplx_interleaved(ure_scr, uim_scr, _vec(vec_ref, "abre"), _vec(vec_ref, "abim"), sre_ref, sim_ref, T)
    glu = _s5_output(ure_scr, uim_scr, u, vec_ref, wcre_ref, wcim_ref, wglu_ref)
    zs = proj_ref[:, MIX_WIDTH + M_WIDTH + R_WIDTH:2 * MIX_WIDTH]
    mixed_ref[:, M_WIDTH + R_WIDTH:MIX_WIDTH] = (glu * _silu(zs)).astype(BF16)


_PROMPT_STATE_SHAPES = ((M_HEADS, M_HEAD_DIM, M_HEAD_DIM), (M_HEADS, M_HEAD_DIM), (1, LANES),
                        ((CONV_W - 1) * SUBLANES, M_WIDTH), (1, R_WIDTH), ((CONV_W - 1) * SUBLANES, R_WIDTH),
                        (1, S_LANES), (1, S_LANES))


def _prompt_mixer(x, total_rows, l, bsz, seq, vec, w_in_b, mats, prev_states):
    rows = PROMPT_CHUNK
    nchunk = seq // rows
    nstep = bsz * nchunk
    aliases = {}
    alias_args = []
    n_in = 4 + len(mats)
    if prev_states is not None:
        alias_args = list(prev_states)
        aliases = {n_in + k: 1 + k for k in range(len(alias_args))}

    def state_spec(shape):
        nd = len(shape)
        return pl.BlockSpec((None, None) + shape, lambda t: (l, t // nchunk) + (0,) * nd)

    out_shape = ([jax.ShapeDtypeStruct((total_rows, MIX_WIDTH), BF16)]
                 + [jax.ShapeDtypeStruct((DEPTH, bsz) + s, F32) for s in _PROMPT_STATE_SHAPES])
    scratch = ([pltpu.VMEM((rows, w), F32)
                for w in (M_WIDTH, M_WIDTH, M_WIDTH, M_WIDTH, R_WIDTH, R_WIDTH, S_LANES, S_LANES)]
               + [pltpu.VMEM((2, rows, 2 * MIX_WIDTH), F32), pltpu.VMEM((rows, D_MODEL), BF16)])
    return pl.pallas_call(
        functools.partial(_prompt_mixer_kernel, rows=rows, nchunk=nchunk, n_alias=len(alias_args)),
        grid=(nstep,),
        in_specs=([pl.BlockSpec((rows, D_MODEL), lambda t: (jnp.minimum(t + 1, nstep - 1), 0)),
                   pl.BlockSpec((rows, D_MODEL), lambda t: (0, 0), pipeline_mode=pl.Buffered(1)),
                   _layer_spec(vec, l),
                   pl.BlockSpec(w_in_b.shape, lambda t: (0, 0), pipeline_mode=pl.Buffered(1))]
                  + [_layer_spec(w, l) for w in mats] + [_ANY_SPEC] * len(alias_args)),
        out_specs=([pl.BlockSpec((rows, MIX_WIDTH), lambda t: (t, 0))]
                   + [state_spec(s) for s in _PROMPT_STATE_SHAPES]),
        out_shape=out_shape,
        scratch_shapes=scratch,
        input_output_aliases=aliases,
        compiler_params=pltpu.CompilerParams(
            dimension_semantics=("arbitrary",), vmem_limit_bytes=VMEM_LIMIT_BYTES),
        name="prompt_mixer",
    )(x, x, vec, w_in_b, *mats, *alias_args)


def _seg_last(x, groups):
    x3 = x.reshape(groups, SUBLANES, x.shape[-1])
    return jnp.broadcast_to(x3[:, SUBLANES - 1:SUBLANES, :], x3.shape).reshape(x.shape)


def _seg_max(x, groups):
    x3 = x.reshape(groups, SUBLANES, x.shape[-1])
    return jnp.broadcast_to(jnp.max(x3, axis=1, keepdims=True), x3.shape).reshape(x.shape)


def _seg_sum(x, groups):
    x3 = x.reshape(groups, SUBLANES, x.shape[-1])
    return jnp.broadcast_to(jnp.sum(x3, axis=1, keepdims=True), x3.shape).reshape(x.shape)


def _seg_rows(state_ref, lanes=slice(None)):
    x = state_ref[:, :, lanes]
    return jnp.broadcast_to(x, (x.shape[0], SUBLANES, x.shape[2])).reshape(x.shape[0] * SUBLANES, x.shape[2])


def _group_state_rows(state_ref, g0, ng):
    x = jnp.concatenate([state_ref[:, g:g + 1, :] for g in range(g0, g0 + ng)], axis=-1)
    return jnp.broadcast_to(x, (x.shape[0], SUBLANES, x.shape[2])).reshape(x.shape[0] * SUBLANES, x.shape[2])


def _store_group_state(state_ref, g0, x):
    for j in range(x.shape[-1] // S_STATE):
        state_ref[:, g0 + j:g0 + j + 1, :] = x[:, :, j * S_STATE:(j + 1) * S_STATE]


def _seg_state(x):
    x3 = x.reshape(x.shape[0] // SUBLANES, SUBLANES, x.shape[-1])
    return x3[:, SUBLANES - 1:SUBLANES, :]


def _with_history(x, buf_ref, new_buf_ref):
    g = x.shape[0] // SUBLANES
    x3 = x.reshape(g, SUBLANES, x.shape[1])
    lead = SAMPLE_LEAD - (CONV_W - 1)
    full = jnp.concatenate([jnp.zeros((g, lead, x.shape[1]), F32), buf_ref[...], x3[:, SAMPLE_LEAD:, :]], axis=1)
    new_buf_ref[...] = full[:, SUBLANES - (CONV_W - 1):, :]
    return full.reshape(x.shape)


def _conv_rolled(xf, w, bias):
    out = w[CONV_W - 1:CONV_W, :] * xf + bias
    for j in range(1, CONV_W):
        out = out + w[CONV_W - 1 - j:CONV_W - j, :] * pltpu.roll(xf, j, 0)
    return out


_SAMPLE_STATE_SHAPES = ((1, M_WIDTH), (1, LANES), (CONV_W - 1, M_WIDTH), (1, R_WIDTH), (CONV_W - 1, R_WIDTH),
                        (S_GROUPS, S_STATE), (S_GROUPS, S_STATE))


def _sample_mixer_kernel(*refs, rows, c_seqs, n_alias):
    (proj_ref, c0_ref, n0_ref, m0_ref, mtail0_ref, h0_ref, rtail0_ref, sre0_ref, sim0_ref,
     vec_ref, wqk_ref, wvo_ref, wif_ref, rwa_ref, rwx_ref, wbu_ref, wcre_ref, wcim_ref, wglu_ref) = refs[:19]
    (mixed_ref, c_ref, n_ref, m_ref, mconv_ref, h_ref, rconv_ref, sre_ref, sim_ref,
     q_scr, k_scr, v_scr, o_scr, xc_scr, intert_scr, vwt_scr, qt_scr, ksb_scr, dec_scr, b_scr, li_scr,
     ure_scr, uim_scr) = refs[19 + n_alias:]
    R = rows
    G = R // SUBLANES
    sub = pl.program_id(1)
    k_scale = M_HEAD_DIM ** -0.5
    pos = _sublane_pos((R, 1))
    valid = pos >= SAMPLE_LEAD
    first_token = pos == SAMPLE_LEAD

    @pl.when(sub == 0)
    def _():
        xm = _with_history(proj_ref[:, 0:M_WIDTH], mtail0_ref, mconv_ref)
        xc = _silu(_conv_rolled(xm, _vec(vec_ref, "mcw"), _vec(vec_ref, "mcb")))
        xc_scr[...] = xc
        gates = _mlstm_project(xc, xm, vec_ref, wqk_ref, wvo_ref, wif_ref, q_scr, k_scr, v_scr, o_scr)
        log_i = jnp.where(valid, gates[:, :LANES], -jnp.inf)
        log_f = jnp.where(valid, _log_sigmoid(gates[:, LANES:]), 0.0)
        b_all = _tile_cumsum(log_f)
        b_end = _seg_last(b_all, G)
        m_rows = _seg_rows(m0_ref)
        n_rows = _seg_rows(n0_ref)
        log_src = b_end - b_all + log_i
        m_new = jnp.maximum(b_end + m_rows, _seg_max(log_src, G))
        w_src_all = jnp.exp(log_src - m_new)
        decay_all = jnp.exp(b_end + m_rows - m_new)
        dec_scr[...] = decay_all
        b_scr[...] = b_all
        li_scr[...] = log_i
        m_ref[...] = _seg_state(m_new)
        for h in range(M_HEADS):
            sl = slice(h * M_HEAD_DIM, (h + 1) * M_HEAD_DIM)
            ks = k_scr[:, sl] * k_scale
            w_src = w_src_all[:, h:h + 1]
            n_ref[:, :, sl] = _seg_state(decay_all[:, h:h + 1] * n_rows[:, sl] + _seg_sum(ks * w_src, G))
            vwt_scr[h] = (v_scr[:, sl] * w_src).T
            qt_scr[h] = q_scr[:, sl].T.astype(BF16)
            ksb_scr[h] = ks.astype(BF16)
            intert_scr[h] = jnp.zeros((M_HEAD_DIM, R), F32)

        xr = _with_history(proj_ref[:, M_WIDTH:M_WIDTH + R_WIDTH], rtail0_ref, rconv_ref)
        xcr = _conv_rolled(xr, _vec(vec_ref, "rcw"), _vec(vec_ref, "rcb"))
        a, bb = _rglru_coeffs(xcr, vec_ref, rwa_ref, rwx_ref)
        hs = _tile_scan_real(a, jnp.where(valid, bb, 0.0) + jnp.where(first_token, a * _seg_rows(h0_ref), 0.0))
        h_ref[...] = _seg_state(hs)
        zr = proj_ref[:, MIX_WIDTH + M_WIDTH:MIX_WIDTH + M_WIDTH + R_WIDTH]
        mixed_ref[:, M_WIDTH:M_WIDTH + R_WIDTH] = jnp.where(valid, hs * _silu(zr), 0.0).astype(BF16)

        u = proj_ref[:, M_WIDTH + R_WIDTH:MIX_WIDTH]
        _s5_project(u.astype(BF16), wbu_ref, ure_scr, uim_scr)
        p_re = _vec(vec_ref, "abre")
        p_im = _vec(vec_ref, "abim")
        groups_per_blk = SCAN_LANE_BLOCK // S_STATE
        for blk in range(S_LANES // SCAN_LANE_BLOCK):
            sl = slice(blk * SCAN_LANE_BLOCK, (blk + 1) * SCAN_LANE_BLOCK)
            pr = jnp.broadcast_to(p_re[:, sl], (R, SCAN_LANE_BLOCK))
            pi = jnp.broadcast_to(p_im[:, sl], (R, SCAN_LANE_BLOCK))
            g0 = blk * groups_per_blk
            s0r = _group_state_rows(sre0_ref, g0, groups_per_blk)
            s0i = _group_state_rows(sim0_ref, g0, groups_per_blk)
            sr, si = _tile_scan_cplx(
                jnp.where(valid, ure_scr[:, sl], 0.0) + jnp.where(first_token, pr * s0r - pi * s0i, 0.0),
                jnp.where(valid, uim_scr[:, sl], 0.0) + jnp.where(first_token, pr * s0i + pi * s0r, 0.0), pr, pi)
            ure_scr[:, sl] = sr
            uim_scr[:, sl] = si
            _store_group_state(sre_ref, g0, _seg_state(sr))
            _store_group_state(sim_ref, g0, _seg_state(si))
        glu = _s5_output(ure_scr, uim_scr, u, vec_ref, wcre_ref, wcim_ref, wglu_ref)
        zs = proj_ref[:, MIX_WIDTH + M_WIDTH + R_WIDTH:2 * MIX_WIDTH]
        mixed_ref[:, M_WIDTH + R_WIDTH:MIX_WIDTH] = jnp.where(valid, glu * _silu(zs), 0.0).astype(BF16)

    lane_seq = lax.broadcasted_iota(jnp.int32, (M_HEAD_DIM, R), 1) >> SUBLANE_SHIFT
    seq0 = sub * c_seqs
    for h in range(M_HEADS):
        c_old = c0_ref[:, h].reshape(c_seqs * M_HEAD_DIM, M_HEAD_DIM)
        readout = jnp.dot(c_old.astype(BF16), qt_scr[h], preferred_element_type=F32)
        acc = intert_scr[h]
        vwt = vwt_scr[h]
        lhs = []
        for s in range(c_seqs):
            own = lane_seq == seq0 + s
            acc = jnp.where(own, readout[s * M_HEAD_DIM:(s + 1) * M_HEAD_DIM], acc)
            lhs.append(jnp.where(own, vwt, 0.0))
        intert_scr[h] = acc
        upd = jnp.dot(jnp.concatenate(lhs, axis=0).astype(BF16), ksb_scr[h], preferred_element_type=F32)
        for s in range(c_seqs):
            r0 = pl.multiple_of((seq0 + s) * SUBLANES, SUBLANES)
            c_ref[s, h] = (dec_scr[pl.ds(r0, 1), h:h + 1] * c0_ref[s, h]
                           + upd[s * M_HEAD_DIM:(s + 1) * M_HEAD_DIM])

    @pl.when(sub == pl.num_programs(1) - 1)
    def _():
        _sample_heads(proj_ref, m0_ref, n0_ref, vec_ref, mixed_ref,
                      q_scr, k_scr, v_scr, o_scr, xc_scr, intert_scr, b_scr, li_scr, valid, R)


def _sample_heads(proj_ref, m0_ref, n0_ref, vec_ref, mixed_ref,
                  q_scr, k_scr, v_scr, o_scr, xc_scr, intert_scr, b_scr, li_scr, valid, R):
    rowi = lax.broadcasted_iota(jnp.int32, (R, R), 0)
    coli = lax.broadcasted_iota(jnp.int32, (R, R), 1)
    same_causal = jnp.logical_and(coli <= rowi, (coli >> SUBLANE_SHIFT) == (rowi >> SUBLANE_SHIFT))
    k_scale = M_HEAD_DIM ** -0.5
    mnw = _vec(vec_ref, "mnw")
    mskip = _vec(vec_ref, "mskip")
    n_rows = _seg_rows(n0_ref)
    b_all = b_scr[...]
    log_i = li_scr[...]
    log_inter = b_all + _seg_rows(m0_ref)
    m_t_all = jnp.maximum(log_inter, b_all + _tile_cummax(log_i - b_all))
    w_inter_all = jnp.exp(log_inter - m_t_all)
    floor_all = jnp.exp(-m_t_all)
    b_t = b_all.T
    li_t = log_i.T

    for h in range(M_HEADS):
        sl = slice(h * M_HEAD_DIM, (h + 1) * M_HEAD_DIM)
        q = q_scr[:, sl]
        ks = k_scr[:, sl] * k_scale
        v = v_scr[:, sl]
        q_b = q.astype(BF16)
        ks_b = ks.astype(BF16)
        w_inter = w_inter_all[:, h:h + 1]

        log_d = jnp.where(same_causal, b_all[:, h:h + 1] - b_t[h:h + 1, :] + li_t[h:h + 1, :], -jnp.inf)
        w_intra = jnp.exp(log_d - m_t_all[:, h:h + 1])
        s = lax.dot_general(q_b, ks_b, _NT, preferred_element_type=F32) * w_intra
        num = (jnp.dot(s.astype(BF16), v.astype(BF16), preferred_element_type=F32)
               + w_inter * intert_scr[h].T)
        den = (jnp.sum(s, axis=1, keepdims=True)
               + w_inter * jnp.sum(q * n_rows[:, sl], axis=1, keepdims=True))
        hh = num / jnp.maximum(jnp.abs(den), floor_all[:, h:h + 1])

        z = proj_ref[:, MIX_WIDTH + h * M_HEAD_DIM:MIX_WIDTH + (h + 1) * M_HEAD_DIM]
        out = _head_output(hh, o_scr[:, sl], xc_scr[:, sl], z, mnw[:, sl], mskip[:, sl])
        mixed_ref[:, sl] = jnp.where(valid, out, 0.0).astype(BF16)


def _sample_mixer(proj, l, row0, mixed_prev, c_all, row_state, vec, mats, prev_out):
    R = SAMPLE_ROWS
    nrows = proj.shape[0]
    nsub = (R // SUBLANES) // SAMPLE_C_SEQS
    blk0 = row0 // R
    c_spec = pl.BlockSpec((None, SAMPLE_C_SEQS, M_HEADS, M_HEAD_DIM, M_HEAD_DIM),
                          lambda i, j: (l, i * nsub + j, 0, 0, 0))

    state_specs = [pl.BlockSpec((None, R // SUBLANES) + s, lambda i, j: (l, i, 0, 0)) for s in _SAMPLE_STATE_SHAPES]
    alias_args = [mixed_prev] + (list(prev_out) if prev_out is not None else [])
    first_alias = 2 + len(row_state) + 1 + len(mats)
    aliases = {first_alias + k: k for k in range(len(alias_args))}
    out_shape = ([jax.ShapeDtypeStruct(mixed_prev.shape, BF16), jax.ShapeDtypeStruct(c_all.shape, F32)]
                 + [jax.ShapeDtypeStruct((DEPTH, nrows // SUBLANES) + s, F32) for s in _SAMPLE_STATE_SHAPES])
    scratch = ([pltpu.VMEM((R, M_WIDTH), F32)] * 5
               + [pltpu.VMEM((M_HEADS, M_HEAD_DIM, R), F32), pltpu.VMEM((M_HEADS, M_HEAD_DIM, R), F32),
                  pltpu.VMEM((M_HEADS, M_HEAD_DIM, R), BF16), pltpu.VMEM((M_HEADS, R, M_HEAD_DIM), BF16)]
               + [pltpu.VMEM((R, LANES), F32)] * 3
               + [pltpu.VMEM((R, S_LANES), F32)] * 2)
    return pl.pallas_call(
        functools.partial(_sample_mixer_kernel, rows=R, c_seqs=SAMPLE_C_SEQS, n_alias=len(alias_args)),
        grid=(nrows // R, nsub),
        in_specs=([pl.BlockSpec((R, 2 * MIX_WIDTH), lambda i, j: (i, 0)), c_spec] + state_specs
                  + [_layer_spec(vec, l)] + [_layer_spec(w, l) for w in mats] + [_ANY_SPEC] * len(alias_args)),
        out_specs=[pl.BlockSpec((R, MIX_WIDTH), lambda i, j: (blk0 + i, 0)), c_spec] + state_specs,
        out_shape=out_shape,
        scratch_shapes=scratch,
        input_output_aliases=aliases,
        compiler_params=pltpu.CompilerParams(
            dimension_semantics=("parallel", "arbitrary"), vmem_limit_bytes=VMEM_LIMIT_BYTES),
        name="sample_mixer",
    )(proj, c_all, *row_state, vec, *mats, *alias_args)


def _prepare_weights(p, s5):
    abre, abim, wbu, wcre, wcim = s5

    def split_gates(g):
        pad = [(0, 0)] * (g.ndim - 1) + [(0, LANES - M_HEADS)]
        return jnp.concatenate([jnp.pad(g[..., :M_HEADS], pad), jnp.pad(g[..., M_HEADS:], pad)], axis=-1)

    table = {"mcw": p["m_conv_w"], "rcw": p["r_conv_w"], "mcb": p["m_conv_b"], "mnw": p["m_norm_w"],
             "mskip": p["m_skip"], "rcb": p["r_conv_b"], "rba": p["r_ba"], "rbx": p["r_bx"], "rlam": p["r_lam"],
             "sd": p["s_d"], "bglu": p["s_b_glu"], "bif": split_gates(p["m_b_if"]), "abre": abre, "abim": abim,
             "norm_w": p["norm_w"]}

    def group_rows(group):
        parts = [table[name].reshape(DEPTH, n, w) for name, n, w in group]
        fill = _VEC_WIDTH - sum(w for _, _, w in group)
        if fill:
            parts.append(jnp.zeros((DEPTH, group[0][1], fill), F32))
        return parts[0] if len(parts) == 1 else jnp.concatenate(parts, axis=-1)

    vec = jnp.concatenate([group_rows(g) for g in _VEC_GROUPS]
                          + [jnp.zeros((DEPTH, _VEC_ROWS - _VEC_USED_ROWS, _VEC_WIDTH), F32)], axis=1)

    def block_diag_halves(w):
        nb = R_BLOCKS // 2
        w5 = w.reshape(DEPTH, 2, nb, w.shape[-2], w.shape[-1])
        eye = jnp.eye(nb, dtype=F32)[:, None, :, None]
        return (w5[:, :, :, :, None, :] * eye).reshape(
            DEPTH, 2, nb * w.shape[-2], nb * w.shape[-1]).astype(BF16)

    mats = {
        "wqk": jnp.concatenate([p["m_wq"], p["m_wk"]], axis=-1).astype(BF16),
        "wvo": jnp.concatenate([p["m_wv"], p["m_wo"]], axis=-1).astype(BF16),
        "wif": split_gates(p["m_w_if"]).astype(BF16),
        "rwa": block_diag_halves(p["r_wa"]), "rwx": block_diag_halves(p["r_wx"]),
        "wbu": wbu, "wcre": wcre, "wcim": wcim,
        "wglu": p["s_w_glu"].astype(BF16),
    }
    return vec, [mats[k] for k in _MATRIX_KEYS]


def _sample_rows_state(n, m, mconv, h, rconv, sre, sim):
    bsz = n.shape[1]

    def per_seq(x):
        return x.reshape(DEPTH, bsz, 1, x.shape[-1])

    m_pad = jnp.pad(m, ((0, 0), (0, 0), (0, LANES - M_HEADS)))
    return (per_seq(n.reshape(DEPTH, bsz, M_WIDTH)), per_seq(m_pad), mconv, per_seq(h), rconv, sre, sim)


def _sample_state_from_rows(n, m, mconv, h, rconv, sre, sim):
    bsz = n.shape[1]
    return (n.reshape(DEPTH, bsz, M_HEADS, M_HEAD_DIM), m.reshape(DEPTH, bsz, LANES)[..., :M_HEADS],
            mconv, h.reshape(DEPTH, bsz, R_WIDTH), rconv, sre, sim)


def _prompt_state_from_kernel(c, n, m, mtail, h, rtail, sre, sim):
    bsz = c.shape[1]
    return (c, n, m[:, :, 0, :M_HEADS], mtail[:, :, SUBLANES - 1::SUBLANES], h.reshape(DEPTH, bsz, R_WIDTH),
            rtail[:, :, SUBLANES - 1::SUBLANES],
            sre.reshape(DEPTH, bsz, S_GROUPS, S_STATE), sim.reshape(DEPTH, bsz, S_GROUPS, S_STATE))


def _interleave_chunks(x, inverse=False):
    bsz, seq, d = x.shape
    sub_len = PROMPT_CHUNK // SUBLANES
    inner = (sub_len, SUBLANES) if inverse else (SUBLANES, sub_len)
    return x.reshape(bsz, seq // PROMPT_CHUNK, *inner, d).transpose(0, 1, 3, 2, 4).reshape(bsz, seq, d)


def kernel(x_prompt, x_sample, state_mlstm_C, state_mlstm_n, state_mlstm_m, state_mlstm_conv, state_rglru_h, state_rglru_conv, state_s5_re, state_s5_im, norm_w, w_in, w_out, m_conv_w, m_conv_b, m_wq, m_wk, m_wv, m_wo, m_w_if, m_b_if, m_norm_w, m_skip, r_conv_w, r_conv_b, r_wa, r_ba, r_wx, r_bx, r_lam, s_lam_re, s_lam_im, s_b_re, s_b_im, s_c_re, s_c_im, s_d, s_log_step, s_w_glu, s_b_glu, final_norm_w):
    p = dict(norm_w=norm_w, w_in=w_in, w_out=w_out, m_conv_w=m_conv_w, m_conv_b=m_conv_b, m_wq=m_wq, m_wk=m_wk,
             m_wv=m_wv, m_wo=m_wo, m_w_if=m_w_if, m_b_if=m_b_if, m_norm_w=m_norm_w, m_skip=m_skip,
             r_conv_w=r_conv_w, r_conv_b=r_conv_b, r_wa=r_wa, r_ba=r_ba, r_wx=r_wx, r_bx=r_bx, r_lam=r_lam,
             s_d=s_d, s_w_glu=s_w_glu, s_b_glu=s_b_glu)
    s5 = _s5_prep(s_lam_re, s_lam_im, s_log_step, s_b_re, s_b_im, s_c_re, s_c_im)
    vec, mats = _prepare_weights(p, s5)
    final_w = final_norm_w.reshape(1, D_MODEL)

    bsz, seq, _ = x_prompt.shape
    dec_batch, dec_seq, _ = x_sample.shape
    n_prompt = bsz * seq
    n_sample = dec_batch * SUBLANES
    stream = [_interleave_chunks(x_prompt).reshape(n_prompt, D_MODEL),
              jnp.pad(x_sample, ((0, 0), (SAMPLE_LEAD, 0), (0, 0))).reshape(n_sample, D_MODEL)]
    rows_state = _sample_rows_state(state_mlstm_n, state_mlstm_m, state_mlstm_conv, state_rglru_h,
                                    state_rglru_conv, state_s5_re, state_s5_im)
    pr_states = None
    sa_states = None
    for l in range(DEPTH):
        last = l == DEPTH - 1
        proj_s, w_in_b = _inproj(stream[-1], stream[-1].shape[0] - n_sample, n_sample, vec, w_in, l)
        mixed, *pr_states = _prompt_mixer(stream[0], n_prompt + n_sample, l, bsz, seq, vec, w_in_b, mats, pr_states)
        mixed, *sa_states = _sample_mixer(proj_s, l, n_prompt, mixed, state_mlstm_C, rows_state, vec, mats, sa_states)
        stream = _outproj(mixed, stream, w_out, l, final_w, last, split_rows=n_prompt if last else None)
    y_prompt = _interleave_chunks(stream[0].reshape(bsz, seq, D_MODEL), inverse=True)
    y_sample = stream[1]
    return (y_prompt, y_sample, *_prompt_state_from_kernel(*pr_states),
            sa_states[0], *_sample_state_from_rows(*sa_states[1:]))
```

```python
import functools

import jax
import jax.numpy as jnp
from jax import lax
from jax.experimental import pallas as pl
from jax.experimental.pallas import tpu as pltpu

F32 = jnp.float32
BF16 = jnp.bfloat16

D_MODEL = 2048
DEPTH = 2
MIX_WIDTH = D_MODEL
M_WIDTH = MIX_WIDTH // 2
R_WIDTH = MIX_WIDTH // 4
S_WIDTH = MIX_WIDTH - M_WIDTH - R_WIDTH
M_HEADS = 8
M_HEAD_DIM = M_WIDTH // M_HEADS
R_BLOCKS = 8
RG_C = 8.0
S_GROUP = 16
S_GROUPS = S_WIDTH // S_GROUP
S_STATE = 64
S_LANES = S_GROUPS * S_STATE
CONV_W = 4
EPS = 1e-6

SUBLANES = 8
SUBLANE_SHIFT = SUBLANES.bit_length() - 1
LANES = 128
VMEM_LIMIT_BYTES = 56 * 1024 * 1024

PROMPT_CHUNK = 256
SAMPLE_ROWS = 128
SAMPLE_C_SEQS = 8
SAMPLE_LEAD = SUBLANES - 4
INPROJ_TN = 1024
OUTPROJ_TM = 512
SCAN_LANE_BLOCK = 256

_NT = (((1,), (1,)), ((), ()))

_VEC_WIDTH = D_MODEL
_VEC_GROUPS = ((("mcw", CONV_W, M_WIDTH), ("rcw", CONV_W, R_WIDTH)),
               (("mcb", 1, M_WIDTH), ("mnw", 1, M_WIDTH)),
               (("mskip", 1, M_WIDTH), ("rcb", 1, R_WIDTH), ("rba", 1, R_WIDTH)),
               (("rbx", 1, R_WIDTH), ("rlam", 1, R_WIDTH), ("sd", 1, S_WIDTH), ("bglu", 1, S_WIDTH)),
               (("bif", 1, 2 * LANES),),
               (("abre", 1, S_LANES),), (("abim", 1, S_LANES),), (("norm_w", 1, D_MODEL),))
_VEC_LAYOUT = {}
_row = 0
for _group in _VEC_GROUPS:
    _lane = 0
    for _name, _n, _w in _group:
        _VEC_LAYOUT[_name] = (_row, _n, _lane, _w)
        _lane += _w
    _row += _group[0][1]
_VEC_USED_ROWS = _row
_VEC_ROWS = -(-_row // SUBLANES) * SUBLANES


def _vec(vec_ref, name):
    r0, n, l0, w = _VEC_LAYOUT[name]
    return vec_ref[r0:r0 + n, l0:l0 + w]


def _layer_spec(arr, l):
    nd = arr.ndim - 1
    return pl.BlockSpec((None,) + arr.shape[1:], lambda *_: (l,) + (0,) * nd, pipeline_mode=pl.Buffered(1))


_ANY_SPEC = pl.BlockSpec(memory_space=pl.ANY)


S5_IN_SLICES = S_WIDTH // LANES
S5_IN_GROUPS = LANES // S_GROUP
S5_OUT_HALVES = 2
S5_OUT_GROUPS = S_GROUPS // S5_OUT_HALVES


def _s5_prep_kernel(lr_ref, li_ref, ls_ref, brt_ref, bit_ref, cre_ref, cim_ref,
                    abre_ref, abim_ref, wbu_ref, wcre_ref, wcim_ref):
    lr = lr_ref[0]
    li = li_ref[0]
    dt = jnp.exp(ls_ref[0])
    mag = jnp.exp(lr * dt)
    ang = li * dt
    ab_re = mag * jnp.cos(ang)
    ab_im = mag * jnp.sin(ang)
    den = lr * lr + li * li
    nr = ab_re - 1.0
    f_re = (nr * lr + ab_im * li) / den
    f_im = (ab_im * lr - nr * li) / den
    br = brt_ref[0]
    bi = bit_ref[0]
    abre_ref[0] = ab_re
    abim_ref[0] = ab_im
    bb_re = f_re * br - f_im * bi
    bb_im = f_re * bi + f_im * br

    per_slice = S5_IN_GROUPS * S_STATE
    lane_group = lax.broadcasted_iota(jnp.int32, (S_GROUP, per_slice), 1) >> (S_STATE.bit_length() - 1)
    for k in range(S5_IN_SLICES):
        re_k = bb_re[:, k * per_slice:(k + 1) * per_slice]
        im_k = bb_im[:, k * per_slice:(k + 1) * per_slice]
        for a in range(S5_IN_GROUPS):
            rows = slice(a * S_GROUP, (a + 1) * S_GROUP)
            wbu_ref[0, k, rows, 0:per_slice] = jnp.where(lane_group == a, re_k, 0.0).astype(BF16)
            wbu_ref[0, k, rows, per_slice:2 * per_slice] = jnp.where(lane_group == a, im_k, 0.0).astype(BF16)

    row_group = lax.broadcasted_iota(jnp.int32, (S5_OUT_GROUPS * S_GROUP, S_STATE), 0) >> (S_GROUP.bit_length() - 1)
    for src_ref, dst_ref in ((cre_ref, wcre_ref), (cim_ref, wcim_ref)):
        for m in range(S5_OUT_HALVES):
            c = src_ref[0, m]
            for a in range(S5_OUT_GROUPS):
                dst_ref[0, m, :, a * S_STATE:(a + 1) * S_STATE] = jnp.where(row_group == a, c, 0.0).astype(BF16)


def _s5_prep(s_lam_re, s_lam_im, s_log_step, s_b_re, s_b_im, s_c_re, s_c_im):
    lr = s_lam_re.reshape(DEPTH, 1, S_LANES)
    li = s_lam_im.reshape(DEPTH, 1, S_LANES)
    ls = jnp.repeat(s_log_step, S_STATE, axis=-1).reshape(DEPTH, 1, S_LANES)
    brt = s_b_re.reshape(DEPTH, S_LANES, S_GROUP).transpose(0, 2, 1)
    bit = s_b_im.reshape(DEPTH, S_LANES, S_GROUP).transpose(0, 2, 1)
    c_shape = (DEPTH, S5_OUT_HALVES, S5_OUT_GROUPS * S_GROUP, S_STATE)
    wbu_shape = (DEPTH, S5_IN_SLICES, LANES, 2 * S5_IN_GROUPS * S_STATE)
    wc_shape = (DEPTH, S5_OUT_HALVES, S5_OUT_GROUPS * S_GROUP, S5_OUT_GROUPS * S_STATE)

    def layer_block(shape):
        nd = len(shape) - 1
        return pl.BlockSpec((1,) + shape[1:], lambda l: (l,) + (0,) * nd)

    vec = layer_block((DEPTH, 1, S_LANES))
    mat = layer_block((DEPTH, S_GROUP, S_LANES))
    return pl.pallas_call(
        _s5_prep_kernel,
        grid=(DEPTH,),
        in_specs=[vec, vec, vec, mat, mat, layer_block(c_shape), layer_block(c_shape)],
        out_specs=[vec, vec, layer_block(wbu_shape), layer_block(wc_shape), layer_block(wc_shape)],
        out_shape=[jax.ShapeDtypeStruct((DEPTH, 1, S_LANES), F32),
                   jax.ShapeDtypeStruct((DEPTH, 1, S_LANES), F32),
                   jax.ShapeDtypeStruct(wbu_shape, BF16),
                   jax.ShapeDtypeStruct(wc_shape, BF16),
                   jax.ShapeDtypeStruct(wc_shape, BF16)],
        name="s5_prep",
    )(lr, li, ls, brt, bit, s_c_re.reshape(c_shape), s_c_im.reshape(c_shape))


def _stream_specs(parts, tm):
    if len(parts) == 1:
        return [pl.BlockSpec((tm, D_MODEL), lambda i: (i, 0))]
    nb0 = parts[0].shape[0] // tm
    return [pl.BlockSpec((tm, D_MODEL), lambda i: (jnp.minimum(i, nb0 - 1), 0)),
            pl.BlockSpec((tm, D_MODEL), lambda i: (jnp.maximum(i - nb0, 0), 0))]


def _read_stream(refs, nb0):
    if len(refs) == 1:
        return refs[0][...]
    return jnp.where(pl.program_id(0) < nb0, refs[0][...], refs[1][...])


def _rmsnorm_bf16(x, vec_ref):
    ms = jnp.mean(x * x, axis=-1, keepdims=True)
    return (x * lax.rsqrt(ms + EPS) * _vec(vec_ref, "norm_w")).astype(BF16)


def _inproj_kernel(x_ref, vec_ref, w_ref, o_ref, wb_ref, xn_scr):
    @pl.when(pl.program_id(0) == 0)
    def _():
        xn_scr[...] = _rmsnorm_bf16(x_ref[...], vec_ref)

    wb = w_ref[...].astype(BF16)
    wb_ref[...] = wb
    o_ref[...] = jnp.dot(xn_scr[...], wb, preferred_element_type=F32)


def _inproj(x, row0, nrows, vec, w_in, l):
    blk0 = row0 // nrows
    return pl.pallas_call(
        _inproj_kernel,
        grid=((2 * MIX_WIDTH) // INPROJ_TN,),
        in_specs=[pl.BlockSpec((nrows, D_MODEL), lambda j: (blk0, 0), pipeline_mode=pl.Buffered(1)),
                  _layer_spec(vec, l),
                  pl.BlockSpec((None, D_MODEL, INPROJ_TN), lambda j: (l, 0, j))],
        out_specs=[pl.BlockSpec((nrows, INPROJ_TN), lambda j: (0, j)),
                   pl.BlockSpec((D_MODEL, INPROJ_TN), lambda j: (0, j))],
        out_shape=[jax.ShapeDtypeStruct((nrows, 2 * MIX_WIDTH), F32),
                   jax.ShapeDtypeStruct((D_MODEL, 2 * MIX_WIDTH), BF16)],
        scratch_shapes=[pltpu.VMEM((nrows, D_MODEL), BF16)],
        compiler_params=pltpu.CompilerParams(
            dimension_semantics=("arbitrary",), vmem_limit_bytes=VMEM_LIMIT_BYTES),
        name="inproj",
    )(x, vec, w_in)


def _outproj_kernel(*refs, n_x, nb0, n_out, nb0_out, final):
    mixed_ref, x_refs = refs[0], refs[1:1 + n_x]
    w_ref, fw_ref = refs[1 + n_x:3 + n_x]
    o_refs, wb_scr = refs[3 + n_x:-1], refs[-1]

    @pl.when(pl.program_id(0) == 0)
    def _():
        wb_scr[...] = w_ref[...].astype(BF16)

    y = _read_stream(x_refs, nb0) + jnp.dot(mixed_ref[...], wb_scr[...], preferred_element_type=F32)
    if final:
        ms = jnp.mean(y * y, axis=-1, keepdims=True)
        y = y * lax.rsqrt(ms + EPS) * fw_ref[...]
    if n_out == 1:
        o_refs[0][...] = y
    else:
        @pl.when(pl.program_id(0) < nb0_out)
        def _():
            o_refs[0][...] = y

        @pl.when(pl.program_id(0) >= nb0_out)
        def _():
            o_refs[1][...] = y.reshape(y.shape[0] // SUBLANES, SUBLANES, y.shape[1])[:, SAMPLE_LEAD:, :]


def _outproj(mixed, x_parts, w_out, l, final_w, final, split_rows=None):
    n = mixed.shape[0]
    if split_rows is None:
        out_parts = [jax.ShapeDtypeStruct((n, D_MODEL), F32)]
        out_specs = [pl.BlockSpec((OUTPROJ_TM, D_MODEL), lambda i: (i, 0))]
        nb0_out = n // OUTPROJ_TM
    else:
        nb0_out = split_rows // OUTPROJ_TM
        tokens = SUBLANES - SAMPLE_LEAD
        out_parts = [jax.ShapeDtypeStruct((split_rows, D_MODEL), F32),
                     jax.ShapeDtypeStruct(((n - split_rows) // SUBLANES, tokens, D_MODEL), F32)]
        out_specs = [pl.BlockSpec((OUTPROJ_TM, D_MODEL), lambda i: (jnp.minimum(i, nb0_out - 1), 0)),
                     pl.BlockSpec((OUTPROJ_TM // SUBLANES, tokens, D_MODEL),
                                  lambda i: (jnp.maximum(i - nb0_out, 0), 0, 0))]
    return pl.pallas_call(
        functools.partial(_outproj_kernel, n_x=len(x_parts), nb0=x_parts[0].shape[0] // OUTPROJ_TM,
                          n_out=len(out_parts), nb0_out=nb0_out, final=final),
        grid=(n // OUTPROJ_TM,),
        in_specs=([pl.BlockSpec((OUTPROJ_TM, MIX_WIDTH), lambda i: (i, 0))] + _stream_specs(x_parts, OUTPROJ_TM)
                  + [_layer_spec(w_out, l), pl.BlockSpec((1, D_MODEL), lambda i: (0, 0))]),
        out_specs=out_specs,
        out_shape=out_parts,
        scratch_shapes=[pltpu.VMEM((MIX_WIDTH, D_MODEL), BF16)],
        compiler_params=pltpu.CompilerParams(
            dimension_semantics=("arbitrary",), vmem_limit_bytes=VMEM_LIMIT_BYTES),
        name="outproj",
    )(mixed, *x_parts, w_out, final_w)


def _silu(x):
    return x * jax.nn.sigmoid(x)


def _log_sigmoid(x):
    return jnp.minimum(x, 0.0) - jnp.log1p(jnp.exp(-jnp.abs(x)))


def _softplus(x):
    return jnp.maximum(x, 0.0) + jnp.log1p(jnp.exp(-jnp.abs(x)))


def _sublane_pos(shape):
    return lax.broadcasted_iota(jnp.int32, shape, 0) & (SUBLANES - 1)


def _tile_cumsum(x):
    pos = _sublane_pos(x.shape)
    s = 1
    while s < SUBLANES:
        x = x + jnp.where(pos >= s, pltpu.roll(x, s, 0), 0.0)
        s *= 2
    return x


def _tile_cummax(x):
    pos = _sublane_pos(x.shape)
    s = 1
    while s < SUBLANES:
        x = jnp.maximum(x, jnp.where(pos >= s, pltpu.roll(x, s, 0), -jnp.inf))
        s *= 2
    return x


def _tile_scan_real(a, b):
    pos = _sublane_pos(a.shape)
    s = 1
    while s < SUBLANES:
        m = pos >= s
        b = jnp.where(m, a * pltpu.roll(b, s, 0) + b, b)
        if 2 * s < SUBLANES:
            a = jnp.where(m, a * pltpu.roll(a, s, 0), a)
        s *= 2
    return b


def _tile_scan_cplx(sr, si, pr, pi):
    pos = _sublane_pos(sr.shape)
    s = 1
    while s < SUBLANES:
        m = pos >= s
        sr_sh = pltpu.roll(sr, s, 0)
        si_sh = pltpu.roll(si, s, 0)
        sr, si = (jnp.where(m, sr + (pr * sr_sh - pi * si_sh), sr),
                  jnp.where(m, si + (pr * si_sh + pi * sr_sh), si))
        if 2 * s < SUBLANES:
            pr, pi = pr * pr - pi * pi, 2.0 * (pr * pi)
        s *= 2
    return sr, si


def _mlstm_project(xc, xm, vec_ref, wqk_ref, wvo_ref, wif_ref, q_scr, k_scr, v_scr, o_scr):
    xc_b = xc.astype(BF16)
    xm_b = xm.astype(BF16)
    for h in range(M_HEADS):
        sl = slice(h * M_HEAD_DIM, (h + 1) * M_HEAD_DIM)
        qk = jnp.dot(xc_b[:, sl], wqk_ref[h], preferred_element_type=F32)
        vo = jnp.dot(xm_b[:, sl], wvo_ref[h], preferred_element_type=F32)
        q_scr[:, sl] = qk[:, :M_HEAD_DIM]
        k_scr[:, sl] = qk[:, M_HEAD_DIM:]
        v_scr[:, sl] = vo[:, :M_HEAD_DIM]
        o_scr[:, sl] = vo[:, M_HEAD_DIM:]
    return (jnp.dot(q_scr[...].astype(BF16), wif_ref[0:M_WIDTH, :], preferred_element_type=F32)
            + jnp.dot(k_scr[...].astype(BF16), wif_ref[M_WIDTH:2 * M_WIDTH, :], preferred_element_type=F32)
            + jnp.dot(v_scr[...].astype(BF16), wif_ref[2 * M_WIDTH:3 * M_WIDTH, :], preferred_element_type=F32)
            + _vec(vec_ref, "bif"))


def _head_output(hh, o_pre, xc_h, z_h, mnw_h, mskip_h):
    mu = jnp.mean(hh, axis=1, keepdims=True)
    hc = hh - mu
    var = jnp.mean(hc * hc, axis=1, keepdims=True)
    hn = hc * lax.rsqrt(var + EPS) * mnw_h
    return (jax.nn.sigmoid(o_pre) * hn + mskip_h * xc_h) * _silu(z_h)


def _rglru_coeffs(xcr, vec_ref, rwa_ref, rwx_ref):
    xcr_b = xcr.astype(BF16)
    half = R_WIDTH // 2
    ra_pre = jnp.concatenate(
        [jnp.dot(xcr_b[:, :half], rwa_ref[0], preferred_element_type=F32),
         jnp.dot(xcr_b[:, half:], rwa_ref[1], preferred_element_type=F32)], axis=1) + _vec(vec_ref, "rba")
    rx_pre = jnp.concatenate(
        [jnp.dot(xcr_b[:, :half], rwx_ref[0], preferred_element_type=F32),
         jnp.dot(xcr_b[:, half:], rwx_ref[1], preferred_element_type=F32)], axis=1) + _vec(vec_ref, "rbx")
    log_a = (-RG_C) * jax.nn.sigmoid(ra_pre) * _softplus(-_vec(vec_ref, "rlam"))
    a = jnp.exp(log_a)
    th = jnp.tanh(log_a)
    one_minus_a2 = (-2.0 * th) / (1.0 - th)
    return a, jnp.sqrt(one_minus_a2) * (jax.nn.sigmoid(rx_pre) * xcr)


def _s5_project(u_b, wbu_ref, re_ref, im_ref):
    per_slice = (LANES // S_GROUP) * S_STATE
    for k in range(S_WIDTH // LANES):
        res = jnp.dot(u_b[:, k * LANES:(k + 1) * LANES], wbu_ref[k], preferred_element_type=F32)
        re_ref[:, k * per_slice:(k + 1) * per_slice] = res[:, :per_slice]
        im_ref[:, k * per_slice:(k + 1) * per_slice] = res[:, per_slice:]


def _s5_output(re_ref, im_ref, u, vec_ref, wcre_ref, wcim_ref, wglu_ref):
    nblk = wcre_ref.shape[0]
    k_blk = S_LANES // nblk
    parts = []
    for m in range(nblk):
        ks = slice(m * k_blk, (m + 1) * k_blk)
        parts.append(lax.dot_general(re_ref[:, ks].astype(BF16), wcre_ref[m], _NT, preferred_element_type=F32)
                     - lax.dot_general(im_ref[:, ks].astype(BF16), wcim_ref[m], _NT, preferred_element_type=F32))
    y = jnp.concatenate(parts, axis=1) + _vec(vec_ref, "sd") * u
    g = jax.nn.gelu(y)
    return g * jax.nn.sigmoid(jnp.dot(g.astype(BF16), wglu_ref[...], preferred_element_type=F32)
                              + _vec(vec_ref, "bglu"))


_MATRIX_KEYS = ("wqk", "wvo", "wif", "rwa", "rwx", "wbu", "wcre", "wcim", "wglu")


def _tile(x, i):
    return x[i * SUBLANES:(i + 1) * SUBLANES]


def _conv_interleaved(x, tail_ref, w, bias, rows):
    ntiles = rows // SUBLANES
    pos = lax.broadcasted_iota(jnp.int32, (SUBLANES, x.shape[1]), 0)
    prev = tail_ref[...]
    before = [pltpu.roll(jnp.where(pos == SUBLANES - 1, _tile(prev, CONV_W - 1 - d), _tile(x, ntiles - d)), 1, 0)
              for d in range(1, CONV_W)]
    out = w[CONV_W - 1:CONV_W, :] * x + bias
    for j in range(1, CONV_W):
        shifted = jnp.concatenate(before[:j][::-1] + [x[:rows - j * SUBLANES]], axis=0)
        out = out + w[CONV_W - 1 - j:CONV_W - j, :] * shifted
    tail_ref[...] = x[rows - (CONV_W - 1) * SUBLANES:]
    return out


def _cumsum_interleaved(x, rows):
    tiles = [_tile(x, 0)]
    for i in range(1, rows // SUBLANES):
        tiles.append(tiles[-1] + _tile(x, i))
    total = tiles[-1]
    start = _tile_cumsum(total) - total
    return jnp.concatenate([t + start for t in tiles], axis=0)


def _cummax_interleaved(x, rows):
    tiles = [_tile(x, 0)]
    for i in range(1, rows // SUBLANES):
        tiles.append(jnp.maximum(tiles[-1], _tile(x, i)))
    best = tiles[-1]
    pos = lax.broadcasted_iota(jnp.int32, best.shape, 0)
    s = 1
    while s < SUBLANES:
        best = jnp.maximum(best, jnp.where(pos >= s, pltpu.roll(best, s, 0), -jnp.inf))
        s *= 2
    start = jnp.where(pos == 0, -jnp.inf, pltpu.roll(best, 1, 0))
    return jnp.concatenate([jnp.maximum(t, start) for t in tiles], axis=0)


def _scan_real_interleaved(a_ref, b_ref, carry, rows):
    ntiles = rows // SUBLANES
    pos = lax.broadcasted_iota(jnp.int32, (SUBLANES, a_ref.shape[1]), 0)
    h = _tile(b_ref, 0)
    aprod = _tile(a_ref, 0)
    for i in range(1, ntiles):
        a = _tile(a_ref, i)
        h = a * h + _tile(b_ref, i)
        aprod = aprod * a
    g = _tile_scan_real(aprod, h + jnp.where(pos == 0, aprod * carry, 0.0))
    h = jnp.where(pos == 0, carry, pltpu.roll(g, 1, 0))
    for i in range(ntiles):
        h = _tile(a_ref, i) * h + _tile(b_ref, i)
        b_ref[i * SUBLANES:(i + 1) * SUBLANES, :] = h
    return g[SUBLANES - 1:SUBLANES]


def _scan_cplx_interleaved(re_ref, im_ref, p_re, p_im, cre_ref, cim_ref, rows):
    ntiles = rows // SUBLANES
    width = re_ref.shape[1]
    pos = lax.broadcasted_iota(jnp.int32, (SUBLANES, SCAN_LANE_BLOCK), 0)
    for blk in range(width // SCAN_LANE_BLOCK):
        sl = slice(blk * SCAN_LANE_BLOCK, (blk + 1) * SCAN_LANE_BLOCK)
        pr = jnp.broadcast_to(p_re[:, sl], pos.shape)
        pi = jnp.broadcast_to(p_im[:, sl], pos.shape)
        c_r = cre_ref[:, sl]
        c_i = cim_ref[:, sl]
        sr = re_ref[0:SUBLANES, sl]
        si = im_ref[0:SUBLANES, sl]
        for i in range(1, ntiles):
            rs = slice(i * SUBLANES, (i + 1) * SUBLANES)
            sr, si = pr * sr - pi * si + re_ref[rs, sl], pr * si + pi * sr + im_ref[rs, sl]
        qr, qi = pr, pi
        n = 1
        while n < ntiles:
            qr, qi = qr * qr - qi * qi, 2.0 * (qr * qi)
            n *= 2
        gr, gi = _tile_scan_cplx(sr + jnp.where(pos == 0, qr * c_r - qi * c_i, 0.0),
                                 si + jnp.where(pos == 0, qr * c_i + qi * c_r, 0.0), qr, qi)
        sr = jnp.where(pos == 0, c_r, pltpu.roll(gr, 1, 0))
        si = jnp.where(pos == 0, c_i, pltpu.roll(gi, 1, 0))
        for i in range(ntiles):
            rs = slice(i * SUBLANES, (i + 1) * SUBLANES)
            sr, si = pr * sr - pi * si + re_ref[rs, sl], pr * si + pi * sr + im_ref[rs, sl]
            re_ref[rs, sl] = sr
            im_ref[rs, sl] = si
        cre_ref[:, sl] = gr[SUBLANES - 1:SUBLANES]
        cim_ref[:, sl] = gi[SUBLANES - 1:SUBLANES]


def _prompt_mixer_kernel(*refs, rows, nchunk, n_alias):
    (xnext_ref, xfirst_ref, vec_ref, win_ref,
     wqk_ref, wvo_ref, wif_ref, rwa_ref, rwx_ref, wbu_ref, wcre_ref, wcim_ref, wglu_ref) = refs[:13]
    (mixed_ref, c_ref, n_ref, m_ref, mtail_ref, h_ref, rtail_ref, sre_ref, sim_ref,
     q_scr, k_scr, v_scr, o_scr, ra_scr, rb_scr, ure_scr, uim_scr, proj_scr, xn_scr) = refs[13 + n_alias:]
    T = rows
    step = pl.program_id(0)
    slot = lax.rem(step, 2)
    proj_ref = proj_scr.at[slot]
    next_proj_ref = proj_scr.at[1 - slot]
    slab = (2 * MIX_WIDTH) // M_HEADS

    @pl.when(step == 0)
    def _():
        xn = _rmsnorm_bf16(xfirst_ref[...], vec_ref)
        for j in range(M_HEADS):
            cols = slice(j * slab, (j + 1) * slab)
            proj_scr[0, :, cols] = jnp.dot(xn, win_ref[:, cols], preferred_element_type=F32)

    @pl.when(lax.rem(step, nchunk) == 0)
    def _():
        for ref in (c_ref, n_ref, m_ref, mtail_ref, h_ref, rtail_ref, sre_ref, sim_ref):
            ref[...] = jnp.zeros(ref.shape, F32)

    xn_scr[...] = _rmsnorm_bf16(xnext_ref[...], vec_ref)

    xm = proj_ref[:, 0:M_WIDTH]
    xc = _silu(_conv_interleaved(xm, mtail_ref, _vec(vec_ref, "mcw"), _vec(vec_ref, "mcb"), T))
    gates = _mlstm_project(xc, xm, vec_ref, wqk_ref, wvo_ref, wif_ref, q_scr, k_scr, v_scr, o_scr)
    log_i = gates[:, :LANES]
    b_all = _cumsum_interleaved(_log_sigmoid(gates[:, LANES:]), T)
    a_max = _cummax_interleaved(log_i - b_all, T)
    m_prev = m_ref[...]
    log_inter = b_all + m_prev
    m_t_all = jnp.maximum(log_inter, b_all + a_max)
    w_inter_all = jnp.exp(log_inter - m_t_all)
    floor_all = jnp.exp(-m_t_all)
    b_end = b_all[T - 1:T, :]
    m_new = jnp.maximum(b_end + m_prev, b_end + a_max[T - 1:T, :])
    w_src_all = jnp.exp(b_end - b_all + log_i - m_new)
    decay_all = jnp.exp(b_end + m_prev - m_new)
    m_ref[...] = m_new
    b_t = b_all.T
    li_t = log_i.T

    rowi = lax.broadcasted_iota(jnp.int32, (T, T), 0)
    coli = lax.broadcasted_iota(jnp.int32, (T, T), 1)
    sub_len = T // SUBLANES

    def time_of(r):
        return (r & (SUBLANES - 1)) * sub_len + (r >> SUBLANE_SHIFT)

    causal = time_of(coli) <= time_of(rowi)
    k_scale = M_HEAD_DIM ** -0.5
    mnw = _vec(vec_ref, "mnw")
    mskip = _vec(vec_ref, "mskip")

    for h in range(M_HEADS):
        sl = slice(h * M_HEAD_DIM, (h + 1) * M_HEAD_DIM)
        c_prev = c_ref[h]
        n_prev = n_ref[h:h + 1, :]
        w_inter = w_inter_all[:, h:h + 1]
        w_src = w_src_all[:, h:h + 1]
        decay = decay_all[:, h:h + 1]

        q = q_scr[:, sl]
        ks = k_scr[:, sl] * k_scale
        v = v_scr[:, sl]
        q_b = q.astype(BF16)
        ks_b = ks.astype(BF16)

        log_d = jnp.where(causal, b_all[:, h:h + 1] - b_t[h:h + 1, :] + li_t[h:h + 1, :], -jnp.inf)
        w_intra = jnp.exp(log_d - m_t_all[:, h:h + 1])
        s = lax.dot_general(q_b, ks_b, _NT, preferred_element_type=F32) * w_intra
        inter = lax.dot_general(q_b, c_prev.astype(BF16), _NT, preferred_element_type=F32)
        num = jnp.dot(s.astype(BF16), v.astype(BF16), preferred_element_type=F32) + w_inter * inter
        den = (jnp.sum(s, axis=1, keepdims=True)
               + w_inter * jnp.sum(q * n_prev, axis=1, keepdims=True))
        hh = num / jnp.maximum(jnp.abs(den), floor_all[:, h:h + 1])

        vw_t = (v * w_src).T.astype(BF16)
        c_ref[h] = decay * c_prev + jnp.dot(vw_t, ks_b, preferred_element_type=F32)
        n_ref[h:h + 1, :] = decay * n_prev + jnp.sum(ks * w_src, axis=0, keepdims=True)

        z = proj_ref[:, MIX_WIDTH + h * M_HEAD_DIM:MIX_WIDTH + (h + 1) * M_HEAD_DIM]
        mixed_ref[:, sl] = _head_output(hh, o_scr[:, sl], xc[:, sl], z, mnw[:, sl], mskip[:, sl]).astype(BF16)

        cols = slice(h * slab, (h + 1) * slab)
        next_proj_ref[:, cols] = jnp.dot(xn_scr[...], win_ref[:, cols], preferred_element_type=F32)

    xr = proj_ref[:, M_WIDTH:M_WIDTH + R_WIDTH]
    xcr = _conv_interleaved(xr, rtail_ref, _vec(vec_ref, "rcw"), _vec(vec_ref, "rcb"), T)
    a, bb = _rglru_coeffs(xcr, vec_ref, rwa_ref, rwx_ref)
    ra_scr[...] = a
    rb_scr[...] = bb
    h_ref[...] = _scan_real_interleaved(ra_scr, rb_scr, h_ref[...], T)
    zr = proj_ref[:, MIX_WIDTH + M_WIDTH:MIX_WIDTH + M_WIDTH + R_WIDTH]
    mixed_ref[:, M_WIDTH:M_WIDTH + R_WIDTH] = (rb_scr[...] * _silu(zr)).astype(BF16)

    u = proj_ref[:, M_WIDTH + R_WIDTH:MIX_WIDTH]
    _s5_project(u.astype(BF16), wbu_ref, ure_scr, uim_scr)
    _scan_cplx_interleaved(ure_scr, uim_scr, _vec(vec_ref, "abre"), _vec(vec_ref, "abim"), sre_ref, sim_ref, T)
    glu = _s5_output(ure_scr, uim_scr, u, vec_ref, wcre_ref, wcim_ref, wglu_ref)
    zs = proj_ref[:, MIX_WIDTH + M_WIDTH + R_WIDTH:2 * MIX_WIDTH]
    mixed_ref[:, M_WIDTH + R_WIDTH:MIX_WIDTH] = (glu * _silu(zs)).astype(BF16)


_PROMPT_STATE_SHAPES = ((M_HEADS, M_HEAD_DIM, M_HEAD_DIM), (M_HEADS, M_HEAD_DIM), (1, LANES),
                        ((CONV_W - 1) * SUBLANES, M_WIDTH), (1, R_WIDTH), ((CONV_W - 1) * SUBLANES, R_WIDTH),
                        (1, S_LANES), (1, S_LANES))


def _prompt_mixer(x, total_rows, l, bsz, seq, vec, w_in_b, mats, prev_states):
    rows = PROMPT_CHUNK
    nchunk = seq // rows
    nstep = bsz * nchunk
    aliases = {}
    alias_args = []
    n_in = 4 + len(mats)
    if prev_states is not None:
        alias_args = list(prev_states)
        aliases = {n_in + k: 1 + k for k in range(len(alias_args))}

    def state_spec(shape):
        nd = len(shape)
        return pl.BlockSpec((None, None) + shape, lambda t: (l, t // nchunk) + (0,) * nd)

    out_shape = ([jax.ShapeDtypeStruct((total_rows, MIX_WIDTH), BF16)]
                 + [jax.ShapeDtypeStruct((DEPTH, bsz) + s, F32) for s in _PROMPT_STATE_SHAPES])
    scratch = ([pltpu.VMEM((rows, w), F32)
                for w in (M_WIDTH, M_WIDTH, M_WIDTH, M_WIDTH, R_WIDTH, R_WIDTH, S_LANES, S_LANES)]
               + [pltpu.VMEM((2, rows, 2 * MIX_WIDTH), F32), pltpu.VMEM((rows, D_MODEL), BF16)])
    return pl.pallas_call(
        functools.partial(_prompt_mixer_kernel, rows=rows, nchunk=nchunk, n_alias=len(alias_args)),
        grid=(nstep,),
        in_specs=([pl.BlockSpec((rows, D_MODEL), lambda t: (jnp.minimum(t + 1, nstep - 1), 0)),
                   pl.BlockSpec((rows, D_MODEL), lambda t: (0, 0), pipeline_mode=pl.Buffered(1)),
                   _layer_spec(vec, l),
                   pl.BlockSpec(w_in_b.shape, lambda t: (0, 0), pipeline_mode=pl.Buffered(1))]
                  + [_layer_spec(w, l) for w in mats] + [_ANY_SPEC] * len(alias_args)),
        out_specs=([pl.BlockSpec((rows, MIX_WIDTH), lambda t: (t, 0))]
                   + [state_spec(s) for s in _PROMPT_STATE_SHAPES]),
        out_shape=out_shape,
        scratch_shapes=scratch,
        input_output_aliases=aliases,
        compiler_params=pltpu.CompilerParams(
            dimension_semantics=("arbitrary",), vmem_limit_bytes=VMEM_LIMIT_BYTES),
        name="prompt_mixer",
    )(x, x, vec, w_in_b, *mats, *alias_args)


def _seg_last(x, groups):
    x3 = x.reshape(groups, SUBLANES, x.shape[-1])
    return jnp.broadcast_to(x3[:, SUBLANES - 1:SUBLANES, :], x3.shape).reshape(x.shape)


def _seg_max(x, groups):
    x3 = x.reshape(groups, SUBLANES, x.shape[-1])
    return jnp.broadcast_to(jnp.max(x3, axis=1, keepdims=True), x3.shape).reshape(x.shape)


def _seg_sum(x, groups):
    x3 = x.reshape(groups, SUBLANES, x.shape[-1])
    return jnp.broadcast_to(jnp.sum(x3, axis=1, keepdims=True), x3.shape).reshape(x.shape)


def _seg_rows(state_ref, lanes=slice(None)):
    x = state_ref[:, :, lanes]
    return jnp.broadcast_to(x, (x.shape[0], SUBLANES, x.shape[2])).reshape(x.shape[0] * SUBLANES, x.shape[2])


def _seg_state(x):
    x3 = x.reshape(x.shape[0] // SUBLANES, SUBLANES, x.shape[-1])
    return x3[:, SUBLANES - 1:SUBLANES, :]


def _with_history(x, buf_ref, new_buf_ref):
    g = x.shape[0] // SUBLANES
    x3 = x.reshape(g, SUBLANES, x.shape[1])
    lead = SAMPLE_LEAD - (CONV_W - 1)
    full = jnp.concatenate([jnp.zeros((g, lead, x.shape[1]), F32), buf_ref[...], x3[:, SAMPLE_LEAD:, :]], axis=1)
    new_buf_ref[...] = full[:, SUBLANES - (CONV_W - 1):, :]
    return full.reshape(x.shape)


def _conv_rolled(xf, w, bias):
    out = w[CONV_W - 1:CONV_W, :] * xf + bias
    for j in range(1, CONV_W):
        out = out + w[CONV_W - 1 - j:CONV_W - j, :] * pltpu.roll(xf, j, 0)
    return out


_SAMPLE_STATE_WIDTHS = (M_WIDTH, LANES, M_WIDTH, R_WIDTH, R_WIDTH, S_LANES, S_LANES)
_SAMPLE_STATE_ROWS = (1, 1, CONV_W - 1, 1, CONV_W - 1, 1, 1)


def _sample_mixer_kernel(*refs, rows, c_seqs, n_alias):
    (proj_ref, c0_ref, n0_ref, m0_ref, mtail0_ref, h0_ref, rtail0_ref, sre0_ref, sim0_ref,
     vec_ref, wqk_ref, wvo_ref, wif_ref, rwa_ref, rwx_ref, wbu_ref, wcre_ref, wcim_ref, wglu_ref) = refs[:19]
    (mixed_ref, c_ref, n_ref, m_ref, mconv_ref, h_ref, rconv_ref, sre_ref, sim_ref,
     q_scr, k_scr, v_scr, o_scr, xc_scr, intert_scr, vwt_scr, qt_scr, ksb_scr, dec_scr, b_scr, li_scr,
     ure_scr, uim_scr) = refs[19 + n_alias:]
    R = rows
    G = R // SUBLANES
    sub = pl.program_id(1)
    k_scale = M_HEAD_DIM ** -0.5
    pos = _sublane_pos((R, 1))
    valid = pos >= SAMPLE_LEAD
    first_token = pos == SAMPLE_LEAD

    @pl.when(sub == 0)
    def _():
        xm = _with_history(proj_ref[:, 0:M_WIDTH], mtail0_ref, mconv_ref)
        xc = _silu(_conv_rolled(xm, _vec(vec_ref, "mcw"), _vec(vec_ref, "mcb")))
        xc_scr[...] = xc
        gates = _mlstm_project(xc, xm, vec_ref, wqk_ref, wvo_ref, wif_ref, q_scr, k_scr, v_scr, o_scr)
        log_i = jnp.where(valid, gates[:, :LANES], -jnp.inf)
        log_f = jnp.where(valid, _log_sigmoid(gates[:, LANES:]), 0.0)
        b_all = _tile_cumsum(log_f)
        b_end = _seg_last(b_all, G)
        m_rows = _seg_rows(m0_ref)
        n_rows = _seg_rows(n0_ref)
        log_src = b_end - b_all + log_i
        m_new = jnp.maximum(b_end + m_rows, _seg_max(log_src, G))
        w_src_all = jnp.exp(log_src - m_new)
        decay_all = jnp.exp(b_end + m_rows - m_new)
        dec_scr[...] = decay_all
        b_scr[...] = b_all
        li_scr[...] = log_i
        m_ref[...] = _seg_state(m_new)
        for h in range(M_HEADS):
            sl = slice(h * M_HEAD_DIM, (h + 1) * M_HEAD_DIM)
            ks = k_scr[:, sl] * k_scale
            w_src = w_src_all[:, h:h + 1]
            n_ref[:, :, sl] = _seg_state(decay_all[:, h:h + 1] * n_rows[:, sl] + _seg_sum(ks * w_src, G))
            vwt_scr[h] = (v_scr[:, sl] * w_src).T
            qt_scr[h] = q_scr[:, sl].T.astype(BF16)
            ksb_scr[h] = ks.astype(BF16)
            intert_scr[h] = jnp.zeros((M_HEAD_DIM, R), F32)

        xr = _with_history(proj_ref[:, M_WIDTH:M_WIDTH + R_WIDTH], rtail0_ref, rconv_ref)
        xcr = _conv_rolled(xr, _vec(vec_ref, "rcw"), _vec(vec_ref, "rcb"))
        a, bb = _rglru_coeffs(xcr, vec_ref, rwa_ref, rwx_ref)
        hs = _tile_scan_real(a, jnp.where(valid, bb, 0.0) + jnp.where(first_token, a * _seg_rows(h0_ref), 0.0))
        h_ref[...] = _seg_state(hs)
        zr = proj_ref[:, MIX_WIDTH + M_WIDTH:MIX_WIDTH + M_WIDTH + R_WIDTH]
        mixed_ref[:, M_WIDTH:M_WIDTH + R_WIDTH] = jnp.where(valid, hs * _silu(zr), 0.0).astype(BF16)

        u = proj_ref[:, M_WIDTH + R_WIDTH:MIX_WIDTH]
        _s5_project(u.astype(BF16), wbu_ref, ure_scr, uim_scr)
        p_re = _vec(vec_ref, "abre")
        p_im = _vec(vec_ref, "abim")
        for blk in range(S_LANES // SCAN_LANE_BLOCK):
            sl = slice(blk * SCAN_LANE_BLOCK, (blk + 1) * SCAN_LANE_BLOCK)
            pr = jnp.broadcast_to(p_re[:, sl], (R, SCAN_LANE_BLOCK))
            pi = jnp.broadcast_to(p_im[:, sl], (R, SCAN_LANE_BLOCK))
            s0r = _seg_rows(sre0_ref, sl)
            s0i = _seg_rows(sim0_ref, sl)
            sr, si = _tile_scan_cplx(
                jnp.where(valid, ure_scr[:, sl], 0.0) + jnp.where(first_token, pr * s0r - pi * s0i, 0.0),
                jnp.where(valid, uim_scr[:, sl], 0.0) + jnp.where(first_token, pr * s0i + pi * s0r, 0.0), pr, pi)
            ure_scr[:, sl] = sr
            uim_scr[:, sl] = si
            sre_ref[:, :, sl] = _seg_state(sr)
            sim_ref[:, :, sl] = _seg_state(si)
        glu = _s5_output(ure_scr, uim_scr, u, vec_ref, wcre_ref, wcim_ref, wglu_ref)
        zs = proj_ref[:, MIX_WIDTH + M_WIDTH + R_WIDTH:2 * MIX_WIDTH]
        mixed_ref[:, M_WIDTH + R_WIDTH:MIX_WIDTH] = jnp.where(valid, glu * _silu(zs), 0.0).astype(BF16)

    lane_seq = lax.broadcasted_iota(jnp.int32, (M_HEAD_DIM, R), 1) >> SUBLANE_SHIFT
    seq0 = sub * c_seqs
    for h in range(M_HEADS):
        c_old = c0_ref[:, h].reshape(c_seqs * M_HEAD_DIM, M_HEAD_DIM)
        readout = jnp.dot(c_old.astype(BF16), qt_scr[h], preferred_element_type=F32)
        acc = intert_scr[h]
        vwt = vwt_scr[h]
        lhs = []
        for s in range(c_seqs):
            own = lane_seq == seq0 + s
            acc = jnp.where(own, readout[s * M_HEAD_DIM:(s + 1) * M_HEAD_DIM], acc)
            lhs.append(jnp.where(own, vwt, 0.0))
        intert_scr[h] = acc
        upd = jnp.dot(jnp.concatenate(lhs, axis=0).astype(BF16), ksb_scr[h], preferred_element_type=F32)
        for s in range(c_seqs):
            r0 = pl.multiple_of((seq0 + s) * SUBLANES, SUBLANES)
            c_ref[s, h] = (dec_scr[pl.ds(r0, 1), h:h + 1] * c0_ref[s, h]
                           + upd[s * M_HEAD_DIM:(s + 1) * M_HEAD_DIM])

    @pl.when(sub == pl.num_programs(1) - 1)
    def _():
        _sample_heads(proj_ref, m0_ref, n0_ref, vec_ref, mixed_ref,
                      q_scr, k_scr, v_scr, o_scr, xc_scr, intert_scr, b_scr, li_scr, valid, R)


def _sample_heads(proj_ref, m0_ref, n0_ref, vec_ref, mixed_ref,
                  q_scr, k_scr, v_scr, o_scr, xc_scr, intert_scr, b_scr, li_scr, valid, R):
    rowi = lax.broadcasted_iota(jnp.int32, (R, R), 0)
    coli = lax.broadcasted_iota(jnp.int32, (R, R), 1)
    same_causal = jnp.logical_and(coli <= rowi, (coli >> SUBLANE_SHIFT) == (rowi >> SUBLANE_SHIFT))
    k_scale = M_HEAD_DIM ** -0.5
    mnw = _vec(vec_ref, "mnw")
    mskip = _vec(vec_ref, "mskip")
    n_rows = _seg_rows(n0_ref)
    b_all = b_scr[...]
    log_i = li_scr[...]
    log_inter = b_all + _seg_rows(m0_ref)
    m_t_all = jnp.maximum(log_inter, b_all + _tile_cummax(log_i - b_all))
    w_inter_all = jnp.exp(log_inter - m_t_all)
    floor_all = jnp.exp(-m_t_all)
    b_t = b_all.T
    li_t = log_i.T

    for h in range(M_HEADS):
        sl = slice(h * M_HEAD_DIM, (h + 1) * M_HEAD_DIM)
        q = q_scr[:, sl]
        ks = k_scr[:, sl] * k_scale
        v = v_scr[:, sl]
        q_b = q.astype(BF16)
        ks_b = ks.astype(BF16)
        w_inter = w_inter_all[:, h:h + 1]

        log_d = jnp.where(same_causal, b_all[:, h:h + 1] - b_t[h:h + 1, :] + li_t[h:h + 1, :], -jnp.inf)
        w_intra = jnp.exp(log_d - m_t_all[:, h:h + 1])
        s = lax.dot_general(q_b, ks_b, _NT, preferred_element_type=F32) * w_intra
        num = (jnp.dot(s.astype(BF16), v.astype(BF16), preferred_element_type=F32)
               + w_inter * intert_scr[h].T)
        den = (jnp.sum(s, axis=1, keepdims=True)
               + w_inter * jnp.sum(q * n_rows[:, sl], axis=1, keepdims=True))
        hh = num / jnp.maximum(jnp.abs(den), floor_all[:, h:h + 1])

        z = proj_ref[:, MIX_WIDTH + h * M_HEAD_DIM:MIX_WIDTH + (h + 1) * M_HEAD_DIM]
        out = _head_output(hh, o_scr[:, sl], xc_scr[:, sl], z, mnw[:, sl], mskip[:, sl])
        mixed_ref[:, sl] = jnp.where(valid, out, 0.0).astype(BF16)


def _sample_mixer(proj, l, row0, mixed_prev, c_all, row_state, vec, mats, prev_out):
    R = SAMPLE_ROWS
    nrows = proj.shape[0]
    nsub = (R // SUBLANES) // SAMPLE_C_SEQS
    blk0 = row0 // R
    c_spec = pl.BlockSpec((None, SAMPLE_C_SEQS, M_HEADS, M_HEAD_DIM, M_HEAD_DIM),
                          lambda i, j: (l, i * nsub + j, 0, 0, 0))

    def state_spec(width, rows_per_seq):
        return pl.BlockSpec((None, R // SUBLANES, rows_per_seq, width), lambda i, j: (l, i, 0, 0))

    def state_shape(width, rows_per_seq):
        return (DEPTH, nrows // SUBLANES, rows_per_seq, width)

    state_specs = [state_spec(w, r) for w, r in zip(_SAMPLE_STATE_WIDTHS, _SAMPLE_STATE_ROWS)]
    alias_args = [mixed_prev] + (list(prev_out) if prev_out is not None else [])
    first_alias = 2 + len(row_state) + 1 + len(mats)
    aliases = {first_alias + k: k for k in range(len(alias_args))}
    out_shape = ([jax.ShapeDtypeStruct(mixed_prev.shape, BF16), jax.ShapeDtypeStruct(c_all.shape, F32)]
                 + [jax.ShapeDtypeStruct(state_shape(w, r), F32)
                    for w, r in zip(_SAMPLE_STATE_WIDTHS, _SAMPLE_STATE_ROWS)])
    scratch = ([pltpu.VMEM((R, M_WIDTH), F32)] * 5
               + [pltpu.VMEM((M_HEADS, M_HEAD_DIM, R), F32), pltpu.VMEM((M_HEADS, M_HEAD_DIM, R), F32),
                  pltpu.VMEM((M_HEADS, M_HEAD_DIM, R), BF16), pltpu.VMEM((M_HEADS, R, M_HEAD_DIM), BF16)]
               + [pltpu.VMEM((R, LANES), F32)] * 3
               + [pltpu.VMEM((R, S_LANES), F32)] * 2)
    return pl.pallas_call(
        functools.partial(_sample_mixer_kernel, rows=R, c_seqs=SAMPLE_C_SEQS, n_alias=len(alias_args)),
        grid=(nrows // R, nsub),
        in_specs=([pl.BlockSpec((R, 2 * MIX_WIDTH), lambda i, j: (i, 0)), c_spec] + state_specs
                  + [_layer_spec(vec, l)] + [_layer_spec(w, l) for w in mats] + [_ANY_SPEC] * len(alias_args)),
        out_specs=[pl.BlockSpec((R, MIX_WIDTH), lambda i, j: (blk0 + i, 0)), c_spec] + state_specs,
        out_shape=out_shape,
        scratch_shapes=scratch,
        input_output_aliases=aliases,
        compiler_params=pltpu.CompilerParams(
            dimension_semantics=("parallel", "arbitrary"), vmem_limit_bytes=VMEM_LIMIT_BYTES),
        name="sample_mixer",
    )(proj, c_all, *row_state, vec, *mats, *alias_args)


def _prepare_weights(p, s5):
    abre, abim, wbu, wcre, wcim = s5

    def split_gates(g):
        pad = [(0, 0)] * (g.ndim - 1) + [(0, LANES - M_HEADS)]
        return jnp.concatenate([jnp.pad(g[..., :M_HEADS], pad), jnp.pad(g[..., M_HEADS:], pad)], axis=-1)

    table = {"mcw": p["m_conv_w"], "rcw": p["r_conv_w"], "mcb": p["m_conv_b"], "mnw": p["m_norm_w"],
             "mskip": p["m_skip"], "rcb": p["r_conv_b"], "rba": p["r_ba"], "rbx": p["r_bx"], "rlam": p["r_lam"],
             "sd": p["s_d"], "bglu": p["s_b_glu"], "bif": split_gates(p["m_b_if"]), "abre": abre, "abim": abim,
             "norm_w": p["norm_w"]}

    def group_rows(group):
        parts = [table[name].reshape(DEPTH, n, w) for name, n, w in group]
        fill = _VEC_WIDTH - sum(w for _, _, w in group)
        if fill:
            parts.append(jnp.zeros((DEPTH, group[0][1], fill), F32))
        return parts[0] if len(parts) == 1 else jnp.concatenate(parts, axis=-1)

    vec = jnp.concatenate([group_rows(g) for g in _VEC_GROUPS]
                          + [jnp.zeros((DEPTH, _VEC_ROWS - _VEC_USED_ROWS, _VEC_WIDTH), F32)], axis=1)

    def block_diag_halves(w):
        nb = R_BLOCKS // 2
        w5 = w.reshape(DEPTH, 2, nb, w.shape[-2], w.shape[-1])
        eye = jnp.eye(nb, dtype=F32)[:, None, :, None]
        return (w5[:, :, :, :, None, :] * eye).reshape(
            DEPTH, 2, nb * w.shape[-2], nb * w.shape[-1]).astype(BF16)

    mats = {
        "wqk": jnp.concatenate([p["m_wq"], p["m_wk"]], axis=-1).astype(BF16),
        "wvo": jnp.concatenate([p["m_wv"], p["m_wo"]], axis=-1).astype(BF16),
        "wif": split_gates(p["m_w_if"]).astype(BF16),
        "rwa": block_diag_halves(p["r_wa"]), "rwx": block_diag_halves(p["r_wx"]),
        "wbu": wbu, "wcre": wcre, "wcim": wcim,
        "wglu": p["s_w_glu"].astype(BF16),
    }
    return vec, [mats[k] for k in _MATRIX_KEYS]


def _sample_rows_state(n, m, mconv, h, rconv, sre, sim):
    bsz = n.shape[1]

    def per_seq(x):
        return x.reshape(DEPTH, bsz, 1, x.shape[-1])

    m_pad = jnp.pad(m, ((0, 0), (0, 0), (0, LANES - M_HEADS)))
    return (per_seq(n.reshape(DEPTH, bsz, M_WIDTH)), per_seq(m_pad), mconv, per_seq(h), rconv,
            per_seq(sre.reshape(DEPTH, bsz, S_LANES)), per_seq(sim.reshape(DEPTH, bsz, S_LANES)))


def _sample_state_from_rows(n, m, mconv, h, rconv, sre, sim):
    bsz = n.shape[1]
    return (n.reshape(DEPTH, bsz, M_HEADS, M_HEAD_DIM), m.reshape(DEPTH, bsz, LANES)[..., :M_HEADS],
            mconv, h.reshape(DEPTH, bsz, R_WIDTH), rconv,
            sre.reshape(DEPTH, bsz, S_GROUPS, S_STATE), sim.reshape(DEPTH, bsz, S_GROUPS, S_STATE))


def _prompt_state_from_kernel(c, n, m, mtail, h, rtail, sre, sim):
    bsz = c.shape[1]
    return (c, n, m[:, :, 0, :M_HEADS], mtail[:, :, SUBLANES - 1::SUBLANES], h.reshape(DEPTH, bsz, R_WIDTH),
            rtail[:, :, SUBLANES - 1::SUBLANES],
            sre.reshape(DEPTH, bsz, S_GROUPS, S_STATE), sim.reshape(DEPTH, bsz, S_GROUPS, S_STATE))


def _interleave_chunks(x, inverse=False):
    bsz, seq, d = x.shape
    sub_len = PROMPT_CHUNK // SUBLANES
    inner = (sub_len, SUBLANES) if inverse else (SUBLANES, sub_len)
    return x.reshape(bsz, seq // PROMPT_CHUNK, *inner, d).transpose(0, 1, 3, 2, 4).reshape(bsz, seq, d)


def kernel(x_prompt, x_sample, state_mlstm_C, state_mlstm_n, state_mlstm_m, state_mlstm_conv, state_rglru_h, state_rglru_conv, state_s5_re, state_s5_im, norm_w, w_in, w_out, m_conv_w, m_conv_b, m_wq, m_wk, m_wv, m_wo, m_w_if, m_b_if, m_norm_w, m_skip, r_conv_w, r_conv_b, r_wa, r_ba, r_wx, r_bx, r_lam, s_lam_re, s_lam_im, s_b_re, s_b_im, s_c_re, s_c_im, s_d, s_log_step, s_w_glu, s_b_glu, final_norm_w):
    p = dict(norm_w=norm_w, w_in=w_in, w_out=w_out, m_conv_w=m_conv_w, m_conv_b=m_conv_b, m_wq=m_wq, m_wk=m_wk,
             m_wv=m_wv, m_wo=m_wo, m_w_if=m_w_if, m_b_if=m_b_if, m_norm_w=m_norm_w, m_skip=m_skip,
             r_conv_w=r_conv_w, r_conv_b=r_conv_b, r_wa=r_wa, r_ba=r_ba, r_wx=r_wx, r_bx=r_bx, r_lam=r_lam,
             s_d=s_d, s_w_glu=s_w_glu, s_b_glu=s_b_glu)
    s5 = _s5_prep(s_lam_re, s_lam_im, s_log_step, s_b_re, s_b_im, s_c_re, s_c_im)
    vec, mats = _prepare_weights(p, s5)
    final_w = final_norm_w.reshape(1, D_MODEL)

    bsz, seq, _ = x_prompt.shape
    dec_batch, dec_seq, _ = x_sample.shape
    n_prompt = bsz * seq
    n_sample = dec_batch * SUBLANES
    stream = [_interleave_chunks(x_prompt).reshape(n_prompt, D_MODEL),
              jnp.pad(x_sample, ((0, 0), (SAMPLE_LEAD, 0), (0, 0))).reshape(n_sample, D_MODEL)]
    rows_state = _sample_rows_state(state_mlstm_n, state_mlstm_m, state_mlstm_conv, state_rglru_h,
                                    state_rglru_conv, state_s5_re, state_s5_im)
    pr_states = None
    sa_states = None
    for l in range(DEPTH):
        last = l == DEPTH - 1
        proj_s, w_in_b = _inproj(stream[-1], stream[-1].shape[0] - n_sample, n_sample, vec, w_in, l)
        mixed, *pr_states = _prompt_mixer(stream[0], n_prompt + n_sample, l, bsz, seq, vec, w_in_b, mats, pr_states)
        mixed, *sa_states = _sample_mixer(proj_s, l, n_prompt, mixed, state_mlstm_C, rows_state, vec, mats, sa_states)
        stream = _outproj(mixed, stream, w_out, l, final_w, last, split_rows=n_prompt if last else None)
    y_prompt = _interleave_chunks(stream[0].reshape(bsz, seq, D_MODEL), inverse=True)
    y_sample = stream[1]
    return (y_prompt, y_sample, *_prompt_state_from_kernel(*pr_states),
            sa_states[0], *_sample_state_from_rows(*sa_states[1:]))
```

```python
import functools

import jax
import jax.numpy as jnp
from jax import lax
from jax.experimental import pallas as pl
from jax.experimental.pallas import tpu as pltpu

F32 = jnp.float32
BF16 = jnp.bfloat16

D_MODEL = 2048
DEPTH = 2
MIX_WIDTH = D_MODEL
M_WIDTH = MIX_WIDTH // 2
R_WIDTH = MIX_WIDTH // 4
S_WIDTH = MIX_WIDTH - M_WIDTH - R_WIDTH
M_HEADS = 8
M_HEAD_DIM = M_WIDTH // M_HEADS
R_BLOCKS = 8
RG_C = 8.0
S_GROUP = 16
S_GROUPS = S_WIDTH // S_GROUP
S_STATE = 64
S_LANES = S_GROUPS * S_STATE
CONV_W = 4
EPS = 1e-6

SUBLANES = 8
SUBLANE_SHIFT = SUBLANES.bit_length() - 1
LANES = 128
VMEM_LIMIT_BYTES = 56 * 1024 * 1024

PROMPT_CHUNK = 256
SAMPLE_ROWS = 128
SAMPLE_C_SEQS = 16
SAMPLE_LEAD = SUBLANES - 4
INPROJ_TN = 1024
OUTPROJ_TM = 512
SCAN_LANE_BLOCK = 256

_NT = (((1,), (1,)), ((), ()))

_VEC_WIDTH = D_MODEL
_VEC_GROUPS = ((("mcw", CONV_W, M_WIDTH), ("rcw", CONV_W, R_WIDTH)),
               (("mcb", 1, M_WIDTH), ("mnw", 1, M_WIDTH)),
               (("mskip", 1, M_WIDTH), ("rcb", 1, R_WIDTH), ("rba", 1, R_WIDTH)),
               (("rbx", 1, R_WIDTH), ("rlam", 1, R_WIDTH), ("sd", 1, S_WIDTH), ("bglu", 1, S_WIDTH)),
               (("bif", 1, 2 * LANES),),
               (("abre", 1, S_LANES),), (("abim", 1, S_LANES),), (("norm_w", 1, D_MODEL),))
_VEC_LAYOUT = {}
_row = 0
for _group in _VEC_GROUPS:
    _lane = 0
    for _name, _n, _w in _group:
        _VEC_LAYOUT[_name] = (_row, _n, _lane, _w)
        _lane += _w
    _row += _group[0][1]
_VEC_USED_ROWS = _row
_VEC_ROWS = -(-_row // SUBLANES) * SUBLANES


def _vec(vec_ref, name):
    r0, n, l0, w = _VEC_LAYOUT[name]
    return vec_ref[r0:r0 + n, l0:l0 + w]


def _layer_spec(arr, l):
    nd = arr.ndim - 1
    return pl.BlockSpec((None,) + arr.shape[1:], lambda *_: (l,) + (0,) * nd, pipeline_mode=pl.Buffered(1))


_ANY_SPEC = pl.BlockSpec(memory_space=pl.ANY)


S5_IN_SLICES = S_WIDTH // LANES
S5_IN_GROUPS = LANES // S_GROUP
S5_OUT_HALVES = 2
S5_OUT_GROUPS = S_GROUPS // S5_OUT_HALVES


def _s5_prep_kernel(lr_ref, li_ref, ls_ref, brt_ref, bit_ref, cre_ref, cim_ref,
                    abre_ref, abim_ref, wbu_ref, wcre_ref, wcim_ref):
    lr = lr_ref[0]
    li = li_ref[0]
    dt = jnp.exp(ls_ref[0])
    mag = jnp.exp(lr * dt)
    ang = li * dt
    ab_re = mag * jnp.cos(ang)
    ab_im = mag * jnp.sin(ang)
    den = lr * lr + li * li
    nr = ab_re - 1.0
    f_re = (nr * lr + ab_im * li) / den
    f_im = (ab_im * lr - nr * li) / den
    br = brt_ref[0]
    bi = bit_ref[0]
    abre_ref[0] = ab_re
    abim_ref[0] = ab_im
    bb_re = f_re * br - f_im * bi
    bb_im = f_re * bi + f_im * br

    per_slice = S5_IN_GROUPS * S_STATE
    lane_group = lax.broadcasted_iota(jnp.int32, (S_GROUP, per_slice), 1) >> (S_STATE.bit_length() - 1)
    for k in range(S5_IN_SLICES):
        re_k = bb_re[:, k * per_slice:(k + 1) * per_slice]
        im_k = bb_im[:, k * per_slice:(k + 1) * per_slice]
        for a in range(S5_IN_GROUPS):
            rows = slice(a * S_GROUP, (a + 1) * S_GROUP)
            wbu_ref[0, k, rows, 0:per_slice] = jnp.where(lane_group == a, re_k, 0.0).astype(BF16)
            wbu_ref[0, k, rows, per_slice:2 * per_slice] = jnp.where(lane_group == a, im_k, 0.0).astype(BF16)

    row_group = lax.broadcasted_iota(jnp.int32, (S5_OUT_GROUPS * S_GROUP, S_STATE), 0) >> (S_GROUP.bit_length() - 1)
    for src_ref, dst_ref in ((cre_ref, wcre_ref), (cim_ref, wcim_ref)):
        for m in range(S5_OUT_HALVES):
            c = src_ref[0, m]
            for a in range(S5_OUT_GROUPS):
                dst_ref[0, m, :, a * S_STATE:(a + 1) * S_STATE] = jnp.where(row_group == a, c, 0.0).astype(BF16)


def _s5_prep(s_lam_re, s_lam_im, s_log_step, s_b_re, s_b_im, s_c_re, s_c_im):
    lr = s_lam_re.reshape(DEPTH, 1, S_LANES)
    li = s_lam_im.reshape(DEPTH, 1, S_LANES)
    ls = jnp.repeat(s_log_step, S_STATE, axis=-1).reshape(DEPTH, 1, S_LANES)
    brt = s_b_re.reshape(DEPTH, S_LANES, S_GROUP).transpose(0, 2, 1)
    bit = s_b_im.reshape(DEPTH, S_LANES, S_GROUP).transpose(0, 2, 1)
    c_shape = (DEPTH, S5_OUT_HALVES, S5_OUT_GROUPS * S_GROUP, S_STATE)
    wbu_shape = (DEPTH, S5_IN_SLICES, LANES, 2 * S5_IN_GROUPS * S_STATE)
    wc_shape = (DEPTH, S5_OUT_HALVES, S5_OUT_GROUPS * S_GROUP, S5_OUT_GROUPS * S_STATE)

    def layer_block(shape):
        nd = len(shape) - 1
        return pl.BlockSpec((1,) + shape[1:], lambda l: (l,) + (0,) * nd)

    vec = layer_block((DEPTH, 1, S_LANES))
    mat = layer_block((DEPTH, S_GROUP, S_LANES))
    return pl.pallas_call(
        _s5_prep_kernel,
        grid=(DEPTH,),
        in_specs=[vec, vec, vec, mat, mat, layer_block(c_shape), layer_block(c_shape)],
        out_specs=[vec, vec, layer_block(wbu_shape), layer_block(wc_shape), layer_block(wc_shape)],
        out_shape=[jax.ShapeDtypeStruct((DEPTH, 1, S_LANES), F32),
                   jax.ShapeDtypeStruct((DEPTH, 1, S_LANES), F32),
                   jax.ShapeDtypeStruct(wbu_shape, BF16),
                   jax.ShapeDtypeStruct(wc_shape, BF16),
                   jax.ShapeDtypeStruct(wc_shape, BF16)],
        name="s5_prep",
    )(lr, li, ls, brt, bit, s_c_re.reshape(c_shape), s_c_im.reshape(c_shape))


def _stream_specs(parts, tm):
    if len(parts) == 1:
        return [pl.BlockSpec((tm, D_MODEL), lambda i: (i, 0))]
    nb0 = parts[0].shape[0] // tm
    return [pl.BlockSpec((tm, D_MODEL), lambda i: (jnp.minimum(i, nb0 - 1), 0)),
            pl.BlockSpec((tm, D_MODEL), lambda i: (jnp.maximum(i - nb0, 0), 0))]


def _read_stream(refs, nb0):
    if len(refs) == 1:
        return refs[0][...]
    return jnp.where(pl.program_id(0) < nb0, refs[0][...], refs[1][...])


def _rmsnorm_bf16(x, vec_ref):
    ms = jnp.mean(x * x, axis=-1, keepdims=True)
    return (x * lax.rsqrt(ms + EPS) * _vec(vec_ref, "norm_w")).astype(BF16)


def _inproj_kernel(x_ref, vec_ref, w_ref, o_ref, wb_ref, xn_scr):
    @pl.when(pl.program_id(0) == 0)
    def _():
        xn_scr[...] = _rmsnorm_bf16(x_ref[...], vec_ref)

    wb = w_ref[...].astype(BF16)
    wb_ref[...] = wb
    o_ref[...] = jnp.dot(xn_scr[...], wb, preferred_element_type=F32)


def _inproj(x, row0, nrows, vec, w_in, l):
    blk0 = row0 // nrows
    return pl.pallas_call(
        _inproj_kernel,
        grid=((2 * MIX_WIDTH) // INPROJ_TN,),
        in_specs=[pl.BlockSpec((nrows, D_MODEL), lambda j: (blk0, 0), pipeline_mode=pl.Buffered(1)),
                  _layer_spec(vec, l),
                  pl.BlockSpec((None, D_MODEL, INPROJ_TN), lambda j: (l, 0, j))],
        out_specs=[pl.BlockSpec((nrows, INPROJ_TN), lambda j: (0, j)),
                   pl.BlockSpec((D_MODEL, INPROJ_TN), lambda j: (0, j))],
        out_shape=[jax.ShapeDtypeStruct((nrows, 2 * MIX_WIDTH), F32),
                   jax.ShapeDtypeStruct((D_MODEL, 2 * MIX_WIDTH), BF16)],
        scratch_shapes=[pltpu.VMEM((nrows, D_MODEL), BF16)],
        compiler_params=pltpu.CompilerParams(
            dimension_semantics=("arbitrary",), vmem_limit_bytes=VMEM_LIMIT_BYTES),
        name="inproj",
    )(x, vec, w_in)


def _outproj_kernel(*refs, n_x, nb0, n_out, nb0_out, final):
    mixed_ref, x_refs = refs[0], refs[1:1 + n_x]
    w_ref, fw_ref = refs[1 + n_x:3 + n_x]
    o_refs, wb_scr = refs[3 + n_x:-1], refs[-1]

    @pl.when(pl.program_id(0) == 0)
    def _():
        wb_scr[...] = w_ref[...].astype(BF16)

    y = _read_stream(x_refs, nb0) + jnp.dot(mixed_ref[...], wb_scr[...], preferred_element_type=F32)
    if final:
        ms = jnp.mean(y * y, axis=-1, keepdims=True)
        y = y * lax.rsqrt(ms + EPS) * fw_ref[...]
    if n_out == 1:
        o_refs[0][...] = y
    else:
        @pl.when(pl.program_id(0) < nb0_out)
        def _():
            o_refs[0][...] = y

        @pl.when(pl.program_id(0) >= nb0_out)
        def _():
            o_refs[1][...] = y.reshape(y.shape[0] // SUBLANES, SUBLANES, y.shape[1])[:, SAMPLE_LEAD:, :]


def _outproj(mixed, x_parts, w_out, l, final_w, final, split_rows=None):
    n = mixed.shape[0]
    if split_rows is None:
        out_parts = [jax.ShapeDtypeStruct((n, D_MODEL), F32)]
        out_specs = [pl.BlockSpec((OUTPROJ_TM, D_MODEL), lambda i: (i, 0))]
        nb0_out = n // OUTPROJ_TM
    else:
        nb0_out = split_rows // OUTPROJ_TM
        tokens = SUBLANES - SAMPLE_LEAD
        out_parts = [jax.ShapeDtypeStruct((split_rows, D_MODEL), F32),
                     jax.ShapeDtypeStruct(((n - split_rows) // SUBLANES, tokens, D_MODEL), F32)]
        out_specs = [pl.BlockSpec((OUTPROJ_TM, D_MODEL), lambda i: (jnp.minimum(i, nb0_out - 1), 0)),
                     pl.BlockSpec((OUTPROJ_TM // SUBLANES, tokens, D_MODEL),
                                  lambda i: (jnp.maximum(i - nb0_out, 0), 0, 0))]
    return pl.pallas_call(
        functools.partial(_outproj_kernel, n_x=len(x_parts), nb0=x_parts[0].shape[0] // OUTPROJ_TM,
                          n_out=len(out_parts), nb0_out=nb0_out, final=final),
        grid=(n // OUTPROJ_TM,),
        in_specs=([pl.BlockSpec((OUTPROJ_TM, MIX_WIDTH), lambda i: (i, 0))] + _stream_specs(x_parts, OUTPROJ_TM)
                  + [_layer_spec(w_out, l), pl.BlockSpec((1, D_MODEL), lambda i: (0, 0))]),
        out_specs=out_specs,
        out_shape=out_parts,
        scratch_shapes=[pltpu.VMEM((MIX_WIDTH, D_MODEL), BF16)],
        compiler_params=pltpu.CompilerParams(
            dimension_semantics=("arbitrary",), vmem_limit_bytes=VMEM_LIMIT_BYTES),
        name="outproj",
    )(mixed, *x_parts, w_out, final_w)


def _silu(x):
    return x * jax.nn.sigmoid(x)


def _log_sigmoid(x):
    return jnp.minimum(x, 0.0) - jnp.log1p(jnp.exp(-jnp.abs(x)))


def _softplus(x):
    return jnp.maximum(x, 0.0) + jnp.log1p(jnp.exp(-jnp.abs(x)))


def _sublane_pos(shape):
    return lax.broadcasted_iota(jnp.int32, shape, 0) & (SUBLANES - 1)


def _tile_cumsum(x):
    pos = _sublane_pos(x.shape)
    s = 1
    while s < SUBLANES:
        x = x + jnp.where(pos >= s, pltpu.roll(x, s, 0), 0.0)
        s *= 2
    return x


def _tile_cummax(x):
    pos = _sublane_pos(x.shape)
    s = 1
    while s < SUBLANES:
        x = jnp.maximum(x, jnp.where(pos >= s, pltpu.roll(x, s, 0), -jnp.inf))
        s *= 2
    return x


def _tile_scan_real(a, b):
    pos = _sublane_pos(a.shape)
    s = 1
    while s < SUBLANES:
        m = pos >= s
        b = jnp.where(m, a * pltpu.roll(b, s, 0) + b, b)
        if 2 * s < SUBLANES:
            a = jnp.where(m, a * pltpu.roll(a, s, 0), a)
        s *= 2
    return b


def _tile_scan_cplx(sr, si, pr, pi):
    pos = _sublane_pos(sr.shape)
    s = 1
    while s < SUBLANES:
        m = pos >= s
        sr_sh = pltpu.roll(sr, s, 0)
        si_sh = pltpu.roll(si, s, 0)
        sr, si = (jnp.where(m, sr + (pr * sr_sh - pi * si_sh), sr),
                  jnp.where(m, si + (pr * si_sh + pi * sr_sh), si))
        if 2 * s < SUBLANES:
            pr, pi = pr * pr - pi * pi, 2.0 * (pr * pi)
        s *= 2
    return sr, si


def _mlstm_project(xc, xm, vec_ref, wqk_ref, wvo_ref, wif_ref, q_scr, k_scr, v_scr, o_scr):
    xc_b = xc.astype(BF16)
    xm_b = xm.astype(BF16)
    for h in range(M_HEADS):
        sl = slice(h * M_HEAD_DIM, (h + 1) * M_HEAD_DIM)
        qk = jnp.dot(xc_b[:, sl], wqk_ref[h], preferred_element_type=F32)
        vo = jnp.dot(xm_b[:, sl], wvo_ref[h], preferred_element_type=F32)
        q_scr[:, sl] = qk[:, :M_HEAD_DIM]
        k_scr[:, sl] = qk[:, M_HEAD_DIM:]
        v_scr[:, sl] = vo[:, :M_HEAD_DIM]
        o_scr[:, sl] = vo[:, M_HEAD_DIM:]
    return (jnp.dot(q_scr[...].astype(BF16), wif_ref[0:M_WIDTH, :], preferred_element_type=F32)
            + jnp.dot(k_scr[...].astype(BF16), wif_ref[M_WIDTH:2 * M_WIDTH, :], preferred_element_type=F32)
            + jnp.dot(v_scr[...].astype(BF16), wif_ref[2 * M_WIDTH:3 * M_WIDTH, :], preferred_element_type=F32)
            + _vec(vec_ref, "bif"))


def _head_output(hh, o_pre, xc_h, z_h, mnw_h, mskip_h):
    mu = jnp.mean(hh, axis=1, keepdims=True)
    hc = hh - mu
    var = jnp.mean(hc * hc, axis=1, keepdims=True)
    hn = hc * lax.rsqrt(var + EPS) * mnw_h
    return (jax.nn.sigmoid(o_pre) * hn + mskip_h * xc_h) * _silu(z_h)


def _rglru_coeffs(xcr, vec_ref, rwa_ref, rwx_ref):
    xcr_b = xcr.astype(BF16)
    half = R_WIDTH // 2
    ra_pre = jnp.concatenate(
        [jnp.dot(xcr_b[:, :half], rwa_ref[0], preferred_element_type=F32),
         jnp.dot(xcr_b[:, half:], rwa_ref[1], preferred_element_type=F32)], axis=1) + _vec(vec_ref, "rba")
    rx_pre = jnp.concatenate(
        [jnp.dot(xcr_b[:, :half], rwx_ref[0], preferred_element_type=F32),
         jnp.dot(xcr_b[:, half:], rwx_ref[1], preferred_element_type=F32)], axis=1) + _vec(vec_ref, "rbx")
    log_a = (-RG_C) * jax.nn.sigmoid(ra_pre) * _softplus(-_vec(vec_ref, "rlam"))
    a = jnp.exp(log_a)
    th = jnp.tanh(log_a)
    one_minus_a2 = (-2.0 * th) / (1.0 - th)
    return a, jnp.sqrt(one_minus_a2) * (jax.nn.sigmoid(rx_pre) * xcr)


def _s5_project(u_b, wbu_ref, re_ref, im_ref):
    per_slice = (LANES // S_GROUP) * S_STATE
    for k in range(S_WIDTH // LANES):
        res = jnp.dot(u_b[:, k * LANES:(k + 1) * LANES], wbu_ref[k], preferred_element_type=F32)
        re_ref[:, k * per_slice:(k + 1) * per_slice] = res[:, :per_slice]
        im_ref[:, k * per_slice:(k + 1) * per_slice] = res[:, per_slice:]


def _s5_output(re_ref, im_ref, u, vec_ref, wcre_ref, wcim_ref, wglu_ref):
    nblk = wcre_ref.shape[0]
    k_blk = S_LANES // nblk
    parts = []
    for m in range(nblk):
        ks = slice(m * k_blk, (m + 1) * k_blk)
        parts.append(lax.dot_general(re_ref[:, ks].astype(BF16), wcre_ref[m], _NT, preferred_element_type=F32)
                     - lax.dot_general(im_ref[:, ks].astype(BF16), wcim_ref[m], _NT, preferred_element_type=F32))
    y = jnp.concatenate(parts, axis=1) + _vec(vec_ref, "sd") * u
    g = jax.nn.gelu(y)
    return g * jax.nn.sigmoid(jnp.dot(g.astype(BF16), wglu_ref[...], preferred_element_type=F32)
                              + _vec(vec_ref, "bglu"))


_MATRIX_KEYS = ("wqk", "wvo", "wif", "rwa", "rwx", "wbu", "wcre", "wcim", "wglu")


def _tile(x, i):
    return x[i * SUBLANES:(i + 1) * SUBLANES]


def _conv_interleaved(x, tail_ref, w, bias, rows):
    ntiles = rows // SUBLANES
    pos = lax.broadcasted_iota(jnp.int32, (SUBLANES, x.shape[1]), 0)
    prev = tail_ref[...]
    before = [pltpu.roll(jnp.where(pos == SUBLANES - 1, _tile(prev, CONV_W - 1 - d), _tile(x, ntiles - d)), 1, 0)
              for d in range(1, CONV_W)]
    out = w[CONV_W - 1:CONV_W, :] * x + bias
    for j in range(1, CONV_W):
        shifted = jnp.concatenate(before[:j][::-1] + [x[:rows - j * SUBLANES]], axis=0)
        out = out + w[CONV_W - 1 - j:CONV_W - j, :] * shifted
    tail_ref[...] = x[rows - (CONV_W - 1) * SUBLANES:]
    return out


def _cumsum_interleaved(x, rows):
    tiles = [_tile(x, 0)]
    for i in range(1, rows // SUBLANES):
        tiles.append(tiles[-1] + _tile(x, i))
    total = tiles[-1]
    start = _tile_cumsum(total) - total
    return jnp.concatenate([t + start for t in tiles], axis=0)


def _cummax_interleaved(x, rows):
    tiles = [_tile(x, 0)]
    for i in range(1, rows // SUBLANES):
        tiles.append(jnp.maximum(tiles[-1], _tile(x, i)))
    best = tiles[-1]
    pos = lax.broadcasted_iota(jnp.int32, best.shape, 0)
    s = 1
    while s < SUBLANES:
        best = jnp.maximum(best, jnp.where(pos >= s, pltpu.roll(best, s, 0), -jnp.inf))
        s *= 2
    start = jnp.where(pos == 0, -jnp.inf, pltpu.roll(best, 1, 0))
    return jnp.concatenate([jnp.maximum(t, start) for t in tiles], axis=0)


def _scan_real_interleaved(a_ref, b_ref, carry, rows):
    ntiles = rows // SUBLANES
    pos = lax.broadcasted_iota(jnp.int32, (SUBLANES, a_ref.shape[1]), 0)
    h = _tile(b_ref, 0)
    aprod = _tile(a_ref, 0)
    for i in range(1, ntiles):
        a = _tile(a_ref, i)
        h = a * h + _tile(b_ref, i)
        aprod = aprod * a
    g = _tile_scan_real(aprod, h + jnp.where(pos == 0, aprod * carry, 0.0))
    h = jnp.where(pos == 0, carry, pltpu.roll(g, 1, 0))
    for i in range(ntiles):
        h = _tile(a_ref, i) * h + _tile(b_ref, i)
        b_ref[i * SUBLANES:(i + 1) * SUBLANES, :] = h
    return g[SUBLANES - 1:SUBLANES]


def _scan_cplx_interleaved(re_ref, im_ref, p_re, p_im, cre_ref, cim_ref, rows):
    ntiles = rows // SUBLANES
    width = re_ref.shape[1]
    pos = lax.broadcasted_iota(jnp.int32, (SUBLANES, SCAN_LANE_BLOCK), 0)
    for blk in range(width // SCAN_LANE_BLOCK):
        sl = slice(blk * SCAN_LANE_BLOCK, (blk + 1) * SCAN_LANE_BLOCK)
        pr = jnp.broadcast_to(p_re[:, sl], pos.shape)
        pi = jnp.broadcast_to(p_im[:, sl], pos.shape)
        c_r = cre_ref[:, sl]
        c_i = cim_ref[:, sl]
        sr = re_ref[0:SUBLANES, sl]
        si = im_ref[0:SUBLANES, sl]
        for i in range(1, ntiles):
            rs = slice(i * SUBLANES, (i + 1) * SUBLANES)
            sr, si = pr * sr - pi * si + re_ref[rs, sl], pr * si + pi * sr + im_ref[rs, sl]
        qr, qi = pr, pi
        n = 1
        while n < ntiles:
            qr, qi = qr * qr - qi * qi, 2.0 * (qr * qi)
            n *= 2
        gr, gi = _tile_scan_cplx(sr + jnp.where(pos == 0, qr * c_r - qi * c_i, 0.0),
                                 si + jnp.where(pos == 0, qr * c_i + qi * c_r, 0.0), qr, qi)
        sr = jnp.where(pos == 0, c_r, pltpu.roll(gr, 1, 0))
        si = jnp.where(pos == 0, c_i, pltpu.roll(gi, 1, 0))
        for i in range(ntiles):
            rs = slice(i * SUBLANES, (i + 1) * SUBLANES)
            sr, si = pr * sr - pi * si + re_ref[rs, sl], pr * si + pi * sr + im_ref[rs, sl]
            re_ref[rs, sl] = sr
            im_ref[rs, sl] = si
        cre_ref[:, sl] = gr[SUBLANES - 1:SUBLANES]
        cim_ref[:, sl] = gi[SUBLANES - 1:SUBLANES]


def _prompt_mixer_kernel(*refs, rows, nchunk, n_alias):
    (xnext_ref, xfirst_ref, vec_ref, win_ref,
     wqk_ref, wvo_ref, wif_ref, rwa_ref, rwx_ref, wbu_ref, wcre_ref, wcim_ref, wglu_ref) = refs[:13]
    (mixed_ref, c_ref, n_ref, m_ref, mtail_ref, h_ref, rtail_ref, sre_ref, sim_ref,
     q_scr, k_scr, v_scr, o_scr, ra_scr, rb_scr, ure_scr, uim_scr, proj_scr, xn_scr) = refs[13 + n_alias:]
    T = rows
    step = pl.program_id(0)
    slot = lax.rem(step, 2)
    proj_ref = proj_scr.at[slot]
    next_proj_ref = proj_scr.at[1 - slot]
    slab = (2 * MIX_WIDTH) // M_HEADS

    @pl.when(step == 0)
    def _():
        xn = _rmsnorm_bf16(xfirst_ref[...], vec_ref)
        for j in range(M_HEADS):
            cols = slice(j * slab, (j + 1) * slab)
            proj_scr[0, :, cols] = jnp.dot(xn, win_ref[:, cols], preferred_element_type=F32)

    @pl.when(lax.rem(step, nchunk) == 0)
    def _():
        for ref in (c_ref, n_ref, m_ref, mtail_ref, h_ref, rtail_ref, sre_ref, sim_ref):
            ref[...] = jnp.zeros(ref.shape, F32)

    xn_scr[...] = _rmsnorm_bf16(xnext_ref[...], vec_ref)

    xm = proj_ref[:, 0:M_WIDTH]
    xc = _silu(_conv_interleaved(xm, mtail_ref, _vec(vec_ref, "mcw"), _vec(vec_ref, "mcb"), T))
    gates = _mlstm_project(xc, xm, vec_ref, wqk_ref, wvo_ref, wif_ref, q_scr, k_scr, v_scr, o_scr)
    log_i = gates[:, :LANES]
    b_all = _cumsum_interleaved(_log_sigmoid(gates[:, LANES:]), T)
    a_max = _cummax_interleaved(log_i - b_all, T)
    m_prev = m_ref[...]
    log_inter = b_all + m_prev
    m_t_all = jnp.maximum(log_inter, b_all + a_max)
    w_inter_all = jnp.exp(log_inter - m_t_all)
    floor_all = jnp.exp(-m_t_all)
    b_end = b_all[T - 1:T, :]
    m_new = jnp.maximum(b_end + m_prev, b_end + a_max[T - 1:T, :])
    w_src_all = jnp.exp(b_end - b_all + log_i - m_new)
    decay_all = jnp.exp(b_end + m_prev - m_new)
    m_ref[...] = m_new
    b_t = b_all.T
    li_t = log_i.T

    rowi = lax.broadcasted_iota(jnp.int32, (T, T), 0)
    coli = lax.broadcasted_iota(jnp.int32, (T, T), 1)
    sub_len = T // SUBLANES

    def time_of(r):
        return (r & (SUBLANES - 1)) * sub_len + (r >> SUBLANE_SHIFT)

    causal = time_of(coli) <= time_of(rowi)
    k_scale = M_HEAD_DIM ** -0.5
    mnw = _vec(vec_ref, "mnw")
    mskip = _vec(vec_ref, "mskip")

    for h in range(M_HEADS):
        sl = slice(h * M_HEAD_DIM, (h + 1) * M_HEAD_DIM)
        c_prev = c_ref[h]
        n_prev = n_ref[h:h + 1, :]
        w_inter = w_inter_all[:, h:h + 1]
        w_src = w_src_all[:, h:h + 1]
        decay = decay_all[:, h:h + 1]

        q = q_scr[:, sl]
        ks = k_scr[:, sl] * k_scale
        v = v_scr[:, sl]
        q_b = q.astype(BF16)
        ks_b = ks.astype(BF16)

        log_d = jnp.where(causal, b_all[:, h:h + 1] - b_t[h:h + 1, :] + li_t[h:h + 1, :], -jnp.inf)
        w_intra = jnp.exp(log_d - m_t_all[:, h:h + 1])
        s = lax.dot_general(q_b, ks_b, _NT, preferred_element_type=F32) * w_intra
        inter = lax.dot_general(q_b, c_prev.astype(BF16), _NT, preferred_element_type=F32)
        num = jnp.dot(s.astype(BF16), v.astype(BF16), preferred_element_type=F32) + w_inter * inter
        den = (jnp.sum(s, axis=1, keepdims=True)
               + w_inter * jnp.sum(q * n_prev, axis=1, keepdims=True))
        hh = num / jnp.maximum(jnp.abs(den), floor_all[:, h:h + 1])

        vw_t = (v * w_src).T.astype(BF16)
        c_ref[h] = decay * c_prev + jnp.dot(vw_t, ks_b, preferred_element_type=F32)
        n_ref[h:h + 1, :] = decay * n_prev + jnp.sum(ks * w_src, axis=0, keepdims=True)

        z = proj_ref[:, MIX_WIDTH + h * M_HEAD_DIM:MIX_WIDTH + (h + 1) * M_HEAD_DIM]
        mixed_ref[:, sl] = _head_output(hh, o_scr[:, sl], xc[:, sl], z, mnw[:, sl], mskip[:, sl]).astype(BF16)

        cols = slice(h * slab, (h + 1) * slab)
        next_proj_ref[:, cols] = jnp.dot(xn_scr[...], win_ref[:, cols], preferred_element_type=F32)

    xr = proj_ref[:, M_WIDTH:M_WIDTH + R_WIDTH]
    xcr = _conv_interleaved(xr, rtail_ref, _vec(vec_ref, "rcw"), _vec(vec_ref, "rcb"), T)
    a, bb = _rglru_coeffs(xcr, vec_ref, rwa_ref, rwx_ref)
    ra_scr[...] = a
    rb_scr[...] = bb
    h_ref[...] = _scan_real_interleaved(ra_scr, rb_scr, h_ref[...], T)
    zr = proj_ref[:, MIX_WIDTH + M_WIDTH:MIX_WIDTH + M_WIDTH + R_WIDTH]
    mixed_ref[:, M_WIDTH:M_WIDTH + R_WIDTH] = (rb_scr[...] * _silu(zr)).astype(BF16)

    u = proj_ref[:, M_WIDTH + R_WIDTH:MIX_WIDTH]
    _s5_project(u.astype(BF16), wbu_ref, ure_scr, uim_scr)
    _scan_cplx_interleaved(ure_scr, uim_scr, _vec(vec_ref, "abre"), _vec(vec_ref, "abim"), sre_ref, sim_ref, T)
    glu = _s5_output(ure_scr, uim_scr, u, vec_ref, wcre_ref, wcim_ref, wglu_ref)
    zs = proj_ref[:, MIX_WIDTH + M_WIDTH + R_WIDTH:2 * MIX_WIDTH]
    mixed_ref[:, M_WIDTH + R_WIDTH:MIX_WIDTH] = (glu * _silu(zs)).astype(BF16)


_PROMPT_STATE_SHAPES = ((M_HEADS, M_HEAD_DIM, M_HEAD_DIM), (M_HEADS, M_HEAD_DIM), (1, LANES),
                        ((CONV_W - 1) * SUBLANES, M_WIDTH), (1, R_WIDTH), ((CONV_W - 1) * SUBLANES, R_WIDTH),
                        (1, S_LANES), (1, S_LANES))


def _prompt_mixer(x, total_rows, l, bsz, seq, vec, w_in_b, mats, prev_states):
    rows = PROMPT_CHUNK
    nchunk = seq // rows
    nstep = bsz * nchunk
    aliases = {}
    alias_args = []
    n_in = 4 + len(mats)
    if prev_states is not None:
        alias_args = list(prev_states)
        aliases = {n_in + k: 1 + k for k in range(len(alias_args))}

    def state_spec(shape):
        nd = len(shape)
        return pl.BlockSpec((None, None) + shape, lambda t: (l, t // nchunk) + (0,) * nd)

    out_shape = ([jax.ShapeDtypeStruct((total_rows, MIX_WIDTH), BF16)]
                 + [jax.ShapeDtypeStruct((DEPTH, bsz) + s, F32) for s in _PROMPT_STATE_SHAPES])
    scratch = ([pltpu.VMEM((rows, w), F32)
                for w in (M_WIDTH, M_WIDTH, M_WIDTH, M_WIDTH, R_WIDTH, R_WIDTH, S_LANES, S_LANES)]
               + [pltpu.VMEM((2, rows, 2 * MIX_WIDTH), F32), pltpu.VMEM((rows, D_MODEL), BF16)])
    return pl.pallas_call(
        functools.partial(_prompt_mixer_kernel, rows=rows, nchunk=nchunk, n_alias=len(alias_args)),
        grid=(nstep,),
        in_specs=([pl.BlockSpec((rows, D_MODEL), lambda t: (jnp.minimum(t + 1, nstep - 1), 0)),
                   pl.BlockSpec((rows, D_MODEL), lambda t: (0, 0), pipeline_mode=pl.Buffered(1)),
                   _layer_spec(vec, l),
                   pl.BlockSpec(w_in_b.shape, lambda t: (0, 0), pipeline_mode=pl.Buffered(1))]
                  + [_layer_spec(w, l) for w in mats] + [_ANY_SPEC] * len(alias_args)),
        out_specs=([pl.BlockSpec((rows, MIX_WIDTH), lambda t: (t, 0))]
                   + [state_spec(s) for s in _PROMPT_STATE_SHAPES]),
        out_shape=out_shape,
        scratch_shapes=scratch,
        input_output_aliases=aliases,
        compiler_params=pltpu.CompilerParams(
            dimension_semantics=("arbitrary",), vmem_limit_bytes=VMEM_LIMIT_BYTES),
        name="prompt_mixer",
    )(x, x, vec, w_in_b, *mats, *alias_args)


def _seg_last(x, groups):
    x3 = x.reshape(groups, SUBLANES, x.shape[-1])
    return jnp.broadcast_to(x3[:, SUBLANES - 1:SUBLANES, :], x3.shape).reshape(x.shape)


def _seg_max(x, groups):
    x3 = x.reshape(groups, SUBLANES, x.shape[-1])
    return jnp.broadcast_to(jnp.max(x3, axis=1, keepdims=True), x3.shape).reshape(x.shape)


def _seg_sum(x, groups):
    x3 = x.reshape(groups, SUBLANES, x.shape[-1])
    return jnp.broadcast_to(jnp.sum(x3, axis=1, keepdims=True), x3.shape).reshape(x.shape)


def _seg_rows(state_ref, lanes=slice(None)):
    x = state_ref[:, :, lanes]
    return jnp.broadcast_to(x, (x.shape[0], SUBLANES, x.shape[2])).reshape(x.shape[0] * SUBLANES, x.shape[2])


def _seg_state(x):
    x3 = x.reshape(x.shape[0] // SUBLANES, SUBLANES, x.shape[-1])
    return x3[:, SUBLANES - 1:SUBLANES, :]


def _with_history(x, buf_ref, new_buf_ref):
    g = x.shape[0] // SUBLANES
    x3 = x.reshape(g, SUBLANES, x.shape[1])
    lead = SAMPLE_LEAD - (CONV_W - 1)
    full = jnp.concatenate([jnp.zeros((g, lead, x.shape[1]), F32), buf_ref[...], x3[:, SAMPLE_LEAD:, :]], axis=1)
    new_buf_ref[...] = full[:, SUBLANES - (CONV_W - 1):, :]
    return full.reshape(x.shape)


def _conv_rolled(xf, w, bias):
    out = w[CONV_W - 1:CONV_W, :] * xf + bias
    for j in range(1, CONV_W):
        out = out + w[CONV_W - 1 - j:CONV_W - j, :] * pltpu.roll(xf, j, 0)
    return out


_SAMPLE_STATE_WIDTHS = (M_WIDTH, LANES, M_WIDTH, R_WIDTH, R_WIDTH, S_LANES, S_LANES)
_SAMPLE_STATE_ROWS = (1, 1, CONV_W - 1, 1, CONV_W - 1, 1, 1)


def _sample_mixer_kernel(*refs, rows, c_seqs, n_alias):
    (proj_ref, c0_ref, n0_ref, m0_ref, mtail0_ref, h0_ref, rtail0_ref, sre0_ref, sim0_ref,
     vec_ref, wqk_ref, wvo_ref, wif_ref, rwa_ref, rwx_ref, wbu_ref, wcre_ref, wcim_ref, wglu_ref) = refs[:19]
    (mixed_ref, c_ref, n_ref, m_ref, mconv_ref, h_ref, rconv_ref, sre_ref, sim_ref,
     q_scr, k_scr, v_scr, o_scr, xc_scr, intert_scr, vwt_scr, qt_scr, ksb_scr, dec_scr, b_scr, li_scr,
     ure_scr, uim_scr) = refs[19 + n_alias:]
    R = rows
    G = R // SUBLANES
    sub = pl.program_id(1)
    k_scale = M_HEAD_DIM ** -0.5
    pos = _sublane_pos((R, 1))
    valid = pos >= SAMPLE_LEAD
    first_token = pos == SAMPLE_LEAD

    @pl.when(sub == 0)
    def _():
        xm = _with_history(proj_ref[:, 0:M_WIDTH], mtail0_ref, mconv_ref)
        xc = _silu(_conv_rolled(xm, _vec(vec_ref, "mcw"), _vec(vec_ref, "mcb")))
        xc_scr[...] = xc
        gates = _mlstm_project(xc, xm, vec_ref, wqk_ref, wvo_ref, wif_ref, q_scr, k_scr, v_scr, o_scr)
        log_i = jnp.where(valid, gates[:, :LANES], -jnp.inf)
        log_f = jnp.where(valid, _log_sigmoid(gates[:, LANES:]), 0.0)
        b_all = _tile_cumsum(log_f)
        b_end = _seg_last(b_all, G)
        m_rows = _seg_rows(m0_ref)
        n_rows = _seg_rows(n0_ref)
        log_src = b_end - b_all + log_i
        m_new = jnp.maximum(b_end + m_rows, _seg_max(log_src, G))
        w_src_all = jnp.exp(log_src - m_new)
        decay_all = jnp.exp(b_end + m_rows - m_new)
        dec_scr[...] = decay_all
        b_scr[...] = b_all
        li_scr[...] = log_i
        m_ref[...] = _seg_state(m_new)
        for h in range(M_HEADS):
            sl = slice(h * M_HEAD_DIM, (h + 1) * M_HEAD_DIM)
            ks = k_scr[:, sl] * k_scale
            w_src = w_src_all[:, h:h + 1]
            n_ref[:, :, sl] = _seg_state(decay_all[:, h:h + 1] * n_rows[:, sl] + _seg_sum(ks * w_src, G))
            vwt_scr[h] = (v_scr[:, sl] * w_src).T
            qt_scr[h] = q_scr[:, sl].T.astype(BF16)
            ksb_scr[h] = ks.astype(BF16)
            intert_scr[h] = jnp.zeros((M_HEAD_DIM, R), F32)

        xr = _with_history(proj_ref[:, M_WIDTH:M_WIDTH + R_WIDTH], rtail0_ref, rconv_ref)
        xcr = _conv_rolled(xr, _vec(vec_ref, "rcw"), _vec(vec_ref, "rcb"))
        a, bb = _rglru_coeffs(xcr, vec_ref, rwa_ref, rwx_ref)
        hs = _tile_scan_real(a, jnp.where(valid, bb, 0.0) + jnp.where(first_token, a * _seg_rows(h0_ref), 0.0))
        h_ref[...] = _seg_state(hs)
        zr = proj_ref[:, MIX_WIDTH + M_WIDTH:MIX_WIDTH + M_WIDTH + R_WIDTH]
        mixed_ref[:, M_WIDTH:M_WIDTH + R_WIDTH] = jnp.where(valid, hs * _silu(zr), 0.0).astype(BF16)

        u = proj_ref[:, M_WIDTH + R_WIDTH:MIX_WIDTH]
        _s5_project(u.astype(BF16), wbu_ref, ure_scr, uim_scr)
        p_re = _vec(vec_ref, "abre")
        p_im = _vec(vec_ref, "abim")
        for blk in range(S_LANES // SCAN_LANE_BLOCK):
            sl = slice(blk * SCAN_LANE_BLOCK, (blk + 1) * SCAN_LANE_BLOCK)
            pr = jnp.broadcast_to(p_re[:, sl], (R, SCAN_LANE_BLOCK))
            pi = jnp.broadcast_to(p_im[:, sl], (R, SCAN_LANE_BLOCK))
            s0r = _seg_rows(sre0_ref, sl)
            s0i = _seg_rows(sim0_ref, sl)
            sr, si = _tile_scan_cplx(
                jnp.where(valid, ure_scr[:, sl], 0.0) + jnp.where(first_token, pr * s0r - pi * s0i, 0.0),
                jnp.where(valid, uim_scr[:, sl], 0.0) + jnp.where(first_token, pr * s0i + pi * s0r, 0.0), pr, pi)
            ure_scr[:, sl] = sr
            uim_scr[:, sl] = si
            sre_ref[:, :, sl] = _seg_state(sr)
            sim_ref[:, :, sl] = _seg_state(si)
        glu = _s5_output(ure_scr, uim_scr, u, vec_ref, wcre_ref, wcim_ref, wglu_ref)
        zs = proj_ref[:, MIX_WIDTH + M_WIDTH + R_WIDTH:2 * MIX_WIDTH]
        mixed_ref[:, M_WIDTH + R_WIDTH:MIX_WIDTH] = jnp.where(valid, glu * _silu(zs), 0.0).astype(BF16)

    lane_seq = lax.broadcasted_iota(jnp.int32, (M_HEAD_DIM, R), 1) >> SUBLANE_SHIFT
    seq0 = sub * c_seqs
    for h in range(M_HEADS):
        c_old = c0_ref[:, h].reshape(c_seqs * M_HEAD_DIM, M_HEAD_DIM)
        readout = jnp.dot(c_old.astype(BF16), qt_scr[h], preferred_element_type=F32)
        acc = intert_scr[h]
        vwt = vwt_scr[h]
        lhs = []
        for s in range(c_seqs):
            own = lane_seq == seq0 + s
            acc = jnp.where(own, readout[s * M_HEAD_DIM:(s + 1) * M_HEAD_DIM], acc)
            lhs.append(jnp.where(own, vwt, 0.0))
        intert_scr[h] = acc
        upd = jnp.dot(jnp.concatenate(lhs, axis=0).astype(BF16), ksb_scr[h], preferred_element_type=F32)
        for s in range(c_seqs):
            r0 = pl.multiple_of((seq0 + s) * SUBLANES, SUBLANES)
            c_ref[s, h] = (dec_scr[pl.ds(r0, 1), h:h + 1] * c0_ref[s, h]
                           + upd[s * M_HEAD_DIM:(s + 1) * M_HEAD_DIM])

    @pl.when(sub == pl.num_programs(1) - 1)
    def _():
        _sample_heads(proj_ref, m0_ref, n0_ref, vec_ref, mixed_ref,
                      q_scr, k_scr, v_scr, o_scr, xc_scr, intert_scr, b_scr, li_scr, valid, R)


def _sample_heads(proj_ref, m0_ref, n0_ref, vec_ref, mixed_ref,
                  q_scr, k_scr, v_scr, o_scr, xc_scr, intert_scr, b_scr, li_scr, valid, R):
    rowi = lax.broadcasted_iota(jnp.int32, (R, R), 0)
    coli = lax.broadcasted_iota(jnp.int32, (R, R), 1)
    same_causal = jnp.logical_and(coli <= rowi, (coli >> SUBLANE_SHIFT) == (rowi >> SUBLANE_SHIFT))
    k_scale = M_HEAD_DIM ** -0.5
    mnw = _vec(vec_ref, "mnw")
    mskip = _vec(vec_ref, "mskip")
    n_rows = _seg_rows(n0_ref)
    b_all = b_scr[...]
    log_i = li_scr[...]
    log_inter = b_all + _seg_rows(m0_ref)
    m_t_all = jnp.maximum(log_inter, b_all + _tile_cummax(log_i - b_all))
    w_inter_all = jnp.exp(log_inter - m_t_all)
    floor_all = jnp.exp(-m_t_all)
    b_t = b_all.T
    li_t = log_i.T

    for h in range(M_HEADS):
        sl = slice(h * M_HEAD_DIM, (h + 1) * M_HEAD_DIM)
        q = q_scr[:, sl]
        ks = k_scr[:, sl] * k_scale
        v = v_scr[:, sl]
        q_b = q.astype(BF16)
        ks_b = ks.astype(BF16)
        w_inter = w_inter_all[:, h:h + 1]

        log_d = jnp.where(same_causal, b_all[:, h:h + 1] - b_t[h:h + 1, :] + li_t[h:h + 1, :], -jnp.inf)
        w_intra = jnp.exp(log_d - m_t_all[:, h:h + 1])
        s = lax.dot_general(q_b, ks_b, _NT, preferred_element_type=F32) * w_intra
        num = (jnp.dot(s.astype(BF16), v.astype(BF16), preferred_element_type=F32)
               + w_inter * intert_scr[h].T)
        den = (jnp.sum(s, axis=1, keepdims=True)
               + w_inter * jnp.sum(q * n_rows[:, sl], axis=1, keepdims=True))
        hh = num / jnp.maximum(jnp.abs(den), floor_all[:, h:h + 1])

        z = proj_ref[:, MIX_WIDTH + h * M_HEAD_DIM:MIX_WIDTH + (h + 1) * M_HEAD_DIM]
        out = _head_output(hh, o_scr[:, sl], xc_scr[:, sl], z, mnw[:, sl], mskip[:, sl])
        mixed_ref[:, sl] = jnp.where(valid, out, 0.0).astype(BF16)


def _sample_mixer(proj, l, row0, mixed_prev, c_all, row_state, vec, mats, prev_out):
    R = SAMPLE_ROWS
    nrows = proj.shape[0]
    nsub = (R // SUBLANES) // SAMPLE_C_SEQS
    blk0 = row0 // R
    c_spec = pl.BlockSpec((None, SAMPLE_C_SEQS, M_HEADS, M_HEAD_DIM, M_HEAD_DIM),
                          lambda i, j: (l, i * nsub + j, 0, 0, 0))

    def state_spec(width, rows_per_seq):
        return pl.BlockSpec((None, R // SUBLANES, rows_per_seq, width), lambda i, j: (l, i, 0, 0))

    def state_shape(width, rows_per_seq):
        return (DEPTH, nrows // SUBLANES, rows_per_seq, width)

    state_specs = [state_spec(w, r) for w, r in zip(_SAMPLE_STATE_WIDTHS, _SAMPLE_STATE_ROWS)]
    alias_args = [mixed_prev] + (list(prev_out) if prev_out is not None else [])
    first_alias = 2 + len(row_state) + 1 + len(mats)
    aliases = {first_alias + k: k for k in range(len(alias_args))}
    out_shape = ([jax.ShapeDtypeStruct(mixed_prev.shape, BF16), jax.ShapeDtypeStruct(c_all.shape, F32)]
                 + [jax.ShapeDtypeStruct(state_shape(w, r), F32)
                    for w, r in zip(_SAMPLE_STATE_WIDTHS, _SAMPLE_STATE_ROWS)])
    scratch = ([pltpu.VMEM((R, M_WIDTH), F32)] * 5
               + [pltpu.VMEM((M_HEADS, M_HEAD_DIM, R), F32), pltpu.VMEM((M_HEADS, M_HEAD_DIM, R), F32),
                  pltpu.VMEM((M_HEADS, M_HEAD_DIM, R), BF16), pltpu.VMEM((M_HEADS, R, M_HEAD_DIM), BF16)]
               + [pltpu.VMEM((R, LANES), F32)] * 3
               + [pltpu.VMEM((R, S_LANES), F32)] * 2)
    return pl.pallas_call(
        functools.partial(_sample_mixer_kernel, rows=R, c_seqs=SAMPLE_C_SEQS, n_alias=len(alias_args)),
        grid=(nrows // R, nsub),
        in_specs=([pl.BlockSpec((R, 2 * MIX_WIDTH), lambda i, j: (i, 0)), c_spec] + state_specs
                  + [_layer_spec(vec, l)] + [_layer_spec(w, l) for w in mats] + [_ANY_SPEC] * len(alias_args)),
        out_specs=[pl.BlockSpec((R, MIX_WIDTH), lambda i, j: (blk0 + i, 0)), c_spec] + state_specs,
        out_shape=out_shape,
        scratch_shapes=scratch,
        input_output_aliases=aliases,
        compiler_params=pltpu.CompilerParams(
            dimension_semantics=("parallel", "arbitrary"), vmem_limit_bytes=VMEM_LIMIT_BYTES),
        name="sample_mixer",
    )(proj, c_all, *row_state, vec, *mats, *alias_args)


def _prepare_weights(p, s5):
    abre, abim, wbu, wcre, wcim = s5

    def split_gates(g):
        pad = [(0, 0)] * (g.ndim - 1) + [(0, LANES - M_HEADS)]
        return jnp.concatenate([jnp.pad(g[..., :M_HEADS], pad), jnp.pad(g[..., M_HEADS:], pad)], axis=-1)

    table = {"mcw": p["m_conv_w"], "rcw": p["r_conv_w"], "mcb": p["m_conv_b"], "mnw": p["m_norm_w"],
             "mskip": p["m_skip"], "rcb": p["r_conv_b"], "rba": p["r_ba"], "rbx": p["r_bx"], "rlam": p["r_lam"],
             "sd": p["s_d"], "bglu": p["s_b_glu"], "bif": split_gates(p["m_b_if"]), "abre": abre, "abim": abim,
             "norm_w": p["norm_w"]}

    def group_rows(group):
        parts = [table[name].reshape(DEPTH, n, w) for name, n, w in group]
        fill = _VEC_WIDTH - sum(w for _, _, w in group)
        if fill:
            parts.append(jnp.zeros((DEPTH, group[0][1], fill), F32))
        return parts[0] if len(parts) == 1 else jnp.concatenate(parts, axis=-1)

    vec = jnp.concatenate([group_rows(g) for g in _VEC_GROUPS]
                          + [jnp.zeros((DEPTH, _VEC_ROWS - _VEC_USED_ROWS, _VEC_WIDTH), F32)], axis=1)

    def block_diag_halves(w):
        nb = R_BLOCKS // 2
        w5 = w.reshape(DEPTH, 2, nb, w.shape[-2], w.shape[-1])
        eye = jnp.eye(nb, dtype=F32)[:, None, :, None]
        return (w5[:, :, :, :, None, :] * eye).reshape(
            DEPTH, 2, nb * w.shape[-2], nb * w.shape[-1]).astype(BF16)

    mats = {
        "wqk": jnp.concatenate([p["m_wq"], p["m_wk"]], axis=-1).astype(BF16),
        "wvo": jnp.concatenate([p["m_wv"], p["m_wo"]], axis=-1).astype(BF16),
        "wif": split_gates(p["m_w_if"]).astype(BF16),
        "rwa": block_diag_halves(p["r_wa"]), "rwx": block_diag_halves(p["r_wx"]),
        "wbu": wbu, "wcre": wcre, "wcim": wcim,
        "wglu": p["s_w_glu"].astype(BF16),
    }
    return vec, [mats[k] for k in _MATRIX_KEYS]


def _sample_rows_state(n, m, mconv, h, rconv, sre, sim):
    bsz = n.shape[1]

    def per_seq(x):
        return x.reshape(DEPTH, bsz, 1, x.shape[-1])

    m_pad = jnp.pad(m, ((0, 0), (0, 0), (0, LANES - M_HEADS)))
    return (per_seq(n.reshape(DEPTH, bsz, M_WIDTH)), per_seq(m_pad), mconv, per_seq(h), rconv,
            per_seq(sre.reshape(DEPTH, bsz, S_LANES)), per_seq(sim.reshape(DEPTH, bsz, S_LANES)))


def _sample_state_from_rows(n, m, mconv, h, rconv, sre, sim):
    bsz = n.shape[1]
    return (n.reshape(DEPTH, bsz, M_HEADS, M_HEAD_DIM), m.reshape(DEPTH, bsz, LANES)[..., :M_HEADS],
            mconv, h.reshape(DEPTH, bsz, R_WIDTH), rconv,
            sre.reshape(DEPTH, bsz, S_GROUPS, S_STATE), sim.reshape(DEPTH, bsz, S_GROUPS, S_STATE))


def _prompt_state_from_kernel(c, n, m, mtail, h, rtail, sre, sim):
    bsz = c.shape[1]
    return (c, n, m[:, :, 0, :M_HEADS], mtail[:, :, SUBLANES - 1::SUBLANES], h.reshape(DEPTH, bsz, R_WIDTH),
            rtail[:, :, SUBLANES - 1::SUBLANES],
            sre.reshape(DEPTH, bsz, S_GROUPS, S_STATE), sim.reshape(DEPTH, bsz, S_GROUPS, S_STATE))


def _interleave_chunks(x, inverse=False):
    bsz, seq, d = x.shape
    sub_len = PROMPT_CHUNK // SUBLANES
    inner = (sub_len, SUBLANES) if inverse else (SUBLANES, sub_len)
    return x.reshape(bsz, seq // PROMPT_CHUNK, *inner, d).transpose(0, 1, 3, 2, 4).reshape(bsz, seq, d)


def kernel(x_prompt, x_sample, state_mlstm_C, state_mlstm_n, state_mlstm_m, state_mlstm_conv, state_rglru_h, state_rglru_conv, state_s5_re, state_s5_im, norm_w, w_in, w_out, m_conv_w, m_conv_b, m_wq, m_wk, m_wv, m_wo, m_w_if, m_b_if, m_norm_w, m_skip, r_conv_w, r_conv_b, r_wa, r_ba, r_wx, r_bx, r_lam, s_lam_re, s_lam_im, s_b_re, s_b_im, s_c_re, s_c_im, s_d, s_log_step, s_w_glu, s_b_glu, final_norm_w):
    p = dict(norm_w=norm_w, w_in=w_in, w_out=w_out, m_conv_w=m_conv_w, m_conv_b=m_conv_b, m_wq=m_wq, m_wk=m_wk,
             m_wv=m_wv, m_wo=m_wo, m_w_if=m_w_if, m_b_if=m_b_if, m_norm_w=m_norm_w, m_skip=m_skip,
             r_conv_w=r_conv_w, r_conv_b=r_conv_b, r_wa=r_wa, r_ba=r_ba, r_wx=r_wx, r_bx=r_bx, r_lam=r_lam,
             s_d=s_d, s_w_glu=s_w_glu, s_b_glu=s_b_glu)
    s5 = _s5_prep(s_lam_re, s_lam_im, s_log_step, s_b_re, s_b_im, s_c_re, s_c_im)
    vec, mats = _prepare_weights(p, s5)
    final_w = final_norm_w.reshape(1, D_MODEL)

    bsz, seq, _ = x_prompt.shape
    dec_batch, dec_seq, _ = x_sample.shape
    n_prompt = bsz * seq
    n_sample = dec_batch * SUBLANES
    stream = [_interleave_chunks(x_prompt).reshape(n_prompt, D_MODEL),
              jnp.pad(x_sample, ((0, 0), (SAMPLE_LEAD, 0), (0, 0))).reshape(n_sample, D_MODEL)]
    rows_state = _sample_rows_state(state_mlstm_n, state_mlstm_m, state_mlstm_conv, state_rglru_h,
                                    state_rglru_conv, state_s5_re, state_s5_im)
    pr_states = None
    sa_states = None
    for l in range(DEPTH):
        last = l == DEPTH - 1
        proj_s, w_in_b = _inproj(stream[-1], stream[-1].shape[0] - n_sample, n_sample, vec, w_in, l)
        mixed, *pr_states = _prompt_mixer(stream[0], n_prompt + n_sample, l, bsz, seq, vec, w_in_b, mats, pr_states)
        mixed, *sa_states = _sample_mixer(proj_s, l, n_prompt, mixed, state_mlstm_C, rows_state, vec, mats, sa_states)
        stream = _outproj(mixed, stream, w_out, l, final_w, last, split_rows=n_prompt if last else None)
    y_prompt = _interleave_chunks(stream[0].reshape(bsz, seq, D_MODEL), inverse=True)
    y_sample = stream[1]
    return (y_prompt, y_sample, *_prompt_state_from_kernel(*pr_states),
            sa_states[0], *_sample_state_from_rows(*sa_states[1:]))
```

```python
import functools

import jax
import jax.numpy as jnp
from jax import lax
from jax.experimental import pallas as pl
from jax.experimental.pallas import tpu as pltpu

F32 = jnp.float32
BF16 = jnp.bfloat16

D_MODEL = 2048
DEPTH = 2
MIX_WIDTH = D_MODEL
M_WIDTH = MIX_WIDTH // 2
R_WIDTH = MIX_WIDTH // 4
S_WIDTH = MIX_WIDTH - M_WIDTH - R_WIDTH
M_HEADS = 8
M_HEAD_DIM = M_WIDTH // M_HEADS
R_BLOCKS = 8
RG_C = 8.0
S_GROUP = 16
S_GROUPS = S_WIDTH // S_GROUP
S_STATE = 64
S_LANES = S_GROUPS * S_STATE
CONV_W = 4
EPS = 1e-6

SUBLANES = 8
SUBLANE_SHIFT = SUBLANES.bit_length() - 1
LANES = 128
VMEM_LIMIT_BYTES = 56 * 1024 * 1024

PROMPT_CHUNK = 256
SAMPLE_ROWS = 128
SAMPLE_C_SEQS = 16
SAMPLE_LEAD = SUBLANES - 4
INPROJ_TN = 1024
OUTPROJ_TM = 512
SCAN_LANE_BLOCK = 256

_NT = (((1,), (1,)), ((), ()))

_VEC_WIDTH = D_MODEL
_VEC_GROUPS = ((("mcw", CONV_W, M_WIDTH), ("rcw", CONV_W, R_WIDTH)),
               (("mcb", 1, M_WIDTH), ("mnw", 1, M_WIDTH)),
               (("mskip", 1, M_WIDTH), ("rcb", 1, R_WIDTH), ("rba", 1, R_WIDTH)),
               (("rbx", 1, R_WIDTH), ("rlam", 1, R_WIDTH), ("sd", 1, S_WIDTH), ("bglu", 1, S_WIDTH)),
               (("bif", 1, 2 * LANES),),
               (("abre", 1, S_LANES),), (("abim", 1, S_LANES),), (("norm_w", 1, D_MODEL),))
_VEC_LAYOUT = {}
_row = 0
for _group in _VEC_GROUPS:
    _lane = 0
    for _name, _n, _w in _group:
        _VEC_LAYOUT[_name] = (_row, _n, _lane, _w)
        _lane += _w
    _row += _group[0][1]
_VEC_USED_ROWS = _row
_VEC_ROWS = -(-_row // SUBLANES) * SUBLANES


def _vec(vec_ref, name):
    r0, n, l0, w = _VEC_LAYOUT[name]
    return vec_ref[r0:r0 + n, l0:l0 + w]


def _layer_spec(arr, l):
    nd = arr.ndim - 1
    return pl.BlockSpec((None,) + arr.shape[1:], lambda *_: (l,) + (0,) * nd, pipeline_mode=pl.Buffered(1))


_ANY_SPEC = pl.BlockSpec(memory_space=pl.ANY)


S5_IN_SLICES = S_WIDTH // LANES
S5_IN_GROUPS = LANES // S_GROUP
S5_OUT_HALVES = 2
S5_OUT_GROUPS = S_GROUPS // S5_OUT_HALVES


def _s5_prep_kernel(lr_ref, li_ref, ls_ref, brt_ref, bit_ref, cre_ref, cim_ref,
                    abre_ref, abim_ref, wbu_ref, wcre_ref, wcim_ref):
    lr = lr_ref[0]
    li = li_ref[0]
    dt = jnp.exp(ls_ref[0])
    mag = jnp.exp(lr * dt)
    ang = li * dt
    ab_re = mag * jnp.cos(ang)
    ab_im = mag * jnp.sin(ang)
    den = lr * lr + li * li
    nr = ab_re - 1.0
    f_re = (nr * lr + ab_im * li) / den
    f_im = (ab_im * lr - nr * li) / den
    br = brt_ref[0]
    bi = bit_ref[0]
    abre_ref[0] = ab_re
    abim_ref[0] = ab_im
    bb_re = f_re * br - f_im * bi
    bb_im = f_re * bi + f_im * br

    per_slice = S5_IN_GROUPS * S_STATE
    lane_group = lax.broadcasted_iota(jnp.int32, (S_GROUP, per_slice), 1) >> (S_STATE.bit_length() - 1)
    for k in range(S5_IN_SLICES):
        re_k = bb_re[:, k * per_slice:(k + 1) * per_slice]
        im_k = bb_im[:, k * per_slice:(k + 1) * per_slice]
        for a in range(S5_IN_GROUPS):
            rows = slice(a * S_GROUP, (a + 1) * S_GROUP)
            wbu_ref[0, k, rows, 0:per_slice] = jnp.where(lane_group == a, re_k, 0.0).astype(BF16)
            wbu_ref[0, k, rows, per_slice:2 * per_slice] = jnp.where(lane_group == a, im_k, 0.0).astype(BF16)

    row_group = lax.broadcasted_iota(jnp.int32, (S5_OUT_GROUPS * S_GROUP, S_STATE), 0) >> (S_GROUP.bit_length() - 1)
    for src_ref, dst_ref in ((cre_ref, wcre_ref), (cim_ref, wcim_ref)):
        for m in range(S5_OUT_HALVES):
            c = src_ref[0, m]
            for a in range(S5_OUT_GROUPS):
                dst_ref[0, m, :, a * S_STATE:(a + 1) * S_STATE] = jnp.where(row_group == a, c, 0.0).astype(BF16)


def _s5_prep(s_lam_re, s_lam_im, s_log_step, s_b_re, s_b_im, s_c_re, s_c_im):
    lr = s_lam_re.reshape(DEPTH, 1, S_LANES)
    li = s_lam_im.reshape(DEPTH, 1, S_LANES)
    ls = jnp.repeat(s_log_step, S_STATE, axis=-1).reshape(DEPTH, 1, S_LANES)
    brt = s_b_re.reshape(DEPTH, S_LANES, S_GROUP).transpose(0, 2, 1)
    bit = s_b_im.reshape(DEPTH, S_LANES, S_GROUP).transpose(0, 2, 1)
    c_shape = (DEPTH, S5_OUT_HALVES, S5_OUT_GROUPS * S_GROUP, S_STATE)
    wbu_shape = (DEPTH, S5_IN_SLICES, LANES, 2 * S5_IN_GROUPS * S_STATE)
    wc_shape = (DEPTH, S5_OUT_HALVES, S5_OUT_GROUPS * S_GROUP, S5_OUT_GROUPS * S_STATE)

    def layer_block(shape):
        nd = len(shape) - 1
        return pl.BlockSpec((1,) + shape[1:], lambda l: (l,) + (0,) * nd)

    vec = layer_block((DEPTH, 1, S_LANES))
    mat = layer_block((DEPTH, S_GROUP, S_LANES))
    return pl.pallas_call(
        _s5_prep_kernel,
        grid=(DEPTH,),
        in_specs=[vec, vec, vec, mat, mat, layer_block(c_shape), layer_block(c_shape)],
        out_specs=[vec, vec, layer_block(wbu_shape), layer_block(wc_shape), layer_block(wc_shape)],
        out_shape=[jax.ShapeDtypeStruct((DEPTH, 1, S_LANES), F32),
                   jax.ShapeDtypeStruct((DEPTH, 1, S_LANES), F32),
                   jax.ShapeDtypeStruct(wbu_shape, BF16),
                   jax.ShapeDtypeStruct(wc_shape, BF16),
                   jax.ShapeDtypeStruct(wc_shape, BF16)],
        name="s5_prep",
    )(lr, li, ls, brt, bit, s_c_re.reshape(c_shape), s_c_im.reshape(c_shape))


def _stream_specs(parts, tm):
    if len(parts) == 1:
        return [pl.BlockSpec((tm, D_MODEL), lambda i: (i, 0))]
    nb0 = parts[0].shape[0] // tm
    return [pl.BlockSpec((tm, D_MODEL), lambda i: (jnp.minimum(i, nb0 - 1), 0)),
            pl.BlockSpec((tm, D_MODEL), lambda i: (jnp.maximum(i - nb0, 0), 0))]


def _read_stream(refs, nb0):
    if len(refs) == 1:
        return refs[0][...]
    return jnp.where(pl.program_id(0) < nb0, refs[0][...], refs[1][...])


def _rmsnorm_bf16(x, vec_ref):
    ms = jnp.mean(x * x, axis=-1, keepdims=True)
    return (x * lax.rsqrt(ms + EPS) * _vec(vec_ref, "norm_w")).astype(BF16)


def _inproj_kernel(x_ref, vec_ref, w_ref, o_ref, wb_ref, xn_scr):
    @pl.when(pl.program_id(0) == 0)
    def _():
        xn_scr[...] = _rmsnorm_bf16(x_ref[...], vec_ref)

    wb = w_ref[...].astype(BF16)
    wb_ref[...] = wb
    o_ref[...] = jnp.dot(xn_scr[...], wb, preferred_element_type=F32)


def _inproj(x, row0, nrows, vec, w_in, l):
    blk0 = row0 // nrows
    return pl.pallas_call(
        _inproj_kernel,
        grid=((2 * MIX_WIDTH) // INPROJ_TN,),
        in_specs=[pl.BlockSpec((nrows, D_MODEL), lambda j: (blk0, 0), pipeline_mode=pl.Buffered(1)),
                  _layer_spec(vec, l),
                  pl.BlockSpec((None, D_MODEL, INPROJ_TN), lambda j: (l, 0, j))],
        out_specs=[pl.BlockSpec((nrows, INPROJ_TN), lambda j: (0, j)),
                   pl.BlockSpec((D_MODEL, INPROJ_TN), lambda j: (0, j))],
        out_shape=[jax.ShapeDtypeStruct((nrows, 2 * MIX_WIDTH), F32),
                   jax.ShapeDtypeStruct((D_MODEL, 2 * MIX_WIDTH), BF16)],
        scratch_shapes=[pltpu.VMEM((nrows, D_MODEL), BF16)],
        compiler_params=pltpu.CompilerParams(
            dimension_semantics=("arbitrary",), vmem_limit_bytes=VMEM_LIMIT_BYTES),
        name="inproj",
    )(x, vec, w_in)


def _outproj_kernel(*refs, n_x, nb0, n_out, nb0_out, final):
    mixed_ref, x_refs = refs[0], refs[1:1 + n_x]
    w_ref, fw_ref = refs[1 + n_x:3 + n_x]
    o_refs, wb_scr = refs[3 + n_x:-1], refs[-1]

    @pl.when(pl.program_id(0) == 0)
    def _():
        wb_scr[...] = w_ref[...].astype(BF16)

    y = _read_stream(x_refs, nb0) + jnp.dot(mixed_ref[...], wb_scr[...], preferred_element_type=F32)
    if final:
        ms = jnp.mean(y * y, axis=-1, keepdims=True)
        y = y * lax.rsqrt(ms + EPS) * fw_ref[...]
    if n_out == 1:
        o_refs[0][...] = y
    else:
        @pl.when(pl.program_id(0) < nb0_out)
        def _():
            o_refs[0][...] = y

        @pl.when(pl.program_id(0) >= nb0_out)
        def _():
            o_refs[1][...] = y.reshape(y.shape[0] // SUBLANES, SUBLANES, y.shape[1])[:, SAMPLE_LEAD:, :]


def _outproj(mixed, x_parts, w_out, l, final_w, final, split_rows=None):
    n = mixed.shape[0]
    if split_rows is None:
        out_parts = [jax.ShapeDtypeStruct((n, D_MODEL), F32)]
        out_specs = [pl.BlockSpec((OUTPROJ_TM, D_MODEL), lambda i: (i, 0))]
        nb0_out = n // OUTPROJ_TM
    else:
        nb0_out = split_rows // OUTPROJ_TM
        tokens = SUBLANES - SAMPLE_LEAD
        out_parts = [jax.ShapeDtypeStruct((split_rows, D_MODEL), F32),
                     jax.ShapeDtypeStruct(((n - split_rows) // SUBLANES, tokens, D_MODEL), F32)]
        out_specs = [pl.BlockSpec((OUTPROJ_TM, D_MODEL), lambda i: (jnp.minimum(i, nb0_out - 1), 0)),
                     pl.BlockSpec((OUTPROJ_TM // SUBLANES, tokens, D_MODEL),
                                  lambda i: (jnp.maximum(i - nb0_out, 0), 0, 0))]
    return pl.pallas_call(
        functools.partial(_outproj_kernel, n_x=len(x_parts), nb0=x_parts[0].shape[0] // OUTPROJ_TM,
                          n_out=len(out_parts), nb0_out=nb0_out, final=final),
        grid=(n // OUTPROJ_TM,),
        in_specs=([pl.BlockSpec((OUTPROJ_TM, MIX_WIDTH), lambda i: (i, 0))] + _stream_specs(x_parts, OUTPROJ_TM)
                  + [_layer_spec(w_out, l), pl.BlockSpec((1, D_MODEL), lambda i: (0, 0))]),
        out_specs=out_specs,
        out_shape=out_parts,
        scratch_shapes=[pltpu.VMEM((MIX_WIDTH, D_MODEL), BF16)],
        compiler_params=pltpu.CompilerParams(
            dimension_semantics=("arbitrary",), vmem_limit_bytes=VMEM_LIMIT_BYTES),
        name="outproj",
    )(mixed, *x_parts, w_out, final_w)


def _silu(x):
    return x * jax.nn.sigmoid(x)


def _log_sigmoid(x):
    return jnp.minimum(x, 0.0) - jnp.log1p(jnp.exp(-jnp.abs(x)))


def _softplus(x):
    return jnp.maximum(x, 0.0) + jnp.log1p(jnp.exp(-jnp.abs(x)))


def _sublane_pos(shape):
    return lax.broadcasted_iota(jnp.int32, shape, 0) & (SUBLANES - 1)


def _tile_cumsum(x):
    pos = _sublane_pos(x.shape)
    s = 1
    while s < SUBLANES:
        x = x + jnp.where(pos >= s, pltpu.roll(x, s, 0), 0.0)
        s *= 2
    return x


def _tile_cummax(x):
    pos = _sublane_pos(x.shape)
    s = 1
    while s < SUBLANES:
        x = jnp.maximum(x, jnp.where(pos >= s, pltpu.roll(x, s, 0), -jnp.inf))
        s *= 2
    return x


def _tile_scan_real(a, b):
    pos = _sublane_pos(a.shape)
    s = 1
    while s < SUBLANES:
        m = pos >= s
        b = jnp.where(m, a * pltpu.roll(b, s, 0) + b, b)
        if 2 * s < SUBLANES:
            a = jnp.where(m, a * pltpu.roll(a, s, 0), a)
        s *= 2
    return b


def _tile_scan_cplx(sr, si, pr, pi):
    pos = _sublane_pos(sr.shape)
    s = 1
    while s < SUBLANES:
        m = pos >= s
        sr_sh = pltpu.roll(sr, s, 0)
        si_sh = pltpu.roll(si, s, 0)
        sr, si = (jnp.where(m, sr + (pr * sr_sh - pi * si_sh), sr),
                  jnp.where(m, si + (pr * si_sh + pi * sr_sh), si))
        if 2 * s < SUBLANES:
            pr, pi = pr * pr - pi * pi, 2.0 * (pr * pi)
        s *= 2
    return sr, si


def _mlstm_project(xc, xm, vec_ref, wqk_ref, wvo_ref, wif_ref, q_scr, k_scr, v_scr, o_scr):
    xc_b = xc.astype(BF16)
    xm_b = xm.astype(BF16)
    for h in range(M_HEADS):
        sl = slice(h * M_HEAD_DIM, (h + 1) * M_HEAD_DIM)
        qk = jnp.dot(xc_b[:, sl], wqk_ref[h], preferred_element_type=F32)
        vo = jnp.dot(xm_b[:, sl], wvo_ref[h], preferred_element_type=F32)
        q_scr[:, sl] = qk[:, :M_HEAD_DIM]
        k_scr[:, sl] = qk[:, M_HEAD_DIM:]
        v_scr[:, sl] = vo[:, :M_HEAD_DIM]
        o_scr[:, sl] = vo[:, M_HEAD_DIM:]
    return (jnp.dot(q_scr[...].astype(BF16), wif_ref[0:M_WIDTH, :], preferred_element_type=F32)
            + jnp.dot(k_scr[...].astype(BF16), wif_ref[M_WIDTH:2 * M_WIDTH, :], preferred_element_type=F32)
            + jnp.dot(v_scr[...].astype(BF16), wif_ref[2 * M_WIDTH:3 * M_WIDTH, :], preferred_element_type=F32)
            + _vec(vec_ref, "bif"))


def _head_output(hh, o_pre, xc_h, z_h, mnw_h, mskip_h):
    mu = jnp.mean(hh, axis=1, keepdims=True)
    hc = hh - mu
    var = jnp.mean(hc * hc, axis=1, keepdims=True)
    hn = hc * lax.rsqrt(var + EPS) * mnw_h
    return (jax.nn.sigmoid(o_pre) * hn + mskip_h * xc_h) * _silu(z_h)


def _rglru_coeffs(xcr, vec_ref, rwa_ref, rwx_ref):
    xcr_b = xcr.astype(BF16)
    half = R_WIDTH // 2
    ra_pre = jnp.concatenate(
        [jnp.dot(xcr_b[:, :half], rwa_ref[0], preferred_element_type=F32),
         jnp.dot(xcr_b[:, half:], rwa_ref[1], preferred_element_type=F32)], axis=1) + _vec(vec_ref, "rba")
    rx_pre = jnp.concatenate(
        [jnp.dot(xcr_b[:, :half], rwx_ref[0], preferred_element_type=F32),
         jnp.dot(xcr_b[:, half:], rwx_ref[1], preferred_element_type=F32)], axis=1) + _vec(vec_ref, "rbx")
    log_a = (-RG_C) * jax.nn.sigmoid(ra_pre) * _softplus(-_vec(vec_ref, "rlam"))
    a = jnp.exp(log_a)
    th = jnp.tanh(log_a)
    one_minus_a2 = (-2.0 * th) / (1.0 - th)
    return a, jnp.sqrt(one_minus_a2) * (jax.nn.sigmoid(rx_pre) * xcr)


def _s5_project(u_b, wbu_ref, re_ref, im_ref):
    per_slice = (LANES // S_GROUP) * S_STATE
    for k in range(S_WIDTH // LANES):
        res = jnp.dot(u_b[:, k * LANES:(k + 1) * LANES], wbu_ref[k], preferred_element_type=F32)
        re_ref[:, k * per_slice:(k + 1) * per_slice] = res[:, :per_slice]
        im_ref[:, k * per_slice:(k + 1) * per_slice] = res[:, per_slice:]


def _s5_output(re_ref, im_ref, u, vec_ref, wcre_ref, wcim_ref, wglu_ref):
    nblk = wcre_ref.shape[0]
    k_blk = S_LANES // nblk
    parts = []
    for m in range(nblk):
        ks = slice(m * k_blk, (m + 1) * k_blk)
        parts.append(lax.dot_general(re_ref[:, ks].astype(BF16), wcre_ref[m], _NT, preferred_element_type=F32)
                     - lax.dot_general(im_ref[:, ks].astype(BF16), wcim_ref[m], _NT, preferred_element_type=F32))
    y = jnp.concatenate(parts, axis=1) + _vec(vec_ref, "sd") * u
    g = jax.nn.gelu(y)
    return g * jax.nn.sigmoid(jnp.dot(g.astype(BF16), wglu_ref[...], preferred_element_type=F32)
                              + _vec(vec_ref, "bglu"))


_MATRIX_KEYS = ("wqk", "wvo", "wif", "rwa", "rwx", "wbu", "wcre", "wcim", "wglu")


def _tile(x, i):
    return x[i * SUBLANES:(i + 1) * SUBLANES]


def _conv_interleaved(x, tail_ref, w, bias, rows):
    ntiles = rows // SUBLANES
    pos = lax.broadcasted_iota(jnp.int32, (SUBLANES, x.shape[1]), 0)
    prev = tail_ref[...]
    before = [pltpu.roll(jnp.where(pos == SUBLANES - 1, _tile(prev, CONV_W - 1 - d), _tile(x, ntiles - d)), 1, 0)
              for d in range(1, CONV_W)]
    out = w[CONV_W - 1:CONV_W, :] * x + bias
    for j in range(1, CONV_W):
        shifted = jnp.concatenate(before[:j][::-1] + [x[:rows - j * SUBLANES]], axis=0)
        out = out + w[CONV_W - 1 - j:CONV_W - j, :] * shifted
    tail_ref[...] = x[rows - (CONV_W - 1) * SUBLANES:]
    return out


def _cumsum_interleaved(x, rows):
    tiles = [_tile(x, 0)]
    for i in range(1, rows // SUBLANES):
        tiles.append(tiles[-1] + _tile(x, i))
    total = tiles[-1]
    start = _tile_cumsum(total) - total
    return jnp.concatenate([t + start for t in tiles], axis=0)


def _cummax_interleaved(x, rows):
    tiles = [_tile(x, 0)]
    for i in range(1, rows // SUBLANES):
        tiles.append(jnp.maximum(tiles[-1], _tile(x, i)))
    best = tiles[-1]
    pos = lax.broadcasted_iota(jnp.int32, best.shape, 0)
    s = 1
    while s < SUBLANES:
        best = jnp.maximum(best, jnp.where(pos >= s, pltpu.roll(best, s, 0), -jnp.inf))
        s *= 2
    start = jnp.where(pos == 0, -jnp.inf, pltpu.roll(best, 1, 0))
    return jnp.concatenate([jnp.maximum(t, start) for t in tiles], axis=0)


def _scan_real_interleaved(a_ref, b_ref, carry, rows):
    ntiles = rows // SUBLANES
    pos = lax.broadcasted_iota(jnp.int32, (SUBLANES, a_ref.shape[1]), 0)
    h = _tile(b_ref, 0)
    aprod = _tile(a_ref, 0)
    for i in range(1, ntiles):
        a = _tile(a_ref, i)
        h = a * h + _tile(b_ref, i)
        aprod = aprod * a
    g = _tile_scan_real(aprod, h + jnp.where(pos == 0, aprod * carry, 0.0))
    h = jnp.where(pos == 0, carry, pltpu.roll(g, 1, 0))
    for i in range(ntiles):
        h = _tile(a_ref, i) * h + _tile(b_ref, i)
        b_ref[i * SUBLANES:(i + 1) * SUBLANES, :] = h
    return g[SUBLANES - 1:SUBLANES]


def _scan_cplx_interleaved(re_ref, im_ref, p_re, p_im, cre_ref, cim_ref, rows):
    ntiles = rows // SUBLANES
    width = re_ref.shape[1]
    pos = lax.broadcasted_iota(jnp.int32, (SUBLANES, SCAN_LANE_BLOCK), 0)
    for blk in range(width // SCAN_LANE_BLOCK):
        sl = slice(blk * SCAN_LANE_BLOCK, (blk + 1) * SCAN_LANE_BLOCK)
        pr = jnp.broadcast_to(p_re[:, sl], pos.shape)
        pi = jnp.broadcast_to(p_im[:, sl], pos.shape)
        c_r = cre_ref[:, sl]
        c_i = cim_ref[:, sl]
        sr = re_ref[0:SUBLANES, sl]
        si = im_ref[0:SUBLANES, sl]
        for i in range(1, ntiles):
            rs = slice(i * SUBLANES, (i + 1) * SUBLANES)
            sr, si = pr * sr - pi * si + re_ref[rs, sl], pr * si + pi * sr + im_ref[rs, sl]
        qr, qi = pr, pi
        n = 1
        while n < ntiles:
            qr, qi = qr * qr - qi * qi, 2.0 * (qr * qi)
            n *= 2
        gr, gi = _tile_scan_cplx(sr + jnp.where(pos == 0, qr * c_r - qi * c_i, 0.0),
                                 si + jnp.where(pos == 0, qr * c_i + qi * c_r, 0.0), qr, qi)
        sr = jnp.where(pos == 0, c_r, pltpu.roll(gr, 1, 0))
        si = jnp.where(pos == 0, c_i, pltpu.roll(gi, 1, 0))
        for i in range(ntiles):
            rs = slice(i * SUBLANES, (i + 1) * SUBLANES)
            sr, si = pr * sr - pi * si + re_ref[rs, sl], pr * si + pi * sr + im_ref[rs, sl]
            re_ref[rs, sl] = sr
            im_ref[rs, sl] = si
        cre_ref[:, sl] = gr[SUBLANES - 1:SUBLANES]
        cim_ref[:, sl] = gi[SUBLANES - 1:SUBLANES]


def _prompt_mixer_kernel(*refs, rows, nchunk, n_alias):
    (xnext_ref, xfirst_ref, vec_ref, win_ref,
     wqk_ref, wvo_ref, wif_ref, rwa_ref, rwx_ref, wbu_ref, wcre_ref, wcim_ref, wglu_ref) = refs[:13]
    (mixed_ref, c_ref, n_ref, m_ref, mtail_ref, h_ref, rtail_ref, sre_ref, sim_ref,
     q_scr, k_scr, v_scr, o_scr, ra_scr, rb_scr, ure_scr, uim_scr, proj_scr, xn_scr) = refs[13 + n_alias:]
    T = rows
    step = pl.program_id(0)
    slot = lax.rem(step, 2)
    proj_ref = proj_scr.at[slot]
    next_proj_ref = proj_scr.at[1 - slot]
    slab = (2 * MIX_WIDTH) // M_HEADS

    @pl.when(step == 0)
    def _():
        xn = _rmsnorm_bf16(xfirst_ref[...], vec_ref)
        for j in range(M_HEADS):
            cols = slice(j * slab, (j + 1) * slab)
            proj_scr[0, :, cols] = jnp.dot(xn, win_ref[:, cols], preferred_element_type=F32)

    @pl.when(lax.rem(step, nchunk) == 0)
    def _():
        for ref in (c_ref, n_ref, m_ref, mtail_ref, h_ref, rtail_ref, sre_ref, sim_ref):
            ref[...] = jnp.zeros(ref.shape, F32)

    xn_scr[...] = _rmsnorm_bf16(xnext_ref[...], vec_ref)

    xm = proj_ref[:, 0:M_WIDTH]
    xc = _silu(_conv_interleaved(xm, mtail_ref, _vec(vec_ref, "mcw"), _vec(vec_ref, "mcb"), T))
    gates = _mlstm_project(xc, xm, vec_ref, wqk_ref, wvo_ref, wif_ref, q_scr, k_scr, v_scr, o_scr)
    log_i = gates[:, :LANES]
    b_all = _cumsum_interleaved(_log_sigmoid(gates[:, LANES:]), T)
    a_max = _cummax_interleaved(log_i - b_all, T)
    m_prev = m_ref[...]
    log_inter = b_all + m_prev
    m_t_all = jnp.maximum(log_inter, b_all + a_max)
    w_inter_all = jnp.exp(log_inter - m_t_all)
    floor_all = jnp.exp(-m_t_all)
    b_end = b_all[T - 1:T, :]
    m_new = jnp.maximum(b_end + m_prev, b_end + a_max[T - 1:T, :])
    w_src_all = jnp.exp(b_end - b_all + log_i - m_new)
    decay_all = jnp.exp(b_end + m_prev - m_new)
    m_ref[...] = m_new
    b_t = b_all.T
    li_t = log_i.T

    rowi = lax.broadcasted_iota(jnp.int32, (T, T), 0)
    coli = lax.broadcasted_iota(jnp.int32, (T, T), 1)
    sub_len = T // SUBLANES

    def time_of(r):
        return (r & (SUBLANES - 1)) * sub_len + (r >> SUBLANE_SHIFT)

    causal = time_of(coli) <= time_of(rowi)
    k_scale = M_HEAD_DIM ** -0.5
    mnw = _vec(vec_ref, "mnw")
    mskip = _vec(vec_ref, "mskip")

    for h in range(M_HEADS):
        cols = slice(h * slab, (h + 1) * slab)
        next_proj_ref[:, cols] = jnp.dot(xn_scr[...], win_ref[:, cols], preferred_element_type=F32)

        sl = slice(h * M_HEAD_DIM, (h + 1) * M_HEAD_DIM)
        c_prev = c_ref[h]
        n_prev = n_ref[h:h + 1, :]
        w_inter = w_inter_all[:, h:h + 1]
        w_src = w_src_all[:, h:h + 1]
        decay = decay_all[:, h:h + 1]

        q = q_scr[:, sl]
        ks = k_scr[:, sl] * k_scale
        v = v_scr[:, sl]
        q_b = q.astype(BF16)
        ks_b = ks.astype(BF16)

        log_d = jnp.where(causal, b_all[:, h:h + 1] - b_t[h:h + 1, :] + li_t[h:h + 1, :], -jnp.inf)
        w_intra = jnp.exp(log_d - m_t_all[:, h:h + 1])
        s = lax.dot_general(q_b, ks_b, _NT, preferred_element_type=F32) * w_intra
        inter = lax.dot_general(q_b, c_prev.astype(BF16), _NT, preferred_element_type=F32)
        num = jnp.dot(s.astype(BF16), v.astype(BF16), preferred_element_type=F32) + w_inter * inter
        den = (jnp.sum(s, axis=1, keepdims=True)
               + w_inter * jnp.sum(q * n_prev, axis=1, keepdims=True))
        hh = num / jnp.maximum(jnp.abs(den), floor_all[:, h:h + 1])

        vw_t = (v * w_src).T.astype(BF16)
        c_ref[h] = decay * c_prev + jnp.dot(vw_t, ks_b, preferred_element_type=F32)
        n_ref[h:h + 1, :] = decay * n_prev + jnp.sum(ks * w_src, axis=0, keepdims=True)

        z = proj_ref[:, MIX_WIDTH + h * M_HEAD_DIM:MIX_WIDTH + (h + 1) * M_HEAD_DIM]
        mixed_ref[:, sl] = _head_output(hh, o_scr[:, sl], xc[:, sl], z, mnw[:, sl], mskip[:, sl]).astype(BF16)

    xr = proj_ref[:, M_WIDTH:M_WIDTH + R_WIDTH]
    xcr = _conv_interleaved(xr, rtail_ref, _vec(vec_ref, "rcw"), _vec(vec_ref, "rcb"), T)
    a, bb = _rglru_coeffs(xcr, vec_ref, rwa_ref, rwx_ref)
    ra_scr[...] = a
    rb_scr[...] = bb
    h_ref[...] = _scan_real_interleaved(ra_scr, rb_scr, h_ref[...], T)
    zr = proj_ref[:, MIX_WIDTH + M_WIDTH:MIX_WIDTH + M_WIDTH + R_WIDTH]
    mixed_ref[:, M_WIDTH:M_WIDTH + R_WIDTH] = (rb_scr[...] * _silu(zr)).astype(BF16)

    u = proj_ref[:, M_WIDTH + R_WIDTH:MIX_WIDTH]
    _s5_project(u.astype(BF16), wbu_ref, ure_scr, uim_scr)
    _scan_cplx_interleaved(ure_scr, uim_scr, _vec(vec_ref, "abre"), _vec(vec_ref, "abim"), sre_ref, sim_ref, T)
    glu = _s5_output(ure_scr, uim_scr, u, vec_ref, wcre_ref, wcim_ref, wglu_ref)
    zs = proj_ref[:, MIX_WIDTH + M_WIDTH + R_WIDTH:2 * MIX_WIDTH]
    mixed_ref[:, M_WIDTH + R_WIDTH:MIX_WIDTH] = (glu * _silu(zs)).astype(BF16)


_PROMPT_STATE_SHAPES = ((M_HEADS, M_HEAD_DIM, M_HEAD_DIM), (M_HEADS, M_HEAD_DIM), (1, LANES),
                        ((CONV_W - 1) * SUBLANES, M_WIDTH), (1, R_WIDTH), ((CONV_W - 1) * SUBLANES, R_WIDTH),
                        (1, S_LANES), (1, S_LANES))


def _prompt_mixer(x, total_rows, l, bsz, seq, vec, w_in_b, mats, prev_states):
    rows = PROMPT_CHUNK
    nchunk = seq // rows
    nstep = bsz * nchunk
    aliases = {}
    alias_args = []
    n_in = 4 + len(mats)
    if prev_states is not None:
        alias_args = list(prev_states)
        aliases = {n_in + k: 1 + k for k in range(len(alias_args))}

    def state_spec(shape):
        nd = len(shape)
        return pl.BlockSpec((None, None) + shape, lambda t: (l, t // nchunk) + (0,) * nd)

    out_shape = ([jax.ShapeDtypeStruct((total_rows, MIX_WIDTH), BF16)]
                 + [jax.ShapeDtypeStruct((DEPTH, bsz) + s, F32) for s in _PROMPT_STATE_SHAPES])
    scratch = ([pltpu.VMEM((rows, w), F32)
                for w in (M_WIDTH, M_WIDTH, M_WIDTH, M_WIDTH, R_WIDTH, R_WIDTH, S_LANES, S_LANES)]
               + [pltpu.VMEM((2, rows, 2 * MIX_WIDTH), F32), pltpu.VMEM((rows, D_MODEL), BF16)])
    return pl.pallas_call(
        functools.partial(_prompt_mixer_kernel, rows=rows, nchunk=nchunk, n_alias=len(alias_args)),
        grid=(nstep,),
        in_specs=([pl.BlockSpec((rows, D_MODEL), lambda t: (jnp.minimum(t + 1, nstep - 1), 0)),
                   pl.BlockSpec((rows, D_MODEL), lambda t: (0, 0), pipeline_mode=pl.Buffered(1)),
                   _layer_spec(vec, l),
                   pl.BlockSpec(w_in_b.shape, lambda t: (0, 0), pipeline_mode=pl.Buffered(1))]
                  + [_layer_spec(w, l) for w in mats] + [_ANY_SPEC] * len(alias_args)),
        out_specs=([pl.BlockSpec((rows, MIX_WIDTH), lambda t: (t, 0))]
                   + [state_spec(s) for s in _PROMPT_STATE_SHAPES]),
        out_shape=out_shape,
        scratch_shapes=scratch,
        input_output_aliases=aliases,
        compiler_params=pltpu.CompilerParams(
            dimension_semantics=("arbitrary",), vmem_limit_bytes=VMEM_LIMIT_BYTES),
        name="prompt_mixer",
    )(x, x, vec, w_in_b, *mats, *alias_args)


def _seg_last(x, groups):
    x3 = x.reshape(groups, SUBLANES, x.shape[-1])
    return jnp.broadcast_to(x3[:, SUBLANES - 1:SUBLANES, :], x3.shape).reshape(x.shape)


def _seg_max(x, groups):
    x3 = x.reshape(groups, SUBLANES, x.shape[-1])
    return jnp.broadcast_to(jnp.max(x3, axis=1, keepdims=True), x3.shape).reshape(x.shape)


def _seg_sum(x, groups):
    x3 = x.reshape(groups, SUBLANES, x.shape[-1])
    return jnp.broadcast_to(jnp.sum(x3, axis=1, keepdims=True), x3.shape).reshape(x.shape)


def _seg_rows(state_ref, lanes=slice(None)):
    x = state_ref[:, :, lanes]
    return jnp.broadcast_to(x, (x.shape[0], SUBLANES, x.shape[2])).reshape(x.shape[0] * SUBLANES, x.shape[2])


def _seg_state(x):
    x3 = x.reshape(x.shape[0] // SUBLANES, SUBLANES, x.shape[-1])
    return x3[:, SUBLANES - 1:SUBLANES, :]


def _with_history(x, buf_ref, new_buf_ref):
    g = x.shape[0] // SUBLANES
    x3 = x.reshape(g, SUBLANES, x.shape[1])
    lead = SAMPLE_LEAD - (CONV_W - 1)
    full = jnp.concatenate([jnp.zeros((g, lead, x.shape[1]), F32), buf_ref[...], x3[:, SAMPLE_LEAD:, :]], axis=1)
    new_buf_ref[...] = full[:, SUBLANES - (CONV_W - 1):, :]
    return full.reshape(x.shape)


def _conv_rolled(xf, w, bias):
    out = w[CONV_W - 1:CONV_W, :] * xf + bias
    for j in range(1, CONV_W):
        out = out + w[CONV_W - 1 - j:CONV_W - j, :] * pltpu.roll(xf, j, 0)
    return out


_SAMPLE_STATE_WIDTHS = (M_WIDTH, LANES, M_WIDTH, R_WIDTH, R_WIDTH, S_LANES, S_LANES)
_SAMPLE_STATE_ROWS = (1, 1, CONV_W - 1, 1, CONV_W - 1, 1, 1)


def _sample_mixer_kernel(*refs, rows, c_seqs, n_alias):
    (proj_ref, c0_ref, n0_ref, m0_ref, mtail0_ref, h0_ref, rtail0_ref, sre0_ref, sim0_ref,
     vec_ref, wqk_ref, wvo_ref, wif_ref, rwa_ref, rwx_ref, wbu_ref, wcre_ref, wcim_ref, wglu_ref) = refs[:19]
    (mixed_ref, c_ref, n_ref, m_ref, mconv_ref, h_ref, rconv_ref, sre_ref, sim_ref,
     q_scr, k_scr, v_scr, o_scr, xc_scr, intert_scr, vwt_scr, qt_scr, ksb_scr, dec_scr, b_scr, li_scr,
     ure_scr, uim_scr) = refs[19 + n_alias:]
    R = rows
    G = R // SUBLANES
    sub = pl.program_id(1)
    k_scale = M_HEAD_DIM ** -0.5
    pos = _sublane_pos((R, 1))
    valid = pos >= SAMPLE_LEAD
    first_token = pos == SAMPLE_LEAD

    @pl.when(sub == 0)
    def _():
        xm = _with_history(proj_ref[:, 0:M_WIDTH], mtail0_ref, mconv_ref)
        xc = _silu(_conv_rolled(xm, _vec(vec_ref, "mcw"), _vec(vec_ref, "mcb")))
        xc_scr[...] = xc
        gates = _mlstm_project(xc, xm, vec_ref, wqk_ref, wvo_ref, wif_ref, q_scr, k_scr, v_scr, o_scr)
        log_i = jnp.where(valid, gates[:, :LANES], -jnp.inf)
        log_f = jnp.where(valid, _log_sigmoid(gates[:, LANES:]), 0.0)
        b_all = _tile_cumsum(log_f)
        b_end = _seg_last(b_all, G)
        m_rows = _seg_rows(m0_ref)
        n_rows = _seg_rows(n0_ref)
        log_src = b_end - b_all + log_i
        m_new = jnp.maximum(b_end + m_rows, _seg_max(log_src, G))
        w_src_all = jnp.exp(log_src - m_new)
        decay_all = jnp.exp(b_end + m_rows - m_new)
        dec_scr[...] = decay_all
        b_scr[...] = b_all
        li_scr[...] = log_i
        m_ref[...] = _seg_state(m_new)
        for h in range(M_HEADS):
            sl = slice(h * M_HEAD_DIM, (h + 1) * M_HEAD_DIM)
            ks = k_scr[:, sl] * k_scale
            w_src = w_src_all[:, h:h + 1]
            n_ref[:, :, sl] = _seg_state(decay_all[:, h:h + 1] * n_rows[:, sl] + _seg_sum(ks * w_src, G))
            vwt_scr[h] = (v_scr[:, sl] * w_src).T
            qt_scr[h] = q_scr[:, sl].T.astype(BF16)
            ksb_scr[h] = ks.astype(BF16)
            intert_scr[h] = jnp.zeros((M_HEAD_DIM, R), F32)

        xr = _with_history(proj_ref[:, M_WIDTH:M_WIDTH + R_WIDTH], rtail0_ref, rconv_ref)
        xcr = _conv_rolled(xr, _vec(vec_ref, "rcw"), _vec(vec_ref, "rcb"))
        a, bb = _rglru_coeffs(xcr, vec_ref, rwa_ref, rwx_ref)
        hs = _tile_scan_real(a, jnp.where(valid, bb, 0.0) + jnp.where(first_token, a * _seg_rows(h0_ref), 0.0))
        h_ref[...] = _seg_state(hs)
        zr = proj_ref[:, MIX_WIDTH + M_WIDTH:MIX_WIDTH + M_WIDTH + R_WIDTH]
        mixed_ref[:, M_WIDTH:M_WIDTH + R_WIDTH] = jnp.where(valid, hs * _silu(zr), 0.0).astype(BF16)

        u = proj_ref[:, M_WIDTH + R_WIDTH:MIX_WIDTH]
        _s5_project(u.astype(BF16), wbu_ref, ure_scr, uim_scr)
        p_re = _vec(vec_ref, "abre")
        p_im = _vec(vec_ref, "abim")
        for blk in range(S_LANES // SCAN_LANE_BLOCK):
            sl = slice(blk * SCAN_LANE_BLOCK, (blk + 1) * SCAN_LANE_BLOCK)
            pr = jnp.broadcast_to(p_re[:, sl], (R, SCAN_LANE_BLOCK))
            pi = jnp.broadcast_to(p_im[:, sl], (R, SCAN_LANE_BLOCK))
            s0r = _seg_rows(sre0_ref, sl)
            s0i = _seg_rows(sim0_ref, sl)
            sr, si = _tile_scan_cplx(
                jnp.where(valid, ure_scr[:, sl], 0.0) + jnp.where(first_token, pr * s0r - pi * s0i, 0.0),
                jnp.where(valid, uim_scr[:, sl], 0.0) + jnp.where(first_token, pr * s0i + pi * s0r, 0.0), pr, pi)
            ure_scr[:, sl] = sr
            uim_scr[:, sl] = si
            sre_ref[:, :, sl] = _seg_state(sr)
            sim_ref[:, :, sl] = _seg_state(si)
        glu = _s5_output(ure_scr, uim_scr, u, vec_ref, wcre_ref, wcim_ref, wglu_ref)
        zs = proj_ref[:, MIX_WIDTH + M_WIDTH + R_WIDTH:2 * MIX_WIDTH]
        mixed_ref[:, M_WIDTH + R_WIDTH:MIX_WIDTH] = jnp.where(valid, glu * _silu(zs), 0.0).astype(BF16)

    lane_seq = lax.broadcasted_iota(jnp.int32, (M_HEAD_DIM, R), 1) >> SUBLANE_SHIFT
    seq0 = sub * c_seqs
    for h in range(M_HEADS):
        c_old = c0_ref[:, h].reshape(c_seqs * M_HEAD_DIM, M_HEAD_DIM)
        readout = jnp.dot(c_old.astype(BF16), qt_scr[h], preferred_element_type=F32)
        acc = intert_scr[h]
        vwt = vwt_scr[h]
        lhs = []
        for s in range(c_seqs):
            own = lane_seq == seq0 + s
            acc = jnp.where(own, readout[s * M_HEAD_DIM:(s + 1) * M_HEAD_DIM], acc)
            lhs.append(jnp.where(own, vwt, 0.0))
        intert_scr[h] = acc
        upd = jnp.dot(jnp.concatenate(lhs, axis=0).astype(BF16), ksb_scr[h], preferred_element_type=F32)
        for s in range(c_seqs):
            r0 = pl.multiple_of((seq0 + s) * SUBLANES, SUBLANES)
            c_ref[s, h] = (dec_scr[pl.ds(r0, 1), h:h + 1] * c0_ref[s, h]
                           + upd[s * M_HEAD_DIM:(s + 1) * M_HEAD_DIM])

    @pl.when(sub == pl.num_programs(1) - 1)
    def _():
        _sample_heads(proj_ref, m0_ref, n0_ref, vec_ref, mixed_ref,
                      q_scr, k_scr, v_scr, o_scr, xc_scr, intert_scr, b_scr, li_scr, valid, R)


def _sample_heads(proj_ref, m0_ref, n0_ref, vec_ref, mixed_ref,
                  q_scr, k_scr, v_scr, o_scr, xc_scr, intert_scr, b_scr, li_scr, valid, R):
    rowi = lax.broadcasted_iota(jnp.int32, (R, R), 0)
    coli = lax.broadcasted_iota(jnp.int32, (R, R), 1)
    same_causal = jnp.logical_and(coli <= rowi, (coli >> SUBLANE_SHIFT) == (rowi >> SUBLANE_SHIFT))
    k_scale = M_HEAD_DIM ** -0.5
    mnw = _vec(vec_ref, "mnw")
    mskip = _vec(vec_ref, "mskip")
    n_rows = _seg_rows(n0_ref)
    b_all = b_scr[...]
    log_i = li_scr[...]
    log_inter = b_all + _seg_rows(m0_ref)
    m_t_all = jnp.maximum(log_inter, b_all + _tile_cummax(log_i - b_all))
    w_inter_all = jnp.exp(log_inter - m_t_all)
    floor_all = jnp.exp(-m_t_all)
    b_t = b_all.T
    li_t = log_i.T

    for h in range(M_HEADS):
        sl = slice(h * M_HEAD_DIM, (h + 1) * M_HEAD_DIM)
        q = q_scr[:, sl]
        ks = k_scr[:, sl] * k_scale
        v = v_scr[:, sl]
        q_b = q.astype(BF16)
        ks_b = ks.astype(BF16)
        w_inter = w_inter_all[:, h:h + 1]

        log_d = jnp.where(same_causal, b_all[:, h:h + 1] - b_t[h:h + 1, :] + li_t[h:h + 1, :], -jnp.inf)
        w_intra = jnp.exp(log_d - m_t_all[:, h:h + 1])
        s = lax.dot_general(q_b, ks_b, _NT, preferred_element_type=F32) * w_intra
        num = (jnp.dot(s.astype(BF16), v.astype(BF16), preferred_element_type=F32)
               + w_inter * intert_scr[h].T)
        den = (jnp.sum(s, axis=1, keepdims=True)
               + w_inter * jnp.sum(q * n_rows[:, sl], axis=1, keepdims=True))
        hh = num / jnp.maximum(jnp.abs(den), floor_all[:, h:h + 1])

        z = proj_ref[:, MIX_WIDTH + h * M_HEAD_DIM:MIX_WIDTH + (h + 1) * M_HEAD_DIM]
        out = _head_output(hh, o_scr[:, sl], xc_scr[:, sl], z, mnw[:, sl], mskip[:, sl])
        mixed_ref[:, sl] = jnp.where(valid, out, 0.0).astype(BF16)


def _sample_mixer(proj, l, row0, mixed_prev, c_all, row_state, vec, mats, prev_out):
    R = SAMPLE_ROWS
    nrows = proj.shape[0]
    nsub = (R // SUBLANES) // SAMPLE_C_SEQS
    blk0 = row0 // R
    c_spec = pl.BlockSpec((None, SAMPLE_C_SEQS, M_HEADS, M_HEAD_DIM, M_HEAD_DIM),
                          lambda i, j: (l, i * nsub + j, 0, 0, 0))

    def state_spec(width, rows_per_seq):
        return pl.BlockSpec((None, R // SUBLANES, rows_per_seq, width), lambda i, j: (l, i, 0, 0))

    def state_shape(width, rows_per_seq):
        return (DEPTH, nrows // SUBLANES, rows_per_seq, width)

    state_specs = [state_spec(w, r) for w, r in zip(_SAMPLE_STATE_WIDTHS, _SAMPLE_STATE_ROWS)]
    alias_args = [mixed_prev] + (list(prev_out) if prev_out is not None else [])
    first_alias = 2 + len(row_state) + 1 + len(mats)
    aliases = {first_alias + k: k for k in range(len(alias_args))}
    out_shape = ([jax.ShapeDtypeStruct(mixed_prev.shape, BF16), jax.ShapeDtypeStruct(c_all.shape, F32)]
                 + [jax.ShapeDtypeStruct(state_shape(w, r), F32)
                    for w, r in zip(_SAMPLE_STATE_WIDTHS, _SAMPLE_STATE_ROWS)])
    scratch = ([pltpu.VMEM((R, M_WIDTH), F32)] * 5
               + [pltpu.VMEM((M_HEADS, M_HEAD_DIM, R), F32), pltpu.VMEM((M_HEADS, M_HEAD_DIM, R), F32),
                  pltpu.VMEM((M_HEADS, M_HEAD_DIM, R), BF16), pltpu.VMEM((M_HEADS, R, M_HEAD_DIM), BF16)]
               + [pltpu.VMEM((R, LANES), F32)] * 3
               + [pltpu.VMEM((R, S_LANES), F32)] * 2)
    return pl.pallas_call(
        functools.partial(_sample_mixer_kernel, rows=R, c_seqs=SAMPLE_C_SEQS, n_alias=len(alias_args)),
        grid=(nrows // R, nsub),
        in_specs=([pl.BlockSpec((R, 2 * MIX_WIDTH), lambda i, j: (i, 0)), c_spec] + state_specs
                  + [_layer_spec(vec, l)] + [_layer_spec(w, l) for w in mats] + [_ANY_SPEC] * len(alias_args)),
        out_specs=[pl.BlockSpec((R, MIX_WIDTH), lambda i, j: (blk0 + i, 0)), c_spec] + state_specs,
        out_shape=out_shape,
        scratch_shapes=scratch,
        input_output_aliases=aliases,
        compiler_params=pltpu.CompilerParams(
            dimension_semantics=("parallel", "arbitrary"), vmem_limit_bytes=VMEM_LIMIT_BYTES),
        name="sample_mixer",
    )(proj, c_all, *row_state, vec, *mats, *alias_args)


def _prepare_weights(p, s5):
    abre, abim, wbu, wcre, wcim = s5

    def split_gates(g):
        pad = [(0, 0)] * (g.ndim - 1) + [(0, LANES - M_HEADS)]
        return jnp.concatenate([jnp.pad(g[..., :M_HEADS], pad), jnp.pad(g[..., M_HEADS:], pad)], axis=-1)

    table = {"mcw": p["m_conv_w"], "rcw": p["r_conv_w"], "mcb": p["m_conv_b"], "mnw": p["m_norm_w"],
             "mskip": p["m_skip"], "rcb": p["r_conv_b"], "rba": p["r_ba"], "rbx": p["r_bx"], "rlam": p["r_lam"],
             "sd": p["s_d"], "bglu": p["s_b_glu"], "bif": split_gates(p["m_b_if"]), "abre": abre, "abim": abim,
             "norm_w": p["norm_w"]}

    def group_rows(group):
        parts = [table[name].reshape(DEPTH, n, w) for name, n, w in group]
        fill = _VEC_WIDTH - sum(w for _, _, w in group)
        if fill:
            parts.append(jnp.zeros((DEPTH, group[0][1], fill), F32))
        return parts[0] if len(parts) == 1 else jnp.concatenate(parts, axis=-1)

    vec = jnp.concatenate([group_rows(g) for g in _VEC_GROUPS]
                          + [jnp.zeros((DEPTH, _VEC_ROWS - _VEC_USED_ROWS, _VEC_WIDTH), F32)], axis=1)

    def block_diag_halves(w):
        nb = R_BLOCKS // 2
        w5 = w.reshape(DEPTH, 2, nb, w.shape[-2], w.shape[-1])
        eye = jnp.eye(nb, dtype=F32)[:, None, :, None]
        return (w5[:, :, :, :, None, :] * eye).reshape(
            DEPTH, 2, nb * w.shape[-2], nb * w.shape[-1]).astype(BF16)

    mats = {
        "wqk": jnp.concatenate([p["m_wq"], p["m_wk"]], axis=-1).astype(BF16),
        "wvo": jnp.concatenate([p["m_wv"], p["m_wo"]], axis=-1).astype(BF16),
        "wif": split_gates(p["m_w_if"]).astype(BF16),
        "rwa": block_diag_halves(p["r_wa"]), "rwx": block_diag_halves(p["r_wx"]),
        "wbu": wbu, "wcre": wcre, "wcim": wcim,
        "wglu": p["s_w_glu"].astype(BF16),
    }
    return vec, [mats[k] for k in _MATRIX_KEYS]


def _sample_rows_state(n, m, mconv, h, rconv, sre, sim):
    bsz = n.shape[1]

    def per_seq(x):
        return x.reshape(DEPTH, bsz, 1, x.shape[-1])

    m_pad = jnp.pad(m, ((0, 0), (0, 0), (0, LANES - M_HEADS)))
    return (per_seq(n.reshape(DEPTH, bsz, M_WIDTH)), per_seq(m_pad), mconv, per_seq(h), rconv,
            per_seq(sre.reshape(DEPTH, bsz, S_LANES)), per_seq(sim.reshape(DEPTH, bsz, S_LANES)))


def _sample_state_from_rows(n, m, mconv, h, rconv, sre, sim):
    bsz = n.shape[1]
    return (n.reshape(DEPTH, bsz, M_HEADS, M_HEAD_DIM), m.reshape(DEPTH, bsz, LANES)[..., :M_HEADS],
            mconv, h.reshape(DEPTH, bsz, R_WIDTH), rconv,
            sre.reshape(DEPTH, bsz, S_GROUPS, S_STATE), sim.reshape(DEPTH, bsz, S_GROUPS, S_STATE))


def _prompt_state_from_kernel(c, n, m, mtail, h, rtail, sre, sim):
    bsz = c.shape[1]
    return (c, n, m[:, :, 0, :M_HEADS], mtail[:, :, SUBLANES - 1::SUBLANES], h.reshape(DEPTH, bsz, R_WIDTH),
            rtail[:, :, SUBLANES - 1::SUBLANES],
            sre.reshape(DEPTH, bsz, S_GROUPS, S_STATE), sim.reshape(DEPTH, bsz, S_GROUPS, S_STATE))


def _interleave_chunks(x, inverse=False):
    bsz, seq, d = x.shape
    sub_len = PROMPT_CHUNK // SUBLANES
    inner = (sub_len, SUBLANES) if inverse else (SUBLANES, sub_len)
    return x.reshape(bsz, seq // PROMPT_CHUNK, *inner, d).transpose(0, 1, 3, 2, 4).reshape(bsz, seq, d)


def kernel(x_prompt, x_sample, state_mlstm_C, state_mlstm_n, state_mlstm_m, state_mlstm_conv, state_rglru_h, state_rglru_conv, state_s5_re, state_s5_im, norm_w, w_in, w_out, m_conv_w, m_conv_b, m_wq, m_wk, m_wv, m_wo, m_w_if, m_b_if, m_norm_w, m_skip, r_conv_w, r_conv_b, r_wa, r_ba, r_wx, r_bx, r_lam, s_lam_re, s_lam_im, s_b_re, s_b_im, s_c_re, s_c_im, s_d, s_log_step, s_w_glu, s_b_glu, final_norm_w):
    p = dict(norm_w=norm_w, w_in=w_in, w_out=w_out, m_conv_w=m_conv_w, m_conv_b=m_conv_b, m_wq=m_wq, m_wk=m_wk,
             m_wv=m_wv, m_wo=m_wo, m_w_if=m_w_if, m_b_if=m_b_if, m_norm_w=m_norm_w, m_skip=m_skip,
             r_conv_w=r_conv_w, r_conv_b=r_conv_b, r_wa=r_wa, r_ba=r_ba, r_wx=r_wx, r_bx=r_bx, r_lam=r_lam,
             s_d=s_d, s_w_glu=s_w_glu, s_b_glu=s_b_glu)
    s5 = _s5_prep(s_lam_re, s_lam_im, s_log_step, s_b_re, s_b_im, s_c_re, s_c_im)
    vec, mats = _prepare_weights(p, s5)
    final_w = final_norm_w.reshape(1, D_MODEL)

    bsz, seq, _ = x_prompt.shape
    dec_batch, dec_seq, _ = x_sample.shape
    n_prompt = bsz * seq
    n_sample = dec_batch * SUBLANES
    stream = [_interleave_chunks(x_prompt).reshape(n_prompt, D_MODEL),
              jnp.pad(x_sample, ((0, 0), (SAMPLE_LEAD, 0), (0, 0))).reshape(n_sample, D_MODEL)]
    rows_state = _sample_rows_state(state_mlstm_n, state_mlstm_m, state_mlstm_conv, state_rglru_h,
                                    state_rglru_conv, state_s5_re, state_s5_im)
    pr_states = None
    sa_states = None
    for l in range(DEPTH):
        last = l == DEPTH - 1
        proj_s, w_in_b = _inproj(stream[-1], stream[-1].shape[0] - n_sample, n_sample, vec, w_in, l)
        mixed, *pr_states = _prompt_mixer(stream[0], n_prompt + n_sample, l, bsz, seq, vec, w_in_b, mats, pr_states)
        mixed, *sa_states = _sample_mixer(proj_s, l, n_prompt, mixed, state_mlstm_C, rows_state, vec, mats, sa_states)
        stream = _outproj(mixed, stream, w_out, l, final_w, last, split_rows=n_prompt if last else None)
    y_prompt = _interleave_chunks(stream[0].reshape(bsz, seq, D_MODEL), inverse=True)
    y_sample = stream[1]
    return (y_prompt, y_sample, *_prompt_state_from_kernel(*pr_states),
            sa_states[0], *_sample_state_from_rows(*sa_states[1:]))
```

```python
import functools

import jax
import jax.numpy as jnp
from jax import lax
from jax.experimental import pallas as pl
from jax.experimental.pallas import tpu as pltpu

F32 = jnp.float32
BF16 = jnp.bfloat16

D_MODEL = 2048
DEPTH = 2
MIX_WIDTH = D_MODEL
M_WIDTH = MIX_WIDTH // 2
R_WIDTH = MIX_WIDTH // 4
S_WIDTH = MIX_WIDTH - M_WIDTH - R_WIDTH
M_HEADS = 8
M_HEAD_DIM = M_WIDTH // M_HEADS
R_BLOCKS = 8
RG_C = 8.0
S_GROUP = 16
S_GROUPS = S_WIDTH // S_GROUP
S_STATE = 64
S_LANES = S_GROUPS * S_STATE
CONV_W = 4
EPS = 1e-6

SUBLANES = 8
SUBLANE_SHIFT = SUBLANES.bit_length() - 1
LANES = 128
VMEM_LIMIT_BYTES = 56 * 1024 * 1024

PROMPT_CHUNK = 256
SAMPLE_ROWS = 128
SAMPLE_C_SEQS = 16
SAMPLE_LEAD = SUBLANES - 4
INPROJ_TN = 1024
OUTPROJ_TM = 512
SCAN_LANE_BLOCK = 256

_NT = (((1,), (1,)), ((), ()))

_VEC_WIDTH = D_MODEL
_VEC_GROUPS = ((("mcw", CONV_W, M_WIDTH), ("rcw", CONV_W, R_WIDTH)),
               (("mcb", 1, M_WIDTH), ("mnw", 1, M_WIDTH)),
               (("mskip", 1, M_WIDTH), ("rcb", 1, R_WIDTH), ("rba", 1, R_WIDTH)),
               (("rbx", 1, R_WIDTH), ("rlam", 1, R_WIDTH), ("sd", 1, S_WIDTH), ("bglu", 1, S_WIDTH)),
               (("bif", 1, 2 * LANES),),
               (("abre", 1, S_LANES),), (("abim", 1, S_LANES),), (("norm_w", 1, D_MODEL),))
_VEC_LAYOUT = {}
_row = 0
for _group in _VEC_GROUPS:
    _lane = 0
    for _name, _n, _w in _group:
        _VEC_LAYOUT[_name] = (_row, _n, _lane, _w)
        _lane += _w
    _row += _group[0][1]
_VEC_USED_ROWS = _row
_VEC_ROWS = -(-_row // SUBLANES) * SUBLANES


def _vec(vec_ref, name):
    r0, n, l0, w = _VEC_LAYOUT[name]
    return vec_ref[r0:r0 + n, l0:l0 + w]


def _layer_spec(arr, l):
    nd = arr.ndim - 1
    return pl.BlockSpec((None,) + arr.shape[1:], lambda *_: (l,) + (0,) * nd, pipeline_mode=pl.Buffered(1))


_ANY_SPEC = pl.BlockSpec(memory_space=pl.ANY)


S5_IN_SLICES = S_WIDTH // LANES
S5_IN_GROUPS = LANES // S_GROUP
S5_OUT_HALVES = 2
S5_OUT_GROUPS = S_GROUPS // S5_OUT_HALVES


def _s5_prep_kernel(lr_ref, li_ref, ls_ref, brt_ref, bit_ref, cre_ref, cim_ref,
                    abre_ref, abim_ref, wbu_ref, wcre_ref, wcim_ref):
    lr = lr_ref[0]
    li = li_ref[0]
    dt = jnp.exp(ls_ref[0])
    mag = jnp.exp(lr * dt)
    ang = li * dt
    ab_re = mag * jnp.cos(ang)
    ab_im = mag * jnp.sin(ang)
    den = lr * lr + li * li
    nr = ab_re - 1.0
    f_re = (nr * lr + ab_im * li) / den
    f_im = (ab_im * lr - nr * li) / den
    br = brt_ref[0]
    bi = bit_ref[0]
    abre_ref[0] = ab_re
    abim_ref[0] = ab_im
    bb_re = f_re * br - f_im * bi
    bb_im = f_re * bi + f_im * br

    per_slice = S5_IN_GROUPS * S_STATE
    lane_group = lax.broadcasted_iota(jnp.int32, (S_GROUP, per_slice), 1) >> (S_STATE.bit_length() - 1)
    for k in range(S5_IN_SLICES):
        re_k = bb_re[:, k * per_slice:(k + 1) * per_slice]
        im_k = bb_im[:, k * per_slice:(k + 1) * per_slice]
        for a in range(S5_IN_GROUPS):
            rows = slice(a * S_GROUP, (a + 1) * S_GROUP)
            wbu_ref[0, k, rows, 0:per_slice] = jnp.where(lane_group == a, re_k, 0.0).astype(BF16)
            wbu_ref[0, k, rows, per_slice:2 * per_slice] = jnp.where(lane_group == a, im_k, 0.0).astype(BF16)

    row_group = lax.broadcasted_iota(jnp.int32, (S5_OUT_GROUPS * S_GROUP, S_STATE), 0) >> (S_GROUP.bit_length() - 1)
    for src_ref, dst_ref in ((cre_ref, wcre_ref), (cim_ref, wcim_ref)):
        for m in range(S5_OUT_HALVES):
            c = src_ref[0, m]
            for a in range(S5_OUT_GROUPS):
                dst_ref[0, m, :, a * S_STATE:(a + 1) * S_STATE] = jnp.where(row_group == a, c, 0.0).astype(BF16)


def _s5_prep(s_lam_re, s_lam_im, s_log_step, s_b_re, s_b_im, s_c_re, s_c_im):
    lr = s_lam_re.reshape(DEPTH, 1, S_LANES)
    li = s_lam_im.reshape(DEPTH, 1, S_LANES)
    ls = jnp.repeat(s_log_step, S_STATE, axis=-1).reshape(DEPTH, 1, S_LANES)
    brt = s_b_re.reshape(DEPTH, S_LANES, S_GROUP).transpose(0, 2, 1)
    bit = s_b_im.reshape(DEPTH, S_LANES, S_GROUP).transpose(0, 2, 1)
    c_shape = (DEPTH, S5_OUT_HALVES, S5_OUT_GROUPS * S_GROUP, S_STATE)
    wbu_shape = (DEPTH, S5_IN_SLICES, LANES, 2 * S5_IN_GROUPS * S_STATE)
    wc_shape = (DEPTH, S5_OUT_HALVES, S5_OUT_GROUPS * S_GROUP, S5_OUT_GROUPS * S_STATE)

    def layer_block(shape):
        nd = len(shape) - 1
        return pl.BlockSpec((1,) + shape[1:], lambda l: (l,) + (0,) * nd)

    vec = layer_block((DEPTH, 1, S_LANES))
    mat = layer_block((DEPTH, S_GROUP, S_LANES))
    return pl.pallas_call(
        _s5_prep_kernel,
        grid=(DEPTH,),
        in_specs=[vec, vec, vec, mat, mat, layer_block(c_shape), layer_block(c_shape)],
        out_specs=[vec, vec, layer_block(wbu_shape), layer_block(wc_shape), layer_block(wc_shape)],
        out_shape=[jax.ShapeDtypeStruct((DEPTH, 1, S_LANES), F32),
                   jax.ShapeDtypeStruct((DEPTH, 1, S_LANES), F32),
                   jax.ShapeDtypeStruct(wbu_shape, BF16),
                   jax.ShapeDtypeStruct(wc_shape, BF16),
                   jax.ShapeDtypeStruct(wc_shape, BF16)],
        name="s5_prep",
    )(lr, li, ls, brt, bit, s_c_re.reshape(c_shape), s_c_im.reshape(c_shape))


def _stream_specs(parts, tm):
    if len(parts) == 1:
        return [pl.BlockSpec((tm, D_MODEL), lambda i: (i, 0))]
    nb0 = parts[0].shape[0] // tm
    return [pl.BlockSpec((tm, D_MODEL), lambda i: (jnp.minimum(i, nb0 - 1), 0)),
            pl.BlockSpec((tm, D_MODEL), lambda i: (jnp.maximum(i - nb0, 0), 0))]


def _read_stream(refs, nb0):
    if len(refs) == 1:
        return refs[0][...]
    return jnp.where(pl.program_id(0) < nb0, refs[0][...], refs[1][...])


def _rmsnorm_bf16(x, vec_ref):
    ms = jnp.mean(x * x, axis=-1, keepdims=True)
    return (x * lax.rsqrt(ms + EPS) * _vec(vec_ref, "norm_w")).astype(BF16)


def _inproj_kernel(x_ref, vec_ref, w_ref, o_ref, wb_ref, xn_scr):
    @pl.when(pl.program_id(0) == 0)
    def _():
        xn_scr[...] = _rmsnorm_bf16(x_ref[...], vec_ref)

    wb = w_ref[...].astype(BF16)
    wb_ref[...] = wb
    o_ref[...] = jnp.dot(xn_scr[...], wb, preferred_element_type=F32)


def _inproj(x, row0, nrows, vec, w_in, l):
    blk0 = row0 // nrows
    return pl.pallas_call(
        _inproj_kernel,
        grid=((2 * MIX_WIDTH) // INPROJ_TN,),
        in_specs=[pl.BlockSpec((nrows, D_MODEL), lambda j: (blk0, 0), pipeline_mode=pl.Buffered(1)),
                  _layer_spec(vec, l),
                  pl.BlockSpec((None, D_MODEL, INPROJ_TN), lambda j: (l, 0, j))],
        out_specs=[pl.BlockSpec((nrows, INPROJ_TN), lambda j: (0, j)),
                   pl.BlockSpec((D_MODEL, INPROJ_TN), lambda j: (0, j))],
        out_shape=[jax.ShapeDtypeStruct((nrows, 2 * MIX_WIDTH), F32),
                   jax.ShapeDtypeStruct((D_MODEL, 2 * MIX_WIDTH), BF16)],
        scratch_shapes=[pltpu.VMEM((nrows, D_MODEL), BF16)],
        compiler_params=pltpu.CompilerParams(
            dimension_semantics=("arbitrary",), vmem_limit_bytes=VMEM_LIMIT_BYTES),
        name="inproj",
    )(x, vec, w_in)


def _outproj_kernel(*refs, n_x, nb0, n_out, nb0_out, final):
    mixed_ref, x_refs = refs[0], refs[1:1 + n_x]
    w_ref, fw_ref = refs[1 + n_x:3 + n_x]
    o_refs, wb_scr = refs[3 + n_x:-1], refs[-1]

    @pl.when(pl.program_id(0) == 0)
    def _():
        wb_scr[...] = w_ref[...].astype(BF16)

    y = _read_stream(x_refs, nb0) + jnp.dot(mixed_ref[...], wb_scr[...], preferred_element_type=F32)
    if final:
        ms = jnp.mean(y * y, axis=-1, keepdims=True)
        y = y * lax.rsqrt(ms + EPS) * fw_ref[...]
    if n_out == 1:
        o_refs[0][...] = y
    else:
        @pl.when(pl.program_id(0) < nb0_out)
        def _():
            o_refs[0][...] = y

        @pl.when(pl.program_id(0) >= nb0_out)
        def _():
            o_refs[1][...] = y.reshape(y.shape[0] // SUBLANES, SUBLANES, y.shape[1])[:, SAMPLE_LEAD:, :]


def _outproj(mixed, x_parts, w_out, l, final_w, final, split_rows=None):
    n = mixed.shape[0]
    if split_rows is None:
        out_parts = [jax.ShapeDtypeStruct((n, D_MODEL), F32)]
        out_specs = [pl.BlockSpec((OUTPROJ_TM, D_MODEL), lambda i: (i, 0))]
        nb0_out = n // OUTPROJ_TM
    else:
        nb0_out = split_rows // OUTPROJ_TM
        tokens = SUBLANES - SAMPLE_LEAD
        out_parts = [jax.ShapeDtypeStruct((split_rows, D_MODEL), F32),
                     jax.ShapeDtypeStruct(((n - split_rows) // SUBLANES, tokens, D_MODEL), F32)]
        out_specs = [pl.BlockSpec((OUTPROJ_TM, D_MODEL), lambda i: (jnp.minimum(i, nb0_out - 1), 0)),
                     pl.BlockSpec((OUTPROJ_TM // SUBLANES, tokens, D_MODEL),
                                  lambda i: (jnp.maximum(i - nb0_out, 0), 0, 0))]
    return pl.pallas_call(
        functools.partial(_outproj_kernel, n_x=len(x_parts), nb0=x_parts[0].shape[0] // OUTPROJ_TM,
                          n_out=len(out_parts), nb0_out=nb0_out, final=final),
        grid=(n // OUTPROJ_TM,),
        in_specs=([pl.BlockSpec((OUTPROJ_TM, MIX_WIDTH), lambda i: (i, 0))] + _stream_specs(x_parts, OUTPROJ_TM)
                  + [_layer_spec(w_out, l), pl.BlockSpec((1, D_MODEL), lambda i: (0, 0))]),
        out_specs=out_specs,
        out_shape=out_parts,
        scratch_shapes=[pltpu.VMEM((MIX_WIDTH, D_MODEL), BF16)],
        compiler_params=pltpu.CompilerParams(
            dimension_semantics=("arbitrary",), vmem_limit_bytes=VMEM_LIMIT_BYTES),
        name="outproj",
    )(mixed, *x_parts, w_out, final_w)


def _silu(x):
    return x * jax.nn.sigmoid(x)


def _log_sigmoid(x):
    return jnp.minimum(x, 0.0) - jnp.log1p(jnp.exp(-jnp.abs(x)))


def _softplus(x):
    return jnp.maximum(x, 0.0) + jnp.log1p(jnp.exp(-jnp.abs(x)))


def _sublane_pos(shape):
    return lax.broadcasted_iota(jnp.int32, shape, 0) & (SUBLANES - 1)


def _tile_cumsum(x):
    pos = _sublane_pos(x.shape)
    s = 1
    while s < SUBLANES:
        x = x + jnp.where(pos >= s, pltpu.roll(x, s, 0), 0.0)
        s *= 2
    return x


def _tile_cummax(x):
    pos = _sublane_pos(x.shape)
    s = 1
    while s < SUBLANES:
        x = jnp.maximum(x, jnp.where(pos >= s, pltpu.roll(x, s, 0), -jnp.inf))
        s *= 2
    return x


def _tile_scan_real(a, b):
    pos = _sublane_pos(a.shape)
    s = 1
    while s < SUBLANES:
        m = pos >= s
        b = jnp.where(m, a * pltpu.roll(b, s, 0) + b, b)
        if 2 * s < SUBLANES:
            a = jnp.where(m, a * pltpu.roll(a, s, 0), a)
        s *= 2
    return b


def _tile_scan_cplx(sr, si, pr, pi):
    pos = _sublane_pos(sr.shape)
    s = 1
    while s < SUBLANES:
        m = pos >= s
        sr_sh = pltpu.roll(sr, s, 0)
        si_sh = pltpu.roll(si, s, 0)
        sr, si = (jnp.where(m, sr + (pr * sr_sh - pi * si_sh), sr),
                  jnp.where(m, si + (pr * si_sh + pi * sr_sh), si))
        if 2 * s < SUBLANES:
            pr, pi = pr * pr - pi * pi, 2.0 * (pr * pi)
        s *= 2
    return sr, si


def _mlstm_project(xc, xm, vec_ref, wqk_ref, wvo_ref, wif_ref, q_scr, k_scr, v_scr, o_scr):
    xc_b = xc.astype(BF16)
    xm_b = xm.astype(BF16)
    for h in range(M_HEADS):
        sl = slice(h * M_HEAD_DIM, (h + 1) * M_HEAD_DIM)
        qk = jnp.dot(xc_b[:, sl], wqk_ref[h], preferred_element_type=F32)
        vo = jnp.dot(xm_b[:, sl], wvo_ref[h], preferred_element_type=F32)
        q_scr[:, sl] = qk[:, :M_HEAD_DIM]
        k_scr[:, sl] = qk[:, M_HEAD_DIM:]
        v_scr[:, sl] = vo[:, :M_HEAD_DIM]
        o_scr[:, sl] = vo[:, M_HEAD_DIM:]
    return (jnp.dot(q_scr[...].astype(BF16), wif_ref[0:M_WIDTH, :], preferred_element_type=F32)
            + jnp.dot(k_scr[...].astype(BF16), wif_ref[M_WIDTH:2 * M_WIDTH, :], preferred_element_type=F32)
            + jnp.dot(v_scr[...].astype(BF16), wif_ref[2 * M_WIDTH:3 * M_WIDTH, :], preferred_element_type=F32)
            + _vec(vec_ref, "bif"))


def _head_output(hh, o_pre, xc_h, z_h, mnw_h, mskip_h):
    mu = jnp.mean(hh, axis=1, keepdims=True)
    hc = hh - mu
    var = jnp.mean(hc * hc, axis=1, keepdims=True)
    hn = hc * lax.rsqrt(var + EPS) * mnw_h
    return (jax.nn.sigmoid(o_pre) * hn + mskip_h * xc_h) * _silu(z_h)


def _rglru_coeffs(xcr, vec_ref, rwa_ref, rwx_ref):
    xcr_b = xcr.astype(BF16)
    half = R_WIDTH // 2
    ra_pre = jnp.concatenate(
        [jnp.dot(xcr_b[:, :half], rwa_ref[0], preferred_element_type=F32),
         jnp.dot(xcr_b[:, half:], rwa_ref[1], preferred_element_type=F32)], axis=1) + _vec(vec_ref, "rba")
    rx_pre = jnp.concatenate(
        [jnp.dot(xcr_b[:, :half], rwx_ref[0], preferred_element_type=F32),
         jnp.dot(xcr_b[:, half:], rwx_ref[1], preferred_element_type=F32)], axis=1) + _vec(vec_ref, "rbx")
    log_a = (-RG_C) * jax.nn.sigmoid(ra_pre) * _softplus(-_vec(vec_ref, "rlam"))
    a = jnp.exp(log_a)
    th = jnp.tanh(log_a)
    one_minus_a2 = (-2.0 * th) / (1.0 - th)
    return a, jnp.sqrt(one_minus_a2) * (jax.nn.sigmoid(rx_pre) * xcr)


def _s5_project(u_b, wbu_ref, re_ref, im_ref):
    per_slice = (LANES // S_GROUP) * S_STATE
    for k in range(S_WIDTH // LANES):
        res = jnp.dot(u_b[:, k * LANES:(k + 1) * LANES], wbu_ref[k], preferred_element_type=F32)
        re_ref[:, k * per_slice:(k + 1) * per_slice] = res[:, :per_slice]
        im_ref[:, k * per_slice:(k + 1) * per_slice] = res[:, per_slice:]


def _s5_output(re_ref, im_ref, u, vec_ref, wcre_ref, wcim_ref, wglu_ref):
    nblk = wcre_ref.shape[0]
    k_blk = S_LANES // nblk
    parts = []
    for m in range(nblk):
        ks = slice(m * k_blk, (m + 1) * k_blk)
        parts.append(lax.dot_general(re_ref[:, ks].astype(BF16), wcre_ref[m], _NT, preferred_element_type=F32)
                     - lax.dot_general(im_ref[:, ks].astype(BF16), wcim_ref[m], _NT, preferred_element_type=F32))
    y = jnp.concatenate(parts, axis=1) + _vec(vec_ref, "sd") * u
    g = jax.nn.gelu(y)
    return g * jax.nn.sigmoid(jnp.dot(g.astype(BF16), wglu_ref[...], preferred_element_type=F32)
                              + _vec(vec_ref, "bglu"))


_MATRIX_KEYS = ("wqk", "wvo", "wif", "rwa", "rwx", "wbu", "wcre", "wcim", "wglu")


def _tile(x, i):
    return x[i * SUBLANES:(i + 1) * SUBLANES]


def _conv_interleaved(x, tail_ref, w, bias, rows):
    ntiles = rows // SUBLANES
    pos = lax.broadcasted_iota(jnp.int32, (SUBLANES, x.shape[1]), 0)
    prev = tail_ref[...]
    before = [pltpu.roll(jnp.where(pos == SUBLANES - 1, _tile(prev, CONV_W - 1 - d), _tile(x, ntiles - d)), 1, 0)
              for d in range(1, CONV_W)]
    out = w[CONV_W - 1:CONV_W, :] * x + bias
    for j in range(1, CONV_W):
        shifted = jnp.concatenate(before[:j][::-1] + [x[:rows - j * SUBLANES]], axis=0)
        out = out + w[CONV_W - 1 - j:CONV_W - j, :] * shifted
    tail_ref[...] = x[rows - (CONV_W - 1) * SUBLANES:]
    return out


def _cumsum_interleaved(x, rows):
    tiles = [_tile(x, 0)]
    for i in range(1, rows // SUBLANES):
        tiles.append(tiles[-1] + _tile(x, i))
    total = tiles[-1]
    start = _tile_cumsum(total) - total
    return jnp.concatenate([t + start for t in tiles], axis=0)


def _cummax_interleaved(x, rows):
    tiles = [_tile(x, 0)]
    for i in range(1, rows // SUBLANES):
        tiles.append(jnp.maximum(tiles[-1], _tile(x, i)))
    best = tiles[-1]
    pos = lax.broadcasted_iota(jnp.int32, best.shape, 0)
    s = 1
    while s < SUBLANES:
        best = jnp.maximum(best, jnp.where(pos >= s, pltpu.roll(best, s, 0), -jnp.inf))
        s *= 2
    start = jnp.where(pos == 0, -jnp.inf, pltpu.roll(best, 1, 0))
    return jnp.concatenate([jnp.maximum(t, start) for t in tiles], axis=0)


def _scan_real_interleaved(a_ref, b_ref, carry, rows):
    ntiles = rows // SUBLANES
    pos = lax.broadcasted_iota(jnp.int32, (SUBLANES, a_ref.shape[1]), 0)
    h = _tile(b_ref, 0)
    aprod = _tile(a_ref, 0)
    for i in range(1, ntiles):
        a = _tile(a_ref, i)
        h = a * h + _tile(b_ref, i)
        aprod = aprod * a
    g = _tile_scan_real(aprod, h + jnp.where(pos == 0, aprod * carry, 0.0))
    h = jnp.where(pos == 0, carry, pltpu.roll(g, 1, 0))
    for i in range(ntiles):
        h = _tile(a_ref, i) * h + _tile(b_ref, i)
        b_ref[i * SUBLANES:(i + 1) * SUBLANES, :] = h
    return g[SUBLANES - 1:SUBLANES]


def _scan_cplx_interleaved(re_ref, im_ref, p_re, p_im, cre_ref, cim_ref, rows):
    ntiles = rows // SUBLANES
    width = re_ref.shape[1]
    pos = lax.broadcasted_iota(jnp.int32, (SUBLANES, SCAN_LANE_BLOCK), 0)
    for blk in range(width // SCAN_LANE_BLOCK):
        sl = slice(blk * SCAN_LANE_BLOCK, (blk + 1) * SCAN_LANE_BLOCK)
        pr = jnp.broadcast_to(p_re[:, sl], pos.shape)
        pi = jnp.broadcast_to(p_im[:, sl], pos.shape)
        c_r = cre_ref[:, sl]
        c_i = cim_ref[:, sl]
        sr = re_ref[0:SUBLANES, sl]
        si = im_ref[0:SUBLANES, sl]
        for i in range(1, ntiles):
            rs = slice(i * SUBLANES, (i + 1) * SUBLANES)
            sr, si = pr * sr - pi * si + re_ref[rs, sl], pr * si + pi * sr + im_ref[rs, sl]
        qr, qi = pr, pi
        n = 1
        while n < ntiles:
            qr, qi = qr * qr - qi * qi, 2.0 * (qr * qi)
            n *= 2
        gr, gi = _tile_scan_cplx(sr + jnp.where(pos == 0, qr * c_r - qi * c_i, 0.0),
                                 si + jnp.where(pos == 0, qr * c_i + qi * c_r, 0.0), qr, qi)
        sr = jnp.where(pos == 0, c_r, pltpu.roll(gr, 1, 0))
        si = jnp.where(pos == 0, c_i, pltpu.roll(gi, 1, 0))
        for i in range(ntiles):
            rs = slice(i * SUBLANES, (i + 1) * SUBLANES)
            sr, si = pr * sr - pi * si + re_ref[rs, sl], pr * si + pi * sr + im_ref[rs, sl]
            re_ref[rs, sl] = sr
            im_ref[rs, sl] = si
        cre_ref[:, sl] = gr[SUBLANES - 1:SUBLANES]
        cim_ref[:, sl] = gi[SUBLANES - 1:SUBLANES]


def _prompt_mixer_kernel(*refs, rows, nchunk, n_alias):
    (xnext_ref, xfirst_ref, vec_ref, win_ref,
     wqk_ref, wvo_ref, wif_ref, rwa_ref, rwx_ref, wbu_ref, wcre_ref, wcim_ref, wglu_ref) = refs[:13]
    (mixed_ref, c_ref, n_ref, m_ref, mtail_ref, h_ref, rtail_ref, sre_ref, sim_ref,
     q_scr, k_scr, v_scr, o_scr, ra_scr, rb_scr, ure_scr, uim_scr, proj_scr, xn_scr) = refs[13 + n_alias:]
    T = rows
    step = pl.program_id(0)
    slot = lax.rem(step, 2)
    proj_ref = proj_scr.at[slot]
    next_proj_ref = proj_scr.at[1 - slot]
    slab = (2 * MIX_WIDTH) // M_HEADS

    @pl.when(step == 0)
    def _():
        xn = _rmsnorm_bf16(xfirst_ref[...], vec_ref)
        for j in range(M_HEADS):
            cols = slice(j * slab, (j + 1) * slab)
            proj_scr[0, :, cols] = jnp.dot(xn, win_ref[:, cols], preferred_element_type=F32)

    @pl.when(lax.rem(step, nchunk) == 0)
    def _():
        for ref in (c_ref, n_ref, m_ref, mtail_ref, h_ref, rtail_ref, sre_ref, sim_ref):
            ref[...] = jnp.zeros(ref.shape, F32)

    xn_scr[...] = _rmsnorm_bf16(xnext_ref[...], vec_ref)

    xm = proj_ref[:, 0:M_WIDTH]
    xc = _silu(_conv_interleaved(xm, mtail_ref, _vec(vec_ref, "mcw"), _vec(vec_ref, "mcb"), T))
    gates = _mlstm_project(xc, xm, vec_ref, wqk_ref, wvo_ref, wif_ref, q_scr, k_scr, v_scr, o_scr)
    log_i = gates[:, :LANES]
    b_all = _cumsum_interleaved(_log_sigmoid(gates[:, LANES:]), T)
    a_max = _cummax_interleaved(log_i - b_all, T)
    m_prev = m_ref[...]
    log_inter = b_all + m_prev
    m_t_all = jnp.maximum(log_inter, b_all + a_max)
    w_inter_all = jnp.exp(log_inter - m_t_all)
    floor_all = jnp.exp(-m_t_all)
    b_end = b_all[T - 1:T, :]
    m_new = jnp.maximum(b_end + m_prev, b_end + a_max[T - 1:T, :])
    w_src_all = jnp.exp(b_end - b_all + log_i - m_new)
    decay_all = jnp.exp(b_end + m_prev - m_new)
    m_ref[...] = m_new
    b_t = b_all.T
    li_t = log_i.T

    rowi = lax.broadcasted_iota(jnp.int32, (T, T), 0)
    coli = lax.broadcasted_iota(jnp.int32, (T, T), 1)
    sub_len = T // SUBLANES

    def time_of(r):
        return (r & (SUBLANES - 1)) * sub_len + (r >> SUBLANE_SHIFT)

    causal = time_of(coli) <= time_of(rowi)
    k_scale = M_HEAD_DIM ** -0.5
    mnw = _vec(vec_ref, "mnw")
    mskip = _vec(vec_ref, "mskip")

    def project_next(j):
        cols = slice(j * slab, (j + 1) * slab)
        next_proj_ref[:, cols] = jnp.dot(xn_scr[...], win_ref[:, cols], preferred_element_type=F32)

    project_next(0)
    project_next(1)
    for h in range(M_HEADS):
        if h + 2 < M_HEADS:
            project_next(h + 2)

        sl = slice(h * M_HEAD_DIM, (h + 1) * M_HEAD_DIM)
        c_prev = c_ref[h]
        n_prev = n_ref[h:h + 1, :]
        w_inter = w_inter_all[:, h:h + 1]
        w_src = w_src_all[:, h:h + 1]
        decay = decay_all[:, h:h + 1]

        q = q_scr[:, sl]
        ks = k_scr[:, sl] * k_scale
        v = v_scr[:, sl]
        q_b = q.astype(BF16)
        ks_b = ks.astype(BF16)

        log_d = jnp.where(causal, b_all[:, h:h + 1] - b_t[h:h + 1, :] + li_t[h:h + 1, :], -jnp.inf)
        w_intra = jnp.exp(log_d - m_t_all[:, h:h + 1])
        s = lax.dot_general(q_b, ks_b, _NT, preferred_element_type=F32) * w_intra
        inter = lax.dot_general(q_b, c_prev.astype(BF16), _NT, preferred_element_type=F32)
        num = jnp.dot(s.astype(BF16), v.astype(BF16), preferred_element_type=F32) + w_inter * inter
        den = (jnp.sum(s, axis=1, keepdims=True)
               + w_inter * jnp.sum(q * n_prev, axis=1, keepdims=True))
        hh = num / jnp.maximum(jnp.abs(den), floor_all[:, h:h + 1])

        vw_t = (v * w_src).T.astype(BF16)
        c_ref[h] = decay * c_prev + jnp.dot(vw_t, ks_b, preferred_element_type=F32)
        n_ref[h:h + 1, :] = decay * n_prev + jnp.sum(ks * w_src, axis=0, keepdims=True)

        z = proj_ref[:, MIX_WIDTH + h * M_HEAD_DIM:MIX_WIDTH + (h + 1) * M_HEAD_DIM]
        mixed_ref[:, sl] = _head_output(hh, o_scr[:, sl], xc[:, sl], z, mnw[:, sl], mskip[:, sl]).astype(BF16)

    xr = proj_ref[:, M_WIDTH:M_WIDTH + R_WIDTH]
    xcr = _conv_interleaved(xr, rtail_ref, _vec(vec_ref, "rcw"), _vec(vec_ref, "rcb"), T)
    a, bb = _rglru_coeffs(xcr, vec_ref, rwa_ref, rwx_ref)
    ra_scr[...] = a
    rb_scr[...] = bb
    h_ref[...] = _scan_real_interleaved(ra_scr, rb_scr, h_ref[...], T)
    zr = proj_ref[:, MIX_WIDTH + M_WIDTH:MIX_WIDTH + M_WIDTH + R_WIDTH]
    mixed_ref[:, M_WIDTH:M_WIDTH + R_WIDTH] = (rb_scr[...] * _silu(zr)).astype(BF16)

    u = proj_ref[:, M_WIDTH + R_WIDTH:MIX_WIDTH]
    _s5_project(u.astype(BF16), wbu_ref, ure_scr, uim_scr)
    _scan_cplx_interleaved(ure_scr, uim_scr, _vec(vec_ref, "abre"), _vec(vec_ref, "abim"), sre_ref, sim_ref, T)
    glu = _s5_output(ure_scr, uim_scr, u, vec_ref, wcre_ref, wcim_ref, wglu_ref)
    zs = proj_ref[:, MIX_WIDTH + M_WIDTH + R_WIDTH:2 * MIX_WIDTH]
    mixed_ref[:, M_WIDTH + R_WIDTH:MIX_WIDTH] = (glu * _silu(zs)).astype(BF16)


_PROMPT_STATE_SHAPES = ((M_HEADS, M_HEAD_DIM, M_HEAD_DIM), (M_HEADS, M_HEAD_DIM), (1, LANES),
                        ((CONV_W - 1) * SUBLANES, M_WIDTH), (1, R_WIDTH), ((CONV_W - 1) * SUBLANES, R_WIDTH),
                        (1, S_LANES), (1, S_LANES))


def _prompt_mixer(x, total_rows, l, bsz, seq, vec, w_in_b, mats, prev_states):
    rows = PROMPT_CHUNK
    nchunk = seq // rows
    nstep = bsz * nchunk
    aliases = {}
    alias_args = []
    n_in = 4 + len(mats)
    if prev_states is not None:
        alias_args = list(prev_states)
        aliases = {n_in + k: 1 + k for k in range(len(alias_args))}

    def state_spec(shape):
        nd = len(shape)
        return pl.BlockSpec((None, None) + shape, lambda t: (l, t // nchunk) + (0,) * nd)

    out_shape = ([jax.ShapeDtypeStruct((total_rows, MIX_WIDTH), BF16)]
                 + [jax.ShapeDtypeStruct((DEPTH, bsz) + s, F32) for s in _PROMPT_STATE_SHAPES])
    scratch = ([pltpu.VMEM((rows, w), F32)
                for w in (M_WIDTH, M_WIDTH, M_WIDTH, M_WIDTH, R_WIDTH, R_WIDTH, S_LANES, S_LANES)]
               + [pltpu.VMEM((2, rows, 2 * MIX_WIDTH), F32), pltpu.VMEM((rows, D_MODEL), BF16)])
    return pl.pallas_call(
        functools.partial(_prompt_mixer_kernel, rows=rows, nchunk=nchunk, n_alias=len(alias_args)),
        grid=(nstep,),
        in_specs=([pl.BlockSpec((rows, D_MODEL), lambda t: (jnp.minimum(t + 1, nstep - 1), 0)),
                   pl.BlockSpec((rows, D_MODEL), lambda t: (0, 0), pipeline_mode=pl.Buffered(1)),
                   _layer_spec(vec, l),
                   pl.BlockSpec(w_in_b.shape, lambda t: (0, 0), pipeline_mode=pl.Buffered(1))]
                  + [_layer_spec(w, l) for w in mats] + [_ANY_SPEC] * len(alias_args)),
        out_specs=([pl.BlockSpec((rows, MIX_WIDTH), lambda t: (t, 0))]
                   + [state_spec(s) for s in _PROMPT_STATE_SHAPES]),
        out_shape=out_shape,
        scratch_shapes=scratch,
        input_output_aliases=aliases,
        compiler_params=pltpu.CompilerParams(
            dimension_semantics=("arbitrary",), vmem_limit_bytes=VMEM_LIMIT_BYTES),
        name="prompt_mixer",
    )(x, x, vec, w_in_b, *mats, *alias_args)


def _seg_last(x, groups):
    x3 = x.reshape(groups, SUBLANES, x.shape[-1])
    return jnp.broadcast_to(x3[:, SUBLANES - 1:SUBLANES, :], x3.shape).reshape(x.shape)


def _seg_max(x, groups):
    x3 = x.reshape(groups, SUBLANES, x.shape[-1])
    return jnp.broadcast_to(jnp.max(x3, axis=1, keepdims=True), x3.shape).reshape(x.shape)


def _seg_sum(x, groups):
    x3 = x.reshape(groups, SUBLANES, x.shape[-1])
    return jnp.broadcast_to(jnp.sum(x3, axis=1, keepdims=True), x3.shape).reshape(x.shape)


def _seg_rows(state_ref, lanes=slice(None)):
    x = state_ref[:, :, lanes]
    return jnp.broadcast_to(x, (x.shape[0], SUBLANES, x.shape[2])).reshape(x.shape[0] * SUBLANES, x.shape[2])


def _seg_state(x):
    x3 = x.reshape(x.shape[0] // SUBLANES, SUBLANES, x.shape[-1])
    return x3[:, SUBLANES - 1:SUBLANES, :]


def _with_history(x, buf_ref, new_buf_ref):
    g = x.shape[0] // SUBLANES
    x3 = x.reshape(g, SUBLANES, x.shape[1])
    lead = SAMPLE_LEAD - (CONV_W - 1)
    full = jnp.concatenate([jnp.zeros((g, lead, x.shape[1]), F32), buf_ref[...], x3[:, SAMPLE_LEAD:, :]], axis=1)
    new_buf_ref[...] = full[:, SUBLANES - (CONV_W - 1):, :]
    return full.reshape(x.shape)


def _conv_rolled(xf, w, bias):
    out = w[CONV_W - 1:CONV_W, :] * xf + bias
    for j in range(1, CONV_W):
        out = out + w[CONV_W - 1 - j:CONV_W - j, :] * pltpu.roll(xf, j, 0)
    return out


_SAMPLE_STATE_WIDTHS = (M_WIDTH, LANES, M_WIDTH, R_WIDTH, R_WIDTH, S_LANES, S_LANES)
_SAMPLE_STATE_ROWS = (1, 1, CONV_W - 1, 1, CONV_W - 1, 1, 1)


def _sample_mixer_kernel(*refs, rows, c_seqs, n_alias):
    (proj_ref, c0_ref, n0_ref, m0_ref, mtail0_ref, h0_ref, rtail0_ref, sre0_ref, sim0_ref,
     vec_ref, wqk_ref, wvo_ref, wif_ref, rwa_ref, rwx_ref, wbu_ref, wcre_ref, wcim_ref, wglu_ref) = refs[:19]
    (mixed_ref, c_ref, n_ref, m_ref, mconv_ref, h_ref, rconv_ref, sre_ref, sim_ref,
     q_scr, k_scr, v_scr, o_scr, xc_scr, intert_scr, vwt_scr, qt_scr, ksb_scr, dec_scr, b_scr, li_scr,
     ure_scr, uim_scr) = refs[19 + n_alias:]
    R = rows
    G = R // SUBLANES
    sub = pl.program_id(1)
    k_scale = M_HEAD_DIM ** -0.5
    pos = _sublane_pos((R, 1))
    valid = pos >= SAMPLE_LEAD
    first_token = pos == SAMPLE_LEAD

    @pl.when(sub == 0)
    def _():
        xm = _with_history(proj_ref[:, 0:M_WIDTH], mtail0_ref, mconv_ref)
        xc = _silu(_conv_rolled(xm, _vec(vec_ref, "mcw"), _vec(vec_ref, "mcb")))
        xc_scr[...] = xc
        gates = _mlstm_project(xc, xm, vec_ref, wqk_ref, wvo_ref, wif_ref, q_scr, k_scr, v_scr, o_scr)
        log_i = jnp.where(valid, gates[:, :LANES], -jnp.inf)
        log_f = jnp.where(valid, _log_sigmoid(gates[:, LANES:]), 0.0)
        b_all = _tile_cumsum(log_f)
        b_end = _seg_last(b_all, G)
        m_rows = _seg_rows(m0_ref)
        n_rows = _seg_rows(n0_ref)
        log_src = b_end - b_all + log_i
        m_new = jnp.maximum(b_end + m_rows, _seg_max(log_src, G))
        w_src_all = jnp.exp(log_src - m_new)
        decay_all = jnp.exp(b_end + m_rows - m_new)
        dec_scr[...] = decay_all
        b_scr[...] = b_all
        li_scr[...] = log_i
        m_ref[...] = _seg_state(m_new)
        for h in range(M_HEADS):
            sl = slice(h * M_HEAD_DIM, (h + 1) * M_HEAD_DIM)
            ks = k_scr[:, sl] * k_scale
            w_src = w_src_all[:, h:h + 1]
            n_ref[:, :, sl] = _seg_state(decay_all[:, h:h + 1] * n_rows[:, sl] + _seg_sum(ks * w_src, G))
            vwt_scr[h] = (v_scr[:, sl] * w_src).T
            qt_scr[h] = q_scr[:, sl].T.astype(BF16)
            ksb_scr[h] = ks.astype(BF16)
            intert_scr[h] = jnp.zeros((M_HEAD_DIM, R), F32)

        xr = _with_history(proj_ref[:, M_WIDTH:M_WIDTH + R_WIDTH], rtail0_ref, rconv_ref)
        xcr = _conv_rolled(xr, _vec(vec_ref, "rcw"), _vec(vec_ref, "rcb"))
        a, bb = _rglru_coeffs(xcr, vec_ref, rwa_ref, rwx_ref)
        hs = _tile_scan_real(a, jnp.where(valid, bb, 0.0) + jnp.where(first_token, a * _seg_rows(h0_ref), 0.0))
        h_ref[...] = _seg_state(hs)
        zr = proj_ref[:, MIX_WIDTH + M_WIDTH:MIX_WIDTH + M_WIDTH + R_WIDTH]
        mixed_ref[:, M_WIDTH:M_WIDTH + R_WIDTH] = jnp.where(valid, hs * _silu(zr), 0.0).astype(BF16)

        u = proj_ref[:, M_WIDTH + R_WIDTH:MIX_WIDTH]
        _s5_project(u.astype(BF16), wbu_ref, ure_scr, uim_scr)
        p_re = _vec(vec_ref, "abre")
        p_im = _vec(vec_ref, "abim")
        for blk in range(S_LANES // SCAN_LANE_BLOCK):
            sl = slice(blk * SCAN_LANE_BLOCK, (blk + 1) * SCAN_LANE_BLOCK)
            pr = jnp.broadcast_to(p_re[:, sl], (R, SCAN_LANE_BLOCK))
            pi = jnp.broadcast_to(p_im[:, sl], (R, SCAN_LANE_BLOCK))
            s0r = _seg_rows(sre0_ref, sl)
            s0i = _seg_rows(sim0_ref, sl)
            sr, si = _tile_scan_cplx(
                jnp.where(valid, ure_scr[:, sl], 0.0) + jnp.where(first_token, pr * s0r - pi * s0i, 0.0),
                jnp.where(valid, uim_scr[:, sl], 0.0) + jnp.where(first_token, pr * s0i + pi * s0r, 0.0), pr, pi)
            ure_scr[:, sl] = sr
            uim_scr[:, sl] = si
            sre_ref[:, :, sl] = _seg_state(sr)
            sim_ref[:, :, sl] = _seg_state(si)
        glu = _s5_output(ure_scr, uim_scr, u, vec_ref, wcre_ref, wcim_ref, wglu_ref)
        zs = proj_ref[:, MIX_WIDTH + M_WIDTH + R_WIDTH:2 * MIX_WIDTH]
        mixed_ref[:, M_WIDTH + R_WIDTH:MIX_WIDTH] = jnp.where(valid, glu * _silu(zs), 0.0).astype(BF16)

    lane_seq = lax.broadcasted_iota(jnp.int32, (M_HEAD_DIM, R), 1) >> SUBLANE_SHIFT
    seq0 = sub * c_seqs
    for h in range(M_HEADS):
        c_old = c0_ref[:, h].reshape(c_seqs * M_HEAD_DIM, M_HEAD_DIM)
        readout = jnp.dot(c_old.astype(BF16), qt_scr[h], preferred_element_type=F32)
        acc = intert_scr[h]
        vwt = vwt_scr[h]
        lhs = []
        for s in range(c_seqs):
            own = lane_seq == seq0 + s
            acc = jnp.where(own, readout[s * M_HEAD_DIM:(s + 1) * M_HEAD_DIM], acc)
            lhs.append(jnp.where(own, vwt, 0.0))
        intert_scr[h] = acc
        upd = jnp.dot(jnp.concatenate(lhs, axis=0).astype(BF16), ksb_scr[h], preferred_element_type=F32)
        for s in range(c_seqs):
            r0 = pl.multiple_of((seq0 + s) * SUBLANES, SUBLANES)
            c_ref[s, h] = (dec_scr[pl.ds(r0, 1), h:h + 1] * c0_ref[s, h]
                           + upd[s * M_HEAD_DIM:(s + 1) * M_HEAD_DIM])

    @pl.when(sub == pl.num_programs(1) - 1)
    def _():
        _sample_heads(proj_ref, m0_ref, n0_ref, vec_ref, mixed_ref,
                      q_scr, k_scr, v_scr, o_scr, xc_scr, intert_scr, b_scr, li_scr, valid, R)


def _sample_heads(proj_ref, m0_ref, n0_ref, vec_ref, mixed_ref,
                  q_scr, k_scr, v_scr, o_scr, xc_scr, intert_scr, b_scr, li_scr, valid, R):
    rowi = lax.broadcasted_iota(jnp.int32, (R, R), 0)
    coli = lax.broadcasted_iota(jnp.int32, (R, R), 1)
    same_causal = jnp.logical_and(coli <= rowi, (coli >> SUBLANE_SHIFT) == (rowi >> SUBLANE_SHIFT))
    k_scale = M_HEAD_DIM ** -0.5
    mnw = _vec(vec_ref, "mnw")
    mskip = _vec(vec_ref, "mskip")
    n_rows = _seg_rows(n0_ref)
    b_all = b_scr[...]
    log_i = li_scr[...]
    log_inter = b_all + _seg_rows(m0_ref)
    m_t_all = jnp.maximum(log_inter, b_all + _tile_cummax(log_i - b_all))
    w_inter_all = jnp.exp(log_inter - m_t_all)
    floor_all = jnp.exp(-m_t_all)
    b_t = b_all.T
    li_t = log_i.T

    for h in range(M_HEADS):
        sl = slice(h * M_HEAD_DIM, (h + 1) * M_HEAD_DIM)
        q = q_scr[:, sl]
        ks = k_scr[:, sl] * k_scale
        v = v_scr[:, sl]
        q_b = q.astype(BF16)
        ks_b = ks.astype(BF16)
        w_inter = w_inter_all[:, h:h + 1]

        log_d = jnp.where(same_causal, b_all[:, h:h + 1] - b_t[h:h + 1, :] + li_t[h:h + 1, :], -jnp.inf)
        w_intra = jnp.exp(log_d - m_t_all[:, h:h + 1])
        s = lax.dot_general(q_b, ks_b, _NT, preferred_element_type=F32) * w_intra
        num = (jnp.dot(s.astype(BF16), v.astype(BF16), preferred_element_type=F32)
               + w_inter * intert_scr[h].T)
        den = (jnp.sum(s, axis=1, keepdims=True)
               + w_inter * jnp.sum(q * n_rows[:, sl], axis=1, keepdims=True))
        hh = num / jnp.maximum(jnp.abs(den), floor_all[:, h:h + 1])

        z = proj_ref[:, MIX_WIDTH + h * M_HEAD_DIM:MIX_WIDTH + (h + 1) * M_HEAD_DIM]
        out = _head_output(hh, o_scr[:, sl], xc_scr[:, sl], z, mnw[:, sl], mskip[:, sl])
        mixed_ref[:, sl] = jnp.where(valid, out, 0.0).astype(BF16)


def _sample_mixer(proj, l, row0, mixed_prev, c_all, row_state, vec, mats, prev_out):
    R = SAMPLE_ROWS
    nrows = proj.shape[0]
    nsub = (R // SUBLANES) // SAMPLE_C_SEQS
    blk0 = row0 // R
    c_spec = pl.BlockSpec((None, SAMPLE_C_SEQS, M_HEADS, M_HEAD_DIM, M_HEAD_DIM),
                          lambda i, j: (l, i * nsub + j, 0, 0, 0))

    def state_spec(width, rows_per_seq):
        return pl.BlockSpec((None, R // SUBLANES, rows_per_seq, width), lambda i, j: (l, i, 0, 0))

    def state_shape(width, rows_per_seq):
        return (DEPTH, nrows // SUBLANES, rows_per_seq, width)

    state_specs = [state_spec(w, r) for w, r in zip(_SAMPLE_STATE_WIDTHS, _SAMPLE_STATE_ROWS)]
    alias_args = [mixed_prev] + (list(prev_out) if prev_out is not None else [])
    first_alias = 2 + len(row_state) + 1 + len(mats)
    aliases = {first_alias + k: k for k in range(len(alias_args))}
    out_shape = ([jax.ShapeDtypeStruct(mixed_prev.shape, BF16), jax.ShapeDtypeStruct(c_all.shape, F32)]
                 + [jax.ShapeDtypeStruct(state_shape(w, r), F32)
                    for w, r in zip(_SAMPLE_STATE_WIDTHS, _SAMPLE_STATE_ROWS)])
    scratch = ([pltpu.VMEM((R, M_WIDTH), F32)] * 5
               + [pltpu.VMEM((M_HEADS, M_HEAD_DIM, R), F32), pltpu.VMEM((M_HEADS, M_HEAD_DIM, R), F32),
                  pltpu.VMEM((M_HEADS, M_HEAD_DIM, R), BF16), pltpu.VMEM((M_HEADS, R, M_HEAD_DIM), BF16)]
               + [pltpu.VMEM((R, LANES), F32)] * 3
               + [pltpu.VMEM((R, S_LANES), F32)] * 2)
    return pl.pallas_call(
        functools.partial(_sample_mixer_kernel, rows=R, c_seqs=SAMPLE_C_SEQS, n_alias=len(alias_args)),
        grid=(nrows // R, nsub),
        in_specs=([pl.BlockSpec((R, 2 * MIX_WIDTH), lambda i, j: (i, 0)), c_spec] + state_specs
                  + [_layer_spec(vec, l)] + [_layer_spec(w, l) for w in mats] + [_ANY_SPEC] * len(alias_args)),
        out_specs=[pl.BlockSpec((R, MIX_WIDTH), lambda i, j: (blk0 + i, 0)), c_spec] + state_specs,
        out_shape=out_shape,
        scratch_shapes=scratch,
        input_output_aliases=aliases,
        compiler_params=pltpu.CompilerParams(
            dimension_semantics=("parallel", "arbitrary"), vmem_limit_bytes=VMEM_LIMIT_BYTES),
        name="sample_mixer",
    )(proj, c_all, *row_state, vec, *mats, *alias_args)


def _prepare_weights(p, s5):
    abre, abim, wbu, wcre, wcim = s5

    def split_gates(g):
        pad = [(0, 0)] * (g.ndim - 1) + [(0, LANES - M_HEADS)]
        return jnp.concatenate([jnp.pad(g[..., :M_HEADS], pad), jnp.pad(g[..., M_HEADS:], pad)], axis=-1)

    table = {"mcw": p["m_conv_w"], "rcw": p["r_conv_w"], "mcb": p["m_conv_b"], "mnw": p["m_norm_w"],
             "mskip": p["m_skip"], "rcb": p["r_conv_b"], "rba": p["r_ba"], "rbx": p["r_bx"], "rlam": p["r_lam"],
             "sd": p["s_d"], "bglu": p["s_b_glu"], "bif": split_gates(p["m_b_if"]), "abre": abre, "abim": abim,
             "norm_w": p["norm_w"]}

    def group_rows(group):
        parts = [table[name].reshape(DEPTH, n, w) for name, n, w in group]
        fill = _VEC_WIDTH - sum(w for _, _, w in group)
        if fill:
            parts.append(jnp.zeros((DEPTH, group[0][1], fill), F32))
        return parts[0] if len(parts) == 1 else jnp.concatenate(parts, axis=-1)

    vec = jnp.concatenate([group_rows(g) for g in _VEC_GROUPS]
                          + [jnp.zeros((DEPTH, _VEC_ROWS - _VEC_USED_ROWS, _VEC_WIDTH), F32)], axis=1)

    def block_diag_halves(w):
        nb = R_BLOCKS // 2
        w5 = w.reshape(DEPTH, 2, nb, w.shape[-2], w.shape[-1])
        eye = jnp.eye(nb, dtype=F32)[:, None, :, None]
        return (w5[:, :, :, :, None, :] * eye).reshape(
            DEPTH, 2, nb * w.shape[-2], nb * w.shape[-1]).astype(BF16)

    mats = {
        "wqk": jnp.concatenate([p["m_wq"], p["m_wk"]], axis=-1).astype(BF16),
        "wvo": jnp.concatenate([p["m_wv"], p["m_wo"]], axis=-1).astype(BF16),
        "wif": split_gates(p["m_w_if"]).astype(BF16),
        "rwa": block_diag_halves(p["r_wa"]), "rwx": block_diag_halves(p["r_wx"]),
        "wbu": wbu, "wcre": wcre, "wcim": wcim,
        "wglu": p["s_w_glu"].astype(BF16),
    }
    return vec, [mats[k] for k in _MATRIX_KEYS]


def _sample_rows_state(n, m, mconv, h, rconv, sre, sim):
    bsz = n.shape[1]

    def per_seq(x):
        return x.reshape(DEPTH, bsz, 1, x.shape[-1])

    m_pad = jnp.pad(m, ((0, 0), (0, 0), (0, LANES - M_HEADS)))
    return (per_seq(n.reshape(DEPTH, bsz, M_WIDTH)), per_seq(m_pad), mconv, per_seq(h), rconv,
            per_seq(sre.reshape(DEPTH, bsz, S_LANES)), per_seq(sim.reshape(DEPTH, bsz, S_LANES)))


def _sample_state_from_rows(n, m, mconv, h, rconv, sre, sim):
    bsz = n.shape[1]
    return (n.reshape(DEPTH, bsz, M_HEADS, M_HEAD_DIM), m.reshape(DEPTH, bsz, LANES)[..., :M_HEADS],
            mconv, h.reshape(DEPTH, bsz, R_WIDTH), rconv,
            sre.reshape(DEPTH, bsz, S_GROUPS, S_STATE), sim.reshape(DEPTH, bsz, S_GROUPS, S_STATE))


def _prompt_state_from_kernel(c, n, m, mtail, h, rtail, sre, sim):
    bsz = c.shape[1]
    return (c, n, m[:, :, 0, :M_HEADS], mtail[:, :, SUBLANES - 1::SUBLANES], h.reshape(DEPTH, bsz, R_WIDTH),
            rtail[:, :, SUBLANES - 1::SUBLANES],
            sre.reshape(DEPTH, bsz, S_GROUPS, S_STATE), sim.reshape(DEPTH, bsz, S_GROUPS, S_STATE))


def _interleave_chunks(x, inverse=False):
    bsz, seq, d = x.shape
    sub_len = PROMPT_CHUNK // SUBLANES
    inner = (sub_len, SUBLANES) if inverse else (SUBLANES, sub_len)
    return x.reshape(bsz, seq // PROMPT_CHUNK, *inner, d).transpose(0, 1, 3, 2, 4).reshape(bsz, seq, d)


def kernel(x_prompt, x_sample, state_mlstm_C, state_mlstm_n, state_mlstm_m, state_mlstm_conv, state_rglru_h, state_rglru_conv, state_s5_re, state_s5_im, norm_w, w_in, w_out, m_conv_w, m_conv_b, m_wq, m_wk, m_wv, m_wo, m_w_if, m_b_if, m_norm_w, m_skip, r_conv_w, r_conv_b, r_wa, r_ba, r_wx, r_bx, r_lam, s_lam_re, s_lam_im, s_b_re, s_b_im, s_c_re, s_c_im, s_d, s_log_step, s_w_glu, s_b_glu, final_norm_w):
    p = dict(norm_w=norm_w, w_in=w_in, w_out=w_out, m_conv_w=m_conv_w, m_conv_b=m_conv_b, m_wq=m_wq, m_wk=m_wk,
             m_wv=m_wv, m_wo=m_wo, m_w_if=m_w_if, m_b_if=m_b_if, m_norm_w=m_norm_w, m_skip=m_skip,
             r_conv_w=r_conv_w, r_conv_b=r_conv_b, r_wa=r_wa, r_ba=r_ba, r_wx=r_wx, r_bx=r_bx, r_lam=r_lam,
             s_d=s_d, s_w_glu=s_w_glu, s_b_glu=s_b_glu)
    s5 = _s5_prep(s_lam_re, s_lam_im, s_log_step, s_b_re, s_b_im, s_c_re, s_c_im)
    vec, mats = _prepare_weights(p, s5)
    final_w = final_norm_w.reshape(1, D_MODEL)

    bsz, seq, _ = x_prompt.shape
    dec_batch, dec_seq, _ = x_sample.shape
    n_prompt = bsz * seq
    n_sample = dec_batch * SUBLANES
    stream = [_interleave_chunks(x_prompt).reshape(n_prompt, D_MODEL),
              jnp.pad(x_sample, ((0, 0), (SAMPLE_LEAD, 0), (0, 0))).reshape(n_sample, D_MODEL)]
    rows_state = _sample_rows_state(state_mlstm_n, state_mlstm_m, state_mlstm_conv, state_rglru_h,
                                    state_rglru_conv, state_s5_re, state_s5_im)
    pr_states = None
    sa_states = None
    for l in range(DEPTH):
        last = l == DEPTH - 1
        proj_s, w_in_b = _inproj(stream[-1], stream[-1].shape[0] - n_sample, n_sample, vec, w_in, l)
        mixed, *pr_states = _prompt_mixer(stream[0], n_prompt + n_sample, l, bsz, seq, vec, w_in_b, mats, pr_states)
        mixed, *sa_states = _sample_mixer(proj_s, l, n_prompt, mixed, state_mlstm_C, rows_state, vec, mats, sa_states)
        stream = _outproj(mixed, stream, w_out, l, final_w, last, split_rows=n_prompt if last else None)
    y_prompt = _interleave_chunks(stream[0].reshape(bsz, seq, D_MODEL), inverse=True)
    y_sample = stream[1]
    return (y_prompt, y_sample, *_prompt_state_from_kernel(*pr_states),
            sa_states[0], *_sample_state_from_rows(*sa_states[1:]))
```

```python
import functools

import jax
import jax.numpy as jnp
from jax import lax
from jax.experimental import pallas as pl
from jax.experimental.pallas import tpu as pltpu

F32 = jnp.float32
BF16 = jnp.bfloat16

D_MODEL = 2048
DEPTH = 2
MIX_WIDTH = D_MODEL
M_WIDTH = MIX_WIDTH // 2
R_WIDTH = MIX_WIDTH // 4
S_WIDTH = MIX_WIDTH - M_WIDTH - R_WIDTH
M_HEADS = 8
M_HEAD_DIM = M_WIDTH // M_HEADS
R_BLOCKS = 8
RG_C = 8.0
S_GROUP = 16
S_GROUPS = S_WIDTH // S_GROUP
S_STATE = 64
S_LANES = S_GROUPS * S_STATE
CONV_W = 4
EPS = 1e-6

SUBLANES = 8
SUBLANE_SHIFT = SUBLANES.bit_length() - 1
LANES = 128
VMEM_LIMIT_BYTES = 56 * 1024 * 1024

PROMPT_CHUNK = 256
SLAB_LEAD = 2
SAMPLE_ROWS = 128
SAMPLE_C_SEQS = 16
SAMPLE_LEAD = SUBLANES - 4
INPROJ_TN = 1024
OUTPROJ_TM = 512
SCAN_LANE_BLOCK = 256

_NT = (((1,), (1,)), ((), ()))

_VEC_WIDTH = D_MODEL
_VEC_GROUPS = ((("mcw", CONV_W, M_WIDTH), ("rcw", CONV_W, R_WIDTH)),
               (("mcb", 1, M_WIDTH), ("mnw", 1, M_WIDTH)),
               (("mskip", 1, M_WIDTH), ("rcb", 1, R_WIDTH), ("rba", 1, R_WIDTH)),
               (("rbx", 1, R_WIDTH), ("rlam", 1, R_WIDTH), ("sd", 1, S_WIDTH), ("bglu", 1, S_WIDTH)),
               (("bif", 1, 2 * LANES),),
               (("abre", 1, S_LANES),), (("abim", 1, S_LANES),), (("norm_w", 1, D_MODEL),))
_VEC_LAYOUT = {}
_row = 0
for _group in _VEC_GROUPS:
    _lane = 0
    for _name, _n, _w in _group:
        _VEC_LAYOUT[_name] = (_row, _n, _lane, _w)
        _lane += _w
    _row += _group[0][1]
_VEC_USED_ROWS = _row
_VEC_ROWS = -(-_row // SUBLANES) * SUBLANES


def _vec(vec_ref, name):
    r0, n, l0, w = _VEC_LAYOUT[name]
    return vec_ref[r0:r0 + n, l0:l0 + w]


def _layer_spec(arr, l):
    nd = arr.ndim - 1
    return pl.BlockSpec((None,) + arr.shape[1:], lambda *_: (l,) + (0,) * nd, pipeline_mode=pl.Buffered(1))


_ANY_SPEC = pl.BlockSpec(memory_space=pl.ANY)


S5_IN_SLICES = S_WIDTH // LANES
S5_IN_GROUPS = LANES // S_GROUP
S5_OUT_HALVES = 2
S5_OUT_GROUPS = S_GROUPS // S5_OUT_HALVES


def _s5_prep_kernel(lr_ref, li_ref, ls_ref, brt_ref, bit_ref, cre_ref, cim_ref,
                    abre_ref, abim_ref, wbu_ref, wcre_ref, wcim_ref):
    lr = lr_ref[0]
    li = li_ref[0]
    dt = jnp.exp(ls_ref[0])
    mag = jnp.exp(lr * dt)
    ang = li * dt
    ab_re = mag * jnp.cos(ang)
    ab_im = mag * jnp.sin(ang)
    den = lr * lr + li * li
    nr = ab_re - 1.0
    f_re = (nr * lr + ab_im * li) / den
    f_im = (ab_im * lr - nr * li) / den
    br = brt_ref[0]
    bi = bit_ref[0]
    abre_ref[0] = ab_re
    abim_ref[0] = ab_im
    bb_re = f_re * br - f_im * bi
    bb_im = f_re * bi + f_im * br

    per_slice = S5_IN_GROUPS * S_STATE
    lane_group = lax.broadcasted_iota(jnp.int32, (S_GROUP, per_slice), 1) >> (S_STATE.bit_length() - 1)
    for k in range(S5_IN_SLICES):
        re_k = bb_re[:, k * per_slice:(k + 1) * per_slice]
        im_k = bb_im[:, k * per_slice:(k + 1) * per_slice]
        for a in range(S5_IN_GROUPS):
            rows = slice(a * S_GROUP, (a + 1) * S_GROUP)
            wbu_ref[0, k, rows, 0:per_slice] = jnp.where(lane_group == a, re_k, 0.0).astype(BF16)
            wbu_ref[0, k, rows, per_slice:2 * per_slice] = jnp.where(lane_group == a, im_k, 0.0).astype(BF16)

    row_group = lax.broadcasted_iota(jnp.int32, (S5_OUT_GROUPS * S_GROUP, S_STATE), 0) >> (S_GROUP.bit_length() - 1)
    for src_ref, dst_ref in ((cre_ref, wcre_ref), (cim_ref, wcim_ref)):
        for m in range(S5_OUT_HALVES):
            c = src_ref[0, m]
            for a in range(S5_OUT_GROUPS):
                dst_ref[0, m, :, a * S_STATE:(a + 1) * S_STATE] = jnp.where(row_group == a, c, 0.0).astype(BF16)


def _s5_prep(s_lam_re, s_lam_im, s_log_step, s_b_re, s_b_im, s_c_re, s_c_im):
    lr = s_lam_re.reshape(DEPTH, 1, S_LANES)
    li = s_lam_im.reshape(DEPTH, 1, S_LANES)
    ls = jnp.repeat(s_log_step, S_STATE, axis=-1).reshape(DEPTH, 1, S_LANES)
    brt = s_b_re.reshape(DEPTH, S_LANES, S_GROUP).transpose(0, 2, 1)
    bit = s_b_im.reshape(DEPTH, S_LANES, S_GROUP).transpose(0, 2, 1)
    c_shape = (DEPTH, S5_OUT_HALVES, S5_OUT_GROUPS * S_GROUP, S_STATE)
    wbu_shape = (DEPTH, S5_IN_SLICES, LANES, 2 * S5_IN_GROUPS * S_STATE)
    wc_shape = (DEPTH, S5_OUT_HALVES, S5_OUT_GROUPS * S_GROUP, S5_OUT_GROUPS * S_STATE)

    def layer_block(shape):
        nd = len(shape) - 1
        return pl.BlockSpec((1,) + shape[1:], lambda l: (l,) + (0,) * nd)

    vec = layer_block((DEPTH, 1, S_LANES))
    mat = layer_block((DEPTH, S_GROUP, S_LANES))
    return pl.pallas_call(
        _s5_prep_kernel,
        grid=(DEPTH,),
        in_specs=[vec, vec, vec, mat, mat, layer_block(c_shape), layer_block(c_shape)],
        out_specs=[vec, vec, layer_block(wbu_shape), layer_block(wc_shape), layer_block(wc_shape)],
        out_shape=[jax.ShapeDtypeStruct((DEPTH, 1, S_LANES), F32),
                   jax.ShapeDtypeStruct((DEPTH, 1, S_LANES), F32),
                   jax.ShapeDtypeStruct(wbu_shape, BF16),
                   jax.ShapeDtypeStruct(wc_shape, BF16),
                   jax.ShapeDtypeStruct(wc_shape, BF16)],
        name="s5_prep",
    )(lr, li, ls, brt, bit, s_c_re.reshape(c_shape), s_c_im.reshape(c_shape))


def _stream_specs(parts, tm):
    if len(parts) == 1:
        return [pl.BlockSpec((tm, D_MODEL), lambda i: (i, 0))]
    nb0 = parts[0].shape[0] // tm
    return [pl.BlockSpec((tm, D_MODEL), lambda i: (jnp.minimum(i, nb0 - 1), 0)),
            pl.BlockSpec((tm, D_MODEL), lambda i: (jnp.maximum(i - nb0, 0), 0))]


def _read_stream(refs, nb0):
    if len(refs) == 1:
        return refs[0][...]
    return jnp.where(pl.program_id(0) < nb0, refs[0][...], refs[1][...])


def _rmsnorm_bf16(x, vec_ref):
    ms = jnp.mean(x * x, axis=-1, keepdims=True)
    return (x * lax.rsqrt(ms + EPS) * _vec(vec_ref, "norm_w")).astype(BF16)


def _inproj_kernel(x_ref, vec_ref, w_ref, o_ref, wb_ref, xn_scr):
    @pl.when(pl.program_id(0) == 0)
    def _():
        xn_scr[...] = _rmsnorm_bf16(x_ref[...], vec_ref)

    wb = w_ref[...].astype(BF16)
    wb_ref[...] = wb
    o_ref[...] = jnp.dot(xn_scr[...], wb, preferred_element_type=F32)


def _inproj(x, row0, nrows, vec, w_in, l):
    blk0 = row0 // nrows
    return pl.pallas_call(
        _inproj_kernel,
        grid=((2 * MIX_WIDTH) // INPROJ_TN,),
        in_specs=[pl.BlockSpec((nrows, D_MODEL), lambda j: (blk0, 0), pipeline_mode=pl.Buffered(1)),
                  _layer_spec(vec, l),
                  pl.BlockSpec((None, D_MODEL, INPROJ_TN), lambda j: (l, 0, j))],
        out_specs=[pl.BlockSpec((nrows, INPROJ_TN), lambda j: (0, j)),
                   pl.BlockSpec((D_MODEL, INPROJ_TN), lambda j: (0, j))],
        out_shape=[jax.ShapeDtypeStruct((nrows, 2 * MIX_WIDTH), F32),
                   jax.ShapeDtypeStruct((D_MODEL, 2 * MIX_WIDTH), BF16)],
        scratch_shapes=[pltpu.VMEM((nrows, D_MODEL), BF16)],
        compiler_params=pltpu.CompilerParams(
            dimension_semantics=("arbitrary",), vmem_limit_bytes=VMEM_LIMIT_BYTES),
        name="inproj",
    )(x, vec, w_in)


def _outproj_kernel(*refs, n_x, nb0, n_out, nb0_out, final):
    mixed_ref, x_refs = refs[0], refs[1:1 + n_x]
    w_ref, fw_ref = refs[1 + n_x:3 + n_x]
    o_refs, wb_scr = refs[3 + n_x:-1], refs[-1]

    @pl.when(pl.program_id(0) == 0)
    def _():
        wb_scr[...] = w_ref[...].astype(BF16)

    y = _read_stream(x_refs, nb0) + jnp.dot(mixed_ref[...], wb_scr[...], preferred_element_type=F32)
    if final:
        ms = jnp.mean(y * y, axis=-1, keepdims=True)
        y = y * lax.rsqrt(ms + EPS) * fw_ref[...]
    if n_out == 1:
        o_refs[0][...] = y
    else:
        @pl.when(pl.program_id(0) < nb0_out)
        def _():
            o_refs[0][...] = y

        @pl.when(pl.program_id(0) >= nb0_out)
        def _():
            o_refs[1][...] = y.reshape(y.shape[0] // SUBLANES, SUBLANES, y.shape[1])[:, SAMPLE_LEAD:, :]


def _outproj(mixed, x_parts, w_out, l, final_w, final, split_rows=None):
    n = mixed.shape[0]
    if split_rows is None:
        out_parts = [jax.ShapeDtypeStruct((n, D_MODEL), F32)]
        out_specs = [pl.BlockSpec((OUTPROJ_TM, D_MODEL), lambda i: (i, 0))]
        nb0_out = n // OUTPROJ_TM
    else:
        nb0_out = split_rows // OUTPROJ_TM
        tokens = SUBLANES - SAMPLE_LEAD
        out_parts = [jax.ShapeDtypeStruct((split_rows, D_MODEL), F32),
                     jax.ShapeDtypeStruct(((n - split_rows) // SUBLANES, tokens, D_MODEL), F32)]
        out_specs = [pl.BlockSpec((OUTPROJ_TM, D_MODEL), lambda i: (jnp.minimum(i, nb0_out - 1), 0)),
                     pl.BlockSpec((OUTPROJ_TM // SUBLANES, tokens, D_MODEL),
                                  lambda i: (jnp.maximum(i - nb0_out, 0), 0, 0))]
    return pl.pallas_call(
        functools.partial(_outproj_kernel, n_x=len(x_parts), nb0=x_parts[0].shape[0] // OUTPROJ_TM,
                          n_out=len(out_parts), nb0_out=nb0_out, final=final),
        grid=(n // OUTPROJ_TM,),
        in_specs=([pl.BlockSpec((OUTPROJ_TM, MIX_WIDTH), lambda i: (i, 0))] + _stream_specs(x_parts, OUTPROJ_TM)
                  + [_layer_spec(w_out, l), pl.BlockSpec((1, D_MODEL), lambda i: (0, 0))]),
        out_specs=out_specs,
        out_shape=out_parts,
        scratch_shapes=[pltpu.VMEM((MIX_WIDTH, D_MODEL), BF16)],
        compiler_params=pltpu.CompilerParams(
            dimension_semantics=("arbitrary",), vmem_limit_bytes=VMEM_LIMIT_BYTES),
        name="outproj",
    )(mixed, *x_parts, w_out, final_w)


def _silu(x):
    return x * jax.nn.sigmoid(x)


def _log_sigmoid(x):
    return jnp.minimum(x, 0.0) - jnp.log1p(jnp.exp(-jnp.abs(x)))


def _softplus(x):
    return jnp.maximum(x, 0.0) + jnp.log1p(jnp.exp(-jnp.abs(x)))


def _sublane_pos(shape):
    return lax.broadcasted_iota(jnp.int32, shape, 0) & (SUBLANES - 1)


def _tile_cumsum(x):
    pos = _sublane_pos(x.shape)
    s = 1
    while s < SUBLANES:
        x = x + jnp.where(pos >= s, pltpu.roll(x, s, 0), 0.0)
        s *= 2
    return x


def _tile_cummax(x):
    pos = _sublane_pos(x.shape)
    s = 1
    while s < SUBLANES:
        x = jnp.maximum(x, jnp.where(pos >= s, pltpu.roll(x, s, 0), -jnp.inf))
        s *= 2
    return x


def _tile_scan_real(a, b):
    pos = _sublane_pos(a.shape)
    s = 1
    while s < SUBLANES:
        m = pos >= s
        b = jnp.where(m, a * pltpu.roll(b, s, 0) + b, b)
        if 2 * s < SUBLANES:
            a = jnp.where(m, a * pltpu.roll(a, s, 0), a)
        s *= 2
    return b


def _tile_scan_cplx(sr, si, pr, pi):
    pos = _sublane_pos(sr.shape)
    s = 1
    while s < SUBLANES:
        m = pos >= s
        sr_sh = pltpu.roll(sr, s, 0)
        si_sh = pltpu.roll(si, s, 0)
        sr, si = (jnp.where(m, sr + (pr * sr_sh - pi * si_sh), sr),
                  jnp.where(m, si + (pr * si_sh + pi * sr_sh), si))
        if 2 * s < SUBLANES:
            pr, pi = pr * pr - pi * pi, 2.0 * (pr * pi)
        s *= 2
    return sr, si


def _mlstm_project(xc, xm, vec_ref, wqk_ref, wvo_ref, wif_ref, q_scr, k_scr, v_scr, o_scr):
    xc_b = xc.astype(BF16)
    xm_b = xm.astype(BF16)
    for h in range(M_HEADS):
        sl = slice(h * M_HEAD_DIM, (h + 1) * M_HEAD_DIM)
        qk = jnp.dot(xc_b[:, sl], wqk_ref[h], preferred_element_type=F32)
        vo = jnp.dot(xm_b[:, sl], wvo_ref[h], preferred_element_type=F32)
        q_scr[:, sl] = qk[:, :M_HEAD_DIM]
        k_scr[:, sl] = qk[:, M_HEAD_DIM:]
        v_scr[:, sl] = vo[:, :M_HEAD_DIM]
        o_scr[:, sl] = vo[:, M_HEAD_DIM:]
    return (jnp.dot(q_scr[...].astype(BF16), wif_ref[0:M_WIDTH, :], preferred_element_type=F32)
            + jnp.dot(k_scr[...].astype(BF16), wif_ref[M_WIDTH:2 * M_WIDTH, :], preferred_element_type=F32)
            + jnp.dot(v_scr[...].astype(BF16), wif_ref[2 * M_WIDTH:3 * M_WIDTH, :], preferred_element_type=F32)
            + _vec(vec_ref, "bif"))


def _head_output(hh, o_pre, xc_h, z_h, mnw_h, mskip_h):
    mu = jnp.mean(hh, axis=1, keepdims=True)
    hc = hh - mu
    var = jnp.mean(hc * hc, axis=1, keepdims=True)
    hn = hc * lax.rsqrt(var + EPS) * mnw_h
    return (jax.nn.sigmoid(o_pre) * hn + mskip_h * xc_h) * _silu(z_h)


def _rglru_coeffs(xcr, vec_ref, rwa_ref, rwx_ref):
    xcr_b = xcr.astype(BF16)
    half = R_WIDTH // 2
    ra_pre = jnp.concatenate(
        [jnp.dot(xcr_b[:, :half], rwa_ref[0], preferred_element_type=F32),
         jnp.dot(xcr_b[:, half:], rwa_ref[1], preferred_element_type=F32)], axis=1) + _vec(vec_ref, "rba")
    rx_pre = jnp.concatenate(
        [jnp.dot(xcr_b[:, :half], rwx_ref[0], preferred_element_type=F32),
         jnp.dot(xcr_b[:, half:], rwx_ref[1], preferred_element_type=F32)], axis=1) + _vec(vec_ref, "rbx")
    log_a = (-RG_C) * jax.nn.sigmoid(ra_pre) * _softplus(-_vec(vec_ref, "rlam"))
    a = jnp.exp(log_a)
    th = jnp.tanh(log_a)
    one_minus_a2 = (-2.0 * th) / (1.0 - th)
    return a, jnp.sqrt(one_minus_a2) * (jax.nn.sigmoid(rx_pre) * xcr)


def _s5_project(u_b, wbu_ref, re_ref, im_ref):
    per_slice = (LANES // S_GROUP) * S_STATE
    for k in range(S_WIDTH // LANES):
        res = jnp.dot(u_b[:, k * LANES:(k + 1) * LANES], wbu_ref[k], preferred_element_type=F32)
        re_ref[:, k * per_slice:(k + 1) * per_slice] = res[:, :per_slice]
        im_ref[:, k * per_slice:(k + 1) * per_slice] = res[:, per_slice:]


def _s5_output(re_ref, im_ref, u, vec_ref, wcre_ref, wcim_ref, wglu_ref):
    nblk = wcre_ref.shape[0]
    k_blk = S_LANES // nblk
    parts = []
    for m in range(nblk):
        ks = slice(m * k_blk, (m + 1) * k_blk)
        parts.append(lax.dot_general(re_ref[:, ks].astype(BF16), wcre_ref[m], _NT, preferred_element_type=F32)
                     - lax.dot_general(im_ref[:, ks].astype(BF16), wcim_ref[m], _NT, preferred_element_type=F32))
    y = jnp.concatenate(parts, axis=1) + _vec(vec_ref, "sd") * u
    g = jax.nn.gelu(y)
    return g * jax.nn.sigmoid(jnp.dot(g.astype(BF16), wglu_ref[...], preferred_element_type=F32)
                              + _vec(vec_ref, "bglu"))


_MATRIX_KEYS = ("wqk", "wvo", "wif", "rwa", "rwx", "wbu", "wcre", "wcim", "wglu")


def _tile(x, i):
    return x[i * SUBLANES:(i + 1) * SUBLANES]


def _conv_interleaved(x, tail_ref, w, bias, rows):
    ntiles = rows // SUBLANES
    pos = lax.broadcasted_iota(jnp.int32, (SUBLANES, x.shape[1]), 0)
    prev = tail_ref[...]
    before = [pltpu.roll(jnp.where(pos == SUBLANES - 1, _tile(prev, CONV_W - 1 - d), _tile(x, ntiles - d)), 1, 0)
              for d in range(1, CONV_W)]
    out = w[CONV_W - 1:CONV_W, :] * x + bias
    for j in range(1, CONV_W):
        shifted = jnp.concatenate(before[:j][::-1] + [x[:rows - j * SUBLANES]], axis=0)
        out = out + w[CONV_W - 1 - j:CONV_W - j, :] * shifted
    tail_ref[...] = x[rows - (CONV_W - 1) * SUBLANES:]
    return out


def _cumsum_interleaved(x, rows):
    tiles = [_tile(x, 0)]
    for i in range(1, rows // SUBLANES):
        tiles.append(tiles[-1] + _tile(x, i))
    total = tiles[-1]
    start = _tile_cumsum(total) - total
    return jnp.concatenate([t + start for t in tiles], axis=0)


def _cummax_interleaved(x, rows):
    tiles = [_tile(x, 0)]
    for i in range(1, rows // SUBLANES):
        tiles.append(jnp.maximum(tiles[-1], _tile(x, i)))
    best = tiles[-1]
    pos = lax.broadcasted_iota(jnp.int32, best.shape, 0)
    s = 1
    while s < SUBLANES:
        best = jnp.maximum(best, jnp.where(pos >= s, pltpu.roll(best, s, 0), -jnp.inf))
        s *= 2
    start = jnp.where(pos == 0, -jnp.inf, pltpu.roll(best, 1, 0))
    return jnp.concatenate([jnp.maximum(t, start) for t in tiles], axis=0)


def _scan_real_interleaved(a_ref, b_ref, carry, rows):
    ntiles = rows // SUBLANES
    pos = lax.broadcasted_iota(jnp.int32, (SUBLANES, a_ref.shape[1]), 0)
    h = _tile(b_ref, 0)
    aprod = _tile(a_ref, 0)
    for i in range(1, ntiles):
        a = _tile(a_ref, i)
        h = a * h + _tile(b_ref, i)
        aprod = aprod * a
    g = _tile_scan_real(aprod, h + jnp.where(pos == 0, aprod * carry, 0.0))
    h = jnp.where(pos == 0, carry, pltpu.roll(g, 1, 0))
    for i in range(ntiles):
        h = _tile(a_ref, i) * h + _tile(b_ref, i)
        b_ref[i * SUBLANES:(i + 1) * SUBLANES, :] = h
    return g[SUBLANES - 1:SUBLANES]


def _scan_cplx_interleaved(re_ref, im_ref, p_re, p_im, cre_ref, cim_ref, rows):
    ntiles = rows // SUBLANES
    width = re_ref.shape[1]
    pos = lax.broadcasted_iota(jnp.int32, (SUBLANES, SCAN_LANE_BLOCK), 0)
    for blk in range(width // SCAN_LANE_BLOCK):
        sl = slice(blk * SCAN_LANE_BLOCK, (blk + 1) * SCAN_LANE_BLOCK)
        pr = jnp.broadcast_to(p_re[:, sl], pos.shape)
        pi = jnp.broadcast_to(p_im[:, sl], pos.shape)
        c_r = cre_ref[:, sl]
        c_i = cim_ref[:, sl]
        sr = re_ref[0:SUBLANES, sl]
        si = im_ref[0:SUBLANES, sl]
        for i in range(1, ntiles):
            rs = slice(i * SUBLANES, (i + 1) * SUBLANES)
            sr, si = pr * sr - pi * si + re_ref[rs, sl], pr * si + pi * sr + im_ref[rs, sl]
        qr, qi = pr, pi
        n = 1
        while n < ntiles:
            qr, qi = qr * qr - qi * qi, 2.0 * (qr * qi)
            n *= 2
        gr, gi = _tile_scan_cplx(sr + jnp.where(pos == 0, qr * c_r - qi * c_i, 0.0),
                                 si + jnp.where(pos == 0, qr * c_i + qi * c_r, 0.0), qr, qi)
        sr = jnp.where(pos == 0, c_r, pltpu.roll(gr, 1, 0))
        si = jnp.where(pos == 0, c_i, pltpu.roll(gi, 1, 0))
        for i in range(ntiles):
            rs = slice(i * SUBLANES, (i + 1) * SUBLANES)
            sr, si = pr * sr - pi * si + re_ref[rs, sl], pr * si + pi * sr + im_ref[rs, sl]
            re_ref[rs, sl] = sr
            im_ref[rs, sl] = si
        cre_ref[:, sl] = gr[SUBLANES - 1:SUBLANES]
        cim_ref[:, sl] = gi[SUBLANES - 1:SUBLANES]


def _prompt_mixer_kernel(*refs, rows, nchunk, n_alias):
    (xnext_ref, xfirst_ref, vec_ref, win_ref,
     wqk_ref, wvo_ref, wif_ref, rwa_ref, rwx_ref, wbu_ref, wcre_ref, wcim_ref, wglu_ref) = refs[:13]
    (mixed_ref, c_ref, n_ref, m_ref, mtail_ref, h_ref, rtail_ref, sre_ref, sim_ref,
     q_scr, k_scr, v_scr, o_scr, ra_scr, rb_scr, ure_scr, uim_scr, proj_scr, xn_scr) = refs[13 + n_alias:]
    T = rows
    step = pl.program_id(0)
    slot = lax.rem(step, 2)
    proj_ref = proj_scr.at[slot]
    next_proj_ref = proj_scr.at[1 - slot]
    slab = (2 * MIX_WIDTH) // M_HEADS

    @pl.when(step == 0)
    def _():
        xn = _rmsnorm_bf16(xfirst_ref[...], vec_ref)
        for j in range(M_HEADS):
            cols = slice(j * slab, (j + 1) * slab)
            proj_scr[0, :, cols] = jnp.dot(xn, win_ref[:, cols], preferred_element_type=F32)

    @pl.when(lax.rem(step, nchunk) == 0)
    def _():
        for ref in (c_ref, n_ref, m_ref, mtail_ref, h_ref, rtail_ref, sre_ref, sim_ref):
            ref[...] = jnp.zeros(ref.shape, F32)

    xn_scr[...] = _rmsnorm_bf16(xnext_ref[...], vec_ref)

    xm = proj_ref[:, 0:M_WIDTH]
    xc = _silu(_conv_interleaved(xm, mtail_ref, _vec(vec_ref, "mcw"), _vec(vec_ref, "mcb"), T))
    gates = _mlstm_project(xc, xm, vec_ref, wqk_ref, wvo_ref, wif_ref, q_scr, k_scr, v_scr, o_scr)
    log_i = gates[:, :LANES]
    b_all = _cumsum_interleaved(_log_sigmoid(gates[:, LANES:]), T)
    a_max = _cummax_interleaved(log_i - b_all, T)
    m_prev = m_ref[...]
    log_inter = b_all + m_prev
    m_t_all = jnp.maximum(log_inter, b_all + a_max)
    w_inter_all = jnp.exp(log_inter - m_t_all)
    floor_all = jnp.exp(-m_t_all)
    b_end = b_all[T - 1:T, :]
    m_new = jnp.maximum(b_end + m_prev, b_end + a_max[T - 1:T, :])
    w_src_all = jnp.exp(b_end - b_all + log_i - m_new)
    decay_all = jnp.exp(b_end + m_prev - m_new)
    m_ref[...] = m_new
    b_t = b_all.T
    li_t = log_i.T

    rowi = lax.broadcasted_iota(jnp.int32, (T, T), 0)
    coli = lax.broadcasted_iota(jnp.int32, (T, T), 1)
    sub_len = T // SUBLANES

    def time_of(r):
        return (r & (SUBLANES - 1)) * sub_len + (r >> SUBLANE_SHIFT)

    causal = time_of(coli) <= time_of(rowi)
    k_scale = M_HEAD_DIM ** -0.5
    mnw = _vec(vec_ref, "mnw")
    mskip = _vec(vec_ref, "mskip")

    def project_next(j):
        cols = slice(j * slab, (j + 1) * slab)
        next_proj_ref[:, cols] = jnp.dot(xn_scr[...], win_ref[:, cols], preferred_element_type=F32)

    for j in range(SLAB_LEAD):
        project_next(j)
    for h in range(M_HEADS):
        if h + SLAB_LEAD < M_HEADS:
            project_next(h + SLAB_LEAD)

        sl = slice(h * M_HEAD_DIM, (h + 1) * M_HEAD_DIM)
        c_prev = c_ref[h]
        n_prev = n_ref[h:h + 1, :]
        w_inter = w_inter_all[:, h:h + 1]
        w_src = w_src_all[:, h:h + 1]
        decay = decay_all[:, h:h + 1]

        q = q_scr[:, sl]
        ks = k_scr[:, sl] * k_scale
        v = v_scr[:, sl]
        q_b = q.astype(BF16)
        ks_b = ks.astype(BF16)

        log_d = jnp.where(causal, b_all[:, h:h + 1] - b_t[h:h + 1, :] + li_t[h:h + 1, :], -jnp.inf)
        w_intra = jnp.exp(log_d - m_t_all[:, h:h + 1])
        s = lax.dot_general(q_b, ks_b, _NT, preferred_element_type=F32) * w_intra
        inter = lax.dot_general(q_b, c_prev.astype(BF16), _NT, preferred_element_type=F32)
        num = jnp.dot(s.astype(BF16), v.astype(BF16), preferred_element_type=F32) + w_inter * inter
        den = (jnp.sum(s, axis=1, keepdims=True)
               + w_inter * jnp.sum(q * n_prev, axis=1, keepdims=True))
        hh = num / jnp.maximum(jnp.abs(den), floor_all[:, h:h + 1])

        vw_t = (v * w_src).T.astype(BF16)
        c_ref[h] = decay * c_prev + jnp.dot(vw_t, ks_b, preferred_element_type=F32)
        n_ref[h:h + 1, :] = decay * n_prev + jnp.sum(ks * w_src, axis=0, keepdims=True)

        z = proj_ref[:, MIX_WIDTH + h * M_HEAD_DIM:MIX_WIDTH + (h + 1) * M_HEAD_DIM]
        mixed_ref[:, sl] = _head_output(hh, o_scr[:, sl], xc[:, sl], z, mnw[:, sl], mskip[:, sl]).astype(BF16)

    xr = proj_ref[:, M_WIDTH:M_WIDTH + R_WIDTH]
    xcr = _conv_interleaved(xr, rtail_ref, _vec(vec_ref, "rcw"), _vec(vec_ref, "rcb"), T)
    a, bb = _rglru_coeffs(xcr, vec_ref, rwa_ref, rwx_ref)
    ra_scr[...] = a
    rb_scr[...] = bb
    h_ref[...] = _scan_real_interleaved(ra_scr, rb_scr, h_ref[...], T)
    zr = proj_ref[:, MIX_WIDTH + M_WIDTH:MIX_WIDTH + M_WIDTH + R_WIDTH]
    mixed_ref[:, M_WIDTH:M_WIDTH + R_WIDTH] = (rb_scr[...] * _silu(zr)).astype(BF16)

    u = proj_ref[:, M_WIDTH + R_WIDTH:MIX_WIDTH]
    _s5_project(u.astype(BF16), wbu_ref, ure_scr, uim_scr)
    _scan_cplx_interleaved(ure_scr, uim_scr, _vec(vec_ref, "abre"), _vec(vec_ref, "abim"), sre_ref, sim_ref, T)
    glu = _s5_output(ure_scr, uim_scr, u, vec_ref, wcre_ref, wcim_ref, wglu_ref)
    zs = proj_ref[:, MIX_WIDTH + M_WIDTH + R_WIDTH:2 * MIX_WIDTH]
    mixed_ref[:, M_WIDTH + R_WIDTH:MIX_WIDTH] = (glu * _silu(zs)).astype(BF16)


_PROMPT_STATE_SHAPES = ((M_HEADS, M_HEAD_DIM, M_HEAD_DIM), (M_HEADS, M_HEAD_DIM), (1, LANES),
                        ((CONV_W - 1) * SUBLANES, M_WIDTH), (1, R_WIDTH), ((CONV_W - 1) * SUBLANES, R_WIDTH),
                        (1, S_LANES), (1, S_LANES))


def _prompt_mixer(x, total_rows, l, bsz, seq, vec, w_in_b, mats, prev_states):
    rows = PROMPT_CHUNK
    nchunk = seq // rows
    nstep = bsz * nchunk
    aliases = {}
    alias_args = []
    n_in = 4 + len(mats)
    if prev_states is not None:
        alias_args = list(prev_states)
        aliases = {n_in + k: 1 + k for k in range(len(alias_args))}

    def state_spec(shape):
        nd = len(shape)
        return pl.BlockSpec((None, None) + shape, lambda t: (l, t // nchunk) + (0,) * nd)

    out_shape = ([jax.ShapeDtypeStruct((total_rows, MIX_WIDTH), BF16)]
                 + [jax.ShapeDtypeStruct((DEPTH, bsz) + s, F32) for s in _PROMPT_STATE_SHAPES])
    scratch = ([pltpu.VMEM((rows, w), F32)
                for w in (M_WIDTH, M_WIDTH, M_WIDTH, M_WIDTH, R_WIDTH, R_WIDTH, S_LANES, S_LANES)]
               + [pltpu.VMEM((2, rows, 2 * MIX_WIDTH), F32), pltpu.VMEM((rows, D_MODEL), BF16)])
    return pl.pallas_call(
        functools.partial(_prompt_mixer_kernel, rows=rows, nchunk=nchunk, n_alias=len(alias_args)),
        grid=(nstep,),
        in_specs=([pl.BlockSpec((rows, D_MODEL), lambda t: (jnp.minimum(t + 1, nstep - 1), 0)),
                   pl.BlockSpec((rows, D_MODEL), lambda t: (0, 0), pipeline_mode=pl.Buffered(1)),
                   _layer_spec(vec, l),
                   pl.BlockSpec(w_in_b.shape, lambda t: (0, 0), pipeline_mode=pl.Buffered(1))]
                  + [_layer_spec(w, l) for w in mats] + [_ANY_SPEC] * len(alias_args)),
        out_specs=([pl.BlockSpec((rows, MIX_WIDTH), lambda t: (t, 0))]
                   + [state_spec(s) for s in _PROMPT_STATE_SHAPES]),
        out_shape=out_shape,
        scratch_shapes=scratch,
        input_output_aliases=aliases,
        compiler_params=pltpu.CompilerParams(
            dimension_semantics=("arbitrary",), vmem_limit_bytes=VMEM_LIMIT_BYTES),
        name="prompt_mixer",
    )(x, x, vec, w_in_b, *mats, *alias_args)


def _seg_last(x, groups):
    x3 = x.reshape(groups, SUBLANES, x.shape[-1])
    return jnp.broadcast_to(x3[:, SUBLANES - 1:SUBLANES, :], x3.shape).reshape(x.shape)


def _seg_max(x, groups):
    x3 = x.reshape(groups, SUBLANES, x.shape[-1])
    return jnp.broadcast_to(jnp.max(x3, axis=1, keepdims=True), x3.shape).reshape(x.shape)


def _seg_sum(x, groups):
    x3 = x.reshape(groups, SUBLANES, x.shape[-1])
    return jnp.broadcast_to(jnp.sum(x3, axis=1, keepdims=True), x3.shape).reshape(x.shape)


def _seg_rows(state_ref, lanes=slice(None)):
    x = state_ref[:, :, lanes]
    return jnp.broadcast_to(x, (x.shape[0], SUBLANES, x.shape[2])).reshape(x.shape[0] * SUBLANES, x.shape[2])


def _seg_state(x):
    x3 = x.reshape(x.shape[0] // SUBLANES, SUBLANES, x.shape[-1])
    return x3[:, SUBLANES - 1:SUBLANES, :]


def _with_history(x, buf_ref, new_buf_ref):
    g = x.shape[0] // SUBLANES
    x3 = x.reshape(g, SUBLANES, x.shape[1])
    lead = SAMPLE_LEAD - (CONV_W - 1)
    full = jnp.concatenate([jnp.zeros((g, lead, x.shape[1]), F32), buf_ref[...], x3[:, SAMPLE_LEAD:, :]], axis=1)
    new_buf_ref[...] = full[:, SUBLANES - (CONV_W - 1):, :]
    return full.reshape(x.shape)


def _conv_rolled(xf, w, bias):
    out = w[CONV_W - 1:CONV_W, :] * xf + bias
    for j in range(1, CONV_W):
        out = out + w[CONV_W - 1 - j:CONV_W - j, :] * pltpu.roll(xf, j, 0)
    return out


_SAMPLE_STATE_WIDTHS = (M_WIDTH, LANES, M_WIDTH, R_WIDTH, R_WIDTH, S_LANES, S_LANES)
_SAMPLE_STATE_ROWS = (1, 1, CONV_W - 1, 1, CONV_W - 1, 1, 1)


def _sample_mixer_kernel(*refs, rows, c_seqs, n_alias):
    (proj_ref, c0_ref, n0_ref, m0_ref, mtail0_ref, h0_ref, rtail0_ref, sre0_ref, sim0_ref,
     vec_ref, wqk_ref, wvo_ref, wif_ref, rwa_ref, rwx_ref, wbu_ref, wcre_ref, wcim_ref, wglu_ref) = refs[:19]
    (mixed_ref, c_ref, n_ref, m_ref, mconv_ref, h_ref, rconv_ref, sre_ref, sim_ref,
     q_scr, k_scr, v_scr, o_scr, xc_scr, intert_scr, vwt_scr, qt_scr, ksb_scr, dec_scr, b_scr, li_scr,
     ure_scr, uim_scr) = refs[19 + n_alias:]
    R = rows
    G = R // SUBLANES
    sub = pl.program_id(1)
    k_scale = M_HEAD_DIM ** -0.5
    pos = _sublane_pos((R, 1))
    valid = pos >= SAMPLE_LEAD
    first_token = pos == SAMPLE_LEAD

    @pl.when(sub == 0)
    def _():
        xm = _with_history(proj_ref[:, 0:M_WIDTH], mtail0_ref, mconv_ref)
        xc = _silu(_conv_rolled(xm, _vec(vec_ref, "mcw"), _vec(vec_ref, "mcb")))
        xc_scr[...] = xc
        gates = _mlstm_project(xc, xm, vec_ref, wqk_ref, wvo_ref, wif_ref, q_scr, k_scr, v_scr, o_scr)
        log_i = jnp.where(valid, gates[:, :LANES], -jnp.inf)
        log_f = jnp.where(valid, _log_sigmoid(gates[:, LANES:]), 0.0)
        b_all = _tile_cumsum(log_f)
        b_end = _seg_last(b_all, G)
        m_rows = _seg_rows(m0_ref)
        n_rows = _seg_rows(n0_ref)
        log_src = b_end - b_all + log_i
        m_new = jnp.maximum(b_end + m_rows, _seg_max(log_src, G))
        w_src_all = jnp.exp(log_src - m_new)
        decay_all = jnp.exp(b_end + m_rows - m_new)
        dec_scr[...] = decay_all
        b_scr[...] = b_all
        li_scr[...] = log_i
        m_ref[...] = _seg_state(m_new)
        for h in range(M_HEADS):
            sl = slice(h * M_HEAD_DIM, (h + 1) * M_HEAD_DIM)
            ks = k_scr[:, sl] * k_scale
            w_src = w_src_all[:, h:h + 1]
            n_ref[:, :, sl] = _seg_state(decay_all[:, h:h + 1] * n_rows[:, sl] + _seg_sum(ks * w_src, G))
            vwt_scr[h] = (v_scr[:, sl] * w_src).T
            qt_scr[h] = q_scr[:, sl].T.astype(BF16)
            ksb_scr[h] = ks.astype(BF16)
            intert_scr[h] = jnp.zeros((M_HEAD_DIM, R), F32)

        xr = _with_history(proj_ref[:, M_WIDTH:M_WIDTH + R_WIDTH], rtail0_ref, rconv_ref)
        xcr = _conv_rolled(xr, _vec(vec_ref, "rcw"), _vec(vec_ref, "rcb"))
        a, bb = _rglru_coeffs(xcr, vec_ref, rwa_ref, rwx_ref)
        hs = _tile_scan_real(a, jnp.where(valid, bb, 0.0) + jnp.where(first_token, a * _seg_rows(h0_ref), 0.0))
        h_ref[...] = _seg_state(hs)
        zr = proj_ref[:, MIX_WIDTH + M_WIDTH:MIX_WIDTH + M_WIDTH + R_WIDTH]
        mixed_ref[:, M_WIDTH:M_WIDTH + R_WIDTH] = jnp.where(valid, hs * _silu(zr), 0.0).astype(BF16)

        u = proj_ref[:, M_WIDTH + R_WIDTH:MIX_WIDTH]
        _s5_project(u.astype(BF16), wbu_ref, ure_scr, uim_scr)
        p_re = _vec(vec_ref, "abre")
        p_im = _vec(vec_ref, "abim")
        for blk in range(S_LANES // SCAN_LANE_BLOCK):
            sl = slice(blk * SCAN_LANE_BLOCK, (blk + 1) * SCAN_LANE_BLOCK)
            pr = jnp.broadcast_to(p_re[:, sl], (R, SCAN_LANE_BLOCK))
            pi = jnp.broadcast_to(p_im[:, sl], (R, SCAN_LANE_BLOCK))
            s0r = _seg_rows(sre0_ref, sl)
            s0i = _seg_rows(sim0_ref, sl)
            sr, si = _tile_scan_cplx(
                jnp.where(valid, ure_scr[:, sl], 0.0) + jnp.where(first_token, pr * s0r - pi * s0i, 0.0),
                jnp.where(valid, uim_scr[:, sl], 0.0) + jnp.where(first_token, pr * s0i + pi * s0r, 0.0), pr, pi)
            ure_scr[:, sl] = sr
            uim_scr[:, sl] = si
            sre_ref[:, :, sl] = _seg_state(sr)
            sim_ref[:, :, sl] = _seg_state(si)
        glu = _s5_output(ure_scr, uim_scr, u, vec_ref, wcre_ref, wcim_ref, wglu_ref)
        zs = proj_ref[:, MIX_WIDTH + M_WIDTH + R_WIDTH:2 * MIX_WIDTH]
        mixed_ref[:, M_WIDTH + R_WIDTH:MIX_WIDTH] = jnp.where(valid, glu * _silu(zs), 0.0).astype(BF16)

    lane_seq = lax.broadcasted_iota(jnp.int32, (M_HEAD_DIM, R), 1) >> SUBLANE_SHIFT
    seq0 = sub * c_seqs
    for h in range(M_HEADS):
        c_old = c0_ref[:, h].reshape(c_seqs * M_HEAD_DIM, M_HEAD_DIM)
        readout = jnp.dot(c_old.astype(BF16), qt_scr[h], preferred_element_type=F32)
        acc = intert_scr[h]
        vwt = vwt_scr[h]
        lhs = []
        for s in range(c_seqs):
            own = lane_seq == seq0 + s
            acc = jnp.where(own, readout[s * M_HEAD_DIM:(s + 1) * M_HEAD_DIM], acc)
            lhs.append(jnp.where(own, vwt, 0.0))
        intert_scr[h] = acc
        upd = jnp.dot(jnp.concatenate(lhs, axis=0).astype(BF16), ksb_scr[h], preferred_element_type=F32)
        for s in range(c_seqs):
            r0 = pl.multiple_of((seq0 + s) * SUBLANES, SUBLANES)
            c_ref[s, h] = (dec_scr[pl.ds(r0, 1), h:h + 1] * c0_ref[s, h]
                           + upd[s * M_HEAD_DIM:(s + 1) * M_HEAD_DIM])

    @pl.when(sub == pl.num_programs(1) - 1)
    def _():
        _sample_heads(proj_ref, m0_ref, n0_ref, vec_ref, mixed_ref,
                      q_scr, k_scr, v_scr, o_scr, xc_scr, intert_scr, b_scr, li_scr, valid, R)


def _sample_heads(proj_ref, m0_ref, n0_ref, vec_ref, mixed_ref,
                  q_scr, k_scr, v_scr, o_scr, xc_scr, intert_scr, b_scr, li_scr, valid, R):
    rowi = lax.broadcasted_iota(jnp.int32, (R, R), 0)
    coli = lax.broadcasted_iota(jnp.int32, (R, R), 1)
    same_causal = jnp.logical_and(coli <= rowi, (coli >> SUBLANE_SHIFT) == (rowi >> SUBLANE_SHIFT))
    k_scale = M_HEAD_DIM ** -0.5
    mnw = _vec(vec_ref, "mnw")
    mskip = _vec(vec_ref, "mskip")
    n_rows = _seg_rows(n0_ref)
    b_all = b_scr[...]
    log_i = li_scr[...]
    log_inter = b_all + _seg_rows(m0_ref)
    m_t_all = jnp.maximum(log_inter, b_all + _tile_cummax(log_i - b_all))
    w_inter_all = jnp.exp(log_inter - m_t_all)
    floor_all = jnp.exp(-m_t_all)
    b_t = b_all.T
    li_t = log_i.T

    for h in range(M_HEADS):
        sl = slice(h * M_HEAD_DIM, (h + 1) * M_HEAD_DIM)
        q = q_scr[:, sl]
        ks = k_scr[:, sl] * k_scale
        v = v_scr[:, sl]
        q_b = q.astype(BF16)
        ks_b = ks.astype(BF16)
        w_inter = w_inter_all[:, h:h + 1]

        log_d = jnp.where(same_causal, b_all[:, h:h + 1] - b_t[h:h + 1, :] + li_t[h:h + 1, :], -jnp.inf)
        w_intra = jnp.exp(log_d - m_t_all[:, h:h + 1])
        s = lax.dot_general(q_b, ks_b, _NT, preferred_element_type=F32) * w_intra
        num = (jnp.dot(s.astype(BF16), v.astype(BF16), preferred_element_type=F32)
               + w_inter * intert_scr[h].T)
        den = (jnp.sum(s, axis=1, keepdims=True)
               + w_inter * jnp.sum(q * n_rows[:, sl], axis=1, keepdims=True))
        hh = num / jnp.maximum(jnp.abs(den), floor_all[:, h:h + 1])

        z = proj_ref[:, MIX_WIDTH + h * M_HEAD_DIM:MIX_WIDTH + (h + 1) * M_HEAD_DIM]
        out = _head_output(hh, o_scr[:, sl], xc_scr[:, sl], z, mnw[:, sl], mskip[:, sl])
        mixed_ref[:, sl] = jnp.where(valid, out, 0.0).astype(BF16)


def _sample_mixer(proj, l, row0, mixed_prev, c_all, row_state, vec, mats, prev_out):
    R = SAMPLE_ROWS
    nrows = proj.shape[0]
    nsub = (R // SUBLANES) // SAMPLE_C_SEQS
    blk0 = row0 // R
    c_spec = pl.BlockSpec((None, SAMPLE_C_SEQS, M_HEADS, M_HEAD_DIM, M_HEAD_DIM),
                          lambda i, j: (l, i * nsub + j, 0, 0, 0))

    def state_spec(width, rows_per_seq):
        return pl.BlockSpec((None, R // SUBLANES, rows_per_seq, width), lambda i, j: (l, i, 0, 0))

    def state_shape(width, rows_per_seq):
        return (DEPTH, nrows // SUBLANES, rows_per_seq, width)

    state_specs = [state_spec(w, r) for w, r in zip(_SAMPLE_STATE_WIDTHS, _SAMPLE_STATE_ROWS)]
    alias_args = [mixed_prev] + (list(prev_out) if prev_out is not None else [])
    first_alias = 2 + len(row_state) + 1 + len(mats)
    aliases = {first_alias + k: k for k in range(len(alias_args))}
    out_shape = ([jax.ShapeDtypeStruct(mixed_prev.shape, BF16), jax.ShapeDtypeStruct(c_all.shape, F32)]
                 + [jax.ShapeDtypeStruct(state_shape(w, r), F32)
                    for w, r in zip(_SAMPLE_STATE_WIDTHS, _SAMPLE_STATE_ROWS)])
    scratch = ([pltpu.VMEM((R, M_WIDTH), F32)] * 5
               + [pltpu.VMEM((M_HEADS, M_HEAD_DIM, R), F32), pltpu.VMEM((M_HEADS, M_HEAD_DIM, R), F32),
                  pltpu.VMEM((M_HEADS, M_HEAD_DIM, R), BF16), pltpu.VMEM((M_HEADS, R, M_HEAD_DIM), BF16)]
               + [pltpu.VMEM((R, LANES), F32)] * 3
               + [pltpu.VMEM((R, S_LANES), F32)] * 2)
    return pl.pallas_call(
        functools.partial(_sample_mixer_kernel, rows=R, c_seqs=SAMPLE_C_SEQS, n_alias=len(alias_args)),
        grid=(nrows // R, nsub),
        in_specs=([pl.BlockSpec((R, 2 * MIX_WIDTH), lambda i, j: (i, 0)), c_spec] + state_specs
                  + [_layer_spec(vec, l)] + [_layer_spec(w, l) for w in mats] + [_ANY_SPEC] * len(alias_args)),
        out_specs=[pl.BlockSpec((R, MIX_WIDTH), lambda i, j: (blk0 + i, 0)), c_spec] + state_specs,
        out_shape=out_shape,
        scratch_shapes=scratch,
        input_output_aliases=aliases,
        compiler_params=pltpu.CompilerParams(
            dimension_semantics=("parallel", "arbitrary"), vmem_limit_bytes=VMEM_LIMIT_BYTES),
        name="sample_mixer",
    )(proj, c_all, *row_state, vec, *mats, *alias_args)


def _prepare_weights(p, s5):
    abre, abim, wbu, wcre, wcim = s5

    def split_gates(g):
        pad = [(0, 0)] * (g.ndim - 1) + [(0, LANES - M_HEADS)]
        return jnp.concatenate([jnp.pad(g[..., :M_HEADS], pad), jnp.pad(g[..., M_HEADS:], pad)], axis=-1)

    table = {"mcw": p["m_conv_w"], "rcw": p["r_conv_w"], "mcb": p["m_conv_b"], "mnw": p["m_norm_w"],
             "mskip": p["m_skip"], "rcb": p["r_conv_b"], "rba": p["r_ba"], "rbx": p["r_bx"], "rlam": p["r_lam"],
             "sd": p["s_d"], "bglu": p["s_b_glu"], "bif": split_gates(p["m_b_if"]), "abre": abre, "abim": abim,
             "norm_w": p["norm_w"]}

    def group_rows(group):
        parts = [table[name].reshape(DEPTH, n, w) for name, n, w in group]
        fill = _VEC_WIDTH - sum(w for _, _, w in group)
        if fill:
            parts.append(jnp.zeros((DEPTH, group[0][1], fill), F32))
        return parts[0] if len(parts) == 1 else jnp.concatenate(parts, axis=-1)

    vec = jnp.concatenate([group_rows(g) for g in _VEC_GROUPS]
                          + [jnp.zeros((DEPTH, _VEC_ROWS - _VEC_USED_ROWS, _VEC_WIDTH), F32)], axis=1)

    def block_diag_halves(w):
        nb = R_BLOCKS // 2
        w5 = w.reshape(DEPTH, 2, nb, w.shape[-2], w.shape[-1])
        eye = jnp.eye(nb, dtype=F32)[:, None, :, None]
        return (w5[:, :, :, :, None, :] * eye).reshape(
            DEPTH, 2, nb * w.shape[-2], nb * w.shape[-1]).astype(BF16)

    mats = {
        "wqk": jnp.concatenate([p["m_wq"], p["m_wk"]], axis=-1).astype(BF16),
        "wvo": jnp.concatenate([p["m_wv"], p["m_wo"]], axis=-1).astype(BF16),
        "wif": split_gates(p["m_w_if"]).astype(BF16),
        "rwa": block_diag_halves(p["r_wa"]), "rwx": block_diag_halves(p["r_wx"]),
        "wbu": wbu, "wcre": wcre, "wcim": wcim,
        "wglu": p["s_w_glu"].astype(BF16),
    }
    return vec, [mats[k] for k in _MATRIX_KEYS]


def _sample_rows_state(n, m, mconv, h, rconv, sre, sim):
    bsz = n.shape[1]

    def per_seq(x):
        return x.reshape(DEPTH, bsz, 1, x.shape[-1])

    m_pad = jnp.pad(m, ((0, 0), (0, 0), (0, LANES - M_HEADS)))
    return (per_seq(n.reshape(DEPTH, bsz, M_WIDTH)), per_seq(m_pad), mconv, per_seq(h), rconv,
            per_seq(sre.reshape(DEPTH, bsz, S_LANES)), per_seq(sim.reshape(DEPTH, bsz, S_LANES)))


def _sample_state_from_rows(n, m, mconv, h, rconv, sre, sim):
    bsz = n.shape[1]
    return (n.reshape(DEPTH, bsz, M_HEADS, M_HEAD_DIM), m.reshape(DEPTH, bsz, LANES)[..., :M_HEADS],
            mconv, h.reshape(DEPTH, bsz, R_WIDTH), rconv,
            sre.reshape(DEPTH, bsz, S_GROUPS, S_STATE), sim.reshape(DEPTH, bsz, S_GROUPS, S_STATE))


def _prompt_state_from_kernel(c, n, m, mtail, h, rtail, sre, sim):
    bsz = c.shape[1]
    return (c, n, m[:, :, 0, :M_HEADS], mtail[:, :, SUBLANES - 1::SUBLANES], h.reshape(DEPTH, bsz, R_WIDTH),
            rtail[:, :, SUBLANES - 1::SUBLANES],
            sre.reshape(DEPTH, bsz, S_GROUPS, S_STATE), sim.reshape(DEPTH, bsz, S_GROUPS, S_STATE))


def _interleave_chunks(x, inverse=False):
    bsz, seq, d = x.shape
    sub_len = PROMPT_CHUNK // SUBLANES
    inner = (sub_len, SUBLANES) if inverse else (SUBLANES, sub_len)
    return x.reshape(bsz, seq // PROMPT_CHUNK, *inner, d).transpose(0, 1, 3, 2, 4).reshape(bsz, seq, d)


def kernel(x_prompt, x_sample, state_mlstm_C, state_mlstm_n, state_mlstm_m, state_mlstm_conv, state_rglru_h, state_rglru_conv, state_s5_re, state_s5_im, norm_w, w_in, w_out, m_conv_w, m_conv_b, m_wq, m_wk, m_wv, m_wo, m_w_if, m_b_if, m_norm_w, m_skip, r_conv_w, r_conv_b, r_wa, r_ba, r_wx, r_bx, r_lam, s_lam_re, s_lam_im, s_b_re, s_b_im, s_c_re, s_c_im, s_d, s_log_step, s_w_glu, s_b_glu, final_norm_w):
    p = dict(norm_w=norm_w, w_in=w_in, w_out=w_out, m_conv_w=m_conv_w, m_conv_b=m_conv_b, m_wq=m_wq, m_wk=m_wk,
             m_wv=m_wv, m_wo=m_wo, m_w_if=m_w_if, m_b_if=m_b_if, m_norm_w=m_norm_w, m_skip=m_skip,
             r_conv_w=r_conv_w, r_conv_b=r_conv_b, r_wa=r_wa, r_ba=r_ba, r_wx=r_wx, r_bx=r_bx, r_lam=r_lam,
             s_d=s_d, s_w_glu=s_w_glu, s_b_glu=s_b_glu)
    s5 = _s5_prep(s_lam_re, s_lam_im, s_log_step, s_b_re, s_b_im, s_c_re, s_c_im)
    vec, mats = _prepare_weights(p, s5)
    final_w = final_norm_w.reshape(1, D_MODEL)

    bsz, seq, _ = x_prompt.shape
    dec_batch, dec_seq, _ = x_sample.shape
    n_prompt = bsz * seq
    n_sample = dec_batch * SUBLANES
    stream = [_interleave_chunks(x_prompt).reshape(n_prompt, D_MODEL),
              jnp.pad(x_sample, ((0, 0), (SAMPLE_LEAD, 0), (0, 0))).reshape(n_sample, D_MODEL)]
    rows_state = _sample_rows_state(state_mlstm_n, state_mlstm_m, state_mlstm_conv, state_rglru_h,
                                    state_rglru_conv, state_s5_re, state_s5_im)
    pr_states = None
    sa_states = None
    for l in range(DEPTH):
        last = l == DEPTH - 1
        proj_s, w_in_b = _inproj(stream[-1], stream[-1].shape[0] - n_sample, n_sample, vec, w_in, l)
        mixed, *pr_states = _prompt_mixer(stream[0], n_prompt + n_sample, l, bsz, seq, vec, w_in_b, mats, pr_states)
        mixed, *sa_states = _sample_mixer(proj_s, l, n_prompt, mixed, state_mlstm_C, rows_state, vec, mats, sa_states)
        stream = _outproj(mixed, stream, w_out, l, final_w, last, split_rows=n_prompt if last else None)
    y_prompt = _interleave_chunks(stream[0].reshape(bsz, seq, D_MODEL), inverse=True)
    y_sample = stream[1]
    return (y_prompt, y_sample, *_prompt_state_from_kernel(*pr_states),
            sa_states[0], *_sample_state_from_rows(*sa_states[1:]))
```
